```python
import jax, jax.numpy as jnp
from jax import lax
import numpy as np

D_MODEL = 1024
BATCH = 8
SEQ = 8192
DEPTH = 1

PLE_DIM = 256
EPS = 1e-6

A_GROUPS = 8
A_WIDTH = 512
A_GROUP_DIM = A_WIDTH // A_GROUPS
A_CHUNK = 128

B_HEADS = 4
B_KEY = D_MODEL // 2
B_VAL = D_MODEL
B_HK = B_KEY // B_HEADS
B_HV = B_VAL // B_HEADS
B_GATE_RANK = 16
B_GATE_NORM = 16.0
B_CHUNK = 64

D_FF = 2816
CONV_W = 3

IN_SPLITS = (A_WIDTH, A_WIDTH, B_KEY, B_KEY, B_VAL, B_VAL, B_GATE_RANK, D_MODEL, D_MODEL)
IN_COLS = A_WIDTH * 2 + B_KEY * 2 + B_VAL * 2 + B_GATE_RANK + D_MODEL * 2

kernel_name = "hybrid_gmlp_gla_convffn_block"


def rms_norm(x, w):
    x32 = x.astype(jnp.float32)
    y = x32 * lax.rsqrt(jnp.mean(x32 * x32, axis=-1, keepdims=True) + EPS)
    return (y * w.astype(jnp.float32)).astype(x.dtype)


def layer_norm(x, g, b):
    x32 = x.astype(jnp.float32)
    mu = jnp.mean(x32, axis=-1, keepdims=True)
    var = jnp.mean(jnp.square(x32 - mu), axis=-1, keepdims=True)
    y = (x32 - mu) * lax.rsqrt(var + EPS) * g.astype(jnp.float32) + b.astype(jnp.float32)
    return y.astype(x.dtype)


def split_cols(z):
    out, off = [], 0
    for width in IN_SPLITS:
        out.append(z[..., off:off + width])
        off += width
    return out


def spatial_gating_unit(u, v, ln_g, ln_b, w_s, b_s):
    bsz, seq, _ = v.shape
    n_chunks = seq // A_CHUNK
    v = layer_norm(v, ln_g, ln_b).reshape(bsz, n_chunks, A_CHUNK, A_GROUPS, A_GROUP_DIM)
    causal = jnp.tril(jnp.ones((A_CHUNK, A_CHUNK), dtype=bool))
    w = jnp.where(causal, w_s, 0.0).astype(v.dtype)
    s = jnp.einsum('gts,bcsgd->bctgd', w, v) + b_s.T[None, None, :, :, None].astype(v.dtype)
    return u * s.reshape(bsz, seq, A_WIDTH)


def gla_chunked(q, k, v, log_a):
    bsz, nh, seq, dk = q.shape
    dv = v.shape[-1]
    nc = seq // B_CHUNK
    r = lambda t: t.reshape(bsz, nh, nc, B_CHUNK, t.shape[-1])
    q, k, v, log_a = r(q), r(k), r(v), r(log_a)
    b = jnp.cumsum(log_a, axis=3)
    b_last = b[:, :, :, -1:, :]
    q_dec = q * jnp.exp(b)
    k_inv = k * jnp.exp(-b)
    k_tail = k * jnp.exp(b_last - b)
    causal = jnp.tril(jnp.ones((B_CHUNK, B_CHUNK), dtype=bool))
    scores = jnp.where(causal, jnp.einsum('bhnid,bhnjd->bhnij', q_dec, k_inv), 0.0)
    o_intra = jnp.einsum('bhnij,bhnjv->bhniv', scores, v)
    kv = jnp.einsum('bhnjd,bhnjv->bhndv', k_tail, v)
    decay = jnp.exp(b_last[:, :, :, 0, :])

    def step(state, inp):
        q_c, kv_c, dec_c = inp
        o = jnp.einsum('bhid,bhdv->bhiv', q_c, state)
        state = state * dec_c[..., None] + kv_c
        return state, o

    init = jnp.zeros((bsz, nh, dk, dv), q.dtype)
    xs = (jnp.moveaxis(q_dec, 2, 0), jnp.moveaxis(kv, 2, 0), jnp.moveaxis(decay, 2, 0))
    _, o_inter = lax.scan(step, init, xs)
    o = o_intra + jnp.moveaxis(o_inter, 0, 2)
    return o.reshape(bsz, nh, seq, dv)


def gla_branch(q, k, v, g_out, g_lr, w_gk, b_gk, onorm_w):
    bsz, seq, _ = q.shape
    heads = lambda t, d: t.reshape(bsz, seq, B_HEADS, d).transpose(0, 2, 1, 3).astype(jnp.float32)
    log_a = jax.nn.log_sigmoid((g_lr @ w_gk + b_gk).astype(jnp.float32)) / B_GATE_NORM
    o = gla_chunked(heads(q, B_HK) * (B_HK ** -0.5), heads(k, B_HK), heads(v, B_HV), heads(log_a, B_HK))
    o = rms_norm(o.transpose(0, 2, 1, 3), onorm_w)
    return o.reshape(bsz, seq, B_VAL).astype(q.dtype) * jax.nn.silu(g_out)


def causal_depthwise_conv(x, w, b):
    seq = x.shape[1]
    xp = jnp.pad(x, ((0, 0), (CONV_W - 1, 0), (0, 0)))
    y = b
    for j in range(CONV_W):
        y = y + xp[:, j:j + seq, :] * w[j]
    return y


def _fwd_setup_inputs(seed: int = 0) -> dict:
    key = jax.random.key(seed)
    ks = iter(jax.random.split(key, 32))
    f32 = jnp.float32
    nrm = lambda shape, scale: jax.random.normal(next(ks), shape, f32) * scale
    gain = lambda n: 1.0 + nrm((DEPTH, n), 0.05)
    return {
        "x": nrm((BATCH, SEQ, D_MODEL), 1.0),
        "p": nrm((DEPTH, BATCH, SEQ, PLE_DIM), 1.0),
        "pre_mix_norm": gain(D_MODEL),
        "w_in": nrm((DEPTH, D_MODEL, IN_COLS), D_MODEL ** -0.5),
        "a_ln_g": gain(A_WIDTH),
        "a_ln_b": nrm((DEPTH, A_WIDTH), 0.02),
        "a_spatial_w": nrm((DEPTH, A_GROUPS, A_CHUNK, A_CHUNK), A_CHUNK ** -0.5),
        "a_spatial_b": 1.0 + nrm((DEPTH, A_GROUPS, A_CHUNK), 0.02),
        "a_out": nrm((DEPTH, A_WIDTH, D_MODEL), A_WIDTH ** -0.5),
        "b_gk": nrm((DEPTH, B_GATE_RANK, B_KEY), B_GATE_RANK ** -0.5),
        "b_gk_bias": nrm((DEPTH, B_KEY), 0.1),
        "b_out_norm": gain(B_HV),
        "b_out": nrm((DEPTH, B_VAL, D_MODEL), B_VAL ** -0.5),
        "w_mix_out": nrm((DEPTH, D_MODEL, D_MODEL), D_MODEL ** -0.5),
        "post_mix_norm": gain(D_MODEL),
        "pre_ffn_norm": gain(D_MODEL),
        "w_up": nrm((DEPTH, D_MODEL, 2 * D_FF), D_MODEL ** -0.5),
        "conv_w": nrm((DEPTH, CONV_W, 2 * D_FF), CONV_W ** -0.5),
        "conv_b": nrm((DEPTH, 2 * D_FF), 0.02),
        "w_down": nrm((DEPTH, D_FF, D_MODEL), D_FF ** -0.5),
        "post_ffn_norm": gain(D_MODEL),
        "w_ple": nrm((DEPTH, PLE_DIM, D_MODEL), PLE_DIM ** -0.5),
        "w_ple_gate": nrm((DEPTH, D_MODEL, D_MODEL), D_MODEL ** -0.5),
        "post_ple_norm": gain(D_MODEL),
    }


def _fwd_reference(x, p, pre_mix_norm, w_in, a_ln_g, a_ln_b, a_spatial_w, a_spatial_b, a_out,
              b_gk, b_gk_bias, b_out_norm, b_out, w_mix_out, post_mix_norm, pre_ffn_norm,
              w_up, conv_w, conv_b, w_down, post_ffn_norm, w_ple, w_ple_gate, post_ple_norm):
    h = x
    for i in range(DEPTH):
        a = rms_norm(h, pre_mix_norm[i])
        u_a, v_a, q_b, k_b, v_b, og_b, lr_b, gate_a, gate_b = split_cols(a @ w_in[i])
        y_a = spatial_gating_unit(jax.nn.gelu(u_a), jax.nn.gelu(v_a), a_ln_g[i], a_ln_b[i],
                                  a_spatial_w[i], a_spatial_b[i]) @ a_out[i]
        y_b = gla_branch(q_b, k_b, v_b, og_b, lr_b, b_gk[i], b_gk_bias[i], b_out_norm[i]) @ b_out[i]
        mixed = (jax.nn.sigmoid(gate_a) * y_a + jax.nn.sigmoid(gate_b) * y_b) @ w_mix_out[i]
        h = h + rms_norm(mixed, post_mix_norm[i])
        c = rms_norm(h, pre_ffn_norm[i])
        up = causal_depthwise_conv(c @ w_up[i], conv_w[i], conv_b[i])
        g_ff, v_ff = up[..., :D_FF], up[..., D_FF:]
        f = (jax.nn.gelu(g_ff, approximate=True) * v_ff) @ w_down[i]
        h = h + rms_norm(f, post_ffn_norm[i])
        ple = jax.nn.sigmoid(h @ w_ple_gate[i]) * (p[i] @ w_ple[i])
        h = h + rms_norm(ple, post_ple_norm[i])
    return h


import jax as _jax
import jax.numpy as _jnp

TWIN_FORMAT = 'train_step'
FWD_PARAMS = ['x', 'p', 'pre_mix_norm', 'w_in', 'a_ln_g', 'a_ln_b', 'a_spatial_w', 'a_spatial_b', 'a_out', 'b_gk', 'b_gk_bias', 'b_out_norm', 'b_out', 'w_mix_out', 'post_mix_norm', 'pre_ffn_norm', 'w_up', 'conv_w', 'conv_b', 'w_down', 'post_ffn_norm', 'w_ple', 'w_ple_gate', 'post_ple_norm']
TWIN_WEIGHTS = ['pre_mix_norm', 'w_in', 'a_ln_g', 'a_ln_b', 'a_spatial_w', 'a_spatial_b', 'a_out', 'b_gk', 'b_gk_bias', 'b_out_norm', 'b_out', 'w_mix_out', 'post_mix_norm', 'pre_ffn_norm', 'w_up', 'conv_w', 'conv_b', 'w_down', 'post_ffn_norm', 'w_ple', 'w_ple_gate', 'post_ple_norm']
TWIN_DIFF_INPUT = 'x'
TWIN_INPUTS = ['x', 'p', 'pre_mix_norm', 'w_in', 'a_ln_g', 'a_ln_b', 'a_spatial_w', 'a_spatial_b', 'a_out', 'b_gk', 'b_gk_bias', 'b_out_norm', 'b_out', 'w_mix_out', 'post_mix_norm', 'pre_ffn_norm', 'w_up', 'conv_w', 'conv_b', 'w_down', 'post_ffn_norm', 'w_ple', 'w_ple_gate', 'post_ple_norm', 'loss_target', 'm_pre_mix_norm', 'm_w_in', 'm_a_ln_g', 'm_a_ln_b', 'm_a_spatial_w', 'm_a_spatial_b', 'm_a_out', 'm_b_gk', 'm_b_gk_bias', 'm_b_out_norm', 'm_b_out', 'm_w_mix_out', 'm_post_mix_norm', 'm_pre_ffn_norm', 'm_w_up', 'm_conv_w', 'm_conv_b', 'm_w_down', 'm_post_ffn_norm', 'm_w_ple', 'm_w_ple_gate', 'm_post_ple_norm', 'v_pre_mix_norm', 'v_w_in', 'v_a_ln_g', 'v_a_ln_b', 'v_a_spatial_w', 'v_a_spatial_b', 'v_a_out', 'v_b_gk', 'v_b_gk_bias', 'v_b_out_norm', 'v_b_out', 'v_w_mix_out', 'v_post_mix_norm', 'v_pre_ffn_norm', 'v_w_up', 'v_conv_w', 'v_conv_b', 'v_w_down', 'v_post_ffn_norm', 'v_w_ple', 'v_w_ple_gate', 'v_post_ple_norm']
TWIN_OUTPUTS = ['loss', 'grad_x', 'grad_pre_mix_norm', 'grad_w_in', 'grad_a_ln_g', 'grad_a_ln_b', 'grad_a_spatial_w', 'grad_a_spatial_b', 'grad_a_out', 'grad_b_gk', 'grad_b_gk_bias', 'grad_b_out_norm', 'grad_b_out', 'grad_w_mix_out', 'grad_post_mix_norm', 'grad_pre_ffn_norm', 'grad_w_up', 'grad_conv_w', 'grad_conv_b', 'grad_w_down', 'grad_post_ffn_norm', 'grad_w_ple', 'grad_w_ple_gate', 'grad_post_ple_norm', 'delta_pre_mix_norm', 'delta_w_in', 'delta_a_ln_g', 'delta_a_ln_b', 'delta_a_spatial_w', 'delta_a_spatial_b', 'delta_a_out', 'delta_b_gk', 'delta_b_gk_bias', 'delta_b_out_norm', 'delta_b_out', 'delta_w_mix_out', 'delta_post_mix_norm', 'delta_pre_ffn_norm', 'delta_w_up', 'delta_conv_w', 'delta_conv_b', 'delta_w_down', 'delta_post_ffn_norm', 'delta_w_ple', 'delta_w_ple_gate', 'delta_post_ple_norm', 'new_m_pre_mix_norm', 'new_m_w_in', 'new_m_a_ln_g', 'new_m_a_ln_b', 'new_m_a_spatial_w', 'new_m_a_spatial_b', 'new_m_a_out', 'new_m_b_gk', 'new_m_b_gk_bias', 'new_m_b_out_norm', 'new_m_b_out', 'new_m_w_mix_out', 'new_m_post_mix_norm', 'new_m_pre_ffn_norm', 'new_m_w_up', 'new_m_conv_w', 'new_m_conv_b', 'new_m_w_down', 'new_m_post_ffn_norm', 'new_m_w_ple', 'new_m_w_ple_gate', 'new_m_post_ple_norm', 'new_v_pre_mix_norm', 'new_v_w_in', 'new_v_a_ln_g', 'new_v_a_ln_b', 'new_v_a_spatial_w', 'new_v_a_spatial_b', 'new_v_a_out', 'new_v_b_gk', 'new_v_b_gk_bias', 'new_v_b_out_norm', 'new_v_b_out', 'new_v_w_mix_out', 'new_v_post_mix_norm', 'new_v_pre_ffn_norm', 'new_v_w_up', 'new_v_conv_w', 'new_v_conv_b', 'new_v_w_down', 'new_v_post_ffn_norm', 'new_v_w_ple', 'new_v_w_ple_gate', 'new_v_post_ple_norm']
TWIN_LEAF_KINDS = {'loss': 'loss', 'grad_x': 'grad_x', 'grad_pre_mix_norm': 'grad_w', 'grad_w_in': 'grad_w', 'grad_a_ln_g': 'grad_w', 'grad_a_ln_b': 'grad_w', 'grad_a_spatial_w': 'grad_w', 'grad_a_spatial_b': 'grad_w', 'grad_a_out': 'grad_w', 'grad_b_gk': 'grad_w', 'grad_b_gk_bias': 'grad_w', 'grad_b_out_norm': 'grad_w', 'grad_b_out': 'grad_w', 'grad_w_mix_out': 'grad_w', 'grad_post_mix_norm': 'grad_w', 'grad_pre_ffn_norm': 'grad_w', 'grad_w_up': 'grad_w', 'grad_conv_w': 'grad_w', 'grad_conv_b': 'grad_w', 'grad_w_down': 'grad_w', 'grad_post_ffn_norm': 'grad_w', 'grad_w_ple': 'grad_w', 'grad_w_ple_gate': 'grad_w', 'grad_post_ple_norm': 'grad_w', 'delta_pre_mix_norm': 'delta_w', 'delta_w_in': 'delta_w', 'delta_a_ln_g': 'delta_w', 'delta_a_ln_b': 'delta_w', 'delta_a_spatial_w': 'delta_w', 'delta_a_spatial_b': 'delta_w', 'delta_a_out': 'delta_w', 'delta_b_gk': 'delta_w', 'delta_b_gk_bias': 'delta_w', 'delta_b_out_norm': 'delta_w', 'delta_b_out': 'delta_w', 'delta_w_mix_out': 'delta_w', 'delta_post_mix_norm': 'delta_w', 'delta_pre_ffn_norm': 'delta_w', 'delta_w_up': 'delta_w', 'delta_conv_w': 'delta_w', 'delta_conv_b': 'delta_w', 'delta_w_down': 'delta_w', 'delta_post_ffn_norm': 'delta_w', 'delta_w_ple': 'delta_w', 'delta_w_ple_gate': 'delta_w', 'delta_post_ple_norm': 'delta_w', 'new_m_pre_mix_norm': 'new_m', 'new_m_w_in': 'new_m', 'new_m_a_ln_g': 'new_m', 'new_m_a_ln_b': 'new_m', 'new_m_a_spatial_w': 'new_m', 'new_m_a_spatial_b': 'new_m', 'new_m_a_out': 'new_m', 'new_m_b_gk': 'new_m', 'new_m_b_gk_bias': 'new_m', 'new_m_b_out_norm': 'new_m', 'new_m_b_out': 'new_m', 'new_m_w_mix_out': 'new_m', 'new_m_post_mix_norm': 'new_m', 'new_m_pre_ffn_norm': 'new_m', 'new_m_w_up': 'new_m', 'new_m_conv_w': 'new_m', 'new_m_conv_b': 'new_m', 'new_m_w_down': 'new_m', 'new_m_post_ffn_norm': 'new_m', 'new_m_w_ple': 'new_m', 'new_m_w_ple_gate': 'new_m', 'new_m_post_ple_norm': 'new_m', 'new_v_pre_mix_norm': 'new_v', 'new_v_w_in': 'new_v', 'new_v_a_ln_g': 'new_v', 'new_v_a_ln_b': 'new_v', 'new_v_a_spatial_w': 'new_v', 'new_v_a_spatial_b': 'new_v', 'new_v_a_out': 'new_v', 'new_v_b_gk': 'new_v', 'new_v_b_gk_bias': 'new_v', 'new_v_b_out_norm': 'new_v', 'new_v_b_out': 'new_v', 'new_v_w_mix_out': 'new_v', 'new_v_post_mix_norm': 'new_v', 'new_v_pre_ffn_norm': 'new_v', 'new_v_w_up': 'new_v', 'new_v_conv_w': 'new_v', 'new_v_conv_b': 'new_v', 'new_v_w_down': 'new_v', 'new_v_post_ffn_norm': 'new_v', 'new_v_w_ple': 'new_v', 'new_v_w_ple_gate': 'new_v', 'new_v_post_ple_norm': 'new_v'}


def _forward(args):
    return _fwd_reference(*[args[k] for k in FWD_PARAMS])


def _output_shape():
    def fwd():
        inp = _fwd_setup_inputs(0)
        return _fwd_reference(*[inp[k] for k in FWD_PARAMS])
    out = _jax.eval_shape(fwd)
    return out.shape, out.dtype

N_MICROBATCH = 1
ADAM_LR = 0.001
ADAM_B1 = 0.9
ADAM_B2 = 0.999
ADAM_EPS = 1e-08
ADAM_WD = 0.01
ADAM_STEP = 10
PER_EXAMPLE_BATCH_AXIS = {'x': 0, 'p': 1, 'loss_target': 0}
SHARED_INPUTS = []
_WEIGHT_DTYPES = {'pre_mix_norm': _jnp.float32, 'w_in': _jnp.float32, 'a_ln_g': _jnp.float32, 'a_ln_b': _jnp.float32, 'a_spatial_w': _jnp.float32, 'a_spatial_b': _jnp.float32, 'a_out': _jnp.float32, 'b_gk': _jnp.float32, 'b_gk_bias': _jnp.float32, 'b_out_norm': _jnp.float32, 'b_out': _jnp.float32, 'w_mix_out': _jnp.float32, 'post_mix_norm': _jnp.float32, 'pre_ffn_norm': _jnp.float32, 'w_up': _jnp.float32, 'conv_w': _jnp.float32, 'conv_b': _jnp.float32, 'w_down': _jnp.float32, 'post_ffn_norm': _jnp.float32, 'w_ple': _jnp.float32, 'w_ple_gate': _jnp.float32, 'post_ple_norm': _jnp.float32}
MOMENT_SCALE = {'pre_mix_norm': 1.358926e+00, 'w_in': 5.457659e-01, 'a_ln_g': 5.028225e-01, 'a_ln_b': 4.989615e-01, 'a_spatial_w': 3.411777e-01, 'a_spatial_b': 4.785417e-01, 'a_out': 1.755571e+00, 'b_gk': 7.479783e-02, 'b_gk_bias': 3.370064e-01, 'b_out_norm': 1.061317e+00, 'b_out': 4.695472e-01, 'w_mix_out': 1.806658e+00, 'post_mix_norm': 6.401423e+01, 'pre_ffn_norm': 1.019720e+00, 'w_up': 5.591746e-01, 'conv_w': 6.984440e-01, 'conv_b': 3.218127e+00, 'w_down': 1.221148e+00, 'post_ffn_norm': 6.436153e+01, 'w_ple': 5.430929e-01, 'w_ple_gate': 3.491188e-01, 'post_ple_norm': 6.483226e+01}


def _to_microbatches(a, axis):
    t = _jnp.moveaxis(a, axis, 0)
    t = t.reshape((N_MICROBATCH, t.shape[0] // N_MICROBATCH) + t.shape[1:])
    return _jnp.moveaxis(t, 1, axis + 1)


def setup_inputs(seed: int = 0) -> dict:
    inp = _fwd_setup_inputs(seed)
    key = _jax.random.fold_in(_jax.random.key(seed), 7919)
    shape, _ = _output_shape()
    out = dict(inp)
    out["loss_target"] = _jax.random.normal(_jax.random.fold_in(key, 0), shape, _jnp.float32)
    for i, name in enumerate(TWIN_WEIGHTS):
        w = inp[name].astype(_jnp.float32)
        if MOMENT_SCALE is None:
            s = _jnp.sqrt(_jnp.mean(_jnp.square(w)) + 1e-30)
        else:
            s = MOMENT_SCALE[name]
        km, kv = _jax.random.split(_jax.random.fold_in(key, i + 1))
        out[name] = w
        out["m_" + name] = s * _jax.random.normal(km, w.shape, _jnp.float32)
        out["v_" + name] = (s * s) * _jax.random.uniform(kv, w.shape, _jnp.float32, 0.5, 1.5)
    if N_MICROBATCH > 1:
        for name, axis in PER_EXAMPLE_BATCH_AXIS.items():
            out[name] = _to_microbatches(out[name], axis)
    return {'x': out['x'], 'p': out['p'], 'pre_mix_norm': out['pre_mix_norm'], 'w_in': out['w_in'], 'a_ln_g': out['a_ln_g'], 'a_ln_b': out['a_ln_b'], 'a_spatial_w': out['a_spatial_w'], 'a_spatial_b': out['a_spatial_b'], 'a_out': out['a_out'], 'b_gk': out['b_gk'], 'b_gk_bias': out['b_gk_bias'], 'b_out_norm': out['b_out_norm'], 'b_out': out['b_out'], 'w_mix_out': out['w_mix_out'], 'post_mix_norm': out['post_mix_norm'], 'pre_ffn_norm': out['pre_ffn_norm'], 'w_up': out['w_up'], 'conv_w': out['conv_w'], 'conv_b': out['conv_b'], 'w_down': out['w_down'], 'post_ffn_norm': out['post_ffn_norm'], 'w_ple': out['w_ple'], 'w_ple_gate': out['w_ple_gate'], 'post_ple_norm': out['post_ple_norm'], 'loss_target': out['loss_target'], 'm_pre_mix_norm': out['m_pre_mix_norm'], 'm_w_in': out['m_w_in'], 'm_a_ln_g': out['m_a_ln_g'], 'm_a_ln_b': out['m_a_ln_b'], 'm_a_spatial_w': out['m_a_spatial_w'], 'm_a_spatial_b': out['m_a_spatial_b'], 'm_a_out': out['m_a_out'], 'm_b_gk': out['m_b_gk'], 'm_b_gk_bias': out['m_b_gk_bias'], 'm_b_out_norm': out['m_b_out_norm'], 'm_b_out': out['m_b_out'], 'm_w_mix_out': out['m_w_mix_out'], 'm_post_mix_norm': out['m_post_mix_norm'], 'm_pre_ffn_norm': out['m_pre_ffn_norm'], 'm_w_up': out['m_w_up'], 'm_conv_w': out['m_conv_w'], 'm_conv_b': out['m_conv_b'], 'm_w_down': out['m_w_down'], 'm_post_ffn_norm': out['m_post_ffn_norm'], 'm_w_ple': out['m_w_ple'], 'm_w_ple_gate': out['m_w_ple_gate'], 'm_post_ple_norm': out['m_post_ple_norm'], 'v_pre_mix_norm': out['v_pre_mix_norm'], 'v_w_in': out['v_w_in'], 'v_a_ln_g': out['v_a_ln_g'], 'v_a_ln_b': out['v_a_ln_b'], 'v_a_spatial_w': out['v_a_spatial_w'], 'v_a_spatial_b': out['v_a_spatial_b'], 'v_a_out': out['v_a_out'], 'v_b_gk': out['v_b_gk'], 'v_b_gk_bias': out['v_b_gk_bias'], 'v_b_out_norm': out['v_b_out_norm'], 'v_b_out': out['v_b_out'], 'v_w_mix_out': out['v_w_mix_out'], 'v_post_mix_norm': out['v_post_mix_norm'], 'v_pre_ffn_norm': out['v_pre_ffn_norm'], 'v_w_up': out['v_w_up'], 'v_conv_w': out['v_conv_w'], 'v_conv_b': out['v_conv_b'], 'v_w_down': out['v_w_down'], 'v_post_ffn_norm': out['v_post_ffn_norm'], 'v_w_ple': out['v_w_ple'], 'v_w_ple_gate': out['v_w_ple_gate'], 'v_post_ple_norm': out['v_post_ple_norm']}


def _loss(weights, diff, rest, loss_target):
    with _jax.named_scope("forward"):
        args = {**rest, TWIN_DIFF_INPUT: diff, **{k: w.astype(_WEIGHT_DTYPES[k]) for k, w in weights.items()}}
        y = _forward(args)
    with _jax.named_scope("loss_head"):
        err = _jnp.square(y.astype(_jnp.float32) - loss_target)
        return 0.5 * _jnp.sum(_jnp.mean(err, axis=-1)) if err.ndim else 0.5 * err


def _adamw(w, g, m, v):
    m = ADAM_B1 * m + (1.0 - ADAM_B1) * g
    v = ADAM_B2 * v + (1.0 - ADAM_B2) * _jnp.square(g)
    m_hat = m / (1.0 - ADAM_B1 ** ADAM_STEP)
    v_hat = v / (1.0 - ADAM_B2 ** ADAM_STEP)
    delta = -ADAM_LR * (m_hat / (_jnp.sqrt(v_hat) + ADAM_EPS) + ADAM_WD * w)
    return delta, m, v


def reference(x, p, pre_mix_norm, w_in, a_ln_g, a_ln_b, a_spatial_w, a_spatial_b, a_out, b_gk, b_gk_bias, b_out_norm, b_out, w_mix_out, post_mix_norm, pre_ffn_norm, w_up, conv_w, conv_b, w_down, post_ffn_norm, w_ple, w_ple_gate, post_ple_norm, loss_target, m_pre_mix_norm, m_w_in, m_a_ln_g, m_a_ln_b, m_a_spatial_w, m_a_spatial_b, m_a_out, m_b_gk, m_b_gk_bias, m_b_out_norm, m_b_out, m_w_mix_out, m_post_mix_norm, m_pre_ffn_norm, m_w_up, m_conv_w, m_conv_b, m_w_down, m_post_ffn_norm, m_w_ple, m_w_ple_gate, m_post_ple_norm, v_pre_mix_norm, v_w_in, v_a_ln_g, v_a_ln_b, v_a_spatial_w, v_a_spatial_b, v_a_out, v_b_gk, v_b_gk_bias, v_b_out_norm, v_b_out, v_w_mix_out, v_post_mix_norm, v_pre_ffn_norm, v_w_up, v_conv_w, v_conv_b, v_w_down, v_post_ffn_norm, v_w_ple, v_w_ple_gate, v_post_ple_norm):
    given = dict(x=x, p=p, pre_mix_norm=pre_mix_norm, w_in=w_in, a_ln_g=a_ln_g, a_ln_b=a_ln_b, a_spatial_w=a_spatial_w, a_spatial_b=a_spatial_b, a_out=a_out, b_gk=b_gk, b_gk_bias=b_gk_bias, b_out_norm=b_out_norm, b_out=b_out, w_mix_out=w_mix_out, post_mix_norm=post_mix_norm, pre_ffn_norm=pre_ffn_norm, w_up=w_up, conv_w=conv_w, conv_b=conv_b, w_down=w_down, post_ffn_norm=post_ffn_norm, w_ple=w_ple, w_ple_gate=w_ple_gate, post_ple_norm=post_ple_norm, loss_target=loss_target, m_pre_mix_norm=m_pre_mix_norm, m_w_in=m_w_in, m_a_ln_g=m_a_ln_g, m_a_ln_b=m_a_ln_b, m_a_spatial_w=m_a_spatial_w, m_a_spatial_b=m_a_spatial_b, m_a_out=m_a_out, m_b_gk=m_b_gk, m_b_gk_bias=m_b_gk_bias, m_b_out_norm=m_b_out_norm, m_b_out=m_b_out, m_w_mix_out=m_w_mix_out, m_post_mix_norm=m_post_mix_norm, m_pre_ffn_norm=m_pre_ffn_norm, m_w_up=m_w_up, m_conv_w=m_conv_w, m_conv_b=m_conv_b, m_w_down=m_w_down, m_post_ffn_norm=m_post_ffn_norm, m_w_ple=m_w_ple, m_w_ple_gate=m_w_ple_gate, m_post_ple_norm=m_post_ple_norm, v_pre_mix_norm=v_pre_mix_norm, v_w_in=v_w_in, v_a_ln_g=v_a_ln_g, v_a_ln_b=v_a_ln_b, v_a_spatial_w=v_a_spatial_w, v_a_spatial_b=v_a_spatial_b, v_a_out=v_a_out, v_b_gk=v_b_gk, v_b_gk_bias=v_b_gk_bias, v_b_out_norm=v_b_out_norm, v_b_out=v_b_out, v_w_mix_out=v_w_mix_out, v_post_mix_norm=v_post_mix_norm, v_pre_ffn_norm=v_pre_ffn_norm, v_w_up=v_w_up, v_conv_w=v_conv_w, v_conv_b=v_conv_b, v_w_down=v_w_down, v_post_ffn_norm=v_post_ffn_norm, v_w_ple=v_w_ple, v_w_ple_gate=v_w_ple_gate, v_post_ple_norm=v_post_ple_norm)
    weights = {n: given[n] for n in TWIN_WEIGHTS}
    shared = {n: given[n] for n in SHARED_INPUTS}
    per_example = {n: given[n] for n in ['x', 'p']}
    grad_fn = _jax.value_and_grad(_loss, argnums=(0, 1))

    def one_microbatch(ex, loss_target):
        ex = dict(ex)
        diff = ex.pop(TWIN_DIFF_INPUT)
        return grad_fn(weights, diff, {**shared, **ex}, loss_target)

    if N_MICROBATCH == 1:
        loss, (grad_w, grad_x) = one_microbatch(per_example, given["loss_target"])
    else:
        def body(carry, xs):
            loss_sum, grad_sum = carry
            l_k, (gw_k, gx_k) = one_microbatch(xs[0], xs[1])
            with _jax.named_scope("update"):
                return (loss_sum + l_k, _jax.tree.map(_jnp.add, grad_sum, gw_k)), gx_k

        init = (_jnp.zeros((), _jnp.float32), _jax.tree.map(_jnp.zeros_like, weights))
        (loss, grad_w), grad_x = _jax.lax.scan(body, init, (per_example, given["loss_target"]))
    with _jax.named_scope("update"):
        delta_w, new_m, new_v = {}, {}, {}
        for n in TWIN_WEIGHTS:
            delta_w[n], new_m[n], new_v[n] = _adamw(weights[n], grad_w[n], given["m_" + n], given["v_" + n])
    return (loss, grad_x, *[grad_w[n] for n in TWIN_WEIGHTS], *[delta_w[n] for n in TWIN_WEIGHTS],
            *[new_m[n] for n in TWIN_WEIGHTS], *[new_v[n] for n in TWIN_WEIGHTS])
```

```python
import functools
import math

import jax
import jax.numpy as jnp
from jax import lax
from jax.experimental import pallas as pl
from jax.experimental.pallas import tpu as pltpu

F32 = jnp.float32
BF = jnp.bfloat16
SDS = jax.ShapeDtypeStruct
MESH = pl.DeviceIdType.MESH

EPS = 1e-6
D = 1024
A_W = 512
A_G, A_C = 8, 128
A_GD = A_W // A_G
B_H, B_HK, B_HV = 4, 128, 256
B_C = 64
B_RANK = 16
D_FF = 2816
PLE = 256
ZW = 6272
LANE = 128
VMEM_LIMIT = 60 * 1024 * 1024

ADAM_LR, ADAM_B1, ADAM_B2, ADAM_EPS, ADAM_WD, ADAM_STEP = 0.001, 0.9, 0.999, 1e-08, 0.01, 10

_GC = math.sqrt(2.0 / math.pi)
_GA = 0.044715

SHARDED = (
    ("w_in", (1024, 1540), "col"), ("a_out", (512, 256), "col"), ("b_gk", (16, 128), "col"),
    ("b_out", (256, 1024), "row"), ("w_mix_out", (256, 1024), "row"), ("w_up", (1024, 1408), "col"),
    ("conv_w", (3, 1408), "col"), ("w_down", (704, 1024), "row"), ("w_ple", (256, 256), "col"),
    ("w_ple_gate", (256, 1024), "row"),
)
FLAT_ROWS = 4640
SMALL = (("pre_mix_norm", 1024), ("a_ln_g", 512), ("a_ln_b", 512), ("a_spatial_w", 131072),
         ("a_spatial_b", 1024), ("b_gk_bias", 512), ("b_out_norm", 256), ("post_mix_norm", 1024),
         ("pre_ffn_norm", 1024), ("conv_b", 5632), ("post_ffn_norm", 1024), ("post_ple_norm", 1024))
SMALL_ROWS = 1136


def _pc(body, **kw):
    return pl.pallas_call(body, **kw)


def _cp(n):
    return pltpu.CompilerParams(dimension_semantics=("arbitrary",) * n, vmem_limit_bytes=VMEM_LIMIT)


def _const(shape):
    nd = len(shape)
    return pl.BlockSpec(shape, lambda *_: (0,) * nd, pipeline_mode=pl.Buffered(1))


def _acc(shape):
    nd = len(shape)
    return pl.BlockSpec(shape, lambda *_: (0,) * nd)


def _dot(a, b):
    return jnp.dot(a, b, preferred_element_type=F32)


def _dot_nt(a, b):
    return lax.dot_general(a, b, (((1,), (1,)), ((), ())), preferred_element_type=F32)


def _dot_tn(a, b):
    return lax.dot_general(a, b, (((0,), (0,)), ((), ())), preferred_element_type=F32)


def _gelu(x):
    return 0.5 * x * (1.0 + jnp.tanh(_GC * (x + _GA * x * x * x)))


def _gelu_and_grad(x):
    x2 = x * x
    t = jnp.tanh(_GC * x * (1.0 + _GA * x2))
    return 0.5 * x * (1.0 + t), 0.5 * (1.0 + t) + 0.5 * x * (1.0 - t * t) * (_GC * (1.0 + 3.0 * _GA * x2))


def _log_sigmoid(x):
    return jnp.minimum(x, 0.0) - jnp.log(1.0 + jnp.exp(-jnp.abs(x)))


def _rms(x, g):
    return x * lax.rsqrt(jnp.mean(x * x, axis=-1, keepdims=True) + EPS) * g


def _rms_bwd(dy, x, g):
    r = lax.rsqrt(jnp.mean(x * x, axis=-1, keepdims=True) + EPS)
    n = x * r
    dn = dy * g
    dx = r * (dn - n * jnp.mean(dn * n, axis=-1, keepdims=True))
    return dx, jnp.sum(dy * n, axis=0, keepdims=True)


def _ldot3(l, x):
    h = x.astype(BF)
    r = x - h.astype(F32)
    m = r.astype(BF)
    lo = (r - m.astype(F32)).astype(BF)
    return _dot(l, h) + _dot(l, m) + _dot(l, lo)


def norm_matmul(x, g, w, bn, t, name):
    s, dm = x.shape
    n = w.shape[1]

    def body(x_ref, g_ref, w_ref, a_ref, z_ref, a_sc):
        @pl.when(pl.program_id(1) == 0)
        def _():
            a = _rms(x_ref[...], g_ref[...]).astype(BF)
            a_sc[...] = a
            a_ref[...] = a

        z_ref[...] = _dot(a_sc[...], w_ref[...]).astype(BF)

    return _pc(
        body, name=name, grid=(s // t, n // bn),
        in_specs=[pl.BlockSpec((t, dm), lambda i, j: (i, 0)), _const((1, dm)),
                  pl.BlockSpec((dm, bn), lambda i, j: (0, j))],
        out_specs=[pl.BlockSpec((t, dm), lambda i, j: (i, 0)), pl.BlockSpec((t, bn), lambda i, j: (i, j))],
        out_shape=[SDS((s, dm), BF), SDS((s, n), BF)],
        scratch_shapes=[pltpu.VMEM((t, dm), BF)], compiler_params=_cp(2))(x, g, w)


def _sgu_masked(ws_ref):
    r = lax.broadcasted_iota(jnp.int32, (A_C, A_C), 0)
    c = lax.broadcasted_iota(jnp.int32, (A_C, A_C), 1)
    return [jnp.where(c <= r, ws_ref[g], 0.0).astype(BF) for g in range(A_G)]


def _sgu_recompute(v, lng, lnb):
    gv, dgv = _gelu_and_grad(v)
    mu = jnp.mean(gv, axis=-1, keepdims=True)
    xc = gv - mu
    rstd = lax.rsqrt(jnp.mean(xc * xc, axis=-1, keepdims=True) + EPS)
    xhat = xc * rstd
    return dgv, rstd, xhat, (xhat * lng + lnb).astype(BF)


def sgu_fwd(z, ln_g, ln_b, w_s, bs_t, t):
    s = z.shape[0]

    def body(u_ref, v_ref, g_ref, b_ref, ws_ref, bs_ref, sa_ref, s_sc):
        wm = _sgu_masked(ws_ref)
        for ci in range(t // A_C):
            rows = pl.ds(ci * A_C, A_C)
            _, _, _, vn = _sgu_recompute(v_ref[rows, :].astype(F32), g_ref[...], b_ref[...])
            for g in range(A_G):
                cols = slice(g * A_GD, (g + 1) * A_GD)
                s_sc[:, cols] = _dot(wm[g], vn[:, cols]) + bs_ref[:, g:g + 1]
            sa_ref[rows, :] = (_gelu(u_ref[rows, :].astype(F32)) * s_sc[...]).astype(BF)

    return _pc(
        body, name="sgu_fwd", grid=(s // t,),
        in_specs=[pl.BlockSpec((t, A_W), lambda i: (i, 0)), pl.BlockSpec((t, A_W), lambda i: (i, 1)),
                  _const((1, A_W)), _const((1, A_W)), _const((A_G, A_C, A_C)), _const((A_C, A_G))],
        out_specs=pl.BlockSpec((t, A_W), lambda i: (i, 0)),
        out_shape=SDS((s, A_W), BF),
        scratch_shapes=[pltpu.VMEM((A_C, A_W), F32)], compiler_params=_cp(1))(z, z, ln_g, ln_b, w_s, bs_t)


def _gla_decays(qk_ref, lr_ref, wgk_ref, bias_ref, l_ref, t):
    nc = t // B_C
    q = qk_ref[:, :B_HK].astype(F32) * (B_HK ** -0.5)
    k = qk_ref[:, B_HK:].astype(F32)
    pre = _dot(lr_ref[...], wgk_ref[...]) + bias_ref[...]
    la = _log_sigmoid(pre) * (1.0 / 16.0)
    b = _ldot3(l_ref[...], la)
    b3 = b.reshape(nc, B_C, B_HK)
    bl = jnp.broadcast_to(b3[:, B_C - 1:B_C, :], (nc, B_C, B_HK)).reshape(t, B_HK)
    eb, enb, etb = jnp.exp(b), jnp.exp(-b), jnp.exp(bl - b)
    return pre, b, bl, eb, enb, etb, q * eb, k * enb, k * etb


def gla_fwd(z, wgk, bias, wn, ltri, t):
    s = z.shape[0]
    nc = t // B_C

    def body(qk_ref, v_ref, og_ref, lr_ref, wgk_ref, bias_ref, wn_ref, l_ref, ob_ref, o_ref, st_ref, st_sc, o_sc):
        h = pl.program_id(1)

        @pl.when(pl.program_id(0) == 0)
        def _():
            st_sc[h] = jnp.zeros((B_HV, B_HK), F32)

        _, _, bl, _, _, _, qd, ki, kt = _gla_decays(qk_ref, lr_ref, wgk_ref, bias_ref, l_ref, t)
        qd, ki, kt = qd.astype(BF), ki.astype(BF), kt.astype(BF)
        vb = v_ref[...]
        sc = jnp.where(l_ref[...] > 0, _dot_nt(qd, ki), 0.0).astype(BF)
        o_sc[...] = _dot(sc, vb)
        for n in range(nc):
            rows = slice(n * B_C, (n + 1) * B_C)
            st = st_sc[h]
            stb = st.astype(BF)
            st_ref[n, 0] = stb
            o_sc[rows, :] += _dot_nt(qd[rows], stb)
            st_sc[h] = st * jnp.exp(bl[n * B_C:n * B_C + 1, :]) + _dot_tn(vb[rows], kt[rows])
        ob = o_sc[...].astype(BF)
        o_ref[...] = ob
        og = og_ref[...].astype(F32)
        ob_ref[...] = (_rms(ob.astype(F32), wn_ref[...]) * og * jax.nn.sigmoid(og)).astype(BF)

    return _pc(
        body, name="gla_fwd", grid=(s // t, B_H),
        in_specs=[pl.BlockSpec((t, 256), lambda i, h: (i, 4 + h)), pl.BlockSpec((t, 256), lambda i, h: (i, 8 + h)),
                  pl.BlockSpec((t, 256), lambda i, h: (i, 12 + h)), pl.BlockSpec((t, LANE), lambda i, h: (i, 48)),
                  pl.BlockSpec((LANE, B_HK), lambda i, h: (0, h)), pl.BlockSpec((1, B_HK), lambda i, h: (0, h)),
                  _const((1, B_HV)), _const((t, t))],
        out_specs=[pl.BlockSpec((t, B_HV), lambda i, h: (i, h)), pl.BlockSpec((t, B_HV), lambda i, h: (i, h)),
                   pl.BlockSpec((nc, 1, B_HV, B_HK), lambda i, h: (i, h, 0, 0))],
        out_shape=[SDS((s, D), BF), SDS((s, D), BF), SDS((s // B_C, B_H, B_HV, B_HK), BF)],
        scratch_shapes=[pltpu.VMEM((B_H, B_HV, B_HK), F32), pltpu.VMEM((t, B_HV), F32)],
        compiler_params=_cp(2))(z, z, z, z, wgk, bias, wn, ltri)


def mix_fwd(sa, ob, z, x, a_out, b_out, w_mix, g2, t):
    s = x.shape[0]

    def body(sa_ref, ob_ref, ga_ref, gb_ref, x_ref, ao_ref, bo_ref, wm_ref, g2_ref,
             ya_ref, yb_ref, mp_ref, mx_ref, h1_ref):
        ya = _dot(sa_ref[...], ao_ref[...]).astype(BF)
        yb = _dot(ob_ref[...], bo_ref[...]).astype(BF)
        ya_ref[...] = ya
        yb_ref[...] = yb
        mp = (jax.nn.sigmoid(ga_ref[...].astype(F32)) * ya.astype(F32)
              + jax.nn.sigmoid(gb_ref[...].astype(F32)) * yb.astype(F32)).astype(BF)
        mp_ref[...] = mp
        mx = _dot(mp, wm_ref[...]).astype(BF)
        mx_ref[...] = mx
        h1_ref[...] = x_ref[...] + _rms(mx.astype(F32), g2_ref[...])

    row = lambda w: pl.BlockSpec((t, w), lambda i: (i, 0))
    return _pc(
        body, name="mix_fwd", grid=(s // t,),
        in_specs=[row(A_W), row(D), pl.BlockSpec((t, D), lambda i: (i, 4)), pl.BlockSpec((t, D), lambda i: (i, 5)),
                  row(D), _const((A_W, D)), _const((D, D)), _const((D, D)), _const((1, D))],
        out_specs=[row(D)] * 5,
        out_shape=[SDS((s, D), BF)] * 4 + [SDS((s, D), F32)],
        compiler_params=_cp(1))(sa, ob, z, z, x, a_out, b_out, w_mix, g2)


def _conv(u, w, b):
    return b + w[0:1] * pltpu.roll(u, 2, 0) + w[1:2] * pltpu.roll(u, 1, 0) + w[2:3] * u


def ffn_gate_fwd(up, conv_w, conv_b, t):
    s = up.shape[0]
    bn = 1408
    hb = t // 8

    def body(ug_ref, uv_ref, hg_ref, hv_ref, wg_ref, wv_ref, bg_ref, bv_ref, ff_ref):
        live = (pl.program_id(1) > 0).astype(F32)

        def branch(u_ref, h_ref, w_ref, b_ref):
            ext = jnp.concatenate([h_ref[...].astype(F32) * live, u_ref[...].astype(F32)], axis=0)
            return _conv(ext, w_ref[...], b_ref[...])[8:]

        ff_ref[...] = (_gelu(branch(ug_ref, hg_ref, wg_ref, bg_ref)) * branch(uv_ref, hv_ref, wv_ref, bv_ref)).astype(BF)

    halo = lambda off: pl.BlockSpec((8, bn), lambda j, i: (jnp.maximum(i * hb - 1, 0), j + off))
    return _pc(
        body, name="ffn_gate_fwd", grid=(2, s // t),
        in_specs=[pl.BlockSpec((t, bn), lambda j, i: (i, j)), pl.BlockSpec((t, bn), lambda j, i: (i, j + 2)),
                  halo(0), halo(2),
                  pl.BlockSpec((3, bn), lambda j, i: (0, j)), pl.BlockSpec((3, bn), lambda j, i: (0, j + 2)),
                  pl.BlockSpec((1, bn), lambda j, i: (0, j)), pl.BlockSpec((1, bn), lambda j, i: (0, j + 2))],
        out_specs=pl.BlockSpec((t, bn), lambda j, i: (i, j)),
        out_shape=SDS((s, D_FF), BF), compiler_params=_cp(2))(up, up, up, up, conv_w, conv_w, conv_b, conv_b)


def out_fwd(ff, h1, p, tgt, w_down, w_pg, w_ple, g4, g5, t):
    s = h1.shape[0]

    def body(ff_ref, h1_ref, p_ref, t_ref, wd_ref, wpg_ref, wpl_ref, g4_ref, g5_ref,
             f_ref, h2_ref, pg_ref, pe_ref, dy_ref, loss_ref):
        @pl.when(pl.program_id(0) == 0)
        def _():
            loss_ref[...] = jnp.zeros((1, 1), F32)

        f = _dot(ff_ref[...], wd_ref[...]).astype(BF)
        f_ref[...] = f
        h2 = h1_ref[...] + _rms(f.astype(F32), g4_ref[...])
        h2b = h2.astype(BF)
        h2_ref[...] = h2b
        pg = _dot(h2b, wpg_ref[...]).astype(BF)
        pe = _dot(p_ref[...].astype(BF), wpl_ref[...]).astype(BF)
        pg_ref[...] = pg
        pe_ref[...] = pe
        y = h2 + _rms(jax.nn.sigmoid(pg.astype(F32)) * pe.astype(F32), g5_ref[...])
        err = y - t_ref[...]
        dy_ref[...] = err * (1.0 / D)
        loss_ref[...] += (0.5 / D) * jnp.sum(err * err)

    row = lambda w: pl.BlockSpec((t, w), lambda i: (i, 0))
    return _pc(
        body, name="out_fwd", grid=(s // t,),
        in_specs=[row(D_FF), row(D), row(PLE), row(D), _const((D_FF, D)), _const((D, D)), _const((PLE, D)),
                  _const((1, D)), _const((1, D))],
        out_specs=[row(D)] * 5 + [_acc((1, 1))],
        out_shape=[SDS((s, D), BF)] * 4 + [SDS((s, D), F32), SDS((1, 1), F32)],
        compiler_params=_cp(1))(ff, h1, p, tgt, w_down, w_pg, w_ple, g4, g5)


def out_bwd(dy, pg, pe, f, g5, g4, w_pg, w_down, t):
    s = dy.shape[0]

    def body(dy_ref, pg_ref, pe_ref, f_ref, g5_ref, g4_ref, wpg_ref, wd_ref,
             dh2_ref, dpe_ref, dpg_ref, df_ref, dff_ref, gg5_ref, gg4_ref):
        @pl.when(pl.program_id(0) == 0)
        def _():
            gg5_ref[...] = jnp.zeros((1, D), F32)
            gg4_ref[...] = jnp.zeros((1, D), F32)

        dy_ = dy_ref[...]
        pg_ = pg_ref[...].astype(F32)
        pe_ = pe_ref[...].astype(F32)
        sg = jax.nn.sigmoid(pg_)
        dple, dg5 = _rms_bwd(dy_, sg * pe_, g5_ref[...])
        gg5_ref[...] += dg5
        dpe_ref[...] = (dple * sg).astype(BF)
        dpg = (dple * pe_ * sg * (1.0 - sg)).astype(BF)
        dpg_ref[...] = dpg
        dh2 = dy_ + _dot_nt(dpg, wpg_ref[...])
        dh2_ref[...] = dh2
        df, dg4 = _rms_bwd(dh2, f_ref[...].astype(F32), g4_ref[...])
        gg4_ref[...] += dg4
        dfb = df.astype(BF)
        df_ref[...] = dfb
        dff_ref[...] = _dot_nt(dfb, wd_ref[...]).astype(BF)

    row = lambda w: pl.BlockSpec((t, w), lambda i: (i, 0))
    return _pc(
        body, name="out_bwd", grid=(s // t,),
        in_specs=[row(D), row(D), row(D), row(D), _const((1, D)), _const((1, D)), _const((D, D)), _const((D_FF, D))],
        out_specs=[row(D), row(D), row(D), row(D), row(D_FF), _acc((1, D)), _acc((1, D))],
        out_shape=[SDS((s, D), F32), SDS((s, D), BF), SDS((s, D), BF), SDS((s, D), BF), SDS((s, D_FF), BF),
                   SDS((1, D), F32), SDS((1, D), F32)],
        compiler_params=_cp(1))(dy, pg, pe, f, g5, g4, w_pg, w_down)


def ffn_gate_bwd(up, dff, conv_w, conv_b, t):
    s = up.shape[0]
    bn = 1408
    hb = t // 8
    nt = s // t
    r = t + 16

    def body(ug_ref, uv_ref, hbg_ref, hbv_ref, hag_ref, hav_ref, d_ref, da_ref, wg_ref, wv_ref, bg_ref, bv_ref,
             dug_ref, duv_ref, gwg_ref, gwv_ref, gbg_ref, gbv_ref):
        i = pl.program_id(1)

        @pl.when(i == 0)
        def _():
            gwg_ref[...] = jnp.zeros((3, bn), F32)
            gwv_ref[...] = jnp.zeros((3, bn), F32)
            gbg_ref[...] = jnp.zeros((1, bn), F32)
            gbv_ref[...] = jnp.zeros((1, bn), F32)

        first = (i > 0).astype(F32)
        last = (i < nt - 1).astype(F32)
        ext = lambda hb_ref, u_ref, ha_ref: jnp.concatenate(
            [hb_ref[...].astype(F32) * first, u_ref[...].astype(F32), ha_ref[...].astype(F32)], axis=0)
        ug, uv = ext(hbg_ref, ug_ref, hag_ref), ext(hbv_ref, uv_ref, hav_ref)
        wg, wv = wg_ref[...], wv_ref[...]
        cg, cv = _conv(ug, wg, bg_ref[...]), _conv(uv, wv, bv_ref[...])
        dff_ = jnp.concatenate([jnp.zeros((8, bn), F32), d_ref[...].astype(F32), da_ref[...].astype(F32) * last], axis=0)
        gl, dgl = _gelu_and_grad(cg)
        dg = dff_ * cv * dgl
        dv = dff_ * gl

        def back(dc, u, w, du_ref, gw_ref, gb_ref):
            du = w[2:3] * dc + w[1:2] * pltpu.roll(dc, r - 1, 0) + w[0:1] * pltpu.roll(dc, r - 2, 0)
            du_ref[...] = du[8:t + 8].astype(BF)
            dct = dc[8:t + 8]
            gw_ref[0:1, :] += jnp.sum(dct * pltpu.roll(u, 2, 0)[8:t + 8], axis=0, keepdims=True)
            gw_ref[1:2, :] += jnp.sum(dct * pltpu.roll(u, 1, 0)[8:t + 8], axis=0, keepdims=True)
            gw_ref[2:3, :] += jnp.sum(dct * u[8:t + 8], axis=0, keepdims=True)
            gb_ref[...] += jnp.sum(dct, axis=0, keepdims=True)

        back(dg, ug, wg, dug_ref, gwg_ref, gbg_ref)
        back(dv, uv, wv, duv_ref, gwv_ref, gbv_ref)

    tile = lambda off: pl.BlockSpec((t, bn), lambda j, i: (i, j + off))
    before = lambda off: pl.BlockSpec((8, bn), lambda j, i: (jnp.maximum(i * hb - 1, 0), j + off))
    after = lambda off: pl.BlockSpec((8, bn), lambda j, i: (jnp.minimum((i + 1) * hb, nt * hb - 1), j + off))
    cw = lambda off: pl.BlockSpec((3, bn), lambda j, i: (0, j + off))
    cb = lambda off: pl.BlockSpec((1, bn), lambda j, i: (0, j + off))
    return _pc(
        body, name="ffn_gate_bwd", grid=(2, nt),
        in_specs=[tile(0), tile(2), before(0), before(2), after(0), after(2), tile(0), after(0),
                  cw(0), cw(2), cb(0), cb(2)],
        out_specs=[tile(0), tile(0), cw(0), cw(0), cb(0), cb(0)],
        out_shape=[SDS((s, D_FF), BF), SDS((s, D_FF), BF), SDS((3, D_FF), F32), SDS((3, D_FF), F32),
                   SDS((1, D_FF), F32), SDS((1, D_FF), F32)],
        compiler_params=_cp(2))(up, up, up, up, up, up, dff, dff, conv_w, conv_w, conv_b, conv_b)


def nt_normbwd(dys, ws, xin, gain, dres, t, name):
    s = xin.shape[0]
    np_ = len(dys)

    def body(*refs):
        dy_refs, w_refs = refs[:np_], refs[np_:2 * np_]
        x_ref, g_ref, dres_ref, dx_ref, gg_ref = refs[2 * np_:]

        @pl.when(pl.program_id(0) == 0)
        def _():
            gg_ref[...] = jnp.zeros((1, D), F32)

        acc = _dot_nt(dy_refs[0][...], w_refs[0][...])
        for k in range(1, np_):
            acc += _dot_nt(dy_refs[k][...], w_refs[k][...])
        dxn, dg = _rms_bwd(acc, x_ref[...], g_ref[...])
        dx_ref[...] = dres_ref[...] + dxn
        gg_ref[...] += dg

    row = lambda w: pl.BlockSpec((t, w), lambda i: (i, 0))
    return _pc(
        body, name=name, grid=(s // t,),
        in_specs=[row(a.shape[1]) for a in dys] + [_const(w.shape) for w in ws] + [row(D), _const((1, D)), row(D)],
        out_specs=[row(D), _acc((1, D))],
        out_shape=[SDS((s, D), F32), SDS((1, D), F32)],
        compiler_params=_cp(1))(*dys, *ws, xin, gain, dres)


def mix_bwd(dh1, mx, z, ya, yb, g2, w_mix, a_out, b_out, t):
    s = dh1.shape[0]

    def body(dh_ref, mx_ref, ga_ref, gb_ref, ya_ref, yb_ref, g2_ref, wm_ref, ao_ref, bo_ref,
             dmx_ref, dya_ref, dyb_ref, dga_ref, dgb_ref, dsa_ref, dob_ref, gg2_ref):
        @pl.when(pl.program_id(0) == 0)
        def _():
            gg2_ref[...] = jnp.zeros((1, D), F32)

        dmx, dg2 = _rms_bwd(dh_ref[...], mx_ref[...].astype(F32), g2_ref[...])
        gg2_ref[...] += dg2
        dmxb = dmx.astype(BF)
        dmx_ref[...] = dmxb
        dmp = _dot_nt(dmxb, wm_ref[...])

        def gate(g_ref, y_ref, dy_ref, dg_ref, w_ref, dz_ref):
            sg = jax.nn.sigmoid(g_ref[...].astype(F32))
            dyb_ = (dmp * sg).astype(BF)
            dy_ref[...] = dyb_
            dg_ref[...] = (dmp * y_ref[...].astype(F32) * sg * (1.0 - sg)).astype(BF)
            dz_ref[...] = _dot_nt(dyb_, w_ref[...]).astype(BF)

        gate(ga_ref, ya_ref, dya_ref, dga_ref, ao_ref, dsa_ref)
        gate(gb_ref, yb_ref, dyb_ref, dgb_ref, bo_ref, dob_ref)

    row = lambda w: pl.BlockSpec((t, w), lambda i: (i, 0))
    return _pc(
        body, name="mix_bwd", grid=(s // t,),
        in_specs=[row(D), row(D), pl.BlockSpec((t, D), lambda i: (i, 4)), pl.BlockSpec((t, D), lambda i: (i, 5)),
                  row(D), row(D), _const((1, D)), _const((D, D)), _const((A_W, D)), _const((D, D))],
        out_specs=[row(D)] * 5 + [row(A_W), row(D), _acc((1, D))],
        out_shape=[SDS((s, D), BF)] * 5 + [SDS((s, A_W), BF), SDS((s, D), BF), SDS((1, D), F32)],
        compiler_params=_cp(1))(dh1, mx, z, z, ya, yb, g2, w_mix, a_out, b_out)


def sgu_bwd(z, dsa, ln_g, ln_b, w_s, bs_t, t):
    s = z.shape[0]
    nt = s // t

    def body(u_ref, v_ref, dsa_ref, g_ref, b_ref, ws_ref, bs_ref,
             duv_ref, glg_ref, glb_ref, gws_ref, gbs_ref, s_sc, dvn_sc, ds_acc):
        i = pl.program_id(0)

        @pl.when(i == 0)
        def _():
            glg_ref[...] = jnp.zeros((1, A_W), F32)
            glb_ref[...] = jnp.zeros((1, A_W), F32)
            gws_ref[...] = jnp.zeros((A_G, A_C, A_C), F32)
            ds_acc[...] = jnp.zeros((A_C, A_W), F32)

        wm = _sgu_masked(ws_ref)
        rr =lax.broadcasted_iota(jnp.int32, (A_C, A_C), 0)
        cc = lax.broadcasted_iota(jnp.int32, (A_C, A_C), 1)
        tril = cc <= rr
        lng = g_ref[...]
        for ci in range(t // A_C):
            rows = pl.ds(ci * A_C, A_C)
            dgv, rstd, xhat, vn = _sgu_recompute(v_ref[rows, :].astype(F32), lng, b_ref[...])
            for g in range(A_G):
                cols = slice(g * A_GD, (g + 1) * A_GD)
                s_sc[:, cols] = _dot(wm[g], vn[:, cols]) + bs_ref[:, g:g + 1]
            gu, dgu = _gelu_and_grad(u_ref[rows, :].astype(F32))
            dsa_ = dsa_ref[rows, :].astype(F32)
            ds = dsa_ * gu
            ds_acc[...] += ds
            dsb = ds.astype(BF)
            for g in range(A_G):
                cols = slice(g * A_GD, (g + 1) * A_GD)
                gws_ref[g] += jnp.where(tril, _dot_nt(dsb[:, cols], vn[:, cols]), 0.0)
                dvn_sc[:, cols] = _dot_tn(wm[g], dsb[:, cols])
            dvn = dvn_sc[...]
            glb_ref[...] += jnp.sum(dvn, axis=0, keepdims=True)
            glg_ref[...] += jnp.sum(dvn * xhat, axis=0, keepdims=True)
            dxh = dvn * lng
            dgv_ = rstd * (dxh - jnp.mean(dxh, axis=-1, keepdims=True)
                           - xhat * jnp.mean(dxh * xhat, axis=-1, keepdims=True))
            duv_ref[rows, :A_W] = (dsa_ * s_sc[...] * dgu).astype(BF)
            duv_ref[rows, A_W:] = (dgv_ * dgv).astype(BF)

        @pl.when(i == nt - 1)
        def _():
            acc = ds_acc[...]
            for g in range(A_G):
                gbs_ref[:, g:g + 1] = jnp.sum(acc[:, g * A_GD:(g + 1) * A_GD], axis=1, keepdims=True)

    return _pc(
        body, name="sgu_bwd", grid=(nt,),
        in_specs=[pl.BlockSpec((t, A_W), lambda i: (i, 0)), pl.BlockSpec((t, A_W), lambda i: (i, 1)),
                  pl.BlockSpec((t, A_W), lambda i: (i, 0)),
                  _const((1, A_W)), _const((1, A_W)), _const((A_G, A_C, A_C)), _const((A_C, A_G))],
        out_specs=[pl.BlockSpec((t, D), lambda i: (i, 0)), _acc((1, A_W)), _acc((1, A_W)),
                   _acc((A_G, A_C, A_C)), _acc((A_C, A_G))],
        out_shape=[SDS((s, D), BF), SDS((1, A_W), F32), SDS((1, A_W), F32), SDS((A_G, A_C, A_C), F32),
                   SDS((A_C, A_G), F32)],
        scratch_shapes=[pltpu.VMEM((A_C, A_W), F32), pltpu.VMEM((A_C, A_W), F32), pltpu.VMEM((A_C, A_W), F32)],
        compiler_params=_cp(1))(z, z, dsa, ln_g, ln_b, w_s, bs_t)


def gla_bwd(z, o, dob, states, wgk, bias, wn, ltri, ltri_t, t):
    s = z.shape[0]
    nt = s // t
    nc = t // B_C

    def body(qk_ref, v_ref, og_ref, lr_ref, o_ref, dob_ref, st_ref, wgk_ref, bias_ref, wn_ref, l_ref, lt_ref,
             dqk_ref, dv_ref, dog_ref, dpre_ref, gbias_ref, gwn_ref, dst_sc, dv_sc, dqd_sc, dkt_sc, ddec_sc):
        i = pl.program_id(0)
        h = pl.program_id(1)

        @pl.when((i == 0) & (h == 0))
        def _():
            gbias_ref[...] = jnp.zeros((B_H, 1, B_HK), F32)
            gwn_ref[...] = jnp.zeros((1, B_HV), F32)

        @pl.when(i == 0)
        def _():
            dst_sc[h] = jnp.zeros((B_HV, B_HK), F32)

        pre, b, bl, eb, enb, etb, qd, ki, kt = _gla_decays(qk_ref, lr_ref, wgk_ref, bias_ref, l_ref, t)
        qdb, kib, ktb = qd.astype(BF), ki.astype(BF), kt.astype(BF)
        vb = v_ref[...]
        o_ = o_ref[...].astype(F32)
        og = og_ref[...].astype(F32)
        sog = jax.nn.sigmoid(og)
        dob_ = dob_ref[...].astype(F32)
        wn_ = wn_ref[...]
        don = dob_ * og * sog
        do, dwn = _rms_bwd(don, o_, wn_)
        gwn_ref[...] += dwn
        dog_ref[...] = (dob_ * _rms(o_, wn_) * sog * (1.0 + og * (1.0 - sog))).astype(BF)
        dob16 = do.astype(BF)
        keep, keep_t = l_ref[...] > 0, lt_ref[...] > 0
        sc_t = jnp.where(keep_t, _dot_nt(kib, qdb), 0.0).astype(BF)
        dsc = jnp.where(keep, _dot_nt(dob16, vb), 0.0).astype(BF)
        dsc_t = jnp.where(keep_t, _dot_nt(vb, dob16), 0.0).astype(BF)
        dv_sc[...] = _dot(sc_t, dob16)
        dqd_sc[...] = _dot(dsc, kib)
        dki = _dot(dsc_t, qdb)
        for n in reversed(range(nc)):
            rows = slice(n * B_C, (n + 1) * B_C)
            dst = dst_sc[h]
            dstb = dst.astype(BF)
            stp = st_ref[n, 0]
            dv_sc[rows, :] += _dot_nt(ktb[rows], dstb)
            dkt_sc[rows, :] = _dot(vb[rows], dstb)
            dqd_sc[rows, :] += _dot(dob16[rows], stp)
            dec = jnp.exp(bl[n * B_C:n * B_C + 1, :])
            ddec_sc[n] = jnp.sum(dst * stp.astype(F32), axis=0, keepdims=True) * dec
            dst_sc[h] = dst * dec + _dot_tn(dob16[rows], qdb[rows])
        dqd, dkt = dqd_sc[...], dkt_sc[...]
        dv_ref[...] = dv_sc[...].astype(BF)
        dqk_ref[:, :B_HK] = (dqd * eb * (B_HK ** -0.5)).astype(BF)
        dqk_ref[:, B_HK:] = (dki * enb + dkt * etb).astype(BF)
        dktkt = dkt * kt
        db3 = (dqd * qd - dki * ki - dktkt).reshape(nc, B_C, B_HK)
        dbl = jnp.sum(dktkt.reshape(nc, B_C, B_HK), axis=1, keepdims=True) + ddec_sc[...]
        last = lax.broadcasted_iota(jnp.int32, (nc, B_C, B_HK), 1) == B_C - 1
        db = (db3 + jnp.where(last, dbl, 0.0)).reshape(t, B_HK)
        dla = _ldot3(lt_ref[...], db)
        dpre = dla * (1.0 / 16.0) * (1.0 - jax.nn.sigmoid(pre))
        dpre_ref[...] = dpre.astype(BF)
        gbias_ref[h] += jnp.sum(dpre, axis=0, keepdims=True)

    rv = lambda i: nt - 1 - i
    return _pc(
        body, name="gla_bwd", grid=(nt, B_H),
        in_specs=[pl.BlockSpec((t, 256), lambda i, h: (rv(i), 4 + h)), pl.BlockSpec((t, 256), lambda i, h: (rv(i), 8 + h)),
                  pl.BlockSpec((t, 256), lambda i, h: (rv(i), 12 + h)), pl.BlockSpec((t, LANE), lambda i, h: (rv(i), 48)),
                  pl.BlockSpec((t, B_HV), lambda i, h: (rv(i), h)), pl.BlockSpec((t, B_HV), lambda i, h: (rv(i), h)),
                  pl.BlockSpec((nc, 1, B_HV, B_HK), lambda i, h: (rv(i), h, 0, 0)),
                  pl.BlockSpec((LANE, B_HK), lambda i, h: (0, h)), pl.BlockSpec((1, B_HK), lambda i, h: (0, h)),
                  _const((1, B_HV)), _const((t, t)), _const((t, t))],
        out_specs=[pl.BlockSpec((t, 256), lambda i, h: (rv(i), h)), pl.BlockSpec((t, B_HV), lambda i, h: (rv(i), h)),
                   pl.BlockSpec((t, B_HV), lambda i, h: (rv(i), h)), pl.BlockSpec((t, B_HK), lambda i, h: (rv(i), h)),
                   _acc((B_H, 1, B_HK)), _acc((1, B_HV))],
        out_shape=[SDS((s, D), BF), SDS((s, D), BF), SDS((s, D), BF), SDS((s, B_H * B_HK), BF),
                   SDS((B_H, 1, B_HK), F32), SDS((1, B_HV), F32)],
        scratch_shapes=[pltpu.VMEM((B_H, B_HV, B_HK), F32), pltpu.VMEM((t, B_HV), F32), pltpu.VMEM((t, B_HK), F32),
                        pltpu.VMEM((t, B_HK), F32), pltpu.VMEM((nc, 1, B_HK), F32)],
        compiler_params=_cp(2))(z, z, z, z, o, dob, states, wgk, bias, wn, ltri, ltri_t)


def mm_tn(a, b, name, col=None, tk=512):
    s = a.shape[0]
    m = LANE if col is not None else a.shape[1]
    n = b.shape[1]
    bn = next(c for c in (1024, 1408, 512, 256, 128) if n % c == 0 and m * c * 4 <= 6 * 1024 * 1024)
    nk = s // tk

    def body(a_ref, b_ref, o_ref, acc):
        k = pl.program_id(1)

        @pl.when(k == 0)
        def _():
            acc[...] = jnp.zeros((m, bn), F32)

        acc[...] += _dot_tn(a_ref[...].astype(BF), b_ref[...])

        @pl.when(k == nk - 1)
        def _():
            o_ref[...] = acc[...].astype(BF)

    ac = 0 if col is None else col
    return _pc(
        body, name=name, grid=(n // bn, nk),
        in_specs=[pl.BlockSpec((tk, m), lambda j, k: (k, ac)), pl.BlockSpec((tk, bn), lambda j, k: (k, j))],
        out_specs=pl.BlockSpec((m, bn), lambda j, k: (0, j)),
        out_shape=SDS((m, n), BF), scratch_shapes=[pltpu.VMEM((m, bn), F32)], compiler_params=_cp(2))(a, b)


def mm_nt_small(a, w, t, name):
    s, k = a.shape
    n = w.shape[0]

    def body(a_ref, w_ref, o_ref):
        o_ref[...] = _dot_nt(a_ref[...], w_ref[...]).astype(BF)

    return _pc(body, name=name, grid=(s // t,),
               in_specs=[pl.BlockSpec((t, k), lambda i: (i, 0)), _const((n, k))],
               out_specs=pl.BlockSpec((t, n), lambda i: (i, 0)), out_shape=SDS((s, n), BF),
               compiler_params=_cp(1))(a, w)


def _adamw(w, g, m, v):
    m = ADAM_B1 * m + (1.0 - ADAM_B1) * g
    v = ADAM_B2 * v + (1.0 - ADAM_B2) * (g * g)
    m_hat = m / (1.0 - ADAM_B1 ** ADAM_STEP)
    v_hat = v / (1.0 - ADAM_B2 ** ADAM_STEP)
    return -ADAM_LR * (m_hat / (jnp.sqrt(v_hat) + ADAM_EPS) + ADAM_WD * w), m, v


def adamw_rows(g, w, m, v, row0, name):
    rows, cols = w.shape
    br = rows
    for cand in (256, 128, 64, 32, 16, 8):
        if rows % cand == 0:
            br = cand
            break

    def body(g_ref, w_ref, m_ref, v_ref, go_ref, d_ref, mo_ref, vo_ref):
        g_ = g_ref[...]
        go_ref[...] = g_
        d_ref[...], mo_ref[...], vo_ref[...] = _adamw(w_ref[...], g_, m_ref[...], v_ref[...])

    blk = pl.BlockSpec((br, cols), lambda i: (i, 0))
    assert row0 % br == 0
    return _pc(body, name=name, grid=(rows // br,),
               in_specs=[pl.BlockSpec((br, cols), lambda i: (row0 // br + i, 0)), blk, blk, blk],
               out_specs=[blk] * 4, out_shape=[SDS((rows, cols), F32)] * 4, compiler_params=_cp(1))(g, w, m, v)


def _pos():
    return lax.axis_index("x"), lax.axis_index("y"), lax.axis_index("c")


def _other_chips(x, y):
    return [(1 - x, y), (x, 1 - y), (1 - x, 1 - y)]


_ANY = pl.BlockSpec(memory_space=pltpu.HBM)


def gather_weights(flat):
    r = flat.shape[0]
    rh = r // 2

    def body(in_ref, out_ref, ssem, rsem, lsem):
        x, y, c = _pos()
        me = 2 * x + y
        mine = pl.ds(pl.multiple_of(c * rh, 16), rh)
        theirs = pl.ds(pl.multiple_of((1 - c) * rh, 16), rh)
        chips = _other_chips(x, y)
        own = pltpu.make_async_copy(in_ref, out_ref.at[me], lsem)
        own.start()

        def copy(k, src, dst, to):
            return pltpu.make_async_remote_copy(src_ref=src, dst_ref=dst, send_sem=ssem.at[k], recv_sem=rsem.at[k],
                                                device_id=to, device_id_type=MESH)

        started = []
        for j, (cx, cy) in enumerate(chips):
            cp = copy(j, in_ref.at[mine], out_ref.at[me, mine], (cx, cy, c))
            cp.start()
            started.append(cp)
        for j, (cx, cy) in enumerate(chips):
            blk = out_ref.at[2 * cx + cy, mine]
            copy(j, blk, blk, (cx, cy, c)).wait_recv()
            cp = copy(3 + j, blk, blk, (x, y, 1 - c))
            cp.start()
            started.append(cp)
        for j, (cx, cy) in enumerate(chips):
            blk = out_ref.at[2 * cx + cy, theirs]
            copy(3 + j, blk, blk, (x, y, 1 - c)).wait_recv()
        for cp in started:
            cp.wait_send()
        own.wait()

    return _pc(body, name="gather_weights", in_specs=[_ANY], out_specs=_ANY,
               out_shape=SDS((4, r, 1024), flat.dtype),
               scratch_shapes=[pltpu.SemaphoreType.DMA((6,)), pltpu.SemaphoreType.DMA((6,)), pltpu.SemaphoreType.DMA])(flat)


def swap_halves(g):
    _, r, w = g.shape
    rh = r // 2

    def body(g_ref, mine_ref, sib_ref, ssem, rsem, lsem):
        x, y, c = _pos()
        keep = pl.ds(pl.multiple_of(c * rh, 16), rh)
        give = pl.ds(pl.multiple_of((1 - c) * rh, 16), rh)
        own = pltpu.make_async_copy(g_ref.at[:, keep], mine_ref, lsem)
        own.start()
        cp = pltpu.make_async_remote_copy(src_ref=g_ref.at[:, give], dst_ref=sib_ref, send_sem=ssem, recv_sem=rsem,
                                          device_id=(x, y, 1 - c), device_id_type=MESH)
        cp.start()
        cp.wait()
        own.wait()

    return _pc(body, name="swap_halves", in_specs=[_ANY], out_specs=[_ANY, _ANY],
               out_shape=[SDS((4, rh, w), g.dtype)] * 2,
               scratch_shapes=[pltpu.SemaphoreType.DMA, pltpu.SemaphoreType.DMA, pltpu.SemaphoreType.DMA])(g)


def add2(a, b, name):
    n, r, w = a.shape
    br = 464

    def body(a_ref, b_ref, o_ref):
        o_ref[...] = (a_ref[...].astype(F32) + b_ref[...].astype(F32)).astype(BF)

    blk = pl.BlockSpec((1, br, w), lambda j, i: (j, i, 0))
    return _pc(body, name=name, grid=(n, r // br), in_specs=[blk, blk], out_specs=blk, out_shape=SDS((n, r, w), BF),
               compiler_params=_cp(2))(a, b)


def exchange_chips(p):
    _, rh, w = p.shape

    def body(p_ref, own_ref, got_ref, ssem, rsem, lsem):
        x, y, c = _pos()
        me = 2 * x + y
        own = pltpu.make_async_copy(p_ref.at[me], own_ref, lsem)
        own.start()
        cps = []
        for j, (cx, cy) in enumerate(_other_chips(x, y)):
            cp = pltpu.make_async_remote_copy(src_ref=p_ref.at[2 * cx + cy], dst_ref=got_ref.at[j], send_sem=ssem.at[j],
                                              recv_sem=rsem.at[j], device_id=(cx, cy, c), device_id_type=MESH)
            cp.start()
            cps.append(cp)
        for cp in cps:
            cp.wait()
        own.wait()

    return _pc(body, name="exchange_chips", in_specs=[_ANY], out_specs=[_ANY, _ANY],
               out_shape=[SDS((rh, w), p.dtype), SDS((3, rh, w), p.dtype)],
               scratch_shapes=[pltpu.SemaphoreType.DMA((3,)), pltpu.SemaphoreType.DMA((3,)), pltpu.SemaphoreType.DMA])(p)


def sum4(own, got):
    rh, w = own.shape
    br = 464

    def body(o_ref, g_ref, r_ref):
        r_ref[...] = ((o_ref[...].astype(F32) + g_ref[0].astype(F32))
                      + (g_ref[1].astype(F32) + g_ref[2].astype(F32)))

    return _pc(body, name="sum4", grid=(rh // br,),
               in_specs=[pl.BlockSpec((br, w), lambda i: (i, 0)), pl.BlockSpec((3, br, w), lambda i: (0, i, 0))],
               out_specs=pl.BlockSpec((br, w), lambda i: (i, 0)), out_shape=SDS((rh, w), F32),
               compiler_params=_cp(1))(own, got)


def join_halves(half):
    rh, w = half.shape

    def body(h_ref, out_ref, ssem, rsem, lsem):
        x, y, c = _pos()
        mine = pl.ds(pl.multiple_of(c * rh, 16), rh)
        own = pltpu.make_async_copy(h_ref, out_ref.at[mine], lsem)
        own.start()
        cp = pltpu.make_async_remote_copy(src_ref=h_ref, dst_ref=out_ref.at[mine], send_sem=ssem, recv_sem=rsem,
                                          device_id=(x, y, 1 - c), device_id_type=MESH)
        cp.start()
        cp.wait()
        own.wait()

    return _pc(body, name="join_halves", in_specs=[_ANY], out_specs=_ANY, out_shape=SDS((2 * rh, w), half.dtype),
               scratch_shapes=[pltpu.SemaphoreType.DMA, pltpu.SemaphoreType.DMA, pltpu.SemaphoreType.DMA])(half)


def allreduce_adamw_small(g, w, m, v):
    rows = g.shape[0]

    def body(g_ref, w_ref, m_ref, v_ref, go_ref, d_ref, mo_ref, vo_ref, buf, ssem, rsem):
        x, y, c = _pos()
        me = 4 * x + 2 * y + c
        flip = lambda a, f: 1 - a if f else a
        peers = [(flip(x, k & 4), flip(y, k & 2), flip(c, k & 1)) for k in range(1, 8)]
        cps = []
        for k, peer in enumerate(peers):
            cp = pltpu.make_async_remote_copy(src_ref=g_ref, dst_ref=buf.at[me], send_sem=ssem.at[k], recv_sem=rsem.at[k],
                                              device_id=peer, device_id_type=MESH)
            cp.start()
            cps.append(cp)
        buf[me] = g_ref[...]
        for k, (px, py, pc_) in enumerate(peers):
            pltpu.make_async_remote_copy(src_ref=g_ref, dst_ref=buf.at[4 * px + 2 * py + pc_], send_sem=ssem.at[k],
                                         recv_sem=rsem.at[k], device_id=(px, py, pc_), device_id_type=MESH).wait_recv()
        for cp in cps:
            cp.wait_send()
        tot = buf[0]
        for dev in range(1, 8):
            tot = tot + buf[dev]
        go_ref[...] = tot
        d_ref[...], mo_ref[...], vo_ref[...] = _adamw(w_ref[...], tot, m_ref[...], v_ref[...])

    vm = pl.BlockSpec(memory_space=pltpu.VMEM)
    return _pc(body, name="allreduce_adamw_small", in_specs=[vm] * 4, out_specs=[vm] * 4,
               out_shape=[SDS((rows, LANE), F32)] * 4,
               scratch_shapes=[pltpu.VMEM((8, rows, LANE), F32), pltpu.SemaphoreType.DMA((7,)),
                               pltpu.SemaphoreType.DMA((7,))],
               compiler_params=pltpu.CompilerParams(vmem_limit_bytes=VMEM_LIMIT))(g, w, m, v)


def _flat_rows(name, shape):
    n = shape[0] * shape[1] * (2 if name == "conv_w" else 1)
    return -(-n // 1024)


def _pack_weights(shards):
    parts = []
    for name, shape, _ in SHARDED:
        w = shards[name].reshape(shape)
        w = lax.bitcast_convert_type(w, BF) if name == "conv_w" else w.astype(BF)
        w = w.reshape(-1)
        parts.append(jnp.pad(w, (0, _flat_rows(name, shape) * 1024 - w.shape[0])))
    flat = jnp.concatenate(parts)
    return jnp.pad(flat, (0, FLAT_ROWS * 1024 - flat.shape[0])).reshape(FLAT_ROWS, 1024)


def _unpack_weights(gathered):
    out, off = {}, 0
    for name, shape, kind in SHARDED:
        nr = _flat_rows(name, shape)
        blk = gathered[:, off:off + nr, :].reshape(4, -1)
        off += nr
        if name == "conv_w":
            blk = lax.bitcast_convert_type(blk[:, :shape[0] * shape[1] * 2].reshape(4, shape[0], shape[1], 2), F32)
        else:
            blk = blk[:, :shape[0] * shape[1]].reshape(4, shape[0], shape[1])
        if kind == "col":
            out[name] = blk.transpose(1, 0, 2).reshape(shape[0], 4 * shape[1])
        else:
            out[name] = blk.reshape(4 * shape[0], shape[1])
    return out


def _grad_rows(shape):
    return -(-(shape[0] * shape[1]) // 1024)


def _pack_grads(full):
    parts = []
    for name, shape, kind in SHARDED:
        g = full[name].astype(BF)
        if kind == "col":
            g = g.reshape(shape[0], 4, shape[1]).transpose(1, 0, 2)
        g = g.reshape(4, -1)
        parts.append(jnp.pad(g, ((0, 0), (0, _grad_rows(shape) * 1024 - g.shape[1]))))
    flat = jnp.concatenate(parts, axis=1)
    return jnp.pad(flat, ((0, 0), (0, FLAT_ROWS * 1024 - flat.shape[1]))).reshape(4, FLAT_ROWS, 1024)


def kernel(x, p, pre_mix_norm, w_in, a_ln_g, a_ln_b, a_spatial_w, a_spatial_b, a_out, b_gk, b_gk_bias, b_out_norm, b_out, w_mix_out, post_mix_norm, pre_ffn_norm, w_up, conv_w, conv_b, w_down, post_ffn_norm, w_ple, w_ple_gate, post_ple_norm, loss_target, m_pre_mix_norm, m_w_in, m_a_ln_g, m_a_ln_b, m_a_spatial_w, m_a_spatial_b, m_a_out, m_b_gk, m_b_gk_bias, m_b_out_norm, m_b_out, m_w_mix_out, m_post_mix_norm, m_pre_ffn_norm, m_w_up, m_conv_w, m_conv_b, m_w_down, m_post_ffn_norm, m_w_ple, m_w_ple_gate, m_post_ple_norm, v_pre_mix_norm, v_w_in, v_a_ln_g, v_a_ln_b, v_a_spatial_w, v_a_spatial_b, v_a_out, v_b_gk, v_b_gk_bias, v_b_out_norm, v_b_out, v_w_mix_out, v_post_mix_norm, v_pre_ffn_norm, v_w_up, v_conv_w, v_conv_b, v_w_down, v_post_ffn_norm, v_w_ple, v_w_ple_gate, v_post_ple_norm):
    args = dict(locals())
    names = [n for n, _, _ in SHARDED] + [n for n, _ in SMALL]
    order = ['pre_mix_norm', 'w_in', 'a_ln_g', 'a_ln_b', 'a_spatial_w', 'a_spatial_b', 'a_out', 'b_gk', 'b_gk_bias',
             'b_out_norm', 'b_out', 'w_mix_out', 'post_mix_norm', 'pre_ffn_norm', 'w_up', 'conv_w', 'conv_b', 'w_down',
             'post_ffn_norm', 'w_ple', 'w_ple_gate', 'post_ple_norm']
    assert sorted(names) == sorted(order)
    s = x.shape[1]
    xs = x.reshape(s, D)
    ps = p.reshape(s, PLE)
    tgt = loss_target.reshape(s, D)
    t_big = min(1024, s)
    t_mid = min(512, s)
    t_small = min(256, s)

    full = _unpack_weights(gather_weights(_pack_weights({n: args[n] for n, _, _ in SHARDED})))
    wi = full["w_in"]
    seg = lambda a, b: wi[:, a:b]
    qk = [seg(1024 + h * B_HK, 1024 + (h + 1) * B_HK) for h in range(B_H)]
    kk = [seg(1536 + h * B_HK, 1536 + (h + 1) * B_HK) for h in range(B_H)]
    w_z = jnp.concatenate([seg(0, 1024)] + [m_ for h in range(B_H) for m_ in (qk[h], kk[h])]
                          + [seg(2048, 4096), seg(4112, 6160), seg(4096, 4112), jnp.zeros((D, LANE - B_RANK), BF)], axis=1)
    wgk = jnp.pad(full["b_gk"], ((0, LANE - B_RANK), (0, 0)))
    g1, g2, g3 = pre_mix_norm.reshape(1, D), post_mix_norm.reshape(1, D), pre_ffn_norm.reshape(1, D)
    g4, g5 = post_ffn_norm.reshape(1, D), post_ple_norm.reshape(1, D)
    ln_g, ln_b = a_ln_g.reshape(1, A_W), a_ln_b.reshape(1, A_W)
    w_s = a_spatial_w.reshape(A_G, A_C, A_C)
    bs_t = a_spatial_b.reshape(A_G, A_C).T
    gk_bias = b_gk_bias.reshape(1, B_H * B_HK)
    wn = b_out_norm.reshape(1, B_HV)
    cb = conv_b.reshape(1, 2 * D_FF)
    idx = jnp.arange(t_mid)
    ltri = ((idx[:, None] // B_C == idx[None, :] // B_C) & (idx[None, :] <= idx[:, None])).astype(BF)

    a, z = norm_matmul(xs, g1, w_z, 896, t_big, "in_proj")
    sa = sgu_fwd(z, ln_g, ln_b, w_s, bs_t, t_mid)
    ob, o, states = gla_fwd(z, wgk, gk_bias, wn, ltri, t_mid)
    ya, yb, mp, mx, h1 = mix_fwd(sa, ob, z, xs, full["a_out"], full["b_out"], full["w_mix_out"], g2, t_small)
    c, up = norm_matmul(h1, g3, full["w_up"], 1408, t_big, "up_proj")
    ff = ffn_gate_fwd(up, full["conv_w"], cb, t_small)
    f, h2, pg, pe, dy, loss = out_fwd(ff, h1, ps, tgt, full["w_down"], full["w_ple_gate"], full["w_ple"], g4, g5, t_small)

    dh2, dpe, dpg, df, dff, gg5, gg4 = out_bwd(dy, pg, pe, f, g5, g4, full["w_ple_gate"], full["w_down"], t_small)
    dup_g, dup_v, gcw_g, gcw_v, gcb_g, gcb_v = ffn_gate_bwd(up, dff, full["conv_w"], cb, t_small)
    dh1, gg3 = nt_normbwd([dup_g, dup_v], [full["w_up"][:, :D_FF], full["w_up"][:, D_FF:]], h1, g3, dh2, t_small, "up_bwd")
    dmx, dya, dyb, dga, dgb, dsa, dob, gg2 = mix_bwd(dh1, mx, z, ya, yb, g2, full["w_mix_out"], full["a_out"],
                                                     full["b_out"], t_small)
    duv, g_lng, g_lnb, g_ws, g_bst = sgu_bwd(z, dsa, ln_g, ln_b, w_s, bs_t, t_mid)
    dqk, dvb, dog, dpre, g_gkb, g_wn = gla_bwd(z, o, dob, states, wgk, gk_bias, wn, ltri, ltri.T, t_mid)
    dlr = mm_nt_small(dpre, wgk, t_mid, "dlr")
    segs = [duv, dqk, dvb, dog, dga, dgb, dlr]
    w_segs = [w_z[:, k * D:(k + 1) * D] for k in range(6)] + [w_z[:, 6 * D:]]
    dx, gg1 = nt_normbwd(segs, w_segs, xs, g1, dh1, t_small, "in_bwd")

    gz = [mm_tn(a, sg_, "dw_in_%d" % k) for k, sg_ in enumerate(segs)]
    gq = [gz[1][:, h * 256:h * 256 + B_HK] for h in range(B_H)]
    gk = [gz[1][:, h * 256 + B_HK:(h + 1) * 256] for h in range(B_H)]
    grads = {
        "w_in": jnp.concatenate([gz[0]] + gq + gk + [gz[2], gz[3], gz[6][:, :B_RANK], gz[4], gz[5]], axis=1),
        "a_out": mm_tn(sa, dya, "dw_a_out"),
        "b_gk": mm_tn(z, dpre, "dw_gk", col=48)[:B_RANK],
        "b_out": mm_tn(ob, dyb, "dw_b_out"),
        "w_mix_out": mm_tn(mp, dmx, "dw_mix"),
        "w_up": jnp.concatenate([mm_tn(c, dup_g, "dw_up_g"), mm_tn(c, dup_v, "dw_up_v")], axis=1),
        "conv_w": jnp.concatenate([gcw_g, gcw_v], axis=1),
        "w_down": mm_tn(ff, df, "dw_down"),
        "w_ple": mm_tn(ps, dpe, "dw_ple"),
        "w_ple_gate": mm_tn(h2, dpg, "dw_ple_gate"),
    }

    mine, sib = swap_halves(_pack_grads(grads))
    own, got = exchange_chips(add2(mine, sib, "chip_partial"))
    red = join_halves(sum4(own, got))

    outs = {}
    off = 0
    for name, shape, _ in SHARDED:
        nr = _grad_rows(shape)
        gsh =red[off:off + nr].reshape(-1)[:shape[0] * shape[1]].reshape(shape)
        off += nr
        res = adamw_rows(gsh, args[name].reshape(shape), args["m_" + name].reshape(shape),
                         args["v_" + name].reshape(shape), 0, "adamw_" + name)
        outs[name] = [r_.reshape(args[name].shape) for r_ in res]

    small_g = {
        "pre_mix_norm": gg1, "a_ln_g": g_lng, "a_ln_b": g_lnb, "a_spatial_w": g_ws, "a_spatial_b": g_bst.T,
        "b_gk_bias": g_gkb, "b_out_norm": g_wn, "post_mix_norm": gg2, "pre_ffn_norm": gg3,
        "conv_b": jnp.concatenate([gcb_g, gcb_v], axis=1), "post_ffn_norm": gg4, "post_ple_norm": gg5,
    }

    def pack_small(get):
        flat = jnp.concatenate([get(n).reshape(-1).astype(F32) for n, _ in SMALL])
        return jnp.pad(flat, (0, SMALL_ROWS * LANE - flat.shape[0])).reshape(SMALL_ROWS, LANE)

    res = allreduce_adamw_small(pack_small(lambda n: small_g[n]), pack_small(lambda n: args[n]),
                                pack_small(lambda n: args["m_" + n]), pack_small(lambda n: args["v_" + n]))
    off = 0
    for name, n in SMALL:
        outs[name] = [r_.reshape(-1)[off:off + n].reshape(args[name].shape) for r_ in res]
        off += n

    total = lax.psum(loss[0, 0], ("x", "y", "c"))
    return (total, dx.reshape(x.shape), *[outs[n][0] for n in order], *[outs[n][1] for n in order],
            *[outs[n][2] for n in order], *[outs[n][3] for n in order])
```

```python
import functools
import math

import jax
import jax.numpy as jnp
from jax import lax
from jax.experimental import pallas as pl
from jax.experimental.pallas import tpu as pltpu

F32 = jnp.float32
BF = jnp.bfloat16
SDS = jax.ShapeDtypeStruct
MESH = pl.DeviceIdType.MESH

EPS = 1e-6
D = 1024
A_W = 512
A_G, A_C = 8, 128
A_GD = A_W // A_G
B_H, B_HK, B_HV = 4, 128, 256
B_C = 64
B_RANK = 16
D_FF = 2816
PLE = 256
ZW = 6272
LANE = 128
VMEM_LIMIT = 60 * 1024 * 1024

ADAM_LR, ADAM_B1, ADAM_B2, ADAM_EPS, ADAM_WD, ADAM_STEP = 0.001, 0.9, 0.999, 1e-08, 0.01, 10

_GC = math.sqrt(2.0 / math.pi)
_GA = 0.044715

BIG = ("w_in", "a_out", "b_out", "w_mix_out", "w_up", "w_down", "w_ple", "w_ple_gate")
TINY = ("b_gk", "conv_w")
SMALL = (("pre_mix_norm", 1024), ("a_ln_g", 512), ("a_ln_b", 512), ("a_spatial_w", 131072),
         ("a_spatial_b", 1024), ("b_gk_bias", 512), ("b_out_norm", 256), ("post_mix_norm", 1024),
         ("pre_ffn_norm", 1024), ("conv_b", 5632), ("post_ffn_norm", 1024), ("post_ple_norm", 1024))


def _pc(body, **kw):
    return pl.pallas_call(body, **kw)


def _cp(n):
    return pltpu.CompilerParams(dimension_semantics=("arbitrary",) * n, vmem_limit_bytes=VMEM_LIMIT)


def _const(shape):
    nd = len(shape)
    return pl.BlockSpec(shape, lambda *_: (0,) * nd, pipeline_mode=pl.Buffered(1))


def _acc(shape):
    nd = len(shape)
    return pl.BlockSpec(shape, lambda *_: (0,) * nd)


def _dot(a, b):
    return jnp.dot(a, b, preferred_element_type=F32)


def _dot_nt(a, b):
    return lax.dot_general(a, b, (((1,), (1,)), ((), ())), preferred_element_type=F32)


def _dot_tn(a, b):
    return lax.dot_general(a, b, (((0,), (0,)), ((), ())), preferred_element_type=F32)


def _gelu(x):
    return 0.5 * x * (1.0 + jnp.tanh(_GC * (x + _GA * x * x * x)))


def _gelu_and_grad(x):
    x2 = x * x
    t = jnp.tanh(_GC * x * (1.0 + _GA * x2))
    return 0.5 * x * (1.0 + t), 0.5 * (1.0 + t) + 0.5 * x * (1.0 - t * t) * (_GC * (1.0 + 3.0 * _GA * x2))


def _log_sigmoid(x):
    return jnp.minimum(x, 0.0) - jnp.log(1.0 + jnp.exp(-jnp.abs(x)))


def _rms(x, g):
    return x * lax.rsqrt(jnp.mean(x * x, axis=-1, keepdims=True) + EPS) * g


def _rms_bwd(dy, x, g):
    r = lax.rsqrt(jnp.mean(x * x, axis=-1, keepdims=True) + EPS)
    n = x * r
    dn = dy * g
    dx = r * (dn - n * jnp.mean(dn * n, axis=-1, keepdims=True))
    return dx, jnp.sum(dy * n, axis=0, keepdims=True)


def _ldot3(l, x):
    h = x.astype(BF)
    r = x - h.astype(F32)
    m = r.astype(BF)
    lo = (r - m.astype(F32)).astype(BF)
    return _dot(l, h) + _dot(l, m) + _dot(l, lo)


def norm_matmul(x, g, w, bn, t, name, nblk=None, f32_blk=None):
    s, dm = x.shape
    if w.ndim == 3:
        nblk = w.shape[0]
        w_spec = pl.BlockSpec((None, dm, bn), lambda i, j: (j, 0, 0))
    else:
        nblk = nblk or w.shape[1] // bn
        w_spec = pl.BlockSpec((dm, bn), lambda i, j: (0, j))

    def body(x_ref, g_ref, w_ref, a_ref, z_ref, *rest):
        a_sc = rest[-1]

        @pl.when(pl.program_id(1) == 0)
        def _():
            a = _rms(x_ref[...], g_ref[...]).astype(BF)
            a_sc[...] = a
            a_ref[...] = a

        acc = _dot(a_sc[...], w_ref[...])
        z_ref[...] = acc.astype(BF)
        if f32_blk is not None:
            @pl.when(pl.program_id(1) == f32_blk)
            def _():
                rest[0][...] = acc

    extra = f32_blk is not None
    return _pc(
        body, name=name, grid=(s // t, nblk),
        in_specs=[pl.BlockSpec((t, dm), lambda i, j: (i, 0)), _const((1, dm)), w_spec],
        out_specs=[pl.BlockSpec((t, dm), lambda i, j: (i, 0)), pl.BlockSpec((t, bn), lambda i, j: (i, j))]
        + [pl.BlockSpec((t, bn), lambda i, j: (i, 0))] * extra,
        out_shape=[SDS((s, dm), BF), SDS((s, nblk * bn), BF)] + [SDS((s, bn), F32)] * extra,
        scratch_shapes=[pltpu.VMEM((t, dm), BF)], compiler_params=_cp(2))(x, g, w)


def _sgu_masked(ws_ref):
    r = lax.broadcasted_iota(jnp.int32, (A_C, A_C), 0)
    c = lax.broadcasted_iota(jnp.int32, (A_C, A_C), 1)
    return [jnp.where(c <= r, ws_ref[g], 0.0).astype(BF) for g in range(A_G)]


def _sgu_recompute(v, lng, lnb):
    gv, dgv = _gelu_and_grad(v)
    mu = jnp.mean(gv, axis=-1, keepdims=True)
    xc = gv - mu
    rstd = lax.rsqrt(jnp.mean(xc * xc, axis=-1, keepdims=True) + EPS)
    xhat = xc * rstd
    return dgv, rstd, xhat, (xhat * lng + lnb).astype(BF)


def sgu_fwd(z, ln_g, ln_b, w_s, bs_t, t):
    s = z.shape[0]

    def body(u_ref, v_ref, g_ref, b_ref, ws_ref, bs_ref, sa_ref, s_sc):
        wm = _sgu_masked(ws_ref)
        for ci in range(t // A_C):
            rows = pl.ds(ci * A_C, A_C)
            _, _, _, vn = _sgu_recompute(v_ref[rows, :].astype(F32), g_ref[...], b_ref[...])
            for g in range(A_G):
                cols = slice(g * A_GD, (g + 1) * A_GD)
                s_sc[:, cols] = _dot(wm[g], vn[:, cols]) + bs_ref[:, g:g + 1]
            sa_ref[rows, :] = (_gelu(u_ref[rows, :].astype(F32)) * s_sc[...]).astype(BF)

    return _pc(
        body, name="sgu_fwd", grid=(s // t,),
        in_specs=[pl.BlockSpec((t, A_W), lambda i: (i, 0)), pl.BlockSpec((t, A_W), lambda i: (i, 1)),
                  _const((1, A_W)), _const((1, A_W)), _const((A_G, A_C, A_C)), _const((A_C, A_G))],
        out_specs=pl.BlockSpec((t, A_W), lambda i: (i, 0)),
        out_shape=SDS((s, A_W), BF),
        scratch_shapes=[pltpu.VMEM((A_C, A_W), F32)], compiler_params=_cp(1))(z, z, ln_g, ln_b, w_s, bs_t)


def _gla_decays(qk_ref, lr_ref, wgk_ref, bias_ref, l_ref, t):
    nc = t // B_C
    q = qk_ref[:, :B_HK].astype(F32) * (B_HK ** -0.5)
    k = qk_ref[:, B_HK:].astype(F32)
    pre = _dot(lr_ref[...], wgk_ref[...]) + bias_ref[...]
    la = _log_sigmoid(pre) * (1.0 / 16.0)
    b = _ldot3(l_ref[...], la)
    b3 = b.reshape(nc, B_C, B_HK)
    bl = jnp.broadcast_to(b3[:, B_C - 1:B_C, :], (nc, B_C, B_HK)).reshape(t, B_HK)
    eb, enb, etb = jnp.exp(b), jnp.exp(-b), jnp.exp(bl - b)
    return pre, b, bl, eb, enb, etb, q * eb, k * enb, k * etb


def gla_fwd(z, qk32, zl, wgk, bias, wn, ltri, t):
    s = z.shape[0]
    nc = t // B_C

    def body(qk_ref, v_ref, og_ref, lr_ref, wgk_ref, bias_ref, wn_ref, l_ref, ob_ref, o_ref, st_ref, st_sc, o_sc):
        h = pl.program_id(1)

        @pl.when(pl.program_id(0) == 0)
        def _():
            st_sc[h] = jnp.zeros((B_HV, B_HK), F32)

        _, _, bl, _, _, _, qd, ki, kt = _gla_decays(qk_ref, lr_ref, wgk_ref, bias_ref, l_ref, t)
        qd, ki, kt = qd.astype(BF), ki.astype(BF), kt.astype(BF)
        vb = v_ref[...]
        sc = jnp.where(l_ref[...] > 0, _dot_nt(qd, ki), 0.0).astype(BF)
        o_sc[...] = _dot(sc, vb)
        for n in range(nc):
            rows = slice(n * B_C, (n + 1) * B_C)
            st = st_sc[h]
            stb = st.astype(BF)
            st_ref[n, 0] = stb
            o_sc[rows, :] += _dot_nt(qd[rows], stb)
            st_sc[h] = st * jnp.exp(bl[n * B_C:n * B_C + 1, :]) + _dot_tn(vb[rows], kt[rows])
        ob = o_sc[...].astype(BF)
        o_ref[...] = ob
        og = og_ref[...].astype(F32)
        ob_ref[...] = (_rms(ob.astype(F32), wn_ref[...]) * og * jax.nn.sigmoid(og)).astype(BF)

    return _pc(
        body, name="gla_fwd", grid=(s // t, B_H),
        in_specs=[pl.BlockSpec((t, 256), lambda i, h: (i, h)), pl.BlockSpec((t, 256), lambda i, h: (i, 8 + h)),
                  pl.BlockSpec((t, 256), lambda i, h: (i, 12 + h)), pl.BlockSpec((t, LANE), lambda i, h: (i, 0)),
                  pl.BlockSpec((LANE, B_HK), lambda i, h: (0, h)), pl.BlockSpec((1, B_HK), lambda i, h: (0, h)),
                  _const((1, B_HV)), _const((t, t))],
        out_specs=[pl.BlockSpec((t, B_HV), lambda i, h: (i, h)), pl.BlockSpec((t, B_HV), lambda i, h: (i, h)),
                   pl.BlockSpec((nc, 1, B_HV, B_HK), lambda i, h: (i, h, 0, 0))],
        out_shape=[SDS((s, D), BF), SDS((s, D), BF), SDS((s // B_C, B_H, B_HV, B_HK), BF)],
        scratch_shapes=[pltpu.VMEM((B_H, B_HV, B_HK), F32), pltpu.VMEM((t, B_HV), F32)],
        compiler_params=_cp(2))(qk32, z, z, zl, wgk, bias, wn, ltri)


def mix_fwd(sa, ob, z, x, a_out, b_out, w_mix, g2, t):
    s = x.shape[0]

    def body(sa_ref, ob_ref, ga_ref, gb_ref, x_ref, ao_ref, bo_ref, wm_ref, g2_ref,
             ya_ref, yb_ref, mp_ref, mx_ref, h1_ref):
        ya = _dot(sa_ref[...], ao_ref[...]).astype(BF)
        yb = _dot(ob_ref[...], bo_ref[...]).astype(BF)
        ya_ref[...] = ya
        yb_ref[...] = yb
        mp = (jax.nn.sigmoid(ga_ref[...].astype(F32)) * ya.astype(F32)
              + jax.nn.sigmoid(gb_ref[...].astype(F32)) * yb.astype(F32)).astype(BF)
        mp_ref[...] = mp
        mx = _dot(mp, wm_ref[...]).astype(BF)
        mx_ref[...] = mx
        h1_ref[...] = x_ref[...] + _rms(mx.astype(F32), g2_ref[...])

    row = lambda w: pl.BlockSpec((t, w), lambda i: (i, 0))
    return _pc(
        body, name="mix_fwd", grid=(s // t,),
        in_specs=[row(A_W), row(D), pl.BlockSpec((t, D), lambda i: (i, 4)), pl.BlockSpec((t, D), lambda i: (i, 5)),
                  row(D), _const((A_W, D)), _const((D, D)), _const((D, D)), _const((1, D))],
        out_specs=[row(D)] * 5,
        out_shape=[SDS((s, D), BF)] * 4 + [SDS((s, D), F32)],
        compiler_params=_cp(1))(sa, ob, z, z, x, a_out, b_out, w_mix, g2)


def _conv(u, w, b):
    return b + w[0:1] * pltpu.roll(u, 2, 0) + w[1:2] * pltpu.roll(u, 1, 0) + w[2:3] * u


def ffn_gate_fwd(up, conv_w, conv_b, t):
    s = up.shape[0]
    bn = 1408
    hb = t // 8

    def body(ug_ref, uv_ref, hg_ref, hv_ref, wg_ref, wv_ref, bg_ref, bv_ref, ff_ref):
        live = (pl.program_id(1) > 0).astype(F32)

        def branch(u_ref, h_ref, w_ref, b_ref):
            ext = jnp.concatenate([h_ref[...].astype(F32) * live, u_ref[...].astype(F32)], axis=0)
            return _conv(ext, w_ref[...], b_ref[...])[8:]

        ff_ref[...] = (_gelu(branch(ug_ref, hg_ref, wg_ref, bg_ref)) * branch(uv_ref, hv_ref, wv_ref, bv_ref)).astype(BF)

    halo = lambda off: pl.BlockSpec((8, bn), lambda j, i: (jnp.maximum(i * hb - 1, 0), j + off))
    return _pc(
        body, name="ffn_gate_fwd", grid=(2, s // t),
        in_specs=[pl.BlockSpec((t, bn), lambda j, i: (i, j)), pl.BlockSpec((t, bn), lambda j, i: (i, j + 2)),
                  halo(0), halo(2),
                  pl.BlockSpec((3, bn), lambda j, i: (0, j)), pl.BlockSpec((3, bn), lambda j, i: (0, j + 2)),
                  pl.BlockSpec((1, bn), lambda j, i: (0, j)), pl.BlockSpec((1, bn), lambda j, i: (0, j + 2))],
        out_specs=pl.BlockSpec((t, bn), lambda j, i: (i, j)),
        out_shape=SDS((s, D_FF), BF), compiler_params=_cp(2))(up, up, up, up, conv_w, conv_w, conv_b, conv_b)


def out_fwd(ff, h1, p, tgt, w_down, w_pg, w_ple, g4, g5, t):
    s = h1.shape[0]

    def body(ff_ref, h1_ref, p_ref, t_ref, wd_ref, wpg_ref, wpl_ref, g4_ref, g5_ref,
             f_ref, h2_ref, pg_ref, pe_ref, dy_ref, loss_ref):
        @pl.when(pl.program_id(0) == 0)
        def _():
            loss_ref[...] = jnp.zeros((1, 1), F32)

        f = _dot(ff_ref[...], wd_ref[...]).astype(BF)
        f_ref[...] = f
        h2 = h1_ref[...] + _rms(f.astype(F32), g4_ref[...])
        h2b = h2.astype(BF)
        h2_ref[...] = h2b
        pg = _dot(h2b, wpg_ref[...]).astype(BF)
        pe = _dot(p_ref[...].astype(BF), wpl_ref[...]).astype(BF)
        pg_ref[...] = pg
        pe_ref[...] = pe
        y = h2 + _rms(jax.nn.sigmoid(pg.astype(F32)) * pe.astype(F32), g5_ref[...])
        err = y - t_ref[...]
        dy_ref[...] = err * (1.0 / D)
        loss_ref[...] += (0.5 / D) * jnp.sum(err * err)

    row = lambda w: pl.BlockSpec((t, w), lambda i: (i, 0))
    return _pc(
        body, name="out_fwd", grid=(s // t,),
        in_specs=[row(D_FF), row(D), row(PLE), row(D), _const((D_FF, D)), _const((D, D)), _const((PLE, D)),
                  _const((1, D)), _const((1, D))],
        out_specs=[row(D)] * 5 + [_acc((1, 1))],
        out_shape=[SDS((s, D), BF)] * 4 + [SDS((s, D), F32), SDS((1, 1), F32)],
        compiler_params=_cp(1))(ff, h1, p, tgt, w_down, w_pg, w_ple, g4, g5)


def out_bwd(dy, pg, pe, f, g5, g4, w_pg, w_down, t):
    s = dy.shape[0]

    def body(dy_ref, pg_ref, pe_ref, f_ref, g5_ref, g4_ref, wpg_ref, wd_ref,
             dh2_ref, dpe_ref, dpg_ref, df_ref, dff_ref, gg5_ref, gg4_ref):
        @pl.when(pl.program_id(0) == 0)
        def _():
            gg5_ref[...] = jnp.zeros((1, D), F32)
            gg4_ref[...] = jnp.zeros((1, D), F32)

        dy_ = dy_ref[...]
        pg_ = pg_ref[...].astype(F32)
        pe_ = pe_ref[...].astype(F32)
        sg = jax.nn.sigmoid(pg_)
        dple, dg5 = _rms_bwd(dy_, sg * pe_, g5_ref[...])
        gg5_ref[...] += dg5
        dpe_ref[...] = (dple * sg).astype(BF)
        dpg = (dple * pe_ * sg * (1.0 - sg)).astype(BF)
        dpg_ref[...] = dpg
        dh2 = dy_ + _dot_nt(dpg, wpg_ref[...])
        dh2_ref[...] = dh2
        df, dg4 = _rms_bwd(dh2, f_ref[...].astype(F32), g4_ref[...])
        gg4_ref[...] += dg4
        dfb = df.astype(BF)
        df_ref[...] = dfb
        dff_ref[...] = _dot_nt(dfb, wd_ref[...]).astype(BF)

    row = lambda w: pl.BlockSpec((t, w), lambda i: (i, 0))
    return _pc(
        body, name="out_bwd", grid=(s // t,),
        in_specs=[row(D), row(D), row(D), row(D), _const((1, D)), _const((1, D)), _const((D, D)), _const((D_FF, D))],
        out_specs=[row(D), row(D), row(D), row(D), row(D_FF), _acc((1, D)), _acc((1, D))],
        out_shape=[SDS((s, D), F32), SDS((s, D), BF), SDS((s, D), BF), SDS((s, D), BF), SDS((s, D_FF), BF),
                   SDS((1, D), F32), SDS((1, D), F32)],
        compiler_params=_cp(1))(dy, pg, pe, f, g5, g4, w_pg, w_down)


def ffn_gate_bwd(up, dff, conv_w, conv_b, t):
    s = up.shape[0]
    bn = 1408
    hb = t // 8
    nt = s // t
    r = t + 16

    def body(ug_ref, uv_ref, hbg_ref, hbv_ref, hag_ref, hav_ref, d_ref, da_ref, wg_ref, wv_ref, bg_ref, bv_ref,
             dug_ref, duv_ref, gwg_ref, gwv_ref, gbg_ref, gbv_ref):
        i = pl.program_id(1)

        @pl.when(i == 0)
        def _():
            gwg_ref[...] = jnp.zeros((3, bn), F32)
            gwv_ref[...] = jnp.zeros((3, bn), F32)
            gbg_ref[...] = jnp.zeros((1, bn), F32)
            gbv_ref[...] = jnp.zeros((1, bn), F32)

        first = (i > 0).astype(F32)
        last = (i < nt - 1).astype(F32)
        ext = lambda hb_ref, u_ref, ha_ref: jnp.concatenate(
            [hb_ref[...].astype(F32) * first, u_ref[...].astype(F32), ha_ref[...].astype(F32)], axis=0)
        ug, uv = ext(hbg_ref, ug_ref, hag_ref), ext(hbv_ref, uv_ref, hav_ref)
        wg, wv = wg_ref[...], wv_ref[...]
        cg, cv = _conv(ug, wg, bg_ref[...]), _conv(uv, wv, bv_ref[...])
        dff_ = jnp.concatenate([jnp.zeros((8, bn), F32), d_ref[...].astype(F32), da_ref[...].astype(F32) * last], axis=0)
        gl, dgl = _gelu_and_grad(cg)
        dg = dff_ * cv * dgl
        dv = dff_ * gl

        def back(dc, u, w, du_ref, gw_ref, gb_ref):
            du = w[2:3] * dc + w[1:2] * pltpu.roll(dc, r - 1, 0) + w[0:1] * pltpu.roll(dc, r - 2, 0)
            du_ref[...] = du[8:t + 8].astype(BF)
            dct = dc[8:t + 8]
            gw_ref[0:1, :] += jnp.sum(dct * pltpu.roll(u, 2, 0)[8:t + 8], axis=0, keepdims=True)
            gw_ref[1:2, :] += jnp.sum(dct * pltpu.roll(u, 1, 0)[8:t + 8], axis=0, keepdims=True)
            gw_ref[2:3, :] += jnp.sum(dct * u[8:t + 8], axis=0, keepdims=True)
            gb_ref[...] += jnp.sum(dct, axis=0, keepdims=True)

        back(dg, ug, wg, dug_ref, gwg_ref, gbg_ref)
        back(dv, uv, wv, duv_ref, gwv_ref, gbv_ref)

    tile = lambda off: pl.BlockSpec((t, bn), lambda j, i: (i, j + off))
    before = lambda off: pl.BlockSpec((8, bn), lambda j, i: (jnp.maximum(i * hb - 1, 0), j + off))
    after = lambda off: pl.BlockSpec((8, bn), lambda j, i: (jnp.minimum((i + 1) * hb, nt * hb - 1), j + off))
    cw = lambda off: pl.BlockSpec((3, bn), lambda j, i: (0, j + off))
    cb = lambda off: pl.BlockSpec((1, bn), lambda j, i: (0, j + off))
    return _pc(
        body, name="ffn_gate_bwd", grid=(2, nt),
        in_specs=[tile(0), tile(2), before(0), before(2), after(0), after(2), tile(0), after(0),
                  cw(0), cw(2), cb(0), cb(2)],
        out_specs=[tile(0), tile(0), cw(0), cw(0), cb(0), cb(0)],
        out_shape=[SDS((s, D_FF), BF), SDS((s, D_FF), BF), SDS((3, D_FF), F32), SDS((3, D_FF), F32),
                   SDS((1, D_FF), F32), SDS((1, D_FF), F32)],
        compiler_params=_cp(2))(up, up, up, up, up, up, dff, dff, conv_w, conv_w, conv_b, conv_b)


def nt_normbwd(pairs, xin, gain, dres, t, name):
    s = xin.shape[0]
    np_ = len(pairs)
    dys = [p_[0] for p_ in pairs]
    ws = [p_[2] for p_ in pairs]

    def body(*refs):
        dy_refs, w_refs = refs[:np_], refs[np_:2 * np_]
        x_ref, g_ref, dres_ref, dx_ref, gg_ref = refs[2 * np_:]

        @pl.when(pl.program_id(0) == 0)
        def _():
            gg_ref[...] = jnp.zeros((1, D), F32)

        acc = _dot_nt(dy_refs[0][...], w_refs[0][...])
        for k in range(1, np_):
            acc += _dot_nt(dy_refs[k][...], w_refs[k][...])
        dxn, dg = _rms_bwd(acc, x_ref[...], g_ref[...])
        dx_ref[...] = dres_ref[...] + dxn
        gg_ref[...] += dg

    row = lambda w: pl.BlockSpec((t, w), lambda i: (i, 0))
    dy_spec = lambda blk, width: pl.BlockSpec((t, width), lambda i: (i, blk))
    return _pc(
        body, name=name, grid=(s // t,),
        in_specs=[dy_spec(p_[1], p_[3].block_shape[-1]) for p_ in pairs] + [p_[3] for p_ in pairs]
        + [row(D), _const((1, D)), row(D)],
        out_specs=[row(D), _acc((1, D))],
        out_shape=[SDS((s, D), F32), SDS((1, D), F32)],
        compiler_params=_cp(1))(*dys, *ws, xin, gain, dres)


def mix_bwd(dh1, mx, z, ya, yb, g2, w_mix, a_out, b_out, t):
    s = dh1.shape[0]

    def body(dh_ref, mx_ref, ga_ref, gb_ref, ya_ref, yb_ref, g2_ref, wm_ref, ao_ref, bo_ref,
             dmx_ref, dya_ref, dyb_ref, dga_ref, dgb_ref, dsa_ref, dob_ref, gg2_ref):
        @pl.when(pl.program_id(0) == 0)
        def _():
            gg2_ref[...] = jnp.zeros((1, D), F32)

        dmx, dg2 = _rms_bwd(dh_ref[...], mx_ref[...].astype(F32), g2_ref[...])
        gg2_ref[...] += dg2
        dmxb = dmx.astype(BF)
        dmx_ref[...] = dmxb
        dmp = _dot_nt(dmxb, wm_ref[...])

        def gate(g_ref, y_ref, dy_ref, dg_ref, w_ref, dz_ref):
            sg = jax.nn.sigmoid(g_ref[...].astype(F32))
            dyb_ = (dmp * sg).astype(BF)
            dy_ref[...] = dyb_
            dg_ref[...] = (dmp * y_ref[...].astype(F32) * sg * (1.0 - sg)).astype(BF)
            dz_ref[...] = _dot_nt(dyb_, w_ref[...]).astype(BF)

        gate(ga_ref, ya_ref, dya_ref, dga_ref, ao_ref, dsa_ref)
        gate(gb_ref, yb_ref, dyb_ref, dgb_ref, bo_ref, dob_ref)

    row = lambda w: pl.BlockSpec((t, w), lambda i: (i, 0))
    return _pc(
        body, name="mix_bwd", grid=(s // t,),
        in_specs=[row(D), row(D), pl.BlockSpec((t, D), lambda i: (i, 4)), pl.BlockSpec((t, D), lambda i: (i, 5)),
                  row(D), row(D), _const((1, D)), _const((D, D)), _const((A_W, D)), _const((D, D))],
        out_specs=[row(D)] * 5 + [row(A_W), row(D), _acc((1, D))],
        out_shape=[SDS((s, D), BF)] * 5 + [SDS((s, A_W), BF), SDS((s, D), BF), SDS((1, D), F32)],
        compiler_params=_cp(1))(dh1, mx, z, z, ya, yb, g2, w_mix, a_out, b_out)


def sgu_bwd(z, dsa, ln_g, ln_b, w_s, bs_t, t):
    s = z.shape[0]
    nt = s // t

    def body(u_ref, v_ref, dsa_ref, g_ref, b_ref, ws_ref, bs_ref,
             duv_ref, glg_ref, glb_ref, gws_ref, gbs_ref, s_sc, dvn_sc, ds_acc):
        i = pl.program_id(0)

        @pl.when(i == 0)
        def _():
            glg_ref[...] = jnp.zeros((1, A_W), F32)
            glb_ref[...] = jnp.zeros((1, A_W), F32)
            gws_ref[...] = jnp.zeros((A_G, A_C, A_C), F32)
            ds_acc[...] = jnp.zeros((A_C, A_W), F32)

        wm = _sgu_masked(ws_ref)
        rr =lax.broadcasted_iota(jnp.int32, (A_C, A_C), 0)
        cc = lax.broadcasted_iota(jnp.int32, (A_C, A_C), 1)
        tril = cc <= rr
        lng = g_ref[...]
        for ci in range(t // A_C):
            rows = pl.ds(ci * A_C, A_C)
            dgv, rstd, xhat, vn = _sgu_recompute(v_ref[rows, :].astype(F32), lng, b_ref[...])
            for g in range(A_G):
                cols = slice(g * A_GD, (g + 1) * A_GD)
                s_sc[:, cols] = _dot(wm[g], vn[:, cols]) + bs_ref[:, g:g + 1]
            gu, dgu = _gelu_and_grad(u_ref[rows, :].astype(F32))
            dsa_ = dsa_ref[rows, :].astype(F32)
            ds = dsa_ * gu
            ds_acc[...] += ds
            dsb = ds.astype(BF)
            for g in range(A_G):
                cols = slice(g * A_GD, (g + 1) * A_GD)
                gws_ref[g] += jnp.where(tril, _dot_nt(dsb[:, cols], vn[:, cols]), 0.0)
                dvn_sc[:, cols] = _dot_tn(wm[g], dsb[:, cols])
            dvn = dvn_sc[...]
            glb_ref[...] += jnp.sum(dvn, axis=0, keepdims=True)
            glg_ref[...] += jnp.sum(dvn * xhat, axis=0, keepdims=True)
            dxh = dvn * lng
            dgv_ = rstd * (dxh - jnp.mean(dxh, axis=-1, keepdims=True)
                           - xhat * jnp.mean(dxh * xhat, axis=-1, keepdims=True))
            duv_ref[rows, :A_W] = (dsa_ * s_sc[...] * dgu).astype(BF)
            duv_ref[rows, A_W:] = (dgv_ * dgv).astype(BF)

        @pl.when(i == nt - 1)
        def _():
            acc = ds_acc[...]
            for g in range(A_G):
                gbs_ref[:, g:g + 1] = jnp.sum(acc[:, g * A_GD:(g + 1) * A_GD], axis=1, keepdims=True)

    return _pc(
        body, name="sgu_bwd", grid=(nt,),
        in_specs=[pl.BlockSpec((t, A_W), lambda i: (i, 0)), pl.BlockSpec((t, A_W), lambda i: (i, 1)),
                  pl.BlockSpec((t, A_W), lambda i: (i, 0)),
                  _const((1, A_W)), _const((1, A_W)), _const((A_G, A_C, A_C)), _const((A_C, A_G))],
        out_specs=[pl.BlockSpec((t, D), lambda i: (i, 0)), _acc((1, A_W)), _acc((1, A_W)),
                   _acc((A_G, A_C, A_C)), _acc((A_C, A_G))],
        out_shape=[SDS((s, D), BF), SDS((1, A_W), F32), SDS((1, A_W), F32), SDS((A_G, A_C, A_C), F32),
                   SDS((A_C, A_G), F32)],
        scratch_shapes=[pltpu.VMEM((A_C, A_W), F32), pltpu.VMEM((A_C, A_W), F32), pltpu.VMEM((A_C, A_W), F32)],
        compiler_params=_cp(1))(z, z, dsa, ln_g, ln_b, w_s, bs_t)


def gla_bwd(z, qk32, zl, o, dob, states, wgk, bias, wn, ltri, ltri_t, t):
    s = z.shape[0]
    nt = s // t
    nc = t // B_C

    def body(qk_ref, v_ref, og_ref, lr_ref, o_ref, dob_ref, st_ref, wgk_ref, bias_ref, wn_ref, l_ref, lt_ref,
             dqk_ref, dv_ref, dog_ref, dpre_ref, gbias_ref, gwn_ref, dst_sc, dv_sc, dqd_sc, dkt_sc, ddec_sc):
        i = pl.program_id(0)
        h = pl.program_id(1)

        @pl.when((i == 0) & (h == 0))
        def _():
            gbias_ref[...] = jnp.zeros((B_H, 1, B_HK), F32)
            gwn_ref[...] = jnp.zeros((1, B_HV), F32)

        @pl.when(i == 0)
        def _():
            dst_sc[h] = jnp.zeros((B_HV, B_HK), F32)

        pre, b, bl, eb, enb, etb, qd, ki, kt = _gla_decays(qk_ref, lr_ref, wgk_ref, bias_ref, l_ref, t)
        qdb, kib, ktb = qd.astype(BF), ki.astype(BF), kt.astype(BF)
        vb = v_ref[...]
        o_ = o_ref[...].astype(F32)
        og = og_ref[...].astype(F32)
        sog = jax.nn.sigmoid(og)
        dob_ = dob_ref[...].astype(F32)
        wn_ = wn_ref[...]
        don = dob_ * og * sog
        do, dwn = _rms_bwd(don, o_, wn_)
        gwn_ref[...] += dwn
        dog_ref[...] = (dob_ * _rms(o_, wn_) * sog * (1.0 + og * (1.0 - sog))).astype(BF)
        dob16 = do.astype(BF)
        keep, keep_t = l_ref[...] > 0, lt_ref[...] > 0
        sc_t = jnp.where(keep_t, _dot_nt(kib, qdb), 0.0).astype(BF)
        dsc = jnp.where(keep, _dot_nt(dob16, vb), 0.0).astype(BF)
        dsc_t = jnp.where(keep_t, _dot_nt(vb, dob16), 0.0).astype(BF)
        dv_sc[...] = _dot(sc_t, dob16)
        dqd_sc[...] = _dot(dsc, kib)
        dki = _dot(dsc_t, qdb)
        for n in reversed(range(nc)):
            rows = slice(n * B_C, (n + 1) * B_C)
            dst = dst_sc[h]
            dstb = dst.astype(BF)
            stp = st_ref[n, 0]
            dv_sc[rows, :] += _dot_nt(ktb[rows], dstb)
            dkt_sc[rows, :] = _dot(vb[rows], dstb)
            dqd_sc[rows, :] += _dot(dob16[rows], stp)
            dec = jnp.exp(bl[n * B_C:n * B_C + 1, :])
            ddec_sc[n] = jnp.sum(dst * stp.astype(F32), axis=0, keepdims=True) * dec
            dst_sc[h] = dst * dec + _dot_tn(dob16[rows], qdb[rows])
        dqd, dkt = dqd_sc[...], dkt_sc[...]
        dv_ref[...] = dv_sc[...].astype(BF)
        dqk_ref[:, :B_HK] = (dqd * eb * (B_HK ** -0.5)).astype(BF)
        dqk_ref[:, B_HK:] = (dki * enb + dkt * etb).astype(BF)
        dktkt = dkt * kt
        db3 = (dqd * qd - dki * ki - dktkt).reshape(nc, B_C, B_HK)
        dbl = jnp.sum(dktkt.reshape(nc, B_C, B_HK), axis=1, keepdims=True) + ddec_sc[...]
        last = lax.broadcasted_iota(jnp.int32, (nc, B_C, B_HK), 1) == B_C - 1
        db = (db3 + jnp.where(last, dbl, 0.0)).reshape(t, B_HK)
        dla = _ldot3(lt_ref[...], db)
        dpre = dla * (1.0 / 16.0) * (1.0 - jax.nn.sigmoid(pre))
        dpre_ref[...] = dpre.astype(BF)
        gbias_ref[h] += jnp.sum(dpre, axis=0, keepdims=True)

    rv = lambda i: nt - 1 - i
    return _pc(
        body, name="gla_bwd", grid=(nt, B_H),
        in_specs=[pl.BlockSpec((t, 256), lambda i, h: (rv(i), h)), pl.BlockSpec((t, 256), lambda i, h: (rv(i), 8 + h)),
                  pl.BlockSpec((t, 256), lambda i, h: (rv(i), 12 + h)), pl.BlockSpec((t, LANE), lambda i, h: (rv(i), 0)),
                  pl.BlockSpec((t, B_HV), lambda i, h: (rv(i), h)), pl.BlockSpec((t, B_HV), lambda i, h: (rv(i), h)),
                  pl.BlockSpec((nc, 1, B_HV, B_HK), lambda i, h: (rv(i), h, 0, 0)),
                  pl.BlockSpec((LANE, B_HK), lambda i, h: (0, h)), pl.BlockSpec((1, B_HK), lambda i, h: (0, h)),
                  _const((1, B_HV)), _const((t, t)), _const((t, t))],
        out_specs=[pl.BlockSpec((t, 256), lambda i, h: (rv(i), h)), pl.BlockSpec((t, B_HV), lambda i, h: (rv(i), h)),
                   pl.BlockSpec((t, B_HV), lambda i, h: (rv(i), h)), pl.BlockSpec((t, B_HK), lambda i, h: (rv(i), h)),
                   _acc((B_H, 1, B_HK)), _acc((1, B_HV))],
        out_shape=[SDS((s, D), BF), SDS((s, D), BF), SDS((s, D), BF), SDS((s, B_H * B_HK), BF),
                   SDS((B_H, 1, B_HK), F32), SDS((1, B_HV), F32)],
        scratch_shapes=[pltpu.VMEM((B_H, B_HV, B_HK), F32), pltpu.VMEM((t, B_HV), F32), pltpu.VMEM((t, B_HK), F32),
                        pltpu.VMEM((t, B_HK), F32), pltpu.VMEM((nc, 1, B_HK), F32)],
        compiler_params=_cp(2))(qk32, z, z, zl, o, dob, states, wgk, bias, wn, ltri, ltri_t)


def mm_tn(a, b, name, tk=512):
    s, m = a.shape
    n = b.shape[1]
    bn = next(c for c in (1024, 1408, 512, 256, 128) if n % c == 0 and m * c * 4 <= 6 * 1024 * 1024)
    nk = s // tk

    def body(a_ref, b_ref, o_ref, acc):
        k = pl.program_id(1)

        @pl.when(k == 0)
        def _():
            acc[...] = jnp.zeros((m, bn), F32)

        acc[...] += _dot_tn(a_ref[...].astype(BF), b_ref[...])

        @pl.when(k == nk - 1)
        def _():
            o_ref[...] = acc[...].astype(BF)

    return _pc(
        body, name=name, grid=(n // bn, nk),
        in_specs=[pl.BlockSpec((tk, m), lambda j, k: (k, 0)), pl.BlockSpec((tk, bn), lambda j, k: (k, j))],
        out_specs=pl.BlockSpec((m, bn), lambda j, k: (0, j)),
        out_shape=SDS((m, n), BF), scratch_shapes=[pltpu.VMEM((m, bn), F32)], compiler_params=_cp(2))(a, b)


def mm_cols(a, w, blk, width, t, name):
    s, k = a.shape

    def body(a_ref, w_ref, o_ref):
        o_ref[...] = _dot(a_ref[...], w_ref[...]).astype(BF)

    return _pc(body, name=name, grid=(s // t,),
               in_specs=[pl.BlockSpec((t, k), lambda i: (i, 0)),
                         pl.BlockSpec((k, width), lambda i: (0, blk), pipeline_mode=pl.Buffered(1))],
               out_specs=pl.BlockSpec((t, width), lambda i: (i, 0)), out_shape=SDS((s, width), BF),
               compiler_params=_cp(1))(a, w)


def mm_nt_small(a, w, t, name):
    s, k = a.shape
    n = w.shape[0]

    def body(a_ref, w_ref, o_ref):
        o_ref[...] = _dot_nt(a_ref[...], w_ref[...]).astype(BF)

    return _pc(body, name=name, grid=(s // t,),
               in_specs=[pl.BlockSpec((t, k), lambda i: (i, 0)), _const((n, k))],
               out_specs=pl.BlockSpec((t, n), lambda i: (i, 0)), out_shape=SDS((s, n), BF),
               compiler_params=_cp(1))(a, w)


def _adamw(w, g, m, v):
    m = ADAM_B1 * m + (1.0 - ADAM_B1) * g
    v = ADAM_B2 * v + (1.0 - ADAM_B2) * (g * g)
    m_hat = m / (1.0 - ADAM_B1 ** ADAM_STEP)
    v_hat = v / (1.0 - ADAM_B2 ** ADAM_STEP)
    return -ADAM_LR * (m_hat / (jnp.sqrt(v_hat) + ADAM_EPS) + ADAM_WD * w), m, v


def _half_rows(rows):
    rh = rows // 2
    return rh, max(b for b in range(16, 257, 16) if rh % b == 0)


def _pc_sp(body, grid, in_specs, out_specs, out_shape, name):
    gs = pltpu.PrefetchScalarGridSpec(num_scalar_prefetch=1, grid=grid, in_specs=in_specs, out_specs=out_specs)
    return _pc(body, grid_spec=gs, out_shape=out_shape, name=name, compiler_params=_cp(len(grid)))


def adamw_halves(sc, own, sib, w, m, v, name):
    rows, cols = w.shape
    rh, br = _half_rows(rows)
    nbk = rh // br

    def body(sc_ref, own_ref, sib_ref, w_ref, m_ref, v_ref, go_ref, d_ref, mo_ref, vo_ref):
        g_ = jnp.where(pl.program_id(0) // nbk == sc_ref[0], own_ref[...], sib_ref[...])
        go_ref[...] = g_
        d_ref[...], mo_ref[...], vo_ref[...] = _adamw(w_ref[...], g_, m_ref[...], v_ref[...])

    half = pl.BlockSpec((br, cols), lambda i, sc_: (i % nbk, 0))
    blk = pl.BlockSpec((br, cols), lambda i, sc_: (i, 0))
    return _pc_sp(body, (2 * nbk,), [half, half, blk, blk, blk], [blk] * 4, [SDS((rows, cols), F32)] * 4,
                  name)(sc, own, sib, w, m, v)


def adamw_small(g, w, m, v):
    def body(g_ref, w_ref, m_ref, v_ref, d_ref, mo_ref, vo_ref):
        d_ref[...], mo_ref[...], vo_ref[...] = _adamw(w_ref[...], g_ref[...], m_ref[...], v_ref[...])

    vm = pl.BlockSpec(memory_space=pltpu.VMEM)
    return _pc(body, name="adamw_small", in_specs=[vm] * 4, out_specs=[vm] * 3, out_shape=[SDS(g.shape, F32)] * 3,
               compiler_params=pltpu.CompilerParams(vmem_limit_bytes=VMEM_LIMIT))(g, w, m, v)


def _pos():
    return lax.axis_index("x"), lax.axis_index("y"), lax.axis_index("c")


def _other_chips(x, y):
    return [(1 - x, y), (x, 1 - y), (1 - x, 1 - y)]


_ANY = pl.BlockSpec(memory_space=pltpu.HBM)


def gather_weights(bigs, tinies):
    nb, nt_ = len(bigs), len(tinies)
    n_all = nb + nt_

    def body(*refs):
        ins, outs = refs[:n_all], refs[n_all:2 * n_all]
        ssem, rsem = refs[2 * n_all:]
        x, y, c = _pos()
        me = 2 * x + y
        chips = _other_chips(x, y)
        sibling = (x, y, 1 - c)
        halves = []
        for w in range(nb):
            rh = bigs[w].shape[0] // 2
            halves.append((pl.ds(pl.multiple_of(c * rh, 16), rh), pl.ds(pl.multiple_of((1 - c) * rh, 16), rh)))

        def copy(k, src, dst, to):
            return pltpu.make_async_remote_copy(src_ref=src, dst_ref=dst, send_sem=ssem.at[k], recv_sem=rsem.at[k],
                                                device_id=to, device_id_type=MESH)

        started = []
        for w in range(nb):
            mine = halves[w][0]
            for j, (cx, cy) in enumerate(chips):
                cp = copy(6 * w + j, ins[w].at[mine], outs[w].at[me, mine], (cx, cy, c))
                cp.start()
                started.append(cp)
        for w in range(nt_):
            for j, (cx, cy) in enumerate(chips):
                cp = copy(6 * nb + 3 * w + j, ins[nb + w], outs[nb + w].at[me], (cx, cy, c))
                cp.start()
                started.append(cp)
        for w in range(nb):
            mine = halves[w][0]
            for j, (cx, cy) in enumerate(chips):
                blk = outs[w].at[2 * cx + cy, mine]
                copy(6 * w + j, blk, blk, (cx, cy, c)).wait_recv()
                cp = copy(6 * w + 3 + j, blk, blk, sibling)
                cp.start()
                started.append(cp)
        for w in range(nt_):
            for j, (cx, cy) in enumerate(chips):
                blk = outs[nb + w].at[2 * cx + cy]
                copy(6 * nb + 3 * w + j, blk, blk, (cx, cy, c)).wait_recv()
        for w in range(nb):
            theirs = halves[w][1]
            for j, (cx, cy) in enumerate(chips):
                blk = outs[w].at[2 * cx + cy, theirs]
                copy(6 * w + 3 + j, blk, blk, sibling).wait_recv()
        for cp in started:
            cp.wait_send()

    nsem = 6 * nb + 3 * nt_
    return _pc(body, name="gather_weights", in_specs=[_ANY] * n_all, out_specs=[_ANY] * n_all,
               out_shape=[SDS((4,) + a.shape, a.dtype) for a in list(bigs) + list(tinies)],
               scratch_shapes=[pltpu.SemaphoreType.DMA((nsem,)), pltpu.SemaphoreType.DMA((nsem,))])(*bigs, *tinies)


def swap_halves(gs):
    n = len(gs)

    def body(*refs):
        g_refs, sib_refs = refs[:n], refs[n:2 * n]
        ssem, rsem = refs[2 * n:]
        x, y, c = _pos()
        cps = []
        for w in range(n):
            rh = gs[w].shape[1] // 2
            give = pl.ds(pl.multiple_of((1 - c) * rh, 16), rh)
            cp = pltpu.make_async_remote_copy(src_ref=g_refs[w].at[:, give], dst_ref=sib_refs[w], send_sem=ssem.at[w],
                                              recv_sem=rsem.at[w], device_id=(x, y, 1 - c), device_id_type=MESH)
            cp.start()
            cps.append(cp)
        for cp in cps:
            cp.wait()

    return _pc(body, name="swap_halves", in_specs=[_ANY] * n, out_specs=[_ANY] * n,
               out_shape=[SDS((g.shape[0], g.shape[1] // 2, g.shape[2]), g.dtype) for g in gs],
               scratch_shapes=[pltpu.SemaphoreType.DMA((n,)), pltpu.SemaphoreType.DMA((n,))])(*gs)


def add_half(sc, g, sib, name):
    l, r, cols = g.shape
    rh, br = _half_rows(r)
    nbk = rh // br

    def body(sc_ref, g_ref, s_ref, o_ref):
        o_ref[...] = (g_ref[...].astype(F32) + s_ref[...].astype(F32)).astype(BF)

    blk = pl.BlockSpec((1, br, cols), lambda j, i, sc_: (j, i, 0))
    return _pc_sp(body, (l, nbk), [pl.BlockSpec((1, br, cols), lambda j, i, sc_: (j, sc_[0] * nbk + i, 0)), blk], blk,
                  SDS((l, rh, cols), BF), name)(sc, g, sib)


def exchange_chips(ps):
    n_ = len(ps)

    def width(p_):
        return p_.shape[2] if p_.shape[0] == 4 else p_.shape[2] // 4

    def body(*refs):
        p_refs, got_refs = refs[:n_], refs[n_:2 * n_]
        ssem, rsem = refs[2 * n_:]
        x, y, c = _pos()
        cps = []
        for w in range(n_):
            wd = width(ps[w])
            for j, (cx, cy) in enumerate(_other_chips(x, y)):
                to = 2 * cx + cy
                src = p_refs[w].at[to] if ps[w].shape[0] == 4 else p_refs[w].at[0, :, pl.ds(pl.multiple_of(to * wd, LANE), wd)]
                cp = pltpu.make_async_remote_copy(src_ref=src, dst_ref=got_refs[w].at[j], send_sem=ssem.at[3 * w + j],
                                                  recv_sem=rsem.at[3 * w + j], device_id=(cx, cy, c), device_id_type=MESH)
                cp.start()
                cps.append(cp)
        for cp in cps:
            cp.wait()

    return _pc(body, name="exchange_chips", in_specs=[_ANY] * n_, out_specs=[_ANY] * n_,
               out_shape=[SDS((3, p_.shape[1], width(p_)), p_.dtype) for p_ in ps],
               scratch_shapes=[pltpu.SemaphoreType.DMA((3 * n_,)), pltpu.SemaphoreType.DMA((3 * n_,))])(*ps)


def sum4(sc, p, got, name):
    _, rh, wd = got.shape
    _, br = _half_rows(2 * rh)

    def body(sc_ref, p_ref, g_ref, r_ref):
        r_ref[...] = ((p_ref[0].astype(F32) + g_ref[0].astype(F32)) + (g_ref[1].astype(F32) + g_ref[2].astype(F32)))

    own = (pl.BlockSpec((1, br, wd), lambda i, sc_: (sc_[1], i, 0)) if p.shape[0] == 4
           else pl.BlockSpec((1, br, wd), lambda i, sc_: (0, i, sc_[1])))
    return _pc_sp(body, (rh // br,), [own, pl.BlockSpec((3, br, wd), lambda i, sc_: (0, i, 0))],
                  pl.BlockSpec((br, wd), lambda i, sc_: (i, 0)), SDS((rh, wd), F32), name)(sc, p, got)


def join_halves(halves):
    n = len(halves)

    def body(*refs):
        h_refs, got_refs = refs[:n], refs[n:2 * n]
        ssem, rsem = refs[2 * n:]
        x, y, c = _pos()
        cps = []
        for w in range(n):
            cp = pltpu.make_async_remote_copy(src_ref=h_refs[w], dst_ref=got_refs[w], send_sem=ssem.at[w],
                                              recv_sem=rsem.at[w], device_id=(x, y, 1 - c), device_id_type=MESH)
            cp.start()
            cps.append(cp)
        for cp in cps:
            cp.wait()

    return _pc(body, name="join_halves", in_specs=[_ANY] * n, out_specs=[_ANY] * n,
               out_shape=[SDS(h.shape, h.dtype) for h in halves],
               scratch_shapes=[pltpu.SemaphoreType.DMA((n,)), pltpu.SemaphoreType.DMA((n,))])(*halves)


def allreduce_small(g):
    rows = g.shape[0]

    def body(g_ref, go_ref, buf, ssem, rsem):
        x, y, c = _pos()
        me = 4 * x + 2 * y + c
        flip = lambda a, f: 1 - a if f else a
        peers = [(flip(x, k & 4), flip(y, k & 2), flip(c, k & 1)) for k in range(1, 8)]
        cps = []
        for k, peer in enumerate(peers):
            cp = pltpu.make_async_remote_copy(src_ref=g_ref, dst_ref=buf.at[me], send_sem=ssem.at[k], recv_sem=rsem.at[k],
                                              device_id=peer, device_id_type=MESH)
            cp.start()
            cps.append(cp)
        buf[me] = g_ref[...]
        for k, (px, py, pc_) in enumerate(peers):
            pltpu.make_async_remote_copy(src_ref=g_ref, dst_ref=buf.at[4 * px + 2 * py + pc_], send_sem=ssem.at[k],
                                         recv_sem=rsem.at[k], device_id=(px, py, pc_), device_id_type=MESH).wait_recv()
        for cp in cps:
            cp.wait_send()
        tot = buf[0]
        for dev in range(1, 8):
            tot = tot + buf[dev]
        go_ref[...] = tot

    vm = pl.BlockSpec(memory_space=pltpu.VMEM)
    return _pc(body, name="allreduce_small", in_specs=[vm], out_specs=vm, out_shape=SDS((rows, LANE), F32),
               scratch_shapes=[pltpu.VMEM((8, rows, LANE), F32), pltpu.SemaphoreType.DMA((7,)),
                               pltpu.SemaphoreType.DMA((7,))],
               compiler_params=pltpu.CompilerParams(vmem_limit_bytes=VMEM_LIMIT))(g)


def _pack_small(entries, get):
    flat = jnp.concatenate([get(n).reshape(-1).astype(F32) for n, _ in entries])
    rows = -(-flat.shape[0] // (8 * LANE)) * 8
    return jnp.pad(flat, (0, rows * LANE - flat.shape[0])).reshape(rows, LANE)


def _unpack_small(entries, packed):
    out, off = {}, 0
    flat = packed.reshape(-1)
    for name, n in entries:
        out[name] = flat[off:off + n]
        off += n
    return out


def _cols_full(blk):
    return blk.transpose(1, 0, 2).reshape(blk.shape[1], 4 * blk.shape[2])


def kernel(x, p, pre_mix_norm, w_in, a_ln_g, a_ln_b, a_spatial_w, a_spatial_b, a_out, b_gk, b_gk_bias, b_out_norm, b_out, w_mix_out, post_mix_norm, pre_ffn_norm, w_up, conv_w, conv_b, w_down, post_ffn_norm, w_ple, w_ple_gate, post_ple_norm, loss_target, m_pre_mix_norm, m_w_in, m_a_ln_g, m_a_ln_b, m_a_spatial_w, m_a_spatial_b, m_a_out, m_b_gk, m_b_gk_bias, m_b_out_norm, m_b_out, m_w_mix_out, m_post_mix_norm, m_pre_ffn_norm, m_w_up, m_conv_w, m_conv_b, m_w_down, m_post_ffn_norm, m_w_ple, m_w_ple_gate, m_post_ple_norm, v_pre_mix_norm, v_w_in, v_a_ln_g, v_a_ln_b, v_a_spatial_w, v_a_spatial_b, v_a_out, v_b_gk, v_b_gk_bias, v_b_out_norm, v_b_out, v_w_mix_out, v_post_mix_norm, v_pre_ffn_norm, v_w_up, v_conv_w, v_conv_b, v_w_down, v_post_ffn_norm, v_w_ple, v_w_ple_gate, v_post_ple_norm):
    args = dict(locals())
    order = ['pre_mix_norm', 'w_in', 'a_ln_g', 'a_ln_b', 'a_spatial_w', 'a_spatial_b', 'a_out', 'b_gk', 'b_gk_bias',
             'b_out_norm', 'b_out', 'w_mix_out', 'post_mix_norm', 'pre_ffn_norm', 'w_up', 'conv_w', 'conv_b', 'w_down',
             'post_ffn_norm', 'w_ple', 'w_ple_gate', 'post_ple_norm']
    assert sorted(BIG + TINY + tuple(n for n, _ in SMALL)) == sorted(order)
    s = x.shape[1]
    xs = x.reshape(s, D)
    ps = p.reshape(s, PLE)
    tgt = loss_target.reshape(s, D)
    t_big = min(1024, s)
    t_mid = min(512, s)
    t_small = min(256, s)
    mx_, my_, mc_ = _pos()
    me = 2 * mx_ + my_
    sc = jnp.stack([mc_, me]).astype(jnp.int32)
    shard = lambda n: args[n].reshape(args[n].shape[1:])

    mine = [shard(n).astype(BF) for n in BIG] + [shard(n) for n in TINY]
    full = {n: lax.dynamic_update_slice(got, own[None], (me, 0, 0))
            for n, own, got in zip(BIG + TINY, mine, gather_weights(mine[:len(BIG)], mine[len(BIG):]))}
    wi = _cols_full(full["w_in"])
    seg = lambda a, b: wi[:, a:b]
    qk = [seg(1024 + h * B_HK, 1024 + (h + 1) * B_HK) for h in range(B_H)]
    kk = [seg(1536 + h * B_HK, 1536 + (h + 1) * B_HK) for h in range(B_H)]
    w_z = jnp.concatenate([seg(0, 1024)] + [m_ for h in range(B_H) for m_ in (qk[h], kk[h])]
                          + [seg(2048, 4096), seg(4112, 6160), seg(4096, 4112), jnp.zeros((D, LANE - B_RANK), BF)], axis=1)
    wgk = jnp.pad(_cols_full(full["b_gk"]).astype(BF), ((0, LANE - B_RANK), (0, 0)))
    w_aout, w_ple_f, w_conv = _cols_full(full["a_out"]), _cols_full(full["w_ple"]), _cols_full(full["conv_w"])
    w_bout, w_mix, w_pg = (full[n].reshape(D, D) for n in ("b_out", "w_mix_out", "w_ple_gate"))
    w_dn, w_up3 = full["w_down"].reshape(D_FF, D), full["w_up"]
    g1, g2, g3 = pre_mix_norm.reshape(1, D), post_mix_norm.reshape(1, D), pre_ffn_norm.reshape(1, D)
    g4, g5 = post_ffn_norm.reshape(1, D), post_ple_norm.reshape(1, D)
    ln_g, ln_b = a_ln_g.reshape(1, A_W), a_ln_b.reshape(1, A_W)
    w_s = a_spatial_w.reshape(A_G, A_C, A_C)
    bs_t = a_spatial_b.reshape(A_G, A_C).T
    gk_bias = b_gk_bias.reshape(1, B_H * B_HK)
    wn = b_out_norm.reshape(1, B_HV)
    cb = conv_b.reshape(1, 2 * D_FF)
    idx = jnp.arange(t_mid)
    ltri = ((idx[:, None] // B_C == idx[None, :] // B_C) & (idx[None, :] <= idx[:, None])).astype(BF)

    a, z, qk32 = norm_matmul(xs, g1, w_z, D, t_big, "in_proj", nblk=6, f32_blk=1)
    zl = mm_cols(a, w_z, 48, LANE, t_big, "lr_proj")
    sa = sgu_fwd(z, ln_g, ln_b, w_s, bs_t, t_mid)
    ob, o, states = gla_fwd(z, qk32, zl, wgk, gk_bias, wn, ltri, t_mid)
    ya, yb, mp, mx, h1 = mix_fwd(sa, ob, z, xs, w_aout, w_bout, w_mix, g2, t_small)
    c, up = norm_matmul(h1, g3, w_up3, 1408, t_big, "up_proj")
    ff = ffn_gate_fwd(up, w_conv, cb, t_small)
    f, h2, pg, pe, dy, loss = out_fwd(ff, h1, ps, tgt, w_dn, w_pg, w_ple_f, g4, g5, t_small)

    dh2, dpe, dpg, df, dff, gg5, gg4 = out_bwd(dy, pg, pe, f, g5, g4, w_pg, w_dn, t_small)
    dup_g, dup_v, gcw_g, gcw_v, gcb_g, gcb_v = ffn_gate_bwd(up, dff, w_conv, cb, t_small)
    wu = lambda k: pl.BlockSpec((None, D, 1408), lambda i: (k, 0, 0), pipeline_mode=pl.Buffered(1))
    dh1, gg3 = nt_normbwd([(dup_g, 0, w_up3, wu(0)), (dup_g, 1, w_up3, wu(1)), (dup_v, 0, w_up3, wu(2)),
                           (dup_v, 1, w_up3, wu(3))], h1, g3, dh2, t_small, "up_bwd")
    dmx, dya, dyb, dga, dgb, dsa, dob, gg2 = mix_bwd(dh1, mx, z, ya, yb, g2, w_mix, w_aout, w_bout, t_small)
    duv, g_lng, g_lnb, g_ws, g_bst = sgu_bwd(z, dsa, ln_g, ln_b, w_s, bs_t, t_mid)
    dqk, dvb, dog, dpre, g_gkb, g_wn = gla_bwd(z, qk32, zl, o, dob, states, wgk, gk_bias, wn, ltri, ltri.T, t_mid)
    dlr = mm_nt_small(dpre, wgk, t_mid, "dlr")
    segs = [duv, dqk, dvb, dog, dga, dgb, dlr]
    wz = lambda k, width: pl.BlockSpec((D, width), lambda i: (0, k), pipeline_mode=pl.Buffered(1))
    dx, gg1 = nt_normbwd([(segs[k], 0, w_z, wz(k, D)) for k in range(6)] + [(dlr, 0, w_z, wz(48, LANE))],
                         xs, g1, dh1, t_small, "in_bwd")

    gz = [mm_tn(a, sg_, "dw_in_%d" % k) for k, sg_ in enumerate(segs)]
    gq = [gz[1][:, h * 256:h * 256 + B_HK] for h in range(B_H)]
    gk = [gz[1][:, h * 256 + B_HK:(h + 1) * 256] for h in range(B_H)]
    g_in = jnp.concatenate([gz[0]] + gq + gk + [gz[2], gz[3], gz[6][:, :B_RANK], gz[4], gz[5]], axis=1)
    grads = {
        "w_in": g_in.reshape(D, 4, 1540).transpose(1, 0, 2),
        "a_out": mm_tn(sa, dya, "dw_a_out")[None],
        "b_out": mm_tn(ob, dyb, "dw_b_out").reshape(4, D // 4, D),
        "w_mix_out": mm_tn(mp, dmx, "dw_mix").reshape(4, D // 4, D),
        "w_up": jnp.concatenate([mm_tn(c, dup_g, "dw_up_g"), mm_tn(c, dup_v, "dw_up_v")], axis=1)[None],
        "w_down": mm_tn(ff, df, "dw_down").reshape(4, D_FF // 4, D),
        "w_ple": mm_tn(ps, dpe, "dw_ple")[None],
        "w_ple_gate": mm_tn(h2, dpg, "dw_ple_gate").reshape(4, D // 4, D),
    }

    gs = [grads[n] for n in BIG]
    parts = [add_half(sc, g, sib, "partial_" + n) for n, g, sib in zip(BIG, gs, swap_halves(gs))]
    reds = [sum4(sc, p_, got, "sum_" + n) for n, p_, got in zip(BIG, parts, exchange_chips(parts))]
    outs = {}
    for n, red, sib in zip(BIG, reds, join_halves(reds)):
        res = adamw_halves(sc, red, sib, shard(n), shard("m_" + n), shard("v_" + n), "adamw_" + n)
        outs[n] = [r_.reshape(args[n].shape) for r_ in res]

    small_g = {
        "pre_mix_norm": gg1, "a_ln_g": g_lng, "a_ln_b": g_lnb, "a_spatial_w": g_ws, "a_spatial_b": g_bst.T,
        "b_gk_bias": g_gkb, "b_out_norm": g_wn, "post_mix_norm": gg2, "pre_ffn_norm": gg3,
        "conv_b": jnp.concatenate([gcb_g, gcb_v], axis=1), "post_ffn_norm": gg4, "post_ple_norm": gg5,
        "b_gk": mm_tn(zl, dpre, "dw_gk")[:B_RANK], "conv_w": jnp.concatenate([gcw_g, gcw_v], axis=1),
    }
    red_entries = SMALL + (("b_gk", B_RANK * 512), ("conv_w", 3 * 2 * D_FF))
    g_fin = _unpack_small(red_entries, allreduce_small(_pack_small(red_entries, lambda n: small_g[n])))
    g_fin["b_gk"] = lax.dynamic_slice(g_fin["b_gk"].reshape(B_RANK, 512), (0, me * B_HK), (B_RANK, B_HK))
    g_fin["conv_w"] = lax.dynamic_slice(g_fin["conv_w"].reshape(3, 2 * D_FF), (0, me * 1408), (3, 1408))
    upd_entries = SMALL + (("b_gk", B_RANK * B_HK), ("conv_w", 3 * 1408))
    res = adamw_small(*[_pack_small(upd_entries, get) for get in
                        (lambda n: g_fin[n], lambda n: args[n], lambda n: args["m_" + n], lambda n: args["v_" + n])])
    res = [_unpack_small(upd_entries, r_) for r_ in res]
    for n, _ in upd_entries:
        outs[n] = [r_[n].reshape(args[n].shape) for r_ in [g_fin] + res]

    total = lax.psum(loss[0, 0], ("x", "y", "c"))
    return (total, dx.reshape(x.shape), *[outs[n][0] for n in order], *[outs[n][1] for n in order],
            *[outs[n][2] for n in order], *[outs[n][3] for n in order])
```

```python
import functools
import math

import jax
import jax.numpy as jnp
from jax import lax
from jax.experimental import pallas as pl
from jax.experimental.pallas import tpu as pltpu

F32 = jnp.float32
BF = jnp.bfloat16
SDS = jax.ShapeDtypeStruct
MESH = pl.DeviceIdType.MESH

EPS = 1e-6
D = 1024
A_W = 512
A_G, A_C = 8, 128
A_GD = A_W // A_G
B_H, B_HK, B_HV = 4, 128, 256
B_C = 64
B_RANK = 16
D_FF = 2816
PLE = 256
ZW = 6272
LANE = 128
VMEM_LIMIT = 60 * 1024 * 1024

ADAM_LR, ADAM_B1, ADAM_B2, ADAM_EPS, ADAM_WD, ADAM_STEP = 0.001, 0.9, 0.999, 1e-08, 0.01, 10

_GC = math.sqrt(2.0 / math.pi)
_GA = 0.044715

BIG = ("w_in", "a_out", "b_out", "w_mix_out", "w_up", "w_down", "w_ple", "w_ple_gate")
TINY = ("b_gk", "conv_w")
SMALL = (("pre_mix_norm", 1024), ("a_ln_g", 512), ("a_ln_b", 512), ("a_spatial_w", 131072),
         ("a_spatial_b", 1024), ("b_gk_bias", 512), ("b_out_norm", 256), ("post_mix_norm", 1024),
         ("pre_ffn_norm", 1024), ("conv_b", 5632), ("post_ffn_norm", 1024), ("post_ple_norm", 1024))


def _pc(body, **kw):
    return pl.pallas_call(body, **kw)


def _cp(n):
    return pltpu.CompilerParams(dimension_semantics=("arbitrary",) * n, vmem_limit_bytes=VMEM_LIMIT)


def _const(shape):
    nd = len(shape)
    return pl.BlockSpec(shape, lambda *_: (0,) * nd, pipeline_mode=pl.Buffered(1))


def _acc(shape):
    nd = len(shape)
    return pl.BlockSpec(shape, lambda *_: (0,) * nd)


def _dot(a, b):
    return jnp.dot(a, b, preferred_element_type=F32)


def _dot_nt(a, b):
    return lax.dot_general(a, b, (((1,), (1,)), ((), ())), preferred_element_type=F32)


def _dot_tn(a, b):
    return lax.dot_general(a, b, (((0,), (0,)), ((), ())), preferred_element_type=F32)


def _gelu(x):
    return 0.5 * x * (1.0 + jnp.tanh(_GC * (x + _GA * x * x * x)))


def _gelu_and_grad(x):
    x2 = x * x
    t = jnp.tanh(_GC * x * (1.0 + _GA * x2))
    return 0.5 * x * (1.0 + t), 0.5 * (1.0 + t) + 0.5 * x * (1.0 - t * t) * (_GC * (1.0 + 3.0 * _GA * x2))


def _log_sigmoid(x):
    return jnp.minimum(x, 0.0) - jnp.log(1.0 + jnp.exp(-jnp.abs(x)))


def _rms(x, g):
    return x * lax.rsqrt(jnp.mean(x * x, axis=-1, keepdims=True) + EPS) * g


def _rms_bwd(dy, x, g):
    r = lax.rsqrt(jnp.mean(x * x, axis=-1, keepdims=True) + EPS)
    n = x * r
    dn = dy * g
    dx = r * (dn - n * jnp.mean(dn * n, axis=-1, keepdims=True))
    return dx, jnp.sum(dy * n, axis=0, keepdims=True)


def _ldot3(l, x):
    h = x.astype(BF)
    r = x - h.astype(F32)
    m = r.astype(BF)
    lo = (r - m.astype(F32)).astype(BF)
    return _dot(l, h) + _dot(l, m) + _dot(l, lo)


def _split_side(refs, n_in, n_out, n_scratch, side):
    si, so = (side.n_in, side.n_out) if side else (0, 0)
    cuts = [n_in, si, n_out, so, n_scratch]
    out, at = [], 0
    for c in cuts:
        out.append(refs[at:at + c])
        at += c
    return (*out, refs[at:])


def _side_specs(side):
    return ([_ANY] * side.n_in, [_ANY] * side.n_out, side.out_shapes, side.scratch, side.ins) if side else ([],) * 5


def norm_matmul(x, g, w, bn, t, name, nblk=None, f32_blk=None, side=None):
    s, dm = x.shape
    if w.ndim == 3:
        nblk = w.shape[0]
        w_spec = pl.BlockSpec((None, dm, bn), lambda i, j: (j, 0, 0))
    else:
        nblk = nblk or w.shape[1] // bn
        w_spec = pl.BlockSpec((dm, bn), lambda i, j: (0, j))
    extra = f32_blk is not None
    nt = s // t

    def body(*refs):
        (x_ref, g_ref, w_ref), s_in, outs, s_out, (a_sc,), s_scr = _split_side(refs, 3, 2 + extra, 1, side)
        a_ref, z_ref = outs[:2]
        i, j = pl.program_id(0), pl.program_id(1)
        if side:
            @pl.when((i == 0) & (j == 0))
            def _():
                side.start(s_in, s_out, *s_scr)

        @pl.when(j == 0)
        def _():
            a = _rms(x_ref[...], g_ref[...]).astype(BF)
            a_sc[...] = a
            a_ref[...] = a

        acc = _dot(a_sc[...], w_ref[...])
        z_ref[...] = acc.astype(BF)
        if extra:
            @pl.when(j == f32_blk)
            def _():
                outs[2][...] = acc
        if side:
            @pl.when((i == nt - 1) & (j == nblk - 1))
            def _():
                side.finish(s_in, s_out, *s_scr)

    si_specs, so_specs, so_shapes, s_scratch, s_ins = _side_specs(side)
    return _pc(
        body, name=name, grid=(nt, nblk),
        in_specs=[pl.BlockSpec((t, dm), lambda i, j: (i, 0)), _const((1, dm)), w_spec] + si_specs,
        out_specs=[pl.BlockSpec((t, dm), lambda i, j: (i, 0)), pl.BlockSpec((t, bn), lambda i, j: (i, j))]
        + [pl.BlockSpec((t, bn), lambda i, j: (i, 0))] * extra + so_specs,
        out_shape=[SDS((s, dm), BF), SDS((s, nblk * bn), BF)] + [SDS((s, bn), F32)] * extra + so_shapes,
        scratch_shapes=[pltpu.VMEM((t, dm), BF)] + s_scratch, compiler_params=_cp(2))(x, g, w, *s_ins)


def _sgu_masked(ws_ref):
    r = lax.broadcasted_iota(jnp.int32, (A_C, A_C), 0)
    c = lax.broadcasted_iota(jnp.int32, (A_C, A_C), 1)
    return [jnp.where(c <= r, ws_ref[g], 0.0).astype(BF) for g in range(A_G)]


def _sgu_recompute(v, lng, lnb):
    gv, dgv = _gelu_and_grad(v)
    mu = jnp.mean(gv, axis=-1, keepdims=True)
    xc = gv - mu
    rstd = lax.rsqrt(jnp.mean(xc * xc, axis=-1, keepdims=True) + EPS)
    xhat = xc * rstd
    return dgv, rstd, xhat, (xhat * lng + lnb).astype(BF)


def sgu_fwd(z, ln_g, ln_b, w_s, bs_t, t):
    s = z.shape[0]

    def body(u_ref, v_ref, g_ref, b_ref, ws_ref, bs_ref, sa_ref, s_sc):
        wm = _sgu_masked(ws_ref)
        for ci in range(t // A_C):
            rows = pl.ds(ci * A_C, A_C)
            _, _, _, vn = _sgu_recompute(v_ref[rows, :].astype(F32), g_ref[...], b_ref[...])
            for g in range(A_G):
                cols = slice(g * A_GD, (g + 1) * A_GD)
                s_sc[:, cols] = _dot(wm[g], vn[:, cols]) + bs_ref[:, g:g + 1]
            sa_ref[rows, :] = (_gelu(u_ref[rows, :].astype(F32)) * s_sc[...]).astype(BF)

    return _pc(
        body, name="sgu_fwd", grid=(s // t,),
        in_specs=[pl.BlockSpec((t, A_W), lambda i: (i, 0)), pl.BlockSpec((t, A_W), lambda i: (i, 1)),
                  _const((1, A_W)), _const((1, A_W)), _const((A_G, A_C, A_C)), _const((A_C, A_G))],
        out_specs=pl.BlockSpec((t, A_W), lambda i: (i, 0)),
        out_shape=SDS((s, A_W), BF),
        scratch_shapes=[pltpu.VMEM((A_C, A_W), F32)], compiler_params=_cp(1))(z, z, ln_g, ln_b, w_s, bs_t)


def _gla_decays(qk_ref, lr_ref, wgk_ref, bias_ref, l_ref, t):
    nc = t // B_C
    q = qk_ref[:, :B_HK].astype(F32) * (B_HK ** -0.5)
    k = qk_ref[:, B_HK:].astype(F32)
    pre = _dot(lr_ref[...], wgk_ref[...]) + bias_ref[...]
    la = _log_sigmoid(pre) * (1.0 / 16.0)
    b = _ldot3(l_ref[...], la)
    b3 = b.reshape(nc, B_C, B_HK)
    bl = jnp.broadcast_to(b3[:, B_C - 1:B_C, :], (nc, B_C, B_HK)).reshape(t, B_HK)
    eb, enb, etb = jnp.exp(b), jnp.exp(-b), jnp.exp(bl - b)
    return pre, b, bl, eb, enb, etb, q * eb, k * enb, k * etb


def gla_fwd(z, qk32, zl, wgk, bias, wn, ltri, t):
    s = z.shape[0]
    nc = t // B_C

    def body(qk_ref, v_ref, og_ref, lr_ref, wgk_ref, bias_ref, wn_ref, l_ref, ob_ref, o_ref, st_ref, st_sc, o_sc):
        h = pl.program_id(1)

        @pl.when(pl.program_id(0) == 0)
        def _():
            st_sc[h] = jnp.zeros((B_HV, B_HK), F32)

        _, _, bl, _, _, _, qd, ki, kt = _gla_decays(qk_ref, lr_ref, wgk_ref, bias_ref, l_ref, t)
        qd, ki, kt = qd.astype(BF), ki.astype(BF), kt.astype(BF)
        vb = v_ref[...]
        sc = jnp.where(l_ref[...] > 0, _dot_nt(qd, ki), 0.0).astype(BF)
        o_sc[...] = _dot(sc, vb)
        for n in range(nc):
            rows = slice(n * B_C, (n + 1) * B_C)
            st = st_sc[h]
            stb = st.astype(BF)
            st_ref[n, 0] = stb
            o_sc[rows, :] += _dot_nt(qd[rows], stb)
            st_sc[h] = st * jnp.exp(bl[n * B_C:n * B_C + 1, :]) + _dot_tn(vb[rows], kt[rows])
        ob = o_sc[...].astype(BF)
        o_ref[...] = ob
        og = og_ref[...].astype(F32)
        ob_ref[...] = (_rms(ob.astype(F32), wn_ref[...]) * og * jax.nn.sigmoid(og)).astype(BF)

    return _pc(
        body, name="gla_fwd", grid=(s // t, B_H),
        in_specs=[pl.BlockSpec((t, 256), lambda i, h: (i, h)), pl.BlockSpec((t, 256), lambda i, h: (i, 8 + h)),
                  pl.BlockSpec((t, 256), lambda i, h: (i, 12 + h)), pl.BlockSpec((t, LANE), lambda i, h: (i, 0)),
                  pl.BlockSpec((LANE, B_HK), lambda i, h: (0, h)), pl.BlockSpec((1, B_HK), lambda i, h: (0, h)),
                  _const((1, B_HV)), _const((t, t))],
        out_specs=[pl.BlockSpec((t, B_HV), lambda i, h: (i, h)), pl.BlockSpec((t, B_HV), lambda i, h: (i, h)),
                   pl.BlockSpec((nc, 1, B_HV, B_HK), lambda i, h: (i, h, 0, 0))],
        out_shape=[SDS((s, D), BF), SDS((s, D), BF), SDS((s // B_C, B_H, B_HV, B_HK), BF)],
        scratch_shapes=[pltpu.VMEM((B_H, B_HV, B_HK), F32), pltpu.VMEM((t, B_HV), F32)],
        compiler_params=_cp(2))(qk32, z, z, zl, wgk, bias, wn, ltri)


def mix_fwd(sa, ob, z, x, a_out, b_out, w_mix, g2, t):
    s = x.shape[0]

    def body(sa_ref, ob_ref, ga_ref, gb_ref, x_ref, ao_ref, bo_ref, wm_ref, g2_ref,
             ya_ref, yb_ref, mp_ref, mx_ref, h1_ref):
        ya = _dot(sa_ref[...], ao_ref[...]).astype(BF)
        yb = _dot(ob_ref[...], bo_ref[...]).astype(BF)
        ya_ref[...] = ya
        yb_ref[...] = yb
        mp = (jax.nn.sigmoid(ga_ref[...].astype(F32)) * ya.astype(F32)
              + jax.nn.sigmoid(gb_ref[...].astype(F32)) * yb.astype(F32)).astype(BF)
        mp_ref[...] = mp
        mx = _dot(mp, wm_ref[...]).astype(BF)
        mx_ref[...] = mx
        h1_ref[...] = x_ref[...] + _rms(mx.astype(F32), g2_ref[...])

    row = lambda w: pl.BlockSpec((t, w), lambda i: (i, 0))
    return _pc(
        body, name="mix_fwd", grid=(s // t,),
        in_specs=[row(A_W), row(D), pl.BlockSpec((t, D), lambda i: (i, 4)), pl.BlockSpec((t, D), lambda i: (i, 5)),
                  row(D), _const((A_W, D)), _const((D, D)), _const((D, D)), _const((1, D))],
        out_specs=[row(D)] * 5,
        out_shape=[SDS((s, D), BF)] * 4 + [SDS((s, D), F32)],
        compiler_params=_cp(1))(sa, ob, z, z, x, a_out, b_out, w_mix, g2)


def _conv(u, w, b):
    return b + w[0:1] * pltpu.roll(u, 2, 0) + w[1:2] * pltpu.roll(u, 1, 0) + w[2:3] * u


def ffn_gate_fwd(up, conv_w, conv_b, t):
    s = up.shape[0]
    bn = 1408
    hb = t // 8

    def body(ug_ref, uv_ref, hg_ref, hv_ref, wg_ref, wv_ref, bg_ref, bv_ref, ff_ref):
        live = (pl.program_id(1) > 0).astype(F32)

        def branch(u_ref, h_ref, w_ref, b_ref):
            ext = jnp.concatenate([h_ref[...].astype(F32) * live, u_ref[...].astype(F32)], axis=0)
            return _conv(ext, w_ref[...], b_ref[...])[8:]

        ff_ref[...] = (_gelu(branch(ug_ref, hg_ref, wg_ref, bg_ref)) * branch(uv_ref, hv_ref, wv_ref, bv_ref)).astype(BF)

    halo = lambda off: pl.BlockSpec((8, bn), lambda j, i: (jnp.maximum(i * hb - 1, 0), j + off))
    return _pc(
        body, name="ffn_gate_fwd", grid=(2, s // t),
        in_specs=[pl.BlockSpec((t, bn), lambda j, i: (i, j)), pl.BlockSpec((t, bn), lambda j, i: (i, j + 2)),
                  halo(0), halo(2),
                  pl.BlockSpec((3, bn), lambda j, i: (0, j)), pl.BlockSpec((3, bn), lambda j, i: (0, j + 2)),
                  pl.BlockSpec((1, bn), lambda j, i: (0, j)), pl.BlockSpec((1, bn), lambda j, i: (0, j + 2))],
        out_specs=pl.BlockSpec((t, bn), lambda j, i: (i, j)),
        out_shape=SDS((s, D_FF), BF), compiler_params=_cp(2))(up, up, up, up, conv_w, conv_w, conv_b, conv_b)


def out_fwd(ff, h1, p, tgt, w_down, w_pg, w_ple, g4, g5, t):
    s = h1.shape[0]

    def body(ff_ref, h1_ref, p_ref, t_ref, wd_ref, wpg_ref, wpl_ref, g4_ref, g5_ref,
             f_ref, h2_ref, pg_ref, pe_ref, dy_ref, loss_ref):
        @pl.when(pl.program_id(0) == 0)
        def _():
            loss_ref[...] = jnp.zeros((1, 1), F32)

        f = _dot(ff_ref[...], wd_ref[...]).astype(BF)
        f_ref[...] = f
        h2 = h1_ref[...] + _rms(f.astype(F32), g4_ref[...])
        h2b = h2.astype(BF)
        h2_ref[...] = h2b
        pg = _dot(h2b, wpg_ref[...]).astype(BF)
        pe = _dot(p_ref[...].astype(BF), wpl_ref[...]).astype(BF)
        pg_ref[...] = pg
        pe_ref[...] = pe
        y = h2 + _rms(jax.nn.sigmoid(pg.astype(F32)) * pe.astype(F32), g5_ref[...])
        err = y - t_ref[...]
        dy_ref[...] = err * (1.0 / D)
        loss_ref[...] += (0.5 / D) * jnp.sum(err * err)

    row = lambda w: pl.BlockSpec((t, w), lambda i: (i, 0))
    return _pc(
        body, name="out_fwd", grid=(s // t,),
        in_specs=[row(D_FF), row(D), row(PLE), row(D), _const((D_FF, D)), _const((D, D)), _const((PLE, D)),
                  _const((1, D)), _const((1, D))],
        out_specs=[row(D)] * 5 + [_acc((1, 1))],
        out_shape=[SDS((s, D), BF)] * 4 + [SDS((s, D), F32), SDS((1, 1), F32)],
        compiler_params=_cp(1))(ff, h1, p, tgt, w_down, w_pg, w_ple, g4, g5)


def out_bwd(dy, pg, pe, f, g5, g4, w_pg, w_down, t):
    s = dy.shape[0]

    def body(dy_ref, pg_ref, pe_ref, f_ref, g5_ref, g4_ref, wpg_ref, wd_ref,
             dh2_ref, dpe_ref, dpg_ref, df_ref, dff_ref, gg5_ref, gg4_ref):
        @pl.when(pl.program_id(0) == 0)
        def _():
            gg5_ref[...] = jnp.zeros((1, D), F32)
            gg4_ref[...] = jnp.zeros((1, D), F32)

        dy_ = dy_ref[...]
        pg_ = pg_ref[...].astype(F32)
        pe_ = pe_ref[...].astype(F32)
        sg = jax.nn.sigmoid(pg_)
        dple, dg5 = _rms_bwd(dy_, sg * pe_, g5_ref[...])
        gg5_ref[...] += dg5
        dpe_ref[...] = (dple * sg).astype(BF)
        dpg = (dple * pe_ * sg * (1.0 - sg)).astype(BF)
        dpg_ref[...] = dpg
        dh2 = dy_ + _dot_nt(dpg, wpg_ref[...])
        dh2_ref[...] = dh2
        df, dg4 = _rms_bwd(dh2, f_ref[...].astype(F32), g4_ref[...])
        gg4_ref[...] += dg4
        dfb = df.astype(BF)
        df_ref[...] = dfb
        dff_ref[...] = _dot_nt(dfb, wd_ref[...]).astype(BF)

    row = lambda w: pl.BlockSpec((t, w), lambda i: (i, 0))
    return _pc(
        body, name="out_bwd", grid=(s // t,),
        in_specs=[row(D), row(D), row(D), row(D), _const((1, D)), _const((1, D)), _const((D, D)), _const((D_FF, D))],
        out_specs=[row(D), row(D), row(D), row(D), row(D_FF), _acc((1, D)), _acc((1, D))],
        out_shape=[SDS((s, D), F32), SDS((s, D), BF), SDS((s, D), BF), SDS((s, D), BF), SDS((s, D_FF), BF),
                   SDS((1, D), F32), SDS((1, D), F32)],
        compiler_params=_cp(1))(dy, pg, pe, f, g5, g4, w_pg, w_down)


def ffn_gate_bwd(up, dff, conv_w, conv_b, t):
    s = up.shape[0]
    bn = 1408
    hb = t // 8
    nt = s // t
    r = t + 16

    def body(ug_ref, uv_ref, hbg_ref, hbv_ref, hag_ref, hav_ref, d_ref, da_ref, wg_ref, wv_ref, bg_ref, bv_ref,
             dug_ref, duv_ref, gwg_ref, gwv_ref, gbg_ref, gbv_ref):
        i = pl.program_id(1)

        @pl.when(i == 0)
        def _():
            gwg_ref[...] = jnp.zeros((3, bn), F32)
            gwv_ref[...] = jnp.zeros((3, bn), F32)
            gbg_ref[...] = jnp.zeros((1, bn), F32)
            gbv_ref[...] = jnp.zeros((1, bn), F32)

        first = (i > 0).astype(F32)
        last = (i < nt - 1).astype(F32)
        ext = lambda hb_ref, u_ref, ha_ref: jnp.concatenate(
            [hb_ref[...].astype(F32) * first, u_ref[...].astype(F32), ha_ref[...].astype(F32)], axis=0)
        ug, uv = ext(hbg_ref, ug_ref, hag_ref), ext(hbv_ref, uv_ref, hav_ref)
        wg, wv = wg_ref[...], wv_ref[...]
        cg, cv = _conv(ug, wg, bg_ref[...]), _conv(uv, wv, bv_ref[...])
        dff_ = jnp.concatenate([jnp.zeros((8, bn), F32), d_ref[...].astype(F32), da_ref[...].astype(F32) * last], axis=0)
        gl, dgl = _gelu_and_grad(cg)
        dg = dff_ * cv * dgl
        dv = dff_ * gl

        def back(dc, u, w, du_ref, gw_ref, gb_ref):
            du = w[2:3] * dc + w[1:2] * pltpu.roll(dc, r - 1, 0) + w[0:1] * pltpu.roll(dc, r - 2, 0)
            du_ref[...] = du[8:t + 8].astype(BF)
            dct = dc[8:t + 8]
            gw_ref[0:1, :] += jnp.sum(dct * pltpu.roll(u, 2, 0)[8:t + 8], axis=0, keepdims=True)
            gw_ref[1:2, :] += jnp.sum(dct * pltpu.roll(u, 1, 0)[8:t + 8], axis=0, keepdims=True)
            gw_ref[2:3, :] += jnp.sum(dct * u[8:t + 8], axis=0, keepdims=True)
            gb_ref[...] += jnp.sum(dct, axis=0, keepdims=True)

        back(dg, ug, wg, dug_ref, gwg_ref, gbg_ref)
        back(dv, uv, wv, duv_ref, gwv_ref, gbv_ref)

    tile = lambda off: pl.BlockSpec((t, bn), lambda j, i: (i, j + off))
    before = lambda off: pl.BlockSpec((8, bn), lambda j, i: (jnp.maximum(i * hb - 1, 0), j + off))
    after = lambda off: pl.BlockSpec((8, bn), lambda j, i: (jnp.minimum((i + 1) * hb, nt * hb - 1), j + off))
    cw = lambda off: pl.BlockSpec((3, bn), lambda j, i: (0, j + off))
    cb = lambda off: pl.BlockSpec((1, bn), lambda j, i: (0, j + off))
    return _pc(
        body, name="ffn_gate_bwd", grid=(2, nt),
        in_specs=[tile(0), tile(2), before(0), before(2), after(0), after(2), tile(0), after(0),
                  cw(0), cw(2), cb(0), cb(2)],
        out_specs=[tile(0), tile(0), cw(0), cw(0), cb(0), cb(0)],
        out_shape=[SDS((s, D_FF), BF), SDS((s, D_FF), BF), SDS((3, D_FF), F32), SDS((3, D_FF), F32),
                   SDS((1, D_FF), F32), SDS((1, D_FF), F32)],
        compiler_params=_cp(2))(up, up, up, up, up, up, dff, dff, conv_w, conv_w, conv_b, conv_b)


def nt_normbwd(pairs, xin, gain, dres, t, name):
    s = xin.shape[0]
    np_ = len(pairs)
    dys = [p_[0] for p_ in pairs]
    ws = [p_[2] for p_ in pairs]

    def body(*refs):
        dy_refs, w_refs = refs[:np_], refs[np_:2 * np_]
        x_ref, g_ref, dres_ref, dx_ref, gg_ref = refs[2 * np_:]

        @pl.when(pl.program_id(0) == 0)
        def _():
            gg_ref[...] = jnp.zeros((1, D), F32)

        acc = _dot_nt(dy_refs[0][...], w_refs[0][...])
        for k in range(1, np_):
            acc += _dot_nt(dy_refs[k][...], w_refs[k][...])
        dxn, dg = _rms_bwd(acc, x_ref[...], g_ref[...])
        dx_ref[...] = dres_ref[...] + dxn
        gg_ref[...] += dg

    row = lambda w: pl.BlockSpec((t, w), lambda i: (i, 0))
    dy_spec = lambda blk, width: pl.BlockSpec((t, width), lambda i: (i, blk))
    return _pc(
        body, name=name, grid=(s // t,),
        in_specs=[dy_spec(p_[1], p_[3].block_shape[-1]) for p_ in pairs] + [p_[3] for p_ in pairs]
        + [row(D), _const((1, D)), row(D)],
        out_specs=[row(D), _acc((1, D))],
        out_shape=[SDS((s, D), F32), SDS((1, D), F32)],
        compiler_params=_cp(1))(*dys, *ws, xin, gain, dres)


def mix_bwd(dh1, mx, z, ya, yb, g2, w_mix, a_out, b_out, t):
    s = dh1.shape[0]

    def body(dh_ref, mx_ref, ga_ref, gb_ref, ya_ref, yb_ref, g2_ref, wm_ref, ao_ref, bo_ref,
             dmx_ref, dya_ref, dyb_ref, dga_ref, dgb_ref, dsa_ref, dob_ref, gg2_ref):
        @pl.when(pl.program_id(0) == 0)
        def _():
            gg2_ref[...] = jnp.zeros((1, D), F32)

        dmx, dg2 = _rms_bwd(dh_ref[...], mx_ref[...].astype(F32), g2_ref[...])
        gg2_ref[...] += dg2
        dmxb = dmx.astype(BF)
        dmx_ref[...] = dmxb
        dmp = _dot_nt(dmxb, wm_ref[...])

        def gate(g_ref, y_ref, dy_ref, dg_ref, w_ref, dz_ref):
            sg = jax.nn.sigmoid(g_ref[...].astype(F32))
            dyb_ = (dmp * sg).astype(BF)
            dy_ref[...] = dyb_
            dg_ref[...] = (dmp * y_ref[...].astype(F32) * sg * (1.0 - sg)).astype(BF)
            dz_ref[...] = _dot_nt(dyb_, w_ref[...]).astype(BF)

        gate(ga_ref, ya_ref, dya_ref, dga_ref, ao_ref, dsa_ref)
        gate(gb_ref, yb_ref, dyb_ref, dgb_ref, bo_ref, dob_ref)

    row = lambda w: pl.BlockSpec((t, w), lambda i: (i, 0))
    return _pc(
        body, name="mix_bwd", grid=(s // t,),
        in_specs=[row(D), row(D), pl.BlockSpec((t, D), lambda i: (i, 4)), pl.BlockSpec((t, D), lambda i: (i, 5)),
                  row(D), row(D), _const((1, D)), _const((D, D)), _const((A_W, D)), _const((D, D))],
        out_specs=[row(D)] * 5 + [row(A_W), row(D), _acc((1, D))],
        out_shape=[SDS((s, D), BF)] * 5 + [SDS((s, A_W), BF), SDS((s, D), BF), SDS((1, D), F32)],
        compiler_params=_cp(1))(dh1, mx, z, z, ya, yb, g2, w_mix, a_out, b_out)


def sgu_bwd(z, dsa, ln_g, ln_b, w_s, bs_t, t):
    s = z.shape[0]
    nt = s // t

    def body(u_ref, v_ref, dsa_ref, g_ref, b_ref, ws_ref, bs_ref,
             duv_ref, glg_ref, glb_ref, gws_ref, gbs_ref, s_sc, dvn_sc, ds_acc):
        i = pl.program_id(0)

        @pl.when(i == 0)
        def _():
            glg_ref[...] = jnp.zeros((1, A_W), F32)
            glb_ref[...] = jnp.zeros((1, A_W), F32)
            gws_ref[...] = jnp.zeros((A_G, A_C, A_C), F32)
            ds_acc[...] = jnp.zeros((A_C, A_W), F32)

        wm = _sgu_masked(ws_ref)
        rr =lax.broadcasted_iota(jnp.int32, (A_C, A_C), 0)
        cc = lax.broadcasted_iota(jnp.int32, (A_C, A_C), 1)
        tril = cc <= rr
        lng = g_ref[...]
        for ci in range(t // A_C):
            rows = pl.ds(ci * A_C, A_C)
            dgv, rstd, xhat, vn = _sgu_recompute(v_ref[rows, :].astype(F32), lng, b_ref[...])
            for g in range(A_G):
                cols = slice(g * A_GD, (g + 1) * A_GD)
                s_sc[:, cols] = _dot(wm[g], vn[:, cols]) + bs_ref[:, g:g + 1]
            gu, dgu = _gelu_and_grad(u_ref[rows, :].astype(F32))
            dsa_ = dsa_ref[rows, :].astype(F32)
            ds = dsa_ * gu
            ds_acc[...] += ds
            dsb = ds.astype(BF)
            for g in range(A_G):
                cols = slice(g * A_GD, (g + 1) * A_GD)
                gws_ref[g] += jnp.where(tril, _dot_nt(dsb[:, cols], vn[:, cols]), 0.0)
                dvn_sc[:, cols] = _dot_tn(wm[g], dsb[:, cols])
            dvn = dvn_sc[...]
            glb_ref[...] += jnp.sum(dvn, axis=0, keepdims=True)
            glg_ref[...] += jnp.sum(dvn * xhat, axis=0, keepdims=True)
            dxh = dvn * lng
            dgv_ = rstd * (dxh - jnp.mean(dxh, axis=-1, keepdims=True)
                           - xhat * jnp.mean(dxh * xhat, axis=-1, keepdims=True))
            duv_ref[rows, :A_W] = (dsa_ * s_sc[...] * dgu).astype(BF)
            duv_ref[rows, A_W:] = (dgv_ * dgv).astype(BF)

        @pl.when(i == nt - 1)
        def _():
            acc = ds_acc[...]
            for g in range(A_G):
                gbs_ref[:, g:g + 1] = jnp.sum(acc[:, g * A_GD:(g + 1) * A_GD], axis=1, keepdims=True)

    return _pc(
        body, name="sgu_bwd", grid=(nt,),
        in_specs=[pl.BlockSpec((t, A_W), lambda i: (i, 0)), pl.BlockSpec((t, A_W), lambda i: (i, 1)),
                  pl.BlockSpec((t, A_W), lambda i: (i, 0)),
                  _const((1, A_W)), _const((1, A_W)), _const((A_G, A_C, A_C)), _const((A_C, A_G))],
        out_specs=[pl.BlockSpec((t, D), lambda i: (i, 0)), _acc((1, A_W)), _acc((1, A_W)),
                   _acc((A_G, A_C, A_C)), _acc((A_C, A_G))],
        out_shape=[SDS((s, D), BF), SDS((1, A_W), F32), SDS((1, A_W), F32), SDS((A_G, A_C, A_C), F32),
                   SDS((A_C, A_G), F32)],
        scratch_shapes=[pltpu.VMEM((A_C, A_W), F32), pltpu.VMEM((A_C, A_W), F32), pltpu.VMEM((A_C, A_W), F32)],
        compiler_params=_cp(1))(z, z, dsa, ln_g, ln_b, w_s, bs_t)


def gla_bwd(z, qk32, zl, o, dob, states, wgk, bias, wn, ltri, ltri_t, t, side=None):
    s = z.shape[0]
    nt = s // t
    nc = t // B_C

    def body(*refs):
        ins_, s_in, outs_, s_out, scr_, s_scr = _split_side(refs, 12, 6, 5, side)
        qk_ref, v_ref, og_ref, lr_ref, o_ref, dob_ref, st_ref, wgk_ref, bias_ref, wn_ref, l_ref, lt_ref = ins_
        dqk_ref, dv_ref, dog_ref, dpre_ref, gbias_ref, gwn_ref = outs_
        dst_sc, dv_sc, dqd_sc, dkt_sc, ddec_sc = scr_
        i = pl.program_id(0)
        h = pl.program_id(1)

        @pl.when((i == 0) & (h == 0))
        def _():
            gbias_ref[...] = jnp.zeros((B_H, 1, B_HK), F32)
            gwn_ref[...] = jnp.zeros((1, B_HV), F32)
            if side:
                side.start(s_in, s_out, *s_scr)

        @pl.when(i == 0)
        def _():
            dst_sc[h] = jnp.zeros((B_HV, B_HK), F32)

        pre, b, bl, eb, enb, etb, qd, ki, kt = _gla_decays(qk_ref, lr_ref, wgk_ref, bias_ref, l_ref, t)
        qdb, kib, ktb = qd.astype(BF), ki.astype(BF), kt.astype(BF)
        vb = v_ref[...]
        o_ = o_ref[...].astype(F32)
        og = og_ref[...].astype(F32)
        sog = jax.nn.sigmoid(og)
        dob_ = dob_ref[...].astype(F32)
        wn_ = wn_ref[...]
        don = dob_ * og * sog
        do, dwn = _rms_bwd(don, o_, wn_)
        gwn_ref[...] += dwn
        dog_ref[...] = (dob_ * _rms(o_, wn_) * sog * (1.0 + og * (1.0 - sog))).astype(BF)
        dob16 = do.astype(BF)
        keep, keep_t = l_ref[...] > 0, lt_ref[...] > 0
        sc_t = jnp.where(keep_t, _dot_nt(kib, qdb), 0.0).astype(BF)
        dsc = jnp.where(keep, _dot_nt(dob16, vb), 0.0).astype(BF)
        dsc_t = jnp.where(keep_t, _dot_nt(vb, dob16), 0.0).astype(BF)
        dv_sc[...] = _dot(sc_t, dob16)
        dqd_sc[...] = _dot(dsc, kib)
        dki = _dot(dsc_t, qdb)
        for n in reversed(range(nc)):
            rows = slice(n * B_C, (n + 1) * B_C)
            dst = dst_sc[h]
            dstb = dst.astype(BF)
            stp = st_ref[n, 0]
            dv_sc[rows, :] += _dot_nt(ktb[rows], dstb)
            dkt_sc[rows, :] = _dot(vb[rows], dstb)
            dqd_sc[rows, :] += _dot(dob16[rows], stp)
            dec = jnp.exp(bl[n * B_C:n * B_C + 1, :])
            ddec_sc[n] = jnp.sum(dst * stp.astype(F32), axis=0, keepdims=True) * dec
            dst_sc[h] = dst * dec + _dot_tn(dob16[rows], qdb[rows])
        dqd, dkt = dqd_sc[...], dkt_sc[...]
        dv_ref[...] = dv_sc[...].astype(BF)
        dqk_ref[:, :B_HK] = (dqd * eb * (B_HK ** -0.5)).astype(BF)
        dqk_ref[:, B_HK:] = (dki * enb + dkt * etb).astype(BF)
        dktkt = dkt * kt
        db3 = (dqd * qd - dki * ki - dktkt).reshape(nc, B_C, B_HK)
        dbl = jnp.sum(dktkt.reshape(nc, B_C, B_HK), axis=1, keepdims=True) + ddec_sc[...]
        last = lax.broadcasted_iota(jnp.int32, (nc, B_C, B_HK), 1) == B_C - 1
        db = (db3 + jnp.where(last, dbl, 0.0)).reshape(t, B_HK)
        dla = _ldot3(lt_ref[...], db)
        dpre = dla * (1.0 / 16.0) * (1.0 - jax.nn.sigmoid(pre))
        dpre_ref[...] = dpre.astype(BF)
        gbias_ref[h] += jnp.sum(dpre, axis=0, keepdims=True)
        if side:
            @pl.when((i == nt - 1) & (h == B_H - 1))
            def _():
                side.finish(s_in, s_out, *s_scr)

    rv = lambda i: nt - 1 - i
    si_specs, so_specs, so_shapes, s_scratch, s_ins = _side_specs(side)
    return _pc(
        body, name="gla_bwd", grid=(nt, B_H),
        in_specs=[pl.BlockSpec((t, 256), lambda i, h: (rv(i), h)), pl.BlockSpec((t, 256), lambda i, h: (rv(i), 8 + h)),
                  pl.BlockSpec((t, 256), lambda i, h: (rv(i), 12 + h)), pl.BlockSpec((t, LANE), lambda i, h: (rv(i), 0)),
                  pl.BlockSpec((t, B_HV), lambda i, h: (rv(i), h)), pl.BlockSpec((t, B_HV), lambda i, h: (rv(i), h)),
                  pl.BlockSpec((nc, 1, B_HV, B_HK), lambda i, h: (rv(i), h, 0, 0)),
                  pl.BlockSpec((LANE, B_HK), lambda i, h: (0, h)), pl.BlockSpec((1, B_HK), lambda i, h: (0, h)),
                  _const((1, B_HV)), _const((t, t)), _const((t, t))] + si_specs,
        out_specs=[pl.BlockSpec((t, 256), lambda i, h: (rv(i), h)), pl.BlockSpec((t, B_HV), lambda i, h: (rv(i), h)),
                   pl.BlockSpec((t, B_HV), lambda i, h: (rv(i), h)), pl.BlockSpec((t, B_HK), lambda i, h: (rv(i), h)),
                   _acc((B_H, 1, B_HK)), _acc((1, B_HV))] + so_specs,
        out_shape=[SDS((s, D), BF), SDS((s, D), BF), SDS((s, D), BF), SDS((s, B_H * B_HK), BF),
                   SDS((B_H, 1, B_HK), F32), SDS((1, B_HV), F32)] + so_shapes,
        scratch_shapes=[pltpu.VMEM((B_H, B_HV, B_HK), F32), pltpu.VMEM((t, B_HV), F32), pltpu.VMEM((t, B_HK), F32),
                        pltpu.VMEM((t, B_HK), F32), pltpu.VMEM((nc, 1, B_HK), F32)] + s_scratch,
        compiler_params=_cp(2))(qk32, z, z, zl, o, dob, states, wgk, bias, wn, ltri, ltri_t, *s_ins)


def mm_tn(a, b, name, tk=512):
    s, m = a.shape
    n = b.shape[1]
    bn = next(c for c in (1024, 1408, 512, 256, 128) if n % c == 0 and m * c * 4 <= 6 * 1024 * 1024)
    nk = s // tk

    def body(a_ref, b_ref, o_ref, acc):
        k = pl.program_id(1)

        @pl.when(k == 0)
        def _():
            acc[...] = jnp.zeros((m, bn), F32)

        acc[...] += _dot_tn(a_ref[...].astype(BF), b_ref[...])

        @pl.when(k == nk - 1)
        def _():
            o_ref[...] = acc[...].astype(BF)

    return _pc(
        body, name=name, grid=(n // bn, nk),
        in_specs=[pl.BlockSpec((tk, m), lambda j, k: (k, 0)), pl.BlockSpec((tk, bn), lambda j, k: (k, j))],
        out_specs=pl.BlockSpec((m, bn), lambda j, k: (0, j)),
        out_shape=SDS((m, n), BF), scratch_shapes=[pltpu.VMEM((m, bn), F32)], compiler_params=_cp(2))(a, b)


def mm_cols(a, w, blk, width, t, name):
    s, k = a.shape

    def body(a_ref, w_ref, o_ref):
        o_ref[...] = _dot(a_ref[...], w_ref[...]).astype(BF)

    return _pc(body, name=name, grid=(s // t,),
               in_specs=[pl.BlockSpec((t, k), lambda i: (i, 0)),
                         pl.BlockSpec((k, width), lambda i: (0, blk), pipeline_mode=pl.Buffered(1))],
               out_specs=pl.BlockSpec((t, width), lambda i: (i, 0)), out_shape=SDS((s, width), BF),
               compiler_params=_cp(1))(a, w)


def mm_nt_small(a, w, t, name):
    s, k = a.shape
    n = w.shape[0]

    def body(a_ref, w_ref, o_ref):
        o_ref[...] = _dot_nt(a_ref[...], w_ref[...]).astype(BF)

    return _pc(body, name=name, grid=(s // t,),
               in_specs=[pl.BlockSpec((t, k), lambda i: (i, 0)), _const((n, k))],
               out_specs=pl.BlockSpec((t, n), lambda i: (i, 0)), out_shape=SDS((s, n), BF),
               compiler_params=_cp(1))(a, w)


def _adamw(w, g, m, v):
    m = ADAM_B1 * m + (1.0 - ADAM_B1) * g
    v = ADAM_B2 * v + (1.0 - ADAM_B2) * (g * g)
    m_hat = m / (1.0 - ADAM_B1 ** ADAM_STEP)
    v_hat = v / (1.0 - ADAM_B2 ** ADAM_STEP)
    return -ADAM_LR * (m_hat / (jnp.sqrt(v_hat) + ADAM_EPS) + ADAM_WD * w), m, v


def _half_rows(rows):
    rh = rows // 2
    return rh, max(b for b in range(16, 257, 16) if rh % b == 0)


def _pc_sp(body, grid, in_specs, out_specs, out_shape, name):
    gs = pltpu.PrefetchScalarGridSpec(num_scalar_prefetch=1, grid=grid, in_specs=in_specs, out_specs=out_specs)
    return _pc(body, grid_spec=gs, out_shape=out_shape, name=name, compiler_params=_cp(len(grid)))


def adamw_halves(sc, own, sib, w, m, v, name):
    rows, cols = w.shape
    rh, br = _half_rows(rows)
    nbk = rh // br

    def body(sc_ref, own_ref, sib_ref, w_ref, m_ref, v_ref, go_ref, d_ref, mo_ref, vo_ref):
        g_ = jnp.where(pl.program_id(0) // nbk == sc_ref[0], own_ref[...], sib_ref[...])
        go_ref[...] = g_
        d_ref[...], mo_ref[...], vo_ref[...] = _adamw(w_ref[...], g_, m_ref[...], v_ref[...])

    half = pl.BlockSpec((br, cols), lambda i, sc_: (i % nbk, 0))
    blk = pl.BlockSpec((br, cols), lambda i, sc_: (i, 0))
    return _pc_sp(body, (2 * nbk,), [half, half, blk, blk, blk], [blk] * 4, [SDS((rows, cols), F32)] * 4,
                  name)(sc, own, sib, w, m, v)


def adamw_small(g, w, m, v):
    def body(g_ref, w_ref, m_ref, v_ref, d_ref, mo_ref, vo_ref):
        d_ref[...], mo_ref[...], vo_ref[...] = _adamw(w_ref[...], g_ref[...], m_ref[...], v_ref[...])

    vm = pl.BlockSpec(memory_space=pltpu.VMEM)
    return _pc(body, name="adamw_small", in_specs=[vm] * 4, out_specs=[vm] * 3, out_shape=[SDS(g.shape, F32)] * 3,
               compiler_params=pltpu.CompilerParams(vmem_limit_bytes=VMEM_LIMIT))(g, w, m, v)


def _pos():
    return lax.axis_index("x"), lax.axis_index("y"), lax.axis_index("c")


def _other_chips(x, y):
    return [(1 - x, y), (x, 1 - y), (1 - x, 1 - y)]


_ANY = pl.BlockSpec(memory_space=pltpu.HBM)


class _Side:
    def __init__(self, ins, out_shapes, nsem, start, finish):
        self.ins, self.out_shapes, self.start, self.finish = list(ins), list(out_shapes), start, finish
        self.scratch = [pltpu.SemaphoreType.DMA((nsem,)), pltpu.SemaphoreType.DMA((nsem,))]
        self.n_in, self.n_out = len(self.ins), len(self.out_shapes)


def _run_side(side, name):
    def body(*refs):
        args_ = (refs[:side.n_in], refs[side.n_in:side.n_in + side.n_out], *refs[side.n_in + side.n_out:])
        side.start(*args_)
        side.finish(*args_)

    return _pc(body, name=name, in_specs=[_ANY] * side.n_in, out_specs=[_ANY] * side.n_out,
               out_shape=side.out_shapes, scratch_shapes=side.scratch)(*side.ins)


def gather_side(bigs, tinies):
    nb, nt_ = len(bigs), len(tinies)

    def plan(ins, outs, ssem, rsem):
        x, y, c = _pos()
        me = 2 * x + y
        chips = _other_chips(x, y)
        sibling = (x, y, 1 - c)

        def copy(k, src, dst, to):
            return pltpu.make_async_remote_copy(src_ref=src, dst_ref=dst, send_sem=ssem.at[k], recv_sem=rsem.at[k],
                                                device_id=to, device_id_type=MESH)

        sends, landed, passed_on, tiny_landed = [], [], [], []
        for w in range(nb):
            rh = bigs[w].shape[0] // 2
            mine = pl.ds(pl.multiple_of(c * rh, 16), rh)
            theirs = pl.ds(pl.multiple_of((1 - c) * rh, 16), rh)
            for j, (cx, cy) in enumerate(chips):
                sends.append(copy(6 * w + j, ins[w].at[mine], outs[w].at[me, mine], (cx, cy, c)))
                blk = outs[w].at[2 * cx + cy, mine]
                landed.append((copy(6 * w + j, blk, blk, (cx, cy, c)), copy(6 * w + 3 + j, blk, blk, sibling)))
                blk = outs[w].at[2 * cx + cy, theirs]
                passed_on.append(copy(6 * w + 3 + j, blk, blk, sibling))
        for w in range(nt_):
            for j, (cx, cy) in enumerate(chips):
                k = 6 * nb + 3 * w + j
                sends.append(copy(k, ins[nb + w], outs[nb + w].at[me], (cx, cy, c)))
                blk = outs[nb + w].at[2 * cx + cy]
                tiny_landed.append(copy(k, blk, blk, (cx, cy, c)))
        return sends, landed, passed_on, tiny_landed

    def start(ins, outs, ssem, rsem):
        for cp in plan(ins, outs, ssem, rsem)[0]:
            cp.start()

    def finish(ins, outs, ssem, rsem):
        sends, landed, passed_on, tiny_landed = plan(ins, outs, ssem, rsem)
        for arrived, forward in landed:
            arrived.wait_recv()
            forward.start()
        for arrived in tiny_landed + passed_on:
            arrived.wait_recv()
        for cp in sends + [forward for _, forward in landed]:
            cp.wait_send()

    return _Side(list(bigs) + list(tinies), [SDS((4,) + a.shape, a.dtype) for a in list(bigs) + list(tinies)],
                 6 * nb + 3 * nt_, start, finish)


def swap_halves(gs, name):
    n = len(gs)

    def body(*refs):
        g_refs, sib_refs = refs[:n], refs[n:2 * n]
        ssem, rsem = refs[2 * n:]
        x, y, c = _pos()
        cps = []
        for w in range(n):
            rh = gs[w].shape[1] // 2
            give = pl.ds(pl.multiple_of((1 - c) * rh, 16), rh)
            cp = pltpu.make_async_remote_copy(src_ref=g_refs[w].at[:, give], dst_ref=sib_refs[w], send_sem=ssem.at[w],
                                              recv_sem=rsem.at[w], device_id=(x, y, 1 - c), device_id_type=MESH)
            cp.start()
            cps.append(cp)
        for cp in cps:
            cp.wait()

    return _pc(body, name=name, in_specs=[_ANY] * n, out_specs=[_ANY] * n,
               out_shape=[SDS((g.shape[0], g.shape[1] // 2, g.shape[2]), g.dtype) for g in gs],
               scratch_shapes=[pltpu.SemaphoreType.DMA((n,)), pltpu.SemaphoreType.DMA((n,))])(*gs)


def add_half(sc, g, sib, name):
    l, r, cols = g.shape
    rh, br = _half_rows(r)
    nbk = rh // br

    def body(sc_ref, g_ref, s_ref, o_ref):
        o_ref[...] = (g_ref[...].astype(F32) + s_ref[...].astype(F32)).astype(BF)

    blk = pl.BlockSpec((1, br, cols), lambda j, i, sc_: (j, i, 0))
    return _pc_sp(body, (l, nbk), [pl.BlockSpec((1, br, cols), lambda j, i, sc_: (j, sc_[0] * nbk + i, 0)), blk], blk,
                  SDS((l, rh, cols), BF), name)(sc, g, sib)


def exchange_side(ps):
    n_ = len(ps)

    def width(p_):
        return p_.shape[2] if p_.shape[0] == 4 else p_.shape[2] // 4

    def plan(p_refs, got_refs, ssem, rsem):
        x, y, c = _pos()
        cps = []
        for w in range(n_):
            wd = width(ps[w])
            for j, (cx, cy) in enumerate(_other_chips(x, y)):
                to = 2 * cx + cy
                src = p_refs[w].at[to] if ps[w].shape[0] == 4 else p_refs[w].at[0, :, pl.ds(pl.multiple_of(to * wd, LANE), wd)]
                cps.append(pltpu.make_async_remote_copy(
                    src_ref=src, dst_ref=got_refs[w].at[j], send_sem=ssem.at[3 * w + j], recv_sem=rsem.at[3 * w + j],
                    device_id=(cx, cy, c), device_id_type=MESH))
        return cps

    def start(*refs):
        for cp in plan(*refs):
            cp.start()

    def finish(*refs):
        for cp in plan(*refs):
            cp.wait()

    return _Side(ps, [SDS((3, p_.shape[1], width(p_)), p_.dtype) for p_ in ps], 3 * n_, start, finish)


def sum4(sc, p, got, name):
    _, rh, wd = got.shape
    _, br = _half_rows(2 * rh)

    def body(sc_ref, p_ref, g_ref, r_ref):
        r_ref[...] = ((p_ref[0].astype(F32) + g_ref[0].astype(F32)) + (g_ref[1].astype(F32) + g_ref[2].astype(F32)))

    own = (pl.BlockSpec((1, br, wd), lambda i, sc_: (sc_[1], i, 0)) if p.shape[0] == 4
           else pl.BlockSpec((1, br, wd), lambda i, sc_: (0, i, sc_[1])))
    return _pc_sp(body, (rh // br,), [own, pl.BlockSpec((3, br, wd), lambda i, sc_: (0, i, 0))],
                  pl.BlockSpec((br, wd), lambda i, sc_: (i, 0)), SDS((rh, wd), F32), name)(sc, p, got)


def join_halves(halves):
    n = len(halves)

    def body(*refs):
        h_refs, got_refs = refs[:n], refs[n:2 * n]
        ssem, rsem = refs[2 * n:]
        x, y, c = _pos()
        cps = []
        for w in range(n):
            cp = pltpu.make_async_remote_copy(src_ref=h_refs[w], dst_ref=got_refs[w], send_sem=ssem.at[w],
                                              recv_sem=rsem.at[w], device_id=(x, y, 1 - c), device_id_type=MESH)
            cp.start()
            cps.append(cp)
        for cp in cps:
            cp.wait()

    return _pc(body, name="join_halves", in_specs=[_ANY] * n, out_specs=[_ANY] * n,
               out_shape=[SDS(h.shape, h.dtype) for h in halves],
               scratch_shapes=[pltpu.SemaphoreType.DMA((n,)), pltpu.SemaphoreType.DMA((n,))])(*halves)


def allreduce_small(g):
    rows = g.shape[0]

    def body(g_ref, go_ref, buf, ssem, rsem):
        x, y, c = _pos()
        me = 4 * x + 2 * y + c
        flip = lambda a, f: 1 - a if f else a
        peers = [(flip(x, k & 4), flip(y, k & 2), flip(c, k & 1)) for k in range(1, 8)]
        cps = []
        for k, peer in enumerate(peers):
            cp = pltpu.make_async_remote_copy(src_ref=g_ref, dst_ref=buf.at[me], send_sem=ssem.at[k], recv_sem=rsem.at[k],
                                              device_id=peer, device_id_type=MESH)
            cp.start()
            cps.append(cp)
        buf[me] = g_ref[...]
        for k, (px, py, pc_) in enumerate(peers):
            pltpu.make_async_remote_copy(src_ref=g_ref, dst_ref=buf.at[4 * px + 2 * py + pc_], send_sem=ssem.at[k],
                                         recv_sem=rsem.at[k], device_id=(px, py, pc_), device_id_type=MESH).wait_recv()
        for cp in cps:
            cp.wait_send()
        tot = buf[0]
        for dev in range(1, 8):
            tot = tot + buf[dev]
        go_ref[...] = tot

    vm = pl.BlockSpec(memory_space=pltpu.VMEM)
    return _pc(body, name="allreduce_small", in_specs=[vm], out_specs=vm, out_shape=SDS((rows, LANE), F32),
               scratch_shapes=[pltpu.VMEM((8, rows, LANE), F32), pltpu.SemaphoreType.DMA((7,)),
                               pltpu.SemaphoreType.DMA((7,))],
               compiler_params=pltpu.CompilerParams(vmem_limit_bytes=VMEM_LIMIT))(g)


def _pack_small(entries, get):
    flat = jnp.concatenate([get(n).reshape(-1).astype(F32) for n, _ in entries])
    rows = -(-flat.shape[0] // (8 * LANE)) * 8
    return jnp.pad(flat, (0, rows * LANE - flat.shape[0])).reshape(rows, LANE)


def _unpack_small(entries, packed):
    out, off = {}, 0
    flat = packed.reshape(-1)
    for name, n in entries:
        out[name] = flat[off:off + n]
        off += n
    return out


def _cols_full(blk):
    return blk.transpose(1, 0, 2).reshape(blk.shape[1], 4 * blk.shape[2])


def kernel(x, p, pre_mix_norm, w_in, a_ln_g, a_ln_b, a_spatial_w, a_spatial_b, a_out, b_gk, b_gk_bias, b_out_norm, b_out, w_mix_out, post_mix_norm, pre_ffn_norm, w_up, conv_w, conv_b, w_down, post_ffn_norm, w_ple, w_ple_gate, post_ple_norm, loss_target, m_pre_mix_norm, m_w_in, m_a_ln_g, m_a_ln_b, m_a_spatial_w, m_a_spatial_b, m_a_out, m_b_gk, m_b_gk_bias, m_b_out_norm, m_b_out, m_w_mix_out, m_post_mix_norm, m_pre_ffn_norm, m_w_up, m_conv_w, m_conv_b, m_w_down, m_post_ffn_norm, m_w_ple, m_w_ple_gate, m_post_ple_norm, v_pre_mix_norm, v_w_in, v_a_ln_g, v_a_ln_b, v_a_spatial_w, v_a_spatial_b, v_a_out, v_b_gk, v_b_gk_bias, v_b_out_norm, v_b_out, v_w_mix_out, v_post_mix_norm, v_pre_ffn_norm, v_w_up, v_conv_w, v_conv_b, v_w_down, v_post_ffn_norm, v_w_ple, v_w_ple_gate, v_post_ple_norm):
    args = dict(locals())
    order = ['pre_mix_norm', 'w_in', 'a_ln_g', 'a_ln_b', 'a_spatial_w', 'a_spatial_b', 'a_out', 'b_gk', 'b_gk_bias',
             'b_out_norm', 'b_out', 'w_mix_out', 'post_mix_norm', 'pre_ffn_norm', 'w_up', 'conv_w', 'conv_b', 'w_down',
             'post_ffn_norm', 'w_ple', 'w_ple_gate', 'post_ple_norm']
    assert sorted(BIG + TINY + tuple(n for n, _ in SMALL)) == sorted(order)
    s = x.shape[1]
    xs = x.reshape(s, D)
    ps = p.reshape(s, PLE)
    tgt = loss_target.reshape(s, D)
    t_big = min(1024, s)
    t_mid = min(512, s)
    t_small = min(256, s)
    mx_, my_, mc_ = _pos()
    me = 2 * mx_ + my_
    sc = jnp.stack([mc_, me]).astype(jnp.int32)
    shard = lambda n: args[n].reshape(args[n].shape[1:])

    mine = {n: shard(n).astype(BF) for n in BIG}
    mine.update({n: shard(n) for n in TINY})
    fill = lambda names, gots: {n: lax.dynamic_update_slice(got, mine[n][None], (me, 0, 0)) for n, got in zip(names, gots)}
    first = ("w_in",) + TINY
    full = fill(first, _run_side(gather_side([mine["w_in"]], [mine[n] for n in TINY]), "gather_first"))
    wi = _cols_full(full["w_in"])
    seg = lambda a, b: wi[:, a:b]
    qk = [seg(1024 + h * B_HK, 1024 + (h + 1) * B_HK) for h in range(B_H)]
    kk = [seg(1536 + h * B_HK, 1536 + (h + 1) * B_HK) for h in range(B_H)]
    w_z = jnp.concatenate([seg(0, 1024)] + [m_ for h in range(B_H) for m_ in (qk[h], kk[h])]
                          + [seg(2048, 4096), seg(4112, 6160), seg(4096, 4112), jnp.zeros((D, LANE - B_RANK), BF)], axis=1)
    wgk = jnp.pad(_cols_full(full["b_gk"]).astype(BF), ((0, LANE - B_RANK), (0, 0)))
    w_conv = _cols_full(full["conv_w"])
    g1, g2, g3 = pre_mix_norm.reshape(1, D), post_mix_norm.reshape(1, D), pre_ffn_norm.reshape(1, D)
    g4, g5 = post_ffn_norm.reshape(1, D), post_ple_norm.reshape(1, D)
    ln_g, ln_b = a_ln_g.reshape(1, A_W), a_ln_b.reshape(1, A_W)
    w_s = a_spatial_w.reshape(A_G, A_C, A_C)
    bs_t = a_spatial_b.reshape(A_G, A_C).T
    gk_bias = b_gk_bias.reshape(1, B_H * B_HK)
    wn = b_out_norm.reshape(1, B_HV)
    cb = conv_b.reshape(1, 2 * D_FF)
    idx = jnp.arange(t_mid)
    ltri = ((idx[:, None] // B_C == idx[None, :] // B_C) & (idx[None, :] <= idx[:, None])).astype(BF)

    a, z, qk32, *gots = norm_matmul(xs, g1, w_z, D, t_big, "in_proj", nblk=6, f32_blk=1,
                                    side=gather_side([mine[n] for n in BIG[1:]], []))
    full.update(fill(BIG[1:], gots))
    w_aout, w_ple_f = _cols_full(full["a_out"]), _cols_full(full["w_ple"])
    w_bout, w_mix, w_pg = (full[n].reshape(D, D) for n in ("b_out", "w_mix_out", "w_ple_gate"))
    w_dn, w_up3 = full["w_down"].reshape(D_FF, D), full["w_up"]
    zl = mm_cols(a, w_z, 48, LANE, t_big, "lr_proj")
    sa = sgu_fwd(z, ln_g, ln_b, w_s, bs_t, t_mid)
    ob, o, states = gla_fwd(z, qk32, zl, wgk, gk_bias, wn, ltri, t_mid)
    ya, yb, mp, mx, h1 = mix_fwd(sa, ob, z, xs, w_aout, w_bout, w_mix, g2, t_small)
    c, up = norm_matmul(h1, g3, w_up3, 1408, t_big, "up_proj")
    ff = ffn_gate_fwd(up, w_conv, cb, t_small)
    f, h2, pg, pe, dy, loss = out_fwd(ff, h1, ps, tgt, w_dn, w_pg, w_ple_f, g4, g5, t_small)

    dh2, dpe, dpg, df, dff, gg5, gg4 = out_bwd(dy, pg, pe, f, g5, g4, w_pg, w_dn, t_small)
    dup_g, dup_v, gcw_g, gcw_v, gcb_g, gcb_v = ffn_gate_bwd(up, dff, w_conv, cb, t_small)
    wu = lambda k: pl.BlockSpec((None, D, 1408), lambda i: (k, 0, 0), pipeline_mode=pl.Buffered(1))
    dh1, gg3 = nt_normbwd([(dup_g, 0, w_up3, wu(0)), (dup_g, 1, w_up3, wu(1)), (dup_v, 0, w_up3, wu(2)),
                           (dup_v, 1, w_up3, wu(3))], h1, g3, dh2, t_small, "up_bwd")
    dmx, dya, dyb, dga, dgb, dsa, dob, gg2 = mix_bwd(dh1, mx, z, ya, yb, g2, w_mix, w_aout, w_bout, t_small)
    duv, g_lng, g_lnb, g_ws, g_bst = sgu_bwd(z, dsa, ln_g, ln_b, w_s, bs_t, t_mid)

    grads = {
        "a_out": mm_tn(sa, dya, "dw_a_out")[None],
        "b_out": mm_tn(ob, dyb, "dw_b_out").reshape(4, D // 4, D),
        "w_mix_out": mm_tn(mp, dmx, "dw_mix").reshape(4, D // 4, D),
        "w_up": jnp.concatenate([mm_tn(c, dup_g, "dw_up_g"), mm_tn(c, dup_v, "dw_up_v")], axis=1)[None],
        "w_down": mm_tn(ff, df, "dw_down").reshape(4, D_FF // 4, D),
        "w_ple": mm_tn(ps, dpe, "dw_ple")[None],
        "w_ple_gate": mm_tn(h2, dpg, "dw_ple_gate").reshape(4, D // 4, D),
    }

    def chip_partials(names):
        gs = [grads[n] for n in names]
        return [add_half(sc, g, sib, "partial_" + n)
                for n, g, sib in zip(names, gs, swap_halves(gs, "swap_halves_" + names[0]))]

    parts = dict(zip(BIG[1:], chip_partials(BIG[1:])))
    dqk, dvb, dog, dpre, g_gkb, g_wn, *gots = gla_bwd(z, qk32, zl, o, dob, states, wgk, gk_bias, wn, ltri, ltri.T,
                                                      t_mid, side=exchange_side([parts[n] for n in BIG[1:]]))
    got = dict(zip(BIG[1:], gots))
    dlr = mm_nt_small(dpre, wgk, t_mid, "dlr")
    segs = [duv, dqk, dvb, dog, dga, dgb, dlr]
    wz = lambda k, width: pl.BlockSpec((D, width), lambda i: (0, k), pipeline_mode=pl.Buffered(1))
    dx, gg1 = nt_normbwd([(segs[k], 0, w_z, wz(k, D)) for k in range(6)] + [(dlr, 0, w_z, wz(48, LANE))],
                         xs, g1, dh1, t_small, "in_bwd")

    gz = [mm_tn(a, sg_, "dw_in_%d" % k) for k, sg_ in enumerate(segs)]
    gq = [gz[1][:, h * 256:h * 256 + B_HK] for h in range(B_H)]
    gk = [gz[1][:, h * 256 + B_HK:(h + 1) * 256] for h in range(B_H)]
    g_in = jnp.concatenate([gz[0]] + gq + gk + [gz[2], gz[3], gz[6][:, :B_RANK], gz[4], gz[5]], axis=1)
    grads["w_in"] = g_in.reshape(D, 4, 1540).transpose(1, 0, 2)
    parts["w_in"], = chip_partials(("w_in",))
    got["w_in"], = _run_side(exchange_side([parts["w_in"]]), "exchange_w_in")

    reds = [sum4(sc, parts[n], got[n], "sum_" + n) for n in BIG]
    outs = {}
    for n, red, sib in zip(BIG, reds, join_halves(reds)):
        res = adamw_halves(sc, red, sib, shard(n), shard("m_" + n), shard("v_" + n), "adamw_" + n)
        outs[n] = [r_.reshape(args[n].shape) for r_ in res]

    small_g = {
        "pre_mix_norm": gg1, "a_ln_g": g_lng, "a_ln_b": g_lnb, "a_spatial_w": g_ws, "a_spatial_b": g_bst.T,
        "b_gk_bias": g_gkb, "b_out_norm": g_wn, "post_mix_norm": gg2, "pre_ffn_norm": gg3,
        "conv_b": jnp.concatenate([gcb_g, gcb_v], axis=1), "post_ffn_norm": gg4, "post_ple_norm": gg5,
        "b_gk": mm_tn(zl, dpre, "dw_gk")[:B_RANK], "conv_w": jnp.concatenate([gcw_g, gcw_v], axis=1),
    }
    red_entries = SMALL + (("b_gk", B_RANK * 512), ("conv_w", 3 * 2 * D_FF))
    g_fin = _unpack_small(red_entries, allreduce_small(_pack_small(red_entries, lambda n: small_g[n])))
    g_fin["b_gk"] = lax.dynamic_slice(g_fin["b_gk"].reshape(B_RANK, 512), (0, me * B_HK), (B_RANK, B_HK))
    g_fin["conv_w"] = lax.dynamic_slice(g_fin["conv_w"].reshape(3, 2 * D_FF), (0, me * 1408), (3, 1408))
    upd_entries = SMALL + (("b_gk", B_RANK * B_HK), ("conv_w", 3 * 1408))
    res = adamw_small(*[_pack_small(upd_entries, get) for get in
                        (lambda n: g_fin[n], lambda n: args[n], lambda n: args["m_" + n], lambda n: args["v_" + n])])
    res = [_unpack_small(upd_entries, r_) for r_ in res]
    for n, _ in upd_entries:
        outs[n] = [r_[n].reshape(args[n].shape) for r_ in [g_fin] + res]

    total = lax.psum(loss[0, 0], ("x", "y", "c"))
    return (total, dx.reshape(x.shape), *[outs[n][0] for n in order], *[outs[n][1] for n in order],
            *[outs[n][2] for n in order], *[outs[n][3] for n in order])
```

```python
import functools
import math

import jax
import jax.numpy as jnp
from jax import lax
from jax.experimental import pallas as pl
from jax.experimental.pallas import tpu as pltpu

F32 = jnp.float32
BF = jnp.bfloat16
SDS = jax.ShapeDtypeStruct
MESH = pl.DeviceIdType.MESH

EPS = 1e-6
D = 1024
A_W = 512
A_G, A_C = 8, 128
A_GD = A_W // A_G
B_H, B_HK, B_HV = 4, 128, 256
B_C = 64
B_RANK = 16
D_FF = 2816
PLE = 256
ZW = 6272
LANE = 128
VMEM_LIMIT = 60 * 1024 * 1024

ADAM_LR, ADAM_B1, ADAM_B2, ADAM_EPS, ADAM_WD, ADAM_STEP = 0.001, 0.9, 0.999, 1e-08, 0.01, 10

_GC = math.sqrt(2.0 / math.pi)
_GA = 0.044715

BIG = ("w_in", "a_out", "b_out", "w_mix_out", "w_up", "w_down", "w_ple", "w_ple_gate")
TINY = ("b_gk", "conv_w")
SMALL = (("pre_mix_norm", 1024), ("a_ln_g", 512), ("a_ln_b", 512), ("a_spatial_w", 131072),
         ("a_spatial_b", 1024), ("b_gk_bias", 512), ("b_out_norm", 256), ("post_mix_norm", 1024),
         ("pre_ffn_norm", 1024), ("conv_b", 5632), ("post_ffn_norm", 1024), ("post_ple_norm", 1024))


def _pc(body, **kw):
    return pl.pallas_call(body, **kw)


def _cp(n):
    return pltpu.CompilerParams(dimension_semantics=("arbitrary",) * n, vmem_limit_bytes=VMEM_LIMIT)


def _const(shape):
    nd = len(shape)
    return pl.BlockSpec(shape, lambda *_: (0,) * nd, pipeline_mode=pl.Buffered(1))


def _acc(shape):
    nd = len(shape)
    return pl.BlockSpec(shape, lambda *_: (0,) * nd)


def _dot(a, b):
    return jnp.dot(a, b, preferred_element_type=F32)


def _dot_nt(a, b):
    return lax.dot_general(a, b, (((1,), (1,)), ((), ())), preferred_element_type=F32)


def _dot_tn(a, b):
    return lax.dot_general(a, b, (((0,), (0,)), ((), ())), preferred_element_type=F32)


def _gelu(x):
    return 0.5 * x * (1.0 + jnp.tanh(_GC * (x + _GA * x * x * x)))


def _gelu_and_grad(x):
    x2 = x * x
    s = 0.5 * jnp.tanh((_GC * x) * (1.0 + _GA * x2)) + 0.5
    g = x * s
    return g, s + g * (1.0 - s) * ((6.0 * _GC * _GA) * x2 + 2.0 * _GC)


def _log_sigmoid(x):
    return jnp.minimum(x, 0.0) - jnp.log(1.0 + jnp.exp(-jnp.abs(x)))


def _rms(x, g):
    return x * lax.rsqrt(jnp.mean(x * x, axis=-1, keepdims=True) + EPS) * g


def _rms_bwd(dy, x, g):
    r = lax.rsqrt(jnp.mean(x * x, axis=-1, keepdims=True) + EPS)
    n = x * r
    dn = dy * g
    dx = r * (dn - n * jnp.mean(dn * n, axis=-1, keepdims=True))
    return dx, jnp.sum(dy * n, axis=0, keepdims=True)


def _ldot3(l, x):
    h = x.astype(BF)
    r = x - h.astype(F32)
    m = r.astype(BF)
    lo = (r - m.astype(F32)).astype(BF)
    return _dot(l, h) + _dot(l, m) + _dot(l, lo)


def _split_side(refs, n_in, n_out, n_scratch, side):
    si, so = (side.n_in, side.n_out) if side else (0, 0)
    cuts = [n_in, si, n_out, so, n_scratch]
    out, at = [], 0
    for c in cuts:
        out.append(refs[at:at + c])
        at += c
    return (*out, refs[at:])


def _side_specs(side):
    return ([_ANY] * side.n_in, [_ANY] * side.n_out, side.out_shapes, side.scratch, side.ins) if side else ([],) * 5


def norm_matmul(x, g, w, bn, t, name, nblk=None, f32_blk=None, side=None):
    s, dm = x.shape
    if w.ndim == 3:
        nblk = w.shape[0]
        w_spec = pl.BlockSpec((None, dm, bn), lambda i, j: (j, 0, 0))
    else:
        nblk = nblk or w.shape[1] // bn
        w_spec = pl.BlockSpec((dm, bn), lambda i, j: (0, j))
    extra = f32_blk is not None
    nt = s // t

    def body(*refs):
        (x_ref, g_ref, w_ref), s_in, outs, s_out, (a_sc,), s_scr = _split_side(refs, 3, 2 + extra, 1, side)
        a_ref, z_ref = outs[:2]
        i, j = pl.program_id(0), pl.program_id(1)
        if side:
            @pl.when((i == 0) & (j == 0))
            def _():
                side.start(s_in, s_out, *s_scr)

        @pl.when(j == 0)
        def _():
            a = _rms(x_ref[...], g_ref[...]).astype(BF)
            a_sc[...] = a
            a_ref[...] = a

        acc = _dot(a_sc[...], w_ref[...])
        z_ref[...] = acc.astype(BF)
        if extra:
            @pl.when(j == f32_blk)
            def _():
                outs[2][...] = acc
        if side:
            @pl.when((i == nt - 1) & (j == nblk - 1))
            def _():
                side.finish(s_in, s_out, *s_scr)

    si_specs, so_specs, so_shapes, s_scratch, s_ins = _side_specs(side)
    return _pc(
        body, name=name, grid=(nt, nblk),
        in_specs=[pl.BlockSpec((t, dm), lambda i, j: (i, 0)), _const((1, dm)), w_spec] + si_specs,
        out_specs=[pl.BlockSpec((t, dm), lambda i, j: (i, 0)), pl.BlockSpec((t, bn), lambda i, j: (i, j))]
        + [pl.BlockSpec((t, bn), lambda i, j: (i, 0))] * extra + so_specs,
        out_shape=[SDS((s, dm), BF), SDS((s, nblk * bn), BF)] + [SDS((s, bn), F32)] * extra + so_shapes,
        scratch_shapes=[pltpu.VMEM((t, dm), BF)] + s_scratch, compiler_params=_cp(2))(x, g, w, *s_ins)


def _sgu_masked(ws_ref):
    r = lax.broadcasted_iota(jnp.int32, (A_C, A_C), 0)
    c = lax.broadcasted_iota(jnp.int32, (A_C, A_C), 1)
    return [jnp.where(c <= r, ws_ref[g], 0.0).astype(BF) for g in range(A_G)]


def _sgu_recompute(v, lng, lnb):
    gv, dgv = _gelu_and_grad(v)
    mu = jnp.mean(gv, axis=-1, keepdims=True)
    xc = gv - mu
    rstd = lax.rsqrt(jnp.mean(xc * xc, axis=-1, keepdims=True) + EPS)
    xhat = xc * rstd
    return dgv, rstd, xhat, (xhat * lng + lnb).astype(BF)


def sgu_fwd(z, ln_g, ln_b, w_s, bs_t, t):
    s = z.shape[0]

    def body(u_ref, v_ref, g_ref, b_ref, ws_ref, bs_ref, sa_ref, s_sc):
        wm = _sgu_masked(ws_ref)
        for ci in range(t // A_C):
            rows = pl.ds(ci * A_C, A_C)
            _, _, _, vn = _sgu_recompute(v_ref[rows, :].astype(F32), g_ref[...], b_ref[...])
            for g in range(A_G):
                cols = slice(g * A_GD, (g + 1) * A_GD)
                s_sc[:, cols] = _dot(wm[g], vn[:, cols]) + bs_ref[:, g:g + 1]
            sa_ref[rows, :] = (_gelu(u_ref[rows, :].astype(F32)) * s_sc[...]).astype(BF)

    return _pc(
        body, name="sgu_fwd", grid=(s // t,),
        in_specs=[pl.BlockSpec((t, A_W), lambda i: (i, 0)), pl.BlockSpec((t, A_W), lambda i: (i, 1)),
                  _const((1, A_W)), _const((1, A_W)), _const((A_G, A_C, A_C)), _const((A_C, A_G))],
        out_specs=pl.BlockSpec((t, A_W), lambda i: (i, 0)),
        out_shape=SDS((s, A_W), BF),
        scratch_shapes=[pltpu.VMEM((A_C, A_W), F32)], compiler_params=_cp(1))(z, z, ln_g, ln_b, w_s, bs_t)


def _gla_decays(qk_ref, lr_ref, wgk_ref, bias_ref, l_ref, t):
    nc = t // B_C
    q = qk_ref[:, :B_HK].astype(F32) * (B_HK ** -0.5)
    k = qk_ref[:, B_HK:].astype(F32)
    pre = _dot(lr_ref[...], wgk_ref[...]) + bias_ref[...]
    la = _log_sigmoid(pre) * (1.0 / 16.0)
    b = _ldot3(l_ref[...], la)
    b3 = b.reshape(nc, B_C, B_HK)
    bl = jnp.broadcast_to(b3[:, B_C - 1:B_C, :], (nc, B_C, B_HK)).reshape(t, B_HK)
    eb, enb, etb = jnp.exp(b), jnp.exp(-b), jnp.exp(bl - b)
    return pre, b, bl, eb, enb, etb, q * eb, k * enb, k * etb


def gla_fwd(z, qk32, zl, wgk, bias, wn, ltri, t):
    s = z.shape[0]
    nc = t // B_C

    def body(qk_ref, v_ref, og_ref, lr_ref, wgk_ref, bias_ref, wn_ref, l_ref, ob_ref, o_ref, st_ref, st_sc, o_sc):
        h = pl.program_id(1)

        @pl.when(pl.program_id(0) == 0)
        def _():
            st_sc[h] = jnp.zeros((B_HV, B_HK), F32)

        _, _, bl, _, _, _, qd, ki, kt = _gla_decays(qk_ref, lr_ref, wgk_ref, bias_ref, l_ref, t)
        qd, ki, kt = qd.astype(BF), ki.astype(BF), kt.astype(BF)
        vb = v_ref[...]
        sc = jnp.where(l_ref[...] > 0, _dot_nt(qd, ki), 0.0).astype(BF)
        o_sc[...] = _dot(sc, vb)
        for n in range(nc):
            rows = slice(n * B_C, (n + 1) * B_C)
            st = st_sc[h]
            stb = st.astype(BF)
            st_ref[n, 0] = stb
            o_sc[rows, :] += _dot_nt(qd[rows], stb)
            st_sc[h] = st * jnp.exp(bl[n * B_C:n * B_C + 1, :]) + _dot_tn(vb[rows], kt[rows])
        ob = o_sc[...].astype(BF)
        o_ref[...] = ob
        og = og_ref[...].astype(F32)
        ob_ref[...] = (_rms(ob.astype(F32), wn_ref[...]) * og * jax.nn.sigmoid(og)).astype(BF)

    return _pc(
        body, name="gla_fwd", grid=(s // t, B_H),
        in_specs=[pl.BlockSpec((t, 256), lambda i, h: (i, h)), pl.BlockSpec((t, 256), lambda i, h: (i, 8 + h)),
                  pl.BlockSpec((t, 256), lambda i, h: (i, 12 + h)), pl.BlockSpec((t, LANE), lambda i, h: (i, 0)),
                  pl.BlockSpec((LANE, B_HK), lambda i, h: (0, h)), pl.BlockSpec((1, B_HK), lambda i, h: (0, h)),
                  _const((1, B_HV)), _const((t, t))],
        out_specs=[pl.BlockSpec((t, B_HV), lambda i, h: (i, h)), pl.BlockSpec((t, B_HV), lambda i, h: (i, h)),
                   pl.BlockSpec((nc, 1, B_HV, B_HK), lambda i, h: (i, h, 0, 0))],
        out_shape=[SDS((s, D), BF), SDS((s, D), BF), SDS((s // B_C, B_H, B_HV, B_HK), BF)],
        scratch_shapes=[pltpu.VMEM((B_H, B_HV, B_HK), F32), pltpu.VMEM((t, B_HV), F32)],
        compiler_params=_cp(2))(qk32, z, z, zl, wgk, bias, wn, ltri)


def mix_fwd(sa, ob, z, x, a_out, b_out, w_mix, g2, t):
    s = x.shape[0]

    def body(sa_ref, ob_ref, ga_ref, gb_ref, x_ref, ao_ref, bo_ref, wm_ref, g2_ref,
             ya_ref, yb_ref, mp_ref, mx_ref, h1_ref):
        ya = _dot(sa_ref[...], ao_ref[...]).astype(BF)
        yb = _dot(ob_ref[...], bo_ref[...]).astype(BF)
        ya_ref[...] = ya
        yb_ref[...] = yb
        mp = (jax.nn.sigmoid(ga_ref[...].astype(F32)) * ya.astype(F32)
              + jax.nn.sigmoid(gb_ref[...].astype(F32)) * yb.astype(F32)).astype(BF)
        mp_ref[...] = mp
        mx = _dot(mp, wm_ref[...]).astype(BF)
        mx_ref[...] = mx
        h1_ref[...] = x_ref[...] + _rms(mx.astype(F32), g2_ref[...])

    row = lambda w: pl.BlockSpec((t, w), lambda i: (i, 0))
    return _pc(
        body, name="mix_fwd", grid=(s // t,),
        in_specs=[row(A_W), row(D), pl.BlockSpec((t, D), lambda i: (i, 4)), pl.BlockSpec((t, D), lambda i: (i, 5)),
                  row(D), _const((A_W, D)), _const((D, D)), _const((D, D)), _const((1, D))],
        out_specs=[row(D)] * 5,
        out_shape=[SDS((s, D), BF)] * 4 + [SDS((s, D), F32)],
        compiler_params=_cp(1))(sa, ob, z, z, x, a_out, b_out, w_mix, g2)


def ffn_gate_fwd(up, conv_w, conv_b, t):
    s = up.shape[0]
    bn = 1408
    hb = t // 8

    def body(ug_ref, uv_ref, hg_ref, hv_ref, wg_ref, wv_ref, bg_ref, bv_ref, ff_ref, cg_ref, cv_ref, ext):
        live = (pl.program_id(1) > 0).astype(F32)

        def branch(u_ref, h_ref, w_ref, b_ref, c_ref):
            ext[0:8, :] = h_ref[...].astype(F32) * live
            ext[8:, :] = u_ref[...].astype(F32)
            w = w_ref[...]
            c = (b_ref[...] + w[0:1] * ext[pl.ds(6, t), :] + w[1:2] * ext[pl.ds(7, t), :]
                 + w[2:3] * ext[pl.ds(8, t), :]).astype(BF)
            c_ref[...] = c
            return c.astype(F32)

        g = _gelu(branch(ug_ref, hg_ref, wg_ref, bg_ref, cg_ref))
        ff_ref[...] = (g * branch(uv_ref, hv_ref, wv_ref, bv_ref, cv_ref)).astype(BF)

    halo = lambda off: pl.BlockSpec((8, bn), lambda j, i: (jnp.maximum(i * hb - 1, 0), j + off))
    out = pl.BlockSpec((t, bn), lambda j, i: (i, j))
    return _pc(
        body, name="ffn_gate_fwd", grid=(2, s // t),
        in_specs=[pl.BlockSpec((t, bn), lambda j, i: (i, j)), pl.BlockSpec((t, bn), lambda j, i: (i, j + 2)),
                  halo(0), halo(2),
                  pl.BlockSpec((3, bn), lambda j, i: (0, j)), pl.BlockSpec((3, bn), lambda j, i: (0, j + 2)),
                  pl.BlockSpec((1, bn), lambda j, i: (0, j)), pl.BlockSpec((1, bn), lambda j, i: (0, j + 2))],
        out_specs=[out] * 3, out_shape=[SDS((s, D_FF), BF)] * 3,
        scratch_shapes=[pltpu.VMEM((t + 8, bn), F32)],
        compiler_params=_cp(2))(up, up, up, up, conv_w, conv_w, conv_b, conv_b)


def out_fwd(ff, h1, p, tgt, w_down, w_pg, w_ple, g4, g5, t):
    s = h1.shape[0]

    def body(ff_ref, h1_ref, p_ref, t_ref, wd_ref, wpg_ref, wpl_ref, g4_ref, g5_ref,
             f_ref, h2_ref, pg_ref, pe_ref, dy_ref, loss_ref):
        @pl.when(pl.program_id(0) == 0)
        def _():
            loss_ref[...] = jnp.zeros((1, 1), F32)

        f = _dot(ff_ref[...], wd_ref[...]).astype(BF)
        f_ref[...] = f
        h2 = h1_ref[...] + _rms(f.astype(F32), g4_ref[...])
        h2b = h2.astype(BF)
        h2_ref[...] = h2b
        pg = _dot(h2b, wpg_ref[...]).astype(BF)
        pe = _dot(p_ref[...].astype(BF), wpl_ref[...]).astype(BF)
        pg_ref[...] = pg
        pe_ref[...] = pe
        y = h2 + _rms(jax.nn.sigmoid(pg.astype(F32)) * pe.astype(F32), g5_ref[...])
        err = y - t_ref[...]
        dy_ref[...] = err * (1.0 / D)
        loss_ref[...] += (0.5 / D) * jnp.sum(err * err)

    row = lambda w: pl.BlockSpec((t, w), lambda i: (i, 0))
    return _pc(
        body, name="out_fwd", grid=(s // t,),
        in_specs=[row(D_FF), row(D), row(PLE), row(D), _const((D_FF, D)), _const((D, D)), _const((PLE, D)),
                  _const((1, D)), _const((1, D))],
        out_specs=[row(D)] * 5 + [_acc((1, 1))],
        out_shape=[SDS((s, D), BF)] * 4 + [SDS((s, D), F32), SDS((1, 1), F32)],
        compiler_params=_cp(1))(ff, h1, p, tgt, w_down, w_pg, w_ple, g4, g5)


def out_bwd(dy, pg, pe, f, g5, g4, w_pg, w_down, t):
    s = dy.shape[0]

    def body(dy_ref, pg_ref, pe_ref, f_ref, g5_ref, g4_ref, wpg_ref, wd_ref,
             dh2_ref, dpe_ref, dpg_ref, df_ref, dff_ref, gg5_ref, gg4_ref):
        @pl.when(pl.program_id(0) == 0)
        def _():
            gg5_ref[...] = jnp.zeros((1, D), F32)
            gg4_ref[...] = jnp.zeros((1, D), F32)

        dy_ = dy_ref[...]
        pg_ = pg_ref[...].astype(F32)
        pe_ = pe_ref[...].astype(F32)
        sg = jax.nn.sigmoid(pg_)
        dple, dg5 = _rms_bwd(dy_, sg * pe_, g5_ref[...])
        gg5_ref[...] += dg5
        dpe_ref[...] = (dple * sg).astype(BF)
        dpg = (dple * pe_ * sg * (1.0 - sg)).astype(BF)
        dpg_ref[...] = dpg
        dh2 = dy_ + _dot_nt(dpg, wpg_ref[...])
        dh2_ref[...] = dh2
        df, dg4 = _rms_bwd(dh2, f_ref[...].astype(F32), g4_ref[...])
        gg4_ref[...] += dg4
        dfb = df.astype(BF)
        df_ref[...] = dfb
        dff_ref[...] = _dot_nt(dfb, wd_ref[...]).astype(BF)

    row = lambda w: pl.BlockSpec((t, w), lambda i: (i, 0))
    return _pc(
        body, name="out_bwd", grid=(s // t,),
        in_specs=[row(D), row(D), row(D), row(D), _const((1, D)), _const((1, D)), _const((D, D)), _const((D_FF, D))],
        out_specs=[row(D), row(D), row(D), row(D), row(D_FF), _acc((1, D)), _acc((1, D))],
        out_shape=[SDS((s, D), F32), SDS((s, D), BF), SDS((s, D), BF), SDS((s, D), BF), SDS((s, D_FF), BF),
                   SDS((1, D), F32), SDS((1, D), F32)],
        compiler_params=_cp(1))(dy, pg, pe, f, g5, g4, w_pg, w_down)


def ffn_gate_bwd(up, cg, cv, dff, conv_w, t):
    s = up.shape[0]
    bn = 1408
    hb = t // 8
    nt = s // t

    def body(ug_ref, uv_ref, hbg_ref, hbv_ref, cg_ref, cv_ref, cag_ref, cav_ref, d_ref, da_ref, wg_ref, wv_ref,
             dug_ref, duv_ref, gwg_ref, gwv_ref, gbg_ref, gbv_ref, dg_sc, dv_sc, u_sc):
        i = pl.program_id(1)

        @pl.when(i == 0)
        def _():
            gwg_ref[...] = jnp.zeros((3, bn), F32)
            gwv_ref[...] = jnp.zeros((3, bn), F32)
            gbg_ref[...] = jnp.zeros((1, bn), F32)
            gbv_ref[...] = jnp.zeros((1, bn), F32)

        def gate(c_g, c_v, d_):
            gl, dgl = _gelu_and_grad(c_g.astype(F32))
            d_ = d_.astype(F32)
            return d_ * c_v.astype(F32) * dgl, d_ * gl

        dg, dv = gate(cg_ref[...], cv_ref[...], d_ref[...])
        dg_sc[0:t, :], dv_sc[0:t, :] = dg, dv
        nxt = da_ref[...].astype(F32) * (i < nt - 1).astype(F32)
        dg_sc[t:, :], dv_sc[t:, :] = gate(cag_ref[...], cav_ref[...], nxt)

        def back(dc, d_sc, u_ref, h_ref, w_ref, du_ref, gw_ref, gb_ref):
            w = w_ref[...]
            du_ref[...] = (w[2:3] * dc + w[1:2] * d_sc[pl.ds(1, t), :] + w[0:1] * d_sc[pl.ds(2, t), :]).astype(BF)
            u = u_ref[...].astype(F32)
            u_sc[0:8, :] = h_ref[...].astype(F32) * (i > 0).astype(F32)
            u_sc[8:, :] = u
            gw_ref[0:1, :] += jnp.sum(dc * u_sc[pl.ds(6, t), :], axis=0, keepdims=True)
            gw_ref[1:2, :] += jnp.sum(dc * u_sc[pl.ds(7, t), :], axis=0, keepdims=True)
            gw_ref[2:3, :] += jnp.sum(dc * u, axis=0, keepdims=True)
            gb_ref[...] += jnp.sum(dc, axis=0, keepdims=True)

        back(dg, dg_sc, ug_ref, hbg_ref, wg_ref, dug_ref, gwg_ref, gbg_ref)
        back(dv, dv_sc, uv_ref, hbv_ref, wv_ref, duv_ref, gwv_ref, gbv_ref)

    tile = lambda off: pl.BlockSpec((t, bn), lambda j, i: (i, j + off))
    before = lambda off: pl.BlockSpec((8, bn), lambda j, i: (jnp.maximum(i * hb - 1, 0), j + off))
    after = lambda off: pl.BlockSpec((8, bn), lambda j, i: (jnp.minimum((i + 1) * hb, nt * hb - 1), j + off))
    cw = lambda off: pl.BlockSpec((3, bn), lambda j, i: (0, j + off))
    cb = lambda off: pl.BlockSpec((1, bn), lambda j, i: (0, j + off))
    return _pc(
        body, name="ffn_gate_bwd", grid=(2, nt),
        in_specs=[tile(0), tile(2), before(0), before(2), tile(0), tile(0), after(0), after(0), tile(0), after(0),
                  cw(0), cw(2)],
        out_specs=[tile(0), tile(0), cw(0), cw(0), cb(0), cb(0)],
        out_shape=[SDS((s, D_FF), BF), SDS((s, D_FF), BF), SDS((3, D_FF), F32), SDS((3, D_FF), F32),
                   SDS((1, D_FF), F32), SDS((1, D_FF), F32)],
        scratch_shapes=[pltpu.VMEM((t + 8, bn), F32)] * 3,
        compiler_params=_cp(2))(up, up, up, up, cg, cv, cg, cv, dff, dff, conv_w, conv_w)


def nt_normbwd(pairs, xin, gain, dres, t, name, side=None):
    s = xin.shape[0]
    nt = s // t
    np_ = len(pairs)
    dys = [p_[0] for p_ in pairs]
    ws = [p_[2] for p_ in pairs]

    def body(*refs):
        ins_, s_in, (dx_ref, gg_ref), s_out, _, s_scr = _split_side(refs, 2 * np_ + 3, 2, 0, side)
        dy_refs, w_refs = ins_[:np_], ins_[np_:2 * np_]
        x_ref, g_ref, dres_ref = ins_[2 * np_:]
        i = pl.program_id(0)

        @pl.when(i == 0)
        def _():
            gg_ref[...] = jnp.zeros((1, D), F32)
            if side:
                side.start(s_in, s_out, *s_scr)

        acc = _dot_nt(dy_refs[0][...], w_refs[0][...])
        for k in range(1, np_):
            acc += _dot_nt(dy_refs[k][...], w_refs[k][...])
        dxn, dg = _rms_bwd(acc, x_ref[...], g_ref[...])
        dx_ref[...] = dres_ref[...] + dxn
        gg_ref[...] += dg
        if side:
            @pl.when(i == nt - 1)
            def _():
                side.finish(s_in, s_out, *s_scr)

    row = lambda w: pl.BlockSpec((t, w), lambda i: (i, 0))
    dy_spec = lambda blk, width: pl.BlockSpec((t, width), lambda i: (i, blk))
    si_specs, so_specs, so_shapes, s_scratch, s_ins = _side_specs(side)
    return _pc(
        body, name=name, grid=(nt,),
        in_specs=[dy_spec(p_[1], p_[3].block_shape[-1]) for p_ in pairs] + [p_[3] for p_ in pairs]
        + [row(D), _const((1, D)), row(D)] + si_specs,
        out_specs=[row(D), _acc((1, D))] + so_specs,
        out_shape=[SDS((s, D), F32), SDS((1, D), F32)] + so_shapes, scratch_shapes=s_scratch,
        compiler_params=_cp(1))(*dys, *ws, xin, gain, dres, *s_ins)


def mix_bwd(dh1, mx, z, ya, yb, g2, w_mix, a_out, b_out, t):
    s = dh1.shape[0]

    def body(dh_ref, mx_ref, ga_ref, gb_ref, ya_ref, yb_ref, g2_ref, wm_ref, ao_ref, bo_ref,
             dmx_ref, dya_ref, dyb_ref, dga_ref, dgb_ref, dsa_ref, dob_ref, gg2_ref):
        @pl.when(pl.program_id(0) == 0)
        def _():
            gg2_ref[...] = jnp.zeros((1, D), F32)

        dmx, dg2 = _rms_bwd(dh_ref[...], mx_ref[...].astype(F32), g2_ref[...])
        gg2_ref[...] += dg2
        dmxb = dmx.astype(BF)
        dmx_ref[...] = dmxb
        dmp = _dot_nt(dmxb, wm_ref[...])

        def gate(g_ref, y_ref, dy_ref, dg_ref, w_ref, dz_ref):
            sg = jax.nn.sigmoid(g_ref[...].astype(F32))
            dyb_ = (dmp * sg).astype(BF)
            dy_ref[...] = dyb_
            dg_ref[...] = (dmp * y_ref[...].astype(F32) * sg * (1.0 - sg)).astype(BF)
            dz_ref[...] = _dot_nt(dyb_, w_ref[...]).astype(BF)

        gate(ga_ref, ya_ref, dya_ref, dga_ref, ao_ref, dsa_ref)
        gate(gb_ref, yb_ref, dyb_ref, dgb_ref, bo_ref, dob_ref)

    row = lambda w: pl.BlockSpec((t, w), lambda i: (i, 0))
    return _pc(
        body, name="mix_bwd", grid=(s // t,),
        in_specs=[row(D), row(D), pl.BlockSpec((t, D), lambda i: (i, 4)), pl.BlockSpec((t, D), lambda i: (i, 5)),
                  row(D), row(D), _const((1, D)), _const((D, D)), _const((A_W, D)), _const((D, D))],
        out_specs=[row(D)] * 5 + [row(A_W), row(D), _acc((1, D))],
        out_shape=[SDS((s, D), BF)] * 5 + [SDS((s, A_W), BF), SDS((s, D), BF), SDS((1, D), F32)],
        compiler_params=_cp(1))(dh1, mx, z, z, ya, yb, g2, w_mix, a_out, b_out)


def sgu_bwd(z, dsa, ln_g, ln_b, w_s, bs_t, t):
    s = z.shape[0]
    nt = s // t

    def body(u_ref, v_ref, dsa_ref, g_ref, b_ref, ws_ref, bs_ref,
             duv_ref, glg_ref, glb_ref, gws_ref, gbs_ref, s_sc, dvn_sc, ds_acc):
        i = pl.program_id(0)

        @pl.when(i == 0)
        def _():
            glg_ref[...] = jnp.zeros((1, A_W), F32)
            glb_ref[...] = jnp.zeros((1, A_W), F32)
            gws_ref[...] = jnp.zeros((A_G, A_C, A_C), F32)
            ds_acc[...] = jnp.zeros((A_C, A_W), F32)

        wm = _sgu_masked(ws_ref)
        rr =lax.broadcasted_iota(jnp.int32, (A_C, A_C), 0)
        cc = lax.broadcasted_iota(jnp.int32, (A_C, A_C), 1)
        tril = cc <= rr
        lng = g_ref[...]
        for ci in range(t // A_C):
            rows = pl.ds(ci * A_C, A_C)
            dgv, rstd, xhat, vn = _sgu_recompute(v_ref[rows, :].astype(F32), lng, b_ref[...])
            for g in range(A_G):
                cols = slice(g * A_GD, (g + 1) * A_GD)
                s_sc[:, cols] = _dot(wm[g], vn[:, cols]) + bs_ref[:, g:g + 1]
            gu, dgu = _gelu_and_grad(u_ref[rows, :].astype(F32))
            dsa_ = dsa_ref[rows, :].astype(F32)
            ds = dsa_ * gu
            ds_acc[...] += ds
            dsb = ds.astype(BF)
            for g in range(A_G):
                cols = slice(g * A_GD, (g + 1) * A_GD)
                gws_ref[g] += jnp.where(tril, _dot_nt(dsb[:, cols], vn[:, cols]), 0.0)
                dvn_sc[:, cols] = _dot_tn(wm[g], dsb[:, cols])
            dvn = dvn_sc[...]
            glb_ref[...] += jnp.sum(dvn, axis=0, keepdims=True)
            glg_ref[...] += jnp.sum(dvn * xhat, axis=0, keepdims=True)
            dxh = dvn * lng
            dgv_ = rstd * (dxh - jnp.mean(dxh, axis=-1, keepdims=True)
                           - xhat * jnp.mean(dxh * xhat, axis=-1, keepdims=True))
            duv_ref[rows, :A_W] = (dsa_ * s_sc[...] * dgu).astype(BF)
            duv_ref[rows, A_W:] = (dgv_ * dgv).astype(BF)

        @pl.when(i == nt - 1)
        def _():
            acc = ds_acc[...]
            for g in range(A_G):
                gbs_ref[:, g:g + 1] = jnp.sum(acc[:, g * A_GD:(g + 1) * A_GD], axis=1, keepdims=True)

    return _pc(
        body, name="sgu_bwd", grid=(nt,),
        in_specs=[pl.BlockSpec((t, A_W), lambda i: (i, 0)), pl.BlockSpec((t, A_W), lambda i: (i, 1)),
                  pl.BlockSpec((t, A_W), lambda i: (i, 0)),
                  _const((1, A_W)), _const((1, A_W)), _const((A_G, A_C, A_C)), _const((A_C, A_G))],
        out_specs=[pl.BlockSpec((t, D), lambda i: (i, 0)), _acc((1, A_W)), _acc((1, A_W)),
                   _acc((A_G, A_C, A_C)), _acc((A_C, A_G))],
        out_shape=[SDS((s, D), BF), SDS((1, A_W), F32), SDS((1, A_W), F32), SDS((A_G, A_C, A_C), F32),
                   SDS((A_C, A_G), F32)],
        scratch_shapes=[pltpu.VMEM((A_C, A_W), F32), pltpu.VMEM((A_C, A_W), F32), pltpu.VMEM((A_C, A_W), F32)],
        compiler_params=_cp(1))(z, z, dsa, ln_g, ln_b, w_s, bs_t)


def gla_bwd(z, qk32, zl, o, dob, states, wgk, bias, wn, ltri, ltri_t, t, side=None):
    s = z.shape[0]
    nt = s // t
    nc = t // B_C

    def body(*refs):
        ins_, s_in, outs_, s_out, scr_, s_scr = _split_side(refs, 12, 6, 5, side)
        qk_ref, v_ref, og_ref, lr_ref, o_ref, dob_ref, st_ref, wgk_ref, bias_ref, wn_ref, l_ref, lt_ref = ins_
        dqk_ref, dv_ref, dog_ref, dpre_ref, gbias_ref, gwn_ref = outs_
        dst_sc, dv_sc, dqd_sc, dkt_sc, ddec_sc = scr_
        i = pl.program_id(0)
        h = pl.program_id(1)

        @pl.when((i == 0) & (h == 0))
        def _():
            gbias_ref[...] = jnp.zeros((B_H, 1, B_HK), F32)
            gwn_ref[...] = jnp.zeros((1, B_HV), F32)
            if side:
                side.start(s_in, s_out, *s_scr)

        @pl.when(i == 0)
        def _():
            dst_sc[h] = jnp.zeros((B_HV, B_HK), F32)

        pre, b, bl, eb, enb, etb, qd, ki, kt = _gla_decays(qk_ref, lr_ref, wgk_ref, bias_ref, l_ref, t)
        qdb, kib, ktb = qd.astype(BF), ki.astype(BF), kt.astype(BF)
        vb = v_ref[...]
        o_ = o_ref[...].astype(F32)
        og = og_ref[...].astype(F32)
        sog = jax.nn.sigmoid(og)
        dob_ = dob_ref[...].astype(F32)
        wn_ = wn_ref[...]
        don = dob_ * og * sog
        do, dwn = _rms_bwd(don, o_, wn_)
        gwn_ref[...] += dwn
        dog_ref[...] = (dob_ * _rms(o_, wn_) * sog * (1.0 + og * (1.0 - sog))).astype(BF)
        dob16 = do.astype(BF)
        keep, keep_t = l_ref[...] > 0, lt_ref[...] > 0
        sc_t = jnp.where(keep_t, _dot_nt(kib, qdb), 0.0).astype(BF)
        dsc = jnp.where(keep, _dot_nt(dob16, vb), 0.0).astype(BF)
        dsc_t = jnp.where(keep_t, _dot_nt(vb, dob16), 0.0).astype(BF)
        dv_sc[...] = _dot(sc_t, dob16)
        dqd_sc[...] = _dot(dsc, kib)
        dki = _dot(dsc_t, qdb)
        for n in reversed(range(nc)):
            rows = slice(n * B_C, (n + 1) * B_C)
            dst = dst_sc[h]
            dstb = dst.astype(BF)
            stp = st_ref[n, 0]
            dv_sc[rows, :] += _dot_nt(ktb[rows], dstb)
            dkt_sc[rows, :] = _dot(vb[rows], dstb)
            dqd_sc[rows, :] += _dot(dob16[rows], stp)
            dec = jnp.exp(bl[n * B_C:n * B_C + 1, :])
            ddec_sc[n] = jnp.sum(dst * stp.astype(F32), axis=0, keepdims=True) * dec
            dst_sc[h] = dst * dec + _dot_tn(dob16[rows], qdb[rows])
        dqd, dkt = dqd_sc[...], dkt_sc[...]
        dv_ref[...] = dv_sc[...].astype(BF)
        dqk_ref[:, :B_HK] = (dqd * eb * (B_HK ** -0.5)).astype(BF)
        dqk_ref[:, B_HK:] = (dki * enb + dkt * etb).astype(BF)
        dktkt = dkt * kt
        db3 = (dqd * qd - dki * ki - dktkt).reshape(nc, B_C, B_HK)
        dbl = jnp.sum(dktkt.reshape(nc, B_C, B_HK), axis=1, keepdims=True) + ddec_sc[...]
        last = lax.broadcasted_iota(jnp.int32, (nc, B_C, B_HK), 1) == B_C - 1
        db = (db3 + jnp.where(last, dbl, 0.0)).reshape(t, B_HK)
        dla = _ldot3(lt_ref[...], db)
        dpre = dla * (1.0 / 16.0) * (1.0 - jax.nn.sigmoid(pre))
        dpre_ref[...] = dpre.astype(BF)
        gbias_ref[h] += jnp.sum(dpre, axis=0, keepdims=True)
        if side:
            @pl.when((i == nt - 1) & (h == B_H - 1))
            def _():
                side.finish(s_in, s_out, *s_scr)

    rv = lambda i: nt - 1 - i
    si_specs, so_specs, so_shapes, s_scratch, s_ins = _side_specs(side)
    return _pc(
        body, name="gla_bwd", grid=(nt, B_H),
        in_specs=[pl.BlockSpec((t, 256), lambda i, h: (rv(i), h)), pl.BlockSpec((t, 256), lambda i, h: (rv(i), 8 + h)),
                  pl.BlockSpec((t, 256), lambda i, h: (rv(i), 12 + h)), pl.BlockSpec((t, LANE), lambda i, h: (rv(i), 0)),
                  pl.BlockSpec((t, B_HV), lambda i, h: (rv(i), h)), pl.BlockSpec((t, B_HV), lambda i, h: (rv(i), h)),
                  pl.BlockSpec((nc, 1, B_HV, B_HK), lambda i, h: (rv(i), h, 0, 0)),
                  pl.BlockSpec((LANE, B_HK), lambda i, h: (0, h)), pl.BlockSpec((1, B_HK), lambda i, h: (0, h)),
                  _const((1, B_HV)), _const((t, t)), _const((t, t))] + si_specs,
        out_specs=[pl.BlockSpec((t, 256), lambda i, h: (rv(i), h)), pl.BlockSpec((t, B_HV), lambda i, h: (rv(i), h)),
                   pl.BlockSpec((t, B_HV), lambda i, h: (rv(i), h)), pl.BlockSpec((t, B_HK), lambda i, h: (rv(i), h)),
                   _acc((B_H, 1, B_HK)), _acc((1, B_HV))] + so_specs,
        out_shape=[SDS((s, D), BF), SDS((s, D), BF), SDS((s, D), BF), SDS((s, B_H * B_HK), BF),
                   SDS((B_H, 1, B_HK), F32), SDS((1, B_HV), F32)] + so_shapes,
        scratch_shapes=[pltpu.VMEM((B_H, B_HV, B_HK), F32), pltpu.VMEM((t, B_HV), F32), pltpu.VMEM((t, B_HK), F32),
                        pltpu.VMEM((t, B_HK), F32), pltpu.VMEM((nc, 1, B_HK), F32)] + s_scratch,
        compiler_params=_cp(2))(qk32, z, z, zl, o, dob, states, wgk, bias, wn, ltri, ltri_t, *s_ins)


def mm_tn(a, b, name, tk=512):
    s, m = a.shape
    n = b.shape[1]
    bn = next(c for c in (1024, 1408, 512, 256, 128) if n % c == 0 and m * c * 4 <= 6 * 1024 * 1024)
    nk = s // tk

    def body(a_ref, b_ref, o_ref, acc):
        k = pl.program_id(1)

        @pl.when(k == 0)
        def _():
            acc[...] = jnp.zeros((m, bn), F32)

        acc[...] += _dot_tn(a_ref[...].astype(BF), b_ref[...])

        @pl.when(k == nk - 1)
        def _():
            o_ref[...] = acc[...].astype(BF)

    return _pc(
        body, name=name, grid=(n // bn, nk),
        in_specs=[pl.BlockSpec((tk, m), lambda j, k: (k, 0)), pl.BlockSpec((tk, bn), lambda j, k: (k, j))],
        out_specs=pl.BlockSpec((m, bn), lambda j, k: (0, j)),
        out_shape=SDS((m, n), BF), scratch_shapes=[pltpu.VMEM((m, bn), F32)], compiler_params=_cp(2))(a, b)


def mm_cols(a, w, blk, width, t, name):
    s, k = a.shape

    def body(a_ref, w_ref, o_ref):
        o_ref[...] = _dot(a_ref[...], w_ref[...]).astype(BF)

    return _pc(body, name=name, grid=(s // t,),
               in_specs=[pl.BlockSpec((t, k), lambda i: (i, 0)),
                         pl.BlockSpec((k, width), lambda i: (0, blk), pipeline_mode=pl.Buffered(1))],
               out_specs=pl.BlockSpec((t, width), lambda i: (i, 0)), out_shape=SDS((s, width), BF),
               compiler_params=_cp(1))(a, w)


def mm_nt_small(a, w, t, name):
    s, k = a.shape
    n = w.shape[0]

    def body(a_ref, w_ref, o_ref):
        o_ref[...] = _dot_nt(a_ref[...], w_ref[...]).astype(BF)

    return _pc(body, name=name, grid=(s // t,),
               in_specs=[pl.BlockSpec((t, k), lambda i: (i, 0)), _const((n, k))],
               out_specs=pl.BlockSpec((t, n), lambda i: (i, 0)), out_shape=SDS((s, n), BF),
               compiler_params=_cp(1))(a, w)


def _adamw(w, g, m, v):
    m = ADAM_B1 * m + (1.0 - ADAM_B1) * g
    v = ADAM_B2 * v + (1.0 - ADAM_B2) * (g * g)
    m_hat = m / (1.0 - ADAM_B1 ** ADAM_STEP)
    v_hat = v / (1.0 - ADAM_B2 ** ADAM_STEP)
    return -ADAM_LR * (m_hat / (jnp.sqrt(v_hat) + ADAM_EPS) + ADAM_WD * w), m, v


def _half_rows(rows):
    rh = rows // 2
    return rh, max(b for b in range(16, 257, 16) if rh % b == 0)


def _pc_sp(body, grid, in_specs, out_specs, out_shape, name):
    gs = pltpu.PrefetchScalarGridSpec(num_scalar_prefetch=1, grid=grid, in_specs=in_specs, out_specs=out_specs)
    return _pc(body, grid_spec=gs, out_shape=out_shape, name=name, compiler_params=_cp(len(grid)))


def adamw_halves(sc, own, sib, w, m, v, name):
    rows, cols = w.shape
    rh, br = _half_rows(rows)
    nbk = rh // br

    def body(sc_ref, own_ref, sib_ref, w_ref, m_ref, v_ref, go_ref, d_ref, mo_ref, vo_ref):
        g_ = jnp.where(pl.program_id(0) // nbk == sc_ref[0], own_ref[...], sib_ref[...])
        go_ref[...] = g_
        d_ref[...], mo_ref[...], vo_ref[...] = _adamw(w_ref[...], g_, m_ref[...], v_ref[...])

    half = pl.BlockSpec((br, cols), lambda i, sc_: (i % nbk, 0))
    blk = pl.BlockSpec((br, cols), lambda i, sc_: (i, 0))
    return _pc_sp(body, (2 * nbk,), [half, half, blk, blk, blk], [blk] * 4, [SDS((rows, cols), F32)] * 4,
                  name)(sc, own, sib, w, m, v)


def adamw_small(g, w, m, v):
    def body(g_ref, w_ref, m_ref, v_ref, d_ref, mo_ref, vo_ref):
        d_ref[...], mo_ref[...], vo_ref[...] = _adamw(w_ref[...], g_ref[...], m_ref[...], v_ref[...])

    vm = pl.BlockSpec(memory_space=pltpu.VMEM)
    return _pc(body, name="adamw_small", in_specs=[vm] * 4, out_specs=[vm] * 3, out_shape=[SDS(g.shape, F32)] * 3,
               compiler_params=pltpu.CompilerParams(vmem_limit_bytes=VMEM_LIMIT))(g, w, m, v)


def _pos():
    return lax.axis_index("x"), lax.axis_index("y"), lax.axis_index("c")


def _other_chips(x, y):
    return [(1 - x, y), (x, 1 - y), (1 - x, 1 - y)]


_ANY = pl.BlockSpec(memory_space=pltpu.HBM)


class _Side:
    def __init__(self, ins, out_shapes, nsem, start, finish):
        self.ins, self.out_shapes, self.start, self.finish = list(ins), list(out_shapes), start, finish
        self.scratch = [pltpu.SemaphoreType.DMA((nsem,)), pltpu.SemaphoreType.DMA((nsem,))]
        self.n_in, self.n_out = len(self.ins), len(self.out_shapes)


def _run_side(side, name):
    def body(*refs):
        args_ = (refs[:side.n_in], refs[side.n_in:side.n_in + side.n_out], *refs[side.n_in + side.n_out:])
        side.start(*args_)
        side.finish(*args_)

    return _pc(body, name=name, in_specs=[_ANY] * side.n_in, out_specs=[_ANY] * side.n_out,
               out_shape=side.out_shapes, scratch_shapes=side.scratch)(*side.ins)


def gather_side(bigs, tinies):
    nb, nt_ = len(bigs), len(tinies)

    def plan(ins, outs, ssem, rsem):
        x, y, c = _pos()
        me = 2 * x + y
        chips = _other_chips(x, y)
        sibling = (x, y, 1 - c)

        def copy(k, src, dst, to):
            return pltpu.make_async_remote_copy(src_ref=src, dst_ref=dst, send_sem=ssem.at[k], recv_sem=rsem.at[k],
                                                device_id=to, device_id_type=MESH)

        sends, landed, passed_on, tiny_landed = [], [], [], []
        for w in range(nb):
            rh = bigs[w].shape[0] // 2
            mine = pl.ds(pl.multiple_of(c * rh, 16), rh)
            theirs = pl.ds(pl.multiple_of((1 - c) * rh, 16), rh)
            for j, (cx, cy) in enumerate(chips):
                sends.append(copy(6 * w + j, ins[w].at[mine], outs[w].at[me, mine], (cx, cy, c)))
                blk = outs[w].at[2 * cx + cy, mine]
                landed.append((copy(6 * w + j, blk, blk, (cx, cy, c)), copy(6 * w + 3 + j, blk, blk, sibling)))
                blk = outs[w].at[2 * cx + cy, theirs]
                passed_on.append(copy(6 * w + 3 + j, blk, blk, sibling))
        for w in range(nt_):
            for j, (cx, cy) in enumerate(chips):
                k = 6 * nb + 3 * w + j
                sends.append(copy(k, ins[nb + w], outs[nb + w].at[me], (cx, cy, c)))
                blk = outs[nb + w].at[2 * cx + cy]
                tiny_landed.append(copy(k, blk, blk, (cx, cy, c)))
        return sends, landed, passed_on, tiny_landed

    def start(ins, outs, ssem, rsem):
        for cp in plan(ins, outs, ssem, rsem)[0]:
            cp.start()

    def finish(ins, outs, ssem, rsem):
        sends, landed, passed_on, tiny_landed = plan(ins, outs, ssem, rsem)
        for arrived, forward in landed:
            arrived.wait_recv()
            forward.start()
        for arrived in tiny_landed + passed_on:
            arrived.wait_recv()
        for cp in sends + [forward for _, forward in landed]:
            cp.wait_send()

    return _Side(list(bigs) + list(tinies), [SDS((4,) + a.shape, a.dtype) for a in list(bigs) + list(tinies)],
                 6 * nb + 3 * nt_, start, finish)


def swap_halves(gs, name):
    n = len(gs)

    def body(*refs):
        g_refs, sib_refs = refs[:n], refs[n:2 * n]
        ssem, rsem = refs[2 * n:]
        x, y, c = _pos()
        cps = []
        for w in range(n):
            rh = gs[w].shape[1] // 2
            give = pl.ds(pl.multiple_of((1 - c) * rh, 16), rh)
            cp = pltpu.make_async_remote_copy(src_ref=g_refs[w].at[:, give], dst_ref=sib_refs[w], send_sem=ssem.at[w],
                                              recv_sem=rsem.at[w], device_id=(x, y, 1 - c), device_id_type=MESH)
            cp.start()
            cps.append(cp)
        for cp in cps:
            cp.wait()

    return _pc(body, name=name, in_specs=[_ANY] * n, out_specs=[_ANY] * n,
               out_shape=[SDS((g.shape[0], g.shape[1] // 2, g.shape[2]), g.dtype) for g in gs],
               scratch_shapes=[pltpu.SemaphoreType.DMA((n,)), pltpu.SemaphoreType.DMA((n,))])(*gs)


def add_half(sc, g, sib, name):
    l, r, cols = g.shape
    rh, br = _half_rows(r)
    nbk = rh // br

    def body(sc_ref, g_ref, s_ref, o_ref):
        o_ref[...] = (g_ref[...].astype(F32) + s_ref[...].astype(F32)).astype(BF)

    blk = pl.BlockSpec((1, br, cols), lambda j, i, sc_: (j, i, 0))
    return _pc_sp(body, (l, nbk), [pl.BlockSpec((1, br, cols), lambda j, i, sc_: (j, sc_[0] * nbk + i, 0)), blk], blk,
                  SDS((l, rh, cols), BF), name)(sc, g, sib)


def exchange_side(ps):
    n_ = len(ps)

    def width(p_):
        return p_.shape[2] if p_.shape[0] == 4 else p_.shape[2] // 4

    def plan(p_refs, got_refs, ssem, rsem):
        x, y, c = _pos()
        cps = []
        for w in range(n_):
            wd = width(ps[w])
            for j, (cx, cy) in enumerate(_other_chips(x, y)):
                to = 2 * cx + cy
                src = p_refs[w].at[to] if ps[w].shape[0] == 4 else p_refs[w].at[0, :, pl.ds(pl.multiple_of(to * wd, LANE), wd)]
                cps.append(pltpu.make_async_remote_copy(
                    src_ref=src, dst_ref=got_refs[w].at[j], send_sem=ssem.at[3 * w + j], recv_sem=rsem.at[3 * w + j],
                    device_id=(cx, cy, c), device_id_type=MESH))
        return cps

    def start(*refs):
        for cp in plan(*refs):
            cp.start()

    def finish(*refs):
        for cp in plan(*refs):
            cp.wait()

    return _Side(ps, [SDS((3, p_.shape[1], width(p_)), p_.dtype) for p_ in ps], 3 * n_, start, finish)


def sum4(sc, p, got, name):
    _, rh, wd = got.shape
    _, br = _half_rows(2 * rh)

    def body(sc_ref, p_ref, g_ref, r_ref):
        r_ref[...] = ((p_ref[0].astype(F32) + g_ref[0].astype(F32)) + (g_ref[1].astype(F32) + g_ref[2].astype(F32)))

    own = (pl.BlockSpec((1, br, wd), lambda i, sc_: (sc_[1], i, 0)) if p.shape[0] == 4
           else pl.BlockSpec((1, br, wd), lambda i, sc_: (0, i, sc_[1])))
    return _pc_sp(body, (rh // br,), [own, pl.BlockSpec((3, br, wd), lambda i, sc_: (0, i, 0))],
                  pl.BlockSpec((br, wd), lambda i, sc_: (i, 0)), SDS((rh, wd), F32), name)(sc, p, got)


def join_halves(halves):
    n = len(halves)

    def body(*refs):
        h_refs, got_refs = refs[:n], refs[n:2 * n]
        ssem, rsem = refs[2 * n:]
        x, y, c = _pos()
        cps = []
        for w in range(n):
            cp = pltpu.make_async_remote_copy(src_ref=h_refs[w], dst_ref=got_refs[w], send_sem=ssem.at[w],
                                              recv_sem=rsem.at[w], device_id=(x, y, 1 - c), device_id_type=MESH)
            cp.start()
            cps.append(cp)
        for cp in cps:
            cp.wait()

    return _pc(body, name="join_halves", in_specs=[_ANY] * n, out_specs=[_ANY] * n,
               out_shape=[SDS(h.shape, h.dtype) for h in halves],
               scratch_shapes=[pltpu.SemaphoreType.DMA((n,)), pltpu.SemaphoreType.DMA((n,))])(*halves)


def allreduce_small(g):
    rows = g.shape[0]
    rh = rows // 2

    def body(g_ref, out_ref, sib_buf, chip_buf, sum_sc, ssem, rsem):
        x, y, c = _pos()
        me = 2 * x + y
        sibling = (x, y, 1 - c)
        mine = pl.ds(pl.multiple_of(c * rh, 8), rh)

        def copy(k, src, dst, to):
            return pltpu.make_async_remote_copy(src_ref=src, dst_ref=dst, send_sem=ssem.at[k], recv_sem=rsem.at[k],
                                                device_id=to, device_id_type=MESH)

        cp = copy(0, g_ref, sib_buf, sibling)
        cp.start()
        cp.wait()
        sum_sc[...] = g_ref[...] + sib_buf[...]
        chips = _other_chips(x, y)
        cps = [copy(1 + j, sum_sc.at[mine], chip_buf.at[me], (cx, cy, c)) for j, (cx, cy) in enumerate(chips)]
        for cp in cps:
            cp.start()
        chip_buf[me] = sum_sc[mine, :]
        for j, (cx, cy) in enumerate(chips):
            copy(1 + j, sum_sc.at[mine], chip_buf.at[2 * cx + cy], (cx, cy, c)).wait_recv()
        for cp in cps:
            cp.wait_send()
        out_ref[mine, :] = (chip_buf[0] + chip_buf[1]) + (chip_buf[2] + chip_buf[3])
        cp = copy(4, out_ref.at[mine], out_ref.at[mine], sibling)
        cp.start()
        cp.wait()

    vm = pl.BlockSpec(memory_space=pltpu.VMEM)
    return _pc(body, name="allreduce_small", in_specs=[vm], out_specs=vm, out_shape=SDS((rows, LANE), F32),
               scratch_shapes=[pltpu.VMEM((rows, LANE), F32), pltpu.VMEM((4, rh, LANE), F32), pltpu.VMEM((rows, LANE), F32),
                               pltpu.SemaphoreType.DMA((5,)), pltpu.SemaphoreType.DMA((5,))],
               compiler_params=pltpu.CompilerParams(vmem_limit_bytes=VMEM_LIMIT))(g)


def _pack_small(entries, get):
    flat = jnp.concatenate([get(n).reshape(-1).astype(F32) for n, _ in entries])
    rows = -(-flat.shape[0] // (8 * LANE)) * 8
    return jnp.pad(flat, (0, rows * LANE - flat.shape[0])).reshape(rows, LANE)


def _unpack_small(entries, packed):
    out, off = {}, 0
    flat = packed.reshape(-1)
    for name, n in entries:
        out[name] = flat[off:off + n]
        off += n
    return out


def _cols_full(blk):
    return blk.transpose(1, 0, 2).reshape(blk.shape[1], 4 * blk.shape[2])


def kernel(x, p, pre_mix_norm, w_in, a_ln_g, a_ln_b, a_spatial_w, a_spatial_b, a_out, b_gk, b_gk_bias, b_out_norm, b_out, w_mix_out, post_mix_norm, pre_ffn_norm, w_up, conv_w, conv_b, w_down, post_ffn_norm, w_ple, w_ple_gate, post_ple_norm, loss_target, m_pre_mix_norm, m_w_in, m_a_ln_g, m_a_ln_b, m_a_spatial_w, m_a_spatial_b, m_a_out, m_b_gk, m_b_gk_bias, m_b_out_norm, m_b_out, m_w_mix_out, m_post_mix_norm, m_pre_ffn_norm, m_w_up, m_conv_w, m_conv_b, m_w_down, m_post_ffn_norm, m_w_ple, m_w_ple_gate, m_post_ple_norm, v_pre_mix_norm, v_w_in, v_a_ln_g, v_a_ln_b, v_a_spatial_w, v_a_spatial_b, v_a_out, v_b_gk, v_b_gk_bias, v_b_out_norm, v_b_out, v_w_mix_out, v_post_mix_norm, v_pre_ffn_norm, v_w_up, v_conv_w, v_conv_b, v_w_down, v_post_ffn_norm, v_w_ple, v_w_ple_gate, v_post_ple_norm):
    args = dict(locals())
    order = ['pre_mix_norm', 'w_in', 'a_ln_g', 'a_ln_b', 'a_spatial_w', 'a_spatial_b', 'a_out', 'b_gk', 'b_gk_bias',
             'b_out_norm', 'b_out', 'w_mix_out', 'post_mix_norm', 'pre_ffn_norm', 'w_up', 'conv_w', 'conv_b', 'w_down',
             'post_ffn_norm', 'w_ple', 'w_ple_gate', 'post_ple_norm']
    assert sorted(BIG + TINY + tuple(n for n, _ in SMALL)) == sorted(order)
    s = x.shape[1]
    xs = x.reshape(s, D)
    ps = p.reshape(s, PLE)
    tgt = loss_target.reshape(s, D)
    t_big = min(1024, s)
    t_mid = min(512, s)
    t_small = min(256, s)
    mx_, my_, mc_ = _pos()
    me = 2 * mx_ + my_
    sc = jnp.stack([mc_, me]).astype(jnp.int32)
    shard = lambda n: args[n].reshape(args[n].shape[1:])

    mine = {n: shard(n).astype(BF) for n in BIG}
    mine.update({n: shard(n) for n in TINY})
    fill = lambda names, gots: {n: lax.dynamic_update_slice(got, mine[n][None], (me, 0, 0)) for n, got in zip(names, gots)}
    first = ("w_in",) + TINY
    full = fill(first, _run_side(gather_side([mine["w_in"]], [mine[n] for n in TINY]), "gather_first"))
    wi = _cols_full(full["w_in"])
    seg = lambda a, b: wi[:, a:b]
    qk = [seg(1024 + h * B_HK, 1024 + (h + 1) * B_HK) for h in range(B_H)]
    kk = [seg(1536 + h * B_HK, 1536 + (h + 1) * B_HK) for h in range(B_H)]
    w_z = jnp.concatenate([seg(0, 1024)] + [m_ for h in range(B_H) for m_ in (qk[h], kk[h])]
                          + [seg(2048, 4096), seg(4112, 6160), seg(4096, 4112), jnp.zeros((D, LANE - B_RANK), BF)], axis=1)
    wgk = jnp.pad(_cols_full(full["b_gk"]).astype(BF), ((0, LANE - B_RANK), (0, 0)))
    w_conv = _cols_full(full["conv_w"])
    g1, g2, g3 = pre_mix_norm.reshape(1, D), post_mix_norm.reshape(1, D), pre_ffn_norm.reshape(1, D)
    g4, g5 = post_ffn_norm.reshape(1, D), post_ple_norm.reshape(1, D)
    ln_g, ln_b = a_ln_g.reshape(1, A_W), a_ln_b.reshape(1, A_W)
    w_s = a_spatial_w.reshape(A_G, A_C, A_C)
    bs_t = a_spatial_b.reshape(A_G, A_C).T
    gk_bias = b_gk_bias.reshape(1, B_H * B_HK)
    wn = b_out_norm.reshape(1, B_HV)
    cb = conv_b.reshape(1, 2 * D_FF)
    idx = jnp.arange(t_mid)
    ltri = ((idx[:, None] // B_C == idx[None, :] // B_C) & (idx[None, :] <= idx[:, None])).astype(BF)

    a, z, qk32, *gots = norm_matmul(xs, g1, w_z, D, t_big, "in_proj", nblk=6, f32_blk=1,
                                    side=gather_side([mine[n] for n in BIG[1:]], []))
    full.update(fill(BIG[1:], gots))
    w_aout, w_ple_f = _cols_full(full["a_out"]), _cols_full(full["w_ple"])
    w_bout, w_mix, w_pg = (full[n].reshape(D, D) for n in ("b_out", "w_mix_out", "w_ple_gate"))
    w_dn, w_up3 = full["w_down"].reshape(D_FF, D), full["w_up"]
    zl = mm_cols(a, w_z, 48, LANE, t_big, "lr_proj")
    sa = sgu_fwd(z, ln_g, ln_b, w_s, bs_t, t_mid)
    ob, o, states = gla_fwd(z, qk32, zl, wgk, gk_bias, wn, ltri, t_mid)
    ya, yb, mp, mx, h1 = mix_fwd(sa, ob, z, xs, w_aout, w_bout, w_mix, g2, t_small)
    c, up = norm_matmul(h1, g3, w_up3, 1408, t_big, "up_proj")
    ff, cg, cv = ffn_gate_fwd(up, w_conv, cb, t_small)
    f, h2, pg, pe, dy, loss = out_fwd(ff, h1, ps, tgt, w_dn, w_pg, w_ple_f, g4, g5, t_small)

    dh2, dpe, dpg, df, dff, gg5, gg4 = out_bwd(dy, pg, pe, f, g5, g4, w_pg, w_dn, t_small)
    dup_g, dup_v, gcw_g, gcw_v, gcb_g, gcb_v = ffn_gate_bwd(up, cg, cv, dff, w_conv, t_small)
    wu = lambda k: pl.BlockSpec((None, D, 1408), lambda i: (k, 0, 0), pipeline_mode=pl.Buffered(1))
    dh1, gg3 = nt_normbwd([(dup_g, 0, w_up3, wu(0)), (dup_g, 1, w_up3, wu(1)), (dup_v, 0, w_up3, wu(2)),
                           (dup_v, 1, w_up3, wu(3))], h1, g3, dh2, t_small, "up_bwd")
    dmx, dya, dyb, dga, dgb, dsa, dob, gg2 = mix_bwd(dh1, mx, z, ya, yb, g2, w_mix, w_aout, w_bout, t_small)
    duv, g_lng, g_lnb, g_ws, g_bst = sgu_bwd(z, dsa, ln_g, ln_b, w_s, bs_t, t_mid)

    grads = {
        "a_out": mm_tn(sa, dya, "dw_a_out")[None],
        "b_out": mm_tn(ob, dyb, "dw_b_out").reshape(4, D // 4, D),
        "w_mix_out": mm_tn(mp, dmx, "dw_mix").reshape(4, D // 4, D),
        "w_up": jnp.concatenate([mm_tn(c, dup_g, "dw_up_g"), mm_tn(c, dup_v, "dw_up_v")], axis=1)[None],
        "w_down": mm_tn(ff, df, "dw_down").reshape(4, D_FF // 4, D),
        "w_ple": mm_tn(ps, dpe, "dw_ple")[None],
        "w_ple_gate": mm_tn(h2, dpg, "dw_ple_gate").reshape(4, D // 4, D),
    }

    def chip_partials(names):
        gs = [grads[n] for n in names]
        return [add_half(sc, g, sib, "partial_" + n)
                for n, g, sib in zip(names, gs, swap_halves(gs, "swap_halves_" + names[0]))]

    parts = dict(zip(BIG[1:], chip_partials(BIG[1:])))
    dqk, dvb, dog, dpre, g_gkb, g_wn, *gots = gla_bwd(z, qk32, zl, o, dob, states, wgk, gk_bias, wn, ltri, ltri.T,
                                                      t_mid, side=exchange_side([parts[n] for n in BIG[1:]]))
    got = dict(zip(BIG[1:], gots))
    dlr = mm_nt_small(dpre, wgk, t_mid, "dlr")
    segs = [duv, dqk, dvb, dog, dga, dgb, dlr]

    gz = [mm_tn(a, sg_, "dw_in_%d" % k) for k, sg_ in enumerate(segs)]
    gq = [gz[1][:, h * 256:h * 256 + B_HK] for h in range(B_H)]
    gk = [gz[1][:, h * 256 + B_HK:(h + 1) * 256] for h in range(B_H)]
    g_in = jnp.concatenate([gz[0]] + gq + gk + [gz[2], gz[3], gz[6][:, :B_RANK], gz[4], gz[5]], axis=1)
    grads["w_in"] = g_in.reshape(D, 4, 1540).transpose(1, 0, 2)
    parts["w_in"], = chip_partials(("w_in",))
    wz = lambda k, width: pl.BlockSpec((D, width), lambda i: (0, k), pipeline_mode=pl.Buffered(1))
    dx, gg1, got["w_in"] = nt_normbwd(
        [(segs[k], 0, w_z, wz(k, D)) for k in range(6)] + [(dlr, 0, w_z, wz(48, LANE))], xs, g1, dh1, t_small,
        "in_bwd", side=exchange_side([parts["w_in"]]))

    reds = [sum4(sc, parts[n], got[n], "sum_" + n) for n in BIG]
    outs = {}
    for n, red, sib in zip(BIG, reds, join_halves(reds)):
        res = adamw_halves(sc, red, sib, shard(n), shard("m_" + n), shard("v_" + n), "adamw_" + n)
        outs[n] = [r_.reshape(args[n].shape) for r_ in res]

    small_g = {
        "pre_mix_norm": gg1, "a_ln_g": g_lng, "a_ln_b": g_lnb, "a_spatial_w": g_ws, "a_spatial_b": g_bst.T,
        "b_gk_bias": g_gkb, "b_out_norm": g_wn, "post_mix_norm": gg2, "pre_ffn_norm": gg3,
        "conv_b": jnp.concatenate([gcb_g, gcb_v], axis=1), "post_ffn_norm": gg4, "post_ple_norm": gg5,
        "b_gk": mm_tn(zl, dpre, "dw_gk")[:B_RANK], "conv_w": jnp.concatenate([gcw_g, gcw_v], axis=1),
    }
    red_entries = SMALL + (("b_gk", B_RANK * 512), ("conv_w", 3 * 2 * D_FF))
    g_fin = _unpack_small(red_entries, allreduce_small(_pack_small(red_entries, lambda n: small_g[n])))
    g_fin["b_gk"] = lax.dynamic_slice(g_fin["b_gk"].reshape(B_RANK, 512), (0, me * B_HK), (B_RANK, B_HK))
    g_fin["conv_w"] = lax.dynamic_slice(g_fin["conv_w"].reshape(3, 2 * D_FF), (0, me * 1408), (3, 1408))
    upd_entries = SMALL + (("b_gk", B_RANK * B_HK), ("conv_w", 3 * 1408))
    res = adamw_small(*[_pack_small(upd_entries, get) for get in
                        (lambda n: g_fin[n], lambda n: args[n], lambda n: args["m_" + n], lambda n: args["v_" + n])])
    res = [_unpack_small(upd_entries, r_) for r_ in res]
    for n, _ in upd_entries:
        outs[n] = [r_[n].reshape(args[n].shape) for r_ in [g_fin] + res]

    total = lax.psum(loss[0, 0], ("x", "y", "c"))
    return (total, dx.reshape(x.shape), *[outs[n][0] for n in order], *[outs[n][1] for n in order],
            *[outs[n][2] for n in order], *[outs[n][3] for n in order])
```

```python
import functools
import math

import jax
import jax.numpy as jnp
from jax import lax
from jax.experimental import pallas as pl
from jax.experimental.pallas import tpu as pltpu

F32 = jnp.float32
BF = jnp.bfloat16
SDS = jax.ShapeDtypeStruct
MESH = pl.DeviceIdType.MESH

EPS = 1e-6
D = 1024
A_W = 512
A_G, A_C = 8, 128
A_GD = A_W // A_G
B_H, B_HK, B_HV = 4, 128, 256
B_C = 64
B_RANK = 16
D_FF = 2816
PLE = 256
ZW = 6272
LANE = 128
VMEM_LIMIT = 60 * 1024 * 1024

ADAM_LR, ADAM_B1, ADAM_B2, ADAM_EPS, ADAM_WD, ADAM_STEP = 0.001, 0.9, 0.999, 1e-08, 0.01, 10

_GC = math.sqrt(2.0 / math.pi)
_GA = 0.044715

BIG = ("w_in", "a_out", "b_out", "w_mix_out", "w_up", "w_down", "w_ple", "w_ple_gate")
TINY = ("b_gk", "conv_w")
SMALL = (("pre_mix_norm", 1024), ("a_ln_g", 512), ("a_ln_b", 512), ("a_spatial_w", 131072),
         ("a_spatial_b", 1024), ("b_gk_bias", 512), ("b_out_norm", 256), ("post_mix_norm", 1024),
         ("pre_ffn_norm", 1024), ("conv_b", 5632), ("post_ffn_norm", 1024), ("post_ple_norm", 1024))


def _pc(body, **kw):
    return pl.pallas_call(body, **kw)


def _cp(n):
    return pltpu.CompilerParams(dimension_semantics=("arbitrary",) * n, vmem_limit_bytes=VMEM_LIMIT)


def _const(shape):
    nd = len(shape)
    return pl.BlockSpec(shape, lambda *_: (0,) * nd, pipeline_mode=pl.Buffered(1))


def _acc(shape):
    nd = len(shape)
    return pl.BlockSpec(shape, lambda *_: (0,) * nd)


def _dot(a, b):
    return jnp.dot(a, b, preferred_element_type=F32)


def _dot_nt(a, b):
    return lax.dot_general(a, b, (((1,), (1,)), ((), ())), preferred_element_type=F32)


def _dot_tn(a, b):
    return lax.dot_general(a, b, (((0,), (0,)), ((), ())), preferred_element_type=F32)


def _gelu(x):
    return 0.5 * x * (1.0 + jnp.tanh(_GC * (x + _GA * x * x * x)))


def _gelu_and_grad(x):
    x2 = x * x
    s = 0.5 * jnp.tanh((_GC * x) * (1.0 + _GA * x2)) + 0.5
    g = x * s
    return g, s + g * (1.0 - s) * ((6.0 * _GC * _GA) * x2 + 2.0 * _GC)


def _log_sigmoid(x):
    return jnp.minimum(x, 0.0) - jnp.log(1.0 + jnp.exp(-jnp.abs(x)))


def _rms(x, g):
    return x * lax.rsqrt(jnp.mean(x * x, axis=-1, keepdims=True) + EPS) * g


def _rms_bwd(dy, x, g):
    r = lax.rsqrt(jnp.mean(x * x, axis=-1, keepdims=True) + EPS)
    n = x * r
    dn = dy * g
    dx = r * (dn - n * jnp.mean(dn * n, axis=-1, keepdims=True))
    return dx, jnp.sum(dy * n, axis=0, keepdims=True)


def _ldot3(l, x):
    h = x.astype(BF)
    r = x - h.astype(F32)
    m = r.astype(BF)
    lo = (r - m.astype(F32)).astype(BF)
    return _dot(l, h) + _dot(l, m) + _dot(l, lo)


def _split_side(refs, n_in, n_out, n_scratch, side):
    si, so = (side.n_in, side.n_out) if side else (0, 0)
    cuts = [n_in, si, n_out, so, n_scratch]
    out, at = [], 0
    for c in cuts:
        out.append(refs[at:at + c])
        at += c
    return (*out, refs[at:])


def _side_specs(side):
    return ([_ANY] * side.n_in, [_ANY] * side.n_out, side.out_shapes, side.scratch, side.ins) if side else ([],) * 5


def norm_matmul(x, g, w, bn, t, name, nblk=None, f32_blk=None, side=None):
    s, dm = x.shape
    if w.ndim == 3:
        nblk = w.shape[0]
        w_spec = pl.BlockSpec((None, dm, bn), lambda i, j: (j, 0, 0))
    else:
        nblk = nblk or w.shape[1] // bn
        w_spec = pl.BlockSpec((dm, bn), lambda i, j: (0, j))
    extra = f32_blk is not None
    nt = s // t

    def body(*refs):
        (x_ref, g_ref, w_ref), s_in, outs, s_out, (a_sc,), s_scr = _split_side(refs, 3, 2 + extra, 1, side)
        a_ref, z_ref = outs[:2]
        i, j = pl.program_id(0), pl.program_id(1)
        if side:
            @pl.when((i == 0) & (j == 0))
            def _():
                side.start(s_in, s_out, *s_scr)

        @pl.when(j == 0)
        def _():
            a = _rms(x_ref[...], g_ref[...]).astype(BF)
            a_sc[...] = a
            a_ref[...] = a

        acc = _dot(a_sc[...], w_ref[...])
        z_ref[...] = acc.astype(BF)
        if extra:
            @pl.when(j == f32_blk)
            def _():
                outs[2][...] = acc
        if side:
            @pl.when((i == nt - 1) & (j == nblk - 1))
            def _():
                side.finish(s_in, s_out, *s_scr)

    si_specs, so_specs, so_shapes, s_scratch, s_ins = _side_specs(side)
    return _pc(
        body, name=name, grid=(nt, nblk),
        in_specs=[pl.BlockSpec((t, dm), lambda i, j: (i, 0)), _const((1, dm)), w_spec] + si_specs,
        out_specs=[pl.BlockSpec((t, dm), lambda i, j: (i, 0)), pl.BlockSpec((t, bn), lambda i, j: (i, j))]
        + [pl.BlockSpec((t, bn), lambda i, j: (i, 0))] * extra + so_specs,
        out_shape=[SDS((s, dm), BF), SDS((s, nblk * bn), BF)] + [SDS((s, bn), F32)] * extra + so_shapes,
        scratch_shapes=[pltpu.VMEM((t, dm), BF)] + s_scratch, compiler_params=_cp(2))(x, g, w, *s_ins)


def _sgu_masked(ws_ref):
    r = lax.broadcasted_iota(jnp.int32, (A_C, A_C), 0)
    c = lax.broadcasted_iota(jnp.int32, (A_C, A_C), 1)
    return [jnp.where(c <= r, ws_ref[g], 0.0).astype(BF) for g in range(A_G)]


def _sgu_recompute(v, lng, lnb):
    gv, dgv = _gelu_and_grad(v)
    mu = jnp.mean(gv, axis=-1, keepdims=True)
    xc = gv - mu
    rstd = lax.rsqrt(jnp.mean(xc * xc, axis=-1, keepdims=True) + EPS)
    xhat = xc * rstd
    return dgv, rstd, xhat, (xhat * lng + lnb).astype(BF)


def sgu_fwd(z, ln_g, ln_b, w_s, bs_t, t):
    s = z.shape[0]

    def body(u_ref, v_ref, g_ref, b_ref, ws_ref, bs_ref, sa_ref, s_sc):
        wm = _sgu_masked(ws_ref)
        for ci in range(t // A_C):
            rows = pl.ds(ci * A_C, A_C)
            _, _, _, vn = _sgu_recompute(v_ref[rows, :].astype(F32), g_ref[...], b_ref[...])
            for g in range(A_G):
                cols = slice(g * A_GD, (g + 1) * A_GD)
                s_sc[:, cols] = _dot(wm[g], vn[:, cols]) + bs_ref[:, g:g + 1]
            sa_ref[rows, :] = (_gelu(u_ref[rows, :].astype(F32)) * s_sc[...]).astype(BF)

    return _pc(
        body, name="sgu_fwd", grid=(s // t,),
        in_specs=[pl.BlockSpec((t, A_W), lambda i: (i, 0)), pl.BlockSpec((t, A_W), lambda i: (i, 1)),
                  _const((1, A_W)), _const((1, A_W)), _const((A_G, A_C, A_C)), _const((A_C, A_G))],
        out_specs=pl.BlockSpec((t, A_W), lambda i: (i, 0)),
        out_shape=SDS((s, A_W), BF),
        scratch_shapes=[pltpu.VMEM((A_C, A_W), F32)], compiler_params=_cp(1))(z, z, ln_g, ln_b, w_s, bs_t)


def _gla_decays(qk_ref, lr_ref, wgk_ref, bias_ref, l_ref, t):
    nc = t // B_C
    q = qk_ref[:, :B_HK].astype(F32) * (B_HK ** -0.5)
    k = qk_ref[:, B_HK:].astype(F32)
    pre = _dot(lr_ref[...], wgk_ref[...]) + bias_ref[...]
    la = _log_sigmoid(pre) * (1.0 / 16.0)
    b = _ldot3(l_ref[...], la)
    b3 = b.reshape(nc, B_C, B_HK)
    bl = jnp.broadcast_to(b3[:, B_C - 1:B_C, :], (nc, B_C, B_HK)).reshape(t, B_HK)
    eb, enb, etb = jnp.exp(b), jnp.exp(-b), jnp.exp(bl - b)
    return pre, b, bl, eb, enb, etb, q * eb, k * enb, k * etb


def gla_fwd(z, qk32, zl, wgk, bias, wn, ltri, t):
    s = z.shape[0]
    nc = t // B_C

    def body(qk_ref, v_ref, og_ref, lr_ref, wgk_ref, bias_ref, wn_ref, l_ref, ob_ref, o_ref, st_ref, st_sc, o_sc):
        h = pl.program_id(1)

        @pl.when(pl.program_id(0) == 0)
        def _():
            st_sc[h] = jnp.zeros((B_HV, B_HK), F32)

        _, _, bl, _, _, _, qd, ki, kt = _gla_decays(qk_ref, lr_ref, wgk_ref, bias_ref, l_ref, t)
        qd, ki, kt = qd.astype(BF), ki.astype(BF), kt.astype(BF)
        vb = v_ref[...]
        sc = jnp.where(l_ref[...] > 0, _dot_nt(qd, ki), 0.0).astype(BF)
        o_sc[...] = _dot(sc, vb)
        for n in range(nc):
            rows = slice(n * B_C, (n + 1) * B_C)
            st = st_sc[h]
            stb = st.astype(BF)
            st_ref[n, 0] = stb
            o_sc[rows, :] += _dot_nt(qd[rows], stb)
            st_sc[h] = st * jnp.exp(bl[n * B_C:n * B_C + 1, :]) + _dot_tn(vb[rows], kt[rows])
        ob = o_sc[...].astype(BF)
        o_ref[...] = ob
        og = og_ref[...].astype(F32)
        ob_ref[...] = (_rms(ob.astype(F32), wn_ref[...]) * og * jax.nn.sigmoid(og)).astype(BF)

    return _pc(
        body, name="gla_fwd", grid=(s // t, B_H),
        in_specs=[pl.BlockSpec((t, 256), lambda i, h: (i, h)), pl.BlockSpec((t, 256), lambda i, h: (i, 8 + h)),
                  pl.BlockSpec((t, 256), lambda i, h: (i, 12 + h)), pl.BlockSpec((t, LANE), lambda i, h: (i, 0)),
                  pl.BlockSpec((LANE, B_HK), lambda i, h: (0, h)), pl.BlockSpec((1, B_HK), lambda i, h: (0, h)),
                  _const((1, B_HV)), _const((t, t))],
        out_specs=[pl.BlockSpec((t, B_HV), lambda i, h: (i, h)), pl.BlockSpec((t, B_HV), lambda i, h: (i, h)),
                   pl.BlockSpec((nc, 1, B_HV, B_HK), lambda i, h: (i, h, 0, 0))],
        out_shape=[SDS((s, D), BF), SDS((s, D), BF), SDS((s // B_C, B_H, B_HV, B_HK), BF)],
        scratch_shapes=[pltpu.VMEM((B_H, B_HV, B_HK), F32), pltpu.VMEM((t, B_HV), F32)],
        compiler_params=_cp(2))(qk32, z, z, zl, wgk, bias, wn, ltri)


def mix_fwd(sa, ob, z, x, a_out, b_out, w_mix, g2, t):
    s = x.shape[0]

    def body(sa_ref, ob_ref, ga_ref, gb_ref, x_ref, ao_ref, bo_ref, wm_ref, g2_ref,
             ya_ref, yb_ref, mp_ref, mx_ref, h1_ref):
        ya = _dot(sa_ref[...], ao_ref[...]).astype(BF)
        yb = _dot(ob_ref[...], bo_ref[...]).astype(BF)
        ya_ref[...] = ya
        yb_ref[...] = yb
        mp = (jax.nn.sigmoid(ga_ref[...].astype(F32)) * ya.astype(F32)
              + jax.nn.sigmoid(gb_ref[...].astype(F32)) * yb.astype(F32)).astype(BF)
        mp_ref[...] = mp
        mx = _dot(mp, wm_ref[...]).astype(BF)
        mx_ref[...] = mx
        h1_ref[...] = x_ref[...] + _rms(mx.astype(F32), g2_ref[...])

    row = lambda w: pl.BlockSpec((t, w), lambda i: (i, 0))
    return _pc(
        body, name="mix_fwd", grid=(s // t,),
        in_specs=[row(A_W), row(D), pl.BlockSpec((t, D), lambda i: (i, 4)), pl.BlockSpec((t, D), lambda i: (i, 5)),
                  row(D), _const((A_W, D)), _const((D, D)), _const((D, D)), _const((1, D))],
        out_specs=[row(D)] * 5,
        out_shape=[SDS((s, D), BF)] * 4 + [SDS((s, D), F32)],
        compiler_params=_cp(1))(sa, ob, z, z, x, a_out, b_out, w_mix, g2)


def ffn_gate_fwd(up, conv_w, conv_b, t):
    s = up.shape[0]
    bn = 1408
    hb = t // 8

    def body(ug_ref, uv_ref, hg_ref, hv_ref, wg_ref, wv_ref, bg_ref, bv_ref, ff_ref, cg_ref, cv_ref):
        live = (pl.program_id(1) > 0).astype(F32)

        def branch(u_ref, h_ref, w_ref, b_ref, c_ref):
            ext = jnp.concatenate([h_ref[...].astype(F32) * live, u_ref[...].astype(F32)], axis=0)
            w = w_ref[...]
            c = (b_ref[...] + w[0:1] * pltpu.roll(ext, 2, 0) + w[1:2] * pltpu.roll(ext, 1, 0) + w[2:3] * ext)[8:]
            c = c.astype(BF)
            c_ref[...] = c
            return c.astype(F32)

        g = _gelu(branch(ug_ref, hg_ref, wg_ref, bg_ref, cg_ref))
        ff_ref[...] = (g * branch(uv_ref, hv_ref, wv_ref, bv_ref, cv_ref)).astype(BF)

    halo = lambda off: pl.BlockSpec((8, bn), lambda j, i: (jnp.maximum(i * hb - 1, 0), j + off))
    out = pl.BlockSpec((t, bn), lambda j, i: (i, j))
    return _pc(
        body, name="ffn_gate_fwd", grid=(2, s // t),
        in_specs=[pl.BlockSpec((t, bn), lambda j, i: (i, j)), pl.BlockSpec((t, bn), lambda j, i: (i, j + 2)),
                  halo(0), halo(2),
                  pl.BlockSpec((3, bn), lambda j, i: (0, j)), pl.BlockSpec((3, bn), lambda j, i: (0, j + 2)),
                  pl.BlockSpec((1, bn), lambda j, i: (0, j)), pl.BlockSpec((1, bn), lambda j, i: (0, j + 2))],
        out_specs=[out] * 3, out_shape=[SDS((s, D_FF), BF)] * 3,
        compiler_params=_cp(2))(up, up, up, up, conv_w, conv_w, conv_b, conv_b)


def out_fwd(ff, h1, p, tgt, w_down, w_pg, w_ple, g4, g5, t):
    s = h1.shape[0]

    def body(ff_ref, h1_ref, p_ref, t_ref, wd_ref, wpg_ref, wpl_ref, g4_ref, g5_ref,
             f_ref, h2_ref, pg_ref, pe_ref, dy_ref, loss_ref):
        @pl.when(pl.program_id(0) == 0)
        def _():
            loss_ref[...] = jnp.zeros((1, 1), F32)

        f = _dot(ff_ref[...], wd_ref[...]).astype(BF)
        f_ref[...] = f
        h2 = h1_ref[...] + _rms(f.astype(F32), g4_ref[...])
        h2b = h2.astype(BF)
        h2_ref[...] = h2b
        pg = _dot(h2b, wpg_ref[...]).astype(BF)
        pe = _dot(p_ref[...].astype(BF), wpl_ref[...]).astype(BF)
        pg_ref[...] = pg
        pe_ref[...] = pe
        y = h2 + _rms(jax.nn.sigmoid(pg.astype(F32)) * pe.astype(F32), g5_ref[...])
        err = y - t_ref[...]
        dy_ref[...] = err * (1.0 / D)
        loss_ref[...] += (0.5 / D) * jnp.sum(err * err)

    row = lambda w: pl.BlockSpec((t, w), lambda i: (i, 0))
    return _pc(
        body, name="out_fwd", grid=(s // t,),
        in_specs=[row(D_FF), row(D), row(PLE), row(D), _const((D_FF, D)), _const((D, D)), _const((PLE, D)),
                  _const((1, D)), _const((1, D))],
        out_specs=[row(D)] * 5 + [_acc((1, 1))],
        out_shape=[SDS((s, D), BF)] * 4 + [SDS((s, D), F32), SDS((1, 1), F32)],
        compiler_params=_cp(1))(ff, h1, p, tgt, w_down, w_pg, w_ple, g4, g5)


def out_bwd(dy, pg, pe, f, g5, g4, w_pg, w_down, t):
    s = dy.shape[0]

    def body(dy_ref, pg_ref, pe_ref, f_ref, g5_ref, g4_ref, wpg_ref, wd_ref,
             dh2_ref, dpe_ref, dpg_ref, df_ref, dff_ref, gg5_ref, gg4_ref):
        @pl.when(pl.program_id(0) == 0)
        def _():
            gg5_ref[...] = jnp.zeros((1, D), F32)
            gg4_ref[...] = jnp.zeros((1, D), F32)

        dy_ = dy_ref[...]
        pg_ = pg_ref[...].astype(F32)
        pe_ = pe_ref[...].astype(F32)
        sg = jax.nn.sigmoid(pg_)
        dple, dg5 = _rms_bwd(dy_, sg * pe_, g5_ref[...])
        gg5_ref[...] += dg5
        dpe_ref[...] = (dple * sg).astype(BF)
        dpg = (dple * pe_ * sg * (1.0 - sg)).astype(BF)
        dpg_ref[...] = dpg
        dh2 = dy_ + _dot_nt(dpg, wpg_ref[...])
        dh2_ref[...] = dh2
        df, dg4 = _rms_bwd(dh2, f_ref[...].astype(F32), g4_ref[...])
        gg4_ref[...] += dg4
        dfb = df.astype(BF)
        df_ref[...] = dfb
        dff_ref[...] = _dot_nt(dfb, wd_ref[...]).astype(BF)

    row = lambda w: pl.BlockSpec((t, w), lambda i: (i, 0))
    return _pc(
        body, name="out_bwd", grid=(s // t,),
        in_specs=[row(D), row(D), row(D), row(D), _const((1, D)), _const((1, D)), _const((D, D)), _const((D_FF, D))],
        out_specs=[row(D), row(D), row(D), row(D), row(D_FF), _acc((1, D)), _acc((1, D))],
        out_shape=[SDS((s, D), F32), SDS((s, D), BF), SDS((s, D), BF), SDS((s, D), BF), SDS((s, D_FF), BF),
                   SDS((1, D), F32), SDS((1, D), F32)],
        compiler_params=_cp(1))(dy, pg, pe, f, g5, g4, w_pg, w_down)


def ffn_gate_bwd(up, cg, cv, dff, conv_w, t):
    s = up.shape[0]
    bn = 1408
    hb = t // 8
    nt = s // t
    r = t + 8

    def body(ug_ref, uv_ref, hbg_ref, hbv_ref, cg_ref, cv_ref, cag_ref, cav_ref, d_ref, da_ref, wg_ref, wv_ref,
             dug_ref, duv_ref, gwg_ref, gwv_ref, gbg_ref, gbv_ref):
        i = pl.program_id(1)

        @pl.when(i == 0)
        def _():
            gwg_ref[...] = jnp.zeros((3, bn), F32)
            gwv_ref[...] = jnp.zeros((3, bn), F32)
            gbg_ref[...] = jnp.zeros((1, bn), F32)
            gbv_ref[...] = jnp.zeros((1, bn), F32)

        def gate(c_g, c_v, d_):
            gl, dgl = _gelu_and_grad(c_g.astype(F32))
            d_ = d_.astype(F32)
            return d_ * c_v.astype(F32) * dgl, d_ * gl

        dg, dv = gate(cg_ref[...], cv_ref[...], d_ref[...])
        nxt = da_ref[...].astype(F32) * (i < nt - 1).astype(F32)
        dg_n, dv_n = gate(cag_ref[...], cav_ref[...], nxt)

        def back(dc, dc_next, u_ref, h_ref, w_ref, du_ref, gw_ref, gb_ref):
            w = w_ref[...]
            d_ext = jnp.concatenate([dc, dc_next], axis=0)
            du_ref[...] = (w[2:3] * dc + w[1:2] * pltpu.roll(d_ext, r - 1, 0)[:t]
                           + w[0:1] * pltpu.roll(d_ext, r - 2, 0)[:t]).astype(BF)
            u = u_ref[...].astype(F32)
            u_ext = jnp.concatenate([h_ref[...].astype(F32) * (i > 0).astype(F32), u], axis=0)
            gw_ref[0:1, :] += jnp.sum(dc * pltpu.roll(u_ext, 2, 0)[8:], axis=0, keepdims=True)
            gw_ref[1:2, :] += jnp.sum(dc * pltpu.roll(u_ext, 1, 0)[8:], axis=0, keepdims=True)
            gw_ref[2:3, :] += jnp.sum(dc * u, axis=0, keepdims=True)
            gb_ref[...] += jnp.sum(dc, axis=0, keepdims=True)

        back(dg, dg_n, ug_ref, hbg_ref, wg_ref, dug_ref, gwg_ref, gbg_ref)
        back(dv, dv_n, uv_ref, hbv_ref, wv_ref, duv_ref, gwv_ref, gbv_ref)

    tile = lambda off: pl.BlockSpec((t, bn), lambda j, i: (i, j + off))
    before = lambda off: pl.BlockSpec((8, bn), lambda j, i: (jnp.maximum(i * hb - 1, 0), j + off))
    after = lambda off: pl.BlockSpec((8, bn), lambda j, i: (jnp.minimum((i + 1) * hb, nt * hb - 1), j + off))
    cw = lambda off: pl.BlockSpec((3, bn), lambda j, i: (0, j + off))
    cb = lambda off: pl.BlockSpec((1, bn), lambda j, i: (0, j + off))
    return _pc(
        body, name="ffn_gate_bwd", grid=(2, nt),
        in_specs=[tile(0), tile(2), before(0), before(2), tile(0), tile(0), after(0), after(0), tile(0), after(0),
                  cw(0), cw(2)],
        out_specs=[tile(0), tile(0), cw(0), cw(0), cb(0), cb(0)],
        out_shape=[SDS((s, D_FF), BF), SDS((s, D_FF), BF), SDS((3, D_FF), F32), SDS((3, D_FF), F32),
                   SDS((1, D_FF), F32), SDS((1, D_FF), F32)],
        compiler_params=_cp(2))(up, up, up, up, cg, cv, cg, cv, dff, dff, conv_w, conv_w)


def nt_normbwd(pairs, xin, gain, dres, t, name, side=None):
    s = xin.shape[0]
    nt = s // t
    np_ = len(pairs)
    dys = [p_[0] for p_ in pairs]
    ws = [p_[2] for p_ in pairs]

    def body(*refs):
        ins_, s_in, (dx_ref, gg_ref), s_out, _, s_scr = _split_side(refs, 2 * np_ + 3, 2, 0, side)
        dy_refs, w_refs = ins_[:np_], ins_[np_:2 * np_]
        x_ref, g_ref, dres_ref = ins_[2 * np_:]
        i = pl.program_id(0)

        @pl.when(i == 0)
        def _():
            gg_ref[...] = jnp.zeros((1, D), F32)
            if side:
                side.start(s_in, s_out, *s_scr)

        acc = _dot_nt(dy_refs[0][...], w_refs[0][...])
        for k in range(1, np_):
            acc += _dot_nt(dy_refs[k][...], w_refs[k][...])
        dxn, dg = _rms_bwd(acc, x_ref[...], g_ref[...])
        dx_ref[...] = dres_ref[...] + dxn
        gg_ref[...] += dg
        if side:
            @pl.when(i == nt - 1)
            def _():
                side.finish(s_in, s_out, *s_scr)

    row = lambda w: pl.BlockSpec((t, w), lambda i: (i, 0))
    dy_spec = lambda blk, width: pl.BlockSpec((t, width), lambda i: (i, blk))
    si_specs, so_specs, so_shapes, s_scratch, s_ins = _side_specs(side)
    return _pc(
        body, name=name, grid=(nt,),
        in_specs=[dy_spec(p_[1], p_[3].block_shape[-1]) for p_ in pairs] + [p_[3] for p_ in pairs]
        + [row(D), _const((1, D)), row(D)] + si_specs,
        out_specs=[row(D), _acc((1, D))] + so_specs,
        out_shape=[SDS((s, D), F32), SDS((1, D), F32)] + so_shapes, scratch_shapes=s_scratch,
        compiler_params=_cp(1))(*dys, *ws, xin, gain, dres, *s_ins)


def mix_bwd(dh1, mx, z, ya, yb, g2, w_mix, a_out, b_out, t):
    s = dh1.shape[0]

    def body(dh_ref, mx_ref, ga_ref, gb_ref, ya_ref, yb_ref, g2_ref, wm_ref, ao_ref, bo_ref,
             dmx_ref, dya_ref, dyb_ref, dga_ref, dgb_ref, dsa_ref, dob_ref, gg2_ref):
        @pl.when(pl.program_id(0) == 0)
        def _():
            gg2_ref[...] = jnp.zeros((1, D), F32)

        dmx, dg2 = _rms_bwd(dh_ref[...], mx_ref[...].astype(F32), g2_ref[...])
        gg2_ref[...] += dg2
        dmxb = dmx.astype(BF)
        dmx_ref[...] = dmxb
        dmp = _dot_nt(dmxb, wm_ref[...])

        def gate(g_ref, y_ref, dy_ref, dg_ref, w_ref, dz_ref):
            sg = jax.nn.sigmoid(g_ref[...].astype(F32))
            dyb_ = (dmp * sg).astype(BF)
            dy_ref[...] = dyb_
            dg_ref[...] = (dmp * y_ref[...].astype(F32) * sg * (1.0 - sg)).astype(BF)
            dz_ref[...] = _dot_nt(dyb_, w_ref[...]).astype(BF)

        gate(ga_ref, ya_ref, dya_ref, dga_ref, ao_ref, dsa_ref)
        gate(gb_ref, yb_ref, dyb_ref, dgb_ref, bo_ref, dob_ref)

    row = lambda w: pl.BlockSpec((t, w), lambda i: (i, 0))
    return _pc(
        body, name="mix_bwd", grid=(s // t,),
        in_specs=[row(D), row(D), pl.BlockSpec((t, D), lambda i: (i, 4)), pl.BlockSpec((t, D), lambda i: (i, 5)),
                  row(D), row(D), _const((1, D)), _const((D, D)), _const((A_W, D)), _const((D, D))],
        out_specs=[row(D)] * 5 + [row(A_W), row(D), _acc((1, D))],
        out_shape=[SDS((s, D), BF)] * 5 + [SDS((s, A_W), BF), SDS((s, D), BF), SDS((1, D), F32)],
        compiler_params=_cp(1))(dh1, mx, z, z, ya, yb, g2, w_mix, a_out, b_out)


def sgu_bwd(z, dsa, ln_g, ln_b, w_s, bs_t, t):
    s = z.shape[0]
    nt = s // t

    def body(u_ref, v_ref, dsa_ref, g_ref, b_ref, ws_ref, bs_ref,
             duv_ref, glg_ref, glb_ref, gws_ref, gbs_ref, s_sc, dvn_sc, ds_acc):
        i = pl.program_id(0)

        @pl.when(i == 0)
        def _():
            glg_ref[...] = jnp.zeros((1, A_W), F32)
            glb_ref[...] = jnp.zeros((1, A_W), F32)
            gws_ref[...] = jnp.zeros((A_G, A_C, A_C), F32)
            ds_acc[...] = jnp.zeros((A_C, A_W), F32)

        wm = _sgu_masked(ws_ref)
        rr =lax.broadcasted_iota(jnp.int32, (A_C, A_C), 0)
        cc = lax.broadcasted_iota(jnp.int32, (A_C, A_C), 1)
        tril = cc <= rr
        lng = g_ref[...]
        for ci in range(t // A_C):
            rows = pl.ds(ci * A_C, A_C)
            dgv, rstd, xhat, vn = _sgu_recompute(v_ref[rows, :].astype(F32), lng, b_ref[...])
            for g in range(A_G):
                cols = slice(g * A_GD, (g + 1) * A_GD)
                s_sc[:, cols] = _dot(wm[g], vn[:, cols]) + bs_ref[:, g:g + 1]
            gu, dgu = _gelu_and_grad(u_ref[rows, :].astype(F32))
            dsa_ = dsa_ref[rows, :].astype(F32)
            ds = dsa_ * gu
            ds_acc[...] += ds
            dsb = ds.astype(BF)
            for g in range(A_G):
                cols = slice(g * A_GD, (g + 1) * A_GD)
                gws_ref[g] += jnp.where(tril, _dot_nt(dsb[:, cols], vn[:, cols]), 0.0)
                dvn_sc[:, cols] = _dot_tn(wm[g], dsb[:, cols])
            dvn = dvn_sc[...]
            glb_ref[...] += jnp.sum(dvn, axis=0, keepdims=True)
            glg_ref[...] += jnp.sum(dvn * xhat, axis=0, keepdims=True)
            dxh = dvn * lng
            dgv_ = rstd * (dxh - jnp.mean(dxh, axis=-1, keepdims=True)
                           - xhat * jnp.mean(dxh * xhat, axis=-1, keepdims=True))
            duv_ref[rows, :A_W] = (dsa_ * s_sc[...] * dgu).astype(BF)
            duv_ref[rows, A_W:] = (dgv_ * dgv).astype(BF)

        @pl.when(i == nt - 1)
        def _():
            acc = ds_acc[...]
            for g in range(A_G):
                gbs_ref[:, g:g + 1] = jnp.sum(acc[:, g * A_GD:(g + 1) * A_GD], axis=1, keepdims=True)

    return _pc(
        body, name="sgu_bwd", grid=(nt,),
        in_specs=[pl.BlockSpec((t, A_W), lambda i: (i, 0)), pl.BlockSpec((t, A_W), lambda i: (i, 1)),
                  pl.BlockSpec((t, A_W), lambda i: (i, 0)),
                  _const((1, A_W)), _const((1, A_W)), _const((A_G, A_C, A_C)), _const((A_C, A_G))],
        out_specs=[pl.BlockSpec((t, D), lambda i: (i, 0)), _acc((1, A_W)), _acc((1, A_W)),
                   _acc((A_G, A_C, A_C)), _acc((A_C, A_G))],
        out_shape=[SDS((s, D), BF), SDS((1, A_W), F32), SDS((1, A_W), F32), SDS((A_G, A_C, A_C), F32),
                   SDS((A_C, A_G), F32)],
        scratch_shapes=[pltpu.VMEM((A_C, A_W), F32), pltpu.VMEM((A_C, A_W), F32), pltpu.VMEM((A_C, A_W), F32)],
        compiler_params=_cp(1))(z, z, dsa, ln_g, ln_b, w_s, bs_t)


def gla_bwd(z, qk32, zl, o, dob, states, wgk, bias, wn, ltri, ltri_t, t, side=None):
    s = z.shape[0]
    nt = s // t
    nc = t // B_C

    def body(*refs):
        ins_, s_in, outs_, s_out, scr_, s_scr = _split_side(refs, 12, 6, 5, side)
        qk_ref, v_ref, og_ref, lr_ref, o_ref, dob_ref, st_ref, wgk_ref, bias_ref, wn_ref, l_ref, lt_ref = ins_
        dqk_ref, dv_ref, dog_ref, dpre_ref, gbias_ref, gwn_ref = outs_
        dst_sc, dv_sc, dqd_sc, dkt_sc, ddec_sc = scr_
        i = pl.program_id(0)
        h = pl.program_id(1)

        @pl.when((i == 0) & (h == 0))
        def _():
            gbias_ref[...] = jnp.zeros((B_H, 1, B_HK), F32)
            gwn_ref[...] = jnp.zeros((1, B_HV), F32)
            if side:
                side.start(s_in, s_out, *s_scr)

        @pl.when(i == 0)
        def _():
            dst_sc[h] = jnp.zeros((B_HV, B_HK), F32)

        pre, b, bl, eb, enb, etb, qd, ki, kt = _gla_decays(qk_ref, lr_ref, wgk_ref, bias_ref, l_ref, t)
        qdb, kib, ktb = qd.astype(BF), ki.astype(BF), kt.astype(BF)
        vb = v_ref[...]
        o_ = o_ref[...].astype(F32)
        og = og_ref[...].astype(F32)
        sog = jax.nn.sigmoid(og)
        dob_ = dob_ref[...].astype(F32)
        wn_ = wn_ref[...]
        don = dob_ * og * sog
        do, dwn = _rms_bwd(don, o_, wn_)
        gwn_ref[...] += dwn
        dog_ref[...] = (dob_ * _rms(o_, wn_) * sog * (1.0 + og * (1.0 - sog))).astype(BF)
        dob16 = do.astype(BF)
        keep, keep_t = l_ref[...] > 0, lt_ref[...] > 0
        sc_t = jnp.where(keep_t, _dot_nt(kib, qdb), 0.0).astype(BF)
        dsc = jnp.where(keep, _dot_nt(dob16, vb), 0.0).astype(BF)
        dsc_t = jnp.where(keep_t, _dot_nt(vb, dob16), 0.0).astype(BF)
        dv_sc[...] = _dot(sc_t, dob16)
        dqd_sc[...] = _dot(dsc, kib)
        dki = _dot(dsc_t, qdb)
        for n in reversed(range(nc)):
            rows = slice(n * B_C, (n + 1) * B_C)
            dst = dst_sc[h]
            dstb = dst.astype(BF)
            stp = st_ref[n, 0]
            dv_sc[rows, :] += _dot_nt(ktb[rows], dstb)
            dkt_sc[rows, :] = _dot(vb[rows], dstb)
            dqd_sc[rows, :] += _dot(dob16[rows], stp)
            dec = jnp.exp(bl[n * B_C:n * B_C + 1, :])
            ddec_sc[n] = jnp.sum(dst * stp.astype(F32), axis=0, keepdims=True) * dec
            dst_sc[h] = dst * dec + _dot_tn(dob16[rows], qdb[rows])
        dqd, dkt = dqd_sc[...], dkt_sc[...]
        dv_ref[...] = dv_sc[...].astype(BF)
        dqk_ref[:, :B_HK] = (dqd * eb * (B_HK ** -0.5)).astype(BF)
        dqk_ref[:, B_HK:] = (dki * enb + dkt * etb).astype(BF)
        dktkt = dkt * kt
        db3 = (dqd * qd - dki * ki - dktkt).reshape(nc, B_C, B_HK)
        dbl = jnp.sum(dktkt.reshape(nc, B_C, B_HK), axis=1, keepdims=True) + ddec_sc[...]
        last = lax.broadcasted_iota(jnp.int32, (nc, B_C, B_HK), 1) == B_C - 1
        db = (db3 + jnp.where(last, dbl, 0.0)).reshape(t, B_HK)
        dla = _ldot3(lt_ref[...], db)
        dpre = dla * (1.0 / 16.0) * (1.0 - jax.nn.sigmoid(pre))
        dpre_ref[...] = dpre.astype(BF)
        gbias_ref[h] += jnp.sum(dpre, axis=0, keepdims=True)
        if side:
            @pl.when((i == nt - 1) & (h == B_H - 1))
            def _():
                side.finish(s_in, s_out, *s_scr)

    rv = lambda i: nt - 1 - i
    si_specs, so_specs, so_shapes, s_scratch, s_ins = _side_specs(side)
    return _pc(
        body, name="gla_bwd", grid=(nt, B_H),
        in_specs=[pl.BlockSpec((t, 256), lambda i, h: (rv(i), h)), pl.BlockSpec((t, 256), lambda i, h: (rv(i), 8 + h)),
                  pl.BlockSpec((t, 256), lambda i, h: (rv(i), 12 + h)), pl.BlockSpec((t, LANE), lambda i, h: (rv(i), 0)),
                  pl.BlockSpec((t, B_HV), lambda i, h: (rv(i), h)), pl.BlockSpec((t, B_HV), lambda i, h: (rv(i), h)),
                  pl.BlockSpec((nc, 1, B_HV, B_HK), lambda i, h: (rv(i), h, 0, 0)),
                  pl.BlockSpec((LANE, B_HK), lambda i, h: (0, h)), pl.BlockSpec((1, B_HK), lambda i, h: (0, h)),
                  _const((1, B_HV)), _const((t, t)), _const((t, t))] + si_specs,
        out_specs=[pl.BlockSpec((t, 256), lambda i, h: (rv(i), h)), pl.BlockSpec((t, B_HV), lambda i, h: (rv(i), h)),
                   pl.BlockSpec((t, B_HV), lambda i, h: (rv(i), h)), pl.BlockSpec((t, B_HK), lambda i, h: (rv(i), h)),
                   _acc((B_H, 1, B_HK)), _acc((1, B_HV))] + so_specs,
        out_shape=[SDS((s, D), BF), SDS((s, D), BF), SDS((s, D), BF), SDS((s, B_H * B_HK), BF),
                   SDS((B_H, 1, B_HK), F32), SDS((1, B_HV), F32)] + so_shapes,
        scratch_shapes=[pltpu.VMEM((B_H, B_HV, B_HK), F32), pltpu.VMEM((t, B_HV), F32), pltpu.VMEM((t, B_HK), F32),
                        pltpu.VMEM((t, B_HK), F32), pltpu.VMEM((nc, 1, B_HK), F32)] + s_scratch,
        compiler_params=_cp(2))(qk32, z, z, zl, o, dob, states, wgk, bias, wn, ltri, ltri_t, *s_ins)


def mm_tn(a, b, name, tk=512):
    s, m = a.shape
    n = b.shape[1]
    bn = next(c for c in (1024, 1408, 512, 256, 128) if n % c == 0 and m * c * 4 <= 6 * 1024 * 1024)
    nk = s // tk

    def body(a_ref, b_ref, o_ref, acc):
        k = pl.program_id(1)

        @pl.when(k == 0)
        def _():
            acc[...] = jnp.zeros((m, bn), F32)

        acc[...] += _dot_tn(a_ref[...].astype(BF), b_ref[...])

        @pl.when(k == nk - 1)
        def _():
            o_ref[...] = acc[...].astype(BF)

    return _pc(
        body, name=name, grid=(n // bn, nk),
        in_specs=[pl.BlockSpec((tk, m), lambda j, k: (k, 0)), pl.BlockSpec((tk, bn), lambda j, k: (k, j))],
        out_specs=pl.BlockSpec((m, bn), lambda j, k: (0, j)),
        out_shape=SDS((m, n), BF), scratch_shapes=[pltpu.VMEM((m, bn), F32)], compiler_params=_cp(2))(a, b)


def mm_cols(a, w, blk, width, t, name):
    s, k = a.shape

    def body(a_ref, w_ref, o_ref):
        o_ref[...] = _dot(a_ref[...], w_ref[...]).astype(BF)

    return _pc(body, name=name, grid=(s // t,),
               in_specs=[pl.BlockSpec((t, k), lambda i: (i, 0)),
                         pl.BlockSpec((k, width), lambda i: (0, blk), pipeline_mode=pl.Buffered(1))],
               out_specs=pl.BlockSpec((t, width), lambda i: (i, 0)), out_shape=SDS((s, width), BF),
               compiler_params=_cp(1))(a, w)


def mm_nt_small(a, w, t, name):
    s, k = a.shape
    n = w.shape[0]

    def body(a_ref, w_ref, o_ref):
        o_ref[...] = _dot_nt(a_ref[...], w_ref[...]).astype(BF)

    return _pc(body, name=name, grid=(s // t,),
               in_specs=[pl.BlockSpec((t, k), lambda i: (i, 0)), _const((n, k))],
               out_specs=pl.BlockSpec((t, n), lambda i: (i, 0)), out_shape=SDS((s, n), BF),
               compiler_params=_cp(1))(a, w)


def _adamw(w, g, m, v):
    m = ADAM_B1 * m + (1.0 - ADAM_B1) * g
    v = ADAM_B2 * v + (1.0 - ADAM_B2) * (g * g)
    m_hat = m / (1.0 - ADAM_B1 ** ADAM_STEP)
    v_hat = v / (1.0 - ADAM_B2 ** ADAM_STEP)
    return -ADAM_LR * (m_hat / (jnp.sqrt(v_hat) + ADAM_EPS) + ADAM_WD * w), m, v


def _half_rows(rows):
    rh = rows // 2
    return rh, max(b for b in range(16, 257, 16) if rh % b == 0)


def _pc_sp(body, grid, in_specs, out_specs, out_shape, name):
    gs = pltpu.PrefetchScalarGridSpec(num_scalar_prefetch=1, grid=grid, in_specs=in_specs, out_specs=out_specs)
    return _pc(body, grid_spec=gs, out_shape=out_shape, name=name, compiler_params=_cp(len(grid)))


def adamw_halves(sc, own, sib, w, m, v, name):
    rows, cols = w.shape
    rh, br = _half_rows(rows)
    nbk = rh // br

    def body(sc_ref, own_ref, sib_ref, w_ref, m_ref, v_ref, go_ref, d_ref, mo_ref, vo_ref):
        g_ = jnp.where(pl.program_id(0) // nbk == sc_ref[0], own_ref[...], sib_ref[...])
        go_ref[...] = g_
        d_ref[...], mo_ref[...], vo_ref[...] = _adamw(w_ref[...], g_, m_ref[...], v_ref[...])

    half = pl.BlockSpec((br, cols), lambda i, sc_: (i % nbk, 0))
    blk = pl.BlockSpec((br, cols), lambda i, sc_: (i, 0))
    return _pc_sp(body, (2 * nbk,), [half, half, blk, blk, blk], [blk] * 4, [SDS((rows, cols), F32)] * 4,
                  name)(sc, own, sib, w, m, v)


def adamw_cols(sc, own, sib, w, m, v, name, cb=256):
    rows, cols = w.shape
    nk = cols // 2 // cb

    def body(sc_ref, own_ref, sib_ref, w_ref, m_ref, v_ref, go_ref, d_ref, mo_ref, vo_ref):
        g_ = jnp.where(pl.program_id(0) == sc_ref[0], own_ref[...], sib_ref[...])
        go_ref[...] = g_
        d_ref[...], mo_ref[...], vo_ref[...] = _adamw(w_ref[...], g_, m_ref[...], v_ref[...])

    half = pl.BlockSpec((rows, cb), lambda h, k, sc_: (0, k))
    blk = pl.BlockSpec((rows, cb), lambda h, k, sc_: (0, h * nk + k))
    return _pc_sp(body, (2, nk), [half, half, blk, blk, blk], [blk] * 4, [SDS((rows, cols), F32)] * 4,
                  name)(sc, own, sib, w, m, v)


def adamw_small(g, w, m, v):
    def body(g_ref, w_ref, m_ref, v_ref, d_ref, mo_ref, vo_ref):
        d_ref[...], mo_ref[...], vo_ref[...] = _adamw(w_ref[...], g_ref[...], m_ref[...], v_ref[...])

    vm = pl.BlockSpec(memory_space=pltpu.VMEM)
    return _pc(body, name="adamw_small", in_specs=[vm] * 4, out_specs=[vm] * 3, out_shape=[SDS(g.shape, F32)] * 3,
               compiler_params=pltpu.CompilerParams(vmem_limit_bytes=VMEM_LIMIT))(g, w, m, v)


def _pos():
    return lax.axis_index("x"), lax.axis_index("y"), lax.axis_index("c")


def _other_chips(x, y):
    return [(1 - x, y), (x, 1 - y), (1 - x, 1 - y)]


_ANY = pl.BlockSpec(memory_space=pltpu.HBM)


class _Side:
    def __init__(self, ins, out_shapes, nsem, start, finish):
        self.ins, self.out_shapes, self.start, self.finish = list(ins), list(out_shapes), start, finish
        self.scratch = [pltpu.SemaphoreType.DMA((nsem,)), pltpu.SemaphoreType.DMA((nsem,))]
        self.n_in, self.n_out = len(self.ins), len(self.out_shapes)


def _run_side(side, name):
    def body(*refs):
        args_ = (refs[:side.n_in], refs[side.n_in:side.n_in + side.n_out], *refs[side.n_in + side.n_out:])
        side.start(*args_)
        side.finish(*args_)

    return _pc(body, name=name, in_specs=[_ANY] * side.n_in, out_specs=[_ANY] * side.n_out,
               out_shape=side.out_shapes, scratch_shapes=side.scratch)(*side.ins)


def gather_side(bigs, tinies):
    nb, nt_ = len(bigs), len(tinies)

    def plan(ins, outs, ssem, rsem):
        x, y, c = _pos()
        me = 2 * x + y
        chips = _other_chips(x, y)
        sibling = (x, y, 1 - c)

        def copy(k, src, dst, to):
            return pltpu.make_async_remote_copy(src_ref=src, dst_ref=dst, send_sem=ssem.at[k], recv_sem=rsem.at[k],
                                                device_id=to, device_id_type=MESH)

        sends, landed, passed_on, tiny_landed = [], [], [], []
        for w in range(nb):
            rh = bigs[w].shape[0] // 2
            mine = pl.ds(pl.multiple_of(c * rh, 16), rh)
            theirs = pl.ds(pl.multiple_of((1 - c) * rh, 16), rh)
            for j, (cx, cy) in enumerate(chips):
                sends.append(copy(6 * w + j, ins[w].at[mine], outs[w].at[me, mine], (cx, cy, c)))
                blk = outs[w].at[2 * cx + cy, mine]
                landed.append((copy(6 * w + j, blk, blk, (cx, cy, c)), copy(6 * w + 3 + j, blk, blk, sibling)))
                blk = outs[w].at[2 * cx + cy, theirs]
                passed_on.append(copy(6 * w + 3 + j, blk, blk, sibling))
        for w in range(nt_):
            for j, (cx, cy) in enumerate(chips):
                k = 6 * nb + 3 * w + j
                sends.append(copy(k, ins[nb + w], outs[nb + w].at[me], (cx, cy, c)))
                blk = outs[nb + w].at[2 * cx + cy]
                tiny_landed.append(copy(k, blk, blk, (cx, cy, c)))
        return sends, landed, passed_on, tiny_landed

    def start(ins, outs, ssem, rsem):
        for cp in plan(ins, outs, ssem, rsem)[0]:
            cp.start()

    def finish(ins, outs, ssem, rsem):
        sends, landed, passed_on, tiny_landed = plan(ins, outs, ssem, rsem)
        for arrived, forward in landed:
            arrived.wait_recv()
            forward.start()
        for arrived in tiny_landed + passed_on:
            arrived.wait_recv()
        for cp in sends + [forward for _, forward in landed]:
            cp.wait_send()

    return _Side(list(bigs) + list(tinies), [SDS((4,) + a.shape, a.dtype) for a in list(bigs) + list(tinies)],
                 6 * nb + 3 * nt_, start, finish)


def swap_halves(gs, name):
    n = len(gs)

    def body(*refs):
        g_refs, sib_refs = refs[:n], refs[n:2 * n]
        ssem, rsem = refs[2 * n:]
        x, y, c = _pos()
        cps = []
        for w in range(n):
            rh = gs[w].shape[1] // 2
            give = pl.ds(pl.multiple_of((1 - c) * rh, 16), rh)
            cp = pltpu.make_async_remote_copy(src_ref=g_refs[w].at[:, give], dst_ref=sib_refs[w], send_sem=ssem.at[w],
                                              recv_sem=rsem.at[w], device_id=(x, y, 1 - c), device_id_type=MESH)
            cp.start()
            cps.append(cp)
        for cp in cps:
            cp.wait()

    return _pc(body, name=name, in_specs=[_ANY] * n, out_specs=[_ANY] * n,
               out_shape=[SDS((g.shape[0], g.shape[1] // 2, g.shape[2]), g.dtype) for g in gs],
               scratch_shapes=[pltpu.SemaphoreType.DMA((n,)), pltpu.SemaphoreType.DMA((n,))])(*gs)


def add_half(sc, g, sib, name):
    l, r, cols = g.shape
    rh, br = _half_rows(r)
    nbk = rh // br

    def body(sc_ref, g_ref, s_ref, o_ref):
        o_ref[...] = (g_ref[...].astype(F32) + s_ref[...].astype(F32)).astype(BF)

    blk = pl.BlockSpec((1, br, cols), lambda j, i, sc_: (j, i, 0))
    return _pc_sp(body, (l, nbk), [pl.BlockSpec((1, br, cols), lambda j, i, sc_: (j, sc_[0] * nbk + i, 0)), blk], blk,
                  SDS((l, rh, cols), BF), name)(sc, g, sib)


def exchange_side(ps):
    n_ = len(ps)

    def width(p_):
        return p_.shape[2] if p_.shape[0] == 4 else p_.shape[2] // 4

    def plan(p_refs, got_refs, ssem, rsem):
        x, y, c = _pos()
        cps = []
        for w in range(n_):
            wd = width(ps[w])
            for j, (cx, cy) in enumerate(_other_chips(x, y)):
                to = 2 * cx + cy
                src = p_refs[w].at[to] if ps[w].shape[0] == 4 else p_refs[w].at[0, :, pl.ds(pl.multiple_of(to * wd, LANE), wd)]
                cps.append(pltpu.make_async_remote_copy(
                    src_ref=src, dst_ref=got_refs[w].at[j], send_sem=ssem.at[3 * w + j], recv_sem=rsem.at[3 * w + j],
                    device_id=(cx, cy, c), device_id_type=MESH))
        return cps

    def start(*refs):
        for cp in plan(*refs):
            cp.start()

    def finish(*refs):
        for cp in plan(*refs):
            cp.wait()

    return _Side(ps, [SDS((3, p_.shape[1], width(p_)), p_.dtype) for p_ in ps], 3 * n_, start, finish)


def sum4(sc, p, got, name):
    _, rh, wd = got.shape
    _, br = _half_rows(2 * rh)

    def body(sc_ref, p_ref, g_ref, r_ref):
        r_ref[...] = ((p_ref[0].astype(F32) + g_ref[0].astype(F32)) + (g_ref[1].astype(F32) + g_ref[2].astype(F32)))

    own = (pl.BlockSpec((1, br, wd), lambda i, sc_: (sc_[1], i, 0)) if p.shape[0] == 4
           else pl.BlockSpec((1, br, wd), lambda i, sc_: (0, i, sc_[1])))
    return _pc_sp(body, (rh // br,), [own, pl.BlockSpec((3, br, wd), lambda i, sc_: (0, i, 0))],
                  pl.BlockSpec((br, wd), lambda i, sc_: (i, 0)), SDS((rh, wd), F32), name)(sc, p, got)


def join_halves(halves):
    n = len(halves)

    def body(*refs):
        h_refs, got_refs = refs[:n], refs[n:2 * n]
        ssem, rsem = refs[2 * n:]
        x, y, c = _pos()
        cps = []
        for w in range(n):
            cp = pltpu.make_async_remote_copy(src_ref=h_refs[w], dst_ref=got_refs[w], send_sem=ssem.at[w],
                                              recv_sem=rsem.at[w], device_id=(x, y, 1 - c), device_id_type=MESH)
            cp.start()
            cps.append(cp)
        for cp in cps:
            cp.wait()

    return _pc(body, name="join_halves", in_specs=[_ANY] * n, out_specs=[_ANY] * n,
               out_shape=[SDS(h.shape, h.dtype) for h in halves],
               scratch_shapes=[pltpu.SemaphoreType.DMA((n,)), pltpu.SemaphoreType.DMA((n,))])(*halves)


def allreduce_small(g):
    rows = g.shape[0]
    rh = rows // 2

    def body(g_ref, out_ref, sib_buf, chip_buf, sum_sc, ssem, rsem):
        x, y, c = _pos()
        me = 2 * x + y
        sibling = (x, y, 1 - c)
        mine = pl.ds(pl.multiple_of(c * rh, 8), rh)

        def copy(k, src, dst, to):
            return pltpu.make_async_remote_copy(src_ref=src, dst_ref=dst, send_sem=ssem.at[k], recv_sem=rsem.at[k],
                                                device_id=to, device_id_type=MESH)

        cp = copy(0, g_ref, sib_buf, sibling)
        cp.start()
        cp.wait()
        sum_sc[...] = g_ref[...] + sib_buf[...]
        chips = _other_chips(x, y)
        cps = [copy(1 + j, sum_sc.at[mine], chip_buf.at[me], (cx, cy, c)) for j, (cx, cy) in enumerate(chips)]
        for cp in cps:
            cp.start()
        chip_buf[me] = sum_sc[mine, :]
        for j, (cx, cy) in enumerate(chips):
            copy(1 + j, sum_sc.at[mine], chip_buf.at[2 * cx + cy], (cx, cy, c)).wait_recv()
        for cp in cps:
            cp.wait_send()
        out_ref[mine, :] = (chip_buf[0] + chip_buf[1]) + (chip_buf[2] + chip_buf[3])
        cp = copy(4, out_ref.at[mine], out_ref.at[mine], sibling)
        cp.start()
        cp.wait()

    vm = pl.BlockSpec(memory_space=pltpu.VMEM)
    return _pc(body, name="allreduce_small", in_specs=[vm], out_specs=vm, out_shape=SDS((rows, LANE), F32),
               scratch_shapes=[pltpu.VMEM((rows, LANE), F32), pltpu.VMEM((4, rh, LANE), F32), pltpu.VMEM((rows, LANE), F32),
                               pltpu.SemaphoreType.DMA((5,)), pltpu.SemaphoreType.DMA((5,))],
               compiler_params=pltpu.CompilerParams(vmem_limit_bytes=VMEM_LIMIT))(g)


def _pack_small(entries, get):
    flat = jnp.concatenate([get(n).reshape(-1).astype(F32) for n, _ in entries])
    rows = -(-flat.shape[0] // (8 * LANE)) * 8
    return jnp.pad(flat, (0, rows * LANE - flat.shape[0])).reshape(rows, LANE)


def _unpack_small(entries, packed):
    out, off = {}, 0
    flat = packed.reshape(-1)
    for name, n in entries:
        out[name] = flat[off:off + n]
        off += n
    return out


def _cols_full(blk):
    return blk.transpose(1, 0, 2).reshape(blk.shape[1], 4 * blk.shape[2])


def kernel(x, p, pre_mix_norm, w_in, a_ln_g, a_ln_b, a_spatial_w, a_spatial_b, a_out, b_gk, b_gk_bias, b_out_norm, b_out, w_mix_out, post_mix_norm, pre_ffn_norm, w_up, conv_w, conv_b, w_down, post_ffn_norm, w_ple, w_ple_gate, post_ple_norm, loss_target, m_pre_mix_norm, m_w_in, m_a_ln_g, m_a_ln_b, m_a_spatial_w, m_a_spatial_b, m_a_out, m_b_gk, m_b_gk_bias, m_b_out_norm, m_b_out, m_w_mix_out, m_post_mix_norm, m_pre_ffn_norm, m_w_up, m_conv_w, m_conv_b, m_w_down, m_post_ffn_norm, m_w_ple, m_w_ple_gate, m_post_ple_norm, v_pre_mix_norm, v_w_in, v_a_ln_g, v_a_ln_b, v_a_spatial_w, v_a_spatial_b, v_a_out, v_b_gk, v_b_gk_bias, v_b_out_norm, v_b_out, v_w_mix_out, v_post_mix_norm, v_pre_ffn_norm, v_w_up, v_conv_w, v_conv_b, v_w_down, v_post_ffn_norm, v_w_ple, v_w_ple_gate, v_post_ple_norm):
    args = dict(locals())
    order = ['pre_mix_norm', 'w_in', 'a_ln_g', 'a_ln_b', 'a_spatial_w', 'a_spatial_b', 'a_out', 'b_gk', 'b_gk_bias',
             'b_out_norm', 'b_out', 'w_mix_out', 'post_mix_norm', 'pre_ffn_norm', 'w_up', 'conv_w', 'conv_b', 'w_down',
             'post_ffn_norm', 'w_ple', 'w_ple_gate', 'post_ple_norm']
    assert sorted(BIG + TINY + tuple(n for n, _ in SMALL)) == sorted(order)
    s = x.shape[1]
    xs = x.reshape(s, D)
    ps = p.reshape(s, PLE)
    tgt = loss_target.reshape(s, D)
    t_big = min(1024, s)
    t_mid = min(512, s)
    t_small = min(256, s)
    mx_, my_, mc_ = _pos()
    me = 2 * mx_ + my_
    sc = jnp.stack([mc_, me]).astype(jnp.int32)
    shard = lambda n: args[n].reshape(args[n].shape[1:])

    mine = {n: shard(n).astype(BF) for n in BIG}
    mine.update({n: shard(n) for n in TINY})
    fill = lambda names, gots: {n: lax.dynamic_update_slice(got, mine[n][None], (me, 0, 0)) for n, got in zip(names, gots)}
    first = ("w_in",) + TINY
    full = fill(first, _run_side(gather_side([mine["w_in"]], [mine[n] for n in TINY]), "gather_first"))
    wi = _cols_full(full["w_in"])
    seg = lambda a, b: wi[:, a:b]
    qk = [seg(1024 + h * B_HK, 1024 + (h + 1) * B_HK) for h in range(B_H)]
    kk = [seg(1536 + h * B_HK, 1536 + (h + 1) * B_HK) for h in range(B_H)]
    w_z = jnp.concatenate([seg(0, 1024)] + [m_ for h in range(B_H) for m_ in (qk[h], kk[h])]
                          + [seg(2048, 4096), seg(4112, 6160), seg(4096, 4112), jnp.zeros((D, LANE - B_RANK), BF)], axis=1)
    wgk = jnp.pad(_cols_full(full["b_gk"]).astype(BF), ((0, LANE - B_RANK), (0, 0)))
    w_conv = _cols_full(full["conv_w"])
    g1, g2, g3 = pre_mix_norm.reshape(1, D), post_mix_norm.reshape(1, D), pre_ffn_norm.reshape(1, D)
    g4, g5 = post_ffn_norm.reshape(1, D), post_ple_norm.reshape(1, D)
    ln_g, ln_b = a_ln_g.reshape(1, A_W), a_ln_b.reshape(1, A_W)
    w_s = a_spatial_w.reshape(A_G, A_C, A_C)
    bs_t = a_spatial_b.reshape(A_G, A_C).T
    gk_bias = b_gk_bias.reshape(1, B_H * B_HK)
    wn = b_out_norm.reshape(1, B_HV)
    cb = conv_b.reshape(1, 2 * D_FF)
    idx = jnp.arange(t_mid)
    ltri = ((idx[:, None] // B_C == idx[None, :] // B_C) & (idx[None, :] <= idx[:, None])).astype(BF)

    a, z, qk32, *gots = norm_matmul(xs, g1, w_z, D, t_big, "in_proj", nblk=6, f32_blk=1,
                                    side=gather_side([mine[n] for n in BIG[1:]], []))
    full.update(fill(BIG[1:], gots))
    w_aout, w_ple_f = _cols_full(full["a_out"]), _cols_full(full["w_ple"])
    w_bout, w_mix, w_pg = (full[n].reshape(D, D) for n in ("b_out", "w_mix_out", "w_ple_gate"))
    w_dn, w_up3 = full["w_down"].reshape(D_FF, D), full["w_up"]
    zl = mm_cols(a, w_z, 48, LANE, t_big, "lr_proj")
    sa = sgu_fwd(z, ln_g, ln_b, w_s, bs_t, t_mid)
    ob, o, states = gla_fwd(z, qk32, zl, wgk, gk_bias, wn, ltri, t_mid)
    ya, yb, mp, mx, h1 = mix_fwd(sa, ob, z, xs, w_aout, w_bout, w_mix, g2, t_small)
    c, up = norm_matmul(h1, g3, w_up3, 1408, t_big, "up_proj")
    ff, cg, cv = ffn_gate_fwd(up, w_conv, cb, t_small)
    f, h2, pg, pe, dy, loss = out_fwd(ff, h1, ps, tgt, w_dn, w_pg, w_ple_f, g4, g5, t_small)

    dh2, dpe, dpg, df, dff, gg5, gg4 = out_bwd(dy, pg, pe, f, g5, g4, w_pg, w_dn, t_small)
    dup_g, dup_v, gcw_g, gcw_v, gcb_g, gcb_v = ffn_gate_bwd(up, cg, cv, dff, w_conv, t_small)
    wu = lambda k: pl.BlockSpec((None, D, 1408), lambda i: (k, 0, 0), pipeline_mode=pl.Buffered(1))
    dh1, gg3 = nt_normbwd([(dup_g, 0, w_up3, wu(0)), (dup_g, 1, w_up3, wu(1)), (dup_v, 0, w_up3, wu(2)),
                           (dup_v, 1, w_up3, wu(3))], h1, g3, dh2, t_small, "up_bwd")
    dmx, dya, dyb, dga, dgb, dsa, dob, gg2 = mix_bwd(dh1, mx, z, ya, yb, g2, w_mix, w_aout, w_bout, t_small)
    duv, g_lng, g_lnb, g_ws, g_bst = sgu_bwd(z, dsa, ln_g, ln_b, w_s, bs_t, t_mid)

    grads = {
        "a_out": mm_tn(sa, dya, "dw_a_out")[None],
        "b_out": mm_tn(ob, dyb, "dw_b_out").reshape(4, D // 4, D),
        "w_mix_out": mm_tn(mp, dmx, "dw_mix").reshape(4, D // 4, D),
        "w_up": jnp.concatenate([mm_tn(c, dup_g, "dw_up_g"), mm_tn(c, dup_v, "dw_up_v")], axis=1)[None],
        "w_down": mm_tn(ff, df, "dw_down").reshape(4, D_FF // 4, D),
        "w_ple": mm_tn(ps, dpe, "dw_ple")[None],
        "w_ple_gate": mm_tn(h2, dpg, "dw_ple_gate").reshape(4, D // 4, D),
    }

    def chip_partials(names):
        gs = [grads[n] for n in names]
        return [add_half(sc, g, sib, "partial_" + n)
                for n, g, sib in zip(names, gs, swap_halves(gs, "swap_halves_" + names[0]))]

    parts = dict(zip(BIG[1:], chip_partials(BIG[1:])))
    dqk, dvb, dog, dpre, g_gkb, g_wn, *gots = gla_bwd(z, qk32, zl, o, dob, states, wgk, gk_bias, wn, ltri, ltri.T,
                                                      t_mid, side=exchange_side([parts[n] for n in BIG[1:]]))
    got = dict(zip(BIG[1:], gots))
    dlr = mm_nt_small(dpre, wgk, t_mid, "dlr")
    segs = [duv, dqk, dvb, dog, dga, dgb, dlr]

    gz = [mm_tn(a, sg_, "dw_in_%d" % k) for k, sg_ in enumerate(segs)]
    gq = [gz[1][:, h * 256:h * 256 + B_HK] for h in range(B_H)]
    gk = [gz[1][:, h * 256 + B_HK:(h + 1) * 256] for h in range(B_H)]
    g_in = jnp.concatenate([gz[0]] + gq + gk + [gz[2], gz[3], gz[6][:, :B_RANK], gz[4], gz[5]], axis=1)
    grads["w_in"] = g_in.reshape(D, 4, 1540).transpose(1, 0, 2)
    parts["w_in"], = chip_partials(("w_in",))
    wz = lambda k, width: pl.BlockSpec((D, width), lambda i: (0, k), pipeline_mode=pl.Buffered(1))
    dx, gg1, got["w_in"] = nt_normbwd(
        [(segs[k], 0, w_z, wz(k, D)) for k in range(6)] + [(dlr, 0, w_z, wz(48, LANE))], xs, g1, dh1, t_small,
        "in_bwd", side=exchange_side([parts["w_in"]]))

    reds = [sum4(sc, parts[n], got[n], "sum_" + n) for n in BIG]
    outs = {}
    for n, red, sib in zip(BIG, reds, join_halves(reds)):
        if n == "w_in":
            res = adamw_cols(sc, red.T, sib.T, shard(n).T, shard("m_" + n).T, shard("v_" + n).T, "adamw_" + n)
            res = [r_.T for r_ in res]
        else:
            res = adamw_halves(sc, red, sib, shard(n), shard("m_" + n), shard("v_" + n), "adamw_" + n)
        outs[n] = [r_.reshape(args[n].shape) for r_ in res]

    small_g = {
        "pre_mix_norm": gg1, "a_ln_g": g_lng, "a_ln_b": g_lnb, "a_spatial_w": g_ws, "a_spatial_b": g_bst.T,
        "b_gk_bias": g_gkb, "b_out_norm": g_wn, "post_mix_norm": gg2, "pre_ffn_norm": gg3,
        "conv_b": jnp.concatenate([gcb_g, gcb_v], axis=1), "post_ffn_norm": gg4, "post_ple_norm": gg5,
        "b_gk": mm_tn(zl, dpre, "dw_gk")[:B_RANK], "conv_w": jnp.concatenate([gcw_g, gcw_v], axis=1),
    }
    red_entries = SMALL + (("b_gk", B_RANK * 512), ("conv_w", 3 * 2 * D_FF))
    g_fin = _unpack_small(red_entries, allreduce_small(_pack_small(red_entries, lambda n: small_g[n])))
    g_fin["b_gk"] = lax.dynamic_slice(g_fin["b_gk"].reshape(B_RANK, 512), (0, me * B_HK), (B_RANK, B_HK))
    g_fin["conv_w"] = lax.dynamic_slice(g_fin["conv_w"].reshape(3, 2 * D_FF), (0, me * 1408), (3, 1408))
    upd_entries = SMALL + (("b_gk", B_RANK * B_HK), ("conv_w", 3 * 1408))
    res = adamw_small(*[_pack_small(upd_entries, get) for get in
                        (lambda n: g_fin[n], lambda n: args[n], lambda n: args["m_" + n], lambda n: args["v_" + n])])
    res = [_unpack_small(upd_entries, r_) for r_ in res]
    for n, _ in upd_entries:
        outs[n] = [r_[n].reshape(args[n].shape) for r_ in [g_fin] + res]

    total = lax.psum(loss[0, 0], ("x", "y", "c"))
    return (total, dx.reshape(x.shape), *[outs[n][0] for n in order], *[outs[n][1] for n in order],
            *[outs[n][2] for n in order], *[outs[n][3] for n in order])
```

```python
import functools
import math

import jax
import jax.numpy as jnp
from jax import lax
from jax.experimental import pallas as pl
from jax.experimental.pallas import tpu as pltpu

F32 = jnp.float32
BF = jnp.bfloat16
SDS = jax.ShapeDtypeStruct
MESH = pl.DeviceIdType.MESH

EPS = 1e-6
D = 1024
A_W = 512
A_G, A_C = 8, 128
A_GD = A_W // A_G
B_H, B_HK, B_HV = 4, 128, 256
B_C = 64
B_RANK = 16
D_FF = 2816
PLE = 256
ZW = 6272
LANE = 128
VMEM_LIMIT = 60 * 1024 * 1024

ADAM_LR, ADAM_B1, ADAM_B2, ADAM_EPS, ADAM_WD, ADAM_STEP = 0.001, 0.9, 0.999, 1e-08, 0.01, 10

_GC = math.sqrt(2.0 / math.pi)
_GA = 0.044715

BIG = ("w_in", "a_out", "b_out", "w_mix_out", "w_up", "w_down", "w_ple", "w_ple_gate")
TINY = ("b_gk", "conv_w")
SMALL = (("pre_mix_norm", 1024), ("a_ln_g", 512), ("a_ln_b", 512), ("a_spatial_w", 131072),
         ("a_spatial_b", 1024), ("b_gk_bias", 512), ("b_out_norm", 256), ("post_mix_norm", 1024),
         ("pre_ffn_norm", 1024), ("conv_b", 5632), ("post_ffn_norm", 1024), ("post_ple_norm", 1024))


def _pc(body, **kw):
    return pl.pallas_call(body, **kw)


def _cp(n):
    return pltpu.CompilerParams(dimension_semantics=("arbitrary",) * n, vmem_limit_bytes=VMEM_LIMIT)


def _const(shape):
    nd = len(shape)
    return pl.BlockSpec(shape, lambda *_: (0,) * nd, pipeline_mode=pl.Buffered(1))


def _acc(shape):
    nd = len(shape)
    return pl.BlockSpec(shape, lambda *_: (0,) * nd)


def _dot(a, b):
    return jnp.dot(a, b, preferred_element_type=F32)


def _dot_nt(a, b):
    return lax.dot_general(a, b, (((1,), (1,)), ((), ())), preferred_element_type=F32)


def _dot_tn(a, b):
    return lax.dot_general(a, b, (((0,), (0,)), ((), ())), preferred_element_type=F32)


def _gelu(x):
    return 0.5 * x * (1.0 + jnp.tanh(_GC * (x + _GA * x * x * x)))


def _gelu_and_grad(x):
    x2 = x * x
    s = 0.5 * jnp.tanh((_GC * x) * (1.0 + _GA * x2)) + 0.5
    g = x * s
    return g, s + g * (1.0 - s) * ((6.0 * _GC * _GA) * x2 + 2.0 * _GC)


def _log_sigmoid(x):
    return jnp.minimum(x, 0.0) - jnp.log(1.0 + jnp.exp(-jnp.abs(x)))


def _rms(x, g):
    return x * lax.rsqrt(jnp.mean(x * x, axis=-1, keepdims=True) + EPS) * g


def _rms_bwd(dy, x, g):
    r = lax.rsqrt(jnp.mean(x * x, axis=-1, keepdims=True) + EPS)
    n = x * r
    dn = dy * g
    dx = r * (dn - n * jnp.mean(dn * n, axis=-1, keepdims=True))
    return dx, jnp.sum(dy * n, axis=0, keepdims=True)


def _ldot3(l, x):
    h = x.astype(BF)
    r = x - h.astype(F32)
    m = r.astype(BF)
    lo = (r - m.astype(F32)).astype(BF)
    return _dot(l, h) + _dot(l, m) + _dot(l, lo)


def _split_side(refs, n_in, n_out, n_scratch, side):
    si, so = (side.n_in, side.n_out) if side else (0, 0)
    cuts = [n_in, si, n_out, so, n_scratch]
    out, at = [], 0
    for c in cuts:
        out.append(refs[at:at + c])
        at += c
    return (*out, refs[at:])


def _side_specs(side):
    return ([_ANY] * side.n_in, [_ANY] * side.n_out, side.out_shapes, side.scratch, side.ins) if side else ([],) * 5


def norm_matmul(x, g, w, bn, t, name, nblk=None, f32_blk=None, side=None):
    s, dm = x.shape
    if w.ndim == 3:
        nblk = w.shape[0]
        w_spec = pl.BlockSpec((None, dm, bn), lambda i, j: (j, 0, 0))
    else:
        nblk = nblk or w.shape[1] // bn
        w_spec = pl.BlockSpec((dm, bn), lambda i, j: (0, j))
    extra = f32_blk is not None
    nt = s // t

    def body(*refs):
        (x_ref, g_ref, w_ref), s_in, outs, s_out, (a_sc,), s_scr = _split_side(refs, 3, 2 + extra, 1, side)
        a_ref, z_ref = outs[:2]
        i, j = pl.program_id(0), pl.program_id(1)
        if side:
            @pl.when((i == 0) & (j == 0))
            def _():
                side.start(s_in, s_out, *s_scr)

        @pl.when(j == 0)
        def _():
            a = _rms(x_ref[...], g_ref[...]).astype(BF)
            a_sc[...] = a
            a_ref[...] = a

        acc = _dot(a_sc[...], w_ref[...])
        z_ref[...] = acc.astype(BF)
        if extra:
            @pl.when(j == f32_blk)
            def _():
                outs[2][...] = acc
        if side:
            @pl.when((i == nt - 1) & (j == nblk - 1))
            def _():
                side.finish(s_in, s_out, *s_scr)

    si_specs, so_specs, so_shapes, s_scratch, s_ins = _side_specs(side)
    return _pc(
        body, name=name, grid=(nt, nblk),
        in_specs=[pl.BlockSpec((t, dm), lambda i, j: (i, 0)), _const((1, dm)), w_spec] + si_specs,
        out_specs=[pl.BlockSpec((t, dm), lambda i, j: (i, 0)), pl.BlockSpec((t, bn), lambda i, j: (i, j))]
        + [pl.BlockSpec((t, bn), lambda i, j: (i, 0))] * extra + so_specs,
        out_shape=[SDS((s, dm), BF), SDS((s, nblk * bn), BF)] + [SDS((s, bn), F32)] * extra + so_shapes,
        scratch_shapes=[pltpu.VMEM((t, dm), BF)] + s_scratch, compiler_params=_cp(2))(x, g, w, *s_ins)


def _sgu_masked(ws_ref):
    r = lax.broadcasted_iota(jnp.int32, (A_C, A_C), 0)
    c = lax.broadcasted_iota(jnp.int32, (A_C, A_C), 1)
    return [jnp.where(c <= r, ws_ref[g], 0.0).astype(BF) for g in range(A_G)]


def _sgu_recompute(v, lng, lnb):
    gv, dgv = _gelu_and_grad(v)
    mu = jnp.mean(gv, axis=-1, keepdims=True)
    xc = gv - mu
    rstd = lax.rsqrt(jnp.mean(xc * xc, axis=-1, keepdims=True) + EPS)
    xhat = xc * rstd
    return dgv, rstd, xhat, (xhat * lng + lnb).astype(BF)


def sgu_fwd(z, ln_g, ln_b, w_s, bs_t, t):
    s = z.shape[0]

    def body(u_ref, v_ref, g_ref, b_ref, ws_ref, bs_ref, sa_ref, s_sc):
        wm = _sgu_masked(ws_ref)
        for ci in range(t // A_C):
            rows = pl.ds(ci * A_C, A_C)
            _, _, _, vn = _sgu_recompute(v_ref[rows, :].astype(F32), g_ref[...], b_ref[...])
            for g in range(A_G):
                cols = slice(g * A_GD, (g + 1) * A_GD)
                s_sc[:, cols] = _dot(wm[g], vn[:, cols]) + bs_ref[:, g:g + 1]
            sa_ref[rows, :] = (_gelu(u_ref[rows, :].astype(F32)) * s_sc[...]).astype(BF)

    return _pc(
        body, name="sgu_fwd", grid=(s // t,),
        in_specs=[pl.BlockSpec((t, A_W), lambda i: (i, 0)), pl.BlockSpec((t, A_W), lambda i: (i, 1)),
                  _const((1, A_W)), _const((1, A_W)), _const((A_G, A_C, A_C)), _const((A_C, A_G))],
        out_specs=pl.BlockSpec((t, A_W), lambda i: (i, 0)),
        out_shape=SDS((s, A_W), BF),
        scratch_shapes=[pltpu.VMEM((A_C, A_W), F32)], compiler_params=_cp(1))(z, z, ln_g, ln_b, w_s, bs_t)


def _gla_decays(qk_ref, lr_ref, wgk_ref, bias_ref, l_ref, t):
    nc = t // B_C
    q = qk_ref[:, :B_HK].astype(F32) * (B_HK ** -0.5)
    k = qk_ref[:, B_HK:].astype(F32)
    pre = _dot(lr_ref[...], wgk_ref[...]) + bias_ref[...]
    la = _log_sigmoid(pre) * (1.0 / 16.0)
    b = _ldot3(l_ref[...], la)
    b3 = b.reshape(nc, B_C, B_HK)
    bl = jnp.broadcast_to(b3[:, B_C - 1:B_C, :], (nc, B_C, B_HK)).reshape(t, B_HK)
    eb, enb, etb = jnp.exp(b), jnp.exp(-b), jnp.exp(bl - b)
    return pre, b, bl, eb, enb, etb, q * eb, k * enb, k * etb


def gla_fwd(z, qk32, zl, wgk, bias, wn, ltri, t):
    s = z.shape[0]
    nc = t // B_C

    def body(qk_ref, v_ref, og_ref, lr_ref, wgk_ref, bias_ref, wn_ref, l_ref, ob_ref, o_ref, st_ref, st_sc, o_sc):
        h = pl.program_id(1)

        @pl.when(pl.program_id(0) == 0)
        def _():
            st_sc[h] = jnp.zeros((B_HV, B_HK), F32)

        _, _, bl, _, _, _, qd, ki, kt = _gla_decays(qk_ref, lr_ref, wgk_ref, bias_ref, l_ref, t)
        qd, ki, kt = qd.astype(BF), ki.astype(BF), kt.astype(BF)
        vb = v_ref[...]
        sc = jnp.where(l_ref[...] > 0, _dot_nt(qd, ki), 0.0).astype(BF)
        o_sc[...] = _dot(sc, vb)
        for n in range(nc):
            rows = slice(n * B_C, (n + 1) * B_C)
            st = st_sc[h]
            stb = st.astype(BF)
            st_ref[n, 0] = stb
            o_sc[rows, :] += _dot_nt(qd[rows], stb)
            st_sc[h] = st * jnp.exp(bl[n * B_C:n * B_C + 1, :]) + _dot_tn(vb[rows], kt[rows])
        ob = o_sc[...].astype(BF)
        o_ref[...] = ob
        og = og_ref[...].astype(F32)
        ob_ref[...] = (_rms(ob.astype(F32), wn_ref[...]) * og * jax.nn.sigmoid(og)).astype(BF)

    return _pc(
        body, name="gla_fwd", grid=(s // t, B_H),
        in_specs=[pl.BlockSpec((t, 256), lambda i, h: (i, h)), pl.BlockSpec((t, 256), lambda i, h: (i, 8 + h)),
                  pl.BlockSpec((t, 256), lambda i, h: (i, 12 + h)), pl.BlockSpec((t, LANE), lambda i, h: (i, 0)),
                  pl.BlockSpec((LANE, B_HK), lambda i, h: (0, h)), pl.BlockSpec((1, B_HK), lambda i, h: (0, h)),
                  _const((1, B_HV)), _const((t, t))],
        out_specs=[pl.BlockSpec((t, B_HV), lambda i, h: (i, h)), pl.BlockSpec((t, B_HV), lambda i, h: (i, h)),
                   pl.BlockSpec((nc, 1, B_HV, B_HK), lambda i, h: (i, h, 0, 0))],
        out_shape=[SDS((s, D), BF), SDS((s, D), BF), SDS((s // B_C, B_H, B_HV, B_HK), BF)],
        scratch_shapes=[pltpu.VMEM((B_H, B_HV, B_HK), F32), pltpu.VMEM((t, B_HV), F32)],
        compiler_params=_cp(2))(qk32, z, z, zl, wgk, bias, wn, ltri)


def mix_fwd(sa, ob, z, x, a_out, b_out, w_mix, g2, t):
    s = x.shape[0]

    def body(sa_ref, ob_ref, ga_ref, gb_ref, x_ref, ao_ref, bo_ref, wm_ref, g2_ref,
             ya_ref, yb_ref, mp_ref, mx_ref, h1_ref):
        ya = _dot(sa_ref[...], ao_ref[...]).astype(BF)
        yb = _dot(ob_ref[...], bo_ref[...]).astype(BF)
        ya_ref[...] = ya
        yb_ref[...] = yb
        mp = (jax.nn.sigmoid(ga_ref[...].astype(F32)) * ya.astype(F32)
              + jax.nn.sigmoid(gb_ref[...].astype(F32)) * yb.astype(F32)).astype(BF)
        mp_ref[...] = mp
        mx = _dot(mp, wm_ref[...]).astype(BF)
        mx_ref[...] = mx
        h1_ref[...] = x_ref[...] + _rms(mx.astype(F32), g2_ref[...])

    row = lambda w: pl.BlockSpec((t, w), lambda i: (i, 0))
    return _pc(
        body, name="mix_fwd", grid=(s // t,),
        in_specs=[row(A_W), row(D), pl.BlockSpec((t, D), lambda i: (i, 4)), pl.BlockSpec((t, D), lambda i: (i, 5)),
                  row(D), _const((A_W, D)), _const((D, D)), _const((D, D)), _const((1, D))],
        out_specs=[row(D)] * 5,
        out_shape=[SDS((s, D), BF)] * 4 + [SDS((s, D), F32)],
        compiler_params=_cp(1))(sa, ob, z, z, x, a_out, b_out, w_mix, g2)


def ffn_gate_fwd(up, conv_w, conv_b, t):
    s = up.shape[0]
    bn = 1408
    hb = t // 8

    def body(ug_ref, uv_ref, hg_ref, hv_ref, wg_ref, wv_ref, bg_ref, bv_ref, ff_ref, cg_ref, cv_ref):
        live = (pl.program_id(1) > 0).astype(F32)

        def branch(u_ref, h_ref, w_ref, b_ref, c_ref):
            ext = jnp.concatenate([h_ref[...].astype(F32) * live, u_ref[...].astype(F32)], axis=0)
            w = w_ref[...]
            c = (b_ref[...] + w[0:1] * pltpu.roll(ext, 2, 0) + w[1:2] * pltpu.roll(ext, 1, 0) + w[2:3] * ext)[8:]
            c = c.astype(BF)
            c_ref[...] = c
            return c.astype(F32)

        g = _gelu(branch(ug_ref, hg_ref, wg_ref, bg_ref, cg_ref))
        ff_ref[...] = (g * branch(uv_ref, hv_ref, wv_ref, bv_ref, cv_ref)).astype(BF)

    halo = lambda off: pl.BlockSpec((8, bn), lambda j, i: (jnp.maximum(i * hb - 1, 0), j + off))
    out = pl.BlockSpec((t, bn), lambda j, i: (i, j))
    return _pc(
        body, name="ffn_gate_fwd", grid=(2, s // t),
        in_specs=[pl.BlockSpec((t, bn), lambda j, i: (i, j)), pl.BlockSpec((t, bn), lambda j, i: (i, j + 2)),
                  halo(0), halo(2),
                  pl.BlockSpec((3, bn), lambda j, i: (0, j)), pl.BlockSpec((3, bn), lambda j, i: (0, j + 2)),
                  pl.BlockSpec((1, bn), lambda j, i: (0, j)), pl.BlockSpec((1, bn), lambda j, i: (0, j + 2))],
        out_specs=[out] * 3, out_shape=[SDS((s, D_FF), BF)] * 3,
        compiler_params=_cp(2))(up, up, up, up, conv_w, conv_w, conv_b, conv_b)


def out_fwd(ff, h1, p, tgt, w_down, w_pg, w_ple, g4, g5, t):
    s = h1.shape[0]

    def body(ff_ref, h1_ref, p_ref, t_ref, wd_ref, wpg_ref, wpl_ref, g4_ref, g5_ref,
             f_ref, h2_ref, pg_ref, pe_ref, dy_ref, loss_ref):
        @pl.when(pl.program_id(0) == 0)
        def _():
            loss_ref[...] = jnp.zeros((1, 1), F32)

        f = _dot(ff_ref[...], wd_ref[...]).astype(BF)
        f_ref[...] = f
        h2 = h1_ref[...] + _rms(f.astype(F32), g4_ref[...])
        h2b = h2.astype(BF)
        h2_ref[...] = h2b
        pg = _dot(h2b, wpg_ref[...]).astype(BF)
        pe = _dot(p_ref[...].astype(BF), wpl_ref[...]).astype(BF)
        pg_ref[...] = pg
        pe_ref[...] = pe
        y = h2 + _rms(jax.nn.sigmoid(pg.astype(F32)) * pe.astype(F32), g5_ref[...])
        err = y - t_ref[...]
        dy_ref[...] = err * (1.0 / D)
        loss_ref[...] += (0.5 / D) * jnp.sum(err * err)

    row = lambda w: pl.BlockSpec((t, w), lambda i: (i, 0))
    return _pc(
        body, name="out_fwd", grid=(s // t,),
        in_specs=[row(D_FF), row(D), row(PLE), row(D), _const((D_FF, D)), _const((D, D)), _const((PLE, D)),
                  _const((1, D)), _const((1, D))],
        out_specs=[row(D)] * 5 + [_acc((1, 1))],
        out_shape=[SDS((s, D), BF)] * 4 + [SDS((s, D), F32), SDS((1, 1), F32)],
        compiler_params=_cp(1))(ff, h1, p, tgt, w_down, w_pg, w_ple, g4, g5)


def out_bwd(dy, pg, pe, f, g5, g4, w_pg, w_down, t):
    s = dy.shape[0]

    def body(dy_ref, pg_ref, pe_ref, f_ref, g5_ref, g4_ref, wpg_ref, wd_ref,
             dh2_ref, dpe_ref, dpg_ref, df_ref, dff_ref, gg5_ref, gg4_ref):
        @pl.when(pl.program_id(0) == 0)
        def _():
            gg5_ref[...] = jnp.zeros((1, D), F32)
            gg4_ref[...] = jnp.zeros((1, D), F32)

        dy_ = dy_ref[...]
        pg_ = pg_ref[...].astype(F32)
        pe_ = pe_ref[...].astype(F32)
        sg = jax.nn.sigmoid(pg_)
        dple, dg5 = _rms_bwd(dy_, sg * pe_, g5_ref[...])
        gg5_ref[...] += dg5
        dpe_ref[...] = (dple * sg).astype(BF)
        dpg = (dple * pe_ * sg * (1.0 - sg)).astype(BF)
        dpg_ref[...] = dpg
        dh2 = dy_ + _dot_nt(dpg, wpg_ref[...])
        dh2_ref[...] = dh2
        df, dg4 = _rms_bwd(dh2, f_ref[...].astype(F32), g4_ref[...])
        gg4_ref[...] += dg4
        dfb = df.astype(BF)
        df_ref[...] = dfb
        dff_ref[...] = _dot_nt(dfb, wd_ref[...]).astype(BF)

    row = lambda w: pl.BlockSpec((t, w), lambda i: (i, 0))
    return _pc(
        body, name="out_bwd", grid=(s // t,),
        in_specs=[row(D), row(D), row(D), row(D), _const((1, D)), _const((1, D)), _const((D, D)), _const((D_FF, D))],
        out_specs=[row(D), row(D), row(D), row(D), row(D_FF), _acc((1, D)), _acc((1, D))],
        out_shape=[SDS((s, D), F32), SDS((s, D), BF), SDS((s, D), BF), SDS((s, D), BF), SDS((s, D_FF), BF),
                   SDS((1, D), F32), SDS((1, D), F32)],
        compiler_params=_cp(1))(dy, pg, pe, f, g5, g4, w_pg, w_down)


def ffn_gate_bwd(up, cg, cv, dff, conv_w, t):
    s = up.shape[0]
    bn = 1408
    hb = t // 8
    nt = s // t
    r = t + 8

    def body(ug_ref, uv_ref, cg_ref, cv_ref, cag_ref, cav_ref, d_ref, da_ref, wg_ref, wv_ref,
             dug_ref, duv_ref, gwg_ref, gwv_ref, gbg_ref, gbv_ref):
        i = pl.program_id(1)

        @pl.when(i == 0)
        def _():
            gwg_ref[...] = jnp.zeros((3, bn), F32)
            gwv_ref[...] = jnp.zeros((3, bn), F32)
            gbg_ref[...] = jnp.zeros((1, bn), F32)
            gbv_ref[...] = jnp.zeros((1, bn), F32)

        def gate(c_g, c_v, d_):
            gl, dgl = _gelu_and_grad(c_g.astype(F32))
            d_ = d_.astype(F32)
            return d_ * c_v.astype(F32) * dgl, d_ * gl

        dg, dv = gate(cg_ref[...], cv_ref[...], d_ref[...])
        nxt = da_ref[...].astype(F32) * (i < nt - 1).astype(F32)
        dg_n, dv_n = gate(cag_ref[...], cav_ref[...], nxt)

        def back(dc, dc_next, u_ref, w_ref, du_ref, gw_ref, gb_ref):
            w = w_ref[...]
            d_ext = jnp.concatenate([dc, dc_next], axis=0)
            d1, d2 = pltpu.roll(d_ext, r - 1, 0)[:t], pltpu.roll(d_ext, r - 2, 0)[:t]
            du_ref[...] = (w[2:3] * dc + w[1:2] * d1 + w[0:1] * d2).astype(BF)
            u = u_ref[...].astype(F32)
            gw_ref[0:1, :] += jnp.sum(d2 * u, axis=0, keepdims=True)
            gw_ref[1:2, :] += jnp.sum(d1 * u, axis=0, keepdims=True)
            gw_ref[2:3, :] += jnp.sum(dc * u, axis=0, keepdims=True)
            gb_ref[...] += jnp.sum(dc, axis=0, keepdims=True)

        back(dg, dg_n, ug_ref, wg_ref, dug_ref, gwg_ref, gbg_ref)
        back(dv, dv_n, uv_ref, wv_ref, duv_ref, gwv_ref, gbv_ref)

    tile = lambda off: pl.BlockSpec((t, bn), lambda j, i: (i, j + off))
    after = lambda off: pl.BlockSpec((8, bn), lambda j, i: (jnp.minimum((i + 1) * hb, nt * hb - 1), j + off))
    cw = lambda off: pl.BlockSpec((3, bn), lambda j, i: (0, j + off))
    cb = lambda off: pl.BlockSpec((1, bn), lambda j, i: (0, j + off))
    return _pc(
        body, name="ffn_gate_bwd", grid=(2, nt),
        in_specs=[tile(0), tile(2), tile(0), tile(0), after(0), after(0), tile(0), after(0), cw(0), cw(2)],
        out_specs=[tile(0), tile(0), cw(0), cw(0), cb(0), cb(0)],
        out_shape=[SDS((s, D_FF), BF), SDS((s, D_FF), BF), SDS((3, D_FF), F32), SDS((3, D_FF), F32),
                   SDS((1, D_FF), F32), SDS((1, D_FF), F32)],
        compiler_params=_cp(2))(up, up, cg, cv, cg, cv, dff, dff, conv_w, conv_w)


def nt_normbwd(pairs, xin, gain, dres, t, name, side=None):
    s = xin.shape[0]
    nt = s // t
    np_ = len(pairs)
    dys = [p_[0] for p_ in pairs]
    ws = [p_[2] for p_ in pairs]

    def body(*refs):
        ins_, s_in, (dx_ref, gg_ref), s_out, _, s_scr = _split_side(refs, 2 * np_ + 3, 2, 0, side)
        dy_refs, w_refs = ins_[:np_], ins_[np_:2 * np_]
        x_ref, g_ref, dres_ref = ins_[2 * np_:]
        i = pl.program_id(0)

        @pl.when(i == 0)
        def _():
            gg_ref[...] = jnp.zeros((1, D), F32)
            if side:
                side.start(s_in, s_out, *s_scr)

        acc = _dot_nt(dy_refs[0][...], w_refs[0][...])
        for k in range(1, np_):
            acc += _dot_nt(dy_refs[k][...], w_refs[k][...])
        dxn, dg = _rms_bwd(acc, x_ref[...], g_ref[...])
        dx_ref[...] = dres_ref[...] + dxn
        gg_ref[...] += dg
        if side:
            @pl.when(i == nt - 1)
            def _():
                side.finish(s_in, s_out, *s_scr)

    row = lambda w: pl.BlockSpec((t, w), lambda i: (i, 0))
    dy_spec = lambda blk, width: pl.BlockSpec((t, width), lambda i: (i, blk))
    si_specs, so_specs, so_shapes, s_scratch, s_ins = _side_specs(side)
    return _pc(
        body, name=name, grid=(nt,),
        in_specs=[dy_spec(p_[1], p_[3].block_shape[-1]) for p_ in pairs] + [p_[3] for p_ in pairs]
        + [row(D), _const((1, D)), row(D)] + si_specs,
        out_specs=[row(D), _acc((1, D))] + so_specs,
        out_shape=[SDS((s, D), F32), SDS((1, D), F32)] + so_shapes, scratch_shapes=s_scratch,
        compiler_params=_cp(1))(*dys, *ws, xin, gain, dres, *s_ins)


def mix_bwd(dh1, mx, z, ya, yb, g2, w_mix, a_out, b_out, t):
    s = dh1.shape[0]

    def body(dh_ref, mx_ref, ga_ref, gb_ref, ya_ref, yb_ref, g2_ref, wm_ref, ao_ref, bo_ref,
             dmx_ref, dya_ref, dyb_ref, dga_ref, dgb_ref, dsa_ref, dob_ref, gg2_ref):
        @pl.when(pl.program_id(0) == 0)
        def _():
            gg2_ref[...] = jnp.zeros((1, D), F32)

        dmx, dg2 = _rms_bwd(dh_ref[...], mx_ref[...].astype(F32), g2_ref[...])
        gg2_ref[...] += dg2
        dmxb = dmx.astype(BF)
        dmx_ref[...] = dmxb
        dmp = _dot_nt(dmxb, wm_ref[...])

        def gate(g_ref, y_ref, dy_ref, dg_ref, w_ref, dz_ref):
            sg = jax.nn.sigmoid(g_ref[...].astype(F32))
            dyb_ = (dmp * sg).astype(BF)
            dy_ref[...] = dyb_
            dg_ref[...] = (dmp * y_ref[...].astype(F32) * sg * (1.0 - sg)).astype(BF)
            dz_ref[...] = _dot_nt(dyb_, w_ref[...]).astype(BF)

        gate(ga_ref, ya_ref, dya_ref, dga_ref, ao_ref, dsa_ref)
        gate(gb_ref, yb_ref, dyb_ref, dgb_ref, bo_ref, dob_ref)

    row = lambda w: pl.BlockSpec((t, w), lambda i: (i, 0))
    return _pc(
        body, name="mix_bwd", grid=(s // t,),
        in_specs=[row(D), row(D), pl.BlockSpec((t, D), lambda i: (i, 4)), pl.BlockSpec((t, D), lambda i: (i, 5)),
                  row(D), row(D), _const((1, D)), _const((D, D)), _const((A_W, D)), _const((D, D))],
        out_specs=[row(D)] * 5 + [row(A_W), row(D), _acc((1, D))],
        out_shape=[SDS((s, D), BF)] * 5 + [SDS((s, A_W), BF), SDS((s, D), BF), SDS((1, D), F32)],
        compiler_params=_cp(1))(dh1, mx, z, z, ya, yb, g2, w_mix, a_out, b_out)


def sgu_bwd(z, dsa, ln_g, ln_b, w_s, bs_t, t):
    s = z.shape[0]
    nt = s // t

    def body(u_ref, v_ref, dsa_ref, g_ref, b_ref, ws_ref, bs_ref,
             duv_ref, glg_ref, glb_ref, gws_ref, gbs_ref, s_sc, dvn_sc, ds_acc):
        i = pl.program_id(0)

        @pl.when(i == 0)
        def _():
            glg_ref[...] = jnp.zeros((1, A_W), F32)
            glb_ref[...] = jnp.zeros((1, A_W), F32)
            gws_ref[...] = jnp.zeros((A_G, A_C, A_C), F32)
            ds_acc[...] = jnp.zeros((A_C, A_W), F32)

        wm = _sgu_masked(ws_ref)
        rr =lax.broadcasted_iota(jnp.int32, (A_C, A_C), 0)
        cc = lax.broadcasted_iota(jnp.int32, (A_C, A_C), 1)
        tril = cc <= rr
        lng = g_ref[...]
        for ci in range(t // A_C):
            rows = pl.ds(ci * A_C, A_C)
            dgv, rstd, xhat, vn = _sgu_recompute(v_ref[rows, :].astype(F32), lng, b_ref[...])
            for g in range(A_G):
                cols = slice(g * A_GD, (g + 1) * A_GD)
                s_sc[:, cols] = _dot(wm[g], vn[:, cols]) + bs_ref[:, g:g + 1]
            gu, dgu = _gelu_and_grad(u_ref[rows, :].astype(F32))
            dsa_ = dsa_ref[rows, :].astype(F32)
            ds = dsa_ * gu
            ds_acc[...] += ds
            dsb = ds.astype(BF)
            for g in range(A_G):
                cols = slice(g * A_GD, (g + 1) * A_GD)
                gws_ref[g] += jnp.where(tril, _dot_nt(dsb[:, cols], vn[:, cols]), 0.0)
                dvn_sc[:, cols] = _dot_tn(wm[g], dsb[:, cols])
            dvn = dvn_sc[...]
            glb_ref[...] += jnp.sum(dvn, axis=0, keepdims=True)
            glg_ref[...] += jnp.sum(dvn * xhat, axis=0, keepdims=True)
            dxh = dvn * lng
            dgv_ = rstd * (dxh - jnp.mean(dxh, axis=-1, keepdims=True)
                           - xhat * jnp.mean(dxh * xhat, axis=-1, keepdims=True))
            duv_ref[rows, :A_W] = (dsa_ * s_sc[...] * dgu).astype(BF)
            duv_ref[rows, A_W:] = (dgv_ * dgv).astype(BF)

        @pl.when(i == nt - 1)
        def _():
            acc = ds_acc[...]
            for g in range(A_G):
                gbs_ref[:, g:g + 1] = jnp.sum(acc[:, g * A_GD:(g + 1) * A_GD], axis=1, keepdims=True)

    return _pc(
        body, name="sgu_bwd", grid=(nt,),
        in_specs=[pl.BlockSpec((t, A_W), lambda i: (i, 0)), pl.BlockSpec((t, A_W), lambda i: (i, 1)),
                  pl.BlockSpec((t, A_W), lambda i: (i, 0)),
                  _const((1, A_W)), _const((1, A_W)), _const((A_G, A_C, A_C)), _const((A_C, A_G))],
        out_specs=[pl.BlockSpec((t, D), lambda i: (i, 0)), _acc((1, A_W)), _acc((1, A_W)),
                   _acc((A_G, A_C, A_C)), _acc((A_C, A_G))],
        out_shape=[SDS((s, D), BF), SDS((1, A_W), F32), SDS((1, A_W), F32), SDS((A_G, A_C, A_C), F32),
                   SDS((A_C, A_G), F32)],
        scratch_shapes=[pltpu.VMEM((A_C, A_W), F32), pltpu.VMEM((A_C, A_W), F32), pltpu.VMEM((A_C, A_W), F32)],
        compiler_params=_cp(1))(z, z, dsa, ln_g, ln_b, w_s, bs_t)


def gla_bwd(z, qk32, zl, o, dob, states, wgk, bias, wn, ltri, ltri_t, t, side=None):
    s = z.shape[0]
    nt = s // t
    nc = t // B_C

    def body(*refs):
        ins_, s_in, outs_, s_out, scr_, s_scr = _split_side(refs, 12, 6, 5, side)
        qk_ref, v_ref, og_ref, lr_ref, o_ref, dob_ref, st_ref, wgk_ref, bias_ref, wn_ref, l_ref, lt_ref = ins_
        dqk_ref, dv_ref, dog_ref, dpre_ref, gbias_ref, gwn_ref = outs_
        dst_sc, dv_sc, dqd_sc, dkt_sc, ddec_sc = scr_
        i = pl.program_id(0)
        h = pl.program_id(1)

        @pl.when((i == 0) & (h == 0))
        def _():
            gbias_ref[...] = jnp.zeros((B_H, 1, B_HK), F32)
            gwn_ref[...] = jnp.zeros((1, B_HV), F32)
            if side:
                side.start(s_in, s_out, *s_scr)

        @pl.when(i == 0)
        def _():
            dst_sc[h] = jnp.zeros((B_HV, B_HK), F32)

        pre, b, bl, eb, enb, etb, qd, ki, kt = _gla_decays(qk_ref, lr_ref, wgk_ref, bias_ref, l_ref, t)
        qdb, kib, ktb = qd.astype(BF), ki.astype(BF), kt.astype(BF)
        vb = v_ref[...]
        o_ = o_ref[...].astype(F32)
        og = og_ref[...].astype(F32)
        sog = jax.nn.sigmoid(og)
        dob_ = dob_ref[...].astype(F32)
        wn_ = wn_ref[...]
        don = dob_ * og * sog
        do, dwn = _rms_bwd(don, o_, wn_)
        gwn_ref[...] += dwn
        dog_ref[...] = (dob_ * _rms(o_, wn_) * sog * (1.0 + og * (1.0 - sog))).astype(BF)
        dob16 = do.astype(BF)
        keep, keep_t = l_ref[...] > 0, lt_ref[...] > 0
        sc_t = jnp.where(keep_t, _dot_nt(kib, qdb), 0.0).astype(BF)
        dsc = jnp.where(keep, _dot_nt(dob16, vb), 0.0).astype(BF)
        dsc_t = jnp.where(keep_t, _dot_nt(vb, dob16), 0.0).astype(BF)
        dv_sc[...] = _dot(sc_t, dob16)
        dqd_sc[...] = _dot(dsc, kib)
        dki = _dot(dsc_t, qdb)
        for n in reversed(range(nc)):
            rows = slice(n * B_C, (n + 1) * B_C)
            dst = dst_sc[h]
            dstb = dst.astype(BF)
            stp = st_ref[n, 0]
            dv_sc[rows, :] += _dot_nt(ktb[rows], dstb)
            dkt_sc[rows, :] = _dot(vb[rows], dstb)
            dqd_sc[rows, :] += _dot(dob16[rows], stp)
            dec = jnp.exp(bl[n * B_C:n * B_C + 1, :])
            ddec_sc[n] = jnp.sum(dst * stp.astype(F32), axis=0, keepdims=True) * dec
            dst_sc[h] = dst * dec + _dot_tn(dob16[rows], qdb[rows])
        dqd, dkt = dqd_sc[...], dkt_sc[...]
        dv_ref[...] = dv_sc[...].astype(BF)
        dqk_ref[:, :B_HK] = (dqd * eb * (B_HK ** -0.5)).astype(BF)
        dqk_ref[:, B_HK:] = (dki * enb + dkt * etb).astype(BF)
        dktkt = dkt * kt
        db3 = (dqd * qd - dki * ki - dktkt).reshape(nc, B_C, B_HK)
        dbl = jnp.sum(dktkt.reshape(nc, B_C, B_HK), axis=1, keepdims=True) + ddec_sc[...]
        last = lax.broadcasted_iota(jnp.int32, (nc, B_C, B_HK), 1) == B_C - 1
        db = (db3 + jnp.where(last, dbl, 0.0)).reshape(t, B_HK)
        dla = _ldot3(lt_ref[...], db)
        dpre = dla * (1.0 / 16.0) * (1.0 - jax.nn.sigmoid(pre))
        dpre_ref[...] = dpre.astype(BF)
        gbias_ref[h] += jnp.sum(dpre, axis=0, keepdims=True)
        if side:
            @pl.when((i == nt - 1) & (h == B_H - 1))
            def _():
                side.finish(s_in, s_out, *s_scr)

    rv = lambda i: nt - 1 - i
    si_specs, so_specs, so_shapes, s_scratch, s_ins = _side_specs(side)
    return _pc(
        body, name="gla_bwd", grid=(nt, B_H),
        in_specs=[pl.BlockSpec((t, 256), lambda i, h: (rv(i), h)), pl.BlockSpec((t, 256), lambda i, h: (rv(i), 8 + h)),
                  pl.BlockSpec((t, 256), lambda i, h: (rv(i), 12 + h)), pl.BlockSpec((t, LANE), lambda i, h: (rv(i), 0)),
                  pl.BlockSpec((t, B_HV), lambda i, h: (rv(i), h)), pl.BlockSpec((t, B_HV), lambda i, h: (rv(i), h)),
                  pl.BlockSpec((nc, 1, B_HV, B_HK), lambda i, h: (rv(i), h, 0, 0)),
                  pl.BlockSpec((LANE, B_HK), lambda i, h: (0, h)), pl.BlockSpec((1, B_HK), lambda i, h: (0, h)),
                  _const((1, B_HV)), _const((t, t)), _const((t, t))] + si_specs,
        out_specs=[pl.BlockSpec((t, 256), lambda i, h: (rv(i), h)), pl.BlockSpec((t, B_HV), lambda i, h: (rv(i), h)),
                   pl.BlockSpec((t, B_HV), lambda i, h: (rv(i), h)), pl.BlockSpec((t, B_HK), lambda i, h: (rv(i), h)),
                   _acc((B_H, 1, B_HK)), _acc((1, B_HV))] + so_specs,
        out_shape=[SDS((s, D), BF), SDS((s, D), BF), SDS((s, D), BF), SDS((s, B_H * B_HK), BF),
                   SDS((B_H, 1, B_HK), F32), SDS((1, B_HV), F32)] + so_shapes,
        scratch_shapes=[pltpu.VMEM((B_H, B_HV, B_HK), F32), pltpu.VMEM((t, B_HV), F32), pltpu.VMEM((t, B_HK), F32),
                        pltpu.VMEM((t, B_HK), F32), pltpu.VMEM((nc, 1, B_HK), F32)] + s_scratch,
        compiler_params=_cp(2))(qk32, z, z, zl, o, dob, states, wgk, bias, wn, ltri, ltri_t, *s_ins)


def mm_tn(a, b, name, tk=2048):
    s, m = a.shape
    n = b.shape[1]
    bn = next(c for c in (1024, 1408, 512, 256, 128) if n % c == 0 and m * c * 4 <= 6 * 1024 * 1024)
    tk = min(tk, s)
    nk = s // tk

    def body(a_ref, b_ref, o_ref, acc):
        k = pl.program_id(1)

        @pl.when(k == 0)
        def _():
            acc[...] = jnp.zeros((m, bn), F32)

        acc[...] += _dot_tn(a_ref[...].astype(BF), b_ref[...])

        @pl.when(k == nk - 1)
        def _():
            o_ref[...] = acc[...].astype(BF)

    return _pc(
        body, name=name, grid=(n // bn, nk),
        in_specs=[pl.BlockSpec((tk, m), lambda j, k: (k, 0)), pl.BlockSpec((tk, bn), lambda j, k: (k, j))],
        out_specs=pl.BlockSpec((m, bn), lambda j, k: (0, j)),
        out_shape=SDS((m, n), BF), scratch_shapes=[pltpu.VMEM((m, bn), F32)], compiler_params=_cp(2))(a, b)


def mm_cols(a, w, blk, width, t, name):
    s, k = a.shape

    def body(a_ref, w_ref, o_ref):
        o_ref[...] = _dot(a_ref[...], w_ref[...]).astype(BF)

    return _pc(body, name=name, grid=(s // t,),
               in_specs=[pl.BlockSpec((t, k), lambda i: (i, 0)),
                         pl.BlockSpec((k, width), lambda i: (0, blk), pipeline_mode=pl.Buffered(1))],
               out_specs=pl.BlockSpec((t, width), lambda i: (i, 0)), out_shape=SDS((s, width), BF),
               compiler_params=_cp(1))(a, w)


def mm_nt_small(a, w, t, name):
    s, k = a.shape
    n = w.shape[0]

    def body(a_ref, w_ref, o_ref):
        o_ref[...] = _dot_nt(a_ref[...], w_ref[...]).astype(BF)

    return _pc(body, name=name, grid=(s // t,),
               in_specs=[pl.BlockSpec((t, k), lambda i: (i, 0)), _const((n, k))],
               out_specs=pl.BlockSpec((t, n), lambda i: (i, 0)), out_shape=SDS((s, n), BF),
               compiler_params=_cp(1))(a, w)


def _adamw(w, g, m, v):
    m = ADAM_B1 * m + (1.0 - ADAM_B1) * g
    v = ADAM_B2 * v + (1.0 - ADAM_B2) * (g * g)
    m_hat = m / (1.0 - ADAM_B1 ** ADAM_STEP)
    v_hat = v / (1.0 - ADAM_B2 ** ADAM_STEP)
    return -ADAM_LR * (m_hat / (jnp.sqrt(v_hat) + ADAM_EPS) + ADAM_WD * w), m, v


def _half_rows(rows):
    rh = rows // 2
    return rh, max(b for b in range(16, 257, 16) if rh % b == 0)


def _pc_sp(body, grid, in_specs, out_specs, out_shape, name):
    gs = pltpu.PrefetchScalarGridSpec(num_scalar_prefetch=1, grid=grid, in_specs=in_specs, out_specs=out_specs)
    return _pc(body, grid_spec=gs, out_shape=out_shape, name=name, compiler_params=_cp(len(grid)))


def adamw_halves(sc, own, sib, w, m, v, name):
    rows, cols = w.shape
    rh, br = _half_rows(rows)
    nbk = rh // br

    def body(sc_ref, own_ref, sib_ref, w_ref, m_ref, v_ref, go_ref, d_ref, mo_ref, vo_ref):
        g_ = jnp.where(pl.program_id(0) // nbk == sc_ref[0], own_ref[...], sib_ref[...])
        go_ref[...] = g_
        d_ref[...], mo_ref[...], vo_ref[...] = _adamw(w_ref[...], g_, m_ref[...], v_ref[...])

    half = pl.BlockSpec((br, cols), lambda i, sc_: (i % nbk, 0))
    blk = pl.BlockSpec((br, cols), lambda i, sc_: (i, 0))
    return _pc_sp(body, (2 * nbk,), [half, half, blk, blk, blk], [blk] * 4, [SDS((rows, cols), F32)] * 4,
                  name)(sc, own, sib, w, m, v)


def adamw_cols(sc, own, sib, w, m, v, name, cb=256):
    rows, cols = w.shape
    nk = cols // 2 // cb

    def body(sc_ref, own_ref, sib_ref, w_ref, m_ref, v_ref, go_ref, d_ref, mo_ref, vo_ref):
        g_ = jnp.where(pl.program_id(0) == sc_ref[0], own_ref[...], sib_ref[...])
        go_ref[...] = g_
        d_ref[...], mo_ref[...], vo_ref[...] = _adamw(w_ref[...], g_, m_ref[...], v_ref[...])

    half = pl.BlockSpec((rows, cb), lambda h, k, sc_: (0, k))
    blk = pl.BlockSpec((rows, cb), lambda h, k, sc_: (0, h * nk + k))
    return _pc_sp(body, (2, nk), [half, half, blk, blk, blk], [blk] * 4, [SDS((rows, cols), F32)] * 4,
                  name)(sc, own, sib, w, m, v)


def adamw_small(g, w, m, v):
    def body(g_ref, w_ref, m_ref, v_ref, d_ref, mo_ref, vo_ref):
        d_ref[...], mo_ref[...], vo_ref[...] = _adamw(w_ref[...], g_ref[...], m_ref[...], v_ref[...])

    vm = pl.BlockSpec(memory_space=pltpu.VMEM)
    return _pc(body, name="adamw_small", in_specs=[vm] * 4, out_specs=[vm] * 3, out_shape=[SDS(g.shape, F32)] * 3,
               compiler_params=pltpu.CompilerParams(vmem_limit_bytes=VMEM_LIMIT))(g, w, m, v)


def _pos():
    return lax.axis_index("x"), lax.axis_index("y"), lax.axis_index("c")


def _other_chips(x, y):
    return [(1 - x, y), (x, 1 - y), (1 - x, 1 - y)]


_ANY = pl.BlockSpec(memory_space=pltpu.HBM)


class _Side:
    def __init__(self, ins, out_shapes, nsem, start, finish):
        self.ins, self.out_shapes, self.start, self.finish = list(ins), list(out_shapes), start, finish
        self.scratch = [pltpu.SemaphoreType.DMA((nsem,)), pltpu.SemaphoreType.DMA((nsem,))]
        self.n_in, self.n_out = len(self.ins), len(self.out_shapes)


def _run_side(side, name):
    def body(*refs):
        args_ = (refs[:side.n_in], refs[side.n_in:side.n_in + side.n_out], *refs[side.n_in + side.n_out:])
        side.start(*args_)
        side.finish(*args_)

    return _pc(body, name=name, in_specs=[_ANY] * side.n_in, out_specs=[_ANY] * side.n_out,
               out_shape=side.out_shapes, scratch_shapes=side.scratch)(*side.ins)


def gather_side(bigs, tinies):
    nb, nt_ = len(bigs), len(tinies)

    def plan(ins, outs, ssem, rsem):
        x, y, c = _pos()
        me = 2 * x + y
        chips = _other_chips(x, y)
        sibling = (x, y, 1 - c)

        def copy(k, src, dst, to):
            return pltpu.make_async_remote_copy(src_ref=src, dst_ref=dst, send_sem=ssem.at[k], recv_sem=rsem.at[k],
                                                device_id=to, device_id_type=MESH)

        sends, landed, passed_on, tiny_landed = [], [], [], []
        for w in range(nb):
            rh = bigs[w].shape[0] // 2
            mine = pl.ds(pl.multiple_of(c * rh, 16), rh)
            theirs = pl.ds(pl.multiple_of((1 - c) * rh, 16), rh)
            for j, (cx, cy) in enumerate(chips):
                sends.append(copy(6 * w + j, ins[w].at[mine], outs[w].at[me, mine], (cx, cy, c)))
                blk = outs[w].at[2 * cx + cy, mine]
                landed.append((copy(6 * w + j, blk, blk, (cx, cy, c)), copy(6 * w + 3 + j, blk, blk, sibling)))
                blk = outs[w].at[2 * cx + cy, theirs]
                passed_on.append(copy(6 * w + 3 + j, blk, blk, sibling))
        for w in range(nt_):
            for j, (cx, cy) in enumerate(chips):
                k = 6 * nb + 3 * w + j
                sends.append(copy(k, ins[nb + w], outs[nb + w].at[me], (cx, cy, c)))
                blk = outs[nb + w].at[2 * cx + cy]
                tiny_landed.append(copy(k, blk, blk, (cx, cy, c)))
        return sends, landed, passed_on, tiny_landed

    def start(ins, outs, ssem, rsem):
        for cp in plan(ins, outs, ssem, rsem)[0]:
            cp.start()

    def finish(ins, outs, ssem, rsem):
        sends, landed, passed_on, tiny_landed = plan(ins, outs, ssem, rsem)
        for arrived, forward in landed:
            arrived.wait_recv()
            forward.start()
        for arrived in tiny_landed + passed_on:
            arrived.wait_recv()
        for cp in sends + [forward for _, forward in landed]:
            cp.wait_send()

    return _Side(list(bigs) + list(tinies), [SDS((4,) + a.shape, a.dtype) for a in list(bigs) + list(tinies)],
                 6 * nb + 3 * nt_, start, finish)


def swap_halves(gs, name):
    n = len(gs)

    def body(*refs):
        g_refs, sib_refs = refs[:n], refs[n:2 * n]
        ssem, rsem = refs[2 * n:]
        x, y, c = _pos()
        cps = []
        for w in range(n):
            rh = gs[w].shape[1] // 2
            give = pl.ds(pl.multiple_of((1 - c) * rh, 16), rh)
            cp = pltpu.make_async_remote_copy(src_ref=g_refs[w].at[:, give], dst_ref=sib_refs[w], send_sem=ssem.at[w],
                                              recv_sem=rsem.at[w], device_id=(x, y, 1 - c), device_id_type=MESH)
            cp.start()
            cps.append(cp)
        for cp in cps:
            cp.wait()

    return _pc(body, name=name, in_specs=[_ANY] * n, out_specs=[_ANY] * n,
               out_shape=[SDS((g.shape[0], g.shape[1] // 2, g.shape[2]), g.dtype) for g in gs],
               scratch_shapes=[pltpu.SemaphoreType.DMA((n,)), pltpu.SemaphoreType.DMA((n,))])(*gs)


def add_half(sc, g, sib, name):
    l, r, cols = g.shape
    rh, br = _half_rows(r)
    nbk = rh // br

    def body(sc_ref, g_ref, s_ref, o_ref):
        o_ref[...] = (g_ref[...].astype(F32) + s_ref[...].astype(F32)).astype(BF)

    blk = pl.BlockSpec((1, br, cols), lambda j, i, sc_: (j, i, 0))
    return _pc_sp(body, (l, nbk), [pl.BlockSpec((1, br, cols), lambda j, i, sc_: (j, sc_[0] * nbk + i, 0)), blk], blk,
                  SDS((l, rh, cols), BF), name)(sc, g, sib)


def exchange_side(ps):
    n_ = len(ps)

    def width(p_):
        return p_.shape[2] if p_.shape[0] == 4 else p_.shape[2] // 4

    def plan(p_refs, got_refs, ssem, rsem):
        x, y, c = _pos()
        cps = []
        for w in range(n_):
            wd = width(ps[w])
            for j, (cx, cy) in enumerate(_other_chips(x, y)):
                to = 2 * cx + cy
                src = p_refs[w].at[to] if ps[w].shape[0] == 4 else p_refs[w].at[0, :, pl.ds(pl.multiple_of(to * wd, LANE), wd)]
                cps.append(pltpu.make_async_remote_copy(
                    src_ref=src, dst_ref=got_refs[w].at[j], send_sem=ssem.at[3 * w + j], recv_sem=rsem.at[3 * w + j],
                    device_id=(cx, cy, c), device_id_type=MESH))
        return cps

    def start(*refs):
        for cp in plan(*refs):
            cp.start()

    def finish(*refs):
        for cp in plan(*refs):
            cp.wait()

    return _Side(ps, [SDS((3, p_.shape[1], width(p_)), p_.dtype) for p_ in ps], 3 * n_, start, finish)


def sum4(sc, p, got, name):
    _, rh, wd = got.shape
    _, br = _half_rows(2 * rh)

    def body(sc_ref, p_ref, g_ref, r_ref):
        r_ref[...] = ((p_ref[0].astype(F32) + g_ref[0].astype(F32)) + (g_ref[1].astype(F32) + g_ref[2].astype(F32)))

    own = (pl.BlockSpec((1, br, wd), lambda i, sc_: (sc_[1], i, 0)) if p.shape[0] == 4
           else pl.BlockSpec((1, br, wd), lambda i, sc_: (0, i, sc_[1])))
    return _pc_sp(body, (rh // br,), [own, pl.BlockSpec((3, br, wd), lambda i, sc_: (0, i, 0))],
                  pl.BlockSpec((br, wd), lambda i, sc_: (i, 0)), SDS((rh, wd), F32), name)(sc, p, got)


def join_halves(halves):
    n = len(halves)

    def body(*refs):
        h_refs, got_refs = refs[:n], refs[n:2 * n]
        ssem, rsem = refs[2 * n:]
        x, y, c = _pos()
        cps = []
        for w in range(n):
            cp = pltpu.make_async_remote_copy(src_ref=h_refs[w], dst_ref=got_refs[w], send_sem=ssem.at[w],
                                              recv_sem=rsem.at[w], device_id=(x, y, 1 - c), device_id_type=MESH)
            cp.start()
            cps.append(cp)
        for cp in cps:
            cp.wait()

    return _pc(body, name="join_halves", in_specs=[_ANY] * n, out_specs=[_ANY] * n,
               out_shape=[SDS(h.shape, h.dtype) for h in halves],
               scratch_shapes=[pltpu.SemaphoreType.DMA((n,)), pltpu.SemaphoreType.DMA((n,))])(*halves)


def allreduce_small(g):
    rows = g.shape[0]
    rh = rows // 2

    def body(g_ref, out_ref, sib_buf, chip_buf, sum_sc, ssem, rsem):
        x, y, c = _pos()
        me = 2 * x + y
        sibling = (x, y, 1 - c)
        mine = pl.ds(pl.multiple_of(c * rh, 8), rh)

        def copy(k, src, dst, to):
            return pltpu.make_async_remote_copy(src_ref=src, dst_ref=dst, send_sem=ssem.at[k], recv_sem=rsem.at[k],
                                                device_id=to, device_id_type=MESH)

        cp = copy(0, g_ref, sib_buf, sibling)
        cp.start()
        cp.wait()
        sum_sc[...] = g_ref[...] + sib_buf[...]
        chips = _other_chips(x, y)
        cps = [copy(1 + j, sum_sc.at[mine], chip_buf.at[me], (cx, cy, c)) for j, (cx, cy) in enumerate(chips)]
        for cp in cps:
            cp.start()
        chip_buf[me] = sum_sc[mine, :]
        for j, (cx, cy) in enumerate(chips):
            copy(1 + j, sum_sc.at[mine], chip_buf.at[2 * cx + cy], (cx, cy, c)).wait_recv()
        for cp in cps:
            cp.wait_send()
        out_ref[mine, :] = (chip_buf[0] + chip_buf[1]) + (chip_buf[2] + chip_buf[3])
        cp = copy(4, out_ref.at[mine], out_ref.at[mine], sibling)
        cp.start()
        cp.wait()

    vm = pl.BlockSpec(memory_space=pltpu.VMEM)
    return _pc(body, name="allreduce_small", in_specs=[vm], out_specs=vm, out_shape=SDS((rows, LANE), F32),
               scratch_shapes=[pltpu.VMEM((rows, LANE), F32), pltpu.VMEM((4, rh, LANE), F32), pltpu.VMEM((rows, LANE), F32),
                               pltpu.SemaphoreType.DMA((5,)), pltpu.SemaphoreType.DMA((5,))],
               compiler_params=pltpu.CompilerParams(vmem_limit_bytes=VMEM_LIMIT))(g)


def _pack_small(entries, get):
    flat = jnp.concatenate([get(n).reshape(-1).astype(F32) for n, _ in entries])
    rows = -(-flat.shape[0] // (8 * LANE)) * 8
    return jnp.pad(flat, (0, rows * LANE - flat.shape[0])).reshape(rows, LANE)


def _unpack_small(entries, packed):
    out, off = {}, 0
    flat = packed.reshape(-1)
    for name, n in entries:
        out[name] = flat[off:off + n]
        off += n
    return out


def _cols_full(blk):
    return blk.transpose(1, 0, 2).reshape(blk.shape[1], 4 * blk.shape[2])


def kernel(x, p, pre_mix_norm, w_in, a_ln_g, a_ln_b, a_spatial_w, a_spatial_b, a_out, b_gk, b_gk_bias, b_out_norm, b_out, w_mix_out, post_mix_norm, pre_ffn_norm, w_up, conv_w, conv_b, w_down, post_ffn_norm, w_ple, w_ple_gate, post_ple_norm, loss_target, m_pre_mix_norm, m_w_in, m_a_ln_g, m_a_ln_b, m_a_spatial_w, m_a_spatial_b, m_a_out, m_b_gk, m_b_gk_bias, m_b_out_norm, m_b_out, m_w_mix_out, m_post_mix_norm, m_pre_ffn_norm, m_w_up, m_conv_w, m_conv_b, m_w_down, m_post_ffn_norm, m_w_ple, m_w_ple_gate, m_post_ple_norm, v_pre_mix_norm, v_w_in, v_a_ln_g, v_a_ln_b, v_a_spatial_w, v_a_spatial_b, v_a_out, v_b_gk, v_b_gk_bias, v_b_out_norm, v_b_out, v_w_mix_out, v_post_mix_norm, v_pre_ffn_norm, v_w_up, v_conv_w, v_conv_b, v_w_down, v_post_ffn_norm, v_w_ple, v_w_ple_gate, v_post_ple_norm):
    args = dict(locals())
    order = ['pre_mix_norm', 'w_in', 'a_ln_g', 'a_ln_b', 'a_spatial_w', 'a_spatial_b', 'a_out', 'b_gk', 'b_gk_bias',
             'b_out_norm', 'b_out', 'w_mix_out', 'post_mix_norm', 'pre_ffn_norm', 'w_up', 'conv_w', 'conv_b', 'w_down',
             'post_ffn_norm', 'w_ple', 'w_ple_gate', 'post_ple_norm']
    assert sorted(BIG + TINY + tuple(n for n, _ in SMALL)) == sorted(order)
    s = x.shape[1]
    xs = x.reshape(s, D)
    ps = p.reshape(s, PLE)
    tgt = loss_target.reshape(s, D)
    t_big = min(1024, s)
    t_mid = min(512, s)
    t_small = min(256, s)
    t_gla = min(256, s)
    mx_, my_, mc_ = _pos()
    me = 2 * mx_ + my_
    sc = jnp.stack([mc_, me]).astype(jnp.int32)
    shard = lambda n: args[n].reshape(args[n].shape[1:])

    mine = {n: shard(n).astype(BF) for n in BIG}
    mine.update({n: shard(n) for n in TINY})
    fill = lambda names, gots: {n: lax.dynamic_update_slice(got, mine[n][None], (me, 0, 0)) for n, got in zip(names, gots)}
    first = ("w_in",) + TINY
    full = fill(first, _run_side(gather_side([mine["w_in"]], [mine[n] for n in TINY]), "gather_first"))
    wi = _cols_full(full["w_in"])
    seg = lambda a, b: wi[:, a:b]
    qk = [seg(1024 + h * B_HK, 1024 + (h + 1) * B_HK) for h in range(B_H)]
    kk = [seg(1536 + h * B_HK, 1536 + (h + 1) * B_HK) for h in range(B_H)]
    w_z = jnp.concatenate([seg(0, 1024)] + [m_ for h in range(B_H) for m_ in (qk[h], kk[h])]
                          + [seg(2048, 4096), seg(4112, 6160), seg(4096, 4112), jnp.zeros((D, LANE - B_RANK), BF)], axis=1)
    wgk = jnp.pad(_cols_full(full["b_gk"]).astype(BF), ((0, LANE - B_RANK), (0, 0)))
    w_conv = _cols_full(full["conv_w"])
    g1, g2, g3 = pre_mix_norm.reshape(1, D), post_mix_norm.reshape(1, D), pre_ffn_norm.reshape(1, D)
    g4, g5 = post_ffn_norm.reshape(1, D), post_ple_norm.reshape(1, D)
    ln_g, ln_b = a_ln_g.reshape(1, A_W), a_ln_b.reshape(1, A_W)
    w_s = a_spatial_w.reshape(A_G, A_C, A_C)
    bs_t = a_spatial_b.reshape(A_G, A_C).T
    gk_bias = b_gk_bias.reshape(1, B_H * B_HK)
    wn = b_out_norm.reshape(1, B_HV)
    cb = conv_b.reshape(1, 2 * D_FF)
    idx = jnp.arange(t_gla)
    ltri = ((idx[:, None] // B_C == idx[None, :] // B_C) & (idx[None, :] <= idx[:, None])).astype(BF)

    a, z, qk32, *gots = norm_matmul(xs, g1, w_z, D, t_big, "in_proj", nblk=6, f32_blk=1,
                                    side=gather_side([mine[n] for n in BIG[1:]], []))
    full.update(fill(BIG[1:], gots))
    w_aout, w_ple_f = _cols_full(full["a_out"]), _cols_full(full["w_ple"])
    w_bout, w_mix, w_pg = (full[n].reshape(D, D) for n in ("b_out", "w_mix_out", "w_ple_gate"))
    w_dn, w_up3 = full["w_down"].reshape(D_FF, D), full["w_up"]
    zl = mm_cols(a, w_z, 48, LANE, t_big, "lr_proj")
    sa = sgu_fwd(z, ln_g, ln_b, w_s, bs_t, t_mid)
    ob, o, states = gla_fwd(z, qk32, zl, wgk, gk_bias, wn, ltri, t_gla)
    ya, yb, mp, mx, h1 = mix_fwd(sa, ob, z, xs, w_aout, w_bout, w_mix, g2, t_small)
    c, up = norm_matmul(h1, g3, w_up3, 1408, t_big, "up_proj")
    ff, cg, cv = ffn_gate_fwd(up, w_conv, cb, t_small)
    f, h2, pg, pe, dy, loss = out_fwd(ff, h1, ps, tgt, w_dn, w_pg, w_ple_f, g4, g5, t_small)

    dh2, dpe, dpg, df, dff, gg5, gg4 = out_bwd(dy, pg, pe, f, g5, g4, w_pg, w_dn, t_small)
    dup_g, dup_v, gcw_g, gcw_v, gcb_g, gcb_v = ffn_gate_bwd(up, cg, cv, dff, w_conv, t_small)
    wu = lambda k: pl.BlockSpec((None, D, 1408), lambda i: (k, 0, 0), pipeline_mode=pl.Buffered(1))
    dh1, gg3 = nt_normbwd([(dup_g, 0, w_up3, wu(0)), (dup_g, 1, w_up3, wu(1)), (dup_v, 0, w_up3, wu(2)),
                           (dup_v, 1, w_up3, wu(3))], h1, g3, dh2, t_small, "up_bwd")
    dmx, dya, dyb, dga, dgb, dsa, dob, gg2 = mix_bwd(dh1, mx, z, ya, yb, g2, w_mix, w_aout, w_bout, t_small)
    duv, g_lng, g_lnb, g_ws, g_bst = sgu_bwd(z, dsa, ln_g, ln_b, w_s, bs_t, t_mid)

    grads = {
        "a_out": mm_tn(sa, dya, "dw_a_out")[None],
        "b_out": mm_tn(ob, dyb, "dw_b_out").reshape(4, D // 4, D),
        "w_mix_out": mm_tn(mp, dmx, "dw_mix").reshape(4, D // 4, D),
        "w_up": jnp.concatenate([mm_tn(c, dup_g, "dw_up_g"), mm_tn(c, dup_v, "dw_up_v")], axis=1)[None],
        "w_down": mm_tn(ff, df, "dw_down").reshape(4, D_FF // 4, D),
        "w_ple": mm_tn(ps, dpe, "dw_ple")[None],
        "w_ple_gate": mm_tn(h2, dpg, "dw_ple_gate").reshape(4, D // 4, D),
    }

    def chip_partials(names):
        gs = [grads[n] for n in names]
        return [add_half(sc, g, sib, "partial_" + n)
                for n, g, sib in zip(names, gs, swap_halves(gs, "swap_halves_" + names[0]))]

    parts = dict(zip(BIG[1:], chip_partials(BIG[1:])))
    dqk, dvb, dog, dpre, g_gkb, g_wn, *gots = gla_bwd(z, qk32, zl, o, dob, states, wgk, gk_bias, wn, ltri, ltri.T,
                                                      t_gla, side=exchange_side([parts[n] for n in BIG[1:]]))
    got = dict(zip(BIG[1:], gots))
    dlr = mm_nt_small(dpre, wgk, t_mid, "dlr")
    segs = [duv, dqk, dvb, dog, dga, dgb, dlr]

    gz = [mm_tn(a, sg_, "dw_in_%d" % k) for k, sg_ in enumerate(segs)]
    gq = [gz[1][:, h * 256:h * 256 + B_HK] for h in range(B_H)]
    gk = [gz[1][:, h * 256 + B_HK:(h + 1) * 256] for h in range(B_H)]
    g_in = jnp.concatenate([gz[0]] + gq + gk + [gz[2], gz[3], gz[6][:, :B_RANK], gz[4], gz[5]], axis=1)
    grads["w_in"] = g_in.reshape(D, 4, 1540).transpose(1, 0, 2)
    parts["w_in"], = chip_partials(("w_in",))
    wz = lambda k, width: pl.BlockSpec((D, width), lambda i: (0, k), pipeline_mode=pl.Buffered(1))
    dx, gg1, got["w_in"] = nt_normbwd(
        [(segs[k], 0, w_z, wz(k, D)) for k in range(6)] + [(dlr, 0, w_z, wz(48, LANE))], xs, g1, dh1, t_small,
        "in_bwd", side=exchange_side([parts["w_in"]]))

    reds = [sum4(sc, parts[n], got[n], "sum_" + n) for n in BIG]
    outs = {}
    for n, red, sib in zip(BIG, reds, join_halves(reds)):
        if n == "w_in":
            res = adamw_cols(sc, red.T, sib.T, shard(n).T, shard("m_" + n).T, shard("v_" + n).T, "adamw_" + n)
            res = [r_.T for r_ in res]
        else:
            res = adamw_halves(sc, red, sib, shard(n), shard("m_" + n), shard("v_" + n), "adamw_" + n)
        outs[n] = [r_.reshape(args[n].shape) for r_ in res]

    small_g = {
        "pre_mix_norm": gg1, "a_ln_g": g_lng, "a_ln_b": g_lnb, "a_spatial_w": g_ws, "a_spatial_b": g_bst.T,
        "b_gk_bias": g_gkb, "b_out_norm": g_wn, "post_mix_norm": gg2, "pre_ffn_norm": gg3,
        "conv_b": jnp.concatenate([gcb_g, gcb_v], axis=1), "post_ffn_norm": gg4, "post_ple_norm": gg5,
        "b_gk": mm_tn(zl, dpre, "dw_gk")[:B_RANK], "conv_w": jnp.concatenate([gcw_g, gcw_v], axis=1),
    }
    red_entries = SMALL + (("b_gk", B_RANK * 512), ("conv_w", 3 * 2 * D_FF))
    g_fin = _unpack_small(red_entries, allreduce_small(_pack_small(red_entries, lambda n: small_g[n])))
    g_fin["b_gk"] = lax.dynamic_slice(g_fin["b_gk"].reshape(B_RANK, 512), (0, me * B_HK), (B_RANK, B_HK))
    g_fin["conv_w"] = lax.dynamic_slice(g_fin["conv_w"].reshape(3, 2 * D_FF), (0, me * 1408), (3, 1408))
    upd_entries = SMALL + (("b_gk", B_RANK * B_HK), ("conv_w", 3 * 1408))
    res = adamw_small(*[_pack_small(upd_entries, get) for get in
                        (lambda n: g_fin[n], lambda n: args[n], lambda n: args["m_" + n], lambda n: args["v_" + n])])
    res = [_unpack_small(upd_entries, r_) for r_ in res]
    for n, _ in upd_entries:
        outs[n] = [r_[n].reshape(args[n].shape) for r_ in [g_fin] + res]

    total = lax.psum(loss[0, 0], ("x", "y", "c"))
    return (total, dx.reshape(x.shape), *[outs[n][0] for n in order], *[outs[n][1] for n in order],
            *[outs[n][2] for n in order], *[outs[n][3] for n in order])
```

```python
import functools
import math

import jax
import jax.numpy as jnp
from jax import lax
from jax.experimental import pallas as pl
from jax.experimental.pallas import tpu as pltpu

F32 = jnp.float32
BF = jnp.bfloat16
SDS = jax.ShapeDtypeStruct
MESH = pl.DeviceIdType.MESH

EPS = 1e-6
D = 1024
A_W = 512
A_G, A_C = 8, 128
A_GD = A_W // A_G
B_H, B_HK, B_HV = 4, 128, 256
B_C = 64
GLA_HPB = 4
B_RANK = 16
D_FF = 2816
PLE = 256
ZW = 6272
LANE = 128
VMEM_LIMIT = 60 * 1024 * 1024

ADAM_LR, ADAM_B1, ADAM_B2, ADAM_EPS, ADAM_WD, ADAM_STEP = 0.001, 0.9, 0.999, 1e-08, 0.01, 10

_GC = math.sqrt(2.0 / math.pi)
_GA = 0.044715

BIG = ("w_in", "a_out", "b_out", "w_mix_out", "w_up", "w_down", "w_ple", "w_ple_gate")
TINY = ("b_gk", "conv_w")
SMALL = (("pre_mix_norm", 1024), ("a_ln_g", 512), ("a_ln_b", 512), ("a_spatial_w", 131072),
         ("a_spatial_b", 1024), ("b_gk_bias", 512), ("b_out_norm", 256), ("post_mix_norm", 1024),
         ("pre_ffn_norm", 1024), ("conv_b", 5632), ("post_ffn_norm", 1024), ("post_ple_norm", 1024))


def _pc(body, **kw):
    return pl.pallas_call(body, **kw)


def _cp(n):
    return pltpu.CompilerParams(dimension_semantics=("arbitrary",) * n, vmem_limit_bytes=VMEM_LIMIT)


def _const(shape):
    nd = len(shape)
    return pl.BlockSpec(shape, lambda *_: (0,) * nd, pipeline_mode=pl.Buffered(1))


def _acc(shape):
    nd = len(shape)
    return pl.BlockSpec(shape, lambda *_: (0,) * nd)


def _dot(a, b):
    return jnp.dot(a, b, preferred_element_type=F32)


def _dot_nt(a, b):
    return lax.dot_general(a, b, (((1,), (1,)), ((), ())), preferred_element_type=F32)


def _dot_tn(a, b):
    return lax.dot_general(a, b, (((0,), (0,)), ((), ())), preferred_element_type=F32)


def _gelu(x):
    return 0.5 * x * (1.0 + jnp.tanh(_GC * (x + _GA * x * x * x)))


def _gelu_and_grad(x):
    x2 = x * x
    s = 0.5 * jnp.tanh((_GC * x) * (1.0 + _GA * x2)) + 0.5
    g = x * s
    return g, s + g * (1.0 - s) * ((6.0 * _GC * _GA) * x2 + 2.0 * _GC)


def _log_sigmoid(x):
    return jnp.minimum(x, 0.0) - jnp.log(1.0 + jnp.exp(-jnp.abs(x)))


def _rms(x, g):
    return x * lax.rsqrt(jnp.mean(x * x, axis=-1, keepdims=True) + EPS) * g


def _rms_bwd(dy, x, g):
    r = lax.rsqrt(jnp.mean(x * x, axis=-1, keepdims=True) + EPS)
    n = x * r
    dn = dy * g
    dx = r * (dn - n * jnp.mean(dn * n, axis=-1, keepdims=True))
    return dx, jnp.sum(dy * n, axis=0, keepdims=True)


def _ldot3(l, x):
    h = x.astype(BF)
    r = x - h.astype(F32)
    m = r.astype(BF)
    lo = (r - m.astype(F32)).astype(BF)
    return _dot(l, h) + _dot(l, m) + _dot(l, lo)


def _split_side(refs, n_in, n_out, n_scratch, side):
    si, so = (side.n_in, side.n_out) if side else (0, 0)
    cuts = [n_in, si, n_out, so, n_scratch]
    out, at = [], 0
    for c in cuts:
        out.append(refs[at:at + c])
        at += c
    return (*out, refs[at:])


def _side_specs(side):
    return ([_ANY] * side.n_in, [_ANY] * side.n_out, side.out_shapes, side.scratch, side.ins) if side else ([],) * 5


def norm_matmul(x, g, w, bn, t, name, nblk=None, f32_blk=None, side=None):
    s, dm = x.shape
    if w.ndim == 3:
        nblk = w.shape[0]
        w_spec = pl.BlockSpec((None, dm, bn), lambda i, j: (j, 0, 0))
    else:
        nblk = nblk or w.shape[1] // bn
        w_spec = pl.BlockSpec((dm, bn), lambda i, j: (0, j))
    extra = f32_blk is not None
    nt = s // t

    def body(*refs):
        (x_ref, g_ref, w_ref), s_in, outs, s_out, (a_sc,), s_scr = _split_side(refs, 3, 2 + extra, 1, side)
        a_ref, z_ref = outs[:2]
        i, j = pl.program_id(0), pl.program_id(1)
        if side:
            @pl.when((i == 0) & (j == 0))
            def _():
                side.start(s_in, s_out, *s_scr)

        @pl.when(j == 0)
        def _():
            a = _rms(x_ref[...], g_ref[...]).astype(BF)
            a_sc[...] = a
            a_ref[...] = a

        acc = _dot(a_sc[...], w_ref[...])
        z_ref[...] = acc.astype(BF)
        if extra:
            @pl.when(j == f32_blk)
            def _():
                outs[2][...] = acc
        if side:
            @pl.when((i == nt - 1) & (j == nblk - 1))
            def _():
                side.finish(s_in, s_out, *s_scr)

    si_specs, so_specs, so_shapes, s_scratch, s_ins = _side_specs(side)
    return _pc(
        body, name=name, grid=(nt, nblk),
        in_specs=[pl.BlockSpec((t, dm), lambda i, j: (i, 0)), _const((1, dm)), w_spec] + si_specs,
        out_specs=[pl.BlockSpec((t, dm), lambda i, j: (i, 0)), pl.BlockSpec((t, bn), lambda i, j: (i, j))]
        + [pl.BlockSpec((t, bn), lambda i, j: (i, 0))] * extra + so_specs,
        out_shape=[SDS((s, dm), BF), SDS((s, nblk * bn), BF)] + [SDS((s, bn), F32)] * extra + so_shapes,
        scratch_shapes=[pltpu.VMEM((t, dm), BF)] + s_scratch, compiler_params=_cp(2))(x, g, w, *s_ins)


def _sgu_masked(ws_ref):
    r = lax.broadcasted_iota(jnp.int32, (A_C, A_C), 0)
    c = lax.broadcasted_iota(jnp.int32, (A_C, A_C), 1)
    return [jnp.where(c <= r, ws_ref[g], 0.0).astype(BF) for g in range(A_G)]


def _sgu_recompute(v, lng, lnb):
    gv, dgv = _gelu_and_grad(v)
    mu = jnp.mean(gv, axis=-1, keepdims=True)
    xc = gv - mu
    rstd = lax.rsqrt(jnp.mean(xc * xc, axis=-1, keepdims=True) + EPS)
    xhat = xc * rstd
    return dgv, rstd, xhat, (xhat * lng + lnb).astype(BF)


def sgu_fwd(z, ln_g, ln_b, w_s, bs_t, t):
    s = z.shape[0]

    def body(u_ref, v_ref, g_ref, b_ref, ws_ref, bs_ref, sa_ref, s_sc):
        wm = _sgu_masked(ws_ref)
        for ci in range(t // A_C):
            rows = pl.ds(ci * A_C, A_C)
            _, _, _, vn = _sgu_recompute(v_ref[rows, :].astype(F32), g_ref[...], b_ref[...])
            for g in range(A_G):
                cols = slice(g * A_GD, (g + 1) * A_GD)
                s_sc[:, cols] = _dot(wm[g], vn[:, cols]) + bs_ref[:, g:g + 1]
            sa_ref[rows, :] = (_gelu(u_ref[rows, :].astype(F32)) * s_sc[...]).astype(BF)

    return _pc(
        body, name="sgu_fwd", grid=(s // t,),
        in_specs=[pl.BlockSpec((t, A_W), lambda i: (i, 0)), pl.BlockSpec((t, A_W), lambda i: (i, 1)),
                  _const((1, A_W)), _const((1, A_W)), _const((A_G, A_C, A_C)), _const((A_C, A_G))],
        out_specs=pl.BlockSpec((t, A_W), lambda i: (i, 0)),
        out_shape=SDS((s, A_W), BF),
        scratch_shapes=[pltpu.VMEM((A_C, A_W), F32)], compiler_params=_cp(1))(z, z, ln_g, ln_b, w_s, bs_t)


def _gla_decays(qk, lr, wgk, bias, l, t):
    nc = t // B_C
    q = qk[:, :B_HK].astype(F32) * (B_HK ** -0.5)
    k = qk[:, B_HK:].astype(F32)
    pre = _dot(lr, wgk) + bias
    la = _log_sigmoid(pre) * (1.0 / 16.0)
    b = _ldot3(l, la)
    b3 = b.reshape(nc, B_C, B_HK)
    bl = jnp.broadcast_to(b3[:, B_C - 1:B_C, :], (nc, B_C, B_HK)).reshape(t, B_HK)
    eb, enb, etb = jnp.exp(b), jnp.exp(-b), jnp.exp(bl - b)
    return pre, b, bl, eb, enb, etb, q * eb, k * enb, k * etb


def gla_fwd(z, qk32, zl, wgk, bias, wn, ltri, t):
    s = z.shape[0]
    nc = t // B_C
    hpb = GLA_HPB
    kw, vw = hpb * B_HK, hpb * B_HV

    def body(qk_ref, v_ref, og_ref, lr_ref, wgk_ref, bias_ref, wn_ref, l_ref, ob_ref, o_ref, st_ref, st_sc, o_sc):
        g = pl.program_id(1)

        @pl.when(pl.program_id(0) == 0)
        def _():
            for hh in range(hpb):
                st_sc[g * hpb + hh] = jnp.zeros((B_HV, B_HK), F32)

        lr, l = lr_ref[...], l_ref[...]
        for hh in range(hpb):
            h = g * hpb + hh
            cv, ck = slice(hh * B_HV, (hh + 1) * B_HV), slice(hh * B_HK, (hh + 1) * B_HK)
            _, _, bl, _, _, _, qd, ki, kt = _gla_decays(qk_ref[:, cv], lr, wgk_ref[:, ck], bias_ref[:, ck], l, t)
            qd, ki, kt = qd.astype(BF), ki.astype(BF), kt.astype(BF)
            vb = v_ref[:, cv]
            sc = jnp.where(l > 0, _dot_nt(qd, ki), 0.0).astype(BF)
            o_sc[hh] = _dot(sc, vb)
            for n in range(nc):
                rows = slice(n * B_C, (n + 1) * B_C)
                st = st_sc[h]
                stb = st.astype(BF)
                st_ref[n, hh] = stb
                o_sc[hh, rows, :] += _dot_nt(qd[rows], stb)
                st_sc[h] = st * jnp.exp(bl[n * B_C:n * B_C + 1, :]) + _dot_tn(vb[rows], kt[rows])
            ob = o_sc[hh].astype(BF)
            o_ref[:, cv] = ob
            og = og_ref[:, cv].astype(F32)
            ob_ref[:, cv] = (_rms(ob.astype(F32), wn_ref[...]) * og * jax.nn.sigmoid(og)).astype(BF)

    vo, go = 2048 // vw, 3072 // vw
    return _pc(
        body, name="gla_fwd", grid=(s // t, B_H // hpb),
        in_specs=[pl.BlockSpec((t, vw), lambda i, g: (i, g)), pl.BlockSpec((t, vw), lambda i, g: (i, vo + g)),
                  pl.BlockSpec((t, vw), lambda i, g: (i, go + g)), pl.BlockSpec((t, LANE), lambda i, g: (i, 0)),
                  pl.BlockSpec((LANE, kw), lambda i, g: (0, g)), pl.BlockSpec((1, kw), lambda i, g: (0, g)),
                  _const((1, B_HV)), _const((t, t))],
        out_specs=[pl.BlockSpec((t, vw), lambda i, g: (i, g)), pl.BlockSpec((t, vw), lambda i, g: (i, g)),
                   pl.BlockSpec((nc, hpb, B_HV, B_HK), lambda i, g: (i, g, 0, 0))],
        out_shape=[SDS((s, D), BF), SDS((s, D), BF), SDS((s // B_C, B_H, B_HV, B_HK), BF)],
        scratch_shapes=[pltpu.VMEM((B_H, B_HV, B_HK), F32), pltpu.VMEM((hpb, t, B_HV), F32)],
        compiler_params=_cp(2))(qk32, z, z, zl, wgk, bias, wn, ltri)


def mix_fwd(sa, ob, z, x, a_out, b_out, w_mix, g2, t):
    s = x.shape[0]

    def body(sa_ref, ob_ref, ga_ref, gb_ref, x_ref, ao_ref, bo_ref, wm_ref, g2_ref,
             ya_ref, yb_ref, mp_ref, mx_ref, h1_ref):
        ya = _dot(sa_ref[...], ao_ref[...]).astype(BF)
        yb = _dot(ob_ref[...], bo_ref[...]).astype(BF)
        ya_ref[...] = ya
        yb_ref[...] = yb
        mp = (jax.nn.sigmoid(ga_ref[...].astype(F32)) * ya.astype(F32)
              + jax.nn.sigmoid(gb_ref[...].astype(F32)) * yb.astype(F32)).astype(BF)
        mp_ref[...] = mp
        mx = _dot(mp, wm_ref[...]).astype(BF)
        mx_ref[...] = mx
        h1_ref[...] = x_ref[...] + _rms(mx.astype(F32), g2_ref[...])

    row = lambda w: pl.BlockSpec((t, w), lambda i: (i, 0))
    return _pc(
        body, name="mix_fwd", grid=(s // t,),
        in_specs=[row(A_W), row(D), pl.BlockSpec((t, D), lambda i: (i, 4)), pl.BlockSpec((t, D), lambda i: (i, 5)),
                  row(D), _const((A_W, D)), _const((D, D)), _const((D, D)), _const((1, D))],
        out_specs=[row(D)] * 5,
        out_shape=[SDS((s, D), BF)] * 4 + [SDS((s, D), F32)],
        compiler_params=_cp(1))(sa, ob, z, z, x, a_out, b_out, w_mix, g2)


def ffn_gate_fwd(up, conv_w, conv_b, t):
    s = up.shape[0]
    bn = 1408
    hb = t // 8

    def body(ug_ref, uv_ref, hg_ref, hv_ref, wg_ref, wv_ref, bg_ref, bv_ref, ff_ref, cg_ref, cv_ref):
        live = (pl.program_id(1) > 0).astype(F32)

        def branch(u_ref, h_ref, w_ref, b_ref, c_ref):
            ext = jnp.concatenate([h_ref[...].astype(F32) * live, u_ref[...].astype(F32)], axis=0)
            w = w_ref[...]
            c = (b_ref[...] + w[0:1] * pltpu.roll(ext, 2, 0) + w[1:2] * pltpu.roll(ext, 1, 0) + w[2:3] * ext)[8:]
            c = c.astype(BF)
            c_ref[...] = c
            return c.astype(F32)

        g = _gelu(branch(ug_ref, hg_ref, wg_ref, bg_ref, cg_ref))
        ff_ref[...] = (g * branch(uv_ref, hv_ref, wv_ref, bv_ref, cv_ref)).astype(BF)

    halo = lambda off: pl.BlockSpec((8, bn), lambda j, i: (jnp.maximum(i * hb - 1, 0), j + off))
    out = pl.BlockSpec((t, bn), lambda j, i: (i, j))
    return _pc(
        body, name="ffn_gate_fwd", grid=(2, s // t),
        in_specs=[pl.BlockSpec((t, bn), lambda j, i: (i, j)), pl.BlockSpec((t, bn), lambda j, i: (i, j + 2)),
                  halo(0), halo(2),
                  pl.BlockSpec((3, bn), lambda j, i: (0, j)), pl.BlockSpec((3, bn), lambda j, i: (0, j + 2)),
                  pl.BlockSpec((1, bn), lambda j, i: (0, j)), pl.BlockSpec((1, bn), lambda j, i: (0, j + 2))],
        out_specs=[out] * 3, out_shape=[SDS((s, D_FF), BF)] * 3,
        compiler_params=_cp(2))(up, up, up, up, conv_w, conv_w, conv_b, conv_b)


def out_fwd(ff, h1, p, tgt, w_down, w_pg, w_ple, g4, g5, t):
    s = h1.shape[0]

    def body(ff_ref, h1_ref, p_ref, t_ref, wd_ref, wpg_ref, wpl_ref, g4_ref, g5_ref,
             f_ref, h2_ref, pg_ref, pe_ref, dy_ref, loss_ref):
        @pl.when(pl.program_id(0) == 0)
        def _():
            loss_ref[...] = jnp.zeros((1, 1), F32)

        f = _dot(ff_ref[...], wd_ref[...]).astype(BF)
        f_ref[...] = f
        h2 = h1_ref[...] + _rms(f.astype(F32), g4_ref[...])
        h2b = h2.astype(BF)
        h2_ref[...] = h2b
        pg = _dot(h2b, wpg_ref[...]).astype(BF)
        pe = _dot(p_ref[...].astype(BF), wpl_ref[...]).astype(BF)
        pg_ref[...] = pg
        pe_ref[...] = pe
        y = h2 + _rms(jax.nn.sigmoid(pg.astype(F32)) * pe.astype(F32), g5_ref[...])
        err = y - t_ref[...]
        dy_ref[...] = err * (1.0 / D)
        loss_ref[...] += (0.5 / D) * jnp.sum(err * err)

    row = lambda w: pl.BlockSpec((t, w), lambda i: (i, 0))
    return _pc(
        body, name="out_fwd", grid=(s // t,),
        in_specs=[row(D_FF), row(D), row(PLE), row(D), _const((D_FF, D)), _const((D, D)), _const((PLE, D)),
                  _const((1, D)), _const((1, D))],
        out_specs=[row(D)] * 5 + [_acc((1, 1))],
        out_shape=[SDS((s, D), BF)] * 4 + [SDS((s, D), F32), SDS((1, 1), F32)],
        compiler_params=_cp(1))(ff, h1, p, tgt, w_down, w_pg, w_ple, g4, g5)


def out_bwd(dy, pg, pe, f, g5, g4, w_pg, w_down, t):
    s = dy.shape[0]

    def body(dy_ref, pg_ref, pe_ref, f_ref, g5_ref, g4_ref, wpg_ref, wd_ref,
             dh2_ref, dpe_ref, dpg_ref, df_ref, dff_ref, gg5_ref, gg4_ref):
        @pl.when(pl.program_id(0) == 0)
        def _():
            gg5_ref[...] = jnp.zeros((1, D), F32)
            gg4_ref[...] = jnp.zeros((1, D), F32)

        dy_ = dy_ref[...]
        pg_ = pg_ref[...].astype(F32)
        pe_ = pe_ref[...].astype(F32)
        sg = jax.nn.sigmoid(pg_)
        dple, dg5 = _rms_bwd(dy_, sg * pe_, g5_ref[...])
        gg5_ref[...] += dg5
        dpe_ref[...] = (dple * sg).astype(BF)
        dpg = (dple * pe_ * sg * (1.0 - sg)).astype(BF)
        dpg_ref[...] = dpg
        dh2 = dy_ + _dot_nt(dpg, wpg_ref[...])
        dh2_ref[...] = dh2
        df, dg4 = _rms_bwd(dh2, f_ref[...].astype(F32), g4_ref[...])
        gg4_ref[...] += dg4
        dfb = df.astype(BF)
        df_ref[...] = dfb
        dff_ref[...] = _dot_nt(dfb, wd_ref[...]).astype(BF)

    row = lambda w: pl.BlockSpec((t, w), lambda i: (i, 0))
    return _pc(
        body, name="out_bwd", grid=(s // t,),
        in_specs=[row(D), row(D), row(D), row(D), _const((1, D)), _const((1, D)), _const((D, D)), _const((D_FF, D))],
        out_specs=[row(D), row(D), row(D), row(D), row(D_FF), _acc((1, D)), _acc((1, D))],
        out_shape=[SDS((s, D), F32), SDS((s, D), BF), SDS((s, D), BF), SDS((s, D), BF), SDS((s, D_FF), BF),
                   SDS((1, D), F32), SDS((1, D), F32)],
        compiler_params=_cp(1))(dy, pg, pe, f, g5, g4, w_pg, w_down)


def ffn_gate_bwd(up, cg, cv, dff, conv_w, t):
    s = up.shape[0]
    bn = 1408
    hb = t // 8
    nt = s // t
    r = t + 8

    def body(ug_ref, uv_ref, cg_ref, cv_ref, cag_ref, cav_ref, d_ref, da_ref, wg_ref, wv_ref,
             dug_ref, duv_ref, gwg_ref, gwv_ref, gbg_ref, gbv_ref):
        i = pl.program_id(1)

        @pl.when(i == 0)
        def _():
            gwg_ref[...] = jnp.zeros((3, bn), F32)
            gwv_ref[...] = jnp.zeros((3, bn), F32)
            gbg_ref[...] = jnp.zeros((1, bn), F32)
            gbv_ref[...] = jnp.zeros((1, bn), F32)

        def gate(c_g, c_v, d_):
            gl, dgl = _gelu_and_grad(c_g.astype(F32))
            d_ = d_.astype(F32)
            return d_ * c_v.astype(F32) * dgl, d_ * gl

        dg, dv = gate(cg_ref[...], cv_ref[...], d_ref[...])
        nxt = da_ref[...].astype(F32) * (i < nt - 1).astype(F32)
        dg_n, dv_n = gate(cag_ref[...], cav_ref[...], nxt)

        def back(dc, dc_next, u_ref, w_ref, du_ref, gw_ref, gb_ref):
            w = w_ref[...]
            d_ext = jnp.concatenate([dc, dc_next], axis=0)
            d1, d2 = pltpu.roll(d_ext, r - 1, 0)[:t], pltpu.roll(d_ext, r - 2, 0)[:t]
            du_ref[...] = (w[2:3] * dc + w[1:2] * d1 + w[0:1] * d2).astype(BF)
            u = u_ref[...].astype(F32)
            gw_ref[0:1, :] += jnp.sum(d2 * u, axis=0, keepdims=True)
            gw_ref[1:2, :] += jnp.sum(d1 * u, axis=0, keepdims=True)
            gw_ref[2:3, :] += jnp.sum(dc * u, axis=0, keepdims=True)
            gb_ref[...] += jnp.sum(dc, axis=0, keepdims=True)

        back(dg, dg_n, ug_ref, wg_ref, dug_ref, gwg_ref, gbg_ref)
        back(dv, dv_n, uv_ref, wv_ref, duv_ref, gwv_ref, gbv_ref)

    tile = lambda off: pl.BlockSpec((t, bn), lambda j, i: (i, j + off))
    after = lambda off: pl.BlockSpec((8, bn), lambda j, i: (jnp.minimum((i + 1) * hb, nt * hb - 1), j + off))
    cw = lambda off: pl.BlockSpec((3, bn), lambda j, i: (0, j + off))
    cb = lambda off: pl.BlockSpec((1, bn), lambda j, i: (0, j + off))
    return _pc(
        body, name="ffn_gate_bwd", grid=(2, nt),
        in_specs=[tile(0), tile(2), tile(0), tile(0), after(0), after(0), tile(0), after(0), cw(0), cw(2)],
        out_specs=[tile(0), tile(0), cw(0), cw(0), cb(0), cb(0)],
        out_shape=[SDS((s, D_FF), BF), SDS((s, D_FF), BF), SDS((3, D_FF), F32), SDS((3, D_FF), F32),
                   SDS((1, D_FF), F32), SDS((1, D_FF), F32)],
        compiler_params=_cp(2))(up, up, cg, cv, cg, cv, dff, dff, conv_w, conv_w)


def nt_normbwd(dys, w, xin, gain, dres, t, name, side=None):
    s = xin.shape[0]
    nt = s // t
    np_ = len(dys)

    def body(*refs):
        ins_, s_in, (dx_ref, gg_ref), s_out, _, s_scr = _split_side(refs, np_ + 4, 2, 0, side)
        dy_refs = ins_[:np_]
        w_ref, x_ref, g_ref, dres_ref = ins_[np_:]
        i = pl.program_id(0)

        @pl.when(i == 0)
        def _():
            gg_ref[...] = jnp.zeros((1, D), F32)
            if side:
                side.start(s_in, s_out, *s_scr)

        acc = _dot_nt(jnp.concatenate([r_[...] for r_ in dy_refs], axis=1), w_ref[...])
        dxn, dg = _rms_bwd(acc, x_ref[...], g_ref[...])
        dx_ref[...] = dres_ref[...] + dxn
        gg_ref[...] += dg
        if side:
            @pl.when(i == nt - 1)
            def _():
                side.finish(s_in, s_out, *s_scr)

    row = lambda width: pl.BlockSpec((t, width), lambda i: (i, 0))
    si_specs, so_specs, so_shapes, s_scratch, s_ins = _side_specs(side)
    assert sum(dy.shape[1] for dy in dys) == w.shape[1]
    return _pc(
        body, name=name, grid=(nt,),
        in_specs=[row(dy.shape[1]) for dy in dys] + [_const(w.shape), row(D), _const((1, D)), row(D)] + si_specs,
        out_specs=[row(D), _acc((1, D))] + so_specs,
        out_shape=[SDS((s, D), F32), SDS((1, D), F32)] + so_shapes, scratch_shapes=s_scratch,
        compiler_params=_cp(1))(*dys, w, xin, gain, dres, *s_ins)


def mix_bwd(dh1, mx, z, ya, yb, g2, w_mix, a_out, b_out, t):
    s = dh1.shape[0]

    def body(dh_ref, mx_ref, ga_ref, gb_ref, ya_ref, yb_ref, g2_ref, wm_ref, ao_ref, bo_ref,
             dmx_ref, dya_ref, dyb_ref, dga_ref, dgb_ref, dsa_ref, dob_ref, gg2_ref):
        @pl.when(pl.program_id(0) == 0)
        def _():
            gg2_ref[...] = jnp.zeros((1, D), F32)

        dmx, dg2 = _rms_bwd(dh_ref[...], mx_ref[...].astype(F32), g2_ref[...])
        gg2_ref[...] += dg2
        dmxb = dmx.astype(BF)
        dmx_ref[...] = dmxb
        dmp = _dot_nt(dmxb, wm_ref[...])

        def gate(g_ref, y_ref, dy_ref, dg_ref, w_ref, dz_ref):
            sg = jax.nn.sigmoid(g_ref[...].astype(F32))
            dyb_ = (dmp * sg).astype(BF)
            dy_ref[...] = dyb_
            dg_ref[...] = (dmp * y_ref[...].astype(F32) * sg * (1.0 - sg)).astype(BF)
            dz_ref[...] = _dot_nt(dyb_, w_ref[...]).astype(BF)

        gate(ga_ref, ya_ref, dya_ref, dga_ref, ao_ref, dsa_ref)
        gate(gb_ref, yb_ref, dyb_ref, dgb_ref, bo_ref, dob_ref)

    row = lambda w: pl.BlockSpec((t, w), lambda i: (i, 0))
    return _pc(
        body, name="mix_bwd", grid=(s // t,),
        in_specs=[row(D), row(D), pl.BlockSpec((t, D), lambda i: (i, 4)), pl.BlockSpec((t, D), lambda i: (i, 5)),
                  row(D), row(D), _const((1, D)), _const((D, D)), _const((A_W, D)), _const((D, D))],
        out_specs=[row(D)] * 5 + [row(A_W), row(D), _acc((1, D))],
        out_shape=[SDS((s, D), BF)] * 5 + [SDS((s, A_W), BF), SDS((s, D), BF), SDS((1, D), F32)],
        compiler_params=_cp(1))(dh1, mx, z, z, ya, yb, g2, w_mix, a_out, b_out)


def sgu_bwd(z, dsa, ln_g, ln_b, w_s, bs_t, t):
    s = z.shape[0]
    nt = s // t

    def body(u_ref, v_ref, dsa_ref, g_ref, b_ref, ws_ref, bs_ref,
             duv_ref, glg_ref, glb_ref, gws_ref, gbs_ref, s_sc, dvn_sc, ds_acc):
        i = pl.program_id(0)

        @pl.when(i == 0)
        def _():
            glg_ref[...] = jnp.zeros((1, A_W), F32)
            glb_ref[...] = jnp.zeros((1, A_W), F32)
            gws_ref[...] = jnp.zeros((A_G, A_C, A_C), F32)
            ds_acc[...] = jnp.zeros((A_C, A_W), F32)

        wm = _sgu_masked(ws_ref)
        rr =lax.broadcasted_iota(jnp.int32, (A_C, A_C), 0)
        cc = lax.broadcasted_iota(jnp.int32, (A_C, A_C), 1)
        tril = cc <= rr
        lng = g_ref[...]
        for ci in range(t // A_C):
            rows = pl.ds(ci * A_C, A_C)
            dgv, rstd, xhat, vn = _sgu_recompute(v_ref[rows, :].astype(F32), lng, b_ref[...])
            for g in range(A_G):
                cols = slice(g * A_GD, (g + 1) * A_GD)
                s_sc[:, cols] = _dot(wm[g], vn[:, cols]) + bs_ref[:, g:g + 1]
            gu, dgu = _gelu_and_grad(u_ref[rows, :].astype(F32))
            dsa_ = dsa_ref[rows, :].astype(F32)
            ds = dsa_ * gu
            ds_acc[...] += ds
            dsb = ds.astype(BF)
            for g in range(A_G):
                cols = slice(g * A_GD, (g + 1) * A_GD)
                gws_ref[g] += jnp.where(tril, _dot_nt(dsb[:, cols], vn[:, cols]), 0.0)
                dvn_sc[:, cols] = _dot_tn(wm[g], dsb[:, cols])
            dvn = dvn_sc[...]
            glb_ref[...] += jnp.sum(dvn, axis=0, keepdims=True)
            glg_ref[...] += jnp.sum(dvn * xhat, axis=0, keepdims=True)
            dxh = dvn * lng
            dgv_ = rstd * (dxh - jnp.mean(dxh, axis=-1, keepdims=True)
                           - xhat * jnp.mean(dxh * xhat, axis=-1, keepdims=True))
            duv_ref[rows, :A_W] = (dsa_ * s_sc[...] * dgu).astype(BF)
            duv_ref[rows, A_W:] = (dgv_ * dgv).astype(BF)

        @pl.when(i == nt - 1)
        def _():
            acc = ds_acc[...]
            for g in range(A_G):
                gbs_ref[:, g:g + 1] = jnp.sum(acc[:, g * A_GD:(g + 1) * A_GD], axis=1, keepdims=True)

    return _pc(
        body, name="sgu_bwd", grid=(nt,),
        in_specs=[pl.BlockSpec((t, A_W), lambda i: (i, 0)), pl.BlockSpec((t, A_W), lambda i: (i, 1)),
                  pl.BlockSpec((t, A_W), lambda i: (i, 0)),
                  _const((1, A_W)), _const((1, A_W)), _const((A_G, A_C, A_C)), _const((A_C, A_G))],
        out_specs=[pl.BlockSpec((t, D), lambda i: (i, 0)), _acc((1, A_W)), _acc((1, A_W)),
                   _acc((A_G, A_C, A_C)), _acc((A_C, A_G))],
        out_shape=[SDS((s, D), BF), SDS((1, A_W), F32), SDS((1, A_W), F32), SDS((A_G, A_C, A_C), F32),
                   SDS((A_C, A_G), F32)],
        scratch_shapes=[pltpu.VMEM((A_C, A_W), F32), pltpu.VMEM((A_C, A_W), F32), pltpu.VMEM((A_C, A_W), F32)],
        compiler_params=_cp(1))(z, z, dsa, ln_g, ln_b, w_s, bs_t)


def gla_bwd(z, qk32, zl, o, dob, states, wgk, bias, wn, ltri, ltri_t, t, side=None):
    s = z.shape[0]
    nt = s // t
    nc = t // B_C
    hpb = GLA_HPB
    kw, vw = hpb * B_HK, hpb * B_HV

    def body(*refs):
        ins_, s_in, outs_, s_out, scr_, s_scr = _split_side(refs, 12, 6, 5, side)
        qk_ref, v_ref, og_ref, lr_ref, o_ref, dob_ref, st_ref, wgk_ref, bias_ref, wn_ref, l_ref, lt_ref = ins_
        dqk_ref, dv_ref, dog_ref, dpre_ref, gbias_ref, gwn_ref = outs_
        dst_sc, dv_sc, dqd_sc, dkt_sc, ddec_sc = scr_
        i = pl.program_id(0)
        g = pl.program_id(1)

        @pl.when((i == 0) & (g == 0))
        def _():
            gbias_ref[...] = jnp.zeros((B_H, 1, B_HK), F32)
            gwn_ref[...] = jnp.zeros((1, B_HV), F32)
            if side:
                side.start(s_in, s_out, *s_scr)

        @pl.when(i == 0)
        def _():
            for hh in range(hpb):
                dst_sc[g * hpb + hh] = jnp.zeros((B_HV, B_HK), F32)

        lr, l, lt = lr_ref[...], l_ref[...], lt_ref[...]
        keep, keep_t = l > 0, lt > 0
        wn_ = wn_ref[...]
        last = lax.broadcasted_iota(jnp.int32, (nc, B_C, B_HK), 1) == B_C - 1
        for hh in range(hpb):
            h = g * hpb + hh
            cv, ck = slice(hh * B_HV, (hh + 1) * B_HV), slice(hh * B_HK, (hh + 1) * B_HK)
            pre, b, bl, eb, enb, etb, qd, ki, kt = _gla_decays(qk_ref[:, cv], lr, wgk_ref[:, ck], bias_ref[:, ck], l, t)
            qdb, kib, ktb = qd.astype(BF), ki.astype(BF), kt.astype(BF)
            vb = v_ref[:, cv]
            o_ = o_ref[:, cv].astype(F32)
            og = og_ref[:, cv].astype(F32)
            sog = jax.nn.sigmoid(og)
            dob_ = dob_ref[:, cv].astype(F32)
            don = dob_ * og * sog
            do, dwn = _rms_bwd(don, o_, wn_)
            gwn_ref[...] += dwn
            dog_ref[:, cv] = (dob_ * _rms(o_, wn_) * sog * (1.0 + og * (1.0 - sog))).astype(BF)
            dob16 = do.astype(BF)
            sc_t = jnp.where(keep_t, _dot_nt(kib, qdb), 0.0).astype(BF)
            dsc = jnp.where(keep, _dot_nt(dob16, vb), 0.0).astype(BF)
            dsc_t = jnp.where(keep_t, _dot_nt(vb, dob16), 0.0).astype(BF)
            dv_sc[hh] = _dot(sc_t, dob16)
            dqd_sc[hh] = _dot(dsc, kib)
            dki = _dot(dsc_t, qdb)
            for n in reversed(range(nc)):
                rows = slice(n * B_C, (n + 1) * B_C)
                dst = dst_sc[h]
                dstb = dst.astype(BF)
                stp = st_ref[n, hh]
                dv_sc[hh, rows, :] += _dot_nt(ktb[rows], dstb)
                dkt_sc[hh, rows, :] = _dot(vb[rows], dstb)
                dqd_sc[hh, rows, :] += _dot(dob16[rows], stp)
                dec = jnp.exp(bl[n * B_C:n * B_C + 1, :])
                ddec_sc[hh, n] = jnp.sum(dst * stp.astype(F32), axis=0, keepdims=True) * dec
                dst_sc[h] = dst * dec + _dot_tn(dob16[rows], qdb[rows])
            dqd, dkt = dqd_sc[hh], dkt_sc[hh]
            dv_ref[:, cv] = dv_sc[hh].astype(BF)
            dqk_ref[:, hh * B_HV:hh * B_HV + B_HK] = (dqd * eb * (B_HK ** -0.5)).astype(BF)
            dqk_ref[:, hh * B_HV + B_HK:(hh + 1) * B_HV] = (dki * enb + dkt * etb).astype(BF)
            dktkt = dkt * kt
            db3 = (dqd * qd - dki * ki - dktkt).reshape(nc, B_C, B_HK)
            dbl = jnp.sum(dktkt.reshape(nc, B_C, B_HK), axis=1, keepdims=True) + ddec_sc[hh]
            db = (db3 + jnp.where(last, dbl, 0.0)).reshape(t, B_HK)
            dla = _ldot3(lt, db)
            dpre = dla * (1.0 / 16.0) * (1.0 - jax.nn.sigmoid(pre))
            dpre_ref[:, ck] = dpre.astype(BF)
            gbias_ref[h] += jnp.sum(dpre, axis=0, keepdims=True)
        if side:
            @pl.when((i == nt - 1) & (g == B_H // hpb - 1))
            def _():
                side.finish(s_in, s_out, *s_scr)

    rv = lambda i: nt - 1 - i
    si_specs, so_specs, so_shapes, s_scratch, s_ins = _side_specs(side)
    vo, go = 2048 // vw, 3072 // vw
    tile = lambda off: pl.BlockSpec((t, vw), lambda i, g: (rv(i), off + g))
    return _pc(
        body, name="gla_bwd", grid=(nt, B_H // hpb),
        in_specs=[tile(0), tile(vo), tile(go), pl.BlockSpec((t, LANE), lambda i, g: (rv(i), 0)), tile(0), tile(0),
                  pl.BlockSpec((nc, hpb, B_HV, B_HK), lambda i, g: (rv(i), g, 0, 0)),
                  pl.BlockSpec((LANE, kw), lambda i, g: (0, g)), pl.BlockSpec((1, kw), lambda i, g: (0, g)),
                  _const((1, B_HV)), _const((t, t)), _const((t, t))] + si_specs,
        out_specs=[tile(0), tile(0), tile(0), pl.BlockSpec((t, kw), lambda i, g: (rv(i), g)),
                   _acc((B_H, 1, B_HK)), _acc((1, B_HV))] + so_specs,
        out_shape=[SDS((s, D), BF), SDS((s, D), BF), SDS((s, D), BF), SDS((s, B_H * B_HK), BF),
                   SDS((B_H, 1, B_HK), F32), SDS((1, B_HV), F32)] + so_shapes,
        scratch_shapes=[pltpu.VMEM((B_H, B_HV, B_HK), F32), pltpu.VMEM((hpb, t, B_HV), F32),
                        pltpu.VMEM((hpb, t, B_HK), F32), pltpu.VMEM((hpb, t, B_HK), F32),
                        pltpu.VMEM((hpb, nc, 1, B_HK), F32)] + s_scratch,
        compiler_params=_cp(2))(qk32, z, z, zl, o, dob, states, wgk, bias, wn, ltri, ltri_t, *s_ins)


def mm_tn(a, b, name, tk=2048):
    s, m = a.shape
    n = b.shape[1]
    bn = next(c for c in (1024, 1408, 512, 256, 128) if n % c == 0 and m * c * 4 <= 6 * 1024 * 1024)
    tk = min(tk, s)
    nk = s // tk

    def body(a_ref, b_ref, o_ref, acc):
        k = pl.program_id(1)

        @pl.when(k == 0)
        def _():
            acc[...] = jnp.zeros((m, bn), F32)

        acc[...] += _dot_tn(a_ref[...].astype(BF), b_ref[...])

        @pl.when(k == nk - 1)
        def _():
            o_ref[...] = acc[...].astype(BF)

    return _pc(
        body, name=name, grid=(n // bn, nk),
        in_specs=[pl.BlockSpec((tk, m), lambda j, k: (k, 0)), pl.BlockSpec((tk, bn), lambda j, k: (k, j))],
        out_specs=pl.BlockSpec((m, bn), lambda j, k: (0, j)),
        out_shape=SDS((m, n), BF), scratch_shapes=[pltpu.VMEM((m, bn), F32)], compiler_params=_cp(2))(a, b)


def mm_cols(a, w, blk, width, t, name):
    s, k = a.shape

    def body(a_ref, w_ref, o_ref):
        o_ref[...] = _dot(a_ref[...], w_ref[...]).astype(BF)

    return _pc(body, name=name, grid=(s // t,),
               in_specs=[pl.BlockSpec((t, k), lambda i: (i, 0)),
                         pl.BlockSpec((k, width), lambda i: (0, blk), pipeline_mode=pl.Buffered(1))],
               out_specs=pl.BlockSpec((t, width), lambda i: (i, 0)), out_shape=SDS((s, width), BF),
               compiler_params=_cp(1))(a, w)


def mm_nt_small(a, w, t, name):
    s, k = a.shape
    n = w.shape[0]

    def body(a_ref, w_ref, o_ref):
        o_ref[...] = _dot_nt(a_ref[...], w_ref[...]).astype(BF)

    return _pc(body, name=name, grid=(s // t,),
               in_specs=[pl.BlockSpec((t, k), lambda i: (i, 0)), _const((n, k))],
               out_specs=pl.BlockSpec((t, n), lambda i: (i, 0)), out_shape=SDS((s, n), BF),
               compiler_params=_cp(1))(a, w)


def _adamw(w, g, m, v):
    m = ADAM_B1 * m + (1.0 - ADAM_B1) * g
    v = ADAM_B2 * v + (1.0 - ADAM_B2) * (g * g)
    m_hat = m / (1.0 - ADAM_B1 ** ADAM_STEP)
    v_hat = v / (1.0 - ADAM_B2 ** ADAM_STEP)
    return -ADAM_LR * (m_hat / (jnp.sqrt(v_hat) + ADAM_EPS) + ADAM_WD * w), m, v


def _half_rows(rows):
    rh = rows // 2
    return rh, max(b for b in range(16, 257, 16) if rh % b == 0)


def _pc_sp(body, grid, in_specs, out_specs, out_shape, name):
    gs = pltpu.PrefetchScalarGridSpec(num_scalar_prefetch=1, grid=grid, in_specs=in_specs, out_specs=out_specs)
    return _pc(body, grid_spec=gs, out_shape=out_shape, name=name, compiler_params=_cp(len(grid)))


def adamw_halves(sc, own, sib, w, m, v, name):
    rows, cols = w.shape
    rh, br = _half_rows(rows)
    nbk = rh // br

    def body(sc_ref, own_ref, sib_ref, w_ref, m_ref, v_ref, go_ref, d_ref, mo_ref, vo_ref):
        g_ = jnp.where(pl.program_id(0) // nbk == sc_ref[0], own_ref[...], sib_ref[...])
        go_ref[...] = g_
        d_ref[...], mo_ref[...], vo_ref[...] = _adamw(w_ref[...], g_, m_ref[...], v_ref[...])

    half = pl.BlockSpec((br, cols), lambda i, sc_: (i % nbk, 0))
    blk = pl.BlockSpec((br, cols), lambda i, sc_: (i, 0))
    return _pc_sp(body, (2 * nbk,), [half, half, blk, blk, blk], [blk] * 4, [SDS((rows, cols), F32)] * 4,
                  name)(sc, own, sib, w, m, v)


def adamw_cols(sc, own, sib, w, m, v, name, cb=256):
    rows, cols = w.shape
    nk = cols // 2 // cb

    def body(sc_ref, own_ref, sib_ref, w_ref, m_ref, v_ref, go_ref, d_ref, mo_ref, vo_ref):
        g_ = jnp.where(pl.program_id(0) == sc_ref[0], own_ref[...], sib_ref[...])
        go_ref[...] = g_
        d_ref[...], mo_ref[...], vo_ref[...] = _adamw(w_ref[...], g_, m_ref[...], v_ref[...])

    half = pl.BlockSpec((rows, cb), lambda h, k, sc_: (0, k))
    blk = pl.BlockSpec((rows, cb), lambda h, k, sc_: (0, h * nk + k))
    return _pc_sp(body, (2, nk), [half, half, blk, blk, blk], [blk] * 4, [SDS((rows, cols), F32)] * 4,
                  name)(sc, own, sib, w, m, v)


def adamw_small(g, w, m, v):
    def body(g_ref, w_ref, m_ref, v_ref, d_ref, mo_ref, vo_ref):
        d_ref[...], mo_ref[...], vo_ref[...] = _adamw(w_ref[...], g_ref[...], m_ref[...], v_ref[...])

    vm = pl.BlockSpec(memory_space=pltpu.VMEM)
    return _pc(body, name="adamw_small", in_specs=[vm] * 4, out_specs=[vm] * 3, out_shape=[SDS(g.shape, F32)] * 3,
               compiler_params=pltpu.CompilerParams(vmem_limit_bytes=VMEM_LIMIT))(g, w, m, v)


def _pos():
    return lax.axis_index("x"), lax.axis_index("y"), lax.axis_index("c")


def _other_chips(x, y):
    return [(1 - x, y), (x, 1 - y), (1 - x, 1 - y)]


_ANY = pl.BlockSpec(memory_space=pltpu.HBM)


class _Side:
    def __init__(self, ins, out_shapes, nsem, start, finish):
        self.ins, self.out_shapes, self.start, self.finish = list(ins), list(out_shapes), start, finish
        self.scratch = [pltpu.SemaphoreType.DMA((nsem,)), pltpu.SemaphoreType.DMA((nsem,))]
        self.n_in, self.n_out = len(self.ins), len(self.out_shapes)


def _run_side(side, name):
    def body(*refs):
        args_ = (refs[:side.n_in], refs[side.n_in:side.n_in + side.n_out], *refs[side.n_in + side.n_out:])
        side.start(*args_)
        side.finish(*args_)

    return _pc(body, name=name, in_specs=[_ANY] * side.n_in, out_specs=[_ANY] * side.n_out,
               out_shape=side.out_shapes, scratch_shapes=side.scratch)(*side.ins)


def gather_side(bigs, tinies):
    nb, nt_ = len(bigs), len(tinies)

    def plan(ins, outs, ssem, rsem):
        x, y, c = _pos()
        me = 2 * x + y
        chips = _other_chips(x, y)
        sibling = (x, y, 1 - c)

        def copy(k, src, dst, to):
            return pltpu.make_async_remote_copy(src_ref=src, dst_ref=dst, send_sem=ssem.at[k], recv_sem=rsem.at[k],
                                                device_id=to, device_id_type=MESH)

        sends, landed, passed_on, tiny_landed = [], [], [], []
        for w in range(nb):
            rh = bigs[w].shape[0] // 2
            mine = pl.ds(pl.multiple_of(c * rh, 16), rh)
            theirs = pl.ds(pl.multiple_of((1 - c) * rh, 16), rh)
            for j, (cx, cy) in enumerate(chips):
                sends.append(copy(6 * w + j, ins[w].at[mine], outs[w].at[me, mine], (cx, cy, c)))
                blk = outs[w].at[2 * cx + cy, mine]
                landed.append((copy(6 * w + j, blk, blk, (cx, cy, c)), copy(6 * w + 3 + j, blk, blk, sibling)))
                blk = outs[w].at[2 * cx + cy, theirs]
                passed_on.append(copy(6 * w + 3 + j, blk, blk, sibling))
        for w in range(nt_):
            for j, (cx, cy) in enumerate(chips):
                k = 6 * nb + 3 * w + j
                sends.append(copy(k, ins[nb + w], outs[nb + w].at[me], (cx, cy, c)))
                blk = outs[nb + w].at[2 * cx + cy]
                tiny_landed.append(copy(k, blk, blk, (cx, cy, c)))
        return sends, landed, passed_on, tiny_landed

    def start(ins, outs, ssem, rsem):
        for cp in plan(ins, outs, ssem, rsem)[0]:
            cp.start()

    def finish(ins, outs, ssem, rsem):
        sends, landed, passed_on, tiny_landed = plan(ins, outs, ssem, rsem)
        for arrived, forward in landed:
            arrived.wait_recv()
            forward.start()
        for arrived in tiny_landed + passed_on:
            arrived.wait_recv()
        for cp in sends + [forward for _, forward in landed]:
            cp.wait_send()

    return _Side(list(bigs) + list(tinies), [SDS((4,) + a.shape, a.dtype) for a in list(bigs) + list(tinies)],
                 6 * nb + 3 * nt_, start, finish)


def swap_halves(gs, name):
    n = len(gs)

    def body(*refs):
        g_refs, sib_refs = refs[:n], refs[n:2 * n]
        ssem, rsem = refs[2 * n:]
        x, y, c = _pos()
        cps = []
        for w in range(n):
            rh = gs[w].shape[1] // 2
            give = pl.ds(pl.multiple_of((1 - c) * rh, 16), rh)
            cp = pltpu.make_async_remote_copy(src_ref=g_refs[w].at[:, give], dst_ref=sib_refs[w], send_sem=ssem.at[w],
                                              recv_sem=rsem.at[w], device_id=(x, y, 1 - c), device_id_type=MESH)
            cp.start()
            cps.append(cp)
        for cp in cps:
            cp.wait()

    return _pc(body, name=name, in_specs=[_ANY] * n, out_specs=[_ANY] * n,
               out_shape=[SDS((g.shape[0], g.shape[1] // 2, g.shape[2]), g.dtype) for g in gs],
               scratch_shapes=[pltpu.SemaphoreType.DMA((n,)), pltpu.SemaphoreType.DMA((n,))])(*gs)


def add_half(sc, g, sib, name):
    l, r, cols = g.shape
    rh, br = _half_rows(r)
    nbk = rh // br

    def body(sc_ref, g_ref, s_ref, o_ref):
        o_ref[...] = (g_ref[...].astype(F32) + s_ref[...].astype(F32)).astype(BF)

    blk = pl.BlockSpec((1, br, cols), lambda j, i, sc_: (j, i, 0))
    return _pc_sp(body, (l, nbk), [pl.BlockSpec((1, br, cols), lambda j, i, sc_: (j, sc_[0] * nbk + i, 0)), blk], blk,
                  SDS((l, rh, cols), BF), name)(sc, g, sib)


def exchange_side(ps):
    n_ = len(ps)

    def width(p_):
        return p_.shape[2] if p_.shape[0] == 4 else p_.shape[2] // 4

    def plan(p_refs, got_refs, ssem, rsem):
        x, y, c = _pos()
        cps = []
        for w in range(n_):
            wd = width(ps[w])
            for j, (cx, cy) in enumerate(_other_chips(x, y)):
                to = 2 * cx + cy
                src = p_refs[w].at[to] if ps[w].shape[0] == 4 else p_refs[w].at[0, :, pl.ds(pl.multiple_of(to * wd, LANE), wd)]
                cps.append(pltpu.make_async_remote_copy(
                    src_ref=src, dst_ref=got_refs[w].at[j], send_sem=ssem.at[3 * w + j], recv_sem=rsem.at[3 * w + j],
                    device_id=(cx, cy, c), device_id_type=MESH))
        return cps

    def start(*refs):
        for cp in plan(*refs):
            cp.start()

    def finish(*refs):
        for cp in plan(*refs):
            cp.wait()

    return _Side(ps, [SDS((3, p_.shape[1], width(p_)), p_.dtype) for p_ in ps], 3 * n_, start, finish)


def sum4(sc, p, got, name):
    _, rh, wd = got.shape
    _, br = _half_rows(2 * rh)

    def body(sc_ref, p_ref, g_ref, r_ref):
        r_ref[...] = ((p_ref[0].astype(F32) + g_ref[0].astype(F32)) + (g_ref[1].astype(F32) + g_ref[2].astype(F32)))

    own = (pl.BlockSpec((1, br, wd), lambda i, sc_: (sc_[1], i, 0)) if p.shape[0] == 4
           else pl.BlockSpec((1, br, wd), lambda i, sc_: (0, i, sc_[1])))
    return _pc_sp(body, (rh // br,), [own, pl.BlockSpec((3, br, wd), lambda i, sc_: (0, i, 0))],
                  pl.BlockSpec((br, wd), lambda i, sc_: (i, 0)), SDS((rh, wd), F32), name)(sc, p, got)


def join_halves(halves):
    n = len(halves)

    def body(*refs):
        h_refs, got_refs = refs[:n], refs[n:2 * n]
        ssem, rsem = refs[2 * n:]
        x, y, c = _pos()
        cps = []
        for w in range(n):
            cp = pltpu.make_async_remote_copy(src_ref=h_refs[w], dst_ref=got_refs[w], send_sem=ssem.at[w],
                                              recv_sem=rsem.at[w], device_id=(x, y, 1 - c), device_id_type=MESH)
            cp.start()
            cps.append(cp)
        for cp in cps:
            cp.wait()

    return _pc(body, name="join_halves", in_specs=[_ANY] * n, out_specs=[_ANY] * n,
               out_shape=[SDS(h.shape, h.dtype) for h in halves],
               scratch_shapes=[pltpu.SemaphoreType.DMA((n,)), pltpu.SemaphoreType.DMA((n,))])(*halves)


def allreduce_small(g):
    rows = g.shape[0]
    rh = rows // 2

    def body(g_ref, out_ref, sib_buf, chip_buf, sum_sc, ssem, rsem):
        x, y, c = _pos()
        me = 2 * x + y
        sibling = (x, y, 1 - c)
        mine = pl.ds(pl.multiple_of(c * rh, 8), rh)

        def copy(k, src, dst, to):
            return pltpu.make_async_remote_copy(src_ref=src, dst_ref=dst, send_sem=ssem.at[k], recv_sem=rsem.at[k],
                                                device_id=to, device_id_type=MESH)

        cp = copy(0, g_ref, sib_buf, sibling)
        cp.start()
        cp.wait()
        sum_sc[...] = g_ref[...] + sib_buf[...]
        chips = _other_chips(x, y)
        cps = [copy(1 + j, sum_sc.at[mine], chip_buf.at[me], (cx, cy, c)) for j, (cx, cy) in enumerate(chips)]
        for cp in cps:
            cp.start()
        chip_buf[me] = sum_sc[mine, :]
        for j, (cx, cy) in enumerate(chips):
            copy(1 + j, sum_sc.at[mine], chip_buf.at[2 * cx + cy], (cx, cy, c)).wait_recv()
        for cp in cps:
            cp.wait_send()
        out_ref[mine, :] = (chip_buf[0] + chip_buf[1]) + (chip_buf[2] + chip_buf[3])
        cp = copy(4, out_ref.at[mine], out_ref.at[mine], sibling)
        cp.start()
        cp.wait()

    vm = pl.BlockSpec(memory_space=pltpu.VMEM)
    return _pc(body, name="allreduce_small", in_specs=[vm], out_specs=vm, out_shape=SDS((rows, LANE), F32),
               scratch_shapes=[pltpu.VMEM((rows, LANE), F32), pltpu.VMEM((4, rh, LANE), F32), pltpu.VMEM((rows, LANE), F32),
                               pltpu.SemaphoreType.DMA((5,)), pltpu.SemaphoreType.DMA((5,))],
               compiler_params=pltpu.CompilerParams(vmem_limit_bytes=VMEM_LIMIT))(g)


def _pack_small(entries, get):
    flat = jnp.concatenate([get(n).reshape(-1).astype(F32) for n, _ in entries])
    rows = -(-flat.shape[0] // (8 * LANE)) * 8
    return jnp.pad(flat, (0, rows * LANE - flat.shape[0])).reshape(rows, LANE)


def _unpack_small(entries, packed):
    out, off = {}, 0
    flat = packed.reshape(-1)
    for name, n in entries:
        out[name] = flat[off:off + n]
        off += n
    return out


def _cols_full(blk):
    return blk.transpose(1, 0, 2).reshape(blk.shape[1], 4 * blk.shape[2])


def kernel(x, p, pre_mix_norm, w_in, a_ln_g, a_ln_b, a_spatial_w, a_spatial_b, a_out, b_gk, b_gk_bias, b_out_norm, b_out, w_mix_out, post_mix_norm, pre_ffn_norm, w_up, conv_w, conv_b, w_down, post_ffn_norm, w_ple, w_ple_gate, post_ple_norm, loss_target, m_pre_mix_norm, m_w_in, m_a_ln_g, m_a_ln_b, m_a_spatial_w, m_a_spatial_b, m_a_out, m_b_gk, m_b_gk_bias, m_b_out_norm, m_b_out, m_w_mix_out, m_post_mix_norm, m_pre_ffn_norm, m_w_up, m_conv_w, m_conv_b, m_w_down, m_post_ffn_norm, m_w_ple, m_w_ple_gate, m_post_ple_norm, v_pre_mix_norm, v_w_in, v_a_ln_g, v_a_ln_b, v_a_spatial_w, v_a_spatial_b, v_a_out, v_b_gk, v_b_gk_bias, v_b_out_norm, v_b_out, v_w_mix_out, v_post_mix_norm, v_pre_ffn_norm, v_w_up, v_conv_w, v_conv_b, v_w_down, v_post_ffn_norm, v_w_ple, v_w_ple_gate, v_post_ple_norm):
    args = dict(locals())
    order = ['pre_mix_norm', 'w_in', 'a_ln_g', 'a_ln_b', 'a_spatial_w', 'a_spatial_b', 'a_out', 'b_gk', 'b_gk_bias',
             'b_out_norm', 'b_out', 'w_mix_out', 'post_mix_norm', 'pre_ffn_norm', 'w_up', 'conv_w', 'conv_b', 'w_down',
             'post_ffn_norm', 'w_ple', 'w_ple_gate', 'post_ple_norm']
    assert sorted(BIG + TINY + tuple(n for n, _ in SMALL)) == sorted(order)
    s = x.shape[1]
    xs = x.reshape(s, D)
    ps = p.reshape(s, PLE)
    tgt = loss_target.reshape(s, D)
    t_big = min(1024, s)
    t_mid = min(512, s)
    t_small = min(256, s)
    t_gla = min(256, s)
    mx_, my_, mc_ = _pos()
    me = 2 * mx_ + my_
    sc = jnp.stack([mc_, me]).astype(jnp.int32)
    shard = lambda n: args[n].reshape(args[n].shape[1:])

    mine = {n: shard(n).astype(BF) for n in BIG}
    mine.update({n: shard(n) for n in TINY})
    fill = lambda names, gots: {n: lax.dynamic_update_slice(got, mine[n][None], (me, 0, 0)) for n, got in zip(names, gots)}
    first = ("w_in",) + TINY
    full = fill(first, _run_side(gather_side([mine["w_in"]], [mine[n] for n in TINY]), "gather_first"))
    wi = _cols_full(full["w_in"])
    seg = lambda a, b: wi[:, a:b]
    qk = [seg(1024 + h * B_HK, 1024 + (h + 1) * B_HK) for h in range(B_H)]
    kk = [seg(1536 + h * B_HK, 1536 + (h + 1) * B_HK) for h in range(B_H)]
    w_z = jnp.concatenate([seg(0, 1024)] + [m_ for h in range(B_H) for m_ in (qk[h], kk[h])]
                          + [seg(2048, 4096), seg(4112, 6160), seg(4096, 4112), jnp.zeros((D, LANE - B_RANK), BF)], axis=1)
    wgk = jnp.pad(_cols_full(full["b_gk"]).astype(BF), ((0, LANE - B_RANK), (0, 0)))
    w_conv = _cols_full(full["conv_w"])
    g1, g2, g3 = pre_mix_norm.reshape(1, D), post_mix_norm.reshape(1, D), pre_ffn_norm.reshape(1, D)
    g4, g5 = post_ffn_norm.reshape(1, D), post_ple_norm.reshape(1, D)
    ln_g, ln_b = a_ln_g.reshape(1, A_W), a_ln_b.reshape(1, A_W)
    w_s = a_spatial_w.reshape(A_G, A_C, A_C)
    bs_t = a_spatial_b.reshape(A_G, A_C).T
    gk_bias = b_gk_bias.reshape(1, B_H * B_HK)
    wn = b_out_norm.reshape(1, B_HV)
    cb = conv_b.reshape(1, 2 * D_FF)
    idx = jnp.arange(t_gla)
    ltri = ((idx[:, None] // B_C == idx[None, :] // B_C) & (idx[None, :] <= idx[:, None])).astype(BF)

    a, z, qk32, *gots = norm_matmul(xs, g1, w_z, D, t_big, "in_proj", nblk=6, f32_blk=1,
                                    side=gather_side([mine[n] for n in BIG[1:]], []))
    full.update(fill(BIG[1:], gots))
    w_aout, w_ple_f = _cols_full(full["a_out"]), _cols_full(full["w_ple"])
    w_bout, w_mix, w_pg = (full[n].reshape(D, D) for n in ("b_out", "w_mix_out", "w_ple_gate"))
    w_dn, w_up3 = full["w_down"].reshape(D_FF, D), full["w_up"]
    zl = mm_cols(a, w_z, 48, LANE, t_big, "lr_proj")
    sa = sgu_fwd(z, ln_g, ln_b, w_s, bs_t, t_mid)
    ob, o, states = gla_fwd(z, qk32, zl, wgk, gk_bias, wn, ltri, t_gla)
    ya, yb, mp, mx, h1 = mix_fwd(sa, ob, z, xs, w_aout, w_bout, w_mix, g2, t_small)
    c, up = norm_matmul(h1, g3, w_up3, 1408, t_big, "up_proj")
    ff, cg, cv = ffn_gate_fwd(up, w_conv, cb, t_small)
    f, h2, pg, pe, dy, loss = out_fwd(ff, h1, ps, tgt, w_dn, w_pg, w_ple_f, g4, g5, t_small)

    dh2, dpe, dpg, df, dff, gg5, gg4 = out_bwd(dy, pg, pe, f, g5, g4, w_pg, w_dn, t_small)
    dup_g, dup_v, gcw_g, gcw_v, gcb_g, gcb_v = ffn_gate_bwd(up, cg, cv, dff, w_conv, t_small)
    dh1, gg3 = nt_normbwd([dup_g, dup_v], _cols_full(w_up3), h1, g3, dh2, t_small, "up_bwd")
    dmx, dya, dyb, dga, dgb, dsa, dob, gg2 = mix_bwd(dh1, mx, z, ya, yb, g2, w_mix, w_aout, w_bout, t_small)
    duv, g_lng, g_lnb, g_ws, g_bst = sgu_bwd(z, dsa, ln_g, ln_b, w_s, bs_t, t_mid)

    grads = {
        "a_out": mm_tn(sa, dya, "dw_a_out")[None],
        "b_out": mm_tn(ob, dyb, "dw_b_out").reshape(4, D // 4, D),
        "w_mix_out": mm_tn(mp, dmx, "dw_mix").reshape(4, D // 4, D),
        "w_up": jnp.concatenate([mm_tn(c, dup_g, "dw_up_g"), mm_tn(c, dup_v, "dw_up_v")], axis=1)[None],
        "w_down": mm_tn(ff, df, "dw_down").reshape(4, D_FF // 4, D),
        "w_ple": mm_tn(ps, dpe, "dw_ple")[None],
        "w_ple_gate": mm_tn(h2, dpg, "dw_ple_gate").reshape(4, D // 4, D),
    }

    def chip_partials(names):
        gs = [grads[n] for n in names]
        return [add_half(sc, g, sib, "partial_" + n)
                for n, g, sib in zip(names, gs, swap_halves(gs, "swap_halves_" + names[0]))]

    parts = dict(zip(BIG[1:], chip_partials(BIG[1:])))
    dqk, dvb, dog, dpre, g_gkb, g_wn, *gots = gla_bwd(z, qk32, zl, o, dob, states, wgk, gk_bias, wn, ltri, ltri.T,
                                                      t_gla, side=exchange_side([parts[n] for n in BIG[1:]]))
    got = dict(zip(BIG[1:], gots))
    dlr = mm_nt_small(dpre, wgk, t_mid, "dlr")
    segs = [duv, dqk, dvb, dog, dga, dgb, dlr]

    gz = [mm_tn(a, sg_, "dw_in_%d" % k) for k, sg_ in enumerate(segs)]
    gq = [gz[1][:, h * 256:h * 256 + B_HK] for h in range(B_H)]
    gk = [gz[1][:, h * 256 + B_HK:(h + 1) * 256] for h in range(B_H)]
    g_in = jnp.concatenate([gz[0]] + gq + gk + [gz[2], gz[3], gz[6][:, :B_RANK], gz[4], gz[5]], axis=1)
    grads["w_in"] = g_in.reshape(D, 4, 1540).transpose(1, 0, 2)
    parts["w_in"], = chip_partials(("w_in",))
    dx, gg1, got["w_in"] = nt_normbwd(segs, w_z, xs, g1, dh1, t_small, "in_bwd",
                                      side=exchange_side([parts["w_in"]]))

    reds = [sum4(sc, parts[n], got[n], "sum_" + n) for n in BIG]
    outs = {}
    for n, red, sib in zip(BIG, reds, join_halves(reds)):
        if n == "w_in":
            res = adamw_cols(sc, red.T, sib.T, shard(n).T, shard("m_" + n).T, shard("v_" + n).T, "adamw_" + n)
            res = [r_.T for r_ in res]
        else:
            res = adamw_halves(sc, red, sib, shard(n), shard("m_" + n), shard("v_" + n), "adamw_" + n)
        outs[n] = [r_.reshape(args[n].shape) for r_ in res]

    small_g = {
        "pre_mix_norm": gg1, "a_ln_g": g_lng, "a_ln_b": g_lnb, "a_spatial_w": g_ws, "a_spatial_b": g_bst.T,
        "b_gk_bias": g_gkb, "b_out_norm": g_wn, "post_mix_norm": gg2, "pre_ffn_norm": gg3,
        "conv_b": jnp.concatenate([gcb_g, gcb_v], axis=1), "post_ffn_norm": gg4, "post_ple_norm": gg5,
        "b_gk": mm_tn(zl, dpre, "dw_gk")[:B_RANK], "conv_w": jnp.concatenate([gcw_g, gcw_v], axis=1),
    }
    red_entries = SMALL + (("b_gk", B_RANK * 512), ("conv_w", 3 * 2 * D_FF))
    g_fin = _unpack_small(red_entries, allreduce_small(_pack_small(red_entries, lambda n: small_g[n])))
    g_fin["b_gk"] = lax.dynamic_slice(g_fin["b_gk"].reshape(B_RANK, 512), (0, me * B_HK), (B_RANK, B_HK))
    g_fin["conv_w"] = lax.dynamic_slice(g_fin["conv_w"].reshape(3, 2 * D_FF), (0, me * 1408), (3, 1408))
    upd_entries = SMALL + (("b_gk", B_RANK * B_HK), ("conv_w", 3 * 1408))
    res = adamw_small(*[_pack_small(upd_entries, get) for get in
                        (lambda n: g_fin[n], lambda n: args[n], lambda n: args["m_" + n], lambda n: args["v_" + n])])
    res = [_unpack_small(upd_entries, r_) for r_ in res]
    for n, _ in upd_entries:
        outs[n] = [r_[n].reshape(args[n].shape) for r_ in [g_fin] + res]

    total = lax.psum(loss[0, 0], ("x", "y", "c"))
    return (total, dx.reshape(x.shape), *[outs[n][0] for n in order], *[outs[n][1] for n in order],
            *[outs[n][2] for n in order], *[outs[n][3] for n in order])
```

```python
import functools
import math

import jax
import jax.numpy as jnp
from jax import lax
from jax.experimental import pallas as pl
from jax.experimental.pallas import tpu as pltpu

F32 = jnp.float32
BF = jnp.bfloat16
SDS = jax.ShapeDtypeStruct
MESH = pl.DeviceIdType.MESH

EPS = 1e-6
D = 1024
A_W = 512
A_G, A_C = 8, 128
A_GD = A_W // A_G
B_H, B_HK, B_HV = 4, 128, 256
B_C = 64
GLA_HPB = 4
B_RANK = 16
D_FF = 2816
PLE = 256
ZW = 6272
LANE = 128
VMEM_LIMIT = 60 * 1024 * 1024

ADAM_LR, ADAM_B1, ADAM_B2, ADAM_EPS, ADAM_WD, ADAM_STEP = 0.001, 0.9, 0.999, 1e-08, 0.01, 10

_GC = math.sqrt(2.0 / math.pi)
_GA = 0.044715

BIG = ("w_in", "a_out", "b_out", "w_mix_out", "w_up", "w_down", "w_ple", "w_ple_gate")
TINY = ("b_gk", "conv_w")
SMALL = (("pre_mix_norm", 1024), ("a_ln_g", 512), ("a_ln_b", 512), ("a_spatial_w", 131072),
         ("a_spatial_b", 1024), ("b_gk_bias", 512), ("b_out_norm", 256), ("post_mix_norm", 1024),
         ("pre_ffn_norm", 1024), ("conv_b", 5632), ("post_ffn_norm", 1024), ("post_ple_norm", 1024))


def _pc(body, **kw):
    return pl.pallas_call(body, **kw)


def _cp(n):
    return pltpu.CompilerParams(dimension_semantics=("arbitrary",) * n, vmem_limit_bytes=VMEM_LIMIT)


def _const(shape):
    nd = len(shape)
    return pl.BlockSpec(shape, lambda *_: (0,) * nd, pipeline_mode=pl.Buffered(1))


def _acc(shape):
    nd = len(shape)
    return pl.BlockSpec(shape, lambda *_: (0,) * nd)


def _dot(a, b):
    return jnp.dot(a, b, preferred_element_type=F32)


def _dot_nt(a, b):
    return lax.dot_general(a, b, (((1,), (1,)), ((), ())), preferred_element_type=F32)


def _dot_tn(a, b):
    return lax.dot_general(a, b, (((0,), (0,)), ((), ())), preferred_element_type=F32)


def _gelu(x):
    return 0.5 * x * (1.0 + jnp.tanh(_GC * (x + _GA * x * x * x)))


def _gelu_and_grad(x):
    x2 = x * x
    s = 0.5 * jnp.tanh((_GC * x) * (1.0 + _GA * x2)) + 0.5
    g = x * s
    return g, s + g * (1.0 - s) * ((6.0 * _GC * _GA) * x2 + 2.0 * _GC)


def _log_sigmoid(x):
    return jnp.minimum(x, 0.0) - jnp.log(1.0 + jnp.exp(-jnp.abs(x)))


def _rms(x, g):
    return x * lax.rsqrt(jnp.mean(x * x, axis=-1, keepdims=True) + EPS) * g


def _rms_bwd(dy, x, g):
    r = lax.rsqrt(jnp.mean(x * x, axis=-1, keepdims=True) + EPS)
    n = x * r
    dn = dy * g
    dx = r * (dn - n * jnp.mean(dn * n, axis=-1, keepdims=True))
    return dx, jnp.sum(dy * n, axis=0, keepdims=True)


def _ldot3(l, x):
    h = x.astype(BF)
    r = x - h.astype(F32)
    m = r.astype(BF)
    lo = (r - m.astype(F32)).astype(BF)
    return _dot(l, h) + _dot(l, m) + _dot(l, lo)


def _split_side(refs, n_in, n_out, n_scratch, side):
    si, so = (side.n_in, side.n_out) if side else (0, 0)
    cuts = [n_in, si, n_out, so, n_scratch]
    out, at = [], 0
    for c in cuts:
        out.append(refs[at:at + c])
        at += c
    return (*out, refs[at:])


def _side_specs(side):
    return ([_ANY] * side.n_in, [_ANY] * side.n_out, side.out_shapes, side.scratch, side.ins) if side else ([],) * 5


def norm_matmul(x, g, w, bn, t, name, nblk=None, f32_blk=None, side=None):
    s, dm = x.shape
    if w.ndim == 3:
        nblk = w.shape[0]
        w_spec = pl.BlockSpec((None, dm, bn), lambda i, j: (j, 0, 0))
    else:
        nblk = nblk or w.shape[1] // bn
        w_spec = pl.BlockSpec((dm, bn), lambda i, j: (0, j))
    extra = f32_blk is not None
    nt = s // t

    def body(*refs):
        (x_ref, g_ref, w_ref), s_in, outs, s_out, (a_sc,), s_scr = _split_side(refs, 3, 2 + extra, 1, side)
        a_ref, z_ref = outs[:2]
        i, j = pl.program_id(0), pl.program_id(1)
        if side:
            @pl.when((i == 0) & (j == 0))
            def _():
                side.start(s_in, s_out, *s_scr)

        @pl.when(j == 0)
        def _():
            a = _rms(x_ref[...], g_ref[...]).astype(BF)
            a_sc[...] = a
            a_ref[...] = a

        acc = _dot(a_sc[...], w_ref[...])
        z_ref[...] = acc.astype(BF)
        if extra:
            @pl.when(j == f32_blk)
            def _():
                outs[2][...] = acc
        if side:
            @pl.when((i == nt - 1) & (j == nblk - 1))
            def _():
                side.finish(s_in, s_out, *s_scr)

    si_specs, so_specs, so_shapes, s_scratch, s_ins = _side_specs(side)
    return _pc(
        body, name=name, grid=(nt, nblk),
        in_specs=[pl.BlockSpec((t, dm), lambda i, j: (i, 0)), _const((1, dm)), w_spec] + si_specs,
        out_specs=[pl.BlockSpec((t, dm), lambda i, j: (i, 0)), pl.BlockSpec((t, bn), lambda i, j: (i, j))]
        + [pl.BlockSpec((t, bn), lambda i, j: (i, 0))] * extra + so_specs,
        out_shape=[SDS((s, dm), BF), SDS((s, nblk * bn), BF)] + [SDS((s, bn), F32)] * extra + so_shapes,
        scratch_shapes=[pltpu.VMEM((t, dm), BF)] + s_scratch, compiler_params=_cp(2))(x, g, w, *s_ins)


def _sgu_weights(wc_ref, transposed):
    r = lax.broadcasted_iota(jnp.int32, (A_C, A_G * A_C), 0)
    c = lax.broadcasted_iota(jnp.int32, (A_C, A_G * A_C), 1) & (A_C - 1)
    return jnp.where((r <= c) if transposed else (c <= r), wc_ref[...], 0.0).astype(BF)


def _sgu_spread(xs, bdm):
    return jnp.concatenate([jnp.tile(x, (A_G, 1)) * bdm for x in xs], axis=1)


def _sgu_recompute(v, lng, lnb):
    gv, dgv = _gelu_and_grad(v)
    mu = jnp.mean(gv, axis=-1, keepdims=True)
    xc = gv - mu
    rstd = lax.rsqrt(jnp.mean(xc * xc, axis=-1, keepdims=True) + EPS)
    xhat = xc * rstd
    return dgv, rstd, xhat, (xhat * lng + lnb).astype(BF)


def sgu_fwd(z, ln_g, ln_b, w_cat, bias_full, bdm, t):
    s = z.shape[0]
    nch = t // A_C

    def body(u_ref, v_ref, g_ref, b_ref, wc_ref, bias_ref, bdm_ref, sa_ref):
        vns = [_sgu_recompute(v_ref[pl.ds(ci * A_C, A_C), :].astype(F32), g_ref[...], b_ref[...])[3]
               for ci in range(nch)]
        mixed = _dot(_sgu_weights(wc_ref, False), _sgu_spread(vns, bdm_ref[...]))
        for ci in range(nch):
            rows = pl.ds(ci * A_C, A_C)
            s_ = mixed[:, ci * A_W:(ci + 1) * A_W] + bias_ref[...]
            sa_ref[rows, :] = (_gelu(u_ref[rows, :].astype(F32)) * s_).astype(BF)

    return _pc(
        body, name="sgu_fwd", grid=(s // t,),
        in_specs=[pl.BlockSpec((t, A_W), lambda i: (i, 0)), pl.BlockSpec((t, A_W), lambda i: (i, 1)),
                  _const((1, A_W)), _const((1, A_W)), _const((A_C, A_G * A_C)), _const((A_C, A_W)),
                  _const((A_G * A_C, A_W))],
        out_specs=pl.BlockSpec((t, A_W), lambda i: (i, 0)),
        out_shape=SDS((s, A_W), BF), compiler_params=_cp(1))(z, z, ln_g, ln_b, w_cat, bias_full, bdm)


def _gla_decays(qk, lr, wgk, bias, l, t):
    nc = t // B_C
    q = qk[:, :B_HK].astype(F32) * (B_HK ** -0.5)
    k = qk[:, B_HK:].astype(F32)
    pre = _dot(lr, wgk) + bias
    la = _log_sigmoid(pre) * (1.0 / 16.0)
    b = _ldot3(l, la)
    b3 = b.reshape(nc, B_C, B_HK)
    bl = jnp.broadcast_to(b3[:, B_C - 1:B_C, :], (nc, B_C, B_HK)).reshape(t, B_HK)
    eb, enb, etb = jnp.exp(b), jnp.exp(-b), jnp.exp(bl - b)
    return pre, b, bl, eb, enb, etb, q * eb, k * enb, k * etb


def gla_fwd(z, qk32, zl, wgk, bias, wn, ltri, t):
    s = z.shape[0]
    nc = t // B_C
    hpb = GLA_HPB
    kw, vw = hpb * B_HK, hpb * B_HV

    def body(qk_ref, v_ref, og_ref, lr_ref, wgk_ref, bias_ref, wn_ref, l_ref, ob_ref, o_ref, st_ref, st_sc, o_sc):
        g = pl.program_id(1)

        @pl.when(pl.program_id(0) == 0)
        def _():
            for hh in range(hpb):
                st_sc[g * hpb + hh] = jnp.zeros((B_HV, B_HK), F32)

        lr, l = lr_ref[...], l_ref[...]
        for hh in range(hpb):
            h = g * hpb + hh
            cv, ck = slice(hh * B_HV, (hh + 1) * B_HV), slice(hh * B_HK, (hh + 1) * B_HK)
            _, _, bl, _, _, _, qd, ki, kt = _gla_decays(qk_ref[:, cv], lr, wgk_ref[:, ck], bias_ref[:, ck], l, t)
            qd, ki, kt = qd.astype(BF), ki.astype(BF), kt.astype(BF)
            vb = v_ref[:, cv]
            sc = jnp.where(l > 0, _dot_nt(qd, ki), 0.0).astype(BF)
            o_sc[hh] = _dot(sc, vb)
            for n in range(nc):
                rows = slice(n * B_C, (n + 1) * B_C)
                st = st_sc[h]
                stb = st.astype(BF)
                st_ref[n, hh] = stb
                o_sc[hh, rows, :] += _dot_nt(qd[rows], stb)
                st_sc[h] = st * jnp.exp(bl[n * B_C:n * B_C + 1, :]) + _dot_tn(vb[rows], kt[rows])
            ob = o_sc[hh].astype(BF)
            o_ref[:, cv] = ob
            og = og_ref[:, cv].astype(F32)
            ob_ref[:, cv] = (_rms(ob.astype(F32), wn_ref[...]) * og * jax.nn.sigmoid(og)).astype(BF)

    vo, go = 2048 // vw, 3072 // vw
    return _pc(
        body, name="gla_fwd", grid=(s // t, B_H // hpb),
        in_specs=[pl.BlockSpec((t, vw), lambda i, g: (i, g)), pl.BlockSpec((t, vw), lambda i, g: (i, vo + g)),
                  pl.BlockSpec((t, vw), lambda i, g: (i, go + g)), pl.BlockSpec((t, LANE), lambda i, g: (i, 0)),
                  pl.BlockSpec((LANE, kw), lambda i, g: (0, g)), pl.BlockSpec((1, kw), lambda i, g: (0, g)),
                  _const((1, B_HV)), _const((t, t))],
        out_specs=[pl.BlockSpec((t, vw), lambda i, g: (i, g)), pl.BlockSpec((t, vw), lambda i, g: (i, g)),
                   pl.BlockSpec((nc, hpb, B_HV, B_HK), lambda i, g: (i, g, 0, 0))],
        out_shape=[SDS((s, D), BF), SDS((s, D), BF), SDS((s // B_C, B_H, B_HV, B_HK), BF)],
        scratch_shapes=[pltpu.VMEM((B_H, B_HV, B_HK), F32), pltpu.VMEM((hpb, t, B_HV), F32)],
        compiler_params=_cp(2))(qk32, z, z, zl, wgk, bias, wn, ltri)


def mix_fwd(sa, ob, z, x, a_out, b_out, w_mix, g2, t):
    s = x.shape[0]

    def body(sa_ref, ob_ref, ga_ref, gb_ref, x_ref, ao_ref, bo_ref, wm_ref, g2_ref,
             ya_ref, yb_ref, mp_ref, mx_ref, h1_ref):
        ya = _dot(sa_ref[...], ao_ref[...]).astype(BF)
        yb = _dot(ob_ref[...], bo_ref[...]).astype(BF)
        ya_ref[...] = ya
        yb_ref[...] = yb
        mp = (jax.nn.sigmoid(ga_ref[...].astype(F32)) * ya.astype(F32)
              + jax.nn.sigmoid(gb_ref[...].astype(F32)) * yb.astype(F32)).astype(BF)
        mp_ref[...] = mp
        mx = _dot(mp, wm_ref[...]).astype(BF)
        mx_ref[...] = mx
        h1_ref[...] = x_ref[...] + _rms(mx.astype(F32), g2_ref[...])

    row = lambda w: pl.BlockSpec((t, w), lambda i: (i, 0))
    return _pc(
        body, name="mix_fwd", grid=(s // t,),
        in_specs=[row(A_W), row(D), pl.BlockSpec((t, D), lambda i: (i, 4)), pl.BlockSpec((t, D), lambda i: (i, 5)),
                  row(D), _const((A_W, D)), _const((D, D)), _const((D, D)), _const((1, D))],
        out_specs=[row(D)] * 5,
        out_shape=[SDS((s, D), BF)] * 4 + [SDS((s, D), F32)],
        compiler_params=_cp(1))(sa, ob, z, z, x, a_out, b_out, w_mix, g2)


def ffn_gate_fwd(up, conv_w, conv_b, t):
    s = up.shape[0]
    bn = 1408
    hb = t // 8

    def body(ug_ref, uv_ref, hg_ref, hv_ref, wg_ref, wv_ref, bg_ref, bv_ref, ff_ref, cg_ref, cv_ref):
        live = (pl.program_id(1) > 0).astype(F32)

        def branch(u_ref, h_ref, w_ref, b_ref, c_ref):
            ext = jnp.concatenate([h_ref[...].astype(F32) * live, u_ref[...].astype(F32)], axis=0)
            w = w_ref[...]
            c = (b_ref[...] + w[0:1] * pltpu.roll(ext, 2, 0) + w[1:2] * pltpu.roll(ext, 1, 0) + w[2:3] * ext)[8:]
            c = c.astype(BF)
            c_ref[...] = c
            return c.astype(F32)

        g = _gelu(branch(ug_ref, hg_ref, wg_ref, bg_ref, cg_ref))
        ff_ref[...] = (g * branch(uv_ref, hv_ref, wv_ref, bv_ref, cv_ref)).astype(BF)

    halo = lambda off: pl.BlockSpec((8, bn), lambda j, i: (jnp.maximum(i * hb - 1, 0), j + off))
    out = pl.BlockSpec((t, bn), lambda j, i: (i, j))
    return _pc(
        body, name="ffn_gate_fwd", grid=(2, s // t),
        in_specs=[pl.BlockSpec((t, bn), lambda j, i: (i, j)), pl.BlockSpec((t, bn), lambda j, i: (i, j + 2)),
                  halo(0), halo(2),
                  pl.BlockSpec((3, bn), lambda j, i: (0, j)), pl.BlockSpec((3, bn), lambda j, i: (0, j + 2)),
                  pl.BlockSpec((1, bn), lambda j, i: (0, j)), pl.BlockSpec((1, bn), lambda j, i: (0, j + 2))],
        out_specs=[out] * 3, out_shape=[SDS((s, D_FF), BF)] * 3,
        compiler_params=_cp(2))(up, up, up, up, conv_w, conv_w, conv_b, conv_b)


def out_fwd(ff, h1, p, tgt, w_down, w_pg, w_ple, g4, g5, t):
    s = h1.shape[0]

    def body(ff_ref, h1_ref, p_ref, t_ref, wd_ref, wpg_ref, wpl_ref, g4_ref, g5_ref,
             f_ref, h2_ref, pg_ref, pe_ref, dy_ref, loss_ref):
        @pl.when(pl.program_id(0) == 0)
        def _():
            loss_ref[...] = jnp.zeros((1, 1), F32)

        f = _dot(ff_ref[...], wd_ref[...]).astype(BF)
        f_ref[...] = f
        h2 = h1_ref[...] + _rms(f.astype(F32), g4_ref[...])
        h2b = h2.astype(BF)
        h2_ref[...] = h2b
        pg = _dot(h2b, wpg_ref[...]).astype(BF)
        pe = _dot(p_ref[...].astype(BF), wpl_ref[...]).astype(BF)
        pg_ref[...] = pg
        pe_ref[...] = pe
        y = h2 + _rms(jax.nn.sigmoid(pg.astype(F32)) * pe.astype(F32), g5_ref[...])
        err = y - t_ref[...]
        dy_ref[...] = err * (1.0 / D)
        loss_ref[...] += (0.5 / D) * jnp.sum(err * err)

    row = lambda w: pl.BlockSpec((t, w), lambda i: (i, 0))
    return _pc(
        body, name="out_fwd", grid=(s // t,),
        in_specs=[row(D_FF), row(D), row(PLE), row(D), _const((D_FF, D)), _const((D, D)), _const((PLE, D)),
                  _const((1, D)), _const((1, D))],
        out_specs=[row(D)] * 5 + [_acc((1, 1))],
        out_shape=[SDS((s, D), BF)] * 4 + [SDS((s, D), F32), SDS((1, 1), F32)],
        compiler_params=_cp(1))(ff, h1, p, tgt, w_down, w_pg, w_ple, g4, g5)


def out_bwd(dy, pg, pe, f, g5, g4, w_pg, w_down, t):
    s = dy.shape[0]

    def body(dy_ref, pg_ref, pe_ref, f_ref, g5_ref, g4_ref, wpg_ref, wd_ref,
             dh2_ref, dpe_ref, dpg_ref, df_ref, dff_ref, gg5_ref, gg4_ref):
        @pl.when(pl.program_id(0) == 0)
        def _():
            gg5_ref[...] = jnp.zeros((1, D), F32)
            gg4_ref[...] = jnp.zeros((1, D), F32)

        dy_ = dy_ref[...]
        pg_ = pg_ref[...].astype(F32)
        pe_ = pe_ref[...].astype(F32)
        sg = jax.nn.sigmoid(pg_)
        dple, dg5 = _rms_bwd(dy_, sg * pe_, g5_ref[...])
        gg5_ref[...] += dg5
        dpe_ref[...] = (dple * sg).astype(BF)
        dpg = (dple * pe_ * sg * (1.0 - sg)).astype(BF)
        dpg_ref[...] = dpg
        dh2 = dy_ + _dot_nt(dpg, wpg_ref[...])
        dh2_ref[...] = dh2
        df, dg4 = _rms_bwd(dh2, f_ref[...].astype(F32), g4_ref[...])
        gg4_ref[...] += dg4
        dfb = df.astype(BF)
        df_ref[...] = dfb
        dff_ref[...] = _dot_nt(dfb, wd_ref[...]).astype(BF)

    row = lambda w: pl.BlockSpec((t, w), lambda i: (i, 0))
    return _pc(
        body, name="out_bwd", grid=(s // t,),
        in_specs=[row(D), row(D), row(D), row(D), _const((1, D)), _const((1, D)), _const((D, D)), _const((D_FF, D))],
        out_specs=[row(D), row(D), row(D), row(D), row(D_FF), _acc((1, D)), _acc((1, D))],
        out_shape=[SDS((s, D), F32), SDS((s, D), BF), SDS((s, D), BF), SDS((s, D), BF), SDS((s, D_FF), BF),
                   SDS((1, D), F32), SDS((1, D), F32)],
        compiler_params=_cp(1))(dy, pg, pe, f, g5, g4, w_pg, w_down)


def ffn_gate_bwd(up, cg, cv, dff, conv_w, t):
    s = up.shape[0]
    bn = 1408
    hb = t // 8
    nt = s // t
    r = t + 8

    def body(ug_ref, uv_ref, cg_ref, cv_ref, cag_ref, cav_ref, d_ref, da_ref, wg_ref, wv_ref,
             dug_ref, duv_ref, gwg_ref, gwv_ref, gbg_ref, gbv_ref):
        i = pl.program_id(1)

        @pl.when(i == 0)
        def _():
            gwg_ref[...] = jnp.zeros((3, bn), F32)
            gwv_ref[...] = jnp.zeros((3, bn), F32)
            gbg_ref[...] = jnp.zeros((1, bn), F32)
            gbv_ref[...] = jnp.zeros((1, bn), F32)

        def gate(c_g, c_v, d_):
            gl, dgl = _gelu_and_grad(c_g.astype(F32))
            d_ = d_.astype(F32)
            return d_ * c_v.astype(F32) * dgl, d_ * gl

        dg, dv = gate(cg_ref[...], cv_ref[...], d_ref[...])
        nxt = da_ref[...].astype(F32) * (i < nt - 1).astype(F32)
        dg_n, dv_n = gate(cag_ref[...], cav_ref[...], nxt)

        def back(dc, dc_next, u_ref, w_ref, du_ref, gw_ref, gb_ref):
            w = w_ref[...]
            d_ext = jnp.concatenate([dc, dc_next], axis=0)
            d1, d2 = pltpu.roll(d_ext, r - 1, 0)[:t], pltpu.roll(d_ext, r - 2, 0)[:t]
            du_ref[...] = (w[2:3] * dc + w[1:2] * d1 + w[0:1] * d2).astype(BF)
            u = u_ref[...].astype(F32)
            gw_ref[0:1, :] += jnp.sum(d2 * u, axis=0, keepdims=True)
            gw_ref[1:2, :] += jnp.sum(d1 * u, axis=0, keepdims=True)
            gw_ref[2:3, :] += jnp.sum(dc * u, axis=0, keepdims=True)
            gb_ref[...] += jnp.sum(dc, axis=0, keepdims=True)

        back(dg, dg_n, ug_ref, wg_ref, dug_ref, gwg_ref, gbg_ref)
        back(dv, dv_n, uv_ref, wv_ref, duv_ref, gwv_ref, gbv_ref)

    tile = lambda off: pl.BlockSpec((t, bn), lambda j, i: (i, j + off))
    after = lambda off: pl.BlockSpec((8, bn), lambda j, i: (jnp.minimum((i + 1) * hb, nt * hb - 1), j + off))
    cw = lambda off: pl.BlockSpec((3, bn), lambda j, i: (0, j + off))
    cb = lambda off: pl.BlockSpec((1, bn), lambda j, i: (0, j + off))
    return _pc(
        body, name="ffn_gate_bwd", grid=(2, nt),
        in_specs=[tile(0), tile(2), tile(0), tile(0), after(0), after(0), tile(0), after(0), cw(0), cw(2)],
        out_specs=[tile(0), tile(0), cw(0), cw(0), cb(0), cb(0)],
        out_shape=[SDS((s, D_FF), BF), SDS((s, D_FF), BF), SDS((3, D_FF), F32), SDS((3, D_FF), F32),
                   SDS((1, D_FF), F32), SDS((1, D_FF), F32)],
        compiler_params=_cp(2))(up, up, cg, cv, cg, cv, dff, dff, conv_w, conv_w)


def nt_normbwd(dys, w, xin, gain, dres, t, name, side=None):
    s = xin.shape[0]
    nt = s // t
    np_ = len(dys)

    def body(*refs):
        ins_, s_in, (dx_ref, gg_ref), s_out, _, s_scr = _split_side(refs, np_ + 4, 2, 0, side)
        dy_refs = ins_[:np_]
        w_ref, x_ref, g_ref, dres_ref = ins_[np_:]
        i = pl.program_id(0)

        @pl.when(i == 0)
        def _():
            gg_ref[...] = jnp.zeros((1, D), F32)
            if side:
                side.start(s_in, s_out, *s_scr)

        acc = _dot_nt(jnp.concatenate([r_[...] for r_ in dy_refs], axis=1), w_ref[...])
        dxn, dg = _rms_bwd(acc, x_ref[...], g_ref[...])
        dx_ref[...] = dres_ref[...] + dxn
        gg_ref[...] += dg
        if side:
            @pl.when(i == nt - 1)
            def _():
                side.finish(s_in, s_out, *s_scr)

    row = lambda width: pl.BlockSpec((t, width), lambda i: (i, 0))
    si_specs, so_specs, so_shapes, s_scratch, s_ins = _side_specs(side)
    assert sum(dy.shape[1] for dy in dys) == w.shape[1]
    return _pc(
        body, name=name, grid=(nt,),
        in_specs=[row(dy.shape[1]) for dy in dys] + [_const(w.shape), row(D), _const((1, D)), row(D)] + si_specs,
        out_specs=[row(D), _acc((1, D))] + so_specs,
        out_shape=[SDS((s, D), F32), SDS((1, D), F32)] + so_shapes, scratch_shapes=s_scratch,
        compiler_params=_cp(1))(*dys, w, xin, gain, dres, *s_ins)


def mix_bwd(dh1, mx, z, ya, yb, g2, w_mix, a_out, b_out, t):
    s = dh1.shape[0]

    def body(dh_ref, mx_ref, ga_ref, gb_ref, ya_ref, yb_ref, g2_ref, wm_ref, ao_ref, bo_ref,
             dmx_ref, dya_ref, dyb_ref, dga_ref, dgb_ref, dsa_ref, dob_ref, gg2_ref):
        @pl.when(pl.program_id(0) == 0)
        def _():
            gg2_ref[...] = jnp.zeros((1, D), F32)

        dmx, dg2 = _rms_bwd(dh_ref[...], mx_ref[...].astype(F32), g2_ref[...])
        gg2_ref[...] += dg2
        dmxb = dmx.astype(BF)
        dmx_ref[...] = dmxb
        dmp = _dot_nt(dmxb, wm_ref[...])

        def gate(g_ref, y_ref, dy_ref, dg_ref, w_ref, dz_ref):
            sg = jax.nn.sigmoid(g_ref[...].astype(F32))
            dyb_ = (dmp * sg).astype(BF)
            dy_ref[...] = dyb_
            dg_ref[...] = (dmp * y_ref[...].astype(F32) * sg * (1.0 - sg)).astype(BF)
            dz_ref[...] = _dot_nt(dyb_, w_ref[...]).astype(BF)

        gate(ga_ref, ya_ref, dya_ref, dga_ref, ao_ref, dsa_ref)
        gate(gb_ref, yb_ref, dyb_ref, dgb_ref, bo_ref, dob_ref)

    row = lambda w: pl.BlockSpec((t, w), lambda i: (i, 0))
    return _pc(
        body, name="mix_bwd", grid=(s // t,),
        in_specs=[row(D), row(D), pl.BlockSpec((t, D), lambda i: (i, 4)), pl.BlockSpec((t, D), lambda i: (i, 5)),
                  row(D), row(D), _const((1, D)), _const((D, D)), _const((A_W, D)), _const((D, D))],
        out_specs=[row(D)] * 5 + [row(A_W), row(D), _acc((1, D))],
        out_shape=[SDS((s, D), BF)] * 5 + [SDS((s, A_W), BF), SDS((s, D), BF), SDS((1, D), F32)],
        compiler_params=_cp(1))(dh1, mx, z, z, ya, yb, g2, w_mix, a_out, b_out)


def sgu_bwd(z, dsa, ln_g, ln_b, w_cat, w_cat_t, bias_full, bdm, t):
    s = z.shape[0]
    nt = s // t
    nch = t // A_C

    def body(u_ref, v_ref, dsa_ref, g_ref, b_ref, wc_ref, wct_ref, bias_ref, bdm_ref,
             duv_ref, glg_ref, glb_ref, gws_ref, gbs_ref, ds_acc):
        i = pl.program_id(0)

        @pl.when(i == 0)
        def _():
            glg_ref[...] = jnp.zeros((1, A_W), F32)
            glb_ref[...] = jnp.zeros((1, A_W), F32)
            gws_ref[...] = jnp.zeros((A_C, A_G * A_C), F32)
            ds_acc[...] = jnp.zeros((A_C, A_W), F32)

        lng, bdm_ = g_ref[...], bdm_ref[...]
        rec = [_sgu_recompute(v_ref[pl.ds(ci * A_C, A_C), :].astype(F32), lng, b_ref[...]) for ci in range(nch)]
        spread_vn = _sgu_spread([r_[3] for r_ in rec], bdm_)
        mixed = _dot(_sgu_weights(wc_ref, False), spread_vn)
        dsas, dss, dgus = [], [], []
        for ci in range(nch):
            rows = pl.ds(ci * A_C, A_C)
            gu, dgu = _gelu_and_grad(u_ref[rows, :].astype(F32))
            dsa_ = dsa_ref[rows, :].astype(F32)
            ds = dsa_ * gu
            ds_acc[...] += ds
            dsas.append(dsa_)
            dgus.append(dgu)
            dss.append(ds.astype(BF))
        r = lax.broadcasted_iota(jnp.int32, (A_C, A_G * A_C), 0)
        c = lax.broadcasted_iota(jnp.int32, (A_C, A_G * A_C), 1) & (A_C - 1)
        gws_ref[...] += jnp.where(c <= r, _dot_nt(jnp.concatenate(dss, axis=1), spread_vn), 0.0)
        dvns = _dot(_sgu_weights(wct_ref, True), _sgu_spread(dss, bdm_))
        for ci in range(nch):
            rows = pl.ds(ci * A_C, A_C)
            dgv, rstd, xhat, _ = rec[ci]
            dvn = dvns[:, ci * A_W:(ci + 1) * A_W]
            glb_ref[...] += jnp.sum(dvn, axis=0, keepdims=True)
            glg_ref[...] += jnp.sum(dvn * xhat, axis=0, keepdims=True)
            dxh = dvn * lng
            dgv_ = rstd * (dxh - jnp.mean(dxh, axis=-1, keepdims=True)
                           - xhat * jnp.mean(dxh * xhat, axis=-1, keepdims=True))
            s_ = mixed[:, ci * A_W:(ci + 1) * A_W] + bias_ref[...]
            duv_ref[rows, :A_W] = (dsas[ci] * s_ * dgus[ci]).astype(BF)
            duv_ref[rows, A_W:] = (dgv_ * dgv).astype(BF)

        @pl.when(i == nt - 1)
        def _():
            acc = ds_acc[...]
            for g in range(A_G):
                gbs_ref[:, g:g + 1] = jnp.sum(acc[:, g * A_GD:(g + 1) * A_GD], axis=1, keepdims=True)

    return _pc(
        body, name="sgu_bwd", grid=(nt,),
        in_specs=[pl.BlockSpec((t, A_W), lambda i: (i, 0)), pl.BlockSpec((t, A_W), lambda i: (i, 1)),
                  pl.BlockSpec((t, A_W), lambda i: (i, 0)),
                  _const((1, A_W)), _const((1, A_W)), _const((A_C, A_G * A_C)), _const((A_C, A_G * A_C)),
                  _const((A_C, A_W)), _const((A_G * A_C, A_W))],
        out_specs=[pl.BlockSpec((t, D), lambda i: (i, 0)), _acc((1, A_W)), _acc((1, A_W)),
                   _acc((A_C, A_G * A_C)), _acc((A_C, A_G))],
        out_shape=[SDS((s, D), BF), SDS((1, A_W), F32), SDS((1, A_W), F32), SDS((A_C, A_G * A_C), F32),
                   SDS((A_C, A_G), F32)],
        scratch_shapes=[pltpu.VMEM((A_C, A_W), F32)],
        compiler_params=_cp(1))(z, z, dsa, ln_g, ln_b, w_cat, w_cat_t, bias_full, bdm)


def gla_bwd(z, qk32, zl, o, dob, states, wgk, bias, wn, ltri, ltri_t, t, side=None):
    s = z.shape[0]
    nt = s // t
    nc = t // B_C
    hpb = GLA_HPB
    kw, vw = hpb * B_HK, hpb * B_HV

    def body(*refs):
        ins_, s_in, outs_, s_out, scr_, s_scr = _split_side(refs, 12, 6, 5, side)
        qk_ref, v_ref, og_ref, lr_ref, o_ref, dob_ref, st_ref, wgk_ref, bias_ref, wn_ref, l_ref, lt_ref = ins_
        dqk_ref, dv_ref, dog_ref, dpre_ref, gbias_ref, gwn_ref = outs_
        dst_sc, dv_sc, dqd_sc, dkt_sc, ddec_sc = scr_
        i = pl.program_id(0)
        g = pl.program_id(1)

        @pl.when((i == 0) & (g == 0))
        def _():
            gbias_ref[...] = jnp.zeros((B_H, 1, B_HK), F32)
            gwn_ref[...] = jnp.zeros((1, B_HV), F32)
            if side:
                side.start(s_in, s_out, *s_scr)

        @pl.when(i == 0)
        def _():
            for hh in range(hpb):
                dst_sc[g * hpb + hh] = jnp.zeros((B_HV, B_HK), F32)

        lr, l, lt = lr_ref[...], l_ref[...], lt_ref[...]
        keep, keep_t = l > 0, lt > 0
        wn_ = wn_ref[...]
        last = lax.broadcasted_iota(jnp.int32, (nc, B_C, B_HK), 1) == B_C - 1
        for hh in range(hpb):
            h = g * hpb + hh
            cv, ck = slice(hh * B_HV, (hh + 1) * B_HV), slice(hh * B_HK, (hh + 1) * B_HK)
            pre, b, bl, eb, enb, etb, qd, ki, kt = _gla_decays(qk_ref[:, cv], lr, wgk_ref[:, ck], bias_ref[:, ck], l, t)
            qdb, kib, ktb = qd.astype(BF), ki.astype(BF), kt.astype(BF)
            vb = v_ref[:, cv]
            o_ = o_ref[:, cv].astype(F32)
            og = og_ref[:, cv].astype(F32)
            sog = jax.nn.sigmoid(og)
            dob_ = dob_ref[:, cv].astype(F32)
            don = dob_ * og * sog
            do, dwn = _rms_bwd(don, o_, wn_)
            gwn_ref[...] += dwn
            dog_ref[:, cv] = (dob_ * _rms(o_, wn_) * sog * (1.0 + og * (1.0 - sog))).astype(BF)
            dob16 = do.astype(BF)
            sc_t = jnp.where(keep_t, _dot_nt(kib, qdb), 0.0).astype(BF)
            dsc = jnp.where(keep, _dot_nt(dob16, vb), 0.0).astype(BF)
            dsc_t = jnp.where(keep_t, _dot_nt(vb, dob16), 0.0).astype(BF)
            dv_sc[hh] = _dot(sc_t, dob16)
            dqd_sc[hh] = _dot(dsc, kib)
            dki = _dot(dsc_t, qdb)
            for n in reversed(range(nc)):
                rows = slice(n * B_C, (n + 1) * B_C)
                dst = dst_sc[h]
                dstb = dst.astype(BF)
                stp = st_ref[n, hh]
                dv_sc[hh, rows, :] += _dot_nt(ktb[rows], dstb)
                dkt_sc[hh, rows, :] = _dot(vb[rows], dstb)
                dqd_sc[hh, rows, :] += _dot(dob16[rows], stp)
                dec = jnp.exp(bl[n * B_C:n * B_C + 1, :])
                ddec_sc[hh, n] = jnp.sum(dst * stp.astype(F32), axis=0, keepdims=True) * dec
                dst_sc[h] = dst * dec + _dot_tn(dob16[rows], qdb[rows])
            dqd, dkt = dqd_sc[hh], dkt_sc[hh]
            dv_ref[:, cv] = dv_sc[hh].astype(BF)
            dqk_ref[:, hh * B_HV:hh * B_HV + B_HK] = (dqd * eb * (B_HK ** -0.5)).astype(BF)
            dqk_ref[:, hh * B_HV + B_HK:(hh + 1) * B_HV] = (dki * enb + dkt * etb).astype(BF)
            dktkt = dkt * kt
            db3 = (dqd * qd - dki * ki - dktkt).reshape(nc, B_C, B_HK)
            dbl = jnp.sum(dktkt.reshape(nc, B_C, B_HK), axis=1, keepdims=True) + ddec_sc[hh]
            db = (db3 + jnp.where(last, dbl, 0.0)).reshape(t, B_HK)
            dla = _ldot3(lt, db)
            dpre = dla * (1.0 / 16.0) * (1.0 - jax.nn.sigmoid(pre))
            dpre_ref[:, ck] = dpre.astype(BF)
            gbias_ref[h] += jnp.sum(dpre, axis=0, keepdims=True)
        if side:
            @pl.when((i == nt - 1) & (g == B_H // hpb - 1))
            def _():
                side.finish(s_in, s_out, *s_scr)

    rv = lambda i: nt - 1 - i
    si_specs, so_specs, so_shapes, s_scratch, s_ins = _side_specs(side)
    vo, go = 2048 // vw, 3072 // vw
    tile = lambda off: pl.BlockSpec((t, vw), lambda i, g: (rv(i), off + g))
    return _pc(
        body, name="gla_bwd", grid=(nt, B_H // hpb),
        in_specs=[tile(0), tile(vo), tile(go), pl.BlockSpec((t, LANE), lambda i, g: (rv(i), 0)), tile(0), tile(0),
                  pl.BlockSpec((nc, hpb, B_HV, B_HK), lambda i, g: (rv(i), g, 0, 0)),
                  pl.BlockSpec((LANE, kw), lambda i, g: (0, g)), pl.BlockSpec((1, kw), lambda i, g: (0, g)),
                  _const((1, B_HV)), _const((t, t)), _const((t, t))] + si_specs,
        out_specs=[tile(0), tile(0), tile(0), pl.BlockSpec((t, kw), lambda i, g: (rv(i), g)),
                   _acc((B_H, 1, B_HK)), _acc((1, B_HV))] + so_specs,
        out_shape=[SDS((s, D), BF), SDS((s, D), BF), SDS((s, D), BF), SDS((s, B_H * B_HK), BF),
                   SDS((B_H, 1, B_HK), F32), SDS((1, B_HV), F32)] + so_shapes,
        scratch_shapes=[pltpu.VMEM((B_H, B_HV, B_HK), F32), pltpu.VMEM((hpb, t, B_HV), F32),
                        pltpu.VMEM((hpb, t, B_HK), F32), pltpu.VMEM((hpb, t, B_HK), F32),
                        pltpu.VMEM((hpb, nc, 1, B_HK), F32)] + s_scratch,
        compiler_params=_cp(2))(qk32, z, z, zl, o, dob, states, wgk, bias, wn, ltri, ltri_t, *s_ins)


def mm_tn(a, b, name, tk=2048):
    s, m = a.shape
    n = b.shape[1]
    bn = next(c for c in (1024, 1408, 512, 256, 128) if n % c == 0 and m * c * 4 <= 6 * 1024 * 1024)
    tk = min(tk, s)
    nk = s // tk

    def body(a_ref, b_ref, o_ref, acc):
        k = pl.program_id(1)

        @pl.when(k == 0)
        def _():
            acc[...] = jnp.zeros((m, bn), F32)

        acc[...] += _dot_tn(a_ref[...].astype(BF), b_ref[...])

        @pl.when(k == nk - 1)
        def _():
            o_ref[...] = acc[...].astype(BF)

    return _pc(
        body, name=name, grid=(n // bn, nk),
        in_specs=[pl.BlockSpec((tk, m), lambda j, k: (k, 0)), pl.BlockSpec((tk, bn), lambda j, k: (k, j))],
        out_specs=pl.BlockSpec((m, bn), lambda j, k: (0, j)),
        out_shape=SDS((m, n), BF), scratch_shapes=[pltpu.VMEM((m, bn), F32)], compiler_params=_cp(2))(a, b)


def mm_cols(a, w, blk, width, t, name):
    s, k = a.shape

    def body(a_ref, w_ref, o_ref):
        o_ref[...] = _dot(a_ref[...], w_ref[...]).astype(BF)

    return _pc(body, name=name, grid=(s // t,),
               in_specs=[pl.BlockSpec((t, k), lambda i: (i, 0)),
                         pl.BlockSpec((k, width), lambda i: (0, blk), pipeline_mode=pl.Buffered(1))],
               out_specs=pl.BlockSpec((t, width), lambda i: (i, 0)), out_shape=SDS((s, width), BF),
               compiler_params=_cp(1))(a, w)


def mm_nt_small(a, w, t, name):
    s, k = a.shape
    n = w.shape[0]

    def body(a_ref, w_ref, o_ref):
        o_ref[...] = _dot_nt(a_ref[...], w_ref[...]).astype(BF)

    return _pc(body, name=name, grid=(s // t,),
               in_specs=[pl.BlockSpec((t, k), lambda i: (i, 0)), _const((n, k))],
               out_specs=pl.BlockSpec((t, n), lambda i: (i, 0)), out_shape=SDS((s, n), BF),
               compiler_params=_cp(1))(a, w)


def _adamw(w, g, m, v):
    m = ADAM_B1 * m + (1.0 - ADAM_B1) * g
    v = ADAM_B2 * v + (1.0 - ADAM_B2) * (g * g)
    m_hat = m / (1.0 - ADAM_B1 ** ADAM_STEP)
    v_hat = v / (1.0 - ADAM_B2 ** ADAM_STEP)
    return -ADAM_LR * (m_hat / (jnp.sqrt(v_hat) + ADAM_EPS) + ADAM_WD * w), m, v


def _half_rows(rows):
    rh = rows // 2
    return rh, max(b for b in range(16, 257, 16) if rh % b == 0)


def _pc_sp(body, grid, in_specs, out_specs, out_shape, name):
    gs = pltpu.PrefetchScalarGridSpec(num_scalar_prefetch=1, grid=grid, in_specs=in_specs, out_specs=out_specs)
    return _pc(body, grid_spec=gs, out_shape=out_shape, name=name, compiler_params=_cp(len(grid)))


def adamw_halves(sc, own, sib, w, m, v, name):
    rows, cols = w.shape
    rh, br = _half_rows(rows)
    nbk = rh // br

    def body(sc_ref, own_ref, sib_ref, w_ref, m_ref, v_ref, go_ref, d_ref, mo_ref, vo_ref):
        g_ = jnp.where(pl.program_id(0) // nbk == sc_ref[0], own_ref[...], sib_ref[...])
        go_ref[...] = g_
        d_ref[...], mo_ref[...], vo_ref[...] = _adamw(w_ref[...], g_, m_ref[...], v_ref[...])

    half = pl.BlockSpec((br, cols), lambda i, sc_: (i % nbk, 0))
    blk = pl.BlockSpec((br, cols), lambda i, sc_: (i, 0))
    return _pc_sp(body, (2 * nbk,), [half, half, blk, blk, blk], [blk] * 4, [SDS((rows, cols), F32)] * 4,
                  name)(sc, own, sib, w, m, v)


def adamw_cols(sc, own, sib, w, m, v, name, cb=256):
    rows, cols = w.shape
    nk = cols // 2 // cb

    def body(sc_ref, own_ref, sib_ref, w_ref, m_ref, v_ref, go_ref, d_ref, mo_ref, vo_ref):
        g_ = jnp.where(pl.program_id(0) == sc_ref[0], own_ref[...], sib_ref[...])
        go_ref[...] = g_
        d_ref[...], mo_ref[...], vo_ref[...] = _adamw(w_ref[...], g_, m_ref[...], v_ref[...])

    half = pl.BlockSpec((rows, cb), lambda h, k, sc_: (0, k))
    blk = pl.BlockSpec((rows, cb), lambda h, k, sc_: (0, h * nk + k))
    return _pc_sp(body, (2, nk), [half, half, blk, blk, blk], [blk] * 4, [SDS((rows, cols), F32)] * 4,
                  name)(sc, own, sib, w, m, v)


def adamw_small(g, w, m, v):
    def body(g_ref, w_ref, m_ref, v_ref, d_ref, mo_ref, vo_ref):
        d_ref[...], mo_ref[...], vo_ref[...] = _adamw(w_ref[...], g_ref[...], m_ref[...], v_ref[...])

    vm = pl.BlockSpec(memory_space=pltpu.VMEM)
    return _pc(body, name="adamw_small", in_specs=[vm] * 4, out_specs=[vm] * 3, out_shape=[SDS(g.shape, F32)] * 3,
               compiler_params=pltpu.CompilerParams(vmem_limit_bytes=VMEM_LIMIT))(g, w, m, v)


def _pos():
    return lax.axis_index("x"), lax.axis_index("y"), lax.axis_index("c")


def _other_chips(x, y):
    return [(1 - x, y), (x, 1 - y), (1 - x, 1 - y)]


_ANY = pl.BlockSpec(memory_space=pltpu.HBM)


class _Side:
    def __init__(self, ins, out_shapes, nsem, start, finish):
        self.ins, self.out_shapes, self.start, self.finish = list(ins), list(out_shapes), start, finish
        self.scratch = [pltpu.SemaphoreType.DMA((nsem,)), pltpu.SemaphoreType.DMA((nsem,))]
        self.n_in, self.n_out = len(self.ins), len(self.out_shapes)


def _run_side(side, name):
    def body(*refs):
        args_ = (refs[:side.n_in], refs[side.n_in:side.n_in + side.n_out], *refs[side.n_in + side.n_out:])
        side.start(*args_)
        side.finish(*args_)

    return _pc(body, name=name, in_specs=[_ANY] * side.n_in, out_specs=[_ANY] * side.n_out,
               out_shape=side.out_shapes, scratch_shapes=side.scratch)(*side.ins)


def gather_side(bigs, tinies):
    nb, nt_ = len(bigs), len(tinies)

    def plan(ins, outs, ssem, rsem):
        x, y, c = _pos()
        me = 2 * x + y
        chips = _other_chips(x, y)
        sibling = (x, y, 1 - c)

        def copy(k, src, dst, to):
            return pltpu.make_async_remote_copy(src_ref=src, dst_ref=dst, send_sem=ssem.at[k], recv_sem=rsem.at[k],
                                                device_id=to, device_id_type=MESH)

        sends, landed, passed_on, tiny_landed = [], [], [], []
        for w in range(nb):
            rh = bigs[w].shape[0] // 2
            mine = pl.ds(pl.multiple_of(c * rh, 16), rh)
            theirs = pl.ds(pl.multiple_of((1 - c) * rh, 16), rh)
            for j, (cx, cy) in enumerate(chips):
                sends.append(copy(6 * w + j, ins[w].at[mine], outs[w].at[me, mine], (cx, cy, c)))
                blk = outs[w].at[2 * cx + cy, mine]
                landed.append((copy(6 * w + j, blk, blk, (cx, cy, c)), copy(6 * w + 3 + j, blk, blk, sibling)))
                blk = outs[w].at[2 * cx + cy, theirs]
                passed_on.append(copy(6 * w + 3 + j, blk, blk, sibling))
        for w in range(nt_):
            for j, (cx, cy) in enumerate(chips):
                k = 6 * nb + 3 * w + j
                sends.append(copy(k, ins[nb + w], outs[nb + w].at[me], (cx, cy, c)))
                blk = outs[nb + w].at[2 * cx + cy]
                tiny_landed.append(copy(k, blk, blk, (cx, cy, c)))
        return sends, landed, passed_on, tiny_landed

    def start(ins, outs, ssem, rsem):
        for cp in plan(ins, outs, ssem, rsem)[0]:
            cp.start()

    def finish(ins, outs, ssem, rsem):
        sends, landed, passed_on, tiny_landed = plan(ins, outs, ssem, rsem)
        for arrived, forward in landed:
            arrived.wait_recv()
            forward.start()
        for arrived in tiny_landed + passed_on:
            arrived.wait_recv()
        for cp in sends + [forward for _, forward in landed]:
            cp.wait_send()

    return _Side(list(bigs) + list(tinies), [SDS((4,) + a.shape, a.dtype) for a in list(bigs) + list(tinies)],
                 6 * nb + 3 * nt_, start, finish)


def swap_halves(gs, name):
    n = len(gs)

    def body(*refs):
        g_refs, sib_refs = refs[:n], refs[n:2 * n]
        ssem, rsem = refs[2 * n:]
        x, y, c = _pos()
        cps = []
        for w in range(n):
            rh = gs[w].shape[1] // 2
            give = pl.ds(pl.multiple_of((1 - c) * rh, 16), rh)
            cp = pltpu.make_async_remote_copy(src_ref=g_refs[w].at[:, give], dst_ref=sib_refs[w], send_sem=ssem.at[w],
                                              recv_sem=rsem.at[w], device_id=(x, y, 1 - c), device_id_type=MESH)
            cp.start()
            cps.append(cp)
        for cp in cps:
            cp.wait()

    return _pc(body, name=name, in_specs=[_ANY] * n, out_specs=[_ANY] * n,
               out_shape=[SDS((g.shape[0], g.shape[1] // 2, g.shape[2]), g.dtype) for g in gs],
               scratch_shapes=[pltpu.SemaphoreType.DMA((n,)), pltpu.SemaphoreType.DMA((n,))])(*gs)


def add_half(sc, g, sib, name):
    l, r, cols = g.shape
    rh, br = _half_rows(r)
    nbk = rh // br

    def body(sc_ref, g_ref, s_ref, o_ref):
        o_ref[...] = (g_ref[...].astype(F32) + s_ref[...].astype(F32)).astype(BF)

    blk = pl.BlockSpec((1, br, cols), lambda j, i, sc_: (j, i, 0))
    return _pc_sp(body, (l, nbk), [pl.BlockSpec((1, br, cols), lambda j, i, sc_: (j, sc_[0] * nbk + i, 0)), blk], blk,
                  SDS((l, rh, cols), BF), name)(sc, g, sib)


def exchange_side(ps):
    n_ = len(ps)

    def width(p_):
        return p_.shape[2] if p_.shape[0] == 4 else p_.shape[2] // 4

    def plan(p_refs, got_refs, ssem, rsem):
        x, y, c = _pos()
        cps = []
        for w in range(n_):
            wd = width(ps[w])
            for j, (cx, cy) in enumerate(_other_chips(x, y)):
                to = 2 * cx + cy
                src = p_refs[w].at[to] if ps[w].shape[0] == 4 else p_refs[w].at[0, :, pl.ds(pl.multiple_of(to * wd, LANE), wd)]
                cps.append(pltpu.make_async_remote_copy(
                    src_ref=src, dst_ref=got_refs[w].at[j], send_sem=ssem.at[3 * w + j], recv_sem=rsem.at[3 * w + j],
                    device_id=(cx, cy, c), device_id_type=MESH))
        return cps

    def start(*refs):
        for cp in plan(*refs):
            cp.start()

    def finish(*refs):
        for cp in plan(*refs):
            cp.wait()

    return _Side(ps, [SDS((3, p_.shape[1], width(p_)), p_.dtype) for p_ in ps], 3 * n_, start, finish)


def sum4(sc, p, got, name):
    _, rh, wd = got.shape
    _, br = _half_rows(2 * rh)

    def body(sc_ref, p_ref, g_ref, r_ref):
        r_ref[...] = ((p_ref[0].astype(F32) + g_ref[0].astype(F32)) + (g_ref[1].astype(F32) + g_ref[2].astype(F32)))

    own = (pl.BlockSpec((1, br, wd), lambda i, sc_: (sc_[1], i, 0)) if p.shape[0] == 4
           else pl.BlockSpec((1, br, wd), lambda i, sc_: (0, i, sc_[1])))
    return _pc_sp(body, (rh // br,), [own, pl.BlockSpec((3, br, wd), lambda i, sc_: (0, i, 0))],
                  pl.BlockSpec((br, wd), lambda i, sc_: (i, 0)), SDS((rh, wd), F32), name)(sc, p, got)


def join_halves(halves):
    n = len(halves)

    def body(*refs):
        h_refs, got_refs = refs[:n], refs[n:2 * n]
        ssem, rsem = refs[2 * n:]
        x, y, c = _pos()
        cps = []
        for w in range(n):
            cp = pltpu.make_async_remote_copy(src_ref=h_refs[w], dst_ref=got_refs[w], send_sem=ssem.at[w],
                                              recv_sem=rsem.at[w], device_id=(x, y, 1 - c), device_id_type=MESH)
            cp.start()
            cps.append(cp)
        for cp in cps:
            cp.wait()

    return _pc(body, name="join_halves", in_specs=[_ANY] * n, out_specs=[_ANY] * n,
               out_shape=[SDS(h.shape, h.dtype) for h in halves],
               scratch_shapes=[pltpu.SemaphoreType.DMA((n,)), pltpu.SemaphoreType.DMA((n,))])(*halves)


def allreduce_small(g):
    rows = g.shape[0]
    rh = rows // 2

    def body(g_ref, out_ref, sib_buf, chip_buf, sum_sc, ssem, rsem):
        x, y, c = _pos()
        me = 2 * x + y
        sibling = (x, y, 1 - c)
        mine = pl.ds(pl.multiple_of(c * rh, 8), rh)

        def copy(k, src, dst, to):
            return pltpu.make_async_remote_copy(src_ref=src, dst_ref=dst, send_sem=ssem.at[k], recv_sem=rsem.at[k],
                                                device_id=to, device_id_type=MESH)

        cp = copy(0, g_ref, sib_buf, sibling)
        cp.start()
        cp.wait()
        sum_sc[...] = g_ref[...] + sib_buf[...]
        chips = _other_chips(x, y)
        cps = [copy(1 + j, sum_sc.at[mine], chip_buf.at[me], (cx, cy, c)) for j, (cx, cy) in enumerate(chips)]
        for cp in cps:
            cp.start()
        chip_buf[me] = sum_sc[mine, :]
        for j, (cx, cy) in enumerate(chips):
            copy(1 + j, sum_sc.at[mine], chip_buf.at[2 * cx + cy], (cx, cy, c)).wait_recv()
        for cp in cps:
            cp.wait_send()
        out_ref[mine, :] = (chip_buf[0] + chip_buf[1]) + (chip_buf[2] + chip_buf[3])
        cp = copy(4, out_ref.at[mine], out_ref.at[mine], sibling)
        cp.start()
        cp.wait()

    vm = pl.BlockSpec(memory_space=pltpu.VMEM)
    return _pc(body, name="allreduce_small", in_specs=[vm], out_specs=vm, out_shape=SDS((rows, LANE), F32),
               scratch_shapes=[pltpu.VMEM((rows, LANE), F32), pltpu.VMEM((4, rh, LANE), F32), pltpu.VMEM((rows, LANE), F32),
                               pltpu.SemaphoreType.DMA((5,)), pltpu.SemaphoreType.DMA((5,))],
               compiler_params=pltpu.CompilerParams(vmem_limit_bytes=VMEM_LIMIT))(g)


def _pack_small(entries, get):
    flat = jnp.concatenate([get(n).reshape(-1).astype(F32) for n, _ in entries])
    rows = -(-flat.shape[0] // (8 * LANE)) * 8
    return jnp.pad(flat, (0, rows * LANE - flat.shape[0])).reshape(rows, LANE)


def _unpack_small(entries, packed):
    out, off = {}, 0
    flat = packed.reshape(-1)
    for name, n in entries:
        out[name] = flat[off:off + n]
        off += n
    return out


def _cols_full(blk):
    return blk.transpose(1, 0, 2).reshape(blk.shape[1], 4 * blk.shape[2])


def kernel(x, p, pre_mix_norm, w_in, a_ln_g, a_ln_b, a_spatial_w, a_spatial_b, a_out, b_gk, b_gk_bias, b_out_norm, b_out, w_mix_out, post_mix_norm, pre_ffn_norm, w_up, conv_w, conv_b, w_down, post_ffn_norm, w_ple, w_ple_gate, post_ple_norm, loss_target, m_pre_mix_norm, m_w_in, m_a_ln_g, m_a_ln_b, m_a_spatial_w, m_a_spatial_b, m_a_out, m_b_gk, m_b_gk_bias, m_b_out_norm, m_b_out, m_w_mix_out, m_post_mix_norm, m_pre_ffn_norm, m_w_up, m_conv_w, m_conv_b, m_w_down, m_post_ffn_norm, m_w_ple, m_w_ple_gate, m_post_ple_norm, v_pre_mix_norm, v_w_in, v_a_ln_g, v_a_ln_b, v_a_spatial_w, v_a_spatial_b, v_a_out, v_b_gk, v_b_gk_bias, v_b_out_norm, v_b_out, v_w_mix_out, v_post_mix_norm, v_pre_ffn_norm, v_w_up, v_conv_w, v_conv_b, v_w_down, v_post_ffn_norm, v_w_ple, v_w_ple_gate, v_post_ple_norm):
    args = dict(locals())
    order = ['pre_mix_norm', 'w_in', 'a_ln_g', 'a_ln_b', 'a_spatial_w', 'a_spatial_b', 'a_out', 'b_gk', 'b_gk_bias',
             'b_out_norm', 'b_out', 'w_mix_out', 'post_mix_norm', 'pre_ffn_norm', 'w_up', 'conv_w', 'conv_b', 'w_down',
             'post_ffn_norm', 'w_ple', 'w_ple_gate', 'post_ple_norm']
    assert sorted(BIG + TINY + tuple(n for n, _ in SMALL)) == sorted(order)
    s = x.shape[1]
    xs = x.reshape(s, D)
    ps = p.reshape(s, PLE)
    tgt = loss_target.reshape(s, D)
    t_big = min(1024, s)
    t_mid = min(512, s)
    t_small = min(256, s)
    t_gla = min(256, s)
    mx_, my_, mc_ = _pos()
    me = 2 * mx_ + my_
    sc = jnp.stack([mc_, me]).astype(jnp.int32)
    shard = lambda n: args[n].reshape(args[n].shape[1:])

    mine = {n: shard(n).astype(BF) for n in BIG}
    mine.update({n: shard(n) for n in TINY})
    fill = lambda names, gots: {n: lax.dynamic_update_slice(got, mine[n][None], (me, 0, 0)) for n, got in zip(names, gots)}
    first = ("w_in",) + TINY
    full = fill(first, _run_side(gather_side([mine["w_in"]], [mine[n] for n in TINY]), "gather_first"))
    wi = _cols_full(full["w_in"])
    seg = lambda a, b: wi[:, a:b]
    qk = [seg(1024 + h * B_HK, 1024 + (h + 1) * B_HK) for h in range(B_H)]
    kk = [seg(1536 + h * B_HK, 1536 + (h + 1) * B_HK) for h in range(B_H)]
    w_z = jnp.concatenate([seg(0, 1024)] + [m_ for h in range(B_H) for m_ in (qk[h], kk[h])]
                          + [seg(2048, 4096), seg(4112, 6160), seg(4096, 4112), jnp.zeros((D, LANE - B_RANK), BF)], axis=1)
    wgk = jnp.pad(_cols_full(full["b_gk"]).astype(BF), ((0, LANE - B_RANK), (0, 0)))
    w_conv = _cols_full(full["conv_w"])
    g1, g2, g3 = pre_mix_norm.reshape(1, D), post_mix_norm.reshape(1, D), pre_ffn_norm.reshape(1, D)
    g4, g5 = post_ffn_norm.reshape(1, D), post_ple_norm.reshape(1, D)
    ln_g, ln_b = a_ln_g.reshape(1, A_W), a_ln_b.reshape(1, A_W)
    w_s = a_spatial_w.reshape(A_G, A_C, A_C)
    w_cat = w_s.transpose(1, 0, 2).reshape(A_C, A_G * A_C)
    w_cat_t = w_s.transpose(2, 0, 1).reshape(A_C, A_G * A_C)
    bias_full = jnp.repeat(a_spatial_b.reshape(A_G, A_C).T, A_GD, axis=1)
    bdm = (jnp.arange(A_G * A_C)[:, None] // A_C == jnp.arange(A_W)[None, :] // A_GD).astype(BF)
    gk_bias = b_gk_bias.reshape(1, B_H * B_HK)
    wn = b_out_norm.reshape(1, B_HV)
    cb = conv_b.reshape(1, 2 * D_FF)
    idx = jnp.arange(t_gla)
    ltri = ((idx[:, None] // B_C == idx[None, :] // B_C) & (idx[None, :] <= idx[:, None])).astype(BF)

    a, z, qk32, *gots = norm_matmul(xs, g1, w_z, D, t_big, "in_proj", nblk=6, f32_blk=1,
                                    side=gather_side([mine[n] for n in BIG[1:]], []))
    full.update(fill(BIG[1:], gots))
    w_aout, w_ple_f = _cols_full(full["a_out"]), _cols_full(full["w_ple"])
    w_bout, w_mix, w_pg = (full[n].reshape(D, D) for n in ("b_out", "w_mix_out", "w_ple_gate"))
    w_dn, w_up3 = full["w_down"].reshape(D_FF, D), full["w_up"]
    zl = mm_cols(a, w_z, 48, LANE, t_big, "lr_proj")
    sa = sgu_fwd(z, ln_g, ln_b, w_cat, bias_full, bdm, t_mid)
    ob, o, states = gla_fwd(z, qk32, zl, wgk, gk_bias, wn, ltri, t_gla)
    ya, yb, mp, mx, h1 = mix_fwd(sa, ob, z, xs, w_aout, w_bout, w_mix, g2, t_small)
    c, up = norm_matmul(h1, g3, w_up3, 1408, t_big, "up_proj")
    ff, cg, cv = ffn_gate_fwd(up, w_conv, cb, t_small)
    f, h2, pg, pe, dy, loss = out_fwd(ff, h1, ps, tgt, w_dn, w_pg, w_ple_f, g4, g5, t_small)

    dh2, dpe, dpg, df, dff, gg5, gg4 = out_bwd(dy, pg, pe, f, g5, g4, w_pg, w_dn, t_small)
    dup_g, dup_v, gcw_g, gcw_v, gcb_g, gcb_v = ffn_gate_bwd(up, cg, cv, dff, w_conv, t_small)
    dh1, gg3 = nt_normbwd([dup_g, dup_v], _cols_full(w_up3), h1, g3, dh2, t_small, "up_bwd")
    dmx, dya, dyb, dga, dgb, dsa, dob, gg2 = mix_bwd(dh1, mx, z, ya, yb, g2, w_mix, w_aout, w_bout, t_small)
    duv, g_lng, g_lnb, g_wcat, g_bst = sgu_bwd(z, dsa, ln_g, ln_b, w_cat, w_cat_t, bias_full, bdm, t_mid)
    g_ws = g_wcat.reshape(A_C, A_G, A_C).transpose(1, 0, 2)

    grads = {
        "a_out": mm_tn(sa, dya, "dw_a_out")[None],
        "b_out": mm_tn(ob, dyb, "dw_b_out").reshape(4, D // 4, D),
        "w_mix_out": mm_tn(mp, dmx, "dw_mix").reshape(4, D // 4, D),
        "w_up": jnp.concatenate([mm_tn(c, dup_g, "dw_up_g"), mm_tn(c, dup_v, "dw_up_v")], axis=1)[None],
        "w_down": mm_tn(ff, df, "dw_down").reshape(4, D_FF // 4, D),
        "w_ple": mm_tn(ps, dpe, "dw_ple")[None],
        "w_ple_gate": mm_tn(h2, dpg, "dw_ple_gate").reshape(4, D // 4, D),
    }

    def chip_partials(names):
        gs = [grads[n] for n in names]
        return [add_half(sc, g, sib, "partial_" + n)
                for n, g, sib in zip(names, gs, swap_halves(gs, "swap_halves_" + names[0]))]

    parts = dict(zip(BIG[1:], chip_partials(BIG[1:])))
    dqk, dvb, dog, dpre, g_gkb, g_wn, *gots = gla_bwd(z, qk32, zl, o, dob, states, wgk, gk_bias, wn, ltri, ltri.T,
                                                      t_gla, side=exchange_side([parts[n] for n in BIG[1:]]))
    got = dict(zip(BIG[1:], gots))
    dlr = mm_nt_small(dpre, wgk, t_mid, "dlr")
    segs = [duv, dqk, dvb, dog, dga, dgb, dlr]

    gz = [mm_tn(a, sg_, "dw_in_%d" % k) for k, sg_ in enumerate(segs)]
    gq = [gz[1][:, h * 256:h * 256 + B_HK] for h in range(B_H)]
    gk = [gz[1][:, h * 256 + B_HK:(h + 1) * 256] for h in range(B_H)]
    g_in = jnp.concatenate([gz[0]] + gq + gk + [gz[2], gz[3], gz[6][:, :B_RANK], gz[4], gz[5]], axis=1)
    grads["w_in"] = g_in.reshape(D, 4, 1540).transpose(1, 0, 2)
    parts["w_in"], = chip_partials(("w_in",))
    dx, gg1, got["w_in"] = nt_normbwd(segs, w_z, xs, g1, dh1, t_small, "in_bwd",
                                      side=exchange_side([parts["w_in"]]))

    reds = [sum4(sc, parts[n], got[n], "sum_" + n) for n in BIG]
    outs = {}
    for n, red, sib in zip(BIG, reds, join_halves(reds)):
        if n == "w_in":
            res = adamw_cols(sc, red.T, sib.T, shard(n).T, shard("m_" + n).T, shard("v_" + n).T, "adamw_" + n)
            res = [r_.T for r_ in res]
        else:
            res = adamw_halves(sc, red, sib, shard(n), shard("m_" + n), shard("v_" + n), "adamw_" + n)
        outs[n] = [r_.reshape(args[n].shape) for r_ in res]

    small_g = {
        "pre_mix_norm": gg1, "a_ln_g": g_lng, "a_ln_b": g_lnb, "a_spatial_w": g_ws, "a_spatial_b": g_bst.T,
        "b_gk_bias": g_gkb, "b_out_norm": g_wn, "post_mix_norm": gg2, "pre_ffn_norm": gg3,
        "conv_b": jnp.concatenate([gcb_g, gcb_v], axis=1), "post_ffn_norm": gg4, "post_ple_norm": gg5,
        "b_gk": mm_tn(zl, dpre, "dw_gk")[:B_RANK], "conv_w": jnp.concatenate([gcw_g, gcw_v], axis=1),
        "loss": loss,
    }
    red_entries = SMALL + (("b_gk", B_RANK * 512), ("conv_w", 3 * 2 * D_FF), ("loss", 1))
    g_fin = _unpack_small(red_entries, allreduce_small(_pack_small(red_entries, lambda n: small_g[n])))
    g_fin["b_gk"] = lax.dynamic_slice(g_fin["b_gk"].reshape(B_RANK, 512), (0, me * B_HK), (B_RANK, B_HK))
    g_fin["conv_w"] = lax.dynamic_slice(g_fin["conv_w"].reshape(3, 2 * D_FF), (0, me * 1408), (3, 1408))
    upd_entries = SMALL + (("b_gk", B_RANK * B_HK), ("conv_w", 3 * 1408))
    res = adamw_small(*[_pack_small(upd_entries, get) for get in
                        (lambda n: g_fin[n], lambda n: args[n], lambda n: args["m_" + n], lambda n: args["v_" + n])])
    res = [_unpack_small(upd_entries, r_) for r_ in res]
    for n, _ in upd_entries:
        outs[n] = [r_[n].reshape(args[n].shape) for r_ in [g_fin] + res]

    return (g_fin["loss"].reshape(()), dx.reshape(x.shape), *[outs[n][0] for n in order], *[outs[n][1] for n in order],
            *[outs[n][2] for n in order], *[outs[n][3] for n in order])
```

```python
import functools
import math

import jax
import jax.numpy as jnp
from jax import lax
from jax.experimental import pallas as pl
from jax.experimental.pallas import tpu as pltpu

F32 = jnp.float32
BF = jnp.bfloat16
SDS = jax.ShapeDtypeStruct
MESH = pl.DeviceIdType.MESH

EPS = 1e-6
D = 1024
A_W = 512
A_G, A_C = 8, 128
A_GD = A_W // A_G
B_H, B_HK, B_HV = 4, 128, 256
B_C = 64
GLA_HPB = 4
B_RANK = 16
D_FF = 2816
PLE = 256
ZW = 6272
LANE = 128
VMEM_LIMIT = 60 * 1024 * 1024

ADAM_LR, ADAM_B1, ADAM_B2, ADAM_EPS, ADAM_WD, ADAM_STEP = 0.001, 0.9, 0.999, 1e-08, 0.01, 10

_GC = math.sqrt(2.0 / math.pi)
_GA = 0.044715

BIG = ("w_in", "a_out", "b_out", "w_mix_out", "w_up", "w_down", "w_ple", "w_ple_gate")
TINY = ("b_gk", "conv_w")
SMALL = (("pre_mix_norm", 1024), ("a_ln_g", 512), ("a_ln_b", 512), ("a_spatial_w", 131072),
         ("a_spatial_b", 1024), ("b_gk_bias", 512), ("b_out_norm", 256), ("post_mix_norm", 1024),
         ("pre_ffn_norm", 1024), ("conv_b", 5632), ("post_ffn_norm", 1024), ("post_ple_norm", 1024))


def _pc(body, **kw):
    return pl.pallas_call(body, **kw)


def _cp(n):
    return pltpu.CompilerParams(dimension_semantics=("arbitrary",) * n, vmem_limit_bytes=VMEM_LIMIT)


def _const(shape):
    nd = len(shape)
    return pl.BlockSpec(shape, lambda *_: (0,) * nd, pipeline_mode=pl.Buffered(1))


def _acc(shape):
    nd = len(shape)
    return pl.BlockSpec(shape, lambda *_: (0,) * nd)


def _dot(a, b):
    return jnp.dot(a, b, preferred_element_type=F32)


def _dot_nt(a, b):
    return lax.dot_general(a, b, (((1,), (1,)), ((), ())), preferred_element_type=F32)


def _dot_tn(a, b):
    return lax.dot_general(a, b, (((0,), (0,)), ((), ())), preferred_element_type=F32)


def _gelu(x):
    return 0.5 * x * (1.0 + jnp.tanh(_GC * (x + _GA * x * x * x)))


def _gelu_and_grad(x):
    x2 = x * x
    s = 0.5 * jnp.tanh((_GC * x) * (1.0 + _GA * x2)) + 0.5
    g = x * s
    return g, s + g * (1.0 - s) * ((6.0 * _GC * _GA) * x2 + 2.0 * _GC)


def _log_sigmoid(x):
    return jnp.minimum(x, 0.0) - jnp.log(1.0 + jnp.exp(-jnp.abs(x)))


def _rms(x, g):
    return x * lax.rsqrt(jnp.mean(x * x, axis=-1, keepdims=True) + EPS) * g


def _rms_bwd(dy, x, g):
    r = lax.rsqrt(jnp.mean(x * x, axis=-1, keepdims=True) + EPS)
    n = x * r
    dn = dy * g
    dx = r * (dn - n * jnp.mean(dn * n, axis=-1, keepdims=True))
    return dx, jnp.sum(dy * n, axis=0, keepdims=True)


def _ldot3(l, x):
    h = x.astype(BF)
    r = x - h.astype(F32)
    m = r.astype(BF)
    lo = (r - m.astype(F32)).astype(BF)
    return _dot(l, h) + _dot(l, m) + _dot(l, lo)


def _split_side(refs, n_in, n_out, n_scratch, side):
    si, so = (side.n_in, side.n_out) if side else (0, 0)
    cuts = [n_in, si, n_out, so, n_scratch]
    out, at = [], 0
    for c in cuts:
        out.append(refs[at:at + c])
        at += c
    return (*out, refs[at:])


def _side_specs(side):
    return ([_ANY] * side.n_in, [_ANY] * side.n_out, side.out_shapes, side.scratch, side.ins) if side else ([],) * 5


def norm_matmul(x, g, w, bn, t, name, nblk=None, f32_blk=None, side=None):
    s, dm = x.shape
    if w.ndim == 3:
        nblk = w.shape[0]
        w_spec = pl.BlockSpec((None, dm, bn), lambda i, j: (j, 0, 0))
    else:
        nblk = nblk or w.shape[1] // bn
        w_spec = pl.BlockSpec((dm, bn), lambda i, j: (0, j))
    extra = f32_blk is not None
    nt = s // t

    def body(*refs):
        (x_ref, g_ref, w_ref), s_in, outs, s_out, (a_sc,), s_scr = _split_side(refs, 3, 2 + extra, 1, side)
        a_ref, z_ref = outs[:2]
        i, j = pl.program_id(0), pl.program_id(1)
        if side:
            @pl.when((i == 0) & (j == 0))
            def _():
                side.start(s_in, s_out, *s_scr)

        @pl.when(j == 0)
        def _():
            a = _rms(x_ref[...], g_ref[...]).astype(BF)
            a_sc[...] = a
            a_ref[...] = a

        acc = _dot(a_sc[...], w_ref[...])
        z_ref[...] = acc.astype(BF)
        if extra:
            @pl.when(j == f32_blk)
            def _():
                outs[2][...] = acc
        if side:
            @pl.when((i == nt - 1) & (j == nblk - 1))
            def _():
                side.finish(s_in, s_out, *s_scr)

    si_specs, so_specs, so_shapes, s_scratch, s_ins = _side_specs(side)
    return _pc(
        body, name=name, grid=(nt, nblk),
        in_specs=[pl.BlockSpec((t, dm), lambda i, j: (i, 0)), _const((1, dm)), w_spec] + si_specs,
        out_specs=[pl.BlockSpec((t, dm), lambda i, j: (i, 0)), pl.BlockSpec((t, bn), lambda i, j: (i, j))]
        + [pl.BlockSpec((t, bn), lambda i, j: (i, 0))] * extra + so_specs,
        out_shape=[SDS((s, dm), BF), SDS((s, nblk * bn), BF)] + [SDS((s, bn), F32)] * extra + so_shapes,
        scratch_shapes=[pltpu.VMEM((t, dm), BF)] + s_scratch, compiler_params=_cp(2))(x, g, w, *s_ins)


def _sgu_weights(wc_ref, transposed):
    r = lax.broadcasted_iota(jnp.int32, (A_C, A_G * A_C), 0)
    c = lax.broadcasted_iota(jnp.int32, (A_C, A_G * A_C), 1) & (A_C - 1)
    return jnp.where((r <= c) if transposed else (c <= r), wc_ref[...], 0.0).astype(BF)


def _sgu_spread(xs, bdm):
    return jnp.concatenate([jnp.tile(x, (A_G, 1)) * bdm for x in xs], axis=1)


def _sgu_recompute(v, lng, lnb):
    gv, dgv = _gelu_and_grad(v)
    mu = jnp.mean(gv, axis=-1, keepdims=True)
    xc = gv - mu
    rstd = lax.rsqrt(jnp.mean(xc * xc, axis=-1, keepdims=True) + EPS)
    xhat = xc * rstd
    return dgv, rstd, xhat, (xhat * lng + lnb).astype(BF)


def sgu_fwd(z, ln_g, ln_b, w_cat, bias_full, bdm, t):
    s = z.shape[0]
    nch = t // A_C

    def body(u_ref, v_ref, g_ref, b_ref, wc_ref, bias_ref, bdm_ref, sa_ref):
        vns = [_sgu_recompute(v_ref[pl.ds(ci * A_C, A_C), :].astype(F32), g_ref[...], b_ref[...])[3]
               for ci in range(nch)]
        mixed = _dot(_sgu_weights(wc_ref, False), _sgu_spread(vns, bdm_ref[...]))
        for ci in range(nch):
            rows = pl.ds(ci * A_C, A_C)
            s_ = mixed[:, ci * A_W:(ci + 1) * A_W] + bias_ref[...]
            sa_ref[rows, :] = (_gelu(u_ref[rows, :].astype(F32)) * s_).astype(BF)

    return _pc(
        body, name="sgu_fwd", grid=(s // t,),
        in_specs=[pl.BlockSpec((t, A_W), lambda i: (i, 0)), pl.BlockSpec((t, A_W), lambda i: (i, 1)),
                  _const((1, A_W)), _const((1, A_W)), _const((A_C, A_G * A_C)), _const((A_C, A_W)),
                  _const((A_G * A_C, A_W))],
        out_specs=pl.BlockSpec((t, A_W), lambda i: (i, 0)),
        out_shape=SDS((s, A_W), BF), compiler_params=_cp(1))(z, z, ln_g, ln_b, w_cat, bias_full, bdm)


def _gla_decays(qk, lr, wgk, bias, l, t):
    nc = t // B_C
    q = qk[:, :B_HK].astype(F32) * (B_HK ** -0.5)
    k = qk[:, B_HK:].astype(F32)
    pre = _dot(lr, wgk) + bias
    la = _log_sigmoid(pre) * (1.0 / 16.0)
    b = _ldot3(l, la)
    b3 = b.reshape(nc, B_C, B_HK)
    bl = jnp.broadcast_to(b3[:, B_C - 1:B_C, :], (nc, B_C, B_HK)).reshape(t, B_HK)
    eb, enb, etb = jnp.exp(b), jnp.exp(-b), jnp.exp(bl - b)
    return pre, b, bl, eb, enb, etb, q * eb, k * enb, k * etb


def gla_fwd(z, qk32, zl, wgk, bias, wn, ltri, t):
    s = z.shape[0]
    nc = t // B_C
    hpb = GLA_HPB
    kw, vw = hpb * B_HK, hpb * B_HV

    def body(qk_ref, v_ref, og_ref, lr_ref, wgk_ref, bias_ref, wn_ref, l_ref, ob_ref, o_ref, st_ref, st_sc, o_sc):
        g = pl.program_id(1)

        @pl.when(pl.program_id(0) == 0)
        def _():
            for hh in range(hpb):
                st_sc[g * hpb + hh] = jnp.zeros((B_HV, B_HK), F32)

        lr, l = lr_ref[...], l_ref[...]
        for hh in range(hpb):
            h = g * hpb + hh
            cv, ck = slice(hh * B_HV, (hh + 1) * B_HV), slice(hh * B_HK, (hh + 1) * B_HK)
            _, _, bl, _, _, _, qd, ki, kt = _gla_decays(qk_ref[:, cv], lr, wgk_ref[:, ck], bias_ref[:, ck], l, t)
            qd, ki, kt = qd.astype(BF), ki.astype(BF), kt.astype(BF)
            vb = v_ref[:, cv]
            sc = jnp.where(l > 0, _dot_nt(qd, ki), 0.0).astype(BF)
            o_sc[hh] = _dot(sc, vb)
            for n in range(nc):
                rows = slice(n * B_C, (n + 1) * B_C)
                st = st_sc[h]
                stb = st.astype(BF)
                st_ref[n, hh] = stb
                o_sc[hh, rows, :] += _dot_nt(qd[rows], stb)
                st_sc[h] = st * jnp.exp(bl[n * B_C:n * B_C + 1, :]) + _dot_tn(vb[rows], kt[rows])
            ob = o_sc[hh].astype(BF)
            o_ref[:, cv] = ob
            og = og_ref[:, cv].astype(F32)
            ob_ref[:, cv] = (_rms(ob.astype(F32), wn_ref[...]) * og * jax.nn.sigmoid(og)).astype(BF)

    vo, go = 2048 // vw, 3072 // vw
    return _pc(
        body, name="gla_fwd", grid=(s // t, B_H // hpb),
        in_specs=[pl.BlockSpec((t, vw), lambda i, g: (i, g)), pl.BlockSpec((t, vw), lambda i, g: (i, vo + g)),
                  pl.BlockSpec((t, vw), lambda i, g: (i, go + g)), pl.BlockSpec((t, LANE), lambda i, g: (i, 0)),
                  pl.BlockSpec((LANE, kw), lambda i, g: (0, g)), pl.BlockSpec((1, kw), lambda i, g: (0, g)),
                  _const((1, B_HV)), _const((t, t))],
        out_specs=[pl.BlockSpec((t, vw), lambda i, g: (i, g)), pl.BlockSpec((t, vw), lambda i, g: (i, g)),
                   pl.BlockSpec((nc, hpb, B_HV, B_HK), lambda i, g: (i, g, 0, 0))],
        out_shape=[SDS((s, D), BF), SDS((s, D), BF), SDS((s // B_C, B_H, B_HV, B_HK), BF)],
        scratch_shapes=[pltpu.VMEM((B_H, B_HV, B_HK), F32), pltpu.VMEM((hpb, t, B_HV), F32)],
        compiler_params=_cp(2))(qk32, z, z, zl, wgk, bias, wn, ltri)


def mix_fwd(sa, ob, z, x, a_out, b_out, w_mix, g2, t):
    s = x.shape[0]

    def body(sa_ref, ob_ref, ga_ref, gb_ref, x_ref, ao_ref, bo_ref, wm_ref, g2_ref,
             ya_ref, yb_ref, mp_ref, mx_ref, h1_ref):
        ya = _dot(sa_ref[...], ao_ref[...]).astype(BF)
        yb = _dot(ob_ref[...], bo_ref[...]).astype(BF)
        ya_ref[...] = ya
        yb_ref[...] = yb
        mp = (jax.nn.sigmoid(ga_ref[...].astype(F32)) * ya.astype(F32)
              + jax.nn.sigmoid(gb_ref[...].astype(F32)) * yb.astype(F32)).astype(BF)
        mp_ref[...] = mp
        mx = _dot(mp, wm_ref[...]).astype(BF)
        mx_ref[...] = mx
        h1_ref[...] = x_ref[...] + _rms(mx.astype(F32), g2_ref[...])

    row = lambda w: pl.BlockSpec((t, w), lambda i: (i, 0))
    return _pc(
        body, name="mix_fwd", grid=(s // t,),
        in_specs=[row(A_W), row(D), pl.BlockSpec((t, D), lambda i: (i, 4)), pl.BlockSpec((t, D), lambda i: (i, 5)),
                  row(D), _const((A_W, D)), _const((D, D)), _const((D, D)), _const((1, D))],
        out_specs=[row(D)] * 5,
        out_shape=[SDS((s, D), BF)] * 4 + [SDS((s, D), F32)],
        compiler_params=_cp(1))(sa, ob, z, z, x, a_out, b_out, w_mix, g2)


def ffn_gate_fwd(up, conv_w, conv_b, t):
    s = up.shape[0]
    bn = 1408
    hb = t // 8

    def body(ug_ref, uv_ref, hg_ref, hv_ref, wg_ref, wv_ref, bg_ref, bv_ref, ff_ref, cg_ref, cv_ref):
        live = (pl.program_id(1) > 0).astype(F32)

        def branch(u_ref, h_ref, w_ref, b_ref, c_ref):
            ext = jnp.concatenate([h_ref[...].astype(F32) * live, u_ref[...].astype(F32)], axis=0)
            w = w_ref[...]
            c = (b_ref[...] + w[0:1] * pltpu.roll(ext, 2, 0) + w[1:2] * pltpu.roll(ext, 1, 0) + w[2:3] * ext)[8:]
            c = c.astype(BF)
            c_ref[...] = c
            return c.astype(F32)

        g = _gelu(branch(ug_ref, hg_ref, wg_ref, bg_ref, cg_ref))
        ff_ref[...] = (g * branch(uv_ref, hv_ref, wv_ref, bv_ref, cv_ref)).astype(BF)

    halo = lambda off: pl.BlockSpec((8, bn), lambda j, i: (jnp.maximum(i * hb - 1, 0), j + off))
    out = pl.BlockSpec((t, bn), lambda j, i: (i, j))
    return _pc(
        body, name="ffn_gate_fwd", grid=(2, s // t),
        in_specs=[pl.BlockSpec((t, bn), lambda j, i: (i, j)), pl.BlockSpec((t, bn), lambda j, i: (i, j + 2)),
                  halo(0), halo(2),
                  pl.BlockSpec((3, bn), lambda j, i: (0, j)), pl.BlockSpec((3, bn), lambda j, i: (0, j + 2)),
                  pl.BlockSpec((1, bn), lambda j, i: (0, j)), pl.BlockSpec((1, bn), lambda j, i: (0, j + 2))],
        out_specs=[out] * 3, out_shape=[SDS((s, D_FF), BF)] * 3,
        compiler_params=_cp(2))(up, up, up, up, conv_w, conv_w, conv_b, conv_b)


def ffn_up_fwd(h1, g3, w_up3, conv_w, conv_b, t):
    s = h1.shape[0]
    bn = w_up3.shape[2]

    def body(x_ref, g_ref, wg_ref, wv_ref, cwg_ref, cwv_ref, cbg_ref, cbv_ref,
             c_ref, ug_ref, uv_ref, cg_ref, cv_ref, ff_ref, c_sc, carry):
        i, j = pl.program_id(0), pl.program_id(1)

        @pl.when(j == 0)
        def _():
            c = _rms(x_ref[...], g_ref[...]).astype(BF)
            c_sc[...] = c
            c_ref[...] = c

        @pl.when(i == 0)
        def _():
            carry[j] = jnp.zeros((2, 8, bn), F32)

        def branch(k, w_ref, cw_ref, cb_ref, u_ref, o_ref):
            ub = _dot(c_sc[...], w_ref[...]).astype(BF)
            u_ref[...] = ub
            u = ub.astype(F32)
            ext = jnp.concatenate([carry[j, k], u], axis=0)
            carry[j, k] = u[t - 8:]
            w = cw_ref[...]
            cc = (cb_ref[...] + w[0:1] * pltpu.roll(ext, 2, 0) + w[1:2] * pltpu.roll(ext, 1, 0) + w[2:3] * ext)[8:]
            cc = cc.astype(BF)
            o_ref[...] = cc
            return cc.astype(F32)

        g = _gelu(branch(0, wg_ref, cwg_ref, cbg_ref, ug_ref, cg_ref))
        ff_ref[...] = (g * branch(1, wv_ref, cwv_ref, cbv_ref, uv_ref, cv_ref)).astype(BF)

    col = lambda rows, off: pl.BlockSpec((rows, bn), lambda i, j: (0, j + off))
    out = pl.BlockSpec((t, bn), lambda i, j: (i, j))
    return _pc(
        body, name="ffn_up_fwd", grid=(s // t, 2),
        in_specs=[pl.BlockSpec((t, D), lambda i, j: (i, 0)), _const((1, D)),
                  pl.BlockSpec((None, D, bn), lambda i, j: (j, 0, 0)), pl.BlockSpec((None, D, bn), lambda i, j: (j + 2, 0, 0)),
                  col(3, 0), col(3, 2), col(1, 0), col(1, 2)],
        out_specs=[pl.BlockSpec((t, D), lambda i, j: (i, 0))] + [out] * 5,
        out_shape=[SDS((s, D), BF)] + [SDS((s, D_FF), BF)] * 5,
        scratch_shapes=[pltpu.VMEM((t, D), BF), pltpu.VMEM((2, 2, 8, bn), F32)],
        compiler_params=_cp(2))(h1, g3, w_up3, w_up3, conv_w, conv_w, conv_b, conv_b)


def ffn_up_bwd(ug, uv, cg, cv, dff, conv_w, w_up, h1, g3, dh2, t):
    s = h1.shape[0]
    nt = s // t
    hb = t // 8
    bn = 1408
    r = t + 8

    def body(ug_ref, uv_ref, cg_ref, cv_ref, cag_ref, cav_ref, d_ref, da_ref, cw_ref, w_ref, x_ref, g_ref, dres_ref,
             dug_ref, duv_ref, gw_ref, gb_ref, dx_ref, gg_ref):
        i = pl.program_id(0)

        @pl.when(i == 0)
        def _():
            gw_ref[...] = jnp.zeros((3, 2 * D_FF), F32)
            gb_ref[...] = jnp.zeros((1, 2 * D_FF), F32)
            gg_ref[...] = jnp.zeros((1, D), F32)

        more = (i < nt - 1).astype(F32)

        def gate(c_g, c_v, d_):
            gl, dgl = _gelu_and_grad(c_g.astype(F32))
            return d_ * c_v.astype(F32) * dgl, d_ * gl

        def back(dc, dc_next, u_ref, cols, off, du_ref):
            w = cw_ref[:, off:off + bn]
            d_ext = jnp.concatenate([dc, dc_next], axis=0)
            d1, d2 = pltpu.roll(d_ext, r - 1, 0)[:t], pltpu.roll(d_ext, r - 2, 0)[:t]
            du_ref[:, cols] = (w[2:3] * dc + w[1:2] * d1 + w[0:1] * d2).astype(BF)
            u = u_ref[:, cols].astype(F32)
            gw_ref[0:1, off:off + bn] += jnp.sum(d2 * u, axis=0, keepdims=True)
            gw_ref[1:2, off:off + bn] += jnp.sum(d1 * u, axis=0, keepdims=True)
            gw_ref[2:3, off:off + bn] += jnp.sum(dc * u, axis=0, keepdims=True)
            gb_ref[:, off:off + bn] += jnp.sum(dc, axis=0, keepdims=True)

        for kb in range(D_FF // bn):
            cols = slice(kb * bn, (kb + 1) * bn)
            dg, dv = gate(cg_ref[:, cols], cv_ref[:, cols], d_ref[:, cols].astype(F32))
            dg_n, dv_n = gate(cag_ref[:, cols], cav_ref[:, cols], da_ref[:, cols].astype(F32) * more)
            back(dg, dg_n, ug_ref, cols, kb * bn, dug_ref)
            back(dv, dv_n, uv_ref, cols, D_FF + kb * bn, duv_ref)

        acc = _dot_nt(dug_ref[...], w_ref[:, :D_FF]) + _dot_nt(duv_ref[...], w_ref[:, D_FF:])
        dxn, dg3 = _rms_bwd(acc, x_ref[...], g_ref[...])
        dx_ref[...] = dres_ref[...] + dxn
        gg_ref[...] += dg3

    tile = lambda width: pl.BlockSpec((t, width), lambda i: (i, 0))
    after = pl.BlockSpec((8, D_FF), lambda i: (jnp.minimum((i + 1) * hb, nt * hb - 1), 0))
    return _pc(
        body, name="ffn_up_bwd", grid=(nt,),
        in_specs=[tile(D_FF)] * 4 + [after, after, tile(D_FF), after, _const((3, 2 * D_FF)), _const((D, 2 * D_FF)),
                                     tile(D), _const((1, D)), tile(D)],
        out_specs=[tile(D_FF), tile(D_FF), _acc((3, 2 * D_FF)), _acc((1, 2 * D_FF)), tile(D), _acc((1, D))],
        out_shape=[SDS((s, D_FF), BF), SDS((s, D_FF), BF), SDS((3, 2 * D_FF), F32), SDS((1, 2 * D_FF), F32),
                   SDS((s, D), F32), SDS((1, D), F32)],
        compiler_params=_cp(1))(ug, uv, cg, cv, cg, cv, dff, dff, conv_w, w_up, h1, g3, dh2)


def out_fwd(ff, h1, p, tgt, w_down, w_pg, w_ple, g4, g5, t):
    s = h1.shape[0]

    def body(ff_ref, h1_ref, p_ref, t_ref, wd_ref, wpg_ref, wpl_ref, g4_ref, g5_ref,
             f_ref, h2_ref, pg_ref, pe_ref, dy_ref, loss_ref):
        @pl.when(pl.program_id(0) == 0)
        def _():
            loss_ref[...] = jnp.zeros((1, 1), F32)

        f = _dot(ff_ref[...], wd_ref[...]).astype(BF)
        f_ref[...] = f
        h2 = h1_ref[...] + _rms(f.astype(F32), g4_ref[...])
        h2b = h2.astype(BF)
        h2_ref[...] = h2b
        pg = _dot(h2b, wpg_ref[...]).astype(BF)
        pe = _dot(p_ref[...].astype(BF), wpl_ref[...]).astype(BF)
        pg_ref[...] = pg
        pe_ref[...] = pe
        y = h2 + _rms(jax.nn.sigmoid(pg.astype(F32)) * pe.astype(F32), g5_ref[...])
        err = y - t_ref[...]
        dy_ref[...] = err * (1.0 / D)
        loss_ref[...] += (0.5 / D) * jnp.sum(err * err)

    row = lambda w: pl.BlockSpec((t, w), lambda i: (i, 0))
    return _pc(
        body, name="out_fwd", grid=(s // t,),
        in_specs=[row(D_FF), row(D), row(PLE), row(D), _const((D_FF, D)), _const((D, D)), _const((PLE, D)),
                  _const((1, D)), _const((1, D))],
        out_specs=[row(D)] * 5 + [_acc((1, 1))],
        out_shape=[SDS((s, D), BF)] * 4 + [SDS((s, D), F32), SDS((1, 1), F32)],
        compiler_params=_cp(1))(ff, h1, p, tgt, w_down, w_pg, w_ple, g4, g5)


def out_bwd(dy, pg, pe, f, g5, g4, w_pg, w_down, t):
    s = dy.shape[0]

    def body(dy_ref, pg_ref, pe_ref, f_ref, g5_ref, g4_ref, wpg_ref, wd_ref,
             dh2_ref, dpe_ref, dpg_ref, df_ref, dff_ref, gg5_ref, gg4_ref):
        @pl.when(pl.program_id(0) == 0)
        def _():
            gg5_ref[...] = jnp.zeros((1, D), F32)
            gg4_ref[...] = jnp.zeros((1, D), F32)

        dy_ = dy_ref[...]
        pg_ = pg_ref[...].astype(F32)
        pe_ = pe_ref[...].astype(F32)
        sg = jax.nn.sigmoid(pg_)
        dple, dg5 = _rms_bwd(dy_, sg * pe_, g5_ref[...])
        gg5_ref[...] += dg5
        dpe_ref[...] = (dple * sg).astype(BF)
        dpg = (dple * pe_ * sg * (1.0 - sg)).astype(BF)
        dpg_ref[...] = dpg
        dh2 = dy_ + _dot_nt(dpg, wpg_ref[...])
        dh2_ref[...] = dh2
        df, dg4 = _rms_bwd(dh2, f_ref[...].astype(F32), g4_ref[...])
        gg4_ref[...] += dg4
        dfb = df.astype(BF)
        df_ref[...] = dfb
        dff_ref[...] = _dot_nt(dfb, wd_ref[...]).astype(BF)

    row = lambda w: pl.BlockSpec((t, w), lambda i: (i, 0))
    return _pc(
        body, name="out_bwd", grid=(s // t,),
        in_specs=[row(D), row(D), row(D), row(D), _const((1, D)), _const((1, D)), _const((D, D)), _const((D_FF, D))],
        out_specs=[row(D), row(D), row(D), row(D), row(D_FF), _acc((1, D)), _acc((1, D))],
        out_shape=[SDS((s, D), F32), SDS((s, D), BF), SDS((s, D), BF), SDS((s, D), BF), SDS((s, D_FF), BF),
                   SDS((1, D), F32), SDS((1, D), F32)],
        compiler_params=_cp(1))(dy, pg, pe, f, g5, g4, w_pg, w_down)


def ffn_gate_bwd(up, cg, cv, dff, conv_w, t):
    s = up.shape[0]
    bn = 1408
    hb = t // 8
    nt = s // t
    r = t + 8

    def body(ug_ref, uv_ref, cg_ref, cv_ref, cag_ref, cav_ref, d_ref, da_ref, wg_ref, wv_ref,
             dug_ref, duv_ref, gwg_ref, gwv_ref, gbg_ref, gbv_ref):
        i = pl.program_id(1)

        @pl.when(i == 0)
        def _():
            gwg_ref[...] = jnp.zeros((3, bn), F32)
            gwv_ref[...] = jnp.zeros((3, bn), F32)
            gbg_ref[...] = jnp.zeros((1, bn), F32)
            gbv_ref[...] = jnp.zeros((1, bn), F32)

        def gate(c_g, c_v, d_):
            gl, dgl = _gelu_and_grad(c_g.astype(F32))
            d_ = d_.astype(F32)
            return d_ * c_v.astype(F32) * dgl, d_ * gl

        dg, dv = gate(cg_ref[...], cv_ref[...], d_ref[...])
        nxt = da_ref[...].astype(F32) * (i < nt - 1).astype(F32)
        dg_n, dv_n = gate(cag_ref[...], cav_ref[...], nxt)

        def back(dc, dc_next, u_ref, w_ref, du_ref, gw_ref, gb_ref):
            w = w_ref[...]
            d_ext = jnp.concatenate([dc, dc_next], axis=0)
            d1, d2 = pltpu.roll(d_ext, r - 1, 0)[:t], pltpu.roll(d_ext, r - 2, 0)[:t]
            du_ref[...] = (w[2:3] * dc + w[1:2] * d1 + w[0:1] * d2).astype(BF)
            u = u_ref[...].astype(F32)
            gw_ref[0:1, :] += jnp.sum(d2 * u, axis=0, keepdims=True)
            gw_ref[1:2, :] += jnp.sum(d1 * u, axis=0, keepdims=True)
            gw_ref[2:3, :] += jnp.sum(dc * u, axis=0, keepdims=True)
            gb_ref[...] += jnp.sum(dc, axis=0, keepdims=True)

        back(dg, dg_n, ug_ref, wg_ref, dug_ref, gwg_ref, gbg_ref)
        back(dv, dv_n, uv_ref, wv_ref, duv_ref, gwv_ref, gbv_ref)

    tile = lambda off: pl.BlockSpec((t, bn), lambda j, i: (i, j + off))
    after = lambda off: pl.BlockSpec((8, bn), lambda j, i: (jnp.minimum((i + 1) * hb, nt * hb - 1), j + off))
    cw = lambda off: pl.BlockSpec((3, bn), lambda j, i: (0, j + off))
    cb = lambda off: pl.BlockSpec((1, bn), lambda j, i: (0, j + off))
    return _pc(
        body, name="ffn_gate_bwd", grid=(2, nt),
        in_specs=[tile(0), tile(2), tile(0), tile(0), after(0), after(0), tile(0), after(0), cw(0), cw(2)],
        out_specs=[tile(0), tile(0), cw(0), cw(0), cb(0), cb(0)],
        out_shape=[SDS((s, D_FF), BF), SDS((s, D_FF), BF), SDS((3, D_FF), F32), SDS((3, D_FF), F32),
                   SDS((1, D_FF), F32), SDS((1, D_FF), F32)],
        compiler_params=_cp(2))(up, up, cg, cv, cg, cv, dff, dff, conv_w, conv_w)


def nt_normbwd(dys, w, xin, gain, dres, t, name, side=None):
    s = xin.shape[0]
    nt = s // t
    np_ = len(dys)

    def body(*refs):
        ins_, s_in, (dx_ref, gg_ref), s_out, _, s_scr = _split_side(refs, np_ + 4, 2, 0, side)
        dy_refs = ins_[:np_]
        w_ref, x_ref, g_ref, dres_ref = ins_[np_:]
        i = pl.program_id(0)

        @pl.when(i == 0)
        def _():
            gg_ref[...] = jnp.zeros((1, D), F32)
            if side:
                side.start(s_in, s_out, *s_scr)

        acc = _dot_nt(jnp.concatenate([r_[...] for r_ in dy_refs], axis=1), w_ref[...])
        dxn, dg = _rms_bwd(acc, x_ref[...], g_ref[...])
        dx_ref[...] = dres_ref[...] + dxn
        gg_ref[...] += dg
        if side:
            @pl.when(i == nt - 1)
            def _():
                side.finish(s_in, s_out, *s_scr)

    row = lambda width: pl.BlockSpec((t, width), lambda i: (i, 0))
    si_specs, so_specs, so_shapes, s_scratch, s_ins = _side_specs(side)
    assert sum(dy.shape[1] for dy in dys) == w.shape[1]
    return _pc(
        body, name=name, grid=(nt,),
        in_specs=[row(dy.shape[1]) for dy in dys] + [_const(w.shape), row(D), _const((1, D)), row(D)] + si_specs,
        out_specs=[row(D), _acc((1, D))] + so_specs,
        out_shape=[SDS((s, D), F32), SDS((1, D), F32)] + so_shapes, scratch_shapes=s_scratch,
        compiler_params=_cp(1))(*dys, w, xin, gain, dres, *s_ins)


def mix_bwd(dh1, mx, z, ya, yb, g2, w_mix, a_out, b_out, t):
    s = dh1.shape[0]

    def body(dh_ref, mx_ref, ga_ref, gb_ref, ya_ref, yb_ref, g2_ref, wm_ref, ao_ref, bo_ref,
             dmx_ref, dya_ref, dyb_ref, dga_ref, dgb_ref, dsa_ref, dob_ref, gg2_ref):
        @pl.when(pl.program_id(0) == 0)
        def _():
            gg2_ref[...] = jnp.zeros((1, D), F32)

        dmx, dg2 = _rms_bwd(dh_ref[...], mx_ref[...].astype(F32), g2_ref[...])
        gg2_ref[...] += dg2
        dmxb = dmx.astype(BF)
        dmx_ref[...] = dmxb
        dmp = _dot_nt(dmxb, wm_ref[...])

        def gate(g_ref, y_ref, dy_ref, dg_ref, w_ref, dz_ref):
            sg = jax.nn.sigmoid(g_ref[...].astype(F32))
            dyb_ = (dmp * sg).astype(BF)
            dy_ref[...] = dyb_
            dg_ref[...] = (dmp * y_ref[...].astype(F32) * sg * (1.0 - sg)).astype(BF)
            dz_ref[...] = _dot_nt(dyb_, w_ref[...]).astype(BF)

        gate(ga_ref, ya_ref, dya_ref, dga_ref, ao_ref, dsa_ref)
        gate(gb_ref, yb_ref, dyb_ref, dgb_ref, bo_ref, dob_ref)

    row = lambda w: pl.BlockSpec((t, w), lambda i: (i, 0))
    return _pc(
        body, name="mix_bwd", grid=(s // t,),
        in_specs=[row(D), row(D), pl.BlockSpec((t, D), lambda i: (i, 4)), pl.BlockSpec((t, D), lambda i: (i, 5)),
                  row(D), row(D), _const((1, D)), _const((D, D)), _const((A_W, D)), _const((D, D))],
        out_specs=[row(D)] * 5 + [row(A_W), row(D), _acc((1, D))],
        out_shape=[SDS((s, D), BF)] * 5 + [SDS((s, A_W), BF), SDS((s, D), BF), SDS((1, D), F32)],
        compiler_params=_cp(1))(dh1, mx, z, z, ya, yb, g2, w_mix, a_out, b_out)


def sgu_bwd(z, dsa, ln_g, ln_b, w_cat, w_cat_t, bias_full, bdm, t):
    s = z.shape[0]
    nt = s // t
    nch = t // A_C

    def body(u_ref, v_ref, dsa_ref, g_ref, b_ref, wc_ref, wct_ref, bias_ref, bdm_ref,
             duv_ref, glg_ref, glb_ref, gws_ref, gbs_ref, ds_acc):
        i = pl.program_id(0)

        @pl.when(i == 0)
        def _():
            glg_ref[...] = jnp.zeros((1, A_W), F32)
            glb_ref[...] = jnp.zeros((1, A_W), F32)
            gws_ref[...] = jnp.zeros((A_C, A_G * A_C), F32)
            ds_acc[...] = jnp.zeros((A_C, A_W), F32)

        lng, bdm_ = g_ref[...], bdm_ref[...]
        rec = [_sgu_recompute(v_ref[pl.ds(ci * A_C, A_C), :].astype(F32), lng, b_ref[...]) for ci in range(nch)]
        spread_vn = _sgu_spread([r_[3] for r_ in rec], bdm_)
        mixed = _dot(_sgu_weights(wc_ref, False), spread_vn)
        dsas, dss, dgus = [], [], []
        for ci in range(nch):
            rows = pl.ds(ci * A_C, A_C)
            gu, dgu = _gelu_and_grad(u_ref[rows, :].astype(F32))
            dsa_ = dsa_ref[rows, :].astype(F32)
            ds = dsa_ * gu
            ds_acc[...] += ds
            dsas.append(dsa_)
            dgus.append(dgu)
            dss.append(ds.astype(BF))
        r = lax.broadcasted_iota(jnp.int32, (A_C, A_G * A_C), 0)
        c = lax.broadcasted_iota(jnp.int32, (A_C, A_G * A_C), 1) & (A_C - 1)
        gws_ref[...] += jnp.where(c <= r, _dot_nt(jnp.concatenate(dss, axis=1), spread_vn), 0.0)
        dvns = _dot(_sgu_weights(wct_ref, True), _sgu_spread(dss, bdm_))
        for ci in range(nch):
            rows = pl.ds(ci * A_C, A_C)
            dgv, rstd, xhat, _ = rec[ci]
            dvn = dvns[:, ci * A_W:(ci + 1) * A_W]
            glb_ref[...] += jnp.sum(dvn, axis=0, keepdims=True)
            glg_ref[...] += jnp.sum(dvn * xhat, axis=0, keepdims=True)
            dxh = dvn * lng
            dgv_ = rstd * (dxh - jnp.mean(dxh, axis=-1, keepdims=True)
                           - xhat * jnp.mean(dxh * xhat, axis=-1, keepdims=True))
            s_ = mixed[:, ci * A_W:(ci + 1) * A_W] + bias_ref[...]
            duv_ref[rows, :A_W] = (dsas[ci] * s_ * dgus[ci]).astype(BF)
            duv_ref[rows, A_W:] = (dgv_ * dgv).astype(BF)

        @pl.when(i == nt - 1)
        def _():
            acc = ds_acc[...]
            for g in range(A_G):
                gbs_ref[:, g:g + 1] = jnp.sum(acc[:, g * A_GD:(g + 1) * A_GD], axis=1, keepdims=True)

    return _pc(
        body, name="sgu_bwd", grid=(nt,),
        in_specs=[pl.BlockSpec((t, A_W), lambda i: (i, 0)), pl.BlockSpec((t, A_W), lambda i: (i, 1)),
                  pl.BlockSpec((t, A_W), lambda i: (i, 0)),
                  _const((1, A_W)), _const((1, A_W)), _const((A_C, A_G * A_C)), _const((A_C, A_G * A_C)),
                  _const((A_C, A_W)), _const((A_G * A_C, A_W))],
        out_specs=[pl.BlockSpec((t, D), lambda i: (i, 0)), _acc((1, A_W)), _acc((1, A_W)),
                   _acc((A_C, A_G * A_C)), _acc((A_C, A_G))],
        out_shape=[SDS((s, D), BF), SDS((1, A_W), F32), SDS((1, A_W), F32), SDS((A_C, A_G * A_C), F32),
                   SDS((A_C, A_G), F32)],
        scratch_shapes=[pltpu.VMEM((A_C, A_W), F32)],
        compiler_params=_cp(1))(z, z, dsa, ln_g, ln_b, w_cat, w_cat_t, bias_full, bdm)


def gla_bwd(z, qk32, zl, o, dob, states, wgk, bias, wn, ltri, ltri_t, t, side=None):
    s = z.shape[0]
    nt = s // t
    nc = t // B_C
    hpb = GLA_HPB
    kw, vw = hpb * B_HK, hpb * B_HV

    def body(*refs):
        ins_, s_in, outs_, s_out, scr_, s_scr = _split_side(refs, 12, 6, 5, side)
        qk_ref, v_ref, og_ref, lr_ref, o_ref, dob_ref, st_ref, wgk_ref, bias_ref, wn_ref, l_ref, lt_ref = ins_
        dqk_ref, dv_ref, dog_ref, dpre_ref, gbias_ref, gwn_ref = outs_
        dst_sc, dv_sc, dqd_sc, dkt_sc, ddec_sc = scr_
        i = pl.program_id(0)
        g = pl.program_id(1)

        @pl.when((i == 0) & (g == 0))
        def _():
            gbias_ref[...] = jnp.zeros((B_H, 1, B_HK), F32)
            gwn_ref[...] = jnp.zeros((1, B_HV), F32)
            if side:
                side.start(s_in, s_out, *s_scr)

        @pl.when(i == 0)
        def _():
            for hh in range(hpb):
                dst_sc[g * hpb + hh] = jnp.zeros((B_HV, B_HK), F32)

        lr, l, lt = lr_ref[...], l_ref[...], lt_ref[...]
        keep, keep_t = l > 0, lt > 0
        wn_ = wn_ref[...]
        last = lax.broadcasted_iota(jnp.int32, (nc, B_C, B_HK), 1) == B_C - 1
        for hh in range(hpb):
            h = g * hpb + hh
            cv, ck = slice(hh * B_HV, (hh + 1) * B_HV), slice(hh * B_HK, (hh + 1) * B_HK)
            pre, b, bl, eb, enb, etb, qd, ki, kt = _gla_decays(qk_ref[:, cv], lr, wgk_ref[:, ck], bias_ref[:, ck], l, t)
            qdb, kib, ktb = qd.astype(BF), ki.astype(BF), kt.astype(BF)
            vb = v_ref[:, cv]
            o_ = o_ref[:, cv].astype(F32)
            og = og_ref[:, cv].astype(F32)
            sog = jax.nn.sigmoid(og)
            dob_ = dob_ref[:, cv].astype(F32)
            don = dob_ * og * sog
            do, dwn = _rms_bwd(don, o_, wn_)
            gwn_ref[...] += dwn
            dog_ref[:, cv] = (dob_ * _rms(o_, wn_) * sog * (1.0 + og * (1.0 - sog))).astype(BF)
            dob16 = do.astype(BF)
            sc_t = jnp.where(keep_t, _dot_nt(kib, qdb), 0.0).astype(BF)
            dsc = jnp.where(keep, _dot_nt(dob16, vb), 0.0).astype(BF)
            dsc_t = jnp.where(keep_t, _dot_nt(vb, dob16), 0.0).astype(BF)
            dv_sc[hh] = _dot(sc_t, dob16)
            dqd_sc[hh] = _dot(dsc, kib)
            dki = _dot(dsc_t, qdb)
            for n in reversed(range(nc)):
                rows = slice(n * B_C, (n + 1) * B_C)
                dst = dst_sc[h]
                dstb = dst.astype(BF)
                stp = st_ref[n, hh]
                dv_sc[hh, rows, :] += _dot_nt(ktb[rows], dstb)
                dkt_sc[hh, rows, :] = _dot(vb[rows], dstb)
                dqd_sc[hh, rows, :] += _dot(dob16[rows], stp)
                dec = jnp.exp(bl[n * B_C:n * B_C + 1, :])
                ddec_sc[hh, n] = jnp.sum(dst * stp.astype(F32), axis=0, keepdims=True) * dec
                dst_sc[h] = dst * dec + _dot_tn(dob16[rows], qdb[rows])
            dqd, dkt = dqd_sc[hh], dkt_sc[hh]
            dv_ref[:, cv] = dv_sc[hh].astype(BF)
            dqk_ref[:, hh * B_HV:hh * B_HV + B_HK] = (dqd * eb * (B_HK ** -0.5)).astype(BF)
            dqk_ref[:, hh * B_HV + B_HK:(hh + 1) * B_HV] = (dki * enb + dkt * etb).astype(BF)
            dktkt = dkt * kt
            db3 = (dqd * qd - dki * ki - dktkt).reshape(nc, B_C, B_HK)
            dbl = jnp.sum(dktkt.reshape(nc, B_C, B_HK), axis=1, keepdims=True) + ddec_sc[hh]
            db = (db3 + jnp.where(last, dbl, 0.0)).reshape(t, B_HK)
            dla = _ldot3(lt, db)
            dpre = dla * (1.0 / 16.0) * (1.0 - jax.nn.sigmoid(pre))
            dpre_ref[:, ck] = dpre.astype(BF)
            gbias_ref[h] += jnp.sum(dpre, axis=0, keepdims=True)
        if side:
            @pl.when((i == nt - 1) & (g == B_H // hpb - 1))
            def _():
                side.finish(s_in, s_out, *s_scr)

    rv = lambda i: nt - 1 - i
    si_specs, so_specs, so_shapes, s_scratch, s_ins = _side_specs(side)
    vo, go = 2048 // vw, 3072 // vw
    tile = lambda off: pl.BlockSpec((t, vw), lambda i, g: (rv(i), off + g))
    return _pc(
        body, name="gla_bwd", grid=(nt, B_H // hpb),
        in_specs=[tile(0), tile(vo), tile(go), pl.BlockSpec((t, LANE), lambda i, g: (rv(i), 0)), tile(0), tile(0),
                  pl.BlockSpec((nc, hpb, B_HV, B_HK), lambda i, g: (rv(i), g, 0, 0)),
                  pl.BlockSpec((LANE, kw), lambda i, g: (0, g)), pl.BlockSpec((1, kw), lambda i, g: (0, g)),
                  _const((1, B_HV)), _const((t, t)), _const((t, t))] + si_specs,
        out_specs=[tile(0), tile(0), tile(0), pl.BlockSpec((t, kw), lambda i, g: (rv(i), g)),
                   _acc((B_H, 1, B_HK)), _acc((1, B_HV))] + so_specs,
        out_shape=[SDS((s, D), BF), SDS((s, D), BF), SDS((s, D), BF), SDS((s, B_H * B_HK), BF),
                   SDS((B_H, 1, B_HK), F32), SDS((1, B_HV), F32)] + so_shapes,
        scratch_shapes=[pltpu.VMEM((B_H, B_HV, B_HK), F32), pltpu.VMEM((hpb, t, B_HV), F32),
                        pltpu.VMEM((hpb, t, B_HK), F32), pltpu.VMEM((hpb, t, B_HK), F32),
                        pltpu.VMEM((hpb, nc, 1, B_HK), F32)] + s_scratch,
        compiler_params=_cp(2))(qk32, z, z, zl, o, dob, states, wgk, bias, wn, ltri, ltri_t, *s_ins)


def mm_tn(a, b, name, tk=2048):
    s, m = a.shape
    n = b.shape[1]
    bn = next(c for c in (1024, 1408, 512, 256, 128) if n % c == 0 and m * c * 4 <= 6 * 1024 * 1024)
    tk = min(tk, s)
    nk = s // tk

    def body(a_ref, b_ref, o_ref, acc):
        k = pl.program_id(1)

        @pl.when(k == 0)
        def _():
            acc[...] = jnp.zeros((m, bn), F32)

        acc[...] += _dot_tn(a_ref[...].astype(BF), b_ref[...])

        @pl.when(k == nk - 1)
        def _():
            o_ref[...] = acc[...].astype(BF)

    return _pc(
        body, name=name, grid=(n // bn, nk),
        in_specs=[pl.BlockSpec((tk, m), lambda j, k: (k, 0)), pl.BlockSpec((tk, bn), lambda j, k: (k, j))],
        out_specs=pl.BlockSpec((m, bn), lambda j, k: (0, j)),
        out_shape=SDS((m, n), BF), scratch_shapes=[pltpu.VMEM((m, bn), F32)], compiler_params=_cp(2))(a, b)


def mm_cols(a, w, blk, width, t, name):
    s, k = a.shape

    def body(a_ref, w_ref, o_ref):
        o_ref[...] = _dot(a_ref[...], w_ref[...]).astype(BF)

    return _pc(body, name=name, grid=(s // t,),
               in_specs=[pl.BlockSpec((t, k), lambda i: (i, 0)),
                         pl.BlockSpec((k, width), lambda i: (0, blk), pipeline_mode=pl.Buffered(1))],
               out_specs=pl.BlockSpec((t, width), lambda i: (i, 0)), out_shape=SDS((s, width), BF),
               compiler_params=_cp(1))(a, w)


def mm_nt_small(a, w, t, name):
    s, k = a.shape
    n = w.shape[0]

    def body(a_ref, w_ref, o_ref):
        o_ref[...] = _dot_nt(a_ref[...], w_ref[...]).astype(BF)

    return _pc(body, name=name, grid=(s // t,),
               in_specs=[pl.BlockSpec((t, k), lambda i: (i, 0)), _const((n, k))],
               out_specs=pl.BlockSpec((t, n), lambda i: (i, 0)), out_shape=SDS((s, n), BF),
               compiler_params=_cp(1))(a, w)


def _adamw(w, g, m, v):
    m = ADAM_B1 * m + (1.0 - ADAM_B1) * g
    v = ADAM_B2 * v + (1.0 - ADAM_B2) * (g * g)
    m_hat = m / (1.0 - ADAM_B1 ** ADAM_STEP)
    v_hat = v / (1.0 - ADAM_B2 ** ADAM_STEP)
    return -ADAM_LR * (m_hat / (jnp.sqrt(v_hat) + ADAM_EPS) + ADAM_WD * w), m, v


def _half_rows(rows):
    rh = rows // 2
    return rh, max(b for b in range(16, 257, 16) if rh % b == 0)


def _pc_sp(body, grid, in_specs, out_specs, out_shape, name):
    gs = pltpu.PrefetchScalarGridSpec(num_scalar_prefetch=1, grid=grid, in_specs=in_specs, out_specs=out_specs)
    return _pc(body, grid_spec=gs, out_shape=out_shape, name=name, compiler_params=_cp(len(grid)))


def adamw_halves(sc, own, sib, w, m, v, name):
    rows, cols = w.shape
    rh, br = _half_rows(rows)
    nbk = rh // br

    def body(sc_ref, own_ref, sib_ref, w_ref, m_ref, v_ref, go_ref, d_ref, mo_ref, vo_ref):
        g_ = jnp.where(pl.program_id(0) // nbk == sc_ref[0], own_ref[...], sib_ref[...])
        go_ref[...] = g_
        d_ref[...], mo_ref[...], vo_ref[...] = _adamw(w_ref[...], g_, m_ref[...], v_ref[...])

    half = pl.BlockSpec((br, cols), lambda i, sc_: (i % nbk, 0))
    blk = pl.BlockSpec((br, cols), lambda i, sc_: (i, 0))
    return _pc_sp(body, (2 * nbk,), [half, half, blk, blk, blk], [blk] * 4, [SDS((rows, cols), F32)] * 4,
                  name)(sc, own, sib, w, m, v)


def adamw_cols(sc, own, sib, w, m, v, name, cb=256):
    rows, cols = w.shape
    nk = cols // 2 // cb

    def body(sc_ref, own_ref, sib_ref, w_ref, m_ref, v_ref, go_ref, d_ref, mo_ref, vo_ref):
        g_ = jnp.where(pl.program_id(0) == sc_ref[0], own_ref[...], sib_ref[...])
        go_ref[...] = g_
        d_ref[...], mo_ref[...], vo_ref[...] = _adamw(w_ref[...], g_, m_ref[...], v_ref[...])

    half = pl.BlockSpec((rows, cb), lambda h, k, sc_: (0, k))
    blk = pl.BlockSpec((rows, cb), lambda h, k, sc_: (0, h * nk + k))
    return _pc_sp(body, (2, nk), [half, half, blk, blk, blk], [blk] * 4, [SDS((rows, cols), F32)] * 4,
                  name)(sc, own, sib, w, m, v)


def adamw_small(g, w, m, v):
    def body(g_ref, w_ref, m_ref, v_ref, d_ref, mo_ref, vo_ref):
        d_ref[...], mo_ref[...], vo_ref[...] = _adamw(w_ref[...], g_ref[...], m_ref[...], v_ref[...])

    vm = pl.BlockSpec(memory_space=pltpu.VMEM)
    return _pc(body, name="adamw_small", in_specs=[vm] * 4, out_specs=[vm] * 3, out_shape=[SDS(g.shape, F32)] * 3,
               compiler_params=pltpu.CompilerParams(vmem_limit_bytes=VMEM_LIMIT))(g, w, m, v)


def _pos():
    return lax.axis_index("x"), lax.axis_index("y"), lax.axis_index("c")


def _other_chips(x, y):
    return [(1 - x, y), (x, 1 - y), (1 - x, 1 - y)]


_ANY = pl.BlockSpec(memory_space=pltpu.HBM)


class _Side:
    def __init__(self, ins, out_shapes, nsem, start, finish):
        self.ins, self.out_shapes, self.start, self.finish = list(ins), list(out_shapes), start, finish
        self.scratch = [pltpu.SemaphoreType.DMA((nsem,)), pltpu.SemaphoreType.DMA((nsem,))]
        self.n_in, self.n_out = len(self.ins), len(self.out_shapes)


def _run_side(side, name):
    def body(*refs):
        args_ = (refs[:side.n_in], refs[side.n_in:side.n_in + side.n_out], *refs[side.n_in + side.n_out:])
        side.start(*args_)
        side.finish(*args_)

    return _pc(body, name=name, in_specs=[_ANY] * side.n_in, out_specs=[_ANY] * side.n_out,
               out_shape=side.out_shapes, scratch_shapes=side.scratch)(*side.ins)


def gather_side(bigs, tinies):
    nb, nt_ = len(bigs), len(tinies)

    def plan(ins, outs, ssem, rsem):
        x, y, c = _pos()
        me = 2 * x + y
        chips = _other_chips(x, y)
        sibling = (x, y, 1 - c)

        def copy(k, src, dst, to):
            return pltpu.make_async_remote_copy(src_ref=src, dst_ref=dst, send_sem=ssem.at[k], recv_sem=rsem.at[k],
                                                device_id=to, device_id_type=MESH)

        sends, landed, passed_on, tiny_landed = [], [], [], []
        for w in range(nb):
            rh = bigs[w].shape[0] // 2
            mine = pl.ds(pl.multiple_of(c * rh, 16), rh)
            theirs = pl.ds(pl.multiple_of((1 - c) * rh, 16), rh)
            for j, (cx, cy) in enumerate(chips):
                sends.append(copy(6 * w + j, ins[w].at[mine], outs[w].at[me, mine], (cx, cy, c)))
                blk = outs[w].at[2 * cx + cy, mine]
                landed.append((copy(6 * w + j, blk, blk, (cx, cy, c)), copy(6 * w + 3 + j, blk, blk, sibling)))
                blk = outs[w].at[2 * cx + cy, theirs]
                passed_on.append(copy(6 * w + 3 + j, blk, blk, sibling))
        for w in range(nt_):
            for j, (cx, cy) in enumerate(chips):
                k = 6 * nb + 3 * w + j
                sends.append(copy(k, ins[nb + w], outs[nb + w].at[me], (cx, cy, c)))
                blk = outs[nb + w].at[2 * cx + cy]
                tiny_landed.append(copy(k, blk, blk, (cx, cy, c)))
        return sends, landed, passed_on, tiny_landed

    def start(ins, outs, ssem, rsem):
        for cp in plan(ins, outs, ssem, rsem)[0]:
            cp.start()

    def finish(ins, outs, ssem, rsem):
        sends, landed, passed_on, tiny_landed = plan(ins, outs, ssem, rsem)
        for arrived, forward in landed:
            arrived.wait_recv()
            forward.start()
        for arrived in tiny_landed + passed_on:
            arrived.wait_recv()
        for cp in sends + [forward for _, forward in landed]:
            cp.wait_send()

    return _Side(list(bigs) + list(tinies), [SDS((4,) + a.shape, a.dtype) for a in list(bigs) + list(tinies)],
                 6 * nb + 3 * nt_, start, finish)


def swap_halves(gs, name):
    n = len(gs)

    def body(*refs):
        g_refs, sib_refs = refs[:n], refs[n:2 * n]
        ssem, rsem = refs[2 * n:]
        x, y, c = _pos()
        cps = []
        for w in range(n):
            rh = gs[w].shape[1] // 2
            give = pl.ds(pl.multiple_of((1 - c) * rh, 16), rh)
            cp = pltpu.make_async_remote_copy(src_ref=g_refs[w].at[:, give], dst_ref=sib_refs[w], send_sem=ssem.at[w],
                                              recv_sem=rsem.at[w], device_id=(x, y, 1 - c), device_id_type=MESH)
            cp.start()
            cps.append(cp)
        for cp in cps:
            cp.wait()

    return _pc(body, name=name, in_specs=[_ANY] * n, out_specs=[_ANY] * n,
               out_shape=[SDS((g.shape[0], g.shape[1] // 2, g.shape[2]), g.dtype) for g in gs],
               scratch_shapes=[pltpu.SemaphoreType.DMA((n,)), pltpu.SemaphoreType.DMA((n,))])(*gs)


def add_half(sc, g, sib, name):
    l, r, cols = g.shape
    rh, br = _half_rows(r)
    nbk = rh // br

    def body(sc_ref, g_ref, s_ref, o_ref):
        o_ref[...] = (g_ref[...].astype(F32) + s_ref[...].astype(F32)).astype(BF)

    blk = pl.BlockSpec((1, br, cols), lambda j, i, sc_: (j, i, 0))
    return _pc_sp(body, (l, nbk), [pl.BlockSpec((1, br, cols), lambda j, i, sc_: (j, sc_[0] * nbk + i, 0)), blk], blk,
                  SDS((l, rh, cols), BF), name)(sc, g, sib)


def exchange_side(ps):
    n_ = len(ps)

    def width(p_):
        return p_.shape[2] if p_.shape[0] == 4 else p_.shape[2] // 4

    def plan(p_refs, got_refs, ssem, rsem):
        x, y, c = _pos()
        cps = []
        for w in range(n_):
            wd = width(ps[w])
            for j, (cx, cy) in enumerate(_other_chips(x, y)):
                to = 2 * cx + cy
                src = p_refs[w].at[to] if ps[w].shape[0] == 4 else p_refs[w].at[0, :, pl.ds(pl.multiple_of(to * wd, LANE), wd)]
                cps.append(pltpu.make_async_remote_copy(
                    src_ref=src, dst_ref=got_refs[w].at[j], send_sem=ssem.at[3 * w + j], recv_sem=rsem.at[3 * w + j],
                    device_id=(cx, cy, c), device_id_type=MESH))
        return cps

    def start(*refs):
        for cp in plan(*refs):
            cp.start()

    def finish(*refs):
        for cp in plan(*refs):
            cp.wait()

    return _Side(ps, [SDS((3, p_.shape[1], width(p_)), p_.dtype) for p_ in ps], 3 * n_, start, finish)


def sum4(sc, p, got, name):
    _, rh, wd = got.shape
    _, br = _half_rows(2 * rh)

    def body(sc_ref, p_ref, g_ref, r_ref):
        r_ref[...] = ((p_ref[0].astype(F32) + g_ref[0].astype(F32)) + (g_ref[1].astype(F32) + g_ref[2].astype(F32)))

    own = (pl.BlockSpec((1, br, wd), lambda i, sc_: (sc_[1], i, 0)) if p.shape[0] == 4
           else pl.BlockSpec((1, br, wd), lambda i, sc_: (0, i, sc_[1])))
    return _pc_sp(body, (rh // br,), [own, pl.BlockSpec((3, br, wd), lambda i, sc_: (0, i, 0))],
                  pl.BlockSpec((br, wd), lambda i, sc_: (i, 0)), SDS((rh, wd), F32), name)(sc, p, got)


def join_halves(halves):
    n = len(halves)

    def body(*refs):
        h_refs, got_refs = refs[:n], refs[n:2 * n]
        ssem, rsem = refs[2 * n:]
        x, y, c = _pos()
        cps = []
        for w in range(n):
            cp = pltpu.make_async_remote_copy(src_ref=h_refs[w], dst_ref=got_refs[w], send_sem=ssem.at[w],
                                              recv_sem=rsem.at[w], device_id=(x, y, 1 - c), device_id_type=MESH)
            cp.start()
            cps.append(cp)
        for cp in cps:
            cp.wait()

    return _pc(body, name="join_halves", in_specs=[_ANY] * n, out_specs=[_ANY] * n,
               out_shape=[SDS(h.shape, h.dtype) for h in halves],
               scratch_shapes=[pltpu.SemaphoreType.DMA((n,)), pltpu.SemaphoreType.DMA((n,))])(*halves)


def allreduce_small(g):
    rows = g.shape[0]
    rh = rows // 2

    def body(g_ref, out_ref, sib_buf, chip_buf, sum_sc, ssem, rsem):
        x, y, c = _pos()
        me = 2 * x + y
        sibling = (x, y, 1 - c)
        mine = pl.ds(pl.multiple_of(c * rh, 8), rh)

        def copy(k, src, dst, to):
            return pltpu.make_async_remote_copy(src_ref=src, dst_ref=dst, send_sem=ssem.at[k], recv_sem=rsem.at[k],
                                                device_id=to, device_id_type=MESH)

        cp = copy(0, g_ref, sib_buf, sibling)
        cp.start()
        cp.wait()
        sum_sc[...] = g_ref[...] + sib_buf[...]
        chips = _other_chips(x, y)
        cps = [copy(1 + j, sum_sc.at[mine], chip_buf.at[me], (cx, cy, c)) for j, (cx, cy) in enumerate(chips)]
        for cp in cps:
            cp.start()
        chip_buf[me] = sum_sc[mine, :]
        for j, (cx, cy) in enumerate(chips):
            copy(1 + j, sum_sc.at[mine], chip_buf.at[2 * cx + cy], (cx, cy, c)).wait_recv()
        for cp in cps:
            cp.wait_send()
        out_ref[mine, :] = (chip_buf[0] + chip_buf[1]) + (chip_buf[2] + chip_buf[3])
        cp = copy(4, out_ref.at[mine], out_ref.at[mine], sibling)
        cp.start()
        cp.wait()

    vm = pl.BlockSpec(memory_space=pltpu.VMEM)
    return _pc(body, name="allreduce_small", in_specs=[vm], out_specs=vm, out_shape=SDS((rows, LANE), F32),
               scratch_shapes=[pltpu.VMEM((rows, LANE), F32), pltpu.VMEM((4, rh, LANE), F32), pltpu.VMEM((rows, LANE), F32),
                               pltpu.SemaphoreType.DMA((5,)), pltpu.SemaphoreType.DMA((5,))],
               compiler_params=pltpu.CompilerParams(vmem_limit_bytes=VMEM_LIMIT))(g)


def _pack_small(entries, get):
    flat = jnp.concatenate([get(n).reshape(-1).astype(F32) for n, _ in entries])
    rows = -(-flat.shape[0] // (8 * LANE)) * 8
    return jnp.pad(flat, (0, rows * LANE - flat.shape[0])).reshape(rows, LANE)


def _unpack_small(entries, packed):
    out, off = {}, 0
    flat = packed.reshape(-1)
    for name, n in entries:
        out[name] = flat[off:off + n]
        off += n
    return out


def _cols_full(blk):
    return blk.transpose(1, 0, 2).reshape(blk.shape[1], 4 * blk.shape[2])


def kernel(x, p, pre_mix_norm, w_in, a_ln_g, a_ln_b, a_spatial_w, a_spatial_b, a_out, b_gk, b_gk_bias, b_out_norm, b_out, w_mix_out, post_mix_norm, pre_ffn_norm, w_up, conv_w, conv_b, w_down, post_ffn_norm, w_ple, w_ple_gate, post_ple_norm, loss_target, m_pre_mix_norm, m_w_in, m_a_ln_g, m_a_ln_b, m_a_spatial_w, m_a_spatial_b, m_a_out, m_b_gk, m_b_gk_bias, m_b_out_norm, m_b_out, m_w_mix_out, m_post_mix_norm, m_pre_ffn_norm, m_w_up, m_conv_w, m_conv_b, m_w_down, m_post_ffn_norm, m_w_ple, m_w_ple_gate, m_post_ple_norm, v_pre_mix_norm, v_w_in, v_a_ln_g, v_a_ln_b, v_a_spatial_w, v_a_spatial_b, v_a_out, v_b_gk, v_b_gk_bias, v_b_out_norm, v_b_out, v_w_mix_out, v_post_mix_norm, v_pre_ffn_norm, v_w_up, v_conv_w, v_conv_b, v_w_down, v_post_ffn_norm, v_w_ple, v_w_ple_gate, v_post_ple_norm):
    args = dict(locals())
    order = ['pre_mix_norm', 'w_in', 'a_ln_g', 'a_ln_b', 'a_spatial_w', 'a_spatial_b', 'a_out', 'b_gk', 'b_gk_bias',
             'b_out_norm', 'b_out', 'w_mix_out', 'post_mix_norm', 'pre_ffn_norm', 'w_up', 'conv_w', 'conv_b', 'w_down',
             'post_ffn_norm', 'w_ple', 'w_ple_gate', 'post_ple_norm']
    assert sorted(BIG + TINY + tuple(n for n, _ in SMALL)) == sorted(order)
    s = x.shape[1]
    xs = x.reshape(s, D)
    ps = p.reshape(s, PLE)
    tgt = loss_target.reshape(s, D)
    t_big = min(1024, s)
    t_mid = min(512, s)
    t_small = min(256, s)
    t_gla = min(256, s)
    mx_, my_, mc_ = _pos()
    me = 2 * mx_ + my_
    sc = jnp.stack([mc_, me]).astype(jnp.int32)
    shard = lambda n: args[n].reshape(args[n].shape[1:])

    mine = {n: shard(n).astype(BF) for n in BIG}
    mine.update({n: shard(n) for n in TINY})
    fill = lambda names, gots: {n: lax.dynamic_update_slice(got, mine[n][None], (me, 0, 0)) for n, got in zip(names, gots)}
    first = ("w_in",) + TINY
    full = fill(first, _run_side(gather_side([mine["w_in"]], [mine[n] for n in TINY]), "gather_first"))
    wi = _cols_full(full["w_in"])
    seg = lambda a, b: wi[:, a:b]
    qk = [seg(1024 + h * B_HK, 1024 + (h + 1) * B_HK) for h in range(B_H)]
    kk = [seg(1536 + h * B_HK, 1536 + (h + 1) * B_HK) for h in range(B_H)]
    w_z = jnp.concatenate([seg(0, 1024)] + [m_ for h in range(B_H) for m_ in (qk[h], kk[h])]
                          + [seg(2048, 4096), seg(4112, 6160), seg(4096, 4112), jnp.zeros((D, LANE - B_RANK), BF)], axis=1)
    wgk = jnp.pad(_cols_full(full["b_gk"]).astype(BF), ((0, LANE - B_RANK), (0, 0)))
    w_conv = _cols_full(full["conv_w"])
    g1, g2, g3 = pre_mix_norm.reshape(1, D), post_mix_norm.reshape(1, D), pre_ffn_norm.reshape(1, D)
    g4, g5 = post_ffn_norm.reshape(1, D), post_ple_norm.reshape(1, D)
    ln_g, ln_b = a_ln_g.reshape(1, A_W), a_ln_b.reshape(1, A_W)
    w_s = a_spatial_w.reshape(A_G, A_C, A_C)
    w_cat = w_s.transpose(1, 0, 2).reshape(A_C, A_G * A_C)
    w_cat_t = w_s.transpose(2, 0, 1).reshape(A_C, A_G * A_C)
    bias_full = jnp.repeat(a_spatial_b.reshape(A_G, A_C).T, A_GD, axis=1)
    bdm = (jnp.arange(A_G * A_C)[:, None] // A_C == jnp.arange(A_W)[None, :] // A_GD).astype(BF)
    gk_bias = b_gk_bias.reshape(1, B_H * B_HK)
    wn = b_out_norm.reshape(1, B_HV)
    cb = conv_b.reshape(1, 2 * D_FF)
    idx = jnp.arange(t_gla)
    ltri = ((idx[:, None] // B_C == idx[None, :] // B_C) & (idx[None, :] <= idx[:, None])).astype(BF)

    a, z, qk32, *gots = norm_matmul(xs, g1, w_z, D, t_big, "in_proj", nblk=6, f32_blk=1,
                                    side=gather_side([mine[n] for n in BIG[1:]], []))
    full.update(fill(BIG[1:], gots))
    w_aout, w_ple_f = _cols_full(full["a_out"]), _cols_full(full["w_ple"])
    w_bout, w_mix, w_pg = (full[n].reshape(D, D) for n in ("b_out", "w_mix_out", "w_ple_gate"))
    w_dn, w_up3 = full["w_down"].reshape(D_FF, D), full["w_up"]
    zl = mm_cols(a, w_z, 48, LANE, t_big, "lr_proj")
    sa = sgu_fwd(z, ln_g, ln_b, w_cat, bias_full, bdm, t_mid)
    ob, o, states = gla_fwd(z, qk32, zl, wgk, gk_bias, wn, ltri, t_gla)
    ya, yb, mp, mx, h1 = mix_fwd(sa, ob, z, xs, w_aout, w_bout, w_mix, g2, t_small)
    c, up_g, up_v, cg, cv, ff = ffn_up_fwd(h1, g3, w_up3, w_conv, cb, t_mid)
    f, h2, pg, pe, dy, loss = out_fwd(ff, h1, ps, tgt, w_dn, w_pg, w_ple_f, g4, g5, t_small)

    dh2, dpe, dpg, df, dff, gg5, gg4 = out_bwd(dy, pg, pe, f, g5, g4, w_pg, w_dn, t_small)
    dup_g, dup_v, gcw, gcb, dh1, gg3 = ffn_up_bwd(up_g, up_v, cg, cv, dff, w_conv, _cols_full(w_up3), h1, g3, dh2,
                                                  t_small)
    dmx, dya, dyb, dga, dgb, dsa, dob, gg2 = mix_bwd(dh1, mx, z, ya, yb, g2, w_mix, w_aout, w_bout, t_small)
    duv, g_lng, g_lnb, g_wcat, g_bst = sgu_bwd(z, dsa, ln_g, ln_b, w_cat, w_cat_t, bias_full, bdm, t_mid)
    g_ws = g_wcat.reshape(A_C, A_G, A_C).transpose(1, 0, 2)

    grads = {
        "a_out": mm_tn(sa, dya, "dw_a_out")[None],
        "b_out": mm_tn(ob, dyb, "dw_b_out").reshape(4, D // 4, D),
        "w_mix_out": mm_tn(mp, dmx, "dw_mix").reshape(4, D // 4, D),
        "w_up": jnp.concatenate([mm_tn(c, dup_g, "dw_up_g"), mm_tn(c, dup_v, "dw_up_v")], axis=1)[None],
        "w_down": mm_tn(ff, df, "dw_down").reshape(4, D_FF // 4, D),
        "w_ple": mm_tn(ps, dpe, "dw_ple")[None],
        "w_ple_gate": mm_tn(h2, dpg, "dw_ple_gate").reshape(4, D // 4, D),
    }

    def chip_partials(names):
        gs = [grads[n] for n in names]
        return [add_half(sc, g, sib, "partial_" + n)
                for n, g, sib in zip(names, gs, swap_halves(gs, "swap_halves_" + names[0]))]

    parts = dict(zip(BIG[1:], chip_partials(BIG[1:])))
    dqk, dvb, dog, dpre, g_gkb, g_wn, *gots = gla_bwd(z, qk32, zl, o, dob, states, wgk, gk_bias, wn, ltri, ltri.T,
                                                      t_gla, side=exchange_side([parts[n] for n in BIG[1:]]))
    got = dict(zip(BIG[1:], gots))
    dlr = mm_nt_small(dpre, wgk, t_mid, "dlr")
    segs = [duv, dqk, dvb, dog, dga, dgb, dlr]

    gz = [mm_tn(a, sg_, "dw_in_%d" % k) for k, sg_ in enumerate(segs)]
    gq = [gz[1][:, h * 256:h * 256 + B_HK] for h in range(B_H)]
    gk = [gz[1][:, h * 256 + B_HK:(h + 1) * 256] for h in range(B_H)]
    g_in = jnp.concatenate([gz[0]] + gq + gk + [gz[2], gz[3], gz[6][:, :B_RANK], gz[4], gz[5]], axis=1)
    grads["w_in"] = g_in.reshape(D, 4, 1540).transpose(1, 0, 2)
    parts["w_in"], = chip_partials(("w_in",))
    dx, gg1, got["w_in"] = nt_normbwd(segs, w_z, xs, g1, dh1, t_small, "in_bwd",
                                      side=exchange_side([parts["w_in"]]))

    reds = [sum4(sc, parts[n], got[n], "sum_" + n) for n in BIG]
    outs = {}
    for n, red, sib in zip(BIG, reds, join_halves(reds)):
        if n == "w_in":
            res = adamw_cols(sc, red.T, sib.T, shard(n).T, shard("m_" + n).T, shard("v_" + n).T, "adamw_" + n)
            res = [r_.T for r_ in res]
        else:
            res = adamw_halves(sc, red, sib, shard(n), shard("m_" + n), shard("v_" + n), "adamw_" + n)
        outs[n] = [r_.reshape(args[n].shape) for r_ in res]

    small_g = {
        "pre_mix_norm": gg1, "a_ln_g": g_lng, "a_ln_b": g_lnb, "a_spatial_w": g_ws, "a_spatial_b": g_bst.T,
        "b_gk_bias": g_gkb, "b_out_norm": g_wn, "post_mix_norm": gg2, "pre_ffn_norm": gg3,
        "conv_b": gcb, "post_ffn_norm": gg4, "post_ple_norm": gg5,
        "b_gk": mm_tn(zl, dpre, "dw_gk")[:B_RANK], "conv_w": gcw,
        "loss": loss,
    }
    red_entries = SMALL + (("b_gk", B_RANK * 512), ("conv_w", 3 * 2 * D_FF), ("loss", 1))
    g_fin = _unpack_small(red_entries, allreduce_small(_pack_small(red_entries, lambda n: small_g[n])))
    g_fin["b_gk"] = lax.dynamic_slice(g_fin["b_gk"].reshape(B_RANK, 512), (0, me * B_HK), (B_RANK, B_HK))
    g_fin["conv_w"] = lax.dynamic_slice(g_fin["conv_w"].reshape(3, 2 * D_FF), (0, me * 1408), (3, 1408))
    upd_entries = SMALL + (("b_gk", B_RANK * B_HK), ("conv_w", 3 * 1408))
    res = adamw_small(*[_pack_small(upd_entries, get) for get in
                        (lambda n: g_fin[n], lambda n: args[n], lambda n: args["m_" + n], lambda n: args["v_" + n])])
    res = [_unpack_small(upd_entries, r_) for r_ in res]
    for n, _ in upd_entries:
        outs[n] = [r_[n].reshape(args[n].shape) for r_ in [g_fin] + res]

    return (g_fin["loss"].reshape(()), dx.reshape(x.shape), *[outs[n][0] for n in order], *[outs[n][1] for n in order],
            *[outs[n][2] for n in order], *[outs[n][3] for n in order])
```

```python
import functools
import math

import jax
import jax.numpy as jnp
from jax import lax
from jax.experimental import pallas as pl
from jax.experimental.pallas import tpu as pltpu

F32 = jnp.float32
BF = jnp.bfloat16
SDS = jax.ShapeDtypeStruct
MESH = pl.DeviceIdType.MESH

EPS = 1e-6
D = 1024
A_W = 512
A_G, A_C = 8, 128
A_GD = A_W // A_G
B_H, B_HK, B_HV = 4, 128, 256
B_C = 64
GLA_HPB = 4
B_RANK = 16
D_FF = 2816
PLE = 256
ZW = 6272
LANE = 128
VMEM_LIMIT = 60 * 1024 * 1024

ADAM_LR, ADAM_B1, ADAM_B2, ADAM_EPS, ADAM_WD, ADAM_STEP = 0.001, 0.9, 0.999, 1e-08, 0.01, 10

_GC = math.sqrt(2.0 / math.pi)
_GA = 0.044715

BIG = ("w_in", "a_out", "b_out", "w_mix_out", "w_up", "w_down", "w_ple", "w_ple_gate")
TINY = ("b_gk", "conv_w")
SMALL = (("pre_mix_norm", 1024), ("a_ln_g", 512), ("a_ln_b", 512), ("a_spatial_w", 131072),
         ("a_spatial_b", 1024), ("b_gk_bias", 512), ("b_out_norm", 256), ("post_mix_norm", 1024),
         ("pre_ffn_norm", 1024), ("conv_b", 5632), ("post_ffn_norm", 1024), ("post_ple_norm", 1024))


def _pc(body, **kw):
    return pl.pallas_call(body, **kw)


def _cp(n):
    return pltpu.CompilerParams(dimension_semantics=("arbitrary",) * n, vmem_limit_bytes=VMEM_LIMIT)


def _const(shape):
    nd = len(shape)
    return pl.BlockSpec(shape, lambda *_: (0,) * nd, pipeline_mode=pl.Buffered(1))


def _acc(shape):
    nd = len(shape)
    return pl.BlockSpec(shape, lambda *_: (0,) * nd)


def _dot(a, b):
    return jnp.dot(a, b, preferred_element_type=F32)


def _dot_nt(a, b):
    return lax.dot_general(a, b, (((1,), (1,)), ((), ())), preferred_element_type=F32)


def _dot_tn(a, b):
    return lax.dot_general(a, b, (((0,), (0,)), ((), ())), preferred_element_type=F32)


def _gelu(x):
    return 0.5 * x * (1.0 + jnp.tanh(_GC * (x + _GA * x * x * x)))


def _gelu_and_grad(x):
    x2 = x * x
    s = 0.5 * jnp.tanh((_GC * x) * (1.0 + _GA * x2)) + 0.5
    g = x * s
    return g, s + g * (1.0 - s) * ((6.0 * _GC * _GA) * x2 + 2.0 * _GC)


def _log_sigmoid(x):
    return jnp.minimum(x, 0.0) - jnp.log(1.0 + jnp.exp(-jnp.abs(x)))


def _rms(x, g):
    return x * lax.rsqrt(jnp.mean(x * x, axis=-1, keepdims=True) + EPS) * g


def _rms_bwd(dy, x, g):
    r = lax.rsqrt(jnp.mean(x * x, axis=-1, keepdims=True) + EPS)
    n = x * r
    dn = dy * g
    dx = r * (dn - n * jnp.mean(dn * n, axis=-1, keepdims=True))
    return dx, jnp.sum(dy * n, axis=0, keepdims=True)


def _ldot3(l, x):
    h = x.astype(BF)
    r = x - h.astype(F32)
    m = r.astype(BF)
    lo = (r - m.astype(F32)).astype(BF)
    return _dot(l, h) + _dot(l, m) + _dot(l, lo)


def _split_side(refs, n_in, n_out, n_scratch, side):
    si, so = (side.n_in, side.n_out) if side else (0, 0)
    cuts = [n_in, si, n_out, so, n_scratch]
    out, at = [], 0
    for c in cuts:
        out.append(refs[at:at + c])
        at += c
    return (*out, refs[at:])


def _side_specs(side):
    return ([_ANY] * side.n_in, [_ANY] * side.n_out, side.out_shapes, side.scratch, side.ins) if side else ([],) * 5


def norm_matmul(x, g, wt, bn, t, name, nblk, f32_blk=None, side=None):
    s, dm = x.shape
    w_spec = pl.BlockSpec((bn, dm), lambda i, j: (j, 0))
    extra = f32_blk is not None
    nt = s // t

    def body(*refs):
        (x_ref, g_ref, w_ref), s_in, outs, s_out, (a_sc,), s_scr = _split_side(refs, 3, 2 + extra, 1, side)
        a_ref, z_ref = outs[:2]
        i, j = pl.program_id(0), pl.program_id(1)
        if side:
            @pl.when((i == 0) & (j == 0))
            def _():
                side.start(s_in, s_out, *s_scr)

        @pl.when(j == 0)
        def _():
            a = _rms(x_ref[...], g_ref[...]).astype(BF)
            a_sc[...] = a
            a_ref[...] = a

        acc = _dot_nt(a_sc[...], w_ref[...])
        z_ref[...] = acc.astype(BF)
        if extra:
            @pl.when(j == f32_blk)
            def _():
                outs[2][...] = acc
        if side:
            @pl.when((i == nt - 1) & (j == nblk - 1))
            def _():
                side.finish(s_in, s_out, *s_scr)

    si_specs, so_specs, so_shapes, s_scratch, s_ins = _side_specs(side)
    return _pc(
        body, name=name, grid=(nt, nblk),
        in_specs=[pl.BlockSpec((t, dm), lambda i, j: (i, 0)), _const((1, dm)), w_spec] + si_specs,
        out_specs=[pl.BlockSpec((t, dm), lambda i, j: (i, 0)), pl.BlockSpec((t, bn), lambda i, j: (i, j))]
        + [pl.BlockSpec((t, bn), lambda i, j: (i, 0))] * extra + so_specs,
        out_shape=[SDS((s, dm), BF), SDS((s, nblk * bn), BF)] + [SDS((s, bn), F32)] * extra + so_shapes,
        scratch_shapes=[pltpu.VMEM((t, dm), BF)] + s_scratch, compiler_params=_cp(2))(x, g, wt, *s_ins)


def _sgu_weights(wc_ref, transposed):
    r = lax.broadcasted_iota(jnp.int32, (A_C, A_G * A_C), 0)
    c = lax.broadcasted_iota(jnp.int32, (A_C, A_G * A_C), 1) & (A_C - 1)
    return jnp.where((r <= c) if transposed else (c <= r), wc_ref[...], 0.0).astype(BF)


def _sgu_spread(xs, bdm):
    return jnp.concatenate([jnp.tile(x, (A_G, 1)) * bdm for x in xs], axis=1)


def _sgu_recompute(v, lng, lnb):
    gv, dgv = _gelu_and_grad(v)
    mu = jnp.mean(gv, axis=-1, keepdims=True)
    xc = gv - mu
    rstd = lax.rsqrt(jnp.mean(xc * xc, axis=-1, keepdims=True) + EPS)
    xhat = xc * rstd
    return dgv, rstd, xhat, (xhat * lng + lnb).astype(BF)


def sgu_fwd(z, ln_g, ln_b, w_cat, bias_full, bdm, t):
    s = z.shape[0]
    nch = t // A_C

    def body(u_ref, v_ref, g_ref, b_ref, wc_ref, bias_ref, bdm_ref, sa_ref):
        vns = [_sgu_recompute(v_ref[pl.ds(ci * A_C, A_C), :].astype(F32), g_ref[...], b_ref[...])[3]
               for ci in range(nch)]
        mixed = _dot(_sgu_weights(wc_ref, False), _sgu_spread(vns, bdm_ref[...]))
        for ci in range(nch):
            rows = pl.ds(ci * A_C, A_C)
            s_ = mixed[:, ci * A_W:(ci + 1) * A_W] + bias_ref[...]
            sa_ref[rows, :] = (_gelu(u_ref[rows, :].astype(F32)) * s_).astype(BF)

    return _pc(
        body, name="sgu_fwd", grid=(s // t,),
        in_specs=[pl.BlockSpec((t, A_W), lambda i: (i, 0)), pl.BlockSpec((t, A_W), lambda i: (i, 1)),
                  _const((1, A_W)), _const((1, A_W)), _const((A_C, A_G * A_C)), _const((A_C, A_W)),
                  _const((A_G * A_C, A_W))],
        out_specs=pl.BlockSpec((t, A_W), lambda i: (i, 0)),
        out_shape=SDS((s, A_W), BF), compiler_params=_cp(1))(z, z, ln_g, ln_b, w_cat, bias_full, bdm)


def _gla_decays(qk, lr, wgk, bias, l, t):
    nc = t // B_C
    q = qk[:, :B_HK].astype(F32) * (B_HK ** -0.5)
    k = qk[:, B_HK:].astype(F32)
    pre = _dot(lr, wgk) + bias
    la = _log_sigmoid(pre) * (1.0 / 16.0)
    b = _ldot3(l, la)
    b3 = b.reshape(nc, B_C, B_HK)
    bl = jnp.broadcast_to(b3[:, B_C - 1:B_C, :], (nc, B_C, B_HK)).reshape(t, B_HK)
    eb, enb, etb = jnp.exp(b), jnp.exp(-b), jnp.exp(bl - b)
    return pre, b, bl, eb, enb, etb, q * eb, k * enb, k * etb


def gla_fwd(z, qk32, zl, wgk, bias, wn, ltri, t):
    s = z.shape[0]
    nc = t // B_C
    hpb = GLA_HPB
    kw, vw = hpb * B_HK, hpb * B_HV

    def body(qk_ref, v_ref, og_ref, lr_ref, wgk_ref, bias_ref, wn_ref, l_ref, ob_ref, o_ref, st_ref, st_sc, o_sc):
        g = pl.program_id(1)

        @pl.when(pl.program_id(0) == 0)
        def _():
            for hh in range(hpb):
                st_sc[g * hpb + hh] = jnp.zeros((B_HV, B_HK), F32)

        lr, l = lr_ref[...], l_ref[...]
        for hh in range(hpb):
            h = g * hpb + hh
            cv, ck = slice(hh * B_HV, (hh + 1) * B_HV), slice(hh * B_HK, (hh + 1) * B_HK)
            _, _, bl, _, _, _, qd, ki, kt = _gla_decays(qk_ref[:, cv], lr, wgk_ref[:, ck], bias_ref[:, ck], l, t)
            qd, ki, kt = qd.astype(BF), ki.astype(BF), kt.astype(BF)
            vb = v_ref[:, cv]
            sc = jnp.where(l > 0, _dot_nt(qd, ki), 0.0).astype(BF)
            o_sc[hh] = _dot(sc, vb)
            for n in range(nc):
                rows = slice(n * B_C, (n + 1) * B_C)
                st = st_sc[h]
                stb = st.astype(BF)
                st_ref[n, hh] = stb
                o_sc[hh, rows, :] += _dot_nt(qd[rows], stb)
                st_sc[h] = st * jnp.exp(bl[n * B_C:n * B_C + 1, :]) + _dot_tn(vb[rows], kt[rows])
            ob = o_sc[hh].astype(BF)
            o_ref[:, cv] = ob
            og = og_ref[:, cv].astype(F32)
            ob_ref[:, cv] = (_rms(ob.astype(F32), wn_ref[...]) * og * jax.nn.sigmoid(og)).astype(BF)

    vo, go = 2048 // vw, 3072 // vw
    return _pc(
        body, name="gla_fwd", grid=(s // t, B_H // hpb),
        in_specs=[pl.BlockSpec((t, vw), lambda i, g: (i, g)), pl.BlockSpec((t, vw), lambda i, g: (i, vo + g)),
                  pl.BlockSpec((t, vw), lambda i, g: (i, go + g)), pl.BlockSpec((t, LANE), lambda i, g: (i, 0)),
                  pl.BlockSpec((LANE, kw), lambda i, g: (0, g)), pl.BlockSpec((1, kw), lambda i, g: (0, g)),
                  _const((1, B_HV)), _const((t, t))],
        out_specs=[pl.BlockSpec((t, vw), lambda i, g: (i, g)), pl.BlockSpec((t, vw), lambda i, g: (i, g)),
                   pl.BlockSpec((nc, hpb, B_HV, B_HK), lambda i, g: (i, g, 0, 0))],
        out_shape=[SDS((s, D), BF), SDS((s, D), BF), SDS((s // B_C, B_H, B_HV, B_HK), BF)],
        scratch_shapes=[pltpu.VMEM((B_H, B_HV, B_HK), F32), pltpu.VMEM((hpb, t, B_HV), F32)],
        compiler_params=_cp(2))(qk32, z, z, zl, wgk, bias, wn, ltri)


def mix_fwd(sa, ob, z, x, a_out, b_out, w_mix, g2, t):
    s = x.shape[0]

    def body(sa_ref, ob_ref, ga_ref, gb_ref, x_ref, ao_ref, bo_ref, wm_ref, g2_ref,
             ya_ref, yb_ref, mp_ref, mx_ref, h1_ref):
        ya = _dot(sa_ref[...], ao_ref[...]).astype(BF)
        yb = _dot(ob_ref[...], bo_ref[...]).astype(BF)
        ya_ref[...] = ya
        yb_ref[...] = yb
        mp = (jax.nn.sigmoid(ga_ref[...].astype(F32)) * ya.astype(F32)
              + jax.nn.sigmoid(gb_ref[...].astype(F32)) * yb.astype(F32)).astype(BF)
        mp_ref[...] = mp
        mx = _dot(mp, wm_ref[...]).astype(BF)
        mx_ref[...] = mx
        h1_ref[...] = x_ref[...] + _rms(mx.astype(F32), g2_ref[...])

    row = lambda w: pl.BlockSpec((t, w), lambda i: (i, 0))
    return _pc(
        body, name="mix_fwd", grid=(s // t,),
        in_specs=[row(A_W), row(D), pl.BlockSpec((t, D), lambda i: (i, 4)), pl.BlockSpec((t, D), lambda i: (i, 5)),
                  row(D), _const((A_W, D)), _const((D, D)), _const((D, D)), _const((1, D))],
        out_specs=[row(D)] * 5,
        out_shape=[SDS((s, D), BF)] * 4 + [SDS((s, D), F32)],
        compiler_params=_cp(1))(sa, ob, z, z, x, a_out, b_out, w_mix, g2)


def ffn_up_fwd(h1, g3, w_up3, conv_w, conv_b, t):
    s = h1.shape[0]
    bn = w_up3.shape[2]

    def body(x_ref, g_ref, wg_ref, wv_ref, cwg_ref, cwv_ref, cbg_ref, cbv_ref,
             c_ref, ug_ref, uv_ref, cg_ref, cv_ref, ff_ref, c_sc, carry):
        i, j = pl.program_id(0), pl.program_id(1)

        @pl.when(j == 0)
        def _():
            c = _rms(x_ref[...], g_ref[...]).astype(BF)
            c_sc[...] = c
            c_ref[...] = c

        @pl.when(i == 0)
        def _():
            carry[j] = jnp.zeros((2, 8, bn), F32)

        def branch(k, w_ref, cw_ref, cb_ref, u_ref, o_ref):
            ub = _dot(c_sc[...], w_ref[...]).astype(BF)
            u_ref[...] = ub
            u = ub.astype(F32)
            ext = jnp.concatenate([carry[j, k], u], axis=0)
            carry[j, k] = u[t - 8:]
            w = cw_ref[...]
            cc = (cb_ref[...] + w[0:1] * pltpu.roll(ext, 2, 0) + w[1:2] * pltpu.roll(ext, 1, 0) + w[2:3] * ext)[8:]
            cc = cc.astype(BF)
            o_ref[...] = cc
            return cc.astype(F32)

        g = _gelu(branch(0, wg_ref, cwg_ref, cbg_ref, ug_ref, cg_ref))
        ff_ref[...] = (g * branch(1, wv_ref, cwv_ref, cbv_ref, uv_ref, cv_ref)).astype(BF)

    col = lambda rows, off: pl.BlockSpec((rows, bn), lambda i, j: (0, j + off))
    out = pl.BlockSpec((t, bn), lambda i, j: (i, j))
    return _pc(
        body, name="ffn_up_fwd", grid=(s // t, 2),
        in_specs=[pl.BlockSpec((t, D), lambda i, j: (i, 0)), _const((1, D)),
                  pl.BlockSpec((None, D, bn), lambda i, j: (j, 0, 0)), pl.BlockSpec((None, D, bn), lambda i, j: (j + 2, 0, 0)),
                  col(3, 0), col(3, 2), col(1, 0), col(1, 2)],
        out_specs=[pl.BlockSpec((t, D), lambda i, j: (i, 0))] + [out] * 5,
        out_shape=[SDS((s, D), BF)] + [SDS((s, D_FF), BF)] * 5,
        scratch_shapes=[pltpu.VMEM((t, D), BF), pltpu.VMEM((2, 2, 8, bn), F32)],
        compiler_params=_cp(2))(h1, g3, w_up3, w_up3, conv_w, conv_w, conv_b, conv_b)


def ffn_up_bwd(ug, uv, cg, cv, dff, conv_w, w_up, h1, g3, dh2, t):
    s = h1.shape[0]
    nt = s // t
    hb = t // 8
    bn = 1408
    r = t + 8

    def body(ug_ref, uv_ref, cg_ref, cv_ref, cag_ref, cav_ref, d_ref, da_ref, cw_ref, w_ref, x_ref, g_ref, dres_ref,
             dug_ref, duv_ref, gw_ref, gb_ref, dx_ref, gg_ref):
        i = pl.program_id(0)

        @pl.when(i == 0)
        def _():
            gw_ref[...] = jnp.zeros((3, 2 * D_FF), F32)
            gb_ref[...] = jnp.zeros((1, 2 * D_FF), F32)
            gg_ref[...] = jnp.zeros((1, D), F32)

        more = (i < nt - 1).astype(F32)

        def gate(c_g, c_v, d_):
            gl, dgl = _gelu_and_grad(c_g.astype(F32))
            return d_ * c_v.astype(F32) * dgl, d_ * gl

        def back(dc, dc_next, u_ref, cols, off, du_ref):
            w = cw_ref[:, off:off + bn]
            d_ext = jnp.concatenate([dc, dc_next], axis=0)
            d1, d2 = pltpu.roll(d_ext, r - 1, 0)[:t], pltpu.roll(d_ext, r - 2, 0)[:t]
            du_ref[:, cols] = (w[2:3] * dc + w[1:2] * d1 + w[0:1] * d2).astype(BF)
            u = u_ref[:, cols].astype(F32)
            gw_ref[0:1, off:off + bn] += jnp.sum(d2 * u, axis=0, keepdims=True)
            gw_ref[1:2, off:off + bn] += jnp.sum(d1 * u, axis=0, keepdims=True)
            gw_ref[2:3, off:off + bn] += jnp.sum(dc * u, axis=0, keepdims=True)
            gb_ref[:, off:off + bn] += jnp.sum(dc, axis=0, keepdims=True)

        for kb in range(D_FF // bn):
            cols = slice(kb * bn, (kb + 1) * bn)
            dg, dv = gate(cg_ref[:, cols], cv_ref[:, cols], d_ref[:, cols].astype(F32))
            dg_n, dv_n = gate(cag_ref[:, cols], cav_ref[:, cols], da_ref[:, cols].astype(F32) * more)
            back(dg, dg_n, ug_ref, cols, kb * bn, dug_ref)
            back(dv, dv_n, uv_ref, cols, D_FF + kb * bn, duv_ref)

        acc = _dot_nt(dug_ref[...], w_ref[:, :D_FF]) + _dot_nt(duv_ref[...], w_ref[:, D_FF:])
        dxn, dg3 = _rms_bwd(acc, x_ref[...], g_ref[...])
        dx_ref[...] = dres_ref[...] + dxn
        gg_ref[...] += dg3

    tile = lambda width: pl.BlockSpec((t, width), lambda i: (i, 0))
    after = pl.BlockSpec((8, D_FF), lambda i: (jnp.minimum((i + 1) * hb, nt * hb - 1), 0))
    return _pc(
        body, name="ffn_up_bwd", grid=(nt,),
        in_specs=[tile(D_FF)] * 4 + [after, after, tile(D_FF), after, _const((3, 2 * D_FF)), _const((D, 2 * D_FF)),
                                     tile(D), _const((1, D)), tile(D)],
        out_specs=[tile(D_FF), tile(D_FF), _acc((3, 2 * D_FF)), _acc((1, 2 * D_FF)), tile(D), _acc((1, D))],
        out_shape=[SDS((s, D_FF), BF), SDS((s, D_FF), BF), SDS((3, 2 * D_FF), F32), SDS((1, 2 * D_FF), F32),
                   SDS((s, D), F32), SDS((1, D), F32)],
        compiler_params=_cp(1))(ug, uv, cg, cv, cg, cv, dff, dff, conv_w, w_up, h1, g3, dh2)


def out_fwd(ff, h1, p, tgt, w_down, w_pg, w_ple, g4, g5, t):
    s = h1.shape[0]

    def body(ff_ref, h1_ref, p_ref, t_ref, wd_ref, wpg_ref, wpl_ref, g4_ref, g5_ref,
             f_ref, h2_ref, pg_ref, pe_ref, dy_ref, loss_ref):
        @pl.when(pl.program_id(0) == 0)
        def _():
            loss_ref[...] = jnp.zeros((1, 1), F32)

        f = _dot(ff_ref[...], wd_ref[...]).astype(BF)
        f_ref[...] = f
        h2 = h1_ref[...] + _rms(f.astype(F32), g4_ref[...])
        h2b = h2.astype(BF)
        h2_ref[...] = h2b
        pg = _dot(h2b, wpg_ref[...]).astype(BF)
        pe = _dot(p_ref[...].astype(BF), wpl_ref[...]).astype(BF)
        pg_ref[...] = pg
        pe_ref[...] = pe
        y = h2 + _rms(jax.nn.sigmoid(pg.astype(F32)) * pe.astype(F32), g5_ref[...])
        err = y - t_ref[...]
        dy_ref[...] = err * (1.0 / D)
        loss_ref[...] += (0.5 / D) * jnp.sum(err * err)

    row = lambda w: pl.BlockSpec((t, w), lambda i: (i, 0))
    return _pc(
        body, name="out_fwd", grid=(s // t,),
        in_specs=[row(D_FF), row(D), row(PLE), row(D), _const((D_FF, D)), _const((D, D)), _const((PLE, D)),
                  _const((1, D)), _const((1, D))],
        out_specs=[row(D)] * 5 + [_acc((1, 1))],
        out_shape=[SDS((s, D), BF)] * 4 + [SDS((s, D), F32), SDS((1, 1), F32)],
        compiler_params=_cp(1))(ff, h1, p, tgt, w_down, w_pg, w_ple, g4, g5)


def out_bwd(dy, pg, pe, f, g5, g4, w_pg, w_down, t):
    s = dy.shape[0]

    def body(dy_ref, pg_ref, pe_ref, f_ref, g5_ref, g4_ref, wpg_ref, wd_ref,
             dh2_ref, dpe_ref, dpg_ref, df_ref, dff_ref, gg5_ref, gg4_ref):
        @pl.when(pl.program_id(0) == 0)
        def _():
            gg5_ref[...] = jnp.zeros((1, D), F32)
            gg4_ref[...] = jnp.zeros((1, D), F32)

        dy_ = dy_ref[...]
        pg_ = pg_ref[...].astype(F32)
        pe_ = pe_ref[...].astype(F32)
        sg = jax.nn.sigmoid(pg_)
        dple, dg5 = _rms_bwd(dy_, sg * pe_, g5_ref[...])
        gg5_ref[...] += dg5
        dpe_ref[...] = (dple * sg).astype(BF)
        dpg = (dple * pe_ * sg * (1.0 - sg)).astype(BF)
        dpg_ref[...] = dpg
        dh2 = dy_ + _dot_nt(dpg, wpg_ref[...])
        dh2_ref[...] = dh2
        df, dg4 = _rms_bwd(dh2, f_ref[...].astype(F32), g4_ref[...])
        gg4_ref[...] += dg4
        dfb = df.astype(BF)
        df_ref[...] = dfb
        dff_ref[...] = _dot_nt(dfb, wd_ref[...]).astype(BF)

    row = lambda w: pl.BlockSpec((t, w), lambda i: (i, 0))
    return _pc(
        body, name="out_bwd", grid=(s // t,),
        in_specs=[row(D), row(D), row(D), row(D), _const((1, D)), _const((1, D)), _const((D, D)), _const((D_FF, D))],
        out_specs=[row(D), row(D), row(D), row(D), row(D_FF), _acc((1, D)), _acc((1, D))],
        out_shape=[SDS((s, D), F32), SDS((s, D), BF), SDS((s, D), BF), SDS((s, D), BF), SDS((s, D_FF), BF),
                   SDS((1, D), F32), SDS((1, D), F32)],
        compiler_params=_cp(1))(dy, pg, pe, f, g5, g4, w_pg, w_down)


def nt_normbwd(dys, w, xin, gain, dres, t, name, side=None):
    s = xin.shape[0]
    nt = s // t
    np_ = len(dys)

    def body(*refs):
        ins_, s_in, (dx_ref, gg_ref), s_out, _, s_scr = _split_side(refs, np_ + 4, 2, 0, side)
        dy_refs = ins_[:np_]
        w_ref, x_ref, g_ref, dres_ref = ins_[np_:]
        i = pl.program_id(0)

        @pl.when(i == 0)
        def _():
            gg_ref[...] = jnp.zeros((1, D), F32)
            if side:
                side.start(s_in, s_out, *s_scr)

        acc = _dot(jnp.concatenate([r_[...] for r_ in dy_refs], axis=1), w_ref[...])
        dxn, dg = _rms_bwd(acc, x_ref[...], g_ref[...])
        dx_ref[...] = dres_ref[...] + dxn
        gg_ref[...] += dg
        if side:
            @pl.when(i == nt - 1)
            def _():
                side.finish(s_in, s_out, *s_scr)

    row = lambda width: pl.BlockSpec((t, width), lambda i: (i, 0))
    si_specs, so_specs, so_shapes, s_scratch, s_ins = _side_specs(side)
    assert sum(dy.shape[1] for dy in dys) == w.shape[0]
    return _pc(
        body, name=name, grid=(nt,),
        in_specs=[row(dy.shape[1]) for dy in dys] + [_const(w.shape), row(D), _const((1, D)), row(D)] + si_specs,
        out_specs=[row(D), _acc((1, D))] + so_specs,
        out_shape=[SDS((s, D), F32), SDS((1, D), F32)] + so_shapes, scratch_shapes=s_scratch,
        compiler_params=_cp(1))(*dys, w, xin, gain, dres, *s_ins)


def mix_bwd(dh1, mx, z, ya, yb, g2, w_mix, a_out, b_out, t):
    s = dh1.shape[0]

    def body(dh_ref, mx_ref, ga_ref, gb_ref, ya_ref, yb_ref, g2_ref, wm_ref, ao_ref, bo_ref,
             dmx_ref, dya_ref, dyb_ref, dga_ref, dgb_ref, dsa_ref, dob_ref, gg2_ref):
        @pl.when(pl.program_id(0) == 0)
        def _():
            gg2_ref[...] = jnp.zeros((1, D), F32)

        dmx, dg2 = _rms_bwd(dh_ref[...], mx_ref[...].astype(F32), g2_ref[...])
        gg2_ref[...] += dg2
        dmxb = dmx.astype(BF)
        dmx_ref[...] = dmxb
        dmp = _dot_nt(dmxb, wm_ref[...])

        def gate(g_ref, y_ref, dy_ref, dg_ref, w_ref, dz_ref):
            sg = jax.nn.sigmoid(g_ref[...].astype(F32))
            dyb_ = (dmp * sg).astype(BF)
            dy_ref[...] = dyb_
            dg_ref[...] = (dmp * y_ref[...].astype(F32) * sg * (1.0 - sg)).astype(BF)
            dz_ref[...] = _dot_nt(dyb_, w_ref[...]).astype(BF)

        gate(ga_ref, ya_ref, dya_ref, dga_ref, ao_ref, dsa_ref)
        gate(gb_ref, yb_ref, dyb_ref, dgb_ref, bo_ref, dob_ref)

    row = lambda w: pl.BlockSpec((t, w), lambda i: (i, 0))
    return _pc(
        body, name="mix_bwd", grid=(s // t,),
        in_specs=[row(D), row(D), pl.BlockSpec((t, D), lambda i: (i, 4)), pl.BlockSpec((t, D), lambda i: (i, 5)),
                  row(D), row(D), _const((1, D)), _const((D, D)), _const((A_W, D)), _const((D, D))],
        out_specs=[row(D)] * 5 + [row(A_W), row(D), _acc((1, D))],
        out_shape=[SDS((s, D), BF)] * 5 + [SDS((s, A_W), BF), SDS((s, D), BF), SDS((1, D), F32)],
        compiler_params=_cp(1))(dh1, mx, z, z, ya, yb, g2, w_mix, a_out, b_out)


def sgu_bwd(z, dsa, ln_g, ln_b, w_cat, w_cat_t, bias_full, bdm, t):
    s = z.shape[0]
    nt = s // t
    nch = t // A_C

    def body(u_ref, v_ref, dsa_ref, g_ref, b_ref, wc_ref, wct_ref, bias_ref, bdm_ref,
             duv_ref, glg_ref, glb_ref, gws_ref, gbs_ref, ds_acc):
        i = pl.program_id(0)

        @pl.when(i == 0)
        def _():
            glg_ref[...] = jnp.zeros((1, A_W), F32)
            glb_ref[...] = jnp.zeros((1, A_W), F32)
            gws_ref[...] = jnp.zeros((A_C, A_G * A_C), F32)
            ds_acc[...] = jnp.zeros((A_C, A_W), F32)

        lng, bdm_ = g_ref[...], bdm_ref[...]
        rec = [_sgu_recompute(v_ref[pl.ds(ci * A_C, A_C), :].astype(F32), lng, b_ref[...]) for ci in range(nch)]
        spread_vn = _sgu_spread([r_[3] for r_ in rec], bdm_)
        mixed = _dot(_sgu_weights(wc_ref, False), spread_vn)
        dsas, dss, dgus = [], [], []
        for ci in range(nch):
            rows = pl.ds(ci * A_C, A_C)
            gu, dgu = _gelu_and_grad(u_ref[rows, :].astype(F32))
            dsa_ = dsa_ref[rows, :].astype(F32)
            ds = dsa_ * gu
            ds_acc[...] += ds
            dsas.append(dsa_)
            dgus.append(dgu)
            dss.append(ds.astype(BF))
        r = lax.broadcasted_iota(jnp.int32, (A_C, A_G * A_C), 0)
        c = lax.broadcasted_iota(jnp.int32, (A_C, A_G * A_C), 1) & (A_C - 1)
        gws_ref[...] += jnp.where(c <= r, _dot_nt(jnp.concatenate(dss, axis=1), spread_vn), 0.0)
        dvns = _dot(_sgu_weights(wct_ref, True), _sgu_spread(dss, bdm_))
        for ci in range(nch):
            rows = pl.ds(ci * A_C, A_C)
            dgv, rstd, xhat, _ = rec[ci]
            dvn = dvns[:, ci * A_W:(ci + 1) * A_W]
            glb_ref[...] += jnp.sum(dvn, axis=0, keepdims=True)
            glg_ref[...] += jnp.sum(dvn * xhat, axis=0, keepdims=True)
            dxh = dvn * lng
            dgv_ = rstd * (dxh - jnp.mean(dxh, axis=-1, keepdims=True)
                           - xhat * jnp.mean(dxh * xhat, axis=-1, keepdims=True))
            s_ = mixed[:, ci * A_W:(ci + 1) * A_W] + bias_ref[...]
            duv_ref[rows, :A_W] = (dsas[ci] * s_ * dgus[ci]).astype(BF)
            duv_ref[rows, A_W:] = (dgv_ * dgv).astype(BF)

        @pl.when(i == nt - 1)
        def _():
            acc = ds_acc[...]
            for g in range(A_G):
                gbs_ref[:, g:g + 1] = jnp.sum(acc[:, g * A_GD:(g + 1) * A_GD], axis=1, keepdims=True)

    return _pc(
        body, name="sgu_bwd", grid=(nt,),
        in_specs=[pl.BlockSpec((t, A_W), lambda i: (i, 0)), pl.BlockSpec((t, A_W), lambda i: (i, 1)),
                  pl.BlockSpec((t, A_W), lambda i: (i, 0)),
                  _const((1, A_W)), _const((1, A_W)), _const((A_C, A_G * A_C)), _const((A_C, A_G * A_C)),
                  _const((A_C, A_W)), _const((A_G * A_C, A_W))],
        out_specs=[pl.BlockSpec((t, D), lambda i: (i, 0)), _acc((1, A_W)), _acc((1, A_W)),
                   _acc((A_C, A_G * A_C)), _acc((A_C, A_G))],
        out_shape=[SDS((s, D), BF), SDS((1, A_W), F32), SDS((1, A_W), F32), SDS((A_C, A_G * A_C), F32),
                   SDS((A_C, A_G), F32)],
        scratch_shapes=[pltpu.VMEM((A_C, A_W), F32)],
        compiler_params=_cp(1))(z, z, dsa, ln_g, ln_b, w_cat, w_cat_t, bias_full, bdm)


def gla_bwd(z, qk32, zl, o, dob, states, wgk, bias, wn, ltri, ltri_t, t, side=None):
    s = z.shape[0]
    nt = s // t
    nc = t // B_C
    hpb = GLA_HPB
    kw, vw = hpb * B_HK, hpb * B_HV

    def body(*refs):
        ins_, s_in, outs_, s_out, scr_, s_scr = _split_side(refs, 12, 6, 5, side)
        qk_ref, v_ref, og_ref, lr_ref, o_ref, dob_ref, st_ref, wgk_ref, bias_ref, wn_ref, l_ref, lt_ref = ins_
        dqk_ref, dv_ref, dog_ref, dpre_ref, gbias_ref, gwn_ref = outs_
        dst_sc, dv_sc, dqd_sc, dkt_sc, ddec_sc = scr_
        i = pl.program_id(0)
        g = pl.program_id(1)

        @pl.when((i == 0) & (g == 0))
        def _():
            gbias_ref[...] = jnp.zeros((B_H, 1, B_HK), F32)
            gwn_ref[...] = jnp.zeros((1, B_HV), F32)
            if side:
                side.start(s_in, s_out, *s_scr)

        @pl.when(i == 0)
        def _():
            for hh in range(hpb):
                dst_sc[g * hpb + hh] = jnp.zeros((B_HV, B_HK), F32)

        lr, l, lt = lr_ref[...], l_ref[...], lt_ref[...]
        keep, keep_t = l > 0, lt > 0
        wn_ = wn_ref[...]
        last = lax.broadcasted_iota(jnp.int32, (nc, B_C, B_HK), 1) == B_C - 1
        for hh in range(hpb):
            h = g * hpb + hh
            cv, ck = slice(hh * B_HV, (hh + 1) * B_HV), slice(hh * B_HK, (hh + 1) * B_HK)
            pre, b, bl, eb, enb, etb, qd, ki, kt = _gla_decays(qk_ref[:, cv], lr, wgk_ref[:, ck], bias_ref[:, ck], l, t)
            qdb, kib, ktb = qd.astype(BF), ki.astype(BF), kt.astype(BF)
            vb = v_ref[:, cv]
            o_ = o_ref[:, cv].astype(F32)
            og = og_ref[:, cv].astype(F32)
            sog = jax.nn.sigmoid(og)
            dob_ = dob_ref[:, cv].astype(F32)
            don = dob_ * og * sog
            do, dwn = _rms_bwd(don, o_, wn_)
            gwn_ref[...] += dwn
            dog_ref[:, cv] = (dob_ * _rms(o_, wn_) * sog * (1.0 + og * (1.0 - sog))).astype(BF)
            dob16 = do.astype(BF)
            sc_t = jnp.where(keep_t, _dot_nt(kib, qdb), 0.0).astype(BF)
            dsc = jnp.where(keep, _dot_nt(dob16, vb), 0.0).astype(BF)
            dsc_t = jnp.where(keep_t, _dot_nt(vb, dob16), 0.0).astype(BF)
            dv_sc[hh] = _dot(sc_t, dob16)
            dqd_sc[hh] = _dot(dsc, kib)
            dki = _dot(dsc_t, qdb)
            for n in reversed(range(nc)):
                rows = slice(n * B_C, (n + 1) * B_C)
                dst = dst_sc[h]
                dstb = dst.astype(BF)
                stp = st_ref[n, hh]
                dv_sc[hh, rows, :] += _dot_nt(ktb[rows], dstb)
                dkt_sc[hh, rows, :] = _dot(vb[rows], dstb)
                dqd_sc[hh, rows, :] += _dot(dob16[rows], stp)
                dec = jnp.exp(bl[n * B_C:n * B_C + 1, :])
                ddec_sc[hh, n] = jnp.sum(dst * stp.astype(F32), axis=0, keepdims=True) * dec
                dst_sc[h] = dst * dec + _dot_tn(dob16[rows], qdb[rows])
            dqd, dkt = dqd_sc[hh], dkt_sc[hh]
            dv_ref[:, cv] = dv_sc[hh].astype(BF)
            dqk_ref[:, hh * B_HV:hh * B_HV + B_HK] = (dqd * eb * (B_HK ** -0.5)).astype(BF)
            dqk_ref[:, hh * B_HV + B_HK:(hh + 1) * B_HV] = (dki * enb + dkt * etb).astype(BF)
            dktkt = dkt * kt
            db3 = (dqd * qd - dki * ki - dktkt).reshape(nc, B_C, B_HK)
            dbl = jnp.sum(dktkt.reshape(nc, B_C, B_HK), axis=1, keepdims=True) + ddec_sc[hh]
            db = (db3 + jnp.where(last, dbl, 0.0)).reshape(t, B_HK)
            dla = _ldot3(lt, db)
            dpre = dla * (1.0 / 16.0) * (1.0 - jax.nn.sigmoid(pre))
            dpre_ref[:, ck] = dpre.astype(BF)
            gbias_ref[h] += jnp.sum(dpre, axis=0, keepdims=True)
        if side:
            @pl.when((i == nt - 1) & (g == B_H // hpb - 1))
            def _():
                side.finish(s_in, s_out, *s_scr)

    rv = lambda i: nt - 1 - i
    si_specs, so_specs, so_shapes, s_scratch, s_ins = _side_specs(side)
    vo, go = 2048 // vw, 3072 // vw
    tile = lambda off: pl.BlockSpec((t, vw), lambda i, g: (rv(i), off + g))
    return _pc(
        body, name="gla_bwd", grid=(nt, B_H // hpb),
        in_specs=[tile(0), tile(vo), tile(go), pl.BlockSpec((t, LANE), lambda i, g: (rv(i), 0)), tile(0), tile(0),
                  pl.BlockSpec((nc, hpb, B_HV, B_HK), lambda i, g: (rv(i), g, 0, 0)),
                  pl.BlockSpec((LANE, kw), lambda i, g: (0, g)), pl.BlockSpec((1, kw), lambda i, g: (0, g)),
                  _const((1, B_HV)), _const((t, t)), _const((t, t))] + si_specs,
        out_specs=[tile(0), tile(0), tile(0), pl.BlockSpec((t, kw), lambda i, g: (rv(i), g)),
                   _acc((B_H, 1, B_HK)), _acc((1, B_HV))] + so_specs,
        out_shape=[SDS((s, D), BF), SDS((s, D), BF), SDS((s, D), BF), SDS((s, B_H * B_HK), BF),
                   SDS((B_H, 1, B_HK), F32), SDS((1, B_HV), F32)] + so_shapes,
        scratch_shapes=[pltpu.VMEM((B_H, B_HV, B_HK), F32), pltpu.VMEM((hpb, t, B_HV), F32),
                        pltpu.VMEM((hpb, t, B_HK), F32), pltpu.VMEM((hpb, t, B_HK), F32),
                        pltpu.VMEM((hpb, nc, 1, B_HK), F32)] + s_scratch,
        compiler_params=_cp(2))(qk32, z, z, zl, o, dob, states, wgk, bias, wn, ltri, ltri_t, *s_ins)


def mm_tn(a, b, name, tk=2048):
    s, m = a.shape
    n = b.shape[1]
    bn = next(c for c in (1024, 1408, 512, 256, 128) if n % c == 0 and m * c * 4 <= 6 * 1024 * 1024)
    tk = min(tk, s)
    nk = s // tk

    def body(a_ref, b_ref, o_ref, acc):
        k = pl.program_id(1)

        @pl.when(k == 0)
        def _():
            acc[...] = jnp.zeros((m, bn), F32)

        acc[...] += _dot_tn(a_ref[...].astype(BF), b_ref[...])

        @pl.when(k == nk - 1)
        def _():
            o_ref[...] = acc[...].astype(BF)

    return _pc(
        body, name=name, grid=(n // bn, nk),
        in_specs=[pl.BlockSpec((tk, m), lambda j, k: (k, 0)), pl.BlockSpec((tk, bn), lambda j, k: (k, j))],
        out_specs=pl.BlockSpec((m, bn), lambda j, k: (0, j)),
        out_shape=SDS((m, n), BF), scratch_shapes=[pltpu.VMEM((m, bn), F32)], compiler_params=_cp(2))(a, b)


def mm_cols(a, wt, blk, width, t, name):
    s, k = a.shape

    def body(a_ref, w_ref, o_ref):
        o_ref[...] = _dot_nt(a_ref[...], w_ref[...]).astype(BF)

    return _pc(body, name=name, grid=(s // t,),
               in_specs=[pl.BlockSpec((t, k), lambda i: (i, 0)),
                         pl.BlockSpec((width, k), lambda i: (blk, 0), pipeline_mode=pl.Buffered(1))],
               out_specs=pl.BlockSpec((t, width), lambda i: (i, 0)), out_shape=SDS((s, width), BF),
               compiler_params=_cp(1))(a, wt)


def mm_nt_small(a, w, t, name):
    s, k = a.shape
    n = w.shape[0]

    def body(a_ref, w_ref, o_ref):
        o_ref[...] = _dot_nt(a_ref[...], w_ref[...]).astype(BF)

    return _pc(body, name=name, grid=(s // t,),
               in_specs=[pl.BlockSpec((t, k), lambda i: (i, 0)), _const((n, k))],
               out_specs=pl.BlockSpec((t, n), lambda i: (i, 0)), out_shape=SDS((s, n), BF),
               compiler_params=_cp(1))(a, w)


def _adamw(w, g, m, v):
    m = ADAM_B1 * m + (1.0 - ADAM_B1) * g
    v = ADAM_B2 * v + (1.0 - ADAM_B2) * (g * g)
    m_hat = m / (1.0 - ADAM_B1 ** ADAM_STEP)
    v_hat = v / (1.0 - ADAM_B2 ** ADAM_STEP)
    return -ADAM_LR * (m_hat / (jnp.sqrt(v_hat) + ADAM_EPS) + ADAM_WD * w), m, v


def _half_rows(rows):
    rh = rows // 2
    return rh, max(b for b in range(16, 257, 16) if rh % b == 0)


def _pc_sp(body, grid, in_specs, out_specs, out_shape, name):
    gs = pltpu.PrefetchScalarGridSpec(num_scalar_prefetch=1, grid=grid, in_specs=in_specs, out_specs=out_specs)
    return _pc(body, grid_spec=gs, out_shape=out_shape, name=name, compiler_params=_cp(len(grid)))


def adamw_halves(sc, own, sib, w, m, v, name):
    rows, cols = w.shape
    rh, br = _half_rows(rows)
    nbk = rh // br

    def body(sc_ref, own_ref, sib_ref, w_ref, m_ref, v_ref, go_ref, d_ref, mo_ref, vo_ref):
        g_ = jnp.where(pl.program_id(0) // nbk == sc_ref[0], own_ref[...], sib_ref[...])
        go_ref[...] = g_
        d_ref[...], mo_ref[...], vo_ref[...] = _adamw(w_ref[...], g_, m_ref[...], v_ref[...])

    half = pl.BlockSpec((br, cols), lambda i, sc_: (i % nbk, 0))
    blk = pl.BlockSpec((br, cols), lambda i, sc_: (i, 0))
    return _pc_sp(body, (2 * nbk,), [half, half, blk, blk, blk], [blk] * 4, [SDS((rows, cols), F32)] * 4,
                  name)(sc, own, sib, w, m, v)


def adamw_cols(sc, own, sib, w, m, v, name, cb=256):
    rows, cols = w.shape
    nk = cols // 2 // cb

    def body(sc_ref, own_ref, sib_ref, w_ref, m_ref, v_ref, go_ref, d_ref, mo_ref, vo_ref):
        g_ = jnp.where(pl.program_id(0) == sc_ref[0], own_ref[...], sib_ref[...])
        go_ref[...] = g_
        d_ref[...], mo_ref[...], vo_ref[...] = _adamw(w_ref[...], g_, m_ref[...], v_ref[...])

    half = pl.BlockSpec((rows, cb), lambda h, k, sc_: (0, k))
    blk = pl.BlockSpec((rows, cb), lambda h, k, sc_: (0, h * nk + k))
    return _pc_sp(body, (2, nk), [half, half, blk, blk, blk], [blk] * 4, [SDS((rows, cols), F32)] * 4,
                  name)(sc, own, sib, w, m, v)


def adamw_small(g, w, m, v):
    def body(g_ref, w_ref, m_ref, v_ref, d_ref, mo_ref, vo_ref):
        d_ref[...], mo_ref[...], vo_ref[...] = _adamw(w_ref[...], g_ref[...], m_ref[...], v_ref[...])

    vm = pl.BlockSpec(memory_space=pltpu.VMEM)
    return _pc(body, name="adamw_small", in_specs=[vm] * 4, out_specs=[vm] * 3, out_shape=[SDS(g.shape, F32)] * 3,
               compiler_params=pltpu.CompilerParams(vmem_limit_bytes=VMEM_LIMIT))(g, w, m, v)


def _pos():
    return lax.axis_index("x"), lax.axis_index("y"), lax.axis_index("c")


def _other_chips(x, y):
    return [(1 - x, y), (x, 1 - y), (1 - x, 1 - y)]


_ANY = pl.BlockSpec(memory_space=pltpu.HBM)


class _Side:
    def __init__(self, ins, out_shapes, nsem, start, finish):
        self.ins, self.out_shapes, self.start, self.finish = list(ins), list(out_shapes), start, finish
        self.scratch = [pltpu.SemaphoreType.DMA((nsem,)), pltpu.SemaphoreType.DMA((nsem,))]
        self.n_in, self.n_out = len(self.ins), len(self.out_shapes)


def _run_side(side, name):
    def body(*refs):
        args_ = (refs[:side.n_in], refs[side.n_in:side.n_in + side.n_out], *refs[side.n_in + side.n_out:])
        side.start(*args_)
        side.finish(*args_)

    return _pc(body, name=name, in_specs=[_ANY] * side.n_in, out_specs=[_ANY] * side.n_out,
               out_shape=side.out_shapes, scratch_shapes=side.scratch)(*side.ins)


def _split_rows(shape):
    return (shape[0] // 2) % 16 == 0


def _core_halves(shape, c):
    if _split_rows(shape):
        h = shape[0] // 2
        return ((pl.ds(pl.multiple_of(c * h, 16), h), slice(None)),
                (pl.ds(pl.multiple_of((1 - c) * h, 16), h), slice(None)))
    h = shape[1] // 2
    assert h % LANE == 0
    return ((slice(None), pl.ds(pl.multiple_of(c * h, LANE), h)),
            (slice(None), pl.ds(pl.multiple_of((1 - c) * h, LANE), h)))


def gather_side(bigs, tinies):
    nb, nt_ = len(bigs), len(tinies)

    def plan(ins, outs, ssem, rsem):
        x, y, c = _pos()
        me = 2 * x + y
        chips = _other_chips(x, y)
        sibling = (x, y, 1 - c)

        def copy(k, src, dst, to):
            return pltpu.make_async_remote_copy(src_ref=src, dst_ref=dst, send_sem=ssem.at[k], recv_sem=rsem.at[k],
                                                device_id=to, device_id_type=MESH)

        sends, landed, passed_on, tiny_landed = [], [], [], []
        for w in range(nb):
            mine, theirs = _core_halves(bigs[w].shape, c)
            for j, (cx, cy) in enumerate(chips):
                sends.append(copy(6 * w + j, ins[w].at[mine], outs[w].at[(me,) + mine], (cx, cy, c)))
                blk = outs[w].at[(2 * cx + cy,) + mine]
                landed.append((copy(6 * w + j, blk, blk, (cx, cy, c)), copy(6 * w + 3 + j, blk, blk, sibling)))
                blk = outs[w].at[(2 * cx + cy,) + theirs]
                passed_on.append(copy(6 * w + 3 + j, blk, blk, sibling))
        for w in range(nt_):
            for j, (cx, cy) in enumerate(chips):
                k = 6 * nb + 3 * w + j
                sends.append(copy(k, ins[nb + w], outs[nb + w].at[me], (cx, cy, c)))
                blk = outs[nb + w].at[2 * cx + cy]
                tiny_landed.append(copy(k, blk, blk, (cx, cy, c)))
        return sends, landed, passed_on, tiny_landed

    def start(ins, outs, ssem, rsem):
        for cp in plan(ins, outs, ssem, rsem)[0]:
            cp.start()

    def finish(ins, outs, ssem, rsem):
        sends, landed, passed_on, tiny_landed = plan(ins, outs, ssem, rsem)
        for arrived, forward in landed:
            arrived.wait_recv()
            forward.start()
        for arrived in tiny_landed + passed_on:
            arrived.wait_recv()
        for cp in sends + [forward for _, forward in landed]:
            cp.wait_send()

    return _Side(list(bigs) + list(tinies), [SDS((4,) + a.shape, a.dtype) for a in list(bigs) + list(tinies)],
                 6 * nb + 3 * nt_, start, finish)


def swap_halves(gs, name):
    n = len(gs)

    def half_shape(g):
        l, r, cols = g.shape
        return (l, r // 2, cols) if _split_rows((r, cols)) else (l, r, cols // 2)

    def body(*refs):
        g_refs, sib_refs = refs[:n], refs[n:2 * n]
        ssem, rsem = refs[2 * n:]
        x, y, c = _pos()
        cps = []
        for w in range(n):
            give = (slice(None),) + _core_halves(gs[w].shape[1:], c)[1]
            cp = pltpu.make_async_remote_copy(src_ref=g_refs[w].at[give], dst_ref=sib_refs[w], send_sem=ssem.at[w],
                                              recv_sem=rsem.at[w], device_id=(x, y, 1 - c), device_id_type=MESH)
            cp.start()
            cps.append(cp)
        for cp in cps:
            cp.wait()

    return _pc(body, name=name, in_specs=[_ANY] * n, out_specs=[_ANY] * n,
               out_shape=[SDS(half_shape(g), g.dtype) for g in gs],
               scratch_shapes=[pltpu.SemaphoreType.DMA((n,)), pltpu.SemaphoreType.DMA((n,))])(*gs)


COL_BLOCK = 256


def add_half(sc, g, sib, name):
    l, r, cols = g.shape

    def body(sc_ref, g_ref, s_ref, o_ref):
        o_ref[...] = (g_ref[...].astype(F32) + s_ref[...].astype(F32)).astype(BF)

    if _split_rows((r, cols)):
        rh, br = _half_rows(r)
        nbk = rh // br
        blk = pl.BlockSpec((1, br, cols), lambda j, i, sc_: (j, i, 0))
        mine = pl.BlockSpec((1, br, cols), lambda j, i, sc_: (j, sc_[0] * nbk + i, 0))
        return _pc_sp(body, (l, nbk), [mine, blk], blk, SDS((l, rh, cols), BF), name)(sc, g, sib)
    nbk = cols // 2 // COL_BLOCK
    blk = pl.BlockSpec((1, r, COL_BLOCK), lambda j, i, sc_: (j, 0, i))
    mine = pl.BlockSpec((1, r, COL_BLOCK), lambda j, i, sc_: (j, 0, sc_[0] * nbk + i))
    return _pc_sp(body, (l, nbk), [mine, blk], blk, SDS((l, r, cols // 2), BF), name)(sc, g, sib)


def exchange_side(ps):
    n_ = len(ps)

    def width(p_):
        return p_.shape[2] if p_.shape[0] == 4 else p_.shape[2] // 4

    def plan(p_refs, got_refs, ssem, rsem):
        x, y, c = _pos()
        cps = []
        for w in range(n_):
            wd = width(ps[w])
            for j, (cx, cy) in enumerate(_other_chips(x, y)):
                to = 2 * cx + cy
                src = p_refs[w].at[to] if ps[w].shape[0] == 4 else p_refs[w].at[0, :, pl.ds(pl.multiple_of(to * wd, LANE), wd)]
                cps.append(pltpu.make_async_remote_copy(
                    src_ref=src, dst_ref=got_refs[w].at[j], send_sem=ssem.at[3 * w + j], recv_sem=rsem.at[3 * w + j],
                    device_id=(cx, cy, c), device_id_type=MESH))
        return cps

    def start(*refs):
        for cp in plan(*refs):
            cp.start()

    def finish(*refs):
        for cp in plan(*refs):
            cp.wait()

    return _Side(ps, [SDS((3, p_.shape[1], width(p_)), p_.dtype) for p_ in ps], 3 * n_, start, finish)


def sum4(sc, p, got, name):
    _, rh, wd = got.shape

    def body(sc_ref, p_ref, g_ref, r_ref):
        r_ref[...] = ((p_ref[0].astype(F32) + g_ref[0].astype(F32)) + (g_ref[1].astype(F32) + g_ref[2].astype(F32)))

    if rh % 16:
        assert p.shape[0] == 4
        return _pc_sp(body, (wd // COL_BLOCK,),
                      [pl.BlockSpec((1, rh, COL_BLOCK), lambda i, sc_: (sc_[1], 0, i)),
                       pl.BlockSpec((3, rh, COL_BLOCK), lambda i, sc_: (0, 0, i))],
                      pl.BlockSpec((rh, COL_BLOCK), lambda i, sc_: (0, i)), SDS((rh, wd), F32), name)(sc, p, got)
    _, br = _half_rows(2 * rh)
    own = (pl.BlockSpec((1, br, wd), lambda i, sc_: (sc_[1], i, 0)) if p.shape[0] == 4
           else pl.BlockSpec((1, br, wd), lambda i, sc_: (0, i, sc_[1])))
    return _pc_sp(body, (rh // br,), [own, pl.BlockSpec((3, br, wd), lambda i, sc_: (0, i, 0))],
                  pl.BlockSpec((br, wd), lambda i, sc_: (i, 0)), SDS((rh, wd), F32), name)(sc, p, got)


def join_halves(halves):
    n = len(halves)

    def body(*refs):
        h_refs, got_refs = refs[:n], refs[n:2 * n]
        ssem, rsem = refs[2 * n:]
        x, y, c = _pos()
        cps = []
        for w in range(n):
            cp = pltpu.make_async_remote_copy(src_ref=h_refs[w], dst_ref=got_refs[w], send_sem=ssem.at[w],
                                              recv_sem=rsem.at[w], device_id=(x, y, 1 - c), device_id_type=MESH)
            cp.start()
            cps.append(cp)
        for cp in cps:
            cp.wait()

    return _pc(body, name="join_halves", in_specs=[_ANY] * n, out_specs=[_ANY] * n,
               out_shape=[SDS(h.shape, h.dtype) for h in halves],
               scratch_shapes=[pltpu.SemaphoreType.DMA((n,)), pltpu.SemaphoreType.DMA((n,))])(*halves)


def allreduce_small(g):
    rows = g.shape[0]
    rh = rows // 2

    def body(g_ref, out_ref, sib_buf, chip_buf, sum_sc, ssem, rsem):
        x, y, c = _pos()
        me = 2 * x + y
        sibling = (x, y, 1 - c)
        mine = pl.ds(pl.multiple_of(c * rh, 8), rh)

        def copy(k, src, dst, to):
            return pltpu.make_async_remote_copy(src_ref=src, dst_ref=dst, send_sem=ssem.at[k], recv_sem=rsem.at[k],
                                                device_id=to, device_id_type=MESH)

        cp = copy(0, g_ref, sib_buf, sibling)
        cp.start()
        cp.wait()
        sum_sc[...] = g_ref[...] + sib_buf[...]
        chips = _other_chips(x, y)
        cps = [copy(1 + j, sum_sc.at[mine], chip_buf.at[me], (cx, cy, c)) for j, (cx, cy) in enumerate(chips)]
        for cp in cps:
            cp.start()
        chip_buf[me] = sum_sc[mine, :]
        for j, (cx, cy) in enumerate(chips):
            copy(1 + j, sum_sc.at[mine], chip_buf.at[2 * cx + cy], (cx, cy, c)).wait_recv()
        for cp in cps:
            cp.wait_send()
        out_ref[mine, :] = (chip_buf[0] + chip_buf[1]) + (chip_buf[2] + chip_buf[3])
        cp = copy(4, out_ref.at[mine], out_ref.at[mine], sibling)
        cp.start()
        cp.wait()

    vm = pl.BlockSpec(memory_space=pltpu.VMEM)
    return _pc(body, name="allreduce_small", in_specs=[vm], out_specs=vm, out_shape=SDS((rows, LANE), F32),
               scratch_shapes=[pltpu.VMEM((rows, LANE), F32), pltpu.VMEM((4, rh, LANE), F32), pltpu.VMEM((rows, LANE), F32),
                               pltpu.SemaphoreType.DMA((5,)), pltpu.SemaphoreType.DMA((5,))],
               compiler_params=pltpu.CompilerParams(vmem_limit_bytes=VMEM_LIMIT))(g)


def _pack_small(entries, get):
    flat = jnp.concatenate([get(n).reshape(-1).astype(F32) for n, _ in entries])
    rows = -(-flat.shape[0] // (8 * LANE)) * 8
    return jnp.pad(flat, (0, rows * LANE - flat.shape[0])).reshape(rows, LANE)


def _unpack_small(entries, packed):
    out, off = {}, 0
    flat = packed.reshape(-1)
    for name, n in entries:
        out[name] = flat[off:off + n]
        off += n
    return out


def _cols_full(blk):
    return blk.transpose(1, 0, 2).reshape(blk.shape[1], 4 * blk.shape[2])


def kernel(x, p, pre_mix_norm, w_in, a_ln_g, a_ln_b, a_spatial_w, a_spatial_b, a_out, b_gk, b_gk_bias, b_out_norm, b_out, w_mix_out, post_mix_norm, pre_ffn_norm, w_up, conv_w, conv_b, w_down, post_ffn_norm, w_ple, w_ple_gate, post_ple_norm, loss_target, m_pre_mix_norm, m_w_in, m_a_ln_g, m_a_ln_b, m_a_spatial_w, m_a_spatial_b, m_a_out, m_b_gk, m_b_gk_bias, m_b_out_norm, m_b_out, m_w_mix_out, m_post_mix_norm, m_pre_ffn_norm, m_w_up, m_conv_w, m_conv_b, m_w_down, m_post_ffn_norm, m_w_ple, m_w_ple_gate, m_post_ple_norm, v_pre_mix_norm, v_w_in, v_a_ln_g, v_a_ln_b, v_a_spatial_w, v_a_spatial_b, v_a_out, v_b_gk, v_b_gk_bias, v_b_out_norm, v_b_out, v_w_mix_out, v_post_mix_norm, v_pre_ffn_norm, v_w_up, v_conv_w, v_conv_b, v_w_down, v_post_ffn_norm, v_w_ple, v_w_ple_gate, v_post_ple_norm):
    args = dict(locals())
    order = ['pre_mix_norm', 'w_in', 'a_ln_g', 'a_ln_b', 'a_spatial_w', 'a_spatial_b', 'a_out', 'b_gk', 'b_gk_bias',
             'b_out_norm', 'b_out', 'w_mix_out', 'post_mix_norm', 'pre_ffn_norm', 'w_up', 'conv_w', 'conv_b', 'w_down',
             'post_ffn_norm', 'w_ple', 'w_ple_gate', 'post_ple_norm']
    assert sorted(BIG + TINY + tuple(n for n, _ in SMALL)) == sorted(order)
    s = x.shape[1]
    xs = x.reshape(s, D)
    ps = p.reshape(s, PLE)
    tgt = loss_target.reshape(s, D)
    t_big = min(1024, s)
    t_mid = min(512, s)
    t_small = min(256, s)
    t_gla = min(256, s)
    mx_, my_, mc_ = _pos()
    me = 2 * mx_ + my_
    sc = jnp.stack([mc_, me]).astype(jnp.int32)
    shard = lambda n: args[n].reshape(args[n].shape[1:])

    mine = {n: shard(n).astype(BF) for n in BIG}
    mine["w_in"] = shard("w_in").T.astype(BF)
    mine.update({n: shard(n) for n in TINY})
    fill = lambda names, gots: {n: lax.dynamic_update_slice(got, mine[n][None], (me, 0, 0)) for n, got in zip(names, gots)}
    first = ("w_in",) + TINY
    full = fill(first, _run_side(gather_side([mine["w_in"]], [mine[n] for n in TINY]), "gather_first"))
    wi = full["w_in"].reshape(4 * 1540, D)
    seg = lambda a, b: wi[a:b]
    qk = [seg(1024 + h * B_HK, 1024 + (h + 1) * B_HK) for h in range(B_H)]
    kk = [seg(1536 + h * B_HK, 1536 + (h + 1) * B_HK) for h in range(B_H)]
    w_z = jnp.concatenate([seg(0, 1024)] + [m_ for h in range(B_H) for m_ in (qk[h], kk[h])]
                          + [seg(2048, 4096), seg(4112, 6160), seg(4096, 4112), jnp.zeros((LANE - B_RANK, D), BF)], axis=0)
    wgk = jnp.pad(_cols_full(full["b_gk"]).astype(BF), ((0, LANE - B_RANK), (0, 0)))
    w_conv = _cols_full(full["conv_w"])
    g1, g2, g3 = pre_mix_norm.reshape(1, D), post_mix_norm.reshape(1, D), pre_ffn_norm.reshape(1, D)
    g4, g5 = post_ffn_norm.reshape(1, D), post_ple_norm.reshape(1, D)
    ln_g, ln_b = a_ln_g.reshape(1, A_W), a_ln_b.reshape(1, A_W)
    w_s = a_spatial_w.reshape(A_G, A_C, A_C)
    w_cat = w_s.transpose(1, 0, 2).reshape(A_C, A_G * A_C)
    w_cat_t = w_s.transpose(2, 0, 1).reshape(A_C, A_G * A_C)
    bias_full = jnp.repeat(a_spatial_b.reshape(A_G, A_C).T, A_GD, axis=1)
    bdm = (jnp.arange(A_G * A_C)[:, None] // A_C == jnp.arange(A_W)[None, :] // A_GD).astype(BF)
    gk_bias = b_gk_bias.reshape(1, B_H * B_HK)
    wn = b_out_norm.reshape(1, B_HV)
    cb = conv_b.reshape(1, 2 * D_FF)
    idx = jnp.arange(t_gla)
    ltri = ((idx[:, None] // B_C == idx[None, :] // B_C) & (idx[None, :] <= idx[:, None])).astype(BF)

    a, z, qk32, *gots = norm_matmul(xs, g1, w_z, D, t_big, "in_proj", nblk=6, f32_blk=1,
                                    side=gather_side([mine[n] for n in BIG[1:]], []))
    full.update(fill(BIG[1:], gots))
    w_aout, w_ple_f = _cols_full(full["a_out"]), _cols_full(full["w_ple"])
    w_bout, w_mix, w_pg = (full[n].reshape(D, D) for n in ("b_out", "w_mix_out", "w_ple_gate"))
    w_dn, w_up3 = full["w_down"].reshape(D_FF, D), full["w_up"]
    zl = mm_cols(a, w_z, 48, LANE, t_big, "lr_proj")
    sa = sgu_fwd(z, ln_g, ln_b, w_cat, bias_full, bdm, t_mid)
    ob, o, states = gla_fwd(z, qk32, zl, wgk, gk_bias, wn, ltri, t_gla)
    ya, yb, mp, mx, h1 = mix_fwd(sa, ob, z, xs, w_aout, w_bout, w_mix, g2, t_small)
    c, up_g, up_v, cg, cv, ff = ffn_up_fwd(h1, g3, w_up3, w_conv, cb, t_mid)
    f, h2, pg, pe, dy, loss = out_fwd(ff, h1, ps, tgt, w_dn, w_pg, w_ple_f, g4, g5, t_small)

    dh2, dpe, dpg, df, dff, gg5, gg4 = out_bwd(dy, pg, pe, f, g5, g4, w_pg, w_dn, t_small)
    dup_g, dup_v, gcw, gcb, dh1, gg3 = ffn_up_bwd(up_g, up_v, cg, cv, dff, w_conv, _cols_full(w_up3), h1, g3, dh2,
                                                  t_small)
    dmx, dya, dyb, dga, dgb, dsa, dob, gg2 = mix_bwd(dh1, mx, z, ya, yb, g2, w_mix, w_aout, w_bout, t_small)
    duv, g_lng, g_lnb, g_wcat, g_bst = sgu_bwd(z, dsa, ln_g, ln_b, w_cat, w_cat_t, bias_full, bdm, t_mid)
    g_ws = g_wcat.reshape(A_C, A_G, A_C).transpose(1, 0, 2)

    grads = {
        "a_out": mm_tn(sa, dya, "dw_a_out")[None],
        "b_out": mm_tn(ob, dyb, "dw_b_out").reshape(4, D // 4, D),
        "w_mix_out": mm_tn(mp, dmx, "dw_mix").reshape(4, D // 4, D),
        "w_up": jnp.concatenate([mm_tn(c, dup_g, "dw_up_g"), mm_tn(c, dup_v, "dw_up_v")], axis=1)[None],
        "w_down": mm_tn(ff, df, "dw_down").reshape(4, D_FF // 4, D),
        "w_ple": mm_tn(ps, dpe, "dw_ple")[None],
        "w_ple_gate": mm_tn(h2, dpg, "dw_ple_gate").reshape(4, D // 4, D),
    }

    def chip_partials(names):
        gs = [grads[n] for n in names]
        return [add_half(sc, g, sib, "partial_" + n)
                for n, g, sib in zip(names, gs, swap_halves(gs, "swap_halves_" + names[0]))]

    parts = dict(zip(BIG[1:], chip_partials(BIG[1:])))
    dqk, dvb, dog, dpre, g_gkb, g_wn, *gots = gla_bwd(z, qk32, zl, o, dob, states, wgk, gk_bias, wn, ltri, ltri.T,
                                                      t_gla, side=exchange_side([parts[n] for n in BIG[1:]]))
    got = dict(zip(BIG[1:], gots))
    dlr = mm_nt_small(dpre, wgk, t_mid, "dlr")
    segs = [duv, dqk, dvb, dog, dga, dgb, dlr]

    gz = [mm_tn(sg_, a, "dw_in_%d" % k) for k, sg_ in enumerate(segs)]
    gq = [gz[1][h * 256:h * 256 + B_HK] for h in range(B_H)]
    gk = [gz[1][h * 256 + B_HK:(h + 1) * 256] for h in range(B_H)]
    g_in = jnp.concatenate([gz[0]] + gq + gk + [gz[2], gz[3], gz[6][:B_RANK], gz[4], gz[5]], axis=0)
    grads["w_in"] = g_in.reshape(4, 1540, D)
    parts["w_in"], = chip_partials(("w_in",))
    dx, gg1, got["w_in"] = nt_normbwd(segs, w_z, xs, g1, dh1, t_small, "in_bwd",
                                      side=exchange_side([parts["w_in"]]))

    reds = [sum4(sc, parts[n], got[n], "sum_" + n) for n in BIG]
    outs = {}
    for n, red, sib in zip(BIG, reds, join_halves(reds)):
        if n == "w_in":
            res = adamw_cols(sc, red, sib, shard(n).T, shard("m_" + n).T, shard("v_" + n).T, "adamw_" + n)
            res = [r_.T for r_ in res]
        else:
            res = adamw_halves(sc, red, sib, shard(n), shard("m_" + n), shard("v_" + n), "adamw_" + n)
        outs[n] = [r_.reshape(args[n].shape) for r_ in res]

    small_g = {
        "pre_mix_norm": gg1, "a_ln_g": g_lng, "a_ln_b": g_lnb, "a_spatial_w": g_ws, "a_spatial_b": g_bst.T,
        "b_gk_bias": g_gkb, "b_out_norm": g_wn, "post_mix_norm": gg2, "pre_ffn_norm": gg3,
        "conv_b": gcb, "post_ffn_norm": gg4, "post_ple_norm": gg5,
        "b_gk": mm_tn(zl, dpre, "dw_gk")[:B_RANK], "conv_w": gcw,
        "loss": loss,
    }
    red_entries = SMALL + (("b_gk", B_RANK * 512), ("conv_w", 3 * 2 * D_FF), ("loss", 1))
    g_fin = _unpack_small(red_entries, allreduce_small(_pack_small(red_entries, lambda n: small_g[n])))
    g_fin["b_gk"] = lax.dynamic_slice(g_fin["b_gk"].reshape(B_RANK, 512), (0, me * B_HK), (B_RANK, B_HK))
    g_fin["conv_w"] = lax.dynamic_slice(g_fin["conv_w"].reshape(3, 2 * D_FF), (0, me * 1408), (3, 1408))
    upd_entries = SMALL + (("b_gk", B_RANK * B_HK), ("conv_w", 3 * 1408))
    res = adamw_small(*[_pack_small(upd_entries, get) for get in
                        (lambda n: g_fin[n], lambda n: args[n], lambda n: args["m_" + n], lambda n: args["v_" + n])])
    res = [_unpack_small(upd_entries, r_) for r_ in res]
    for n, _ in upd_entries:
        outs[n] = [r_[n].reshape(args[n].shape) for r_ in [g_fin] + res]

    return (g_fin["loss"].reshape(()), dx.reshape(x.shape), *[outs[n][0] for n in order], *[outs[n][1] for n in order],
            *[outs[n][2] for n in order], *[outs[n][3] for n in order])
```

```python
import functools
import math

import jax
import jax.numpy as jnp
from jax import lax
from jax.experimental import pallas as pl
from jax.experimental.pallas import tpu as pltpu

F32 = jnp.float32
BF = jnp.bfloat16
SDS = jax.ShapeDtypeStruct
MESH = pl.DeviceIdType.MESH

EPS = 1e-6
D = 1024
A_W = 512
A_G, A_C = 8, 128
A_GD = A_W // A_G
B_H, B_HK, B_HV = 4, 128, 256
B_C = 64
GLA_HPB = 4
B_RANK = 16
D_FF = 2816
PLE = 256
LANE = 128
VMEM_LIMIT = 60 * 1024 * 1024

ADAM_LR, ADAM_B1, ADAM_B2, ADAM_EPS, ADAM_WD, ADAM_STEP = 0.001, 0.9, 0.999, 1e-08, 0.01, 10

_GC = math.sqrt(2.0 / math.pi)
_GA = 0.044715

BIG = ("w_in", "a_out", "b_out", "w_mix_out", "w_up", "w_down", "w_ple", "w_ple_gate")
TINY = ("b_gk", "conv_w")
SMALL = (("pre_mix_norm", 1024), ("a_ln_g", 512), ("a_ln_b", 512), ("a_spatial_w", 131072),
         ("a_spatial_b", 1024), ("b_gk_bias", 512), ("b_out_norm", 256), ("post_mix_norm", 1024),
         ("pre_ffn_norm", 1024), ("conv_b", 5632), ("post_ffn_norm", 1024), ("post_ple_norm", 1024))


def _pc(body, **kw):
    return pl.pallas_call(body, **kw)


def _cp(n):
    return pltpu.CompilerParams(dimension_semantics=("arbitrary",) * n, vmem_limit_bytes=VMEM_LIMIT)


def _const(shape):
    nd = len(shape)
    return pl.BlockSpec(shape, lambda *_: (0,) * nd, pipeline_mode=pl.Buffered(1))


def _acc(shape):
    nd = len(shape)
    return pl.BlockSpec(shape, lambda *_: (0,) * nd)


def _dot(a, b):
    return jnp.dot(a, b, preferred_element_type=F32)


def _dot_nt(a, b):
    return lax.dot_general(a, b, (((1,), (1,)), ((), ())), preferred_element_type=F32)


def _dot_tn(a, b):
    return lax.dot_general(a, b, (((0,), (0,)), ((), ())), preferred_element_type=F32)


def _gelu(x):
    return 0.5 * x * (1.0 + jnp.tanh(_GC * (x + _GA * x * x * x)))


def _gelu_and_grad(x):
    x2 = x * x
    s = 0.5 * jnp.tanh((_GC * x) * (1.0 + _GA * x2)) + 0.5
    g = x * s
    return g, s + g * (1.0 - s) * ((6.0 * _GC * _GA) * x2 + 2.0 * _GC)


def _log_sigmoid(x):
    return jnp.minimum(x, 0.0) - jnp.log(1.0 + jnp.exp(-jnp.abs(x)))


def _rms(x, g):
    return x * lax.rsqrt(jnp.mean(x * x, axis=-1, keepdims=True) + EPS) * g


def _rms_bwd(dy, x, g):
    r = lax.rsqrt(jnp.mean(x * x, axis=-1, keepdims=True) + EPS)
    n = x * r
    dn = dy * g
    dx = r * (dn - n * jnp.mean(dn * n, axis=-1, keepdims=True))
    return dx, jnp.sum(dy * n, axis=0, keepdims=True)


def _ldot3(l, x):
    h = x.astype(BF)
    r = x - h.astype(F32)
    m = r.astype(BF)
    lo = (r - m.astype(F32)).astype(BF)
    return _dot(l, h) + _dot(l, m) + _dot(l, lo)


def _split_side(refs, n_in, n_out, n_scratch, side):
    si, so = (side.n_in, side.n_out) if side else (0, 0)
    cuts = [n_in, si, n_out, so, n_scratch]
    out, at = [], 0
    for c in cuts:
        out.append(refs[at:at + c])
        at += c
    return (*out, refs[at:])


def _side_specs(side):
    return ([_ANY] * side.n_in, [_ANY] * side.n_out, side.out_shapes, side.scratch, side.ins) if side else ([],) * 5


def norm_matmul(x, g, wt, bn, t, name, nblk, f32_blk, tail_blk, side=None):
    s, dm = x.shape
    w_spec = pl.BlockSpec((bn, dm), lambda i, j: (j, 0))
    nt = s // t

    def body(*refs):
        (x_ref, g_ref, w_ref, wl_ref), s_in, outs, s_out, (a_sc,), s_scr = _split_side(refs, 4, 4, 1, side)
        a_ref, z_ref, f32_ref, tail_ref = outs
        i, j = pl.program_id(0), pl.program_id(1)
        if side:
            @pl.when((i == 0) & (j == 0))
            def _():
                side.start(s_in, s_out, *s_scr)

        @pl.when(j == 0)
        def _():
            a = _rms(x_ref[...], g_ref[...]).astype(BF)
            a_sc[...] = a
            a_ref[...] = a
            tail_ref[...] = _dot_nt(a, wl_ref[...]).astype(BF)

        acc = _dot_nt(a_sc[...], w_ref[...])
        z_ref[...] = acc.astype(BF)

        @pl.when(j == f32_blk)
        def _():
            f32_ref[...] = acc
        if side:
            @pl.when((i == nt - 1) & (j == nblk - 1))
            def _():
                side.finish(s_in, s_out, *s_scr)

    si_specs, so_specs, so_shapes, s_scratch, s_ins = _side_specs(side)
    return _pc(
        body, name=name, grid=(nt, nblk),
        in_specs=[pl.BlockSpec((t, dm), lambda i, j: (i, 0)), _const((1, dm)), w_spec,
                  pl.BlockSpec((LANE, dm), lambda i, j: (tail_blk, 0), pipeline_mode=pl.Buffered(1))] + si_specs,
        out_specs=[pl.BlockSpec((t, dm), lambda i, j: (i, 0)), pl.BlockSpec((t, bn), lambda i, j: (i, j)),
                   pl.BlockSpec((t, bn), lambda i, j: (i, 0)), pl.BlockSpec((t, LANE), lambda i, j: (i, 0))] + so_specs,
        out_shape=[SDS((s, dm), BF), SDS((s, nblk * bn), BF), SDS((s, bn), F32), SDS((s, LANE), BF)] + so_shapes,
        scratch_shapes=[pltpu.VMEM((t, dm), BF)] + s_scratch, compiler_params=_cp(2))(x, g, wt, wt, *s_ins)


def _sgu_weights(wc_ref, transposed):
    r = lax.broadcasted_iota(jnp.int32, (A_C, A_G * A_C), 0)
    c = lax.broadcasted_iota(jnp.int32, (A_C, A_G * A_C), 1) & (A_C - 1)
    return jnp.where((r <= c) if transposed else (c <= r), wc_ref[...], 0.0).astype(BF)


def _sgu_spread(xs, bdm):
    return jnp.concatenate([jnp.tile(x, (A_G, 1)) * bdm for x in xs], axis=1)


def _sgu_recompute(v, lng, lnb):
    gv, dgv = _gelu_and_grad(v)
    mu = jnp.mean(gv, axis=-1, keepdims=True)
    xc = gv - mu
    rstd = lax.rsqrt(jnp.mean(xc * xc, axis=-1, keepdims=True) + EPS)
    xhat = xc * rstd
    return dgv, rstd, xhat, (xhat * lng + lnb).astype(BF)


def sgu_fwd(z, ln_g, ln_b, w_cat, bias_full, bdm, t):
    s = z.shape[0]
    nch = t // A_C

    def body(u_ref, v_ref, g_ref, b_ref, wc_ref, bias_ref, bdm_ref, sa_ref):
        vns = [_sgu_recompute(v_ref[pl.ds(ci * A_C, A_C), :].astype(F32), g_ref[...], b_ref[...])[3]
               for ci in range(nch)]
        mixed = _dot(_sgu_weights(wc_ref, False), _sgu_spread(vns, bdm_ref[...]))
        for ci in range(nch):
            rows = pl.ds(ci * A_C, A_C)
            s_ = mixed[:, ci * A_W:(ci + 1) * A_W] + bias_ref[...]
            sa_ref[rows, :] = (_gelu(u_ref[rows, :].astype(F32)) * s_).astype(BF)

    return _pc(
        body, name="sgu_fwd", grid=(s // t,),
        in_specs=[pl.BlockSpec((t, A_W), lambda i: (i, 0)), pl.BlockSpec((t, A_W), lambda i: (i, 1)),
                  _const((1, A_W)), _const((1, A_W)), _const((A_C, A_G * A_C)), _const((A_C, A_W)),
                  _const((A_G * A_C, A_W))],
        out_specs=pl.BlockSpec((t, A_W), lambda i: (i, 0)),
        out_shape=SDS((s, A_W), BF), compiler_params=_cp(1))(z, z, ln_g, ln_b, w_cat, bias_full, bdm)


def _gla_decays(qk, lr, wgk, bias, l, t):
    nc = t // B_C
    q = qk[:, :B_HK].astype(F32) * (B_HK ** -0.5)
    k = qk[:, B_HK:].astype(F32)
    pre = _dot(lr, wgk) + bias
    la = _log_sigmoid(pre) * (1.0 / 16.0)
    b = _ldot3(l, la)
    b3 = b.reshape(nc, B_C, B_HK)
    bl = jnp.broadcast_to(b3[:, B_C - 1:B_C, :], (nc, B_C, B_HK)).reshape(t, B_HK)
    eb, enb, etb = jnp.exp(b), jnp.exp(-b), jnp.exp(bl - b)
    return pre, b, bl, eb, enb, etb, q * eb, k * enb, k * etb


def gla_fwd(z, qk32, zl, wgk, bias, wn, ltri, t):
    s = z.shape[0]
    nc = t // B_C
    hpb = GLA_HPB
    kw, vw = hpb * B_HK, hpb * B_HV

    def body(qk_ref, v_ref, og_ref, lr_ref, wgk_ref, bias_ref, wn_ref, l_ref, ob_ref, o_ref, st_ref, st_sc, o_sc):
        g = pl.program_id(1)

        @pl.when(pl.program_id(0) == 0)
        def _():
            for hh in range(hpb):
                st_sc[g * hpb + hh] = jnp.zeros((B_HV, B_HK), F32)

        lr, l = lr_ref[...], l_ref[...]
        for hh in range(hpb):
            h = g * hpb + hh
            cv, ck = slice(hh * B_HV, (hh + 1) * B_HV), slice(hh * B_HK, (hh + 1) * B_HK)
            _, _, bl, _, _, _, qd, ki, kt = _gla_decays(qk_ref[:, cv], lr, wgk_ref[:, ck], bias_ref[:, ck], l, t)
            qd, ki, kt = qd.astype(BF), ki.astype(BF), kt.astype(BF)
            vb = v_ref[:, cv]
            sc = jnp.where(l > 0, _dot_nt(qd, ki), 0.0).astype(BF)
            o_sc[hh] = _dot(sc, vb)
            for n in range(nc):
                rows = slice(n * B_C, (n + 1) * B_C)
                st = st_sc[h]
                stb = st.astype(BF)
                st_ref[n, hh] = stb
                o_sc[hh, rows, :] += _dot_nt(qd[rows], stb)
                st_sc[h] = st * jnp.exp(bl[n * B_C:n * B_C + 1, :]) + _dot_tn(vb[rows], kt[rows])
            ob = o_sc[hh].astype(BF)
            o_ref[:, cv] = ob
            og = og_ref[:, cv].astype(F32)
            ob_ref[:, cv] = (_rms(ob.astype(F32), wn_ref[...]) * og * jax.nn.sigmoid(og)).astype(BF)

    vo, go = 2048 // vw, 3072 // vw
    return _pc(
        body, name="gla_fwd", grid=(s // t, B_H // hpb),
        in_specs=[pl.BlockSpec((t, vw), lambda i, g: (i, g)), pl.BlockSpec((t, vw), lambda i, g: (i, vo + g)),
                  pl.BlockSpec((t, vw), lambda i, g: (i, go + g)), pl.BlockSpec((t, LANE), lambda i, g: (i, 0)),
                  pl.BlockSpec((LANE, kw), lambda i, g: (0, g)), pl.BlockSpec((1, kw), lambda i, g: (0, g)),
                  _const((1, B_HV)), _const((t, t))],
        out_specs=[pl.BlockSpec((t, vw), lambda i, g: (i, g)), pl.BlockSpec((t, vw), lambda i, g: (i, g)),
                   pl.BlockSpec((nc, hpb, B_HV, B_HK), lambda i, g: (i, g, 0, 0))],
        out_shape=[SDS((s, D), BF), SDS((s, D), BF), SDS((s // B_C, B_H, B_HV, B_HK), BF)],
        scratch_shapes=[pltpu.VMEM((B_H, B_HV, B_HK), F32), pltpu.VMEM((hpb, t, B_HV), F32)],
        compiler_params=_cp(2))(qk32, z, z, zl, wgk, bias, wn, ltri)


def mix_fwd(sa, ob, z, x, a_out, b_out, w_mix, g2, t):
    s = x.shape[0]

    def body(sa_ref, ob_ref, ga_ref, gb_ref, x_ref, ao_ref, bo_ref, wm_ref, g2_ref,
             ya_ref, yb_ref, mp_ref, mx_ref, h1_ref):
        ya = _dot(sa_ref[...], ao_ref[...]).astype(BF)
        yb = _dot(ob_ref[...], bo_ref[...]).astype(BF)
        ya_ref[...] = ya
        yb_ref[...] = yb
        mp = (jax.nn.sigmoid(ga_ref[...].astype(F32)) * ya.astype(F32)
              + jax.nn.sigmoid(gb_ref[...].astype(F32)) * yb.astype(F32)).astype(BF)
        mp_ref[...] = mp
        mx = _dot(mp, wm_ref[...]).astype(BF)
        mx_ref[...] = mx
        h1_ref[...] = x_ref[...] + _rms(mx.astype(F32), g2_ref[...])

    row = lambda w: pl.BlockSpec((t, w), lambda i: (i, 0))
    return _pc(
        body, name="mix_fwd", grid=(s // t,),
        in_specs=[row(A_W), row(D), pl.BlockSpec((t, D), lambda i: (i, 4)), pl.BlockSpec((t, D), lambda i: (i, 5)),
                  row(D), _const((A_W, D)), _const((D, D)), _const((D, D)), _const((1, D))],
        out_specs=[row(D)] * 5,
        out_shape=[SDS((s, D), BF)] * 4 + [SDS((s, D), F32)],
        compiler_params=_cp(1))(sa, ob, z, z, x, a_out, b_out, w_mix, g2)


def ffn_up_fwd(h1, g3, w_up3, conv_w, conv_b, t):
    s = h1.shape[0]
    bn = w_up3.shape[2]

    def body(x_ref, g_ref, wg_ref, wv_ref, cwg_ref, cwv_ref, cbg_ref, cbv_ref,
             c_ref, ug_ref, uv_ref, cg_ref, cv_ref, ff_ref, c_sc, carry):
        i, j = pl.program_id(0), pl.program_id(1)

        @pl.when(j == 0)
        def _():
            c = _rms(x_ref[...], g_ref[...]).astype(BF)
            c_sc[...] = c
            c_ref[...] = c

        @pl.when(i == 0)
        def _():
            carry[j] = jnp.zeros((2, 8, bn), F32)

        def branch(k, w_ref, cw_ref, cb_ref, u_ref, o_ref):
            ub = _dot(c_sc[...], w_ref[...]).astype(BF)
            u_ref[...] = ub
            u = ub.astype(F32)
            ext = jnp.concatenate([carry[j, k], u], axis=0)
            carry[j, k] = u[t - 8:]
            w = cw_ref[...]
            cc = (cb_ref[...] + w[0:1] * pltpu.roll(ext, 2, 0) + w[1:2] * pltpu.roll(ext, 1, 0) + w[2:3] * ext)[8:]
            cc = cc.astype(BF)
            o_ref[...] = cc
            return cc.astype(F32)

        g = _gelu(branch(0, wg_ref, cwg_ref, cbg_ref, ug_ref, cg_ref))
        ff_ref[...] = (g * branch(1, wv_ref, cwv_ref, cbv_ref, uv_ref, cv_ref)).astype(BF)

    col = lambda rows, off: pl.BlockSpec((rows, bn), lambda i, j: (0, j + off))
    out = pl.BlockSpec((t, bn), lambda i, j: (i, j))
    return _pc(
        body, name="ffn_up_fwd", grid=(s // t, 2),
        in_specs=[pl.BlockSpec((t, D), lambda i, j: (i, 0)), _const((1, D)),
                  pl.BlockSpec((None, D, bn), lambda i, j: (j, 0, 0)), pl.BlockSpec((None, D, bn), lambda i, j: (j + 2, 0, 0)),
                  col(3, 0), col(3, 2), col(1, 0), col(1, 2)],
        out_specs=[pl.BlockSpec((t, D), lambda i, j: (i, 0))] + [out] * 5,
        out_shape=[SDS((s, D), BF)] + [SDS((s, D_FF), BF)] * 5,
        scratch_shapes=[pltpu.VMEM((t, D), BF), pltpu.VMEM((2, 2, 8, bn), F32)],
        compiler_params=_cp(2))(h1, g3, w_up3, w_up3, conv_w, conv_w, conv_b, conv_b)


def ffn_up_bwd(ug, uv, cg, cv, dff, conv_w, w_up, h1, g3, dh2, t):
    s = h1.shape[0]
    nt = s // t
    hb = t // 8
    bn = 1408
    r = t + 8

    def body(ug_ref, uv_ref, cg_ref, cv_ref, cag_ref, cav_ref, d_ref, da_ref, cw_ref, w_ref, x_ref, g_ref, dres_ref,
             dug_ref, duv_ref, gw_ref, gb_ref, dx_ref, gg_ref):
        i = pl.program_id(0)

        @pl.when(i == 0)
        def _():
            gw_ref[...] = jnp.zeros((3, 2 * D_FF), F32)
            gb_ref[...] = jnp.zeros((1, 2 * D_FF), F32)
            gg_ref[...] = jnp.zeros((1, D), F32)

        more = (i < nt - 1).astype(F32)

        def gate(c_g, c_v, d_):
            gl, dgl = _gelu_and_grad(c_g.astype(F32))
            return d_ * c_v.astype(F32) * dgl, d_ * gl

        def back(dc, dc_next, u_ref, cols, off, du_ref):
            w = cw_ref[:, off:off + bn]
            d_ext = jnp.concatenate([dc, dc_next], axis=0)
            d1, d2 = pltpu.roll(d_ext, r - 1, 0)[:t], pltpu.roll(d_ext, r - 2, 0)[:t]
            du_ref[:, cols] = (w[2:3] * dc + w[1:2] * d1 + w[0:1] * d2).astype(BF)
            u = u_ref[:, cols].astype(F32)
            gw_ref[0:1, off:off + bn] += jnp.sum(d2 * u, axis=0, keepdims=True)
            gw_ref[1:2, off:off + bn] += jnp.sum(d1 * u, axis=0, keepdims=True)
            gw_ref[2:3, off:off + bn] += jnp.sum(dc * u, axis=0, keepdims=True)
            gb_ref[:, off:off + bn] += jnp.sum(dc, axis=0, keepdims=True)

        for kb in range(D_FF // bn):
            cols = slice(kb * bn, (kb + 1) * bn)
            dg, dv = gate(cg_ref[:, cols], cv_ref[:, cols], d_ref[:, cols].astype(F32))
            dg_n, dv_n = gate(cag_ref[:, cols], cav_ref[:, cols], da_ref[:, cols].astype(F32) * more)
            back(dg, dg_n, ug_ref, cols, kb * bn, dug_ref)
            back(dv, dv_n, uv_ref, cols, D_FF + kb * bn, duv_ref)

        acc = _dot_nt(dug_ref[...], w_ref[:, :D_FF]) + _dot_nt(duv_ref[...], w_ref[:, D_FF:])
        dxn, dg3 = _rms_bwd(acc, x_ref[...], g_ref[...])
        dx_ref[...] = dres_ref[...] + dxn
        gg_ref[...] += dg3

    tile = lambda width: pl.BlockSpec((t, width), lambda i: (i, 0))
    after = pl.BlockSpec((8, D_FF), lambda i: (jnp.minimum((i + 1) * hb, nt * hb - 1), 0))
    return _pc(
        body, name="ffn_up_bwd", grid=(nt,),
        in_specs=[tile(D_FF)] * 4 + [after, after, tile(D_FF), after, _const((3, 2 * D_FF)), _const((D, 2 * D_FF)),
                                     tile(D), _const((1, D)), tile(D)],
        out_specs=[tile(D_FF), tile(D_FF), _acc((3, 2 * D_FF)), _acc((1, 2 * D_FF)), tile(D), _acc((1, D))],
        out_shape=[SDS((s, D_FF), BF), SDS((s, D_FF), BF), SDS((3, 2 * D_FF), F32), SDS((1, 2 * D_FF), F32),
                   SDS((s, D), F32), SDS((1, D), F32)],
        compiler_params=_cp(1))(ug, uv, cg, cv, cg, cv, dff, dff, conv_w, w_up, h1, g3, dh2)


def out_fwd(ff, h1, p, tgt, w_down, w_pg, w_ple, g4, g5, t):
    s = h1.shape[0]

    def body(ff_ref, h1_ref, p_ref, t_ref, wd_ref, wpg_ref, wpl_ref, g4_ref, g5_ref,
             f_ref, h2_ref, pg_ref, pe_ref, dy_ref, loss_ref):
        @pl.when(pl.program_id(0) == 0)
        def _():
            loss_ref[...] = jnp.zeros((1, 1), F32)

        f = _dot(ff_ref[...], wd_ref[...]).astype(BF)
        f_ref[...] = f
        h2 = h1_ref[...] + _rms(f.astype(F32), g4_ref[...])
        h2b = h2.astype(BF)
        h2_ref[...] = h2b
        pg = _dot(h2b, wpg_ref[...]).astype(BF)
        pe = _dot(p_ref[...].astype(BF), wpl_ref[...]).astype(BF)
        pg_ref[...] = pg
        pe_ref[...] = pe
        y = h2 + _rms(jax.nn.sigmoid(pg.astype(F32)) * pe.astype(F32), g5_ref[...])
        err = y - t_ref[...]
        dy_ref[...] = err * (1.0 / D)
        loss_ref[...] += (0.5 / D) * jnp.sum(err * err)

    row = lambda w: pl.BlockSpec((t, w), lambda i: (i, 0))
    return _pc(
        body, name="out_fwd", grid=(s // t,),
        in_specs=[row(D_FF), row(D), row(PLE), row(D), _const((D_FF, D)), _const((D, D)), _const((PLE, D)),
                  _const((1, D)), _const((1, D))],
        out_specs=[row(D)] * 5 + [_acc((1, 1))],
        out_shape=[SDS((s, D), BF)] * 4 + [SDS((s, D), F32), SDS((1, 1), F32)],
        compiler_params=_cp(1))(ff, h1, p, tgt, w_down, w_pg, w_ple, g4, g5)


def out_bwd(dy, pg, pe, f, g5, g4, w_pg, w_down, t):
    s = dy.shape[0]

    def body(dy_ref, pg_ref, pe_ref, f_ref, g5_ref, g4_ref, wpg_ref, wd_ref,
             dh2_ref, dpe_ref, dpg_ref, df_ref, dff_ref, gg5_ref, gg4_ref):
        @pl.when(pl.program_id(0) == 0)
        def _():
            gg5_ref[...] = jnp.zeros((1, D), F32)
            gg4_ref[...] = jnp.zeros((1, D), F32)

        dy_ = dy_ref[...]
        pg_ = pg_ref[...].astype(F32)
        pe_ = pe_ref[...].astype(F32)
        sg = jax.nn.sigmoid(pg_)
        dple, dg5 = _rms_bwd(dy_, sg * pe_, g5_ref[...])
        gg5_ref[...] += dg5
        dpe_ref[...] = (dple * sg).astype(BF)
        dpg = (dple * pe_ * sg * (1.0 - sg)).astype(BF)
        dpg_ref[...] = dpg
        dh2 = dy_ + _dot_nt(dpg, wpg_ref[...])
        dh2_ref[...] = dh2
        df, dg4 = _rms_bwd(dh2, f_ref[...].astype(F32), g4_ref[...])
        gg4_ref[...] += dg4
        dfb = df.astype(BF)
        df_ref[...] = dfb
        dff_ref[...] = _dot_nt(dfb, wd_ref[...]).astype(BF)

    row = lambda w: pl.BlockSpec((t, w), lambda i: (i, 0))
    return _pc(
        body, name="out_bwd", grid=(s // t,),
        in_specs=[row(D), row(D), row(D), row(D), _const((1, D)), _const((1, D)), _const((D, D)), _const((D_FF, D))],
        out_specs=[row(D), row(D), row(D), row(D), row(D_FF), _acc((1, D)), _acc((1, D))],
        out_shape=[SDS((s, D), F32), SDS((s, D), BF), SDS((s, D), BF), SDS((s, D), BF), SDS((s, D_FF), BF),
                   SDS((1, D), F32), SDS((1, D), F32)],
        compiler_params=_cp(1))(dy, pg, pe, f, g5, g4, w_pg, w_down)


def nt_normbwd(dys, w, xin, gain, dres, t, name, side=None):
    s = xin.shape[0]
    nt = s // t
    np_ = len(dys)

    def body(*refs):
        ins_, s_in, (dx_ref, gg_ref), s_out, _, s_scr = _split_side(refs, np_ + 4, 2, 0, side)
        dy_refs = ins_[:np_]
        w_ref, x_ref, g_ref, dres_ref = ins_[np_:]
        i = pl.program_id(0)

        @pl.when(i == 0)
        def _():
            gg_ref[...] = jnp.zeros((1, D), F32)
            if side:
                side.start(s_in, s_out, *s_scr)

        acc = _dot(jnp.concatenate([r_[...] for r_ in dy_refs], axis=1), w_ref[...])
        dxn, dg = _rms_bwd(acc, x_ref[...], g_ref[...])
        dx_ref[...] = dres_ref[...] + dxn
        gg_ref[...] += dg
        if side:
            @pl.when(i == nt - 1)
            def _():
                side.finish(s_in, s_out, *s_scr)

    row = lambda width: pl.BlockSpec((t, width), lambda i: (i, 0))
    si_specs, so_specs, so_shapes, s_scratch, s_ins = _side_specs(side)
    assert sum(dy.shape[1] for dy in dys) == w.shape[0]
    return _pc(
        body, name=name, grid=(nt,),
        in_specs=[row(dy.shape[1]) for dy in dys] + [_const(w.shape), row(D), _const((1, D)), row(D)] + si_specs,
        out_specs=[row(D), _acc((1, D))] + so_specs,
        out_shape=[SDS((s, D), F32), SDS((1, D), F32)] + so_shapes, scratch_shapes=s_scratch,
        compiler_params=_cp(1))(*dys, w, xin, gain, dres, *s_ins)


def mix_bwd(dh1, mx, z, ya, yb, g2, w_mix, a_out, b_out, t):
    s = dh1.shape[0]

    def body(dh_ref, mx_ref, ga_ref, gb_ref, ya_ref, yb_ref, g2_ref, wm_ref, ao_ref, bo_ref,
             dmx_ref, dya_ref, dyb_ref, dga_ref, dgb_ref, dsa_ref, dob_ref, gg2_ref):
        @pl.when(pl.program_id(0) == 0)
        def _():
            gg2_ref[...] = jnp.zeros((1, D), F32)

        dmx, dg2 = _rms_bwd(dh_ref[...], mx_ref[...].astype(F32), g2_ref[...])
        gg2_ref[...] += dg2
        dmxb = dmx.astype(BF)
        dmx_ref[...] = dmxb
        dmp = _dot_nt(dmxb, wm_ref[...])

        def gate(g_ref, y_ref, dy_ref, dg_ref, w_ref, dz_ref):
            sg = jax.nn.sigmoid(g_ref[...].astype(F32))
            dyb_ = (dmp * sg).astype(BF)
            dy_ref[...] = dyb_
            dg_ref[...] = (dmp * y_ref[...].astype(F32) * sg * (1.0 - sg)).astype(BF)
            dz_ref[...] = _dot_nt(dyb_, w_ref[...]).astype(BF)

        gate(ga_ref, ya_ref, dya_ref, dga_ref, ao_ref, dsa_ref)
        gate(gb_ref, yb_ref, dyb_ref, dgb_ref, bo_ref, dob_ref)

    row = lambda w: pl.BlockSpec((t, w), lambda i: (i, 0))
    return _pc(
        body, name="mix_bwd", grid=(s // t,),
        in_specs=[row(D), row(D), pl.BlockSpec((t, D), lambda i: (i, 4)), pl.BlockSpec((t, D), lambda i: (i, 5)),
                  row(D), row(D), _const((1, D)), _const((D, D)), _const((A_W, D)), _const((D, D))],
        out_specs=[row(D)] * 5 + [row(A_W), row(D), _acc((1, D))],
        out_shape=[SDS((s, D), BF)] * 5 + [SDS((s, A_W), BF), SDS((s, D), BF), SDS((1, D), F32)],
        compiler_params=_cp(1))(dh1, mx, z, z, ya, yb, g2, w_mix, a_out, b_out)


def sgu_bwd(z, dsa, ln_g, ln_b, w_cat, w_cat_t, bias_full, bdm, t):
    s = z.shape[0]
    nt = s // t
    nch = t // A_C

    def body(u_ref, v_ref, dsa_ref, g_ref, b_ref, wc_ref, wct_ref, bias_ref, bdm_ref,
             duv_ref, glg_ref, glb_ref, gws_ref, gbs_ref, ds_acc):
        i = pl.program_id(0)

        @pl.when(i == 0)
        def _():
            glg_ref[...] = jnp.zeros((1, A_W), F32)
            glb_ref[...] = jnp.zeros((1, A_W), F32)
            gws_ref[...] = jnp.zeros((A_C, A_G * A_C), F32)
            ds_acc[...] = jnp.zeros((A_C, A_W), F32)

        lng, bdm_ = g_ref[...], bdm_ref[...]
        rec = [_sgu_recompute(v_ref[pl.ds(ci * A_C, A_C), :].astype(F32), lng, b_ref[...]) for ci in range(nch)]
        spread_vn = _sgu_spread([r_[3] for r_ in rec], bdm_)
        mixed = _dot(_sgu_weights(wc_ref, False), spread_vn)
        dsas, dss, dgus = [], [], []
        for ci in range(nch):
            rows = pl.ds(ci * A_C, A_C)
            gu, dgu = _gelu_and_grad(u_ref[rows, :].astype(F32))
            dsa_ = dsa_ref[rows, :].astype(F32)
            ds = dsa_ * gu
            ds_acc[...] += ds
            dsas.append(dsa_)
            dgus.append(dgu)
            dss.append(ds.astype(BF))
        r = lax.broadcasted_iota(jnp.int32, (A_C, A_G * A_C), 0)
        c = lax.broadcasted_iota(jnp.int32, (A_C, A_G * A_C), 1) & (A_C - 1)
        gws_ref[...] += jnp.where(c <= r, _dot_nt(jnp.concatenate(dss, axis=1), spread_vn), 0.0)
        dvns = _dot(_sgu_weights(wct_ref, True), _sgu_spread(dss, bdm_))
        for ci in range(nch):
            rows = pl.ds(ci * A_C, A_C)
            dgv, rstd, xhat, _ = rec[ci]
            dvn = dvns[:, ci * A_W:(ci + 1) * A_W]
            glb_ref[...] += jnp.sum(dvn, axis=0, keepdims=True)
            glg_ref[...] += jnp.sum(dvn * xhat, axis=0, keepdims=True)
            dxh = dvn * lng
            dgv_ = rstd * (dxh - jnp.mean(dxh, axis=-1, keepdims=True)
                           - xhat * jnp.mean(dxh * xhat, axis=-1, keepdims=True))
            s_ = mixed[:, ci * A_W:(ci + 1) * A_W] + bias_ref[...]
            duv_ref[rows, :A_W] = (dsas[ci] * s_ * dgus[ci]).astype(BF)
            duv_ref[rows, A_W:] = (dgv_ * dgv).astype(BF)

        @pl.when(i == nt - 1)
        def _():
            acc = ds_acc[...]
            for g in range(A_G):
                gbs_ref[:, g:g + 1] = jnp.sum(acc[:, g * A_GD:(g + 1) * A_GD], axis=1, keepdims=True)

    return _pc(
        body, name="sgu_bwd", grid=(nt,),
        in_specs=[pl.BlockSpec((t, A_W), lambda i: (i, 0)), pl.BlockSpec((t, A_W), lambda i: (i, 1)),
                  pl.BlockSpec((t, A_W), lambda i: (i, 0)),
                  _const((1, A_W)), _const((1, A_W)), _const((A_C, A_G * A_C)), _const((A_C, A_G * A_C)),
                  _const((A_C, A_W)), _const((A_G * A_C, A_W))],
        out_specs=[pl.BlockSpec((t, D), lambda i: (i, 0)), _acc((1, A_W)), _acc((1, A_W)),
                   _acc((A_C, A_G * A_C)), _acc((A_C, A_G))],
        out_shape=[SDS((s, D), BF), SDS((1, A_W), F32), SDS((1, A_W), F32), SDS((A_C, A_G * A_C), F32),
                   SDS((A_C, A_G), F32)],
        scratch_shapes=[pltpu.VMEM((A_C, A_W), F32)],
        compiler_params=_cp(1))(z, z, dsa, ln_g, ln_b, w_cat, w_cat_t, bias_full, bdm)


def gla_bwd(z, qk32, zl, o, dob, states, wgk, bias, wn, ltri, ltri_t, t, side=None):
    s = z.shape[0]
    nt = s // t
    nc = t // B_C
    hpb = GLA_HPB
    assert hpb == B_H
    kw, vw = hpb * B_HK, hpb * B_HV

    def body(*refs):
        ins_, s_in, outs_, s_out, scr_, s_scr = _split_side(refs, 12, 7, 5, side)
        qk_ref, v_ref, og_ref, lr_ref, o_ref, dob_ref, st_ref, wgk_ref, bias_ref, wn_ref, l_ref, lt_ref = ins_
        dqk_ref, dv_ref, dog_ref, dpre_ref, dlr_ref, gbias_ref, gwn_ref = outs_
        dst_sc, dv_sc, dqd_sc, dkt_sc, ddec_sc = scr_
        i = pl.program_id(0)
        g = pl.program_id(1)

        @pl.when((i == 0) & (g == 0))
        def _():
            gbias_ref[...] = jnp.zeros((B_H, 1, B_HK), F32)
            gwn_ref[...] = jnp.zeros((1, B_HV), F32)
            if side:
                side.start(s_in, s_out, *s_scr)

        @pl.when(i == 0)
        def _():
            for hh in range(hpb):
                dst_sc[g * hpb + hh] = jnp.zeros((B_HV, B_HK), F32)

        lr, l, lt = lr_ref[...], l_ref[...], lt_ref[...]
        keep, keep_t = l > 0, lt > 0
        wn_ = wn_ref[...]
        last = lax.broadcasted_iota(jnp.int32, (nc, B_C, B_HK), 1) == B_C - 1
        for hh in range(hpb):
            h = g * hpb + hh
            cv, ck = slice(hh * B_HV, (hh + 1) * B_HV), slice(hh * B_HK, (hh + 1) * B_HK)
            pre, b, bl, eb, enb, etb, qd, ki, kt = _gla_decays(qk_ref[:, cv], lr, wgk_ref[:, ck], bias_ref[:, ck], l, t)
            qdb, kib, ktb = qd.astype(BF), ki.astype(BF), kt.astype(BF)
            vb = v_ref[:, cv]
            o_ = o_ref[:, cv].astype(F32)
            og = og_ref[:, cv].astype(F32)
            sog = jax.nn.sigmoid(og)
            dob_ = dob_ref[:, cv].astype(F32)
            don = dob_ * og * sog
            do, dwn = _rms_bwd(don, o_, wn_)
            gwn_ref[...] += dwn
            dog_ref[:, cv] = (dob_ * _rms(o_, wn_) * sog * (1.0 + og * (1.0 - sog))).astype(BF)
            dob16 = do.astype(BF)
            sc_t = jnp.where(keep_t, _dot_nt(kib, qdb), 0.0).astype(BF)
            dsc = jnp.where(keep, _dot_nt(dob16, vb), 0.0).astype(BF)
            dsc_t = jnp.where(keep_t, _dot_nt(vb, dob16), 0.0).astype(BF)
            dv_sc[hh] = _dot(sc_t, dob16)
            dqd_sc[hh] = _dot(dsc, kib)
            dki = _dot(dsc_t, qdb)
            for n in reversed(range(nc)):
                rows = slice(n * B_C, (n + 1) * B_C)
                dst = dst_sc[h]
                dstb = dst.astype(BF)
                stp = st_ref[n, hh]
                dv_sc[hh, rows, :] += _dot_nt(ktb[rows], dstb)
                dkt_sc[hh, rows, :] = _dot(vb[rows], dstb)
                dqd_sc[hh, rows, :] += _dot(dob16[rows], stp)
                dec = jnp.exp(bl[n * B_C:n * B_C + 1, :])
                ddec_sc[hh, n] = jnp.sum(dst * stp.astype(F32), axis=0, keepdims=True) * dec
                dst_sc[h] = dst * dec + _dot_tn(dob16[rows], qdb[rows])
            dqd, dkt = dqd_sc[hh], dkt_sc[hh]
            dv_ref[:, cv] = dv_sc[hh].astype(BF)
            dqk_ref[:, hh * B_HV:hh * B_HV + B_HK] = (dqd * eb * (B_HK ** -0.5)).astype(BF)
            dqk_ref[:, hh * B_HV + B_HK:(hh + 1) * B_HV] = (dki * enb + dkt * etb).astype(BF)
            dktkt = dkt * kt
            db3 = (dqd * qd - dki * ki - dktkt).reshape(nc, B_C, B_HK)
            dbl = jnp.sum(dktkt.reshape(nc, B_C, B_HK), axis=1, keepdims=True) + ddec_sc[hh]
            db = (db3 + jnp.where(last, dbl, 0.0)).reshape(t, B_HK)
            dla = _ldot3(lt, db)
            dpre = dla * (1.0 / 16.0) * (1.0 - jax.nn.sigmoid(pre))
            dpreb = dpre.astype(BF)
            dpre_ref[:, ck] = dpreb
            gbias_ref[h] += jnp.sum(dpre, axis=0, keepdims=True)
            dlr_h = _dot_nt(dpreb, wgk_ref[:, ck])
            dlr = dlr_h if hh == 0 else dlr + dlr_h
        dlr_ref[...] = dlr.astype(BF)
        if side:
            @pl.when((i == nt - 1) & (g == B_H // hpb - 1))
            def _():
                side.finish(s_in, s_out, *s_scr)

    rv = lambda i: nt - 1 - i
    si_specs, so_specs, so_shapes, s_scratch, s_ins = _side_specs(side)
    vo, go = 2048 // vw, 3072 // vw
    tile = lambda off: pl.BlockSpec((t, vw), lambda i, g: (rv(i), off + g))
    return _pc(
        body, name="gla_bwd", grid=(nt, B_H // hpb),
        in_specs=[tile(0), tile(vo), tile(go), pl.BlockSpec((t, LANE), lambda i, g: (rv(i), 0)), tile(0), tile(0),
                  pl.BlockSpec((nc, hpb, B_HV, B_HK), lambda i, g: (rv(i), g, 0, 0)),
                  pl.BlockSpec((LANE, kw), lambda i, g: (0, g)), pl.BlockSpec((1, kw), lambda i, g: (0, g)),
                  _const((1, B_HV)), _const((t, t)), _const((t, t))] + si_specs,
        out_specs=[tile(0), tile(0), tile(0), pl.BlockSpec((t, kw), lambda i, g: (rv(i), g)),
                   pl.BlockSpec((t, LANE), lambda i, g: (rv(i), 0)), _acc((B_H, 1, B_HK)), _acc((1, B_HV))] + so_specs,
        out_shape=[SDS((s, D), BF), SDS((s, D), BF), SDS((s, D), BF), SDS((s, B_H * B_HK), BF), SDS((s, LANE), BF),
                   SDS((B_H, 1, B_HK), F32), SDS((1, B_HV), F32)] + so_shapes,
        scratch_shapes=[pltpu.VMEM((B_H, B_HV, B_HK), F32), pltpu.VMEM((hpb, t, B_HV), F32),
                        pltpu.VMEM((hpb, t, B_HK), F32), pltpu.VMEM((hpb, t, B_HK), F32),
                        pltpu.VMEM((hpb, nc, 1, B_HK), F32)] + s_scratch,
        compiler_params=_cp(2))(qk32, z, z, zl, o, dob, states, wgk, bias, wn, ltri, ltri_t, *s_ins)


def mm_tn(a, b, name, tk=2048):
    s, m = a.shape
    n = b.shape[1]
    bn = next(c for c in (1024, 1408, 512, 256, 128) if n % c == 0 and m * c * 4 <= 6 * 1024 * 1024)
    tk = min(tk, s)
    nk = s // tk

    def body(a_ref, b_ref, o_ref, acc):
        k = pl.program_id(1)

        @pl.when(k == 0)
        def _():
            acc[...] = jnp.zeros((m, bn), F32)

        acc[...] += _dot_tn(a_ref[...].astype(BF), b_ref[...])

        @pl.when(k == nk - 1)
        def _():
            o_ref[...] = acc[...].astype(BF)

    return _pc(
        body, name=name, grid=(n // bn, nk),
        in_specs=[pl.BlockSpec((tk, m), lambda j, k: (k, 0)), pl.BlockSpec((tk, bn), lambda j, k: (k, j))],
        out_specs=pl.BlockSpec((m, bn), lambda j, k: (0, j)),
        out_shape=SDS((m, n), BF), scratch_shapes=[pltpu.VMEM((m, bn), F32)], compiler_params=_cp(2))(a, b)


def _adamw(w, g, m, v):
    m = ADAM_B1 * m + (1.0 - ADAM_B1) * g
    v = ADAM_B2 * v + (1.0 - ADAM_B2) * (g * g)
    m_hat = m / (1.0 - ADAM_B1 ** ADAM_STEP)
    v_hat = v / (1.0 - ADAM_B2 ** ADAM_STEP)
    return -ADAM_LR * (m_hat / (jnp.sqrt(v_hat) + ADAM_EPS) + ADAM_WD * w), m, v


def _half_rows(rows):
    rh = rows // 2
    return rh, max(b for b in range(16, 257, 16) if rh % b == 0)


def _pc_sp(body, grid, in_specs, out_specs, out_shape, name):
    gs = pltpu.PrefetchScalarGridSpec(num_scalar_prefetch=1, grid=grid, in_specs=in_specs, out_specs=out_specs)
    return _pc(body, grid_spec=gs, out_shape=out_shape, name=name, compiler_params=_cp(len(grid)))


def adamw_halves(sc, own, sib, w, m, v, name):
    rows, cols = w.shape
    rh, br = _half_rows(rows)
    nbk = rh // br

    def body(sc_ref, own_ref, sib_ref, w_ref, m_ref, v_ref, go_ref, d_ref, mo_ref, vo_ref):
        g_ = jnp.where(pl.program_id(0) // nbk == sc_ref[0], own_ref[...], sib_ref[...])
        go_ref[...] = g_
        d_ref[...], mo_ref[...], vo_ref[...] = _adamw(w_ref[...], g_, m_ref[...], v_ref[...])

    half = pl.BlockSpec((br, cols), lambda i, sc_: (i % nbk, 0))
    blk = pl.BlockSpec((br, cols), lambda i, sc_: (i, 0))
    return _pc_sp(body, (2 * nbk,), [half, half, blk, blk, blk], [blk] * 4, [SDS((rows, cols), F32)] * 4,
                  name)(sc, own, sib, w, m, v)


def adamw_cols(sc, own, sib, w, m, v, name, cb=256):
    rows, cols = w.shape
    nk = cols // 2 // cb

    def body(sc_ref, own_ref, sib_ref, w_ref, m_ref, v_ref, go_ref, d_ref, mo_ref, vo_ref):
        g_ = jnp.where(pl.program_id(0) == sc_ref[0], own_ref[...], sib_ref[...])
        go_ref[...] = g_
        d_ref[...], mo_ref[...], vo_ref[...] = _adamw(w_ref[...], g_, m_ref[...], v_ref[...])

    half = pl.BlockSpec((rows, cb), lambda h, k, sc_: (0, k))
    blk = pl.BlockSpec((rows, cb), lambda h, k, sc_: (0, h * nk + k))
    return _pc_sp(body, (2, nk), [half, half, blk, blk, blk], [blk] * 4, [SDS((rows, cols), F32)] * 4,
                  name)(sc, own, sib, w, m, v)


def adamw_small(g, w, m, v):
    def body(g_ref, w_ref, m_ref, v_ref, d_ref, mo_ref, vo_ref):
        d_ref[...], mo_ref[...], vo_ref[...] = _adamw(w_ref[...], g_ref[...], m_ref[...], v_ref[...])

    vm = pl.BlockSpec(memory_space=pltpu.VMEM)
    return _pc(body, name="adamw_small", in_specs=[vm] * 4, out_specs=[vm] * 3, out_shape=[SDS(g.shape, F32)] * 3,
               compiler_params=pltpu.CompilerParams(vmem_limit_bytes=VMEM_LIMIT))(g, w, m, v)


def _pos():
    return lax.axis_index("x"), lax.axis_index("y"), lax.axis_index("c")


def _other_chips(x, y):
    return [(1 - x, y), (x, 1 - y), (1 - x, 1 - y)]


_ANY = pl.BlockSpec(memory_space=pltpu.HBM)


class _Side:
    def __init__(self, ins, out_shapes, nsem, start, finish):
        self.ins, self.out_shapes, self.start, self.finish = list(ins), list(out_shapes), start, finish
        self.scratch = [pltpu.SemaphoreType.DMA((nsem,)), pltpu.SemaphoreType.DMA((nsem,))]
        self.n_in, self.n_out = len(self.ins), len(self.out_shapes)


def _run_side(side, name):
    def body(*refs):
        args_ = (refs[:side.n_in], refs[side.n_in:side.n_in + side.n_out], *refs[side.n_in + side.n_out:])
        side.start(*args_)
        side.finish(*args_)

    return _pc(body, name=name, in_specs=[_ANY] * side.n_in, out_specs=[_ANY] * side.n_out,
               out_shape=side.out_shapes, scratch_shapes=side.scratch)(*side.ins)


def _split_rows(shape):
    return (shape[0] // 2) % 16 == 0


def _core_halves(shape, c):
    if _split_rows(shape):
        h = shape[0] // 2
        return ((pl.ds(pl.multiple_of(c * h, 16), h), slice(None)),
                (pl.ds(pl.multiple_of((1 - c) * h, 16), h), slice(None)))
    h = shape[1] // 2
    assert h % LANE == 0
    return ((slice(None), pl.ds(pl.multiple_of(c * h, LANE), h)),
            (slice(None), pl.ds(pl.multiple_of((1 - c) * h, LANE), h)))


def gather_side(bigs, tinies):
    nb, nt_ = len(bigs), len(tinies)

    def plan(ins, outs, ssem, rsem):
        x, y, c = _pos()
        me = 2 * x + y
        chips = _other_chips(x, y)
        sibling = (x, y, 1 - c)

        def copy(k, src, dst, to):
            return pltpu.make_async_remote_copy(src_ref=src, dst_ref=dst, send_sem=ssem.at[k], recv_sem=rsem.at[k],
                                                device_id=to, device_id_type=MESH)

        sends, landed, passed_on, tiny_landed = [], [], [], []
        for w in range(nb):
            mine, theirs = _core_halves(bigs[w].shape, c)
            for j, (cx, cy) in enumerate(chips):
                sends.append(copy(6 * w + j, ins[w].at[mine], outs[w].at[(me,) + mine], (cx, cy, c)))
                blk = outs[w].at[(2 * cx + cy,) + mine]
                landed.append((copy(6 * w + j, blk, blk, (cx, cy, c)), copy(6 * w + 3 + j, blk, blk, sibling)))
                blk = outs[w].at[(2 * cx + cy,) + theirs]
                passed_on.append(copy(6 * w + 3 + j, blk, blk, sibling))
        for w in range(nt_):
            for j, (cx, cy) in enumerate(chips):
                k = 6 * nb + 3 * w + j
                sends.append(copy(k, ins[nb + w], outs[nb + w].at[me], (cx, cy, c)))
                blk = outs[nb + w].at[2 * cx + cy]
                tiny_landed.append(copy(k, blk, blk, (cx, cy, c)))
        return sends, landed, passed_on, tiny_landed

    def start(ins, outs, ssem, rsem):
        for cp in plan(ins, outs, ssem, rsem)[0]:
            cp.start()

    def finish(ins, outs, ssem, rsem):
        sends, landed, passed_on, tiny_landed = plan(ins, outs, ssem, rsem)
        for arrived, forward in landed:
            arrived.wait_recv()
            forward.start()
        for arrived in tiny_landed + passed_on:
            arrived.wait_recv()
        for cp in sends + [forward for _, forward in landed]:
            cp.wait_send()

    return _Side(list(bigs) + list(tinies), [SDS((4,) + a.shape, a.dtype) for a in list(bigs) + list(tinies)],
                 6 * nb + 3 * nt_, start, finish)


def swap_halves(gs, name):
    n = len(gs)

    def half_shape(g):
        l, r, cols = g.shape
        return (l, r // 2, cols) if _split_rows((r, cols)) else (l, r, cols // 2)

    def body(*refs):
        g_refs, sib_refs = refs[:n], refs[n:2 * n]
        ssem, rsem = refs[2 * n:]
        x, y, c = _pos()
        cps = []
        for w in range(n):
            give = (slice(None),) + _core_halves(gs[w].shape[1:], c)[1]
            cp = pltpu.make_async_remote_copy(src_ref=g_refs[w].at[give], dst_ref=sib_refs[w], send_sem=ssem.at[w],
                                              recv_sem=rsem.at[w], device_id=(x, y, 1 - c), device_id_type=MESH)
            cp.start()
            cps.append(cp)
        for cp in cps:
            cp.wait()

    return _pc(body, name=name, in_specs=[_ANY] * n, out_specs=[_ANY] * n,
               out_shape=[SDS(half_shape(g), g.dtype) for g in gs],
               scratch_shapes=[pltpu.SemaphoreType.DMA((n,)), pltpu.SemaphoreType.DMA((n,))])(*gs)


COL_BLOCK = 256


def add_half(sc, g, sib, name):
    l, r, cols = g.shape

    def body(sc_ref, g_ref, s_ref, o_ref):
        o_ref[...] = (g_ref[...].astype(F32) + s_ref[...].astype(F32)).astype(BF)

    if _split_rows((r, cols)):
        rh, br = _half_rows(r)
        nbk = rh // br
        blk = pl.BlockSpec((1, br, cols), lambda j, i, sc_: (j, i, 0))
        mine = pl.BlockSpec((1, br, cols), lambda j, i, sc_: (j, sc_[0] * nbk + i, 0))
        return _pc_sp(body, (l, nbk), [mine, blk], blk, SDS((l, rh, cols), BF), name)(sc, g, sib)
    nbk = cols // 2 // COL_BLOCK
    blk = pl.BlockSpec((1, r, COL_BLOCK), lambda j, i, sc_: (j, 0, i))
    mine = pl.BlockSpec((1, r, COL_BLOCK), lambda j, i, sc_: (j, 0, sc_[0] * nbk + i))
    return _pc_sp(body, (l, nbk), [mine, blk], blk, SDS((l, r, cols // 2), BF), name)(sc, g, sib)


def exchange_side(ps):
    n_ = len(ps)

    def width(p_):
        return p_.shape[2] if p_.shape[0] == 4 else p_.shape[2] // 4

    def plan(p_refs, got_refs, ssem, rsem):
        x, y, c = _pos()
        cps = []
        for w in range(n_):
            wd = width(ps[w])
            for j, (cx, cy) in enumerate(_other_chips(x, y)):
                to = 2 * cx + cy
                src = p_refs[w].at[to] if ps[w].shape[0] == 4 else p_refs[w].at[0, :, pl.ds(pl.multiple_of(to * wd, LANE), wd)]
                cps.append(pltpu.make_async_remote_copy(
                    src_ref=src, dst_ref=got_refs[w].at[j], send_sem=ssem.at[3 * w + j], recv_sem=rsem.at[3 * w + j],
                    device_id=(cx, cy, c), device_id_type=MESH))
        return cps

    def start(*refs):
        for cp in plan(*refs):
            cp.start()

    def finish(*refs):
        for cp in plan(*refs):
            cp.wait()

    return _Side(ps, [SDS((3, p_.shape[1], width(p_)), p_.dtype) for p_ in ps], 3 * n_, start, finish)


def sum4(sc, p, got, name):
    _, rh, wd = got.shape

    def body(sc_ref, p_ref, g_ref, r_ref):
        r_ref[...] = ((p_ref[0].astype(F32) + g_ref[0].astype(F32)) + (g_ref[1].astype(F32) + g_ref[2].astype(F32)))

    if rh % 16:
        assert p.shape[0] == 4
        return _pc_sp(body, (wd // COL_BLOCK,),
                      [pl.BlockSpec((1, rh, COL_BLOCK), lambda i, sc_: (sc_[1], 0, i)),
                       pl.BlockSpec((3, rh, COL_BLOCK), lambda i, sc_: (0, 0, i))],
                      pl.BlockSpec((rh, COL_BLOCK), lambda i, sc_: (0, i)), SDS((rh, wd), F32), name)(sc, p, got)
    _, br = _half_rows(2 * rh)
    own = (pl.BlockSpec((1, br, wd), lambda i, sc_: (sc_[1], i, 0)) if p.shape[0] == 4
           else pl.BlockSpec((1, br, wd), lambda i, sc_: (0, i, sc_[1])))
    return _pc_sp(body, (rh // br,), [own, pl.BlockSpec((3, br, wd), lambda i, sc_: (0, i, 0))],
                  pl.BlockSpec((br, wd), lambda i, sc_: (i, 0)), SDS((rh, wd), F32), name)(sc, p, got)


def join_halves(halves):
    n = len(halves)

    def body(*refs):
        h_refs, got_refs = refs[:n], refs[n:2 * n]
        ssem, rsem = refs[2 * n:]
        x, y, c = _pos()
        cps = []
        for w in range(n):
            cp = pltpu.make_async_remote_copy(src_ref=h_refs[w], dst_ref=got_refs[w], send_sem=ssem.at[w],
                                              recv_sem=rsem.at[w], device_id=(x, y, 1 - c), device_id_type=MESH)
            cp.start()
            cps.append(cp)
        for cp in cps:
            cp.wait()

    return _pc(body, name="join_halves", in_specs=[_ANY] * n, out_specs=[_ANY] * n,
               out_shape=[SDS(h.shape, h.dtype) for h in halves],
               scratch_shapes=[pltpu.SemaphoreType.DMA((n,)), pltpu.SemaphoreType.DMA((n,))])(*halves)


def allreduce_small(g):
    rows = g.shape[0]
    rh = rows // 2

    def body(g_ref, out_ref, sib_buf, chip_buf, sum_sc, ssem, rsem):
        x, y, c = _pos()
        me = 2 * x + y
        sibling = (x, y, 1 - c)
        mine = pl.ds(pl.multiple_of(c * rh, 8), rh)

        def copy(k, src, dst, to):
            return pltpu.make_async_remote_copy(src_ref=src, dst_ref=dst, send_sem=ssem.at[k], recv_sem=rsem.at[k],
                                                device_id=to, device_id_type=MESH)

        cp = copy(0, g_ref, sib_buf, sibling)
        cp.start()
        cp.wait()
        sum_sc[...] = g_ref[...] + sib_buf[...]
        chips = _other_chips(x, y)
        cps = [copy(1 + j, sum_sc.at[mine], chip_buf.at[me], (cx, cy, c)) for j, (cx, cy) in enumerate(chips)]
        for cp in cps:
            cp.start()
        chip_buf[me] = sum_sc[mine, :]
        for j, (cx, cy) in enumerate(chips):
            copy(1 + j, sum_sc.at[mine], chip_buf.at[2 * cx + cy], (cx, cy, c)).wait_recv()
        for cp in cps:
            cp.wait_send()
        out_ref[mine, :] = (chip_buf[0] + chip_buf[1]) + (chip_buf[2] + chip_buf[3])
        cp = copy(4, out_ref.at[mine], out_ref.at[mine], sibling)
        cp.start()
        cp.wait()

    vm = pl.BlockSpec(memory_space=pltpu.VMEM)
    return _pc(body, name="allreduce_small", in_specs=[vm], out_specs=vm, out_shape=SDS((rows, LANE), F32),
               scratch_shapes=[pltpu.VMEM((rows, LANE), F32), pltpu.VMEM((4, rh, LANE), F32), pltpu.VMEM((rows, LANE), F32),
                               pltpu.SemaphoreType.DMA((5,)), pltpu.SemaphoreType.DMA((5,))],
               compiler_params=pltpu.CompilerParams(vmem_limit_bytes=VMEM_LIMIT))(g)


def _pack_small(entries, get):
    flat = jnp.concatenate([get(n).reshape(-1).astype(F32) for n, _ in entries])
    rows = -(-flat.shape[0] // (8 * LANE)) * 8
    return jnp.pad(flat, (0, rows * LANE - flat.shape[0])).reshape(rows, LANE)


def _unpack_small(entries, packed):
    out, off = {}, 0
    flat = packed.reshape(-1)
    for name, n in entries:
        out[name] = flat[off:off + n]
        off += n
    return out


def _cols_full(blk):
    return blk.transpose(1, 0, 2).reshape(blk.shape[1], 4 * blk.shape[2])


def kernel(x, p, pre_mix_norm, w_in, a_ln_g, a_ln_b, a_spatial_w, a_spatial_b, a_out, b_gk, b_gk_bias, b_out_norm, b_out, w_mix_out, post_mix_norm, pre_ffn_norm, w_up, conv_w, conv_b, w_down, post_ffn_norm, w_ple, w_ple_gate, post_ple_norm, loss_target, m_pre_mix_norm, m_w_in, m_a_ln_g, m_a_ln_b, m_a_spatial_w, m_a_spatial_b, m_a_out, m_b_gk, m_b_gk_bias, m_b_out_norm, m_b_out, m_w_mix_out, m_post_mix_norm, m_pre_ffn_norm, m_w_up, m_conv_w, m_conv_b, m_w_down, m_post_ffn_norm, m_w_ple, m_w_ple_gate, m_post_ple_norm, v_pre_mix_norm, v_w_in, v_a_ln_g, v_a_ln_b, v_a_spatial_w, v_a_spatial_b, v_a_out, v_b_gk, v_b_gk_bias, v_b_out_norm, v_b_out, v_w_mix_out, v_post_mix_norm, v_pre_ffn_norm, v_w_up, v_conv_w, v_conv_b, v_w_down, v_post_ffn_norm, v_w_ple, v_w_ple_gate, v_post_ple_norm):
    args = dict(locals())
    order = ['pre_mix_norm', 'w_in', 'a_ln_g', 'a_ln_b', 'a_spatial_w', 'a_spatial_b', 'a_out', 'b_gk', 'b_gk_bias',
             'b_out_norm', 'b_out', 'w_mix_out', 'post_mix_norm', 'pre_ffn_norm', 'w_up', 'conv_w', 'conv_b', 'w_down',
             'post_ffn_norm', 'w_ple', 'w_ple_gate', 'post_ple_norm']
    assert sorted(BIG + TINY + tuple(n for n, _ in SMALL)) == sorted(order)
    s = x.shape[1]
    xs = x.reshape(s, D)
    ps = p.reshape(s, PLE)
    tgt = loss_target.reshape(s, D)
    t_big = min(1024, s)
    t_mid = min(512, s)
    t_small = min(256, s)
    t_gla = min(256, s)
    mx_, my_, mc_ = _pos()
    me = 2 * mx_ + my_
    sc = jnp.stack([mc_, me]).astype(jnp.int32)
    shard = lambda n: args[n].reshape(args[n].shape[1:])

    mine = {n: shard(n).astype(BF) for n in BIG}
    mine["w_in"] = shard("w_in").T.astype(BF)
    mine.update({n: shard(n) for n in TINY})
    fill = lambda names, gots: {n: lax.dynamic_update_slice(got, mine[n][None], (me, 0, 0)) for n, got in zip(names, gots)}
    first = ("w_in",) + TINY
    full = fill(first, _run_side(gather_side([mine["w_in"]], [mine[n] for n in TINY]), "gather_first"))
    wi = full["w_in"].reshape(4 * 1540, D)
    seg = lambda a, b: wi[a:b]
    qk = [seg(1024 + h * B_HK, 1024 + (h + 1) * B_HK) for h in range(B_H)]
    kk = [seg(1536 + h * B_HK, 1536 + (h + 1) * B_HK) for h in range(B_H)]
    w_z = jnp.concatenate([seg(0, 1024)] + [m_ for h in range(B_H) for m_ in (qk[h], kk[h])]
                          + [seg(2048, 4096), seg(4112, 6160), seg(4096, 4112), jnp.zeros((LANE - B_RANK, D), BF)], axis=0)
    wgk = jnp.pad(_cols_full(full["b_gk"]).astype(BF), ((0, LANE - B_RANK), (0, 0)))
    w_conv = _cols_full(full["conv_w"])
    g1, g2, g3 = pre_mix_norm.reshape(1, D), post_mix_norm.reshape(1, D), pre_ffn_norm.reshape(1, D)
    g4, g5 = post_ffn_norm.reshape(1, D), post_ple_norm.reshape(1, D)
    ln_g, ln_b = a_ln_g.reshape(1, A_W), a_ln_b.reshape(1, A_W)
    w_s = a_spatial_w.reshape(A_G, A_C, A_C)
    w_cat = w_s.transpose(1, 0, 2).reshape(A_C, A_G * A_C)
    w_cat_t = w_s.transpose(2, 0, 1).reshape(A_C, A_G * A_C)
    bias_full = jnp.repeat(a_spatial_b.reshape(A_G, A_C).T, A_GD, axis=1)
    bdm = (jnp.arange(A_G * A_C)[:, None] // A_C == jnp.arange(A_W)[None, :] // A_GD).astype(BF)
    gk_bias = b_gk_bias.reshape(1, B_H * B_HK)
    wn = b_out_norm.reshape(1, B_HV)
    cb = conv_b.reshape(1, 2 * D_FF)
    idx = jnp.arange(t_gla)
    ltri = ((idx[:, None] // B_C == idx[None, :] // B_C) & (idx[None, :] <= idx[:, None])).astype(BF)

    a, z, qk32, zl, *gots = norm_matmul(xs, g1, w_z, D, t_big, "in_proj", nblk=6, f32_blk=1, tail_blk=48,
                                        side=gather_side([mine[n] for n in BIG[1:]], []))
    full.update(fill(BIG[1:], gots))
    w_aout, w_ple_f = _cols_full(full["a_out"]), _cols_full(full["w_ple"])
    w_bout, w_mix, w_pg = (full[n].reshape(D, D) for n in ("b_out", "w_mix_out", "w_ple_gate"))
    w_dn, w_up3 = full["w_down"].reshape(D_FF, D), full["w_up"]
    sa = sgu_fwd(z, ln_g, ln_b, w_cat, bias_full, bdm, t_mid)
    ob, o, states = gla_fwd(z, qk32, zl, wgk, gk_bias, wn, ltri, t_gla)
    ya, yb, mp, mx, h1 = mix_fwd(sa, ob, z, xs, w_aout, w_bout, w_mix, g2, t_mid)
    c, up_g, up_v, cg, cv, ff = ffn_up_fwd(h1, g3, w_up3, w_conv, cb, t_mid)
    f, h2, pg, pe, dy, loss = out_fwd(ff, h1, ps, tgt, w_dn, w_pg, w_ple_f, g4, g5, t_mid)

    dh2, dpe, dpg, df, dff, gg5, gg4 = out_bwd(dy, pg, pe, f, g5, g4, w_pg, w_dn, t_mid)
    dup_g, dup_v, gcw, gcb, dh1, gg3 = ffn_up_bwd(up_g, up_v, cg, cv, dff, w_conv, _cols_full(w_up3), h1, g3, dh2,
                                                  t_small)
    dmx, dya, dyb, dga, dgb, dsa, dob, gg2 = mix_bwd(dh1, mx, z, ya, yb, g2, w_mix, w_aout, w_bout, t_mid)
    duv, g_lng, g_lnb, g_wcat, g_bst = sgu_bwd(z, dsa, ln_g, ln_b, w_cat, w_cat_t, bias_full, bdm, t_mid)
    g_ws = g_wcat.reshape(A_C, A_G, A_C).transpose(1, 0, 2)

    grads = {
        "a_out": mm_tn(sa, dya, "dw_a_out")[None],
        "b_out": mm_tn(ob, dyb, "dw_b_out").reshape(4, D // 4, D),
        "w_mix_out": mm_tn(mp, dmx, "dw_mix").reshape(4, D // 4, D),
        "w_up": jnp.concatenate([mm_tn(c, dup_g, "dw_up_g"), mm_tn(c, dup_v, "dw_up_v")], axis=1)[None],
        "w_down": mm_tn(ff, df, "dw_down").reshape(4, D_FF // 4, D),
        "w_ple": mm_tn(ps, dpe, "dw_ple")[None],
        "w_ple_gate": mm_tn(h2, dpg, "dw_ple_gate").reshape(4, D // 4, D),
    }

    def chip_partials(names):
        gs = [grads[n] for n in names]
        return [add_half(sc, g, sib, "partial_" + n)
                for n, g, sib in zip(names, gs, swap_halves(gs, "swap_halves_" + names[0]))]

    parts = dict(zip(BIG[1:], chip_partials(BIG[1:])))
    dqk, dvb, dog, dpre, dlr, g_gkb, g_wn, *gots = gla_bwd(
        z, qk32, zl, o, dob, states, wgk, gk_bias, wn, ltri, ltri.T, t_gla,
        side=exchange_side([parts[n] for n in BIG[1:]]))
    got = dict(zip(BIG[1:], gots))
    segs = [duv, dqk, dvb, dog, dga, dgb, dlr]

    gz = [mm_tn(sg_, a, "dw_in_%d" % k) for k, sg_ in enumerate(segs)]
    gq = [gz[1][h * 256:h * 256 + B_HK] for h in range(B_H)]
    gk = [gz[1][h * 256 + B_HK:(h + 1) * 256] for h in range(B_H)]
    g_in = jnp.concatenate([gz[0]] + gq + gk + [gz[2], gz[3], gz[6][:B_RANK], gz[4], gz[5]], axis=0)
    grads["w_in"] = g_in.reshape(4, 1540, D)
    parts["w_in"], = chip_partials(("w_in",))
    dx, gg1, got["w_in"] = nt_normbwd(segs, w_z, xs, g1, dh1, t_small, "in_bwd",
                                      side=exchange_side([parts["w_in"]]))

    reds = [sum4(sc, parts[n], got[n], "sum_" + n) for n in BIG]
    outs = {}
    for n, red, sib in zip(BIG, reds, join_halves(reds)):
        if n == "w_in":
            res = adamw_cols(sc, red, sib, shard(n).T, shard("m_" + n).T, shard("v_" + n).T, "adamw_" + n)
            res = [r_.T for r_ in res]
        else:
            res = adamw_halves(sc, red, sib, shard(n), shard("m_" + n), shard("v_" + n), "adamw_" + n)
        outs[n] = [r_.reshape(args[n].shape) for r_ in res]

    small_g = {
        "pre_mix_norm": gg1, "a_ln_g": g_lng, "a_ln_b": g_lnb, "a_spatial_w": g_ws, "a_spatial_b": g_bst.T,
        "b_gk_bias": g_gkb, "b_out_norm": g_wn, "post_mix_norm": gg2, "pre_ffn_norm": gg3,
        "conv_b": gcb, "post_ffn_norm": gg4, "post_ple_norm": gg5,
        "b_gk": mm_tn(zl, dpre, "dw_gk")[:B_RANK], "conv_w": gcw,
        "loss": loss,
    }
    red_entries = SMALL + (("b_gk", B_RANK * 512), ("conv_w", 3 * 2 * D_FF), ("loss", 1))
    g_fin = _unpack_small(red_entries, allreduce_small(_pack_small(red_entries, lambda n: small_g[n])))
    g_fin["b_gk"] = lax.dynamic_slice(g_fin["b_gk"].reshape(B_RANK, 512), (0, me * B_HK), (B_RANK, B_HK))
    g_fin["conv_w"] = lax.dynamic_slice(g_fin["conv_w"].reshape(3, 2 * D_FF), (0, me * 1408), (3, 1408))
    upd_entries = SMALL + (("b_gk", B_RANK * B_HK), ("conv_w", 3 * 1408))
    res = adamw_small(*[_pack_small(upd_entries, get) for get in
                        (lambda n: g_fin[n], lambda n: args[n], lambda n: args["m_" + n], lambda n: args["v_" + n])])
    res = [_unpack_small(upd_entries, r_) for r_ in res]
    for n, _ in upd_entries:
        outs[n] = [r_[n].reshape(args[n].shape) for r_ in [g_fin] + res]

    return (g_fin["loss"].reshape(()), dx.reshape(x.shape), *[outs[n][0] for n in order], *[outs[n][1] for n in order],
            *[outs[n][2] for n in order], *[outs[n][3] for n in order])
```

```python
import functools
import math

import jax
import jax.numpy as jnp
from jax import lax
from jax.experimental import pallas as pl
from jax.experimental.pallas import tpu as pltpu

F32 = jnp.float32
BF = jnp.bfloat16
SDS = jax.ShapeDtypeStruct
MESH = pl.DeviceIdType.MESH

EPS = 1e-6
D = 1024
A_W = 512
A_G, A_C = 8, 128
A_GD = A_W // A_G
B_H, B_HK, B_HV = 4, 128, 256
B_C = 64
GLA_HPB = 4
B_RANK = 16
D_FF = 2816
PLE = 256
LANE = 128
VMEM_LIMIT = 60 * 1024 * 1024

ADAM_LR, ADAM_B1, ADAM_B2, ADAM_EPS, ADAM_WD, ADAM_STEP = 0.001, 0.9, 0.999, 1e-08, 0.01, 10

_GC = math.sqrt(2.0 / math.pi)
_GA = 0.044715

BIG = ("w_in", "a_out", "b_out", "w_mix_out", "w_up", "w_down", "w_ple", "w_ple_gate")
TINY = ("b_gk", "conv_w")
SMALL = (("pre_mix_norm", 1024), ("a_ln_g", 512), ("a_ln_b", 512), ("a_spatial_w", 131072),
         ("a_spatial_b", 1024), ("b_gk_bias", 512), ("b_out_norm", 256), ("post_mix_norm", 1024),
         ("pre_ffn_norm", 1024), ("conv_b", 5632), ("post_ffn_norm", 1024), ("post_ple_norm", 1024))


def _pc(body, **kw):
    return pl.pallas_call(body, **kw)


def _cp(n):
    return pltpu.CompilerParams(dimension_semantics=("arbitrary",) * n, vmem_limit_bytes=VMEM_LIMIT)


def _const(shape):
    nd = len(shape)
    return pl.BlockSpec(shape, lambda *_: (0,) * nd, pipeline_mode=pl.Buffered(1))


def _acc(shape):
    nd = len(shape)
    return pl.BlockSpec(shape, lambda *_: (0,) * nd)


def _dot(a, b):
    return jnp.dot(a, b, preferred_element_type=F32)


def _dot_nt(a, b):
    return lax.dot_general(a, b, (((1,), (1,)), ((), ())), preferred_element_type=F32)


def _dot_tn(a, b):
    return lax.dot_general(a, b, (((0,), (0,)), ((), ())), preferred_element_type=F32)


def _gelu(x):
    return 0.5 * x * (1.0 + jnp.tanh(_GC * (x + _GA * x * x * x)))


def _gelu_and_grad(x):
    x2 = x * x
    s = 0.5 * jnp.tanh((_GC * x) * (1.0 + _GA * x2)) + 0.5
    g = x * s
    return g, s + g * (1.0 - s) * ((6.0 * _GC * _GA) * x2 + 2.0 * _GC)


def _log_sigmoid(x):
    return jnp.minimum(x, 0.0) - jnp.log(1.0 + jnp.exp(-jnp.abs(x)))


def _rms(x, g):
    return x * lax.rsqrt(jnp.mean(x * x, axis=-1, keepdims=True) + EPS) * g


def _rms_bwd(dy, x, g):
    r = lax.rsqrt(jnp.mean(x * x, axis=-1, keepdims=True) + EPS)
    n = x * r
    dn = dy * g
    dx = r * (dn - n * jnp.mean(dn * n, axis=-1, keepdims=True))
    return dx, jnp.sum(dy * n, axis=0, keepdims=True)


def _ldot3(l, x):
    h = x.astype(BF)
    r = x - h.astype(F32)
    m = r.astype(BF)
    lo = (r - m.astype(F32)).astype(BF)
    return _dot(l, h) + _dot(l, m) + _dot(l, lo)


def _split_side(refs, n_in, n_out, n_scratch, side):
    si, so = (side.n_in, side.n_out) if side else (0, 0)
    cuts = [n_in, si, n_out, so, n_scratch]
    out, at = [], 0
    for c in cuts:
        out.append(refs[at:at + c])
        at += c
    return (*out, refs[at:])


def _side_specs(side):
    return ([_ANY] * side.n_in, [_ANY] * side.n_out, side.out_shapes, side.scratch, side.ins) if side else ([],) * 5


def norm_matmul(x, g, wt, bn, t, name, nblk, f32_blk, tail_blk, side=None):
    s, dm = x.shape
    w_spec = pl.BlockSpec((bn, dm), lambda i, j: (j, 0))
    nt = s // t

    def body(*refs):
        (x_ref, g_ref, w_ref, wl_ref), s_in, outs, s_out, (a_sc,), s_scr = _split_side(refs, 4, 4, 1, side)
        a_ref, z_ref, f32_ref, tail_ref = outs
        i, j = pl.program_id(0), pl.program_id(1)
        if side:
            @pl.when((i == 0) & (j == 0))
            def _():
                side.start(s_in, s_out, *s_scr)

        @pl.when(j == 0)
        def _():
            a = _rms(x_ref[...], g_ref[...]).astype(BF)
            a_sc[...] = a
            a_ref[...] = a
            tail_ref[...] = _dot_nt(a, wl_ref[...]).astype(BF)

        acc = _dot_nt(a_sc[...], w_ref[...])
        z_ref[...] = acc.astype(BF)

        @pl.when(j == f32_blk)
        def _():
            f32_ref[...] = acc
        if side:
            @pl.when((i == nt - 1) & (j == nblk - 1))
            def _():
                side.finish(s_in, s_out, *s_scr)

    si_specs, so_specs, so_shapes, s_scratch, s_ins = _side_specs(side)
    return _pc(
        body, name=name, grid=(nt, nblk),
        in_specs=[pl.BlockSpec((t, dm), lambda i, j: (i, 0)), _const((1, dm)), w_spec,
                  pl.BlockSpec((LANE, dm), lambda i, j: (tail_blk, 0), pipeline_mode=pl.Buffered(1))] + si_specs,
        out_specs=[pl.BlockSpec((t, dm), lambda i, j: (i, 0)), pl.BlockSpec((t, bn), lambda i, j: (i, j)),
                   pl.BlockSpec((t, bn), lambda i, j: (i, 0)), pl.BlockSpec((t, LANE), lambda i, j: (i, 0))] + so_specs,
        out_shape=[SDS((s, dm), BF), SDS((s, nblk * bn), BF), SDS((s, bn), F32), SDS((s, LANE), BF)] + so_shapes,
        scratch_shapes=[pltpu.VMEM((t, dm), BF)] + s_scratch, compiler_params=_cp(2))(x, g, wt, wt, *s_ins)


def _sgu_weights(wc_ref, transposed):
    r = lax.broadcasted_iota(jnp.int32, (A_C, A_G * A_C), 0)
    c = lax.broadcasted_iota(jnp.int32, (A_C, A_G * A_C), 1) & (A_C - 1)
    return jnp.where((r <= c) if transposed else (c <= r), wc_ref[...], 0.0).astype(BF)


def _sgu_spread(xs, bdm):
    return jnp.concatenate([jnp.tile(x, (A_G, 1)) * bdm for x in xs], axis=1)


def _sgu_recompute(v, lng, lnb):
    gv, dgv = _gelu_and_grad(v)
    mu = jnp.mean(gv, axis=-1, keepdims=True)
    xc = gv - mu
    rstd = lax.rsqrt(jnp.mean(xc * xc, axis=-1, keepdims=True) + EPS)
    xhat = xc * rstd
    return dgv, rstd, xhat, (xhat * lng + lnb).astype(BF)


def sgu_fwd(z, ln_g, ln_b, w_cat, bias_full, bdm, t):
    s = z.shape[0]
    nch = t // A_C

    def body(u_ref, v_ref, g_ref, b_ref, wc_ref, bias_ref, bdm_ref, sa_ref):
        vns = [_sgu_recompute(v_ref[pl.ds(ci * A_C, A_C), :].astype(F32), g_ref[...], b_ref[...])[3]
               for ci in range(nch)]
        mixed = _dot(_sgu_weights(wc_ref, False), _sgu_spread(vns, bdm_ref[...]))
        for ci in range(nch):
            rows = pl.ds(ci * A_C, A_C)
            s_ = mixed[:, ci * A_W:(ci + 1) * A_W] + bias_ref[...]
            sa_ref[rows, :] = (_gelu(u_ref[rows, :].astype(F32)) * s_).astype(BF)

    return _pc(
        body, name="sgu_fwd", grid=(s // t,),
        in_specs=[pl.BlockSpec((t, A_W), lambda i: (i, 0)), pl.BlockSpec((t, A_W), lambda i: (i, 1)),
                  _const((1, A_W)), _const((1, A_W)), _const((A_C, A_G * A_C)), _const((A_C, A_W)),
                  _const((A_G * A_C, A_W))],
        out_specs=pl.BlockSpec((t, A_W), lambda i: (i, 0)),
        out_shape=SDS((s, A_W), BF), compiler_params=_cp(1))(z, z, ln_g, ln_b, w_cat, bias_full, bdm)


def _gla_decays(qk, lr, wgk, bias, l, t):
    nc = t // B_C
    q = qk[:, :B_HK].astype(F32) * (B_HK ** -0.5)
    k = qk[:, B_HK:].astype(F32)
    pre = _dot(lr, wgk) + bias
    la = _log_sigmoid(pre) * (1.0 / 16.0)
    b = _ldot3(l, la)
    b3 = b.reshape(nc, B_C, B_HK)
    bl = jnp.broadcast_to(b3[:, B_C - 1:B_C, :], (nc, B_C, B_HK)).reshape(t, B_HK)
    eb, enb, etb = jnp.exp(b), jnp.exp(-b), jnp.exp(bl - b)
    return pre, b, bl, eb, enb, etb, q * eb, k * enb, k * etb


def gla_fwd(z, qk32, zl, wgk, bias, wn, ltri, t):
    s = z.shape[0]
    nc = t // B_C
    hpb = GLA_HPB
    kw, vw = hpb * B_HK, hpb * B_HV

    def body(qk_ref, v_ref, og_ref, lr_ref, wgk_ref, bias_ref, wn_ref, l_ref, ob_ref, o_ref, st_ref, st_sc, o_sc):
        g = pl.program_id(1)

        @pl.when(pl.program_id(0) == 0)
        def _():
            for hh in range(hpb):
                st_sc[g * hpb + hh] = jnp.zeros((B_HV, B_HK), F32)

        lr, l = lr_ref[...], l_ref[...]
        for hh in range(hpb):
            h = g * hpb + hh
            cv, ck = slice(hh * B_HV, (hh + 1) * B_HV), slice(hh * B_HK, (hh + 1) * B_HK)
            _, _, bl, _, _, _, qd, ki, kt = _gla_decays(qk_ref[:, cv], lr, wgk_ref[:, ck], bias_ref[:, ck], l, t)
            qd, ki, kt = qd.astype(BF), ki.astype(BF), kt.astype(BF)
            vb = v_ref[:, cv]
            sc = jnp.where(l > 0, _dot_nt(qd, ki), 0.0).astype(BF)
            o_sc[hh] = _dot(sc, vb)
            for n in range(nc):
                rows = slice(n * B_C, (n + 1) * B_C)
                st = st_sc[h]
                stb = st.astype(BF)
                st_ref[n, hh] = stb
                o_sc[hh, rows, :] += _dot_nt(qd[rows], stb)
                st_sc[h] = st * jnp.exp(bl[n * B_C:n * B_C + 1, :]) + _dot_tn(vb[rows], kt[rows])
            ob = o_sc[hh].astype(BF)
            o_ref[:, cv] = ob
            og = og_ref[:, cv].astype(F32)
            ob_ref[:, cv] = (_rms(ob.astype(F32), wn_ref[...]) * og * jax.nn.sigmoid(og)).astype(BF)

    vo, go = 2048 // vw, 3072 // vw
    return _pc(
        body, name="gla_fwd", grid=(s // t, B_H // hpb),
        in_specs=[pl.BlockSpec((t, vw), lambda i, g: (i, g)), pl.BlockSpec((t, vw), lambda i, g: (i, vo + g)),
                  pl.BlockSpec((t, vw), lambda i, g: (i, go + g)), pl.BlockSpec((t, LANE), lambda i, g: (i, 0)),
                  pl.BlockSpec((LANE, kw), lambda i, g: (0, g)), pl.BlockSpec((1, kw), lambda i, g: (0, g)),
                  _const((1, B_HV)), _const((t, t))],
        out_specs=[pl.BlockSpec((t, vw), lambda i, g: (i, g)), pl.BlockSpec((t, vw), lambda i, g: (i, g)),
                   pl.BlockSpec((nc, hpb, B_HV, B_HK), lambda i, g: (i, g, 0, 0))],
        out_shape=[SDS((s, D), BF), SDS((s, D), BF), SDS((s // B_C, B_H, B_HV, B_HK), BF)],
        scratch_shapes=[pltpu.VMEM((B_H, B_HV, B_HK), F32), pltpu.VMEM((hpb, t, B_HV), F32)],
        compiler_params=_cp(2))(qk32, z, z, zl, wgk, bias, wn, ltri)


def mix_fwd(sa, ob, z, x, a_out, b_out, w_mix, g2, t):
    s = x.shape[0]

    def body(sa_ref, ob_ref, ga_ref, gb_ref, x_ref, ao_ref, bo_ref, wm_ref, g2_ref,
             ya_ref, yb_ref, mp_ref, mx_ref, h1_ref):
        ya = _dot(sa_ref[...], ao_ref[...]).astype(BF)
        yb = _dot(ob_ref[...], bo_ref[...]).astype(BF)
        ya_ref[...] = ya
        yb_ref[...] = yb
        mp = (jax.nn.sigmoid(ga_ref[...].astype(F32)) * ya.astype(F32)
              + jax.nn.sigmoid(gb_ref[...].astype(F32)) * yb.astype(F32)).astype(BF)
        mp_ref[...] = mp
        mx = _dot(mp, wm_ref[...]).astype(BF)
        mx_ref[...] = mx
        h1_ref[...] = x_ref[...] + _rms(mx.astype(F32), g2_ref[...])

    row = lambda w: pl.BlockSpec((t, w), lambda i: (i, 0))
    return _pc(
        body, name="mix_fwd", grid=(s // t,),
        in_specs=[row(A_W), row(D), pl.BlockSpec((t, D), lambda i: (i, 4)), pl.BlockSpec((t, D), lambda i: (i, 5)),
                  row(D), _const((A_W, D)), _const((D, D)), _const((D, D)), _const((1, D))],
        out_specs=[row(D)] * 5,
        out_shape=[SDS((s, D), BF)] * 4 + [SDS((s, D), F32)],
        compiler_params=_cp(1))(sa, ob, z, z, x, a_out, b_out, w_mix, g2)


def ffn_up_fwd(h1, g3, w_up3, conv_w, conv_b, t):
    s = h1.shape[0]
    bn = w_up3.shape[2]

    def body(x_ref, g_ref, wg_ref, wv_ref, cwg_ref, cwv_ref, cbg_ref, cbv_ref,
             c_ref, ug_ref, uv_ref, cg_ref, cv_ref, ff_ref, c_sc, carry):
        i, j = pl.program_id(0), pl.program_id(1)

        @pl.when(j == 0)
        def _():
            c = _rms(x_ref[...], g_ref[...]).astype(BF)
            c_sc[...] = c
            c_ref[...] = c

        @pl.when(i == 0)
        def _():
            carry[j] = jnp.zeros((2, 8, bn), F32)

        def branch(k, w_ref, cw_ref, cb_ref, u_ref, o_ref):
            ub = _dot(c_sc[...], w_ref[...]).astype(BF)
            u_ref[...] = ub
            u = ub.astype(F32)
            ext = jnp.concatenate([carry[j, k], u], axis=0)
            carry[j, k] = u[t - 8:]
            w = cw_ref[...]
            cc = (cb_ref[...] + w[0:1] * pltpu.roll(ext, 2, 0) + w[1:2] * pltpu.roll(ext, 1, 0) + w[2:3] * ext)[8:]
            cc = cc.astype(BF)
            o_ref[...] = cc
            return cc

        g = _gelu(branch(0, wg_ref, cwg_ref, cbg_ref, ug_ref, cg_ref))
        ff_ref[...] = g * branch(1, wv_ref, cwv_ref, cbv_ref, uv_ref, cv_ref)

    col = lambda rows, off: pl.BlockSpec((rows, bn), lambda i, j: (0, j + off))
    out = pl.BlockSpec((t, bn), lambda i, j: (i, j))
    return _pc(
        body, name="ffn_up_fwd", grid=(s // t, 2),
        in_specs=[pl.BlockSpec((t, D), lambda i, j: (i, 0)), _const((1, D)),
                  pl.BlockSpec((None, D, bn), lambda i, j: (j, 0, 0)), pl.BlockSpec((None, D, bn), lambda i, j: (j + 2, 0, 0)),
                  col(3, 0), col(3, 2), col(1, 0), col(1, 2)],
        out_specs=[pl.BlockSpec((t, D), lambda i, j: (i, 0))] + [out] * 5,
        out_shape=[SDS((s, D), BF)] + [SDS((s, D_FF), BF)] * 5,
        scratch_shapes=[pltpu.VMEM((t, D), BF), pltpu.VMEM((2, 2, 8, bn), F32)],
        compiler_params=_cp(2))(h1, g3, w_up3, w_up3, conv_w, conv_w, conv_b, conv_b)


def ffn_up_bwd(ug, uv, cg, cv, dff, conv_w, w_up, h1, g3, dh2, t):
    s = h1.shape[0]
    nt = s // t
    hb = t // 8
    bn = 1408
    r = t + 8

    def body(ug_ref, uv_ref, cg_ref, cv_ref, cag_ref, cav_ref, d_ref, da_ref, cw_ref, w_ref, x_ref, g_ref, dres_ref,
             dug_ref, duv_ref, gw_ref, gb_ref, dx_ref, gg_ref):
        i = pl.program_id(0)

        @pl.when(i == 0)
        def _():
            gw_ref[...] = jnp.zeros((3, 2 * D_FF), F32)
            gb_ref[...] = jnp.zeros((1, 2 * D_FF), F32)
            gg_ref[...] = jnp.zeros((1, D), F32)

        more = (i < nt - 1).astype(BF)

        def gate(c_g, c_v, d_):
            gl, dgl = _gelu_and_grad(c_g)
            return (d_ * c_v * dgl).astype(F32), (d_ * gl).astype(F32)

        def back(dc, dc_next, u_ref, cols, off, du_ref):
            w = cw_ref[:, off:off + bn]
            d_ext = jnp.concatenate([dc, dc_next], axis=0)
            d1, d2 = pltpu.roll(d_ext, r - 1, 0)[:t], pltpu.roll(d_ext, r - 2, 0)[:t]
            du_ref[:, cols] = (w[2:3] * dc + w[1:2] * d1 + w[0:1] * d2).astype(BF)
            u = u_ref[:, cols].astype(F32)
            gw_ref[0:1, off:off + bn] += jnp.sum(d2 * u, axis=0, keepdims=True)
            gw_ref[1:2, off:off + bn] += jnp.sum(d1 * u, axis=0, keepdims=True)
            gw_ref[2:3, off:off + bn] += jnp.sum(dc * u, axis=0, keepdims=True)
            gb_ref[:, off:off + bn] += jnp.sum(dc, axis=0, keepdims=True)

        for kb in range(D_FF // bn):
            cols = slice(kb * bn, (kb + 1) * bn)
            dg, dv = gate(cg_ref[:, cols], cv_ref[:, cols], d_ref[:, cols])
            dg_n, dv_n = gate(cag_ref[:, cols], cav_ref[:, cols], da_ref[:, cols] * more)
            back(dg, dg_n, ug_ref, cols, kb * bn, dug_ref)
            back(dv, dv_n, uv_ref, cols, D_FF + kb * bn, duv_ref)

        acc = _dot_nt(dug_ref[...], w_ref[:, :D_FF]) + _dot_nt(duv_ref[...], w_ref[:, D_FF:])
        dxn, dg3 = _rms_bwd(acc, x_ref[...], g_ref[...])
        dx_ref[...] = dres_ref[...] + dxn
        gg_ref[...] += dg3

    tile = lambda width: pl.BlockSpec((t, width), lambda i: (i, 0))
    after = pl.BlockSpec((8, D_FF), lambda i: (jnp.minimum((i + 1) * hb, nt * hb - 1), 0))
    return _pc(
        body, name="ffn_up_bwd", grid=(nt,),
        in_specs=[tile(D_FF)] * 4 + [after, after, tile(D_FF), after, _const((3, 2 * D_FF)), _const((D, 2 * D_FF)),
                                     tile(D), _const((1, D)), tile(D)],
        out_specs=[tile(D_FF), tile(D_FF), _acc((3, 2 * D_FF)), _acc((1, 2 * D_FF)), tile(D), _acc((1, D))],
        out_shape=[SDS((s, D_FF), BF), SDS((s, D_FF), BF), SDS((3, 2 * D_FF), F32), SDS((1, 2 * D_FF), F32),
                   SDS((s, D), F32), SDS((1, D), F32)],
        compiler_params=_cp(1))(ug, uv, cg, cv, cg, cv, dff, dff, conv_w, w_up, h1, g3, dh2)


def out_fwd(ff, h1, p, tgt, w_down, w_pg, w_ple, g4, g5, t):
    s = h1.shape[0]

    def body(ff_ref, h1_ref, p_ref, t_ref, wd_ref, wpg_ref, wpl_ref, g4_ref, g5_ref,
             f_ref, h2_ref, pg_ref, pe_ref, dy_ref, loss_ref):
        @pl.when(pl.program_id(0) == 0)
        def _():
            loss_ref[...] = jnp.zeros((1, 1), F32)

        f = _dot(ff_ref[...], wd_ref[...]).astype(BF)
        f_ref[...] = f
        h2 = h1_ref[...] + _rms(f.astype(F32), g4_ref[...])
        h2b = h2.astype(BF)
        h2_ref[...] = h2b
        pg = _dot(h2b, wpg_ref[...]).astype(BF)
        pe = _dot(p_ref[...].astype(BF), wpl_ref[...]).astype(BF)
        pg_ref[...] = pg
        pe_ref[...] = pe
        y = h2 + _rms(jax.nn.sigmoid(pg.astype(F32)) * pe.astype(F32), g5_ref[...])
        err = y - t_ref[...]
        dy_ref[...] = err * (1.0 / D)
        loss_ref[...] += (0.5 / D) * jnp.sum(err * err)

    row = lambda w: pl.BlockSpec((t, w), lambda i: (i, 0))
    return _pc(
        body, name="out_fwd", grid=(s // t,),
        in_specs=[row(D_FF), row(D), row(PLE), row(D), _const((D_FF, D)), _const((D, D)), _const((PLE, D)),
                  _const((1, D)), _const((1, D))],
        out_specs=[row(D)] * 5 + [_acc((1, 1))],
        out_shape=[SDS((s, D), BF)] * 4 + [SDS((s, D), F32), SDS((1, 1), F32)],
        compiler_params=_cp(1))(ff, h1, p, tgt, w_down, w_pg, w_ple, g4, g5)


def out_bwd(dy, pg, pe, f, g5, g4, w_pg, w_down, t):
    s = dy.shape[0]

    def body(dy_ref, pg_ref, pe_ref, f_ref, g5_ref, g4_ref, wpg_ref, wd_ref,
             dh2_ref, dpe_ref, dpg_ref, df_ref, dff_ref, gg5_ref, gg4_ref):
        @pl.when(pl.program_id(0) == 0)
        def _():
            gg5_ref[...] = jnp.zeros((1, D), F32)
            gg4_ref[...] = jnp.zeros((1, D), F32)

        dy_ = dy_ref[...]
        pg_ = pg_ref[...].astype(F32)
        pe_ = pe_ref[...].astype(F32)
        sg = jax.nn.sigmoid(pg_)
        dple, dg5 = _rms_bwd(dy_, sg * pe_, g5_ref[...])
        gg5_ref[...] += dg5
        dpe_ref[...] = (dple * sg).astype(BF)
        dpg = (dple * pe_ * sg * (1.0 - sg)).astype(BF)
        dpg_ref[...] = dpg
        dh2 = dy_ + _dot_nt(dpg, wpg_ref[...])
        dh2_ref[...] = dh2
        df, dg4 = _rms_bwd(dh2, f_ref[...].astype(F32), g4_ref[...])
        gg4_ref[...] += dg4
        dfb = df.astype(BF)
        df_ref[...] = dfb
        dff_ref[...] = _dot_nt(dfb, wd_ref[...]).astype(BF)

    row = lambda w: pl.BlockSpec((t, w), lambda i: (i, 0))
    return _pc(
        body, name="out_bwd", grid=(s // t,),
        in_specs=[row(D), row(D), row(D), row(D), _const((1, D)), _const((1, D)), _const((D, D)), _const((D_FF, D))],
        out_specs=[row(D), row(D), row(D), row(D), row(D_FF), _acc((1, D)), _acc((1, D))],
        out_shape=[SDS((s, D), F32), SDS((s, D), BF), SDS((s, D), BF), SDS((s, D), BF), SDS((s, D_FF), BF),
                   SDS((1, D), F32), SDS((1, D), F32)],
        compiler_params=_cp(1))(dy, pg, pe, f, g5, g4, w_pg, w_down)


def nt_normbwd(dys, w, xin, gain, dres, t, name, side=None):
    s = xin.shape[0]
    nt = s // t
    np_ = len(dys)

    def body(*refs):
        ins_, s_in, (dx_ref, gg_ref), s_out, _, s_scr = _split_side(refs, np_ + 4, 2, 0, side)
        dy_refs = ins_[:np_]
        w_ref, x_ref, g_ref, dres_ref = ins_[np_:]
        i = pl.program_id(0)

        @pl.when(i == 0)
        def _():
            gg_ref[...] = jnp.zeros((1, D), F32)
            if side:
                side.start(s_in, s_out, *s_scr)

        acc = _dot(jnp.concatenate([r_[...] for r_ in dy_refs], axis=1), w_ref[...])
        dxn, dg = _rms_bwd(acc, x_ref[...], g_ref[...])
        dx_ref[...] = dres_ref[...] + dxn
        gg_ref[...] += dg
        if side:
            @pl.when(i == nt - 1)
            def _():
                side.finish(s_in, s_out, *s_scr)

    row = lambda width: pl.BlockSpec((t, width), lambda i: (i, 0))
    si_specs, so_specs, so_shapes, s_scratch, s_ins = _side_specs(side)
    assert sum(dy.shape[1] for dy in dys) == w.shape[0]
    return _pc(
        body, name=name, grid=(nt,),
        in_specs=[row(dy.shape[1]) for dy in dys] + [_const(w.shape), row(D), _const((1, D)), row(D)] + si_specs,
        out_specs=[row(D), _acc((1, D))] + so_specs,
        out_shape=[SDS((s, D), F32), SDS((1, D), F32)] + so_shapes, scratch_shapes=s_scratch,
        compiler_params=_cp(1))(*dys, w, xin, gain, dres, *s_ins)


def mix_bwd(dh1, mx, z, ya, yb, g2, w_mix, a_out, b_out, t):
    s = dh1.shape[0]

    def body(dh_ref, mx_ref, ga_ref, gb_ref, ya_ref, yb_ref, g2_ref, wm_ref, ao_ref, bo_ref,
             dmx_ref, dya_ref, dyb_ref, dga_ref, dgb_ref, dsa_ref, dob_ref, gg2_ref):
        @pl.when(pl.program_id(0) == 0)
        def _():
            gg2_ref[...] = jnp.zeros((1, D), F32)

        dmx, dg2 = _rms_bwd(dh_ref[...], mx_ref[...].astype(F32), g2_ref[...])
        gg2_ref[...] += dg2
        dmxb = dmx.astype(BF)
        dmx_ref[...] = dmxb
        dmp = _dot_nt(dmxb, wm_ref[...])

        def gate(g_ref, y_ref, dy_ref, dg_ref, w_ref, dz_ref):
            sg = jax.nn.sigmoid(g_ref[...].astype(F32))
            dyb_ = (dmp * sg).astype(BF)
            dy_ref[...] = dyb_
            dg_ref[...] = (dmp * y_ref[...].astype(F32) * sg * (1.0 - sg)).astype(BF)
            dz_ref[...] = _dot_nt(dyb_, w_ref[...]).astype(BF)

        gate(ga_ref, ya_ref, dya_ref, dga_ref, ao_ref, dsa_ref)
        gate(gb_ref, yb_ref, dyb_ref, dgb_ref, bo_ref, dob_ref)

    row = lambda w: pl.BlockSpec((t, w), lambda i: (i, 0))
    return _pc(
        body, name="mix_bwd", grid=(s // t,),
        in_specs=[row(D), row(D), pl.BlockSpec((t, D), lambda i: (i, 4)), pl.BlockSpec((t, D), lambda i: (i, 5)),
                  row(D), row(D), _const((1, D)), _const((D, D)), _const((A_W, D)), _const((D, D))],
        out_specs=[row(D)] * 5 + [row(A_W), row(D), _acc((1, D))],
        out_shape=[SDS((s, D), BF)] * 5 + [SDS((s, A_W), BF), SDS((s, D), BF), SDS((1, D), F32)],
        compiler_params=_cp(1))(dh1, mx, z, z, ya, yb, g2, w_mix, a_out, b_out)


def sgu_bwd(z, dsa, ln_g, ln_b, w_cat, w_cat_t, bias_full, bdm, t):
    s = z.shape[0]
    nt = s // t
    nch = t // A_C

    def body(u_ref, v_ref, dsa_ref, g_ref, b_ref, wc_ref, wct_ref, bias_ref, bdm_ref,
             duv_ref, glg_ref, glb_ref, gws_ref, gbs_ref, ds_acc):
        i = pl.program_id(0)

        @pl.when(i == 0)
        def _():
            glg_ref[...] = jnp.zeros((1, A_W), F32)
            glb_ref[...] = jnp.zeros((1, A_W), F32)
            gws_ref[...] = jnp.zeros((A_C, A_G * A_C), F32)
            ds_acc[...] = jnp.zeros((A_C, A_W), F32)

        lng, bdm_ = g_ref[...], bdm_ref[...]
        rec = [_sgu_recompute(v_ref[pl.ds(ci * A_C, A_C), :].astype(F32), lng, b_ref[...]) for ci in range(nch)]
        spread_vn = _sgu_spread([r_[3] for r_ in rec], bdm_)
        mixed = _dot(_sgu_weights(wc_ref, False), spread_vn)
        dsas, dss, dgus = [], [], []
        for ci in range(nch):
            rows = pl.ds(ci * A_C, A_C)
            gu, dgu = _gelu_and_grad(u_ref[rows, :].astype(F32))
            dsa_ = dsa_ref[rows, :].astype(F32)
            ds = dsa_ * gu
            ds_acc[...] += ds
            dsas.append(dsa_)
            dgus.append(dgu)
            dss.append(ds.astype(BF))
        r = lax.broadcasted_iota(jnp.int32, (A_C, A_G * A_C), 0)
        c = lax.broadcasted_iota(jnp.int32, (A_C, A_G * A_C), 1) & (A_C - 1)
        gws_ref[...] += jnp.where(c <= r, _dot_nt(jnp.concatenate(dss, axis=1), spread_vn), 0.0)
        dvns = _dot(_sgu_weights(wct_ref, True), _sgu_spread(dss, bdm_))
        for ci in range(nch):
            rows = pl.ds(ci * A_C, A_C)
            dgv, rstd, xhat, _ = rec[ci]
            dvn = dvns[:, ci * A_W:(ci + 1) * A_W]
            glb_ref[...] += jnp.sum(dvn, axis=0, keepdims=True)
            glg_ref[...] += jnp.sum(dvn * xhat, axis=0, keepdims=True)
            dxh = dvn * lng
            dgv_ = rstd * (dxh - jnp.mean(dxh, axis=-1, keepdims=True)
                           - xhat * jnp.mean(dxh * xhat, axis=-1, keepdims=True))
            s_ = mixed[:, ci * A_W:(ci + 1) * A_W] + bias_ref[...]
            duv_ref[rows, :A_W] = (dsas[ci] * s_ * dgus[ci]).astype(BF)
            duv_ref[rows, A_W:] = (dgv_ * dgv).astype(BF)

        @pl.when(i == nt - 1)
        def _():
            acc = ds_acc[...]
            for g in range(A_G):
                gbs_ref[:, g:g + 1] = jnp.sum(acc[:, g * A_GD:(g + 1) * A_GD], axis=1, keepdims=True)

    return _pc(
        body, name="sgu_bwd", grid=(nt,),
        in_specs=[pl.BlockSpec((t, A_W), lambda i: (i, 0)), pl.BlockSpec((t, A_W), lambda i: (i, 1)),
                  pl.BlockSpec((t, A_W), lambda i: (i, 0)),
                  _const((1, A_W)), _const((1, A_W)), _const((A_C, A_G * A_C)), _const((A_C, A_G * A_C)),
                  _const((A_C, A_W)), _const((A_G * A_C, A_W))],
        out_specs=[pl.BlockSpec((t, D), lambda i: (i, 0)), _acc((1, A_W)), _acc((1, A_W)),
                   _acc((A_C, A_G * A_C)), _acc((A_C, A_G))],
        out_shape=[SDS((s, D), BF), SDS((1, A_W), F32), SDS((1, A_W), F32), SDS((A_C, A_G * A_C), F32),
                   SDS((A_C, A_G), F32)],
        scratch_shapes=[pltpu.VMEM((A_C, A_W), F32)],
        compiler_params=_cp(1))(z, z, dsa, ln_g, ln_b, w_cat, w_cat_t, bias_full, bdm)


def gla_bwd(z, qk32, zl, o, dob, states, wgk, bias, wn, ltri, ltri_t, t, side=None):
    s = z.shape[0]
    nt = s // t
    nc = t // B_C
    hpb = GLA_HPB
    assert hpb == B_H
    kw, vw = hpb * B_HK, hpb * B_HV

    def body(*refs):
        ins_, s_in, outs_, s_out, scr_, s_scr = _split_side(refs, 12, 7, 5, side)
        qk_ref, v_ref, og_ref, lr_ref, o_ref, dob_ref, st_ref, wgk_ref, bias_ref, wn_ref, l_ref, lt_ref = ins_
        dqk_ref, dv_ref, dog_ref, dpre_ref, dlr_ref, gbias_ref, gwn_ref = outs_
        dst_sc, dv_sc, dqd_sc, dkt_sc, ddec_sc = scr_
        i = pl.program_id(0)
        g = pl.program_id(1)

        @pl.when((i == 0) & (g == 0))
        def _():
            gbias_ref[...] = jnp.zeros((B_H, 1, B_HK), F32)
            gwn_ref[...] = jnp.zeros((1, B_HV), F32)
            if side:
                side.start(s_in, s_out, *s_scr)

        @pl.when(i == 0)
        def _():
            for hh in range(hpb):
                dst_sc[g * hpb + hh] = jnp.zeros((B_HV, B_HK), F32)

        lr, l, lt = lr_ref[...], l_ref[...], lt_ref[...]
        keep, keep_t = l > 0, lt > 0
        wn_ = wn_ref[...]
        last = lax.broadcasted_iota(jnp.int32, (nc, B_C, B_HK), 1) == B_C - 1
        for hh in range(hpb):
            h = g * hpb + hh
            cv, ck = slice(hh * B_HV, (hh + 1) * B_HV), slice(hh * B_HK, (hh + 1) * B_HK)
            pre, b, bl, eb, enb, etb, qd, ki, kt = _gla_decays(qk_ref[:, cv], lr, wgk_ref[:, ck], bias_ref[:, ck], l, t)
            qdb, kib, ktb = qd.astype(BF), ki.astype(BF), kt.astype(BF)
            vb = v_ref[:, cv]
            o_ = o_ref[:, cv].astype(F32)
            og = og_ref[:, cv].astype(F32)
            sog = jax.nn.sigmoid(og)
            dob_ = dob_ref[:, cv].astype(F32)
            don = dob_ * og * sog
            do, dwn = _rms_bwd(don, o_, wn_)
            gwn_ref[...] += dwn
            dog_ref[:, cv] = (dob_ * _rms(o_, wn_) * sog * (1.0 + og * (1.0 - sog))).astype(BF)
            dob16 = do.astype(BF)
            sc_t = jnp.where(keep_t, _dot_nt(kib, qdb), 0.0).astype(BF)
            dsc = jnp.where(keep, _dot_nt(dob16, vb), 0.0).astype(BF)
            dsc_t = jnp.where(keep_t, _dot_nt(vb, dob16), 0.0).astype(BF)
            dv_sc[hh] = _dot(sc_t, dob16)
            dqd_sc[hh] = _dot(dsc, kib)
            dki = _dot(dsc_t, qdb)
            for n in reversed(range(nc)):
                rows = slice(n * B_C, (n + 1) * B_C)
                dst = dst_sc[h]
                dstb = dst.astype(BF)
                stp = st_ref[n, hh]
                dv_sc[hh, rows, :] += _dot_nt(ktb[rows], dstb)
                dkt_sc[hh, rows, :] = _dot(vb[rows], dstb)
                dqd_sc[hh, rows, :] += _dot(dob16[rows], stp)
                dec = jnp.exp(bl[n * B_C:n * B_C + 1, :])
                ddec_sc[hh, n] = jnp.sum(dst * stp.astype(F32), axis=0, keepdims=True) * dec
                dst_sc[h] = dst * dec + _dot_tn(dob16[rows], qdb[rows])
            dqd, dkt = dqd_sc[hh], dkt_sc[hh]
            dv_ref[:, cv] = dv_sc[hh].astype(BF)
            dqk_ref[:, hh * B_HV:hh * B_HV + B_HK] = (dqd * eb * (B_HK ** -0.5)).astype(BF)
            dqk_ref[:, hh * B_HV + B_HK:(hh + 1) * B_HV] = (dki * enb + dkt * etb).astype(BF)
            dktkt = dkt * kt
            db3 = (dqd * qd - dki * ki - dktkt).reshape(nc, B_C, B_HK)
            dbl = jnp.sum(dktkt.reshape(nc, B_C, B_HK), axis=1, keepdims=True) + ddec_sc[hh]
            db = (db3 + jnp.where(last, dbl, 0.0)).reshape(t, B_HK)
            dla = _ldot3(lt, db)
            dpre = dla * (1.0 / 16.0) * (1.0 - jax.nn.sigmoid(pre))
            dpreb = dpre.astype(BF)
            dpre_ref[:, ck] = dpreb
            gbias_ref[h] += jnp.sum(dpre, axis=0, keepdims=True)
            dlr_h = _dot_nt(dpreb, wgk_ref[:, ck])
            dlr = dlr_h if hh == 0 else dlr + dlr_h
        dlr_ref[...] = dlr.astype(BF)
        if side:
            @pl.when((i == nt - 1) & (g == B_H // hpb - 1))
            def _():
                side.finish(s_in, s_out, *s_scr)

    rv = lambda i: nt - 1 - i
    si_specs, so_specs, so_shapes, s_scratch, s_ins = _side_specs(side)
    vo, go = 2048 // vw, 3072 // vw
    tile = lambda off: pl.BlockSpec((t, vw), lambda i, g: (rv(i), off + g))
    return _pc(
        body, name="gla_bwd", grid=(nt, B_H // hpb),
        in_specs=[tile(0), tile(vo), tile(go), pl.BlockSpec((t, LANE), lambda i, g: (rv(i), 0)), tile(0), tile(0),
                  pl.BlockSpec((nc, hpb, B_HV, B_HK), lambda i, g: (rv(i), g, 0, 0)),
                  pl.BlockSpec((LANE, kw), lambda i, g: (0, g)), pl.BlockSpec((1, kw), lambda i, g: (0, g)),
                  _const((1, B_HV)), _const((t, t)), _const((t, t))] + si_specs,
        out_specs=[tile(0), tile(0), tile(0), pl.BlockSpec((t, kw), lambda i, g: (rv(i), g)),
                   pl.BlockSpec((t, LANE), lambda i, g: (rv(i), 0)), _acc((B_H, 1, B_HK)), _acc((1, B_HV))] + so_specs,
        out_shape=[SDS((s, D), BF), SDS((s, D), BF), SDS((s, D), BF), SDS((s, B_H * B_HK), BF), SDS((s, LANE), BF),
                   SDS((B_H, 1, B_HK), F32), SDS((1, B_HV), F32)] + so_shapes,
        scratch_shapes=[pltpu.VMEM((B_H, B_HV, B_HK), F32), pltpu.VMEM((hpb, t, B_HV), F32),
                        pltpu.VMEM((hpb, t, B_HK), F32), pltpu.VMEM((hpb, t, B_HK), F32),
                        pltpu.VMEM((hpb, nc, 1, B_HK), F32)] + s_scratch,
        compiler_params=_cp(2))(qk32, z, z, zl, o, dob, states, wgk, bias, wn, ltri, ltri_t, *s_ins)


def mm_tn(a, b, name, tk=2048):
    s, m = a.shape
    n = b.shape[1]
    bn = next(c for c in (1024, 1408, 512, 256, 128) if n % c == 0 and m * c * 4 <= 6 * 1024 * 1024)
    if 2 * tk <= s and 2 * (m + bn) * 2 * tk * 2 + m * bn * 8 <= 44 * 1024 * 1024:
        tk *= 2
    tk = min(tk, s)
    nk = s // tk

    def body(a_ref, b_ref, o_ref, acc):
        k = pl.program_id(1)

        @pl.when(k == 0)
        def _():
            acc[...] = jnp.zeros((m, bn), F32)

        acc[...] += _dot_tn(a_ref[...].astype(BF), b_ref[...])

        @pl.when(k == nk - 1)
        def _():
            o_ref[...] = acc[...].astype(BF)

    return _pc(
        body, name=name, grid=(n // bn, nk),
        in_specs=[pl.BlockSpec((tk, m), lambda j, k: (k, 0)), pl.BlockSpec((tk, bn), lambda j, k: (k, j))],
        out_specs=pl.BlockSpec((m, bn), lambda j, k: (0, j)),
        out_shape=SDS((m, n), BF), scratch_shapes=[pltpu.VMEM((m, bn), F32)], compiler_params=_cp(2))(a, b)


def _adamw(w, g, m, v):
    m = ADAM_B1 * m + (1.0 - ADAM_B1) * g
    v = ADAM_B2 * v + (1.0 - ADAM_B2) * (g * g)
    m_hat = m / (1.0 - ADAM_B1 ** ADAM_STEP)
    v_hat = v / (1.0 - ADAM_B2 ** ADAM_STEP)
    return -ADAM_LR * (m_hat / (jnp.sqrt(v_hat) + ADAM_EPS) + ADAM_WD * w), m, v


def _half_rows(rows):
    rh = rows // 2
    return rh, max(b for b in range(16, 257, 16) if rh % b == 0)


def _pc_sp(body, grid, in_specs, out_specs, out_shape, name):
    gs = pltpu.PrefetchScalarGridSpec(num_scalar_prefetch=1, grid=grid, in_specs=in_specs, out_specs=out_specs)
    return _pc(body, grid_spec=gs, out_shape=out_shape, name=name, compiler_params=_cp(len(grid)))


def adamw_halves(sc, own, sib, w, m, v, name):
    rows, cols = w.shape
    rh, br = _half_rows(rows)
    nbk = rh // br

    def body(sc_ref, own_ref, sib_ref, w_ref, m_ref, v_ref, go_ref, d_ref, mo_ref, vo_ref):
        g_ = jnp.where(pl.program_id(0) // nbk == sc_ref[0], own_ref[...], sib_ref[...])
        go_ref[...] = g_
        d_ref[...], mo_ref[...], vo_ref[...] = _adamw(w_ref[...], g_, m_ref[...], v_ref[...])

    half = pl.BlockSpec((br, cols), lambda i, sc_: (i % nbk, 0))
    blk = pl.BlockSpec((br, cols), lambda i, sc_: (i, 0))
    return _pc_sp(body, (2 * nbk,), [half, half, blk, blk, blk], [blk] * 4, [SDS((rows, cols), F32)] * 4,
                  name)(sc, own, sib, w, m, v)


def adamw_cols(sc, own, sib, w, m, v, name, cb=256):
    rows, cols = w.shape
    nk = cols // 2 // cb

    def body(sc_ref, own_ref, sib_ref, w_ref, m_ref, v_ref, go_ref, d_ref, mo_ref, vo_ref):
        g_ = jnp.where(pl.program_id(0) == sc_ref[0], own_ref[...], sib_ref[...])
        go_ref[...] = g_
        d_ref[...], mo_ref[...], vo_ref[...] = _adamw(w_ref[...], g_, m_ref[...], v_ref[...])

    half = pl.BlockSpec((rows, cb), lambda h, k, sc_: (0, k))
    blk = pl.BlockSpec((rows, cb), lambda h, k, sc_: (0, h * nk + k))
    return _pc_sp(body, (2, nk), [half, half, blk, blk, blk], [blk] * 4, [SDS((rows, cols), F32)] * 4,
                  name)(sc, own, sib, w, m, v)


def adamw_small(g, w, m, v):
    def body(g_ref, w_ref, m_ref, v_ref, d_ref, mo_ref, vo_ref):
        d_ref[...], mo_ref[...], vo_ref[...] = _adamw(w_ref[...], g_ref[...], m_ref[...], v_ref[...])

    vm = pl.BlockSpec(memory_space=pltpu.VMEM)
    return _pc(body, name="adamw_small", in_specs=[vm] * 4, out_specs=[vm] * 3, out_shape=[SDS(g.shape, F32)] * 3,
               compiler_params=pltpu.CompilerParams(vmem_limit_bytes=VMEM_LIMIT))(g, w, m, v)


def _pos():
    return lax.axis_index("x"), lax.axis_index("y"), lax.axis_index("c")


def _other_chips(x, y):
    return [(1 - x, y), (x, 1 - y), (1 - x, 1 - y)]


_ANY = pl.BlockSpec(memory_space=pltpu.HBM)


class _Side:
    def __init__(self, ins, out_shapes, nsem, start, finish):
        self.ins, self.out_shapes, self.start, self.finish = list(ins), list(out_shapes), start, finish
        self.scratch = [pltpu.SemaphoreType.DMA((nsem,)), pltpu.SemaphoreType.DMA((nsem,))]
        self.n_in, self.n_out = len(self.ins), len(self.out_shapes)


def _run_side(side, name):
    def body(*refs):
        args_ = (refs[:side.n_in], refs[side.n_in:side.n_in + side.n_out], *refs[side.n_in + side.n_out:])
        side.start(*args_)
        side.finish(*args_)

    return _pc(body, name=name, in_specs=[_ANY] * side.n_in, out_specs=[_ANY] * side.n_out,
               out_shape=side.out_shapes, scratch_shapes=side.scratch)(*side.ins)


def _split_rows(shape):
    return (shape[0] // 2) % 16 == 0


def _core_halves(shape, c):
    if _split_rows(shape):
        h = shape[0] // 2
        return ((pl.ds(pl.multiple_of(c * h, 16), h), slice(None)),
                (pl.ds(pl.multiple_of((1 - c) * h, 16), h), slice(None)))
    h = shape[1] // 2
    assert h % LANE == 0
    return ((slice(None), pl.ds(pl.multiple_of(c * h, LANE), h)),
            (slice(None), pl.ds(pl.multiple_of((1 - c) * h, LANE), h)))


def gather_side(bigs, tinies):
    nb, nt_ = len(bigs), len(tinies)

    def plan(ins, outs, ssem, rsem):
        x, y, c = _pos()
        me = 2 * x + y
        chips = _other_chips(x, y)
        sibling = (x, y, 1 - c)

        def copy(k, src, dst, to):
            return pltpu.make_async_remote_copy(src_ref=src, dst_ref=dst, send_sem=ssem.at[k], recv_sem=rsem.at[k],
                                                device_id=to, device_id_type=MESH)

        sends, landed, passed_on, tiny_landed = [], [], [], []
        for w in range(nb):
            mine, theirs = _core_halves(bigs[w].shape, c)
            for j, (cx, cy) in enumerate(chips):
                sends.append(copy(6 * w + j, ins[w].at[mine], outs[w].at[(me,) + mine], (cx, cy, c)))
                blk = outs[w].at[(2 * cx + cy,) + mine]
                landed.append((copy(6 * w + j, blk, blk, (cx, cy, c)), copy(6 * w + 3 + j, blk, blk, sibling)))
                blk = outs[w].at[(2 * cx + cy,) + theirs]
                passed_on.append(copy(6 * w + 3 + j, blk, blk, sibling))
        for w in range(nt_):
            for j, (cx, cy) in enumerate(chips):
                k = 6 * nb + 3 * w + j
                sends.append(copy(k, ins[nb + w], outs[nb + w].at[me], (cx, cy, c)))
                blk = outs[nb + w].at[2 * cx + cy]
                tiny_landed.append(copy(k, blk, blk, (cx, cy, c)))
        return sends, landed, passed_on, tiny_landed

    def start(ins, outs, ssem, rsem):
        for cp in plan(ins, outs, ssem, rsem)[0]:
            cp.start()

    def finish(ins, outs, ssem, rsem):
        sends, landed, passed_on, tiny_landed = plan(ins, outs, ssem, rsem)
        for arrived, forward in landed:
            arrived.wait_recv()
            forward.start()
        for arrived in tiny_landed + passed_on:
            arrived.wait_recv()
        for cp in sends + [forward for _, forward in landed]:
            cp.wait_send()

    return _Side(list(bigs) + list(tinies), [SDS((4,) + a.shape, a.dtype) for a in list(bigs) + list(tinies)],
                 6 * nb + 3 * nt_, start, finish)


def swap_halves(gs, name):
    n = len(gs)

    def half_shape(g):
        l, r, cols = g.shape
        return (l, r // 2, cols) if _split_rows((r, cols)) else (l, r, cols // 2)

    def body(*refs):
        g_refs, sib_refs = refs[:n], refs[n:2 * n]
        ssem, rsem = refs[2 * n:]
        x, y, c = _pos()
        cps = []
        for w in range(n):
            give = (slice(None),) + _core_halves(gs[w].shape[1:], c)[1]
            cp = pltpu.make_async_remote_copy(src_ref=g_refs[w].at[give], dst_ref=sib_refs[w], send_sem=ssem.at[w],
                                              recv_sem=rsem.at[w], device_id=(x, y, 1 - c), device_id_type=MESH)
            cp.start()
            cps.append(cp)
        for cp in cps:
            cp.wait()

    return _pc(body, name=name, in_specs=[_ANY] * n, out_specs=[_ANY] * n,
               out_shape=[SDS(half_shape(g), g.dtype) for g in gs],
               scratch_shapes=[pltpu.SemaphoreType.DMA((n,)), pltpu.SemaphoreType.DMA((n,))])(*gs)


COL_BLOCK = 256


def add_half(sc, g, sib, name):
    l, r, cols = g.shape

    def body(sc_ref, g_ref, s_ref, o_ref):
        o_ref[...] = (g_ref[...].astype(F32) + s_ref[...].astype(F32)).astype(BF)

    if _split_rows((r, cols)):
        rh, br = _half_rows(r)
        nbk = rh // br
        blk = pl.BlockSpec((1, br, cols), lambda j, i, sc_: (j, i, 0))
        mine = pl.BlockSpec((1, br, cols), lambda j, i, sc_: (j, sc_[0] * nbk + i, 0))
        return _pc_sp(body, (l, nbk), [mine, blk], blk, SDS((l, rh, cols), BF), name)(sc, g, sib)
    nbk = cols // 2 // COL_BLOCK
    blk = pl.BlockSpec((1, r, COL_BLOCK), lambda j, i, sc_: (j, 0, i))
    mine = pl.BlockSpec((1, r, COL_BLOCK), lambda j, i, sc_: (j, 0, sc_[0] * nbk + i))
    return _pc_sp(body, (l, nbk), [mine, blk], blk, SDS((l, r, cols // 2), BF), name)(sc, g, sib)


def exchange_side(ps):
    n_ = len(ps)

    def width(p_):
        return p_.shape[2] if p_.shape[0] == 4 else p_.shape[2] // 4

    def plan(p_refs, got_refs, ssem, rsem):
        x, y, c = _pos()
        cps = []
        for w in range(n_):
            wd = width(ps[w])
            for j, (cx, cy) in enumerate(_other_chips(x, y)):
                to = 2 * cx + cy
                src = p_refs[w].at[to] if ps[w].shape[0] == 4 else p_refs[w].at[0, :, pl.ds(pl.multiple_of(to * wd, LANE), wd)]
                cps.append(pltpu.make_async_remote_copy(
                    src_ref=src, dst_ref=got_refs[w].at[j], send_sem=ssem.at[3 * w + j], recv_sem=rsem.at[3 * w + j],
                    device_id=(cx, cy, c), device_id_type=MESH))
        return cps

    def start(*refs):
        for cp in plan(*refs):
            cp.start()

    def finish(*refs):
        for cp in plan(*refs):
            cp.wait()

    return _Side(ps, [SDS((3, p_.shape[1], width(p_)), p_.dtype) for p_ in ps], 3 * n_, start, finish)


def sum4(sc, p, got, name):
    _, rh, wd = got.shape

    def body(sc_ref, p_ref, g_ref, r_ref):
        r_ref[...] = ((p_ref[0].astype(F32) + g_ref[0].astype(F32)) + (g_ref[1].astype(F32) + g_ref[2].astype(F32)))

    if rh % 16:
        assert p.shape[0] == 4
        return _pc_sp(body, (wd // COL_BLOCK,),
                      [pl.BlockSpec((1, rh, COL_BLOCK), lambda i, sc_: (sc_[1], 0, i)),
                       pl.BlockSpec((3, rh, COL_BLOCK), lambda i, sc_: (0, 0, i))],
                      pl.BlockSpec((rh, COL_BLOCK), lambda i, sc_: (0, i)), SDS((rh, wd), F32), name)(sc, p, got)
    _, br = _half_rows(2 * rh)
    own = (pl.BlockSpec((1, br, wd), lambda i, sc_: (sc_[1], i, 0)) if p.shape[0] == 4
           else pl.BlockSpec((1, br, wd), lambda i, sc_: (0, i, sc_[1])))
    return _pc_sp(body, (rh // br,), [own, pl.BlockSpec((3, br, wd), lambda i, sc_: (0, i, 0))],
                  pl.BlockSpec((br, wd), lambda i, sc_: (i, 0)), SDS((rh, wd), F32), name)(sc, p, got)


def join_halves(halves):
    n = len(halves)

    def body(*refs):
        h_refs, got_refs = refs[:n], refs[n:2 * n]
        ssem, rsem = refs[2 * n:]
        x, y, c = _pos()
        cps = []
        for w in range(n):
            cp = pltpu.make_async_remote_copy(src_ref=h_refs[w], dst_ref=got_refs[w], send_sem=ssem.at[w],
                                              recv_sem=rsem.at[w], device_id=(x, y, 1 - c), device_id_type=MESH)
            cp.start()
            cps.append(cp)
        for cp in cps:
            cp.wait()

    return _pc(body, name="join_halves", in_specs=[_ANY] * n, out_specs=[_ANY] * n,
               out_shape=[SDS(h.shape, h.dtype) for h in halves],
               scratch_shapes=[pltpu.SemaphoreType.DMA((n,)), pltpu.SemaphoreType.DMA((n,))])(*halves)


def allreduce_small(g):
    rows = g.shape[0]
    rh = rows // 2

    def body(g_ref, out_ref, sib_buf, chip_buf, sum_sc, ssem, rsem):
        x, y, c = _pos()
        me = 2 * x + y
        sibling = (x, y, 1 - c)
        mine = pl.ds(pl.multiple_of(c * rh, 8), rh)

        def copy(k, src, dst, to):
            return pltpu.make_async_remote_copy(src_ref=src, dst_ref=dst, send_sem=ssem.at[k], recv_sem=rsem.at[k],
                                                device_id=to, device_id_type=MESH)

        cp = copy(0, g_ref, sib_buf, sibling)
        cp.start()
        cp.wait()
        sum_sc[...] = g_ref[...] + sib_buf[...]
        chips = _other_chips(x, y)
        cps = [copy(1 + j, sum_sc.at[mine], chip_buf.at[me], (cx, cy, c)) for j, (cx, cy) in enumerate(chips)]
        for cp in cps:
            cp.start()
        chip_buf[me] = sum_sc[mine, :]
        for j, (cx, cy) in enumerate(chips):
            copy(1 + j, sum_sc.at[mine], chip_buf.at[2 * cx + cy], (cx, cy, c)).wait_recv()
        for cp in cps:
            cp.wait_send()
        out_ref[mine, :] = (chip_buf[0] + chip_buf[1]) + (chip_buf[2] + chip_buf[3])
        cp = copy(4, out_ref.at[mine], out_ref.at[mine], sibling)
        cp.start()
        cp.wait()

    vm = pl.BlockSpec(memory_space=pltpu.VMEM)
    return _pc(body, name="allreduce_small", in_specs=[vm], out_specs=vm, out_shape=SDS((rows, LANE), F32),
               scratch_shapes=[pltpu.VMEM((rows, LANE), F32), pltpu.VMEM((4, rh, LANE), F32), pltpu.VMEM((rows, LANE), F32),
                               pltpu.SemaphoreType.DMA((5,)), pltpu.SemaphoreType.DMA((5,))],
               compiler_params=pltpu.CompilerParams(vmem_limit_bytes=VMEM_LIMIT))(g)


def _pack_small(entries, get):
    flat = jnp.concatenate([get(n).reshape(-1).astype(F32) for n, _ in entries])
    rows = -(-flat.shape[0] // (8 * LANE)) * 8
    return jnp.pad(flat, (0, rows * LANE - flat.shape[0])).reshape(rows, LANE)


def _unpack_small(entries, packed):
    out, off = {}, 0
    flat = packed.reshape(-1)
    for name, n in entries:
        out[name] = flat[off:off + n]
        off += n
    return out


def _cols_full(blk):
    return blk.transpose(1, 0, 2).reshape(blk.shape[1], 4 * blk.shape[2])


def kernel(x, p, pre_mix_norm, w_in, a_ln_g, a_ln_b, a_spatial_w, a_spatial_b, a_out, b_gk, b_gk_bias, b_out_norm, b_out, w_mix_out, post_mix_norm, pre_ffn_norm, w_up, conv_w, conv_b, w_down, post_ffn_norm, w_ple, w_ple_gate, post_ple_norm, loss_target, m_pre_mix_norm, m_w_in, m_a_ln_g, m_a_ln_b, m_a_spatial_w, m_a_spatial_b, m_a_out, m_b_gk, m_b_gk_bias, m_b_out_norm, m_b_out, m_w_mix_out, m_post_mix_norm, m_pre_ffn_norm, m_w_up, m_conv_w, m_conv_b, m_w_down, m_post_ffn_norm, m_w_ple, m_w_ple_gate, m_post_ple_norm, v_pre_mix_norm, v_w_in, v_a_ln_g, v_a_ln_b, v_a_spatial_w, v_a_spatial_b, v_a_out, v_b_gk, v_b_gk_bias, v_b_out_norm, v_b_out, v_w_mix_out, v_post_mix_norm, v_pre_ffn_norm, v_w_up, v_conv_w, v_conv_b, v_w_down, v_post_ffn_norm, v_w_ple, v_w_ple_gate, v_post_ple_norm):
    args = dict(locals())
    order = ['pre_mix_norm', 'w_in', 'a_ln_g', 'a_ln_b', 'a_spatial_w', 'a_spatial_b', 'a_out', 'b_gk', 'b_gk_bias',
             'b_out_norm', 'b_out', 'w_mix_out', 'post_mix_norm', 'pre_ffn_norm', 'w_up', 'conv_w', 'conv_b', 'w_down',
             'post_ffn_norm', 'w_ple', 'w_ple_gate', 'post_ple_norm']
    assert sorted(BIG + TINY + tuple(n for n, _ in SMALL)) == sorted(order)
    s = x.shape[1]
    xs = x.reshape(s, D)
    ps = p.reshape(s, PLE)
    tgt = loss_target.reshape(s, D)
    t_big = min(1024, s)
    t_mid = min(512, s)
    t_small = min(256, s)
    t_gla = min(256, s)
    mx_, my_, mc_ = _pos()
    me = 2 * mx_ + my_
    sc = jnp.stack([mc_, me]).astype(jnp.int32)
    shard = lambda n: args[n].reshape(args[n].shape[1:])

    mine = {n: shard(n).astype(BF) for n in BIG}
    mine["w_in"] = shard("w_in").T.astype(BF)
    mine.update({n: shard(n) for n in TINY})
    fill = lambda names, gots: {n: lax.dynamic_update_slice(got, mine[n][None], (me, 0, 0)) for n, got in zip(names, gots)}
    first = ("w_in",) + TINY
    full = fill(first, _run_side(gather_side([mine["w_in"]], [mine[n] for n in TINY]), "gather_first"))
    wi = full["w_in"].reshape(4 * 1540, D)
    seg = lambda a, b: wi[a:b]
    qk = [seg(1024 + h * B_HK, 1024 + (h + 1) * B_HK) for h in range(B_H)]
    kk = [seg(1536 + h * B_HK, 1536 + (h + 1) * B_HK) for h in range(B_H)]
    w_z = jnp.concatenate([seg(0, 1024)] + [m_ for h in range(B_H) for m_ in (qk[h], kk[h])]
                          + [seg(2048, 4096), seg(4112, 6160), seg(4096, 4112), jnp.zeros((LANE - B_RANK, D), BF)], axis=0)
    wgk = jnp.pad(_cols_full(full["b_gk"]).astype(BF), ((0, LANE - B_RANK), (0, 0)))
    w_conv = _cols_full(full["conv_w"])
    g1, g2, g3 = pre_mix_norm.reshape(1, D), post_mix_norm.reshape(1, D), pre_ffn_norm.reshape(1, D)
    g4, g5 = post_ffn_norm.reshape(1, D), post_ple_norm.reshape(1, D)
    ln_g, ln_b = a_ln_g.reshape(1, A_W), a_ln_b.reshape(1, A_W)
    w_s = a_spatial_w.reshape(A_G, A_C, A_C)
    w_cat = w_s.transpose(1, 0, 2).reshape(A_C, A_G * A_C)
    w_cat_t = w_s.transpose(2, 0, 1).reshape(A_C, A_G * A_C)
    bias_full = jnp.repeat(a_spatial_b.reshape(A_G, A_C).T, A_GD, axis=1)
    bdm = (jnp.arange(A_G * A_C)[:, None] // A_C == jnp.arange(A_W)[None, :] // A_GD).astype(BF)
    gk_bias = b_gk_bias.reshape(1, B_H * B_HK)
    wn = b_out_norm.reshape(1, B_HV)
    cb = conv_b.reshape(1, 2 * D_FF)
    idx = jnp.arange(t_gla)
    ltri = ((idx[:, None] // B_C == idx[None, :] // B_C) & (idx[None, :] <= idx[:, None])).astype(BF)

    a, z, qk32, zl, *gots = norm_matmul(xs, g1, w_z, D, t_big, "in_proj", nblk=6, f32_blk=1, tail_blk=48,
                                        side=gather_side([mine[n] for n in BIG[1:]], []))
    full.update(fill(BIG[1:], gots))
    w_aout, w_ple_f = _cols_full(full["a_out"]), _cols_full(full["w_ple"])
    w_bout, w_mix, w_pg = (full[n].reshape(D, D) for n in ("b_out", "w_mix_out", "w_ple_gate"))
    w_dn, w_up3 = full["w_down"].reshape(D_FF, D), full["w_up"]
    sa = sgu_fwd(z, ln_g, ln_b, w_cat, bias_full, bdm, t_mid)
    ob, o, states = gla_fwd(z, qk32, zl, wgk, gk_bias, wn, ltri, t_gla)
    ya, yb, mp, mx, h1 = mix_fwd(sa, ob, z, xs, w_aout, w_bout, w_mix, g2, t_mid)
    c, up_g, up_v, cg, cv, ff = ffn_up_fwd(h1, g3, w_up3, w_conv, cb, t_mid)
    f, h2, pg, pe, dy, loss = out_fwd(ff, h1, ps, tgt, w_dn, w_pg, w_ple_f, g4, g5, t_mid)

    dh2, dpe, dpg, df, dff, gg5, gg4 = out_bwd(dy, pg, pe, f, g5, g4, w_pg, w_dn, t_mid)
    dup_g, dup_v, gcw, gcb, dh1, gg3 = ffn_up_bwd(up_g, up_v, cg, cv, dff, w_conv, _cols_full(w_up3), h1, g3, dh2,
                                                  t_small)
    dmx, dya, dyb, dga, dgb, dsa, dob, gg2 = mix_bwd(dh1, mx, z, ya, yb, g2, w_mix, w_aout, w_bout, t_mid)
    duv, g_lng, g_lnb, g_wcat, g_bst = sgu_bwd(z, dsa, ln_g, ln_b, w_cat, w_cat_t, bias_full, bdm, t_mid)
    g_ws = g_wcat.reshape(A_C, A_G, A_C).transpose(1, 0, 2)

    grads = {
        "a_out": mm_tn(sa, dya, "dw_a_out")[None],
        "b_out": mm_tn(ob, dyb, "dw_b_out").reshape(4, D // 4, D),
        "w_mix_out": mm_tn(mp, dmx, "dw_mix").reshape(4, D // 4, D),
        "w_up": jnp.concatenate([mm_tn(c, dup_g, "dw_up_g"), mm_tn(c, dup_v, "dw_up_v")], axis=1)[None],
        "w_down": mm_tn(ff, df, "dw_down").reshape(4, D_FF // 4, D),
        "w_ple": mm_tn(ps, dpe, "dw_ple")[None],
        "w_ple_gate": mm_tn(h2, dpg, "dw_ple_gate").reshape(4, D // 4, D),
    }

    def chip_partials(names):
        gs = [grads[n] for n in names]
        return [add_half(sc, g, sib, "partial_" + n)
                for n, g, sib in zip(names, gs, swap_halves(gs, "swap_halves_" + names[0]))]

    parts = dict(zip(BIG[1:], chip_partials(BIG[1:])))
    dqk, dvb, dog, dpre, dlr, g_gkb, g_wn, *gots = gla_bwd(
        z, qk32, zl, o, dob, states, wgk, gk_bias, wn, ltri, ltri.T, t_gla,
        side=exchange_side([parts[n] for n in BIG[1:]]))
    got = dict(zip(BIG[1:], gots))
    segs = [duv, dqk, dvb, dog, dga, dgb, dlr]

    gz = [mm_tn(sg_, a, "dw_in_%d" % k) for k, sg_ in enumerate(segs)]
    gq = [gz[1][h * 256:h * 256 + B_HK] for h in range(B_H)]
    gk = [gz[1][h * 256 + B_HK:(h + 1) * 256] for h in range(B_H)]
    g_in = jnp.concatenate([gz[0]] + gq + gk + [gz[2], gz[3], gz[6][:B_RANK], gz[4], gz[5]], axis=0)
    grads["w_in"] = g_in.reshape(4, 1540, D)
    parts["w_in"], = chip_partials(("w_in",))
    dx, gg1, got["w_in"] = nt_normbwd(segs, w_z, xs, g1, dh1, t_small, "in_bwd",
                                      side=exchange_side([parts["w_in"]]))

    reds = [sum4(sc, parts[n], got[n], "sum_" + n) for n in BIG]
    outs = {}
    for n, red, sib in zip(BIG, reds, join_halves(reds)):
        if n == "w_in":
            res = adamw_cols(sc, red, sib, shard(n).T, shard("m_" + n).T, shard("v_" + n).T, "adamw_" + n)
            res = [r_.T for r_ in res]
        else:
            res = adamw_halves(sc, red, sib, shard(n), shard("m_" + n), shard("v_" + n), "adamw_" + n)
        outs[n] = [r_.reshape(args[n].shape) for r_ in res]

    small_g = {
        "pre_mix_norm": gg1, "a_ln_g": g_lng, "a_ln_b": g_lnb, "a_spatial_w": g_ws, "a_spatial_b": g_bst.T,
        "b_gk_bias": g_gkb, "b_out_norm": g_wn, "post_mix_norm": gg2, "pre_ffn_norm": gg3,
        "conv_b": gcb, "post_ffn_norm": gg4, "post_ple_norm": gg5,
        "b_gk": mm_tn(zl, dpre, "dw_gk")[:B_RANK], "conv_w": gcw,
        "loss": loss,
    }
    red_entries = SMALL + (("b_gk", B_RANK * 512), ("conv_w", 3 * 2 * D_FF), ("loss", 1))
    g_fin = _unpack_small(red_entries, allreduce_small(_pack_small(red_entries, lambda n: small_g[n])))
    g_fin["b_gk"] = lax.dynamic_slice(g_fin["b_gk"].reshape(B_RANK, 512), (0, me * B_HK), (B_RANK, B_HK))
    g_fin["conv_w"] = lax.dynamic_slice(g_fin["conv_w"].reshape(3, 2 * D_FF), (0, me * 1408), (3, 1408))
    upd_entries = SMALL + (("b_gk", B_RANK * B_HK), ("conv_w", 3 * 1408))
    res = adamw_small(*[_pack_small(upd_entries, get) for get in
                        (lambda n: g_fin[n], lambda n: args[n], lambda n: args["m_" + n], lambda n: args["v_" + n])])
    res = [_unpack_small(upd_entries, r_) for r_ in res]
    for n, _ in upd_entries:
        outs[n] = [r_[n].reshape(args[n].shape) for r_ in [g_fin] + res]

    return (g_fin["loss"].reshape(()), dx.reshape(x.shape), *[outs[n][0] for n in order], *[outs[n][1] for n in order],
            *[outs[n][2] for n in order], *[outs[n][3] for n in order])
```

```python
import functools
import math

import jax
import jax.numpy as jnp
from jax import lax
from jax.experimental import pallas as pl
from jax.experimental.pallas import tpu as pltpu

F32 = jnp.float32
BF = jnp.bfloat16
SDS = jax.ShapeDtypeStruct
MESH = pl.DeviceIdType.MESH

EPS = 1e-6
D = 1024
A_W = 512
A_G, A_C = 8, 128
A_GD = A_W // A_G
B_H, B_HK, B_HV = 4, 128, 256
B_C = 64
GLA_HPB = 4
B_RANK = 16
D_FF = 2816
PLE = 256
LANE = 128
VMEM_LIMIT = 60 * 1024 * 1024

ADAM_LR, ADAM_B1, ADAM_B2, ADAM_EPS, ADAM_WD, ADAM_STEP = 0.001, 0.9, 0.999, 1e-08, 0.01, 10

_GC = math.sqrt(2.0 / math.pi)
_GA = 0.044715

BIG = ("w_in", "a_out", "b_out", "w_mix_out", "w_up", "w_down", "w_ple", "w_ple_gate")
TINY = ("b_gk", "conv_w")
SMALL = (("pre_mix_norm", 1024), ("a_ln_g", 512), ("a_ln_b", 512), ("a_spatial_w", 131072),
         ("a_spatial_b", 1024), ("b_gk_bias", 512), ("b_out_norm", 256), ("post_mix_norm", 1024),
         ("pre_ffn_norm", 1024), ("conv_b", 5632), ("post_ffn_norm", 1024), ("post_ple_norm", 1024))


def _pc(body, **kw):
    return pl.pallas_call(body, **kw)


def _cp(n):
    return pltpu.CompilerParams(dimension_semantics=("arbitrary",) * n, vmem_limit_bytes=VMEM_LIMIT)


def _const(shape):
    nd = len(shape)
    return pl.BlockSpec(shape, lambda *_: (0,) * nd, pipeline_mode=pl.Buffered(1))


def _acc(shape):
    nd = len(shape)
    return pl.BlockSpec(shape, lambda *_: (0,) * nd)


def _dot(a, b):
    return jnp.dot(a, b, preferred_element_type=F32)


def _dot_nt(a, b):
    return lax.dot_general(a, b, (((1,), (1,)), ((), ())), preferred_element_type=F32)


def _dot_tn(a, b):
    return lax.dot_general(a, b, (((0,), (0,)), ((), ())), preferred_element_type=F32)


def _gelu(x):
    return 0.5 * x * (1.0 + jnp.tanh(_GC * (x + _GA * x * x * x)))


def _gelu_and_grad(x):
    x2 = x * x
    s = 0.5 * jnp.tanh((_GC * x) * (1.0 + _GA * x2)) + 0.5
    g = x * s
    return g, s + g * (1.0 - s) * ((6.0 * _GC * _GA) * x2 + 2.0 * _GC)


def _log_sigmoid(x):
    return jnp.minimum(x, 0.0) - jnp.log(1.0 + jnp.exp(-jnp.abs(x)))


def _rms(x, g):
    return x * lax.rsqrt(jnp.mean(x * x, axis=-1, keepdims=True) + EPS) * g


def _rms_bwd(dy, x, g):
    r = lax.rsqrt(jnp.mean(x * x, axis=-1, keepdims=True) + EPS)
    n = x * r
    dn = dy * g
    dx = r * (dn - n * jnp.mean(dn * n, axis=-1, keepdims=True))
    return dx, jnp.sum(dy * n, axis=0, keepdims=True)


def _ldot3(l, x):
    h = x.astype(BF)
    r = x - h.astype(F32)
    m = r.astype(BF)
    lo = (r - m.astype(F32)).astype(BF)
    return _dot(l, h) + _dot(l, m) + _dot(l, lo)


def _split_side(refs, n_in, n_out, n_scratch, side):
    si, so = (side.n_in, side.n_out) if side else (0, 0)
    cuts = [n_in, si, n_out, so, n_scratch]
    out, at = [], 0
    for c in cuts:
        out.append(refs[at:at + c])
        at += c
    return (*out, refs[at:])


def _side_specs(side):
    return ([_ANY] * side.n_in, [_ANY] * side.n_out, side.out_shapes, side.scratch, side.ins) if side else ([],) * 5


def norm_matmul(x, g, wt, bn, t, name, nblk, f32_blk, tail_blk, side=None):
    s, dm = x.shape
    w_spec = pl.BlockSpec((bn, dm), lambda i, j: (j, 0))
    nt = s // t

    def body(*refs):
        (x_ref, g_ref, w_ref, wl_ref), s_in, outs, s_out, (a_sc,), s_scr = _split_side(refs, 4, 4, 1, side)
        a_ref, z_ref, f32_ref, tail_ref = outs
        i, j = pl.program_id(0), pl.program_id(1)
        if side:
            @pl.when((i == 0) & (j == 0))
            def _():
                side.start(s_in, s_out, *s_scr)

        @pl.when(j == 0)
        def _():
            a = _rms(x_ref[...], g_ref[...]).astype(BF)
            a_sc[...] = a
            a_ref[...] = a
            tail_ref[...] = _dot_nt(a, wl_ref[...]).astype(BF)

        acc = _dot_nt(a_sc[...], w_ref[...])
        z_ref[...] = acc.astype(BF)

        @pl.when(j == f32_blk)
        def _():
            f32_ref[...] = acc
        if side:
            @pl.when((i == nt - 1) & (j == nblk - 1))
            def _():
                side.finish(s_in, s_out, *s_scr)

    si_specs, so_specs, so_shapes, s_scratch, s_ins = _side_specs(side)
    return _pc(
        body, name=name, grid=(nt, nblk),
        in_specs=[pl.BlockSpec((t, dm), lambda i, j: (i, 0)), _const((1, dm)), w_spec,
                  pl.BlockSpec((LANE, dm), lambda i, j: (tail_blk, 0), pipeline_mode=pl.Buffered(1))] + si_specs,
        out_specs=[pl.BlockSpec((t, dm), lambda i, j: (i, 0)), pl.BlockSpec((t, bn), lambda i, j: (i, j)),
                   pl.BlockSpec((t, bn), lambda i, j: (i, 0)), pl.BlockSpec((t, LANE), lambda i, j: (i, 0))] + so_specs,
        out_shape=[SDS((s, dm), BF), SDS((s, nblk * bn), BF), SDS((s, bn), F32), SDS((s, LANE), BF)] + so_shapes,
        scratch_shapes=[pltpu.VMEM((t, dm), BF)] + s_scratch, compiler_params=_cp(2))(x, g, wt, wt, *s_ins)


def _sgu_weights(wc_ref, transposed):
    r = lax.broadcasted_iota(jnp.int32, (A_C, A_G * A_C), 0)
    c = lax.broadcasted_iota(jnp.int32, (A_C, A_G * A_C), 1) & (A_C - 1)
    return jnp.where((r <= c) if transposed else (c <= r), wc_ref[...], 0.0).astype(BF)


def _sgu_spread(xs, bdm):
    return jnp.concatenate([jnp.tile(x, (A_G, 1)) * bdm for x in xs], axis=1)


def _sgu_recompute(v, lng, lnb):
    gv, dgv = _gelu_and_grad(v)
    gv = gv.astype(F32)
    mu = jnp.mean(gv, axis=-1, keepdims=True)
    xc = gv - mu
    rstd = lax.rsqrt(jnp.mean(xc * xc, axis=-1, keepdims=True) + EPS)
    xhat = xc * rstd
    return dgv, rstd, xhat, (xhat * lng + lnb).astype(BF)


def sgu_fwd(z, ln_g, ln_b, w_cat, bias_full, bdm, t):
    s = z.shape[0]
    nch = t // A_C

    def body(u_ref, v_ref, g_ref, b_ref, wc_ref, bias_ref, bdm_ref, sa_ref):
        vns = [_sgu_recompute(v_ref[pl.ds(ci * A_C, A_C), :], g_ref[...], b_ref[...])[3]
               for ci in range(nch)]
        mixed = _dot(_sgu_weights(wc_ref, False), _sgu_spread(vns, bdm_ref[...]))
        for ci in range(nch):
            rows = pl.ds(ci * A_C, A_C)
            s_ = mixed[:, ci * A_W:(ci + 1) * A_W] + bias_ref[...]
            sa_ref[rows, :] = _gelu(u_ref[rows, :]) * s_.astype(BF)

    return _pc(
        body, name="sgu_fwd", grid=(s // t,),
        in_specs=[pl.BlockSpec((t, A_W), lambda i: (i, 0)), pl.BlockSpec((t, A_W), lambda i: (i, 1)),
                  _const((1, A_W)), _const((1, A_W)), _const((A_C, A_G * A_C)), _const((A_C, A_W)),
                  _const((A_G * A_C, A_W))],
        out_specs=pl.BlockSpec((t, A_W), lambda i: (i, 0)),
        out_shape=SDS((s, A_W), BF), compiler_params=_cp(1))(z, z, ln_g, ln_b, w_cat, bias_full, bdm)


def _gla_decays(qk, lr, wgk, bias, l, t):
    nc = t // B_C
    q = qk[:, :B_HK].astype(F32) * (B_HK ** -0.5)
    k = qk[:, B_HK:].astype(F32)
    pre = _dot(lr, wgk) + bias
    la = _log_sigmoid(pre) * (1.0 / 16.0)
    b = _ldot3(l, la)
    b3 = b.reshape(nc, B_C, B_HK)
    bl = jnp.broadcast_to(b3[:, B_C - 1:B_C, :], (nc, B_C, B_HK)).reshape(t, B_HK)
    eb, enb, etb = jnp.exp(b), jnp.exp(-b), jnp.exp(bl - b)
    return pre, b, bl, eb, enb, etb, q * eb, k * enb, k * etb


def gla_fwd(z, qk32, zl, wgk, bias, wn, ltri, t):
    s = z.shape[0]
    nc = t // B_C
    hpb = GLA_HPB
    kw, vw = hpb * B_HK, hpb * B_HV

    def body(qk_ref, v_ref, og_ref, lr_ref, wgk_ref, bias_ref, wn_ref, l_ref, ob_ref, o_ref, st_ref, st_sc, o_sc):
        g = pl.program_id(1)

        @pl.when(pl.program_id(0) == 0)
        def _():
            for hh in range(hpb):
                st_sc[g * hpb + hh] = jnp.zeros((B_HV, B_HK), F32)

        lr, l = lr_ref[...], l_ref[...]
        for hh in range(hpb):
            h = g * hpb + hh
            cv, ck = slice(hh * B_HV, (hh + 1) * B_HV), slice(hh * B_HK, (hh + 1) * B_HK)
            _, _, bl, _, _, _, qd, ki, kt = _gla_decays(qk_ref[:, cv], lr, wgk_ref[:, ck], bias_ref[:, ck], l, t)
            qd, ki, kt = qd.astype(BF), ki.astype(BF), kt.astype(BF)
            vb = v_ref[:, cv]
            sc = jnp.where(l > 0, _dot_nt(qd, ki), 0.0).astype(BF)
            o_sc[hh] = _dot(sc, vb)
            for n in range(nc):
                rows = slice(n * B_C, (n + 1) * B_C)
                st = st_sc[h]
                stb = st.astype(BF)
                st_ref[n, hh] = stb
                o_sc[hh, rows, :] += _dot_nt(qd[rows], stb)
                st_sc[h] = st * jnp.exp(bl[n * B_C:n * B_C + 1, :]) + _dot_tn(vb[rows], kt[rows])
            ob = o_sc[hh].astype(BF)
            o_ref[:, cv] = ob
            og = og_ref[:, cv].astype(F32)
            ob_ref[:, cv] = (_rms(ob.astype(F32), wn_ref[...]) * og * jax.nn.sigmoid(og)).astype(BF)

    vo, go = 2048 // vw, 3072 // vw
    return _pc(
        body, name="gla_fwd", grid=(s // t, B_H // hpb),
        in_specs=[pl.BlockSpec((t, vw), lambda i, g: (i, g)), pl.BlockSpec((t, vw), lambda i, g: (i, vo + g)),
                  pl.BlockSpec((t, vw), lambda i, g: (i, go + g)), pl.BlockSpec((t, LANE), lambda i, g: (i, 0)),
                  pl.BlockSpec((LANE, kw), lambda i, g: (0, g)), pl.BlockSpec((1, kw), lambda i, g: (0, g)),
                  _const((1, B_HV)), _const((t, t))],
        out_specs=[pl.BlockSpec((t, vw), lambda i, g: (i, g)), pl.BlockSpec((t, vw), lambda i, g: (i, g)),
                   pl.BlockSpec((nc, hpb, B_HV, B_HK), lambda i, g: (i, g, 0, 0))],
        out_shape=[SDS((s, D), BF), SDS((s, D), BF), SDS((s // B_C, B_H, B_HV, B_HK), BF)],
        scratch_shapes=[pltpu.VMEM((B_H, B_HV, B_HK), F32), pltpu.VMEM((hpb, t, B_HV), F32)],
        compiler_params=_cp(2))(qk32, z, z, zl, wgk, bias, wn, ltri)


def mix_fwd(sa, ob, z, x, a_out, b_out, w_mix, g2, t):
    s = x.shape[0]

    def body(sa_ref, ob_ref, ga_ref, gb_ref, x_ref, ao_ref, bo_ref, wm_ref, g2_ref,
             ya_ref, yb_ref, mp_ref, mx_ref, h1_ref):
        ya = _dot(sa_ref[...], ao_ref[...]).astype(BF)
        yb = _dot(ob_ref[...], bo_ref[...]).astype(BF)
        ya_ref[...] = ya
        yb_ref[...] = yb
        mp = (jax.nn.sigmoid(ga_ref[...].astype(F32)) * ya.astype(F32)
              + jax.nn.sigmoid(gb_ref[...].astype(F32)) * yb.astype(F32)).astype(BF)
        mp_ref[...] = mp
        mx = _dot(mp, wm_ref[...]).astype(BF)
        mx_ref[...] = mx
        h1_ref[...] = x_ref[...] + _rms(mx.astype(F32), g2_ref[...])

    row = lambda w: pl.BlockSpec((t, w), lambda i: (i, 0))
    return _pc(
        body, name="mix_fwd", grid=(s // t,),
        in_specs=[row(A_W), row(D), pl.BlockSpec((t, D), lambda i: (i, 4)), pl.BlockSpec((t, D), lambda i: (i, 5)),
                  row(D), _const((A_W, D)), _const((D, D)), _const((D, D)), _const((1, D))],
        out_specs=[row(D)] * 5,
        out_shape=[SDS((s, D), BF)] * 4 + [SDS((s, D), F32)],
        compiler_params=_cp(1))(sa, ob, z, z, x, a_out, b_out, w_mix, g2)


def ffn_up_fwd(h1, g3, w_up3, conv_w, conv_b, t):
    s = h1.shape[0]
    bn = w_up3.shape[2]

    def body(x_ref, g_ref, wg_ref, wv_ref, cwg_ref, cwv_ref, cbg_ref, cbv_ref,
             c_ref, ug_ref, uv_ref, cg_ref, cv_ref, ff_ref, c_sc, carry):
        i, j = pl.program_id(0), pl.program_id(1)

        @pl.when(j == 0)
        def _():
            c = _rms(x_ref[...], g_ref[...]).astype(BF)
            c_sc[...] = c
            c_ref[...] = c

        @pl.when(i == 0)
        def _():
            carry[j] = jnp.zeros((2, 8, bn), F32)

        def branch(k, w_ref, cw_ref, cb_ref, u_ref, o_ref):
            ub = _dot(c_sc[...], w_ref[...]).astype(BF)
            u_ref[...] = ub
            u = ub.astype(F32)
            ext = jnp.concatenate([carry[j, k], u], axis=0)
            carry[j, k] = u[t - 8:]
            w = cw_ref[...]
            cc = (cb_ref[...] + w[0:1] * pltpu.roll(ext, 2, 0) + w[1:2] * pltpu.roll(ext, 1, 0) + w[2:3] * ext)[8:]
            cc = cc.astype(BF)
            o_ref[...] = cc
            return cc

        g = _gelu(branch(0, wg_ref, cwg_ref, cbg_ref, ug_ref, cg_ref))
        ff_ref[...] = g * branch(1, wv_ref, cwv_ref, cbv_ref, uv_ref, cv_ref)

    col = lambda rows, off: pl.BlockSpec((rows, bn), lambda i, j: (0, j + off))
    out = pl.BlockSpec((t, bn), lambda i, j: (i, j))
    return _pc(
        body, name="ffn_up_fwd", grid=(s // t, 2),
        in_specs=[pl.BlockSpec((t, D), lambda i, j: (i, 0)), _const((1, D)),
                  pl.BlockSpec((None, D, bn), lambda i, j: (j, 0, 0)), pl.BlockSpec((None, D, bn), lambda i, j: (j + 2, 0, 0)),
                  col(3, 0), col(3, 2), col(1, 0), col(1, 2)],
        out_specs=[pl.BlockSpec((t, D), lambda i, j: (i, 0))] + [out] * 5,
        out_shape=[SDS((s, D), BF)] + [SDS((s, D_FF), BF)] * 5,
        scratch_shapes=[pltpu.VMEM((t, D), BF), pltpu.VMEM((2, 2, 8, bn), F32)],
        compiler_params=_cp(2))(h1, g3, w_up3, w_up3, conv_w, conv_w, conv_b, conv_b)


def ffn_up_bwd(ug, uv, cg, cv, dff, conv_w, w_up, h1, g3, dh2, t):
    s = h1.shape[0]
    nt = s // t
    hb = t // 8
    bn = 1408
    r = t + 8

    def body(ug_ref, uv_ref, cg_ref, cv_ref, cag_ref, cav_ref, d_ref, da_ref, cw_ref, w_ref, x_ref, g_ref, dres_ref,
             dug_ref, duv_ref, gw_ref, gb_ref, dx_ref, gg_ref):
        i = pl.program_id(0)

        @pl.when(i == 0)
        def _():
            gw_ref[...] = jnp.zeros((3, 2 * D_FF), F32)
            gb_ref[...] = jnp.zeros((1, 2 * D_FF), F32)
            gg_ref[...] = jnp.zeros((1, D), F32)

        more = (i < nt - 1).astype(BF)

        def gate(c_g, c_v, d_):
            gl, dgl = _gelu_and_grad(c_g)
            return (d_ * c_v * dgl).astype(F32), (d_ * gl).astype(F32)

        def back(dc, dc_next, u_ref, cols, off, du_ref):
            w = cw_ref[:, off:off + bn]
            d_ext = jnp.concatenate([dc, dc_next], axis=0)
            d1, d2 = pltpu.roll(d_ext, r - 1, 0)[:t], pltpu.roll(d_ext, r - 2, 0)[:t]
            du_ref[:, cols] = (w[2:3] * dc + w[1:2] * d1 + w[0:1] * d2).astype(BF)
            u = u_ref[:, cols].astype(F32)
            gw_ref[0:1, off:off + bn] += jnp.sum(d2 * u, axis=0, keepdims=True)
            gw_ref[1:2, off:off + bn] += jnp.sum(d1 * u, axis=0, keepdims=True)
            gw_ref[2:3, off:off + bn] += jnp.sum(dc * u, axis=0, keepdims=True)
            gb_ref[:, off:off + bn] += jnp.sum(dc, axis=0, keepdims=True)

        for kb in range(D_FF // bn):
            cols = slice(kb * bn, (kb + 1) * bn)
            dg, dv = gate(cg_ref[:, cols], cv_ref[:, cols], d_ref[:, cols])
            dg_n, dv_n = gate(cag_ref[:, cols], cav_ref[:, cols], da_ref[:, cols] * more)
            back(dg, dg_n, ug_ref, cols, kb * bn, dug_ref)
            back(dv, dv_n, uv_ref, cols, D_FF + kb * bn, duv_ref)

        acc = _dot_nt(dug_ref[...], w_ref[:, :D_FF]) + _dot_nt(duv_ref[...], w_ref[:, D_FF:])
        dxn, dg3 = _rms_bwd(acc, x_ref[...], g_ref[...])
        dx_ref[...] = dres_ref[...] + dxn
        gg_ref[...] += dg3

    tile = lambda width: pl.BlockSpec((t, width), lambda i: (i, 0))
    after = pl.BlockSpec((8, D_FF), lambda i: (jnp.minimum((i + 1) * hb, nt * hb - 1), 0))
    return _pc(
        body, name="ffn_up_bwd", grid=(nt,),
        in_specs=[tile(D_FF)] * 4 + [after, after, tile(D_FF), after, _const((3, 2 * D_FF)), _const((D, 2 * D_FF)),
                                     tile(D), _const((1, D)), tile(D)],
        out_specs=[tile(D_FF), tile(D_FF), _acc((3, 2 * D_FF)), _acc((1, 2 * D_FF)), tile(D), _acc((1, D))],
        out_shape=[SDS((s, D_FF), BF), SDS((s, D_FF), BF), SDS((3, 2 * D_FF), F32), SDS((1, 2 * D_FF), F32),
                   SDS((s, D), F32), SDS((1, D), F32)],
        compiler_params=_cp(1))(ug, uv, cg, cv, cg, cv, dff, dff, conv_w, w_up, h1, g3, dh2)


def out_fwd(ff, h1, p, tgt, w_down, w_pg, w_ple, g4, g5, t):
    s = h1.shape[0]

    def body(ff_ref, h1_ref, p_ref, t_ref, wd_ref, wpg_ref, wpl_ref, g4_ref, g5_ref,
             f_ref, h2_ref, pg_ref, pe_ref, dy_ref, loss_ref):
        @pl.when(pl.program_id(0) == 0)
        def _():
            loss_ref[...] = jnp.zeros((1, 1), F32)

        f = _dot(ff_ref[...], wd_ref[...]).astype(BF)
        f_ref[...] = f
        h2 = h1_ref[...] + _rms(f.astype(F32), g4_ref[...])
        h2b = h2.astype(BF)
        h2_ref[...] = h2b
        pg = _dot(h2b, wpg_ref[...]).astype(BF)
        pe = _dot(p_ref[...].astype(BF), wpl_ref[...]).astype(BF)
        pg_ref[...] = pg
        pe_ref[...] = pe
        y = h2 + _rms(jax.nn.sigmoid(pg.astype(F32)) * pe.astype(F32), g5_ref[...])
        err = y - t_ref[...]
        dy_ref[...] = err * (1.0 / D)
        loss_ref[...] += (0.5 / D) * jnp.sum(err * err)

    row = lambda w: pl.BlockSpec((t, w), lambda i: (i, 0))
    return _pc(
        body, name="out_fwd", grid=(s // t,),
        in_specs=[row(D_FF), row(D), row(PLE), row(D), _const((D_FF, D)), _const((D, D)), _const((PLE, D)),
                  _const((1, D)), _const((1, D))],
        out_specs=[row(D)] * 5 + [_acc((1, 1))],
        out_shape=[SDS((s, D), BF)] * 4 + [SDS((s, D), F32), SDS((1, 1), F32)],
        compiler_params=_cp(1))(ff, h1, p, tgt, w_down, w_pg, w_ple, g4, g5)


def out_bwd(dy, pg, pe, f, g5, g4, w_pg, w_down, t):
    s = dy.shape[0]

    def body(dy_ref, pg_ref, pe_ref, f_ref, g5_ref, g4_ref, wpg_ref, wd_ref,
             dh2_ref, dpe_ref, dpg_ref, df_ref, dff_ref, gg5_ref, gg4_ref):
        @pl.when(pl.program_id(0) == 0)
        def _():
            gg5_ref[...] = jnp.zeros((1, D), F32)
            gg4_ref[...] = jnp.zeros((1, D), F32)

        dy_ = dy_ref[...]
        pg_ = pg_ref[...].astype(F32)
        pe_ = pe_ref[...].astype(F32)
        sg = jax.nn.sigmoid(pg_)
        dple, dg5 = _rms_bwd(dy_, sg * pe_, g5_ref[...])
        gg5_ref[...] += dg5
        dpe_ref[...] = (dple * sg).astype(BF)
        dpg = (dple * pe_ * sg * (1.0 - sg)).astype(BF)
        dpg_ref[...] = dpg
        dh2 = dy_ + _dot_nt(dpg, wpg_ref[...])
        dh2_ref[...] = dh2
        df, dg4 = _rms_bwd(dh2, f_ref[...].astype(F32), g4_ref[...])
        gg4_ref[...] += dg4
        dfb = df.astype(BF)
        df_ref[...] = dfb
        dff_ref[...] = _dot_nt(dfb, wd_ref[...]).astype(BF)

    row = lambda w: pl.BlockSpec((t, w), lambda i: (i, 0))
    return _pc(
        body, name="out_bwd", grid=(s // t,),
        in_specs=[row(D), row(D), row(D), row(D), _const((1, D)), _const((1, D)), _const((D, D)), _const((D_FF, D))],
        out_specs=[row(D), row(D), row(D), row(D), row(D_FF), _acc((1, D)), _acc((1, D))],
        out_shape=[SDS((s, D), F32), SDS((s, D), BF), SDS((s, D), BF), SDS((s, D), BF), SDS((s, D_FF), BF),
                   SDS((1, D), F32), SDS((1, D), F32)],
        compiler_params=_cp(1))(dy, pg, pe, f, g5, g4, w_pg, w_down)


def nt_normbwd(dys, w, xin, gain, dres, t, name, side=None):
    s = xin.shape[0]
    nt = s // t
    np_ = len(dys)

    def body(*refs):
        ins_, s_in, (dx_ref, gg_ref), s_out, _, s_scr = _split_side(refs, np_ + 4, 2, 0, side)
        dy_refs = ins_[:np_]
        w_ref, x_ref, g_ref, dres_ref = ins_[np_:]
        i = pl.program_id(0)

        @pl.when(i == 0)
        def _():
            gg_ref[...] = jnp.zeros((1, D), F32)
            if side:
                side.start(s_in, s_out, *s_scr)

        acc = _dot(jnp.concatenate([r_[...] for r_ in dy_refs], axis=1), w_ref[...])
        dxn, dg = _rms_bwd(acc, x_ref[...], g_ref[...])
        dx_ref[...] = dres_ref[...] + dxn
        gg_ref[...] += dg
        if side:
            @pl.when(i == nt - 1)
            def _():
                side.finish(s_in, s_out, *s_scr)

    row = lambda width: pl.BlockSpec((t, width), lambda i: (i, 0))
    si_specs, so_specs, so_shapes, s_scratch, s_ins = _side_specs(side)
    assert sum(dy.shape[1] for dy in dys) == w.shape[0]
    return _pc(
        body, name=name, grid=(nt,),
        in_specs=[row(dy.shape[1]) for dy in dys] + [_const(w.shape), row(D), _const((1, D)), row(D)] + si_specs,
        out_specs=[row(D), _acc((1, D))] + so_specs,
        out_shape=[SDS((s, D), F32), SDS((1, D), F32)] + so_shapes, scratch_shapes=s_scratch,
        compiler_params=_cp(1))(*dys, w, xin, gain, dres, *s_ins)


def mix_bwd(dh1, mx, z, ya, yb, g2, w_mix, a_out, b_out, t):
    s = dh1.shape[0]

    def body(dh_ref, mx_ref, ga_ref, gb_ref, ya_ref, yb_ref, g2_ref, wm_ref, ao_ref, bo_ref,
             dmx_ref, dya_ref, dyb_ref, dga_ref, dgb_ref, dsa_ref, dob_ref, gg2_ref):
        @pl.when(pl.program_id(0) == 0)
        def _():
            gg2_ref[...] = jnp.zeros((1, D), F32)

        dmx, dg2 = _rms_bwd(dh_ref[...], mx_ref[...].astype(F32), g2_ref[...])
        gg2_ref[...] += dg2
        dmxb = dmx.astype(BF)
        dmx_ref[...] = dmxb
        dmp = _dot_nt(dmxb, wm_ref[...]).astype(BF)

        def gate(g_ref, y_ref, dy_ref, dg_ref, w_ref, dz_ref):
            sg = jax.nn.sigmoid(g_ref[...])
            dyb_ = dmp * sg
            dy_ref[...] = dyb_
            dg_ref[...] = dyb_ * y_ref[...] * (1.0 - sg)
            dz_ref[...] = _dot_nt(dyb_, w_ref[...]).astype(BF)

        gate(ga_ref, ya_ref, dya_ref, dga_ref, ao_ref, dsa_ref)
        gate(gb_ref, yb_ref, dyb_ref, dgb_ref, bo_ref, dob_ref)

    row = lambda w: pl.BlockSpec((t, w), lambda i: (i, 0))
    return _pc(
        body, name="mix_bwd", grid=(s // t,),
        in_specs=[row(D), row(D), pl.BlockSpec((t, D), lambda i: (i, 4)), pl.BlockSpec((t, D), lambda i: (i, 5)),
                  row(D), row(D), _const((1, D)), _const((D, D)), _const((A_W, D)), _const((D, D))],
        out_specs=[row(D)] * 5 + [row(A_W), row(D), _acc((1, D))],
        out_shape=[SDS((s, D), BF)] * 5 + [SDS((s, A_W), BF), SDS((s, D), BF), SDS((1, D), F32)],
        compiler_params=_cp(1))(dh1, mx, z, z, ya, yb, g2, w_mix, a_out, b_out)


def sgu_bwd(z, dsa, ln_g, ln_b, w_cat, w_cat_t, bias_full, bdm, t):
    s = z.shape[0]
    nt = s // t
    nch = t // A_C

    def body(u_ref, v_ref, dsa_ref, g_ref, b_ref, wc_ref, wct_ref, bias_ref, bdm_ref,
             duv_ref, glg_ref, glb_ref, gws_ref, gbs_ref, ds_acc):
        i = pl.program_id(0)

        @pl.when(i == 0)
        def _():
            glg_ref[...] = jnp.zeros((1, A_W), F32)
            glb_ref[...] = jnp.zeros((1, A_W), F32)
            gws_ref[...] = jnp.zeros((A_C, A_G * A_C), F32)
            ds_acc[...] = jnp.zeros((A_C, A_W), F32)

        lng, bdm_ = g_ref[...], bdm_ref[...]
        rec = [_sgu_recompute(v_ref[pl.ds(ci * A_C, A_C), :], lng, b_ref[...]) for ci in range(nch)]
        spread_vn = _sgu_spread([r_[3] for r_ in rec], bdm_)
        mixed = _dot(_sgu_weights(wc_ref, False), spread_vn)
        dsas, dss, dgus = [], [], []
        for ci in range(nch):
            rows = pl.ds(ci * A_C, A_C)
            gu, dgu = _gelu_and_grad(u_ref[rows, :])
            dsa_ = dsa_ref[rows, :]
            ds = dsa_ * gu
            ds_acc[...] += ds.astype(F32)
            dsas.append(dsa_)
            dgus.append(dgu)
            dss.append(ds)
        r = lax.broadcasted_iota(jnp.int32, (A_C, A_G * A_C), 0)
        c = lax.broadcasted_iota(jnp.int32, (A_C, A_G * A_C), 1) & (A_C - 1)
        gws_ref[...] += jnp.where(c <= r, _dot_nt(jnp.concatenate(dss, axis=1), spread_vn), 0.0)
        dvns = _dot(_sgu_weights(wct_ref, True), _sgu_spread(dss, bdm_))
        for ci in range(nch):
            rows = pl.ds(ci * A_C, A_C)
            dgv, rstd, xhat, _ = rec[ci]
            dvn = dvns[:, ci * A_W:(ci + 1) * A_W]
            glb_ref[...] += jnp.sum(dvn, axis=0, keepdims=True)
            glg_ref[...] += jnp.sum(dvn * xhat, axis=0, keepdims=True)
            dxh = dvn * lng
            dgv_ = rstd * (dxh - jnp.mean(dxh, axis=-1, keepdims=True)
                           - xhat * jnp.mean(dxh * xhat, axis=-1, keepdims=True))
            s_ = mixed[:, ci * A_W:(ci + 1) * A_W] + bias_ref[...]
            duv_ref[rows, :A_W] = dsas[ci] * dgus[ci] * s_.astype(BF)
            duv_ref[rows, A_W:] = (dgv_ * dgv).astype(BF)

        @pl.when(i == nt - 1)
        def _():
            acc = ds_acc[...]
            for g in range(A_G):
                gbs_ref[:, g:g + 1] = jnp.sum(acc[:, g * A_GD:(g + 1) * A_GD], axis=1, keepdims=True)

    return _pc(
        body, name="sgu_bwd", grid=(nt,),
        in_specs=[pl.BlockSpec((t, A_W), lambda i: (i, 0)), pl.BlockSpec((t, A_W), lambda i: (i, 1)),
                  pl.BlockSpec((t, A_W), lambda i: (i, 0)),
                  _const((1, A_W)), _const((1, A_W)), _const((A_C, A_G * A_C)), _const((A_C, A_G * A_C)),
                  _const((A_C, A_W)), _const((A_G * A_C, A_W))],
        out_specs=[pl.BlockSpec((t, D), lambda i: (i, 0)), _acc((1, A_W)), _acc((1, A_W)),
                   _acc((A_C, A_G * A_C)), _acc((A_C, A_G))],
        out_shape=[SDS((s, D), BF), SDS((1, A_W), F32), SDS((1, A_W), F32), SDS((A_C, A_G * A_C), F32),
                   SDS((A_C, A_G), F32)],
        scratch_shapes=[pltpu.VMEM((A_C, A_W), F32)],
        compiler_params=_cp(1))(z, z, dsa, ln_g, ln_b, w_cat, w_cat_t, bias_full, bdm)


def gla_bwd(z, qk32, zl, o, dob, states, wgk, bias, wn, ltri, ltri_t, t, side=None):
    s = z.shape[0]
    nt = s // t
    nc = t // B_C
    hpb = GLA_HPB
    assert hpb == B_H
    kw, vw = hpb * B_HK, hpb * B_HV

    def body(*refs):
        ins_, s_in, outs_, s_out, scr_, s_scr = _split_side(refs, 12, 7, 5, side)
        qk_ref, v_ref, og_ref, lr_ref, o_ref, dob_ref, st_ref, wgk_ref, bias_ref, wn_ref, l_ref, lt_ref = ins_
        dqk_ref, dv_ref, dog_ref, dpre_ref, dlr_ref, gbias_ref, gwn_ref = outs_
        dst_sc, dv_sc, dqd_sc, dkt_sc, ddec_sc = scr_
        i = pl.program_id(0)
        g = pl.program_id(1)

        @pl.when((i == 0) & (g == 0))
        def _():
            gbias_ref[...] = jnp.zeros((B_H, 1, B_HK), F32)
            gwn_ref[...] = jnp.zeros((1, B_HV), F32)
            if side:
                side.start(s_in, s_out, *s_scr)

        @pl.when(i == 0)
        def _():
            for hh in range(hpb):
                dst_sc[g * hpb + hh] = jnp.zeros((B_HV, B_HK), F32)

        lr, l, lt = lr_ref[...], l_ref[...], lt_ref[...]
        keep, keep_t = l > 0, lt > 0
        wn_ = wn_ref[...]
        last = lax.broadcasted_iota(jnp.int32, (nc, B_C, B_HK), 1) == B_C - 1
        for hh in range(hpb):
            h = g * hpb + hh
            cv, ck = slice(hh * B_HV, (hh + 1) * B_HV), slice(hh * B_HK, (hh + 1) * B_HK)
            pre, b, bl, eb, enb, etb, qd, ki, kt = _gla_decays(qk_ref[:, cv], lr, wgk_ref[:, ck], bias_ref[:, ck], l, t)
            qdb, kib, ktb = qd.astype(BF), ki.astype(BF), kt.astype(BF)
            vb = v_ref[:, cv]
            o_ = o_ref[:, cv].astype(F32)
            og = og_ref[:, cv].astype(F32)
            sog = jax.nn.sigmoid(og)
            dob_ = dob_ref[:, cv].astype(F32)
            don = dob_ * og * sog
            do, dwn = _rms_bwd(don, o_, wn_)
            gwn_ref[...] += dwn
            dog_ref[:, cv] = (dob_ * _rms(o_, wn_) * sog * (1.0 + og * (1.0 - sog))).astype(BF)
            dob16 = do.astype(BF)
            sc_t = jnp.where(keep_t, _dot_nt(kib, qdb), 0.0).astype(BF)
            dsc = jnp.where(keep, _dot_nt(dob16, vb), 0.0).astype(BF)
            dsc_t = jnp.where(keep_t, _dot_nt(vb, dob16), 0.0).astype(BF)
            dv_sc[hh] = _dot(sc_t, dob16)
            dqd_sc[hh] = _dot(dsc, kib)
            dki = _dot(dsc_t, qdb)
            for n in reversed(range(nc)):
                rows = slice(n * B_C, (n + 1) * B_C)
                dst = dst_sc[h]
                dstb = dst.astype(BF)
                stp = st_ref[n, hh]
                dv_sc[hh, rows, :] += _dot_nt(ktb[rows], dstb)
                dkt_sc[hh, rows, :] = _dot(vb[rows], dstb)
                dqd_sc[hh, rows, :] += _dot(dob16[rows], stp)
                dec = jnp.exp(bl[n * B_C:n * B_C + 1, :])
                ddec_sc[hh, n] = jnp.sum(dst * stp.astype(F32), axis=0, keepdims=True) * dec
                dst_sc[h] = dst * dec + _dot_tn(dob16[rows], qdb[rows])
            dqd, dkt = dqd_sc[hh], dkt_sc[hh]
            dv_ref[:, cv] = dv_sc[hh].astype(BF)
            dqk_ref[:, hh * B_HV:hh * B_HV + B_HK] = (dqd * eb * (B_HK ** -0.5)).astype(BF)
            dqk_ref[:, hh * B_HV + B_HK:(hh + 1) * B_HV] = (dki * enb + dkt * etb).astype(BF)
            dktkt = dkt * kt
            db3 = (dqd * qd - dki * ki - dktkt).reshape(nc, B_C, B_HK)
            dbl = jnp.sum(dktkt.reshape(nc, B_C, B_HK), axis=1, keepdims=True) + ddec_sc[hh]
            db = (db3 + jnp.where(last, dbl, 0.0)).reshape(t, B_HK)
            dla = _ldot3(lt, db)
            dpre = dla * (1.0 / 16.0) * (1.0 - jax.nn.sigmoid(pre))
            dpreb = dpre.astype(BF)
            dpre_ref[:, ck] = dpreb
            gbias_ref[h] += jnp.sum(dpre, axis=0, keepdims=True)
            dlr_h = _dot_nt(dpreb, wgk_ref[:, ck])
            dlr = dlr_h if hh == 0 else dlr + dlr_h
        dlr_ref[...] = dlr.astype(BF)
        if side:
            @pl.when((i == nt - 1) & (g == B_H // hpb - 1))
            def _():
                side.finish(s_in, s_out, *s_scr)

    rv = lambda i: nt - 1 - i
    si_specs, so_specs, so_shapes, s_scratch, s_ins = _side_specs(side)
    vo, go = 2048 // vw, 3072 // vw
    tile = lambda off: pl.BlockSpec((t, vw), lambda i, g: (rv(i), off + g))
    return _pc(
        body, name="gla_bwd", grid=(nt, B_H // hpb),
        in_specs=[tile(0), tile(vo), tile(go), pl.BlockSpec((t, LANE), lambda i, g: (rv(i), 0)), tile(0), tile(0),
                  pl.BlockSpec((nc, hpb, B_HV, B_HK), lambda i, g: (rv(i), g, 0, 0)),
                  pl.BlockSpec((LANE, kw), lambda i, g: (0, g)), pl.BlockSpec((1, kw), lambda i, g: (0, g)),
                  _const((1, B_HV)), _const((t, t)), _const((t, t))] + si_specs,
        out_specs=[tile(0), tile(0), tile(0), pl.BlockSpec((t, kw), lambda i, g: (rv(i), g)),
                   pl.BlockSpec((t, LANE), lambda i, g: (rv(i), 0)), _acc((B_H, 1, B_HK)), _acc((1, B_HV))] + so_specs,
        out_shape=[SDS((s, D), BF), SDS((s, D), BF), SDS((s, D), BF), SDS((s, B_H * B_HK), BF), SDS((s, LANE), BF),
                   SDS((B_H, 1, B_HK), F32), SDS((1, B_HV), F32)] + so_shapes,
        scratch_shapes=[pltpu.VMEM((B_H, B_HV, B_HK), F32), pltpu.VMEM((hpb, t, B_HV), F32),
                        pltpu.VMEM((hpb, t, B_HK), F32), pltpu.VMEM((hpb, t, B_HK), F32),
                        pltpu.VMEM((hpb, nc, 1, B_HK), F32)] + s_scratch,
        compiler_params=_cp(2))(qk32, z, z, zl, o, dob, states, wgk, bias, wn, ltri, ltri_t, *s_ins)


def mm_tn(a, b, name, tk=2048):
    s, m = a.shape
    n = b.shape[1]
    bn = next(c for c in (1024, 1408, 512, 256, 128) if n % c == 0 and m * c * 4 <= 6 * 1024 * 1024)
    tk = min(tk, s)
    nk = s // tk

    def body(a_ref, b_ref, o_ref, acc):
        k = pl.program_id(1)

        @pl.when(k == 0)
        def _():
            acc[...] = jnp.zeros((m, bn), F32)

        acc[...] += _dot_tn(a_ref[...].astype(BF), b_ref[...])

        @pl.when(k == nk - 1)
        def _():
            o_ref[...] = acc[...].astype(BF)

    return _pc(
        body, name=name, grid=(n // bn, nk),
        in_specs=[pl.BlockSpec((tk, m), lambda j, k: (k, 0)), pl.BlockSpec((tk, bn), lambda j, k: (k, j))],
        out_specs=pl.BlockSpec((m, bn), lambda j, k: (0, j)),
        out_shape=SDS((m, n), BF), scratch_shapes=[pltpu.VMEM((m, bn), F32)], compiler_params=_cp(2))(a, b)


def _adamw(w, g, m, v):
    m = ADAM_B1 * m + (1.0 - ADAM_B1) * g
    v = ADAM_B2 * v + (1.0 - ADAM_B2) * (g * g)
    m_hat = m / (1.0 - ADAM_B1 ** ADAM_STEP)
    v_hat = v / (1.0 - ADAM_B2 ** ADAM_STEP)
    return -ADAM_LR * (m_hat / (jnp.sqrt(v_hat) + ADAM_EPS) + ADAM_WD * w), m, v


def _half_rows(rows):
    rh = rows // 2
    return rh, max(b for b in range(16, 257, 16) if rh % b == 0)


def _pc_sp(body, grid, in_specs, out_specs, out_shape, name):
    gs = pltpu.PrefetchScalarGridSpec(num_scalar_prefetch=1, grid=grid, in_specs=in_specs, out_specs=out_specs)
    return _pc(body, grid_spec=gs, out_shape=out_shape, name=name, compiler_params=_cp(len(grid)))


def adamw_halves(sc, own, sib, w, m, v, name):
    rows, cols = w.shape
    rh, br = _half_rows(rows)
    nbk = rh // br

    def body(sc_ref, own_ref, sib_ref, w_ref, m_ref, v_ref, go_ref, d_ref, mo_ref, vo_ref):
        g_ = jnp.where(pl.program_id(0) // nbk == sc_ref[0], own_ref[...], sib_ref[...])
        go_ref[...] = g_
        d_ref[...], mo_ref[...], vo_ref[...] = _adamw(w_ref[...], g_, m_ref[...], v_ref[...])

    half = pl.BlockSpec((br, cols), lambda i, sc_: (i % nbk, 0))
    blk = pl.BlockSpec((br, cols), lambda i, sc_: (i, 0))
    return _pc_sp(body, (2 * nbk,), [half, half, blk, blk, blk], [blk] * 4, [SDS((rows, cols), F32)] * 4,
                  name)(sc, own, sib, w, m, v)


def adamw_cols(sc, own, sib, w, m, v, name, cb=256):
    rows, cols = w.shape
    nk = cols // 2 // cb

    def body(sc_ref, own_ref, sib_ref, w_ref, m_ref, v_ref, go_ref, d_ref, mo_ref, vo_ref):
        g_ = jnp.where(pl.program_id(0) == sc_ref[0], own_ref[...], sib_ref[...])
        go_ref[...] = g_
        d_ref[...], mo_ref[...], vo_ref[...] = _adamw(w_ref[...], g_, m_ref[...], v_ref[...])

    half = pl.BlockSpec((rows, cb), lambda h, k, sc_: (0, k))
    blk = pl.BlockSpec((rows, cb), lambda h, k, sc_: (0, h * nk + k))
    return _pc_sp(body, (2, nk), [half, half, blk, blk, blk], [blk] * 4, [SDS((rows, cols), F32)] * 4,
                  name)(sc, own, sib, w, m, v)


def adamw_small(g, w, m, v):
    def body(g_ref, w_ref, m_ref, v_ref, d_ref, mo_ref, vo_ref):
        d_ref[...], mo_ref[...], vo_ref[...] = _adamw(w_ref[...], g_ref[...], m_ref[...], v_ref[...])

    vm = pl.BlockSpec(memory_space=pltpu.VMEM)
    return _pc(body, name="adamw_small", in_specs=[vm] * 4, out_specs=[vm] * 3, out_shape=[SDS(g.shape, F32)] * 3,
               compiler_params=pltpu.CompilerParams(vmem_limit_bytes=VMEM_LIMIT))(g, w, m, v)


def _pos():
    return lax.axis_index("x"), lax.axis_index("y"), lax.axis_index("c")


def _other_chips(x, y):
    return [(1 - x, y), (x, 1 - y), (1 - x, 1 - y)]


_ANY = pl.BlockSpec(memory_space=pltpu.HBM)


class _Side:
    def __init__(self, ins, out_shapes, nsem, start, finish):
        self.ins, self.out_shapes, self.start, self.finish = list(ins), list(out_shapes), start, finish
        self.scratch = [pltpu.SemaphoreType.DMA((nsem,)), pltpu.SemaphoreType.DMA((nsem,))]
        self.n_in, self.n_out = len(self.ins), len(self.out_shapes)


def _run_side(side, name):
    def body(*refs):
        args_ = (refs[:side.n_in], refs[side.n_in:side.n_in + side.n_out], *refs[side.n_in + side.n_out:])
        side.start(*args_)
        side.finish(*args_)

    return _pc(body, name=name, in_specs=[_ANY] * side.n_in, out_specs=[_ANY] * side.n_out,
               out_shape=side.out_shapes, scratch_shapes=side.scratch)(*side.ins)


def _split_rows(shape):
    return (shape[0] // 2) % 16 == 0


def _core_halves(shape, c):
    if _split_rows(shape):
        h = shape[0] // 2
        return ((pl.ds(pl.multiple_of(c * h, 16), h), slice(None)),
                (pl.ds(pl.multiple_of((1 - c) * h, 16), h), slice(None)))
    h = shape[1] // 2
    assert h % LANE == 0
    return ((slice(None), pl.ds(pl.multiple_of(c * h, LANE), h)),
            (slice(None), pl.ds(pl.multiple_of((1 - c) * h, LANE), h)))


def gather_side(bigs, tinies):
    nb, nt_ = len(bigs), len(tinies)

    def plan(ins, outs, ssem, rsem):
        x, y, c = _pos()
        me = 2 * x + y
        chips = _other_chips(x, y)
        sibling = (x, y, 1 - c)

        def copy(k, src, dst, to):
            return pltpu.make_async_remote_copy(src_ref=src, dst_ref=dst, send_sem=ssem.at[k], recv_sem=rsem.at[k],
                                                device_id=to, device_id_type=MESH)

        sends, landed, passed_on, tiny_landed = [], [], [], []
        for w in range(nb):
            mine, theirs = _core_halves(bigs[w].shape, c)
            for j, (cx, cy) in enumerate(chips):
                sends.append(copy(6 * w + j, ins[w].at[mine], outs[w].at[(me,) + mine], (cx, cy, c)))
                blk = outs[w].at[(2 * cx + cy,) + mine]
                landed.append((copy(6 * w + j, blk, blk, (cx, cy, c)), copy(6 * w + 3 + j, blk, blk, sibling)))
                blk = outs[w].at[(2 * cx + cy,) + theirs]
                passed_on.append(copy(6 * w + 3 + j, blk, blk, sibling))
        for w in range(nt_):
            for j, (cx, cy) in enumerate(chips):
                k = 6 * nb + 3 * w + j
                sends.append(copy(k, ins[nb + w], outs[nb + w].at[me], (cx, cy, c)))
                blk = outs[nb + w].at[2 * cx + cy]
                tiny_landed.append(copy(k, blk, blk, (cx, cy, c)))
        return sends, landed, passed_on, tiny_landed

    def start(ins, outs, ssem, rsem):
        for cp in plan(ins, outs, ssem, rsem)[0]:
            cp.start()

    def finish(ins, outs, ssem, rsem):
        sends, landed, passed_on, tiny_landed = plan(ins, outs, ssem, rsem)
        for arrived, forward in landed:
            arrived.wait_recv()
            forward.start()
        for arrived in tiny_landed + passed_on:
            arrived.wait_recv()
        for cp in sends + [forward for _, forward in landed]:
            cp.wait_send()

    return _Side(list(bigs) + list(tinies), [SDS((4,) + a.shape, a.dtype) for a in list(bigs) + list(tinies)],
                 6 * nb + 3 * nt_, start, finish)


def swap_halves(gs, name):
    n = len(gs)

    def half_shape(g):
        l, r, cols = g.shape
        return (l, r // 2, cols) if _split_rows((r, cols)) else (l, r, cols // 2)

    def body(*refs):
        g_refs, sib_refs = refs[:n], refs[n:2 * n]
        ssem, rsem = refs[2 * n:]
        x, y, c = _pos()
        cps = []
        for w in range(n):
            give = (slice(None),) + _core_halves(gs[w].shape[1:], c)[1]
            cp = pltpu.make_async_remote_copy(src_ref=g_refs[w].at[give], dst_ref=sib_refs[w], send_sem=ssem.at[w],
                                              recv_sem=rsem.at[w], device_id=(x, y, 1 - c), device_id_type=MESH)
            cp.start()
            cps.append(cp)
        for cp in cps:
            cp.wait()

    return _pc(body, name=name, in_specs=[_ANY] * n, out_specs=[_ANY] * n,
               out_shape=[SDS(half_shape(g), g.dtype) for g in gs],
               scratch_shapes=[pltpu.SemaphoreType.DMA((n,)), pltpu.SemaphoreType.DMA((n,))])(*gs)


COL_BLOCK = 256


def add_half(sc, g, sib, name):
    l, r, cols = g.shape

    def body(sc_ref, g_ref, s_ref, o_ref):
        o_ref[...] = (g_ref[...].astype(F32) + s_ref[...].astype(F32)).astype(BF)

    if _split_rows((r, cols)):
        rh, br = _half_rows(r)
        nbk = rh // br
        blk = pl.BlockSpec((1, br, cols), lambda j, i, sc_: (j, i, 0))
        mine = pl.BlockSpec((1, br, cols), lambda j, i, sc_: (j, sc_[0] * nbk + i, 0))
        return _pc_sp(body, (l, nbk), [mine, blk], blk, SDS((l, rh, cols), BF), name)(sc, g, sib)
    nbk = cols // 2 // COL_BLOCK
    blk = pl.BlockSpec((1, r, COL_BLOCK), lambda j, i, sc_: (j, 0, i))
    mine = pl.BlockSpec((1, r, COL_BLOCK), lambda j, i, sc_: (j, 0, sc_[0] * nbk + i))
    return _pc_sp(body, (l, nbk), [mine, blk], blk, SDS((l, r, cols // 2), BF), name)(sc, g, sib)


def exchange_side(ps):
    n_ = len(ps)

    def width(p_):
        return p_.shape[2] if p_.shape[0] == 4 else p_.shape[2] // 4

    def plan(p_refs, got_refs, ssem, rsem):
        x, y, c = _pos()
        cps = []
        for w in range(n_):
            wd = width(ps[w])
            for j, (cx, cy) in enumerate(_other_chips(x, y)):
                to = 2 * cx + cy
                src = p_refs[w].at[to] if ps[w].shape[0] == 4 else p_refs[w].at[0, :, pl.ds(pl.multiple_of(to * wd, LANE), wd)]
                cps.append(pltpu.make_async_remote_copy(
                    src_ref=src, dst_ref=got_refs[w].at[j], send_sem=ssem.at[3 * w + j], recv_sem=rsem.at[3 * w + j],
                    device_id=(cx, cy, c), device_id_type=MESH))
        return cps

    def start(*refs):
        for cp in plan(*refs):
            cp.start()

    def finish(*refs):
        for cp in plan(*refs):
            cp.wait()

    return _Side(ps, [SDS((3, p_.shape[1], width(p_)), p_.dtype) for p_ in ps], 3 * n_, start, finish)


def sum4(sc, p, got, name):
    _, rh, wd = got.shape

    def body(sc_ref, p_ref, g_ref, r_ref):
        r_ref[...] = ((p_ref[0].astype(F32) + g_ref[0].astype(F32)) + (g_ref[1].astype(F32) + g_ref[2].astype(F32)))

    if rh % 16:
        assert p.shape[0] == 4
        return _pc_sp(body, (wd // COL_BLOCK,),
                      [pl.BlockSpec((1, rh, COL_BLOCK), lambda i, sc_: (sc_[1], 0, i)),
                       pl.BlockSpec((3, rh, COL_BLOCK), lambda i, sc_: (0, 0, i))],
                      pl.BlockSpec((rh, COL_BLOCK), lambda i, sc_: (0, i)), SDS((rh, wd), F32), name)(sc, p, got)
    _, br = _half_rows(2 * rh)
    own = (pl.BlockSpec((1, br, wd), lambda i, sc_: (sc_[1], i, 0)) if p.shape[0] == 4
           else pl.BlockSpec((1, br, wd), lambda i, sc_: (0, i, sc_[1])))
    return _pc_sp(body, (rh // br,), [own, pl.BlockSpec((3, br, wd), lambda i, sc_: (0, i, 0))],
                  pl.BlockSpec((br, wd), lambda i, sc_: (i, 0)), SDS((rh, wd), F32), name)(sc, p, got)


def join_halves(halves):
    n = len(halves)

    def body(*refs):
        h_refs, got_refs = refs[:n], refs[n:2 * n]
        ssem, rsem = refs[2 * n:]
        x, y, c = _pos()
        cps = []
        for w in range(n):
            cp = pltpu.make_async_remote_copy(src_ref=h_refs[w], dst_ref=got_refs[w], send_sem=ssem.at[w],
                                              recv_sem=rsem.at[w], device_id=(x, y, 1 - c), device_id_type=MESH)
            cp.start()
            cps.append(cp)
        for cp in cps:
            cp.wait()

    return _pc(body, name="join_halves", in_specs=[_ANY] * n, out_specs=[_ANY] * n,
               out_shape=[SDS(h.shape, h.dtype) for h in halves],
               scratch_shapes=[pltpu.SemaphoreType.DMA((n,)), pltpu.SemaphoreType.DMA((n,))])(*halves)


def allreduce_small(g):
    rows = g.shape[0]
    rh = rows // 2

    def body(g_ref, out_ref, sib_buf, chip_buf, sum_sc, ssem, rsem):
        x, y, c = _pos()
        me = 2 * x + y
        sibling = (x, y, 1 - c)
        mine = pl.ds(pl.multiple_of(c * rh, 8), rh)

        def copy(k, src, dst, to):
            return pltpu.make_async_remote_copy(src_ref=src, dst_ref=dst, send_sem=ssem.at[k], recv_sem=rsem.at[k],
                                                device_id=to, device_id_type=MESH)

        cp = copy(0, g_ref, sib_buf, sibling)
        cp.start()
        cp.wait()
        sum_sc[...] = g_ref[...] + sib_buf[...]
        chips = _other_chips(x, y)
        cps = [copy(1 + j, sum_sc.at[mine], chip_buf.at[me], (cx, cy, c)) for j, (cx, cy) in enumerate(chips)]
        for cp in cps:
            cp.start()
        chip_buf[me] = sum_sc[mine, :]
        for j, (cx, cy) in enumerate(chips):
            copy(1 + j, sum_sc.at[mine], chip_buf.at[2 * cx + cy], (cx, cy, c)).wait_recv()
        for cp in cps:
            cp.wait_send()
        out_ref[mine, :] = (chip_buf[0] + chip_buf[1]) + (chip_buf[2] + chip_buf[3])
        cp = copy(4, out_ref.at[mine], out_ref.at[mine], sibling)
        cp.start()
        cp.wait()

    vm = pl.BlockSpec(memory_space=pltpu.VMEM)
    return _pc(body, name="allreduce_small", in_specs=[vm], out_specs=vm, out_shape=SDS((rows, LANE), F32),
               scratch_shapes=[pltpu.VMEM((rows, LANE), F32), pltpu.VMEM((4, rh, LANE), F32), pltpu.VMEM((rows, LANE), F32),
                               pltpu.SemaphoreType.DMA((5,)), pltpu.SemaphoreType.DMA((5,))],
               compiler_params=pltpu.CompilerParams(vmem_limit_bytes=VMEM_LIMIT))(g)


def _pack_small(entries, get):
    flat = jnp.concatenate([get(n).reshape(-1).astype(F32) for n, _ in entries])
    rows = -(-flat.shape[0] // (8 * LANE)) * 8
    return jnp.pad(flat, (0, rows * LANE - flat.shape[0])).reshape(rows, LANE)


def _unpack_small(entries, packed):
    out, off = {}, 0
    flat = packed.reshape(-1)
    for name, n in entries:
        out[name] = flat[off:off + n]
        off += n
    return out


def _cols_full(blk):
    return blk.transpose(1, 0, 2).reshape(blk.shape[1], 4 * blk.shape[2])


def kernel(x, p, pre_mix_norm, w_in, a_ln_g, a_ln_b, a_spatial_w, a_spatial_b, a_out, b_gk, b_gk_bias, b_out_norm, b_out, w_mix_out, post_mix_norm, pre_ffn_norm, w_up, conv_w, conv_b, w_down, post_ffn_norm, w_ple, w_ple_gate, post_ple_norm, loss_target, m_pre_mix_norm, m_w_in, m_a_ln_g, m_a_ln_b, m_a_spatial_w, m_a_spatial_b, m_a_out, m_b_gk, m_b_gk_bias, m_b_out_norm, m_b_out, m_w_mix_out, m_post_mix_norm, m_pre_ffn_norm, m_w_up, m_conv_w, m_conv_b, m_w_down, m_post_ffn_norm, m_w_ple, m_w_ple_gate, m_post_ple_norm, v_pre_mix_norm, v_w_in, v_a_ln_g, v_a_ln_b, v_a_spatial_w, v_a_spatial_b, v_a_out, v_b_gk, v_b_gk_bias, v_b_out_norm, v_b_out, v_w_mix_out, v_post_mix_norm, v_pre_ffn_norm, v_w_up, v_conv_w, v_conv_b, v_w_down, v_post_ffn_norm, v_w_ple, v_w_ple_gate, v_post_ple_norm):
    args = dict(locals())
    order = ['pre_mix_norm', 'w_in', 'a_ln_g', 'a_ln_b', 'a_spatial_w', 'a_spatial_b', 'a_out', 'b_gk', 'b_gk_bias',
             'b_out_norm', 'b_out', 'w_mix_out', 'post_mix_norm', 'pre_ffn_norm', 'w_up', 'conv_w', 'conv_b', 'w_down',
             'post_ffn_norm', 'w_ple', 'w_ple_gate', 'post_ple_norm']
    assert sorted(BIG + TINY + tuple(n for n, _ in SMALL)) == sorted(order)
    s = x.shape[1]
    xs = x.reshape(s, D)
    ps = p.reshape(s, PLE)
    tgt = loss_target.reshape(s, D)
    t_big = min(1024, s)
    t_mid = min(512, s)
    t_small = min(256, s)
    t_gla = min(256, s)
    mx_, my_, mc_ = _pos()
    me = 2 * mx_ + my_
    sc = jnp.stack([mc_, me]).astype(jnp.int32)
    shard = lambda n: args[n].reshape(args[n].shape[1:])

    mine = {n: shard(n).astype(BF) for n in BIG}
    mine["w_in"] = shard("w_in").T.astype(BF)
    mine.update({n: shard(n) for n in TINY})
    fill = lambda names, gots: {n: lax.dynamic_update_slice(got, mine[n][None], (me, 0, 0)) for n, got in zip(names, gots)}
    first = ("w_in",) + TINY
    full = fill(first, _run_side(gather_side([mine["w_in"]], [mine[n] for n in TINY]), "gather_first"))
    wi = full["w_in"].reshape(4 * 1540, D)
    seg = lambda a, b: wi[a:b]
    qk = [seg(1024 + h * B_HK, 1024 + (h + 1) * B_HK) for h in range(B_H)]
    kk = [seg(1536 + h * B_HK, 1536 + (h + 1) * B_HK) for h in range(B_H)]
    w_z = jnp.concatenate([seg(0, 1024)] + [m_ for h in range(B_H) for m_ in (qk[h], kk[h])]
                          + [seg(2048, 4096), seg(4112, 6160), seg(4096, 4112), jnp.zeros((LANE - B_RANK, D), BF)], axis=0)
    wgk = jnp.pad(_cols_full(full["b_gk"]).astype(BF), ((0, LANE - B_RANK), (0, 0)))
    w_conv = _cols_full(full["conv_w"])
    g1, g2, g3 = pre_mix_norm.reshape(1, D), post_mix_norm.reshape(1, D), pre_ffn_norm.reshape(1, D)
    g4, g5 = post_ffn_norm.reshape(1, D), post_ple_norm.reshape(1, D)
    ln_g, ln_b = a_ln_g.reshape(1, A_W), a_ln_b.reshape(1, A_W)
    w_s = a_spatial_w.reshape(A_G, A_C, A_C)
    w_cat = w_s.transpose(1, 0, 2).reshape(A_C, A_G * A_C)
    w_cat_t = w_s.transpose(2, 0, 1).reshape(A_C, A_G * A_C)
    bias_full = jnp.repeat(a_spatial_b.reshape(A_G, A_C).T, A_GD, axis=1)
    bdm = (jnp.arange(A_G * A_C)[:, None] // A_C == jnp.arange(A_W)[None, :] // A_GD).astype(BF)
    gk_bias = b_gk_bias.reshape(1, B_H * B_HK)
    wn = b_out_norm.reshape(1, B_HV)
    cb = conv_b.reshape(1, 2 * D_FF)
    idx = jnp.arange(t_gla)
    ltri = ((idx[:, None] // B_C == idx[None, :] // B_C) & (idx[None, :] <= idx[:, None])).astype(BF)

    a, z, qk32, zl, *gots = norm_matmul(xs, g1, w_z, D, t_big, "in_proj", nblk=6, f32_blk=1, tail_blk=48,
                                        side=gather_side([mine[n] for n in BIG[1:]], []))
    full.update(fill(BIG[1:], gots))
    w_aout, w_ple_f = _cols_full(full["a_out"]), _cols_full(full["w_ple"])
    w_bout, w_mix, w_pg = (full[n].reshape(D, D) for n in ("b_out", "w_mix_out", "w_ple_gate"))
    w_dn, w_up3 = full["w_down"].reshape(D_FF, D), full["w_up"]
    sa = sgu_fwd(z, ln_g, ln_b, w_cat, bias_full, bdm, t_mid)
    ob, o, states = gla_fwd(z, qk32, zl, wgk, gk_bias, wn, ltri, t_gla)
    ya, yb, mp, mx, h1 = mix_fwd(sa, ob, z, xs, w_aout, w_bout, w_mix, g2, t_mid)
    c, up_g, up_v, cg, cv, ff = ffn_up_fwd(h1, g3, w_up3, w_conv, cb, t_mid)
    f, h2, pg, pe, dy, loss = out_fwd(ff, h1, ps, tgt, w_dn, w_pg, w_ple_f, g4, g5, t_mid)

    dh2, dpe, dpg, df, dff, gg5, gg4 = out_bwd(dy, pg, pe, f, g5, g4, w_pg, w_dn, t_mid)
    dup_g, dup_v, gcw, gcb, dh1, gg3 = ffn_up_bwd(up_g, up_v, cg, cv, dff, w_conv, _cols_full(w_up3), h1, g3, dh2,
                                                  t_small)
    dmx, dya, dyb, dga, dgb, dsa, dob, gg2 = mix_bwd(dh1, mx, z, ya, yb, g2, w_mix, w_aout, w_bout, t_mid)
    duv, g_lng, g_lnb, g_wcat, g_bst = sgu_bwd(z, dsa, ln_g, ln_b, w_cat, w_cat_t, bias_full, bdm, t_mid)
    g_ws = g_wcat.reshape(A_C, A_G, A_C).transpose(1, 0, 2)

    grads = {
        "a_out": mm_tn(sa, dya, "dw_a_out")[None],
        "b_out": mm_tn(ob, dyb, "dw_b_out").reshape(4, D // 4, D),
        "w_mix_out": mm_tn(mp, dmx, "dw_mix").reshape(4, D // 4, D),
        "w_up": jnp.concatenate([mm_tn(c, dup_g, "dw_up_g"), mm_tn(c, dup_v, "dw_up_v")], axis=1)[None],
        "w_down": mm_tn(ff, df, "dw_down").reshape(4, D_FF // 4, D),
        "w_ple": mm_tn(ps, dpe, "dw_ple")[None],
        "w_ple_gate": mm_tn(h2, dpg, "dw_ple_gate").reshape(4, D // 4, D),
    }

    def chip_partials(names):
        gs = [grads[n] for n in names]
        return [add_half(sc, g, sib, "partial_" + n)
                for n, g, sib in zip(names, gs, swap_halves(gs, "swap_halves_" + names[0]))]

    parts = dict(zip(BIG[1:], chip_partials(BIG[1:])))
    dqk, dvb, dog, dpre, dlr, g_gkb, g_wn, *gots = gla_bwd(
        z, qk32, zl, o, dob, states, wgk, gk_bias, wn, ltri, ltri.T, t_gla,
        side=exchange_side([parts[n] for n in BIG[1:]]))
    got = dict(zip(BIG[1:], gots))
    segs = [duv, dqk, dvb, dog, dga, dgb, dlr]

    gz = [mm_tn(sg_, a, "dw_in_%d" % k) for k, sg_ in enumerate(segs)]
    gq = [gz[1][h * 256:h * 256 + B_HK] for h in range(B_H)]
    gk = [gz[1][h * 256 + B_HK:(h + 1) * 256] for h in range(B_H)]
    g_in = jnp.concatenate([gz[0]] + gq + gk + [gz[2], gz[3], gz[6][:B_RANK], gz[4], gz[5]], axis=0)
    grads["w_in"] = g_in.reshape(4, 1540, D)
    parts["w_in"], = chip_partials(("w_in",))
    dx, gg1, got["w_in"] = nt_normbwd(segs, w_z, xs, g1, dh1, t_small, "in_bwd",
                                      side=exchange_side([parts["w_in"]]))

    reds = [sum4(sc, parts[n], got[n], "sum_" + n) for n in BIG]
    outs = {}
    for n, red, sib in zip(BIG, reds, join_halves(reds)):
        if n == "w_in":
            res = adamw_cols(sc, red, sib, shard(n).T, shard("m_" + n).T, shard("v_" + n).T, "adamw_" + n)
            res = [r_.T for r_ in res]
        else:
            res = adamw_halves(sc, red, sib, shard(n), shard("m_" + n), shard("v_" + n), "adamw_" + n)
        outs[n] = [r_.reshape(args[n].shape) for r_ in res]

    small_g = {
        "pre_mix_norm": gg1, "a_ln_g": g_lng, "a_ln_b": g_lnb, "a_spatial_w": g_ws, "a_spatial_b": g_bst.T,
        "b_gk_bias": g_gkb, "b_out_norm": g_wn, "post_mix_norm": gg2, "pre_ffn_norm": gg3,
        "conv_b": gcb, "post_ffn_norm": gg4, "post_ple_norm": gg5,
        "b_gk": mm_tn(zl, dpre, "dw_gk")[:B_RANK], "conv_w": gcw,
        "loss": loss,
    }
    red_entries = SMALL + (("b_gk", B_RANK * 512), ("conv_w", 3 * 2 * D_FF), ("loss", 1))
    g_fin = _unpack_small(red_entries, allreduce_small(_pack_small(red_entries, lambda n: small_g[n])))
    g_fin["b_gk"] = lax.dynamic_slice(g_fin["b_gk"].reshape(B_RANK, 512), (0, me * B_HK), (B_RANK, B_HK))
    g_fin["conv_w"] = lax.dynamic_slice(g_fin["conv_w"].reshape(3, 2 * D_FF), (0, me * 1408), (3, 1408))
    upd_entries = SMALL + (("b_gk", B_RANK * B_HK), ("conv_w", 3 * 1408))
    res = adamw_small(*[_pack_small(upd_entries, get) for get in
                        (lambda n: g_fin[n], lambda n: args[n], lambda n: args["m_" + n], lambda n: args["v_" + n])])
    res = [_unpack_small(upd_entries, r_) for r_ in res]
    for n, _ in upd_entries:
        outs[n] = [r_[n].reshape(args[n].shape) for r_ in [g_fin] + res]

    return (g_fin["loss"].reshape(()), dx.reshape(x.shape), *[outs[n][0] for n in order], *[outs[n][1] for n in order],
            *[outs[n][2] for n in order], *[outs[n][3] for n in order])
```

```python
import functools
import math

import jax
import jax.numpy as jnp
from jax import lax
from jax.experimental import pallas as pl
from jax.experimental.pallas import tpu as pltpu

F32 = jnp.float32
BF = jnp.bfloat16
SDS = jax.ShapeDtypeStruct
MESH = pl.DeviceIdType.MESH

EPS = 1e-6
D = 1024
A_W = 512
A_G, A_C = 8, 128
A_GD = A_W // A_G
B_H, B_HK, B_HV = 4, 128, 256
B_C = 64
GLA_HPB = 4
B_RANK = 16
D_FF = 2816
PLE = 256
LANE = 128
VMEM_LIMIT = 60 * 1024 * 1024

ADAM_LR, ADAM_B1, ADAM_B2, ADAM_EPS, ADAM_WD, ADAM_STEP = 0.001, 0.9, 0.999, 1e-08, 0.01, 10

_GC = math.sqrt(2.0 / math.pi)
_GA = 0.044715

BIG = ("w_in", "a_out", "b_out", "w_mix_out", "w_up", "w_down", "w_ple", "w_ple_gate")
TINY = ("b_gk", "conv_w")
SMALL = (("pre_mix_norm", 1024), ("a_ln_g", 512), ("a_ln_b", 512), ("a_spatial_w", 131072),
         ("a_spatial_b", 1024), ("b_gk_bias", 512), ("b_out_norm", 256), ("post_mix_norm", 1024),
         ("pre_ffn_norm", 1024), ("conv_b", 5632), ("post_ffn_norm", 1024), ("post_ple_norm", 1024))


def _pc(body, **kw):
    return pl.pallas_call(body, **kw)


def _cp(n):
    return pltpu.CompilerParams(dimension_semantics=("arbitrary",) * n, vmem_limit_bytes=VMEM_LIMIT)


def _const(shape):
    nd = len(shape)
    return pl.BlockSpec(shape, lambda *_: (0,) * nd, pipeline_mode=pl.Buffered(1))


def _acc(shape):
    nd = len(shape)
    return pl.BlockSpec(shape, lambda *_: (0,) * nd)


def _dot(a, b):
    return jnp.dot(a, b, preferred_element_type=F32)


def _dot_nt(a, b):
    return lax.dot_general(a, b, (((1,), (1,)), ((), ())), preferred_element_type=F32)


def _dot_tn(a, b):
    return lax.dot_general(a, b, (((0,), (0,)), ((), ())), preferred_element_type=F32)


def _gelu(x):
    return 0.5 * x * (1.0 + jnp.tanh(_GC * (x + _GA * x * x * x)))


def _gelu_and_grad(x):
    x2 = x * x
    s = 0.5 * jnp.tanh((_GC * x) * (1.0 + _GA * x2)) + 0.5
    g = x * s
    return g, s + g * (1.0 - s) * ((6.0 * _GC * _GA) * x2 + 2.0 * _GC)


def _log_sigmoid(x):
    return jnp.minimum(x, 0.0) - jnp.log(1.0 + jnp.exp(-jnp.abs(x)))


def _rms(x, g):
    return x * lax.rsqrt(jnp.mean(x * x, axis=-1, keepdims=True) + EPS) * g


def _rms_bwd(dy, x, g):
    r = lax.rsqrt(jnp.mean(x * x, axis=-1, keepdims=True) + EPS)
    n = x * r
    dn = dy * g
    dx = r * (dn - n * jnp.mean(dn * n, axis=-1, keepdims=True))
    return dx, jnp.sum(dy * n, axis=0, keepdims=True)


def _ldot3(l, x):
    h = x.astype(BF)
    r = x - h.astype(F32)
    m = r.astype(BF)
    lo = (r - m.astype(F32)).astype(BF)
    return _dot(l, h) + _dot(l, m) + _dot(l, lo)


def _split_side(refs, n_in, n_out, n_scratch, side):
    si, so = (side.n_in, side.n_out) if side else (0, 0)
    cuts = [n_in, si, n_out, so, n_scratch]
    out, at = [], 0
    for c in cuts:
        out.append(refs[at:at + c])
        at += c
    return (*out, refs[at:])


def _side_specs(side):
    return ([_ANY] * side.n_in, [_ANY] * side.n_out, side.out_shapes, side.scratch, side.ins) if side else ([],) * 5


def norm_matmul(x, g, wt, bn, t, name, nblk, f32_blk, tail_blk, side=None):
    s, dm = x.shape
    w_spec = pl.BlockSpec((bn, dm), lambda i, j: (j, 0))
    nt = s // t

    def body(*refs):
        (x_ref, g_ref, w_ref, wl_ref), s_in, outs, s_out, (a_sc,), s_scr = _split_side(refs, 4, 4, 1, side)
        a_ref, z_ref, f32_ref, tail_ref = outs
        i, j = pl.program_id(0), pl.program_id(1)
        if side:
            @pl.when((i == 0) & (j == 0))
            def _():
                side.start(s_in, s_out, *s_scr)

        @pl.when(j == 0)
        def _():
            a = _rms(x_ref[...], g_ref[...]).astype(BF)
            a_sc[...] = a
            a_ref[...] = a
            tail_ref[...] = _dot_nt(a, wl_ref[...]).astype(BF)

        acc = _dot_nt(a_sc[...], w_ref[...])
        z_ref[...] = acc.astype(BF)

        @pl.when(j == f32_blk)
        def _():
            f32_ref[...] = acc
        if side:
            @pl.when((i == nt - 1) & (j == nblk - 1))
            def _():
                side.finish(s_in, s_out, *s_scr)

    si_specs, so_specs, so_shapes, s_scratch, s_ins = _side_specs(side)
    return _pc(
        body, name=name, grid=(nt, nblk),
        in_specs=[pl.BlockSpec((t, dm), lambda i, j: (i, 0)), _const((1, dm)), w_spec,
                  pl.BlockSpec((LANE, dm), lambda i, j: (tail_blk, 0), pipeline_mode=pl.Buffered(1))] + si_specs,
        out_specs=[pl.BlockSpec((t, dm), lambda i, j: (i, 0)), pl.BlockSpec((t, bn), lambda i, j: (i, j)),
                   pl.BlockSpec((t, bn), lambda i, j: (i, 0)), pl.BlockSpec((t, LANE), lambda i, j: (i, 0))] + so_specs,
        out_shape=[SDS((s, dm), BF), SDS((s, nblk * bn), BF), SDS((s, bn), F32), SDS((s, LANE), BF)] + so_shapes,
        scratch_shapes=[pltpu.VMEM((t, dm), BF)] + s_scratch, compiler_params=_cp(2))(x, g, wt, wt, *s_ins)


def _sgu_weights(wc_ref, transposed):
    r = lax.broadcasted_iota(jnp.int32, (A_C, A_G * A_C), 0)
    c = lax.broadcasted_iota(jnp.int32, (A_C, A_G * A_C), 1) & (A_C - 1)
    return jnp.where((r <= c) if transposed else (c <= r), wc_ref[...], 0.0).astype(BF)


def _sgu_spread(xs, bdm):
    return jnp.concatenate([jnp.tile(x, (A_G, 1)) * bdm for x in xs], axis=1)


def _sgu_recompute(v, lng, lnb):
    gv, dgv = _gelu_and_grad(v)
    gv = gv.astype(F32)
    mu = jnp.mean(gv, axis=-1, keepdims=True)
    xc = gv - mu
    rstd = lax.rsqrt(jnp.mean(xc * xc, axis=-1, keepdims=True) + EPS)
    xhat = xc * rstd
    return dgv, rstd, xhat, (xhat * lng + lnb).astype(BF)


def sgu_fwd(z, ln_g, ln_b, w_cat, bias_full, bdm, t):
    s = z.shape[0]
    nch = t // A_C

    def body(u_ref, v_ref, g_ref, b_ref, wc_ref, bias_ref, bdm_ref, sa_ref):
        vns = [_sgu_recompute(v_ref[pl.ds(ci * A_C, A_C), :], g_ref[...], b_ref[...])[3]
               for ci in range(nch)]
        mixed = _dot(_sgu_weights(wc_ref, False), _sgu_spread(vns, bdm_ref[...]))
        for ci in range(nch):
            rows = pl.ds(ci * A_C, A_C)
            s_ = mixed[:, ci * A_W:(ci + 1) * A_W] + bias_ref[...]
            sa_ref[rows, :] = _gelu(u_ref[rows, :]) * s_.astype(BF)

    return _pc(
        body, name="sgu_fwd", grid=(s // t,),
        in_specs=[pl.BlockSpec((t, A_W), lambda i: (i, 0)), pl.BlockSpec((t, A_W), lambda i: (i, 1)),
                  _const((1, A_W)), _const((1, A_W)), _const((A_C, A_G * A_C)), _const((A_C, A_W)),
                  _const((A_G * A_C, A_W))],
        out_specs=pl.BlockSpec((t, A_W), lambda i: (i, 0)),
        out_shape=SDS((s, A_W), BF), compiler_params=_cp(1))(z, z, ln_g, ln_b, w_cat, bias_full, bdm)


def _gla_decays(qk, lr, wgk, bias, l, t):
    nc = t // B_C
    q = qk[:, :B_HK].astype(F32) * (B_HK ** -0.5)
    k = qk[:, B_HK:].astype(F32)
    pre = _dot(lr, wgk) + bias
    la = _log_sigmoid(pre) * (1.0 / 16.0)
    b = _ldot3(l, la)
    b3 = b.reshape(nc, B_C, B_HK)
    bl = jnp.broadcast_to(b3[:, B_C - 1:B_C, :], (nc, B_C, B_HK)).reshape(t, B_HK)
    eb, enb, etb = jnp.exp(b), jnp.exp(-b), jnp.exp(bl - b)
    return pre, b, bl, eb, enb, etb, q * eb, k * enb, k * etb


def gla_fwd(z, qk32, zl, wgk, bias, wn, ltri, t):
    s = z.shape[0]
    nc = t // B_C
    hpb = GLA_HPB
    assert hpb == B_H
    kw, vw = hpb * B_HK, hpb * B_HV

    def body(qk_ref, v_ref, og_ref, lr_ref, wgk_ref, bias_ref, wn_ref, l_ref, ob_ref, o_ref, st_ref, st_sc, o_sc):
        @pl.when(pl.program_id(0) == 0)
        def _():
            st_sc[...] = jnp.zeros((B_H, B_HV, B_HK), F32)

        lr, l = lr_ref[...], l_ref[...]
        for hh in range(hpb):
            h = hh
            cv, ck = slice(hh * B_HV, (hh + 1) * B_HV), slice(hh * B_HK, (hh + 1) * B_HK)
            _, _, bl, _, _, _, qd, ki, kt = _gla_decays(qk_ref[:, cv], lr, wgk_ref[:, ck], bias_ref[:, ck], l, t)
            qd, ki, kt = qd.astype(BF), ki.astype(BF), kt.astype(BF)
            vb = v_ref[:, cv]
            sc = jnp.where(l > 0, _dot_nt(qd, ki), 0.0).astype(BF)
            o_sc[hh] = _dot(sc, vb)
            for n in range(nc):
                rows = slice(n * B_C, (n + 1) * B_C)
                st = st_sc[h]
                stb = st.astype(BF)
                st_ref[n, hh] = stb
                o_sc[hh, rows, :] += _dot_nt(qd[rows], stb)
                st_sc[h] = st * jnp.exp(bl[n * B_C:n * B_C + 1, :]) + _dot_tn(vb[rows], kt[rows])
            ob = o_sc[hh].astype(BF)
            o_ref[:, cv] = ob
            og = og_ref[:, cv].astype(F32)
            ob_ref[:, cv] = (_rms(ob.astype(F32), wn_ref[...]) * og * jax.nn.sigmoid(og)).astype(BF)

    vo, go = 2048 // vw, 3072 // vw
    return _pc(
        body, name="gla_fwd", grid=(s // t, B_H // hpb),
        in_specs=[pl.BlockSpec((t, vw), lambda i, g: (i, g)), pl.BlockSpec((t, vw), lambda i, g: (i, vo + g)),
                  pl.BlockSpec((t, vw), lambda i, g: (i, go + g)), pl.BlockSpec((t, LANE), lambda i, g: (i, 0)),
                  pl.BlockSpec((LANE, kw), lambda i, g: (0, g)), pl.BlockSpec((1, kw), lambda i, g: (0, g)),
                  _const((1, B_HV)), _const((t, t))],
        out_specs=[pl.BlockSpec((t, vw), lambda i, g: (i, g)), pl.BlockSpec((t, vw), lambda i, g: (i, g)),
                   pl.BlockSpec((nc, hpb, B_HV, B_HK), lambda i, g: (i, g, 0, 0))],
        out_shape=[SDS((s, D), BF), SDS((s, D), BF), SDS((s // B_C, B_H, B_HV, B_HK), BF)],
        scratch_shapes=[pltpu.VMEM((B_H, B_HV, B_HK), F32), pltpu.VMEM((hpb, t, B_HV), F32)],
        compiler_params=_cp(2))(qk32, z, z, zl, wgk, bias, wn, ltri)


def mix_fwd(sa, ob, z, x, a_out, b_out, w_mix, g2, t):
    s = x.shape[0]

    def body(sa_ref, ob_ref, ga_ref, gb_ref, x_ref, ao_ref, bo_ref, wm_ref, g2_ref,
             ya_ref, yb_ref, mp_ref, mx_ref, h1_ref):
        ya = _dot(sa_ref[...], ao_ref[...]).astype(BF)
        yb = _dot(ob_ref[...], bo_ref[...]).astype(BF)
        ya_ref[...] = ya
        yb_ref[...] = yb
        mp = (jax.nn.sigmoid(ga_ref[...].astype(F32)) * ya.astype(F32)
              + jax.nn.sigmoid(gb_ref[...].astype(F32)) * yb.astype(F32)).astype(BF)
        mp_ref[...] = mp
        mx = _dot(mp, wm_ref[...]).astype(BF)
        mx_ref[...] = mx
        h1_ref[...] = x_ref[...] + _rms(mx.astype(F32), g2_ref[...])

    row = lambda w: pl.BlockSpec((t, w), lambda i: (i, 0))
    return _pc(
        body, name="mix_fwd", grid=(s // t,),
        in_specs=[row(A_W), row(D), pl.BlockSpec((t, D), lambda i: (i, 4)), pl.BlockSpec((t, D), lambda i: (i, 5)),
                  row(D), _const((A_W, D)), _const((D, D)), _const((D, D)), _const((1, D))],
        out_specs=[row(D)] * 5,
        out_shape=[SDS((s, D), BF)] * 4 + [SDS((s, D), F32)],
        compiler_params=_cp(1))(sa, ob, z, z, x, a_out, b_out, w_mix, g2)


def ffn_up_fwd(h1, g3, w_up3, conv_w, conv_b, t):
    s = h1.shape[0]
    bn = w_up3.shape[2]

    def body(x_ref, g_ref, wg_ref, wv_ref, cwg_ref, cwv_ref, cbg_ref, cbv_ref,
             c_ref, ug_ref, uv_ref, cg_ref, cv_ref, ff_ref, c_sc, carry):
        i, j = pl.program_id(0), pl.program_id(1)

        @pl.when(j == 0)
        def _():
            c = _rms(x_ref[...], g_ref[...]).astype(BF)
            c_sc[...] = c
            c_ref[...] = c

        @pl.when(i == 0)
        def _():
            carry[j] = jnp.zeros((2, 8, bn), F32)

        def branch(k, w_ref, cw_ref, cb_ref, u_ref, o_ref):
            ub = _dot(c_sc[...], w_ref[...]).astype(BF)
            u_ref[...] = ub
            u = ub.astype(F32)
            ext = jnp.concatenate([carry[j, k], u], axis=0)
            carry[j, k] = u[t - 8:]
            w = cw_ref[...]
            cc = (cb_ref[...] + w[0:1] * pltpu.roll(ext, 2, 0) + w[1:2] * pltpu.roll(ext, 1, 0) + w[2:3] * ext)[8:]
            cc = cc.astype(BF)
            o_ref[...] = cc
            return cc

        g = _gelu(branch(0, wg_ref, cwg_ref, cbg_ref, ug_ref, cg_ref))
        ff_ref[...] = g * branch(1, wv_ref, cwv_ref, cbv_ref, uv_ref, cv_ref)

    col = lambda rows, off: pl.BlockSpec((rows, bn), lambda i, j: (0, j + off))
    out = pl.BlockSpec((t, bn), lambda i, j: (i, j))
    return _pc(
        body, name="ffn_up_fwd", grid=(s // t, 2),
        in_specs=[pl.BlockSpec((t, D), lambda i, j: (i, 0)), _const((1, D)),
                  pl.BlockSpec((None, D, bn), lambda i, j: (j, 0, 0)), pl.BlockSpec((None, D, bn), lambda i, j: (j + 2, 0, 0)),
                  col(3, 0), col(3, 2), col(1, 0), col(1, 2)],
        out_specs=[pl.BlockSpec((t, D), lambda i, j: (i, 0))] + [out] * 5,
        out_shape=[SDS((s, D), BF)] + [SDS((s, D_FF), BF)] * 5,
        scratch_shapes=[pltpu.VMEM((t, D), BF), pltpu.VMEM((2, 2, 8, bn), F32)],
        compiler_params=_cp(2))(h1, g3, w_up3, w_up3, conv_w, conv_w, conv_b, conv_b)


def ffn_up_bwd(ug, uv, cg, cv, dff, conv_w, w_up, h1, g3, dh2, t):
    s = h1.shape[0]
    nt = s // t
    hb = t // 8
    bn = 1408
    r = t + 8

    def body(ug_ref, uv_ref, cg_ref, cv_ref, cag_ref, cav_ref, d_ref, da_ref, cw_ref, w_ref, x_ref, g_ref, dres_ref,
             dug_ref, duv_ref, gw_ref, gb_ref, dx_ref, gg_ref):
        i = pl.program_id(0)

        @pl.when(i == 0)
        def _():
            gw_ref[...] = jnp.zeros((3, 2 * D_FF), F32)
            gb_ref[...] = jnp.zeros((1, 2 * D_FF), F32)
            gg_ref[...] = jnp.zeros((1, D), F32)

        more = (i < nt - 1).astype(BF)

        def gate(c_g, c_v, d_):
            gl, dgl = _gelu_and_grad(c_g)
            return (d_ * c_v * dgl).astype(F32), (d_ * gl).astype(F32)

        def back(dc, dc_next, u_ref, cols, off, du_ref):
            w = cw_ref[:, off:off + bn]
            d_ext = jnp.concatenate([dc, dc_next], axis=0)
            d1, d2 = pltpu.roll(d_ext, r - 1, 0)[:t], pltpu.roll(d_ext, r - 2, 0)[:t]
            du_ref[:, cols] = (w[2:3] * dc + w[1:2] * d1 + w[0:1] * d2).astype(BF)
            u = u_ref[:, cols].astype(F32)
            gw_ref[0:1, off:off + bn] += jnp.sum(d2 * u, axis=0, keepdims=True)
            gw_ref[1:2, off:off + bn] += jnp.sum(d1 * u, axis=0, keepdims=True)
            gw_ref[2:3, off:off + bn] += jnp.sum(dc * u, axis=0, keepdims=True)
            gb_ref[:, off:off + bn] += jnp.sum(dc, axis=0, keepdims=True)

        for kb in range(D_FF // bn):
            cols = slice(kb * bn, (kb + 1) * bn)
            dg, dv = gate(cg_ref[:, cols], cv_ref[:, cols], d_ref[:, cols])
            dg_n, dv_n = gate(cag_ref[:, cols], cav_ref[:, cols], da_ref[:, cols] * more)
            back(dg, dg_n, ug_ref, cols, kb * bn, dug_ref)
            back(dv, dv_n, uv_ref, cols, D_FF + kb * bn, duv_ref)

        acc = _dot_nt(dug_ref[...], w_ref[:, :D_FF]) + _dot_nt(duv_ref[...], w_ref[:, D_FF:])
        dxn, dg3 = _rms_bwd(acc, x_ref[...], g_ref[...])
        dx_ref[...] = dres_ref[...] + dxn
        gg_ref[...] += dg3

    tile = lambda width: pl.BlockSpec((t, width), lambda i: (i, 0))
    after = pl.BlockSpec((8, D_FF), lambda i: (jnp.minimum((i + 1) * hb, nt * hb - 1), 0))
    return _pc(
        body, name="ffn_up_bwd", grid=(nt,),
        in_specs=[tile(D_FF)] * 4 + [after, after, tile(D_FF), after, _const((3, 2 * D_FF)), _const((D, 2 * D_FF)),
                                     tile(D), _const((1, D)), tile(D)],
        out_specs=[tile(D_FF), tile(D_FF), _acc((3, 2 * D_FF)), _acc((1, 2 * D_FF)), tile(D), _acc((1, D))],
        out_shape=[SDS((s, D_FF), BF), SDS((s, D_FF), BF), SDS((3, 2 * D_FF), F32), SDS((1, 2 * D_FF), F32),
                   SDS((s, D), F32), SDS((1, D), F32)],
        compiler_params=_cp(1))(ug, uv, cg, cv, cg, cv, dff, dff, conv_w, w_up, h1, g3, dh2)


def out_fwd(ff, h1, p, tgt, w_down, w_pg, w_ple, g4, g5, t):
    s = h1.shape[0]

    def body(ff_ref, h1_ref, p_ref, t_ref, wd_ref, wpg_ref, wpl_ref, g4_ref, g5_ref,
             f_ref, h2_ref, pg_ref, pe_ref, dy_ref, loss_ref):
        @pl.when(pl.program_id(0) == 0)
        def _():
            loss_ref[...] = jnp.zeros((1, 1), F32)

        f = _dot(ff_ref[...], wd_ref[...]).astype(BF)
        f_ref[...] = f
        h2 = h1_ref[...] + _rms(f.astype(F32), g4_ref[...])
        h2b = h2.astype(BF)
        h2_ref[...] = h2b
        pg = _dot(h2b, wpg_ref[...]).astype(BF)
        pe = _dot(p_ref[...].astype(BF), wpl_ref[...]).astype(BF)
        pg_ref[...] = pg
        pe_ref[...] = pe
        y = h2 + _rms(jax.nn.sigmoid(pg.astype(F32)) * pe.astype(F32), g5_ref[...])
        err = y - t_ref[...]
        dy_ref[...] = err * (1.0 / D)
        loss_ref[...] += (0.5 / D) * jnp.sum(err * err)

    row = lambda w: pl.BlockSpec((t, w), lambda i: (i, 0))
    return _pc(
        body, name="out_fwd", grid=(s // t,),
        in_specs=[row(D_FF), row(D), row(PLE), row(D), _const((D_FF, D)), _const((D, D)), _const((PLE, D)),
                  _const((1, D)), _const((1, D))],
        out_specs=[row(D)] * 5 + [_acc((1, 1))],
        out_shape=[SDS((s, D), BF)] * 4 + [SDS((s, D), F32), SDS((1, 1), F32)],
        compiler_params=_cp(1))(ff, h1, p, tgt, w_down, w_pg, w_ple, g4, g5)


def out_bwd(dy, pg, pe, f, g5, g4, w_pg, w_down, t):
    s = dy.shape[0]

    def body(dy_ref, pg_ref, pe_ref, f_ref, g5_ref, g4_ref, wpg_ref, wd_ref,
             dh2_ref, dpe_ref, dpg_ref, df_ref, dff_ref, gg5_ref, gg4_ref):
        @pl.when(pl.program_id(0) == 0)
        def _():
            gg5_ref[...] = jnp.zeros((1, D), F32)
            gg4_ref[...] = jnp.zeros((1, D), F32)

        dy_ = dy_ref[...]
        pg_ = pg_ref[...].astype(F32)
        pe_ = pe_ref[...].astype(F32)
        sg = jax.nn.sigmoid(pg_)
        dple, dg5 = _rms_bwd(dy_, sg * pe_, g5_ref[...])
        gg5_ref[...] += dg5
        dpe_ref[...] = (dple * sg).astype(BF)
        dpg = (dple * pe_ * sg * (1.0 - sg)).astype(BF)
        dpg_ref[...] = dpg
        dh2 = dy_ + _dot_nt(dpg, wpg_ref[...])
        dh2_ref[...] = dh2
        df, dg4 = _rms_bwd(dh2, f_ref[...].astype(F32), g4_ref[...])
        gg4_ref[...] += dg4
        dfb = df.astype(BF)
        df_ref[...] = dfb
        dff_ref[...] = _dot_nt(dfb, wd_ref[...]).astype(BF)

    row = lambda w: pl.BlockSpec((t, w), lambda i: (i, 0))
    return _pc(
        body, name="out_bwd", grid=(s // t,),
        in_specs=[row(D), row(D), row(D), row(D), _const((1, D)), _const((1, D)), _const((D, D)), _const((D_FF, D))],
        out_specs=[row(D), row(D), row(D), row(D), row(D_FF), _acc((1, D)), _acc((1, D))],
        out_shape=[SDS((s, D), F32), SDS((s, D), BF), SDS((s, D), BF), SDS((s, D), BF), SDS((s, D_FF), BF),
                   SDS((1, D), F32), SDS((1, D), F32)],
        compiler_params=_cp(1))(dy, pg, pe, f, g5, g4, w_pg, w_down)


def nt_normbwd(dys, w, xin, gain, dres, t, name, side=None):
    s = xin.shape[0]
    nt = s // t
    np_ = len(dys)

    def body(*refs):
        ins_, s_in, (dx_ref, gg_ref), s_out, _, s_scr = _split_side(refs, np_ + 4, 2, 0, side)
        dy_refs = ins_[:np_]
        w_ref, x_ref, g_ref, dres_ref = ins_[np_:]
        i = pl.program_id(0)

        @pl.when(i == 0)
        def _():
            gg_ref[...] = jnp.zeros((1, D), F32)
            if side:
                side.start(s_in, s_out, *s_scr)

        acc = _dot(jnp.concatenate([r_[...] for r_ in dy_refs], axis=1), w_ref[...])
        dxn, dg = _rms_bwd(acc, x_ref[...], g_ref[...])
        dx_ref[...] = dres_ref[...] + dxn
        gg_ref[...] += dg
        if side:
            @pl.when(i == nt - 1)
            def _():
                side.finish(s_in, s_out, *s_scr)

    row = lambda width: pl.BlockSpec((t, width), lambda i: (i, 0))
    si_specs, so_specs, so_shapes, s_scratch, s_ins = _side_specs(side)
    assert sum(dy.shape[1] for dy in dys) == w.shape[0]
    return _pc(
        body, name=name, grid=(nt,),
        in_specs=[row(dy.shape[1]) for dy in dys] + [_const(w.shape), row(D), _const((1, D)), row(D)] + si_specs,
        out_specs=[row(D), _acc((1, D))] + so_specs,
        out_shape=[SDS((s, D), F32), SDS((1, D), F32)] + so_shapes, scratch_shapes=s_scratch,
        compiler_params=_cp(1))(*dys, w, xin, gain, dres, *s_ins)


def mix_bwd(dh1, mx, z, ya, yb, g2, w_mix, a_out, b_out, t, side=None):
    s = dh1.shape[0]
    nt = s // t

    def body(*refs):
        ins_, s_in, outs_, s_out, _, s_scr = _split_side(refs, 10, 8, 0, side)
        dh_ref, mx_ref, ga_ref, gb_ref, ya_ref, yb_ref, g2_ref, wm_ref, ao_ref, bo_ref = ins_
        dmx_ref, dya_ref, dyb_ref, dga_ref, dgb_ref, dsa_ref, dob_ref, gg2_ref = outs_

        @pl.when(pl.program_id(0) == 0)
        def _():
            gg2_ref[...] = jnp.zeros((1, D), F32)
            if side:
                side.start(s_in, s_out, *s_scr)

        dmx, dg2 = _rms_bwd(dh_ref[...], mx_ref[...].astype(F32), g2_ref[...])
        gg2_ref[...] += dg2
        dmxb = dmx.astype(BF)
        dmx_ref[...] = dmxb
        dmp = _dot_nt(dmxb, wm_ref[...]).astype(BF)

        def gate(g_ref, y_ref, dy_ref, dg_ref, w_ref, dz_ref):
            sg = jax.nn.sigmoid(g_ref[...])
            dyb_ = dmp * sg
            dy_ref[...] = dyb_
            dg_ref[...] = dyb_ * y_ref[...] * (1.0 - sg)
            dz_ref[...] = _dot_nt(dyb_, w_ref[...]).astype(BF)

        gate(ga_ref, ya_ref, dya_ref, dga_ref, ao_ref, dsa_ref)
        gate(gb_ref, yb_ref, dyb_ref, dgb_ref, bo_ref, dob_ref)
        if side:
            @pl.when(pl.program_id(0) == nt - 1)
            def _():
                side.finish(s_in, s_out, *s_scr)

    row = lambda w: pl.BlockSpec((t, w), lambda i: (i, 0))
    si_specs, so_specs, so_shapes, s_scratch, s_ins = _side_specs(side)
    return _pc(
        body, name="mix_bwd", grid=(nt,),
        in_specs=[row(D), row(D), pl.BlockSpec((t, D), lambda i: (i, 4)), pl.BlockSpec((t, D), lambda i: (i, 5)),
                  row(D), row(D), _const((1, D)), _const((D, D)), _const((A_W, D)), _const((D, D))] + si_specs,
        out_specs=[row(D)] * 5 + [row(A_W), row(D), _acc((1, D))] + so_specs,
        out_shape=[SDS((s, D), BF)] * 5 + [SDS((s, A_W), BF), SDS((s, D), BF), SDS((1, D), F32)] + so_shapes,
        scratch_shapes=s_scratch,
        compiler_params=_cp(1))(dh1, mx, z, z, ya, yb, g2, w_mix, a_out, b_out, *s_ins)


def sgu_bwd(z, dsa, ln_g, ln_b, w_cat, w_cat_t, bias_full, bdm, t):
    s = z.shape[0]
    nt = s // t
    nch = t // A_C

    def body(u_ref, v_ref, dsa_ref, g_ref, b_ref, wc_ref, wct_ref, bias_ref, bdm_ref,
             duv_ref, glg_ref, glb_ref, gws_ref, gbs_ref, ds_acc):
        i = pl.program_id(0)

        @pl.when(i == 0)
        def _():
            glg_ref[...] = jnp.zeros((1, A_W), F32)
            glb_ref[...] = jnp.zeros((1, A_W), F32)
            gws_ref[...] = jnp.zeros((A_C, A_G * A_C), F32)
            ds_acc[...] = jnp.zeros((A_C, A_W), F32)

        lng, bdm_ = g_ref[...], bdm_ref[...]
        rec = [_sgu_recompute(v_ref[pl.ds(ci * A_C, A_C), :], lng, b_ref[...]) for ci in range(nch)]
        spread_vn = _sgu_spread([r_[3] for r_ in rec], bdm_)
        mixed = _dot(_sgu_weights(wc_ref, False), spread_vn)
        dsas, dss, dgus = [], [], []
        for ci in range(nch):
            rows = pl.ds(ci * A_C, A_C)
            gu, dgu = _gelu_and_grad(u_ref[rows, :])
            dsa_ = dsa_ref[rows, :]
            ds = dsa_ * gu
            ds_acc[...] += ds.astype(F32)
            dsas.append(dsa_)
            dgus.append(dgu)
            dss.append(ds)
        r = lax.broadcasted_iota(jnp.int32, (A_C, A_G * A_C), 0)
        c = lax.broadcasted_iota(jnp.int32, (A_C, A_G * A_C), 1) & (A_C - 1)
        gws_ref[...] += jnp.where(c <= r, _dot_nt(jnp.concatenate(dss, axis=1), spread_vn), 0.0)
        dvns = _dot(_sgu_weights(wct_ref, True), _sgu_spread(dss, bdm_))
        for ci in range(nch):
            rows = pl.ds(ci * A_C, A_C)
            dgv, rstd, xhat, _ = rec[ci]
            dvn = dvns[:, ci * A_W:(ci + 1) * A_W]
            glb_ref[...] += jnp.sum(dvn, axis=0, keepdims=True)
            glg_ref[...] += jnp.sum(dvn * xhat, axis=0, keepdims=True)
            dxh = dvn * lng
            dgv_ = rstd * (dxh - jnp.mean(dxh, axis=-1, keepdims=True)
                           - xhat * jnp.mean(dxh * xhat, axis=-1, keepdims=True))
            s_ = mixed[:, ci * A_W:(ci + 1) * A_W] + bias_ref[...]
            duv_ref[rows, :A_W] = dsas[ci] * dgus[ci] * s_.astype(BF)
            duv_ref[rows, A_W:] = (dgv_ * dgv).astype(BF)

        @pl.when(i == nt - 1)
        def _():
            acc = ds_acc[...]
            for g in range(A_G):
                gbs_ref[:, g:g + 1] = jnp.sum(acc[:, g * A_GD:(g + 1) * A_GD], axis=1, keepdims=True)

    return _pc(
        body, name="sgu_bwd", grid=(nt,),
        in_specs=[pl.BlockSpec((t, A_W), lambda i: (i, 0)), pl.BlockSpec((t, A_W), lambda i: (i, 1)),
                  pl.BlockSpec((t, A_W), lambda i: (i, 0)),
                  _const((1, A_W)), _const((1, A_W)), _const((A_C, A_G * A_C)), _const((A_C, A_G * A_C)),
                  _const((A_C, A_W)), _const((A_G * A_C, A_W))],
        out_specs=[pl.BlockSpec((t, D), lambda i: (i, 0)), _acc((1, A_W)), _acc((1, A_W)),
                   _acc((A_C, A_G * A_C)), _acc((A_C, A_G))],
        out_shape=[SDS((s, D), BF), SDS((1, A_W), F32), SDS((1, A_W), F32), SDS((A_C, A_G * A_C), F32),
                   SDS((A_C, A_G), F32)],
        scratch_shapes=[pltpu.VMEM((A_C, A_W), F32)],
        compiler_params=_cp(1))(z, z, dsa, ln_g, ln_b, w_cat, w_cat_t, bias_full, bdm)


def gla_bwd(z, qk32, zl, o, dob, states, wgk, bias, wn, ltri, ltri_t, t, side=None):
    s = z.shape[0]
    nt = s // t
    nc = t // B_C
    hpb = GLA_HPB
    assert hpb == B_H
    kw, vw = hpb * B_HK, hpb * B_HV

    def body(*refs):
        ins_, s_in, outs_, s_out, scr_, s_scr = _split_side(refs, 12, 7, 5, side)
        qk_ref, v_ref, og_ref, lr_ref, o_ref, dob_ref, st_ref, wgk_ref, bias_ref, wn_ref, l_ref, lt_ref = ins_
        dqk_ref, dv_ref, dog_ref, dpre_ref, dlr_ref, gbias_ref, gwn_ref = outs_
        dst_sc, dv_sc, dqd_sc, dkt_sc, ddec_sc = scr_
        i = pl.program_id(0)
        g = pl.program_id(1)

        @pl.when((i == 0) & (g == 0))
        def _():
            gbias_ref[...] = jnp.zeros((B_H, 1, B_HK), F32)
            gwn_ref[...] = jnp.zeros((1, B_HV), F32)
            if side:
                side.start(s_in, s_out, *s_scr)

        @pl.when(i == 0)
        def _():
            dst_sc[...] = jnp.zeros((B_H, B_HV, B_HK), F32)

        lr, l, lt = lr_ref[...], l_ref[...], lt_ref[...]
        keep, keep_t = l > 0, lt > 0
        wn_ = wn_ref[...]
        last = lax.broadcasted_iota(jnp.int32, (nc, B_C, B_HK), 1) == B_C - 1
        for hh in range(hpb):
            h = hh
            cv, ck = slice(hh * B_HV, (hh + 1) * B_HV), slice(hh * B_HK, (hh + 1) * B_HK)
            pre, b, bl, eb, enb, etb, qd, ki, kt = _gla_decays(qk_ref[:, cv], lr, wgk_ref[:, ck], bias_ref[:, ck], l, t)
            qdb, kib, ktb = qd.astype(BF), ki.astype(BF), kt.astype(BF)
            vb = v_ref[:, cv]
            o_ = o_ref[:, cv].astype(F32)
            og = og_ref[:, cv].astype(F32)
            sog = jax.nn.sigmoid(og)
            dob_ = dob_ref[:, cv].astype(F32)
            don = dob_ * og * sog
            do, dwn = _rms_bwd(don, o_, wn_)
            gwn_ref[...] += dwn
            dog_ref[:, cv] = (dob_ * _rms(o_, wn_) * sog * (1.0 + og * (1.0 - sog))).astype(BF)
            dob16 = do.astype(BF)
            sc_t = jnp.where(keep_t, _dot_nt(kib, qdb), 0.0).astype(BF)
            dsc = jnp.where(keep, _dot_nt(dob16, vb), 0.0).astype(BF)
            dsc_t = jnp.where(keep_t, _dot_nt(vb, dob16), 0.0).astype(BF)
            dv_sc[hh] = _dot(sc_t, dob16)
            dqd_sc[hh] = _dot(dsc, kib)
            dki = _dot(dsc_t, qdb)
            for n in reversed(range(nc)):
                rows = slice(n * B_C, (n + 1) * B_C)
                dst = dst_sc[h]
                dstb = dst.astype(BF)
                stp = st_ref[n, hh]
                dv_sc[hh, rows, :] += _dot_nt(ktb[rows], dstb)
                dkt_sc[hh, rows, :] = _dot(vb[rows], dstb)
                dqd_sc[hh, rows, :] += _dot(dob16[rows], stp)
                dec = jnp.exp(bl[n * B_C:n * B_C + 1, :])
                ddec_sc[hh, n] = jnp.sum(dst * stp.astype(F32), axis=0, keepdims=True) * dec
                dst_sc[h] = dst * dec + _dot_tn(dob16[rows], qdb[rows])
            dqd, dkt = dqd_sc[hh], dkt_sc[hh]
            dv_ref[:, cv] = dv_sc[hh].astype(BF)
            dqk_ref[:, hh * B_HV:hh * B_HV + B_HK] = (dqd * eb * (B_HK ** -0.5)).astype(BF)
            dqk_ref[:, hh * B_HV + B_HK:(hh + 1) * B_HV] = (dki * enb + dkt * etb).astype(BF)
            dktkt = dkt * kt
            db3 = (dqd * qd - dki * ki - dktkt).reshape(nc, B_C, B_HK)
            dbl = jnp.sum(dktkt.reshape(nc, B_C, B_HK), axis=1, keepdims=True) + ddec_sc[hh]
            db = (db3 + jnp.where(last, dbl, 0.0)).reshape(t, B_HK)
            dla = _ldot3(lt, db)
            dpre = dla * (1.0 / 16.0) * (1.0 - jax.nn.sigmoid(pre))
            dpreb = dpre.astype(BF)
            dpre_ref[:, ck] = dpreb
            gbias_ref[h] += jnp.sum(dpre, axis=0, keepdims=True)
            dlr_h = _dot_nt(dpreb, wgk_ref[:, ck])
            dlr = dlr_h if hh == 0 else dlr + dlr_h
        dlr_ref[...] = dlr.astype(BF)
        if side:
            @pl.when((i == nt - 1) & (g == B_H // hpb - 1))
            def _():
                side.finish(s_in, s_out, *s_scr)

    rv = lambda i: nt - 1 - i
    si_specs, so_specs, so_shapes, s_scratch, s_ins = _side_specs(side)
    vo, go = 2048 // vw, 3072 // vw
    tile = lambda off: pl.BlockSpec((t, vw), lambda i, g: (rv(i), off + g))
    return _pc(
        body, name="gla_bwd", grid=(nt, B_H // hpb),
        in_specs=[tile(0), tile(vo), tile(go), pl.BlockSpec((t, LANE), lambda i, g: (rv(i), 0)), tile(0), tile(0),
                  pl.BlockSpec((nc, hpb, B_HV, B_HK), lambda i, g: (rv(i), g, 0, 0)),
                  pl.BlockSpec((LANE, kw), lambda i, g: (0, g)), pl.BlockSpec((1, kw), lambda i, g: (0, g)),
                  _const((1, B_HV)), _const((t, t)), _const((t, t))] + si_specs,
        out_specs=[tile(0), tile(0), tile(0), pl.BlockSpec((t, kw), lambda i, g: (rv(i), g)),
                   pl.BlockSpec((t, LANE), lambda i, g: (rv(i), 0)), _acc((B_H, 1, B_HK)), _acc((1, B_HV))] + so_specs,
        out_shape=[SDS((s, D), BF), SDS((s, D), BF), SDS((s, D), BF), SDS((s, B_H * B_HK), BF), SDS((s, LANE), BF),
                   SDS((B_H, 1, B_HK), F32), SDS((1, B_HV), F32)] + so_shapes,
        scratch_shapes=[pltpu.VMEM((B_H, B_HV, B_HK), F32), pltpu.VMEM((hpb, t, B_HV), F32),
                        pltpu.VMEM((hpb, t, B_HK), F32), pltpu.VMEM((hpb, t, B_HK), F32),
                        pltpu.VMEM((hpb, nc, 1, B_HK), F32)] + s_scratch,
        compiler_params=_cp(2))(qk32, z, z, zl, o, dob, states, wgk, bias, wn, ltri, ltri_t, *s_ins)


def mm_tn(a, b, name, tk=2048):
    s, m = a.shape
    n = b.shape[1]
    bn = next(c for c in (1024, 1408, 512, 256, 128) if n % c == 0 and m * c * 4 <= 6 * 1024 * 1024)
    tk = min(tk, s)
    nk = s // tk

    def body(a_ref, b_ref, o_ref, acc):
        k = pl.program_id(1)

        @pl.when(k == 0)
        def _():
            acc[...] = jnp.zeros((m, bn), F32)

        acc[...] += _dot_tn(a_ref[...].astype(BF), b_ref[...])

        @pl.when(k == nk - 1)
        def _():
            o_ref[...] = acc[...].astype(BF)

    return _pc(
        body, name=name, grid=(n // bn, nk),
        in_specs=[pl.BlockSpec((tk, m), lambda j, k: (k, 0)), pl.BlockSpec((tk, bn), lambda j, k: (k, j))],
        out_specs=pl.BlockSpec((m, bn), lambda j, k: (0, j)),
        out_shape=SDS((m, n), BF), scratch_shapes=[pltpu.VMEM((m, bn), F32)], compiler_params=_cp(2))(a, b)


def _adamw(w, g, m, v):
    m = ADAM_B1 * m + (1.0 - ADAM_B1) * g
    v = ADAM_B2 * v + (1.0 - ADAM_B2) * (g * g)
    m_hat = m / (1.0 - ADAM_B1 ** ADAM_STEP)
    v_hat = v / (1.0 - ADAM_B2 ** ADAM_STEP)
    return -ADAM_LR * (m_hat / (jnp.sqrt(v_hat) + ADAM_EPS) + ADAM_WD * w), m, v


def _half_rows(rows):
    rh = rows // 2
    return rh, max(b for b in range(16, 257, 16) if rh % b == 0)


def _pc_sp(body, grid, in_specs, out_specs, out_shape, name):
    gs = pltpu.PrefetchScalarGridSpec(num_scalar_prefetch=1, grid=grid, in_specs=in_specs, out_specs=out_specs)
    return _pc(body, grid_spec=gs, out_shape=out_shape, name=name, compiler_params=_cp(len(grid)))


def adamw_halves(sc, own, sib, w, m, v, name):
    rows, cols = w.shape
    rh, br = _half_rows(rows)
    nbk = rh // br

    def body(sc_ref, own_ref, sib_ref, w_ref, m_ref, v_ref, go_ref, d_ref, mo_ref, vo_ref):
        g_ = jnp.where(pl.program_id(0) // nbk == sc_ref[0], own_ref[...], sib_ref[...])
        go_ref[...] = g_
        d_ref[...], mo_ref[...], vo_ref[...] = _adamw(w_ref[...], g_, m_ref[...], v_ref[...])

    half = pl.BlockSpec((br, cols), lambda i, sc_: (i % nbk, 0))
    blk = pl.BlockSpec((br, cols), lambda i, sc_: (i, 0))
    return _pc_sp(body, (2 * nbk,), [half, half, blk, blk, blk], [blk] * 4, [SDS((rows, cols), F32)] * 4,
                  name)(sc, own, sib, w, m, v)


def adamw_cols(sc, own, sib, w, m, v, name, cb=256):
    rows, cols = w.shape
    nk = cols // 2 // cb

    def body(sc_ref, own_ref, sib_ref, w_ref, m_ref, v_ref, go_ref, d_ref, mo_ref, vo_ref):
        g_ = jnp.where(pl.program_id(0) == sc_ref[0], own_ref[...], sib_ref[...])
        go_ref[...] = g_
        d_ref[...], mo_ref[...], vo_ref[...] = _adamw(w_ref[...], g_, m_ref[...], v_ref[...])

    half = pl.BlockSpec((rows, cb), lambda h, k, sc_: (0, k))
    blk = pl.BlockSpec((rows, cb), lambda h, k, sc_: (0, h * nk + k))
    return _pc_sp(body, (2, nk), [half, half, blk, blk, blk], [blk] * 4, [SDS((rows, cols), F32)] * 4,
                  name)(sc, own, sib, w, m, v)


def adamw_small(g, w, m, v):
    def body(g_ref, w_ref, m_ref, v_ref, d_ref, mo_ref, vo_ref):
        d_ref[...], mo_ref[...], vo_ref[...] = _adamw(w_ref[...], g_ref[...], m_ref[...], v_ref[...])

    vm = pl.BlockSpec(memory_space=pltpu.VMEM)
    return _pc(body, name="adamw_small", in_specs=[vm] * 4, out_specs=[vm] * 3, out_shape=[SDS(g.shape, F32)] * 3,
               compiler_params=pltpu.CompilerParams(vmem_limit_bytes=VMEM_LIMIT))(g, w, m, v)


def _pos():
    return lax.axis_index("x"), lax.axis_index("y"), lax.axis_index("c")


def _other_chips(x, y):
    return [(1 - x, y), (x, 1 - y), (1 - x, 1 - y)]


_ANY = pl.BlockSpec(memory_space=pltpu.HBM)


class _Side:
    def __init__(self, ins, out_shapes, nsem, start, finish):
        self.ins, self.out_shapes, self.start, self.finish = list(ins), list(out_shapes), start, finish
        self.scratch = [pltpu.SemaphoreType.DMA((nsem,)), pltpu.SemaphoreType.DMA((nsem,))]
        self.n_in, self.n_out = len(self.ins), len(self.out_shapes)


def _run_side(side, name):
    def body(*refs):
        args_ = (refs[:side.n_in], refs[side.n_in:side.n_in + side.n_out], *refs[side.n_in + side.n_out:])
        side.start(*args_)
        side.finish(*args_)

    return _pc(body, name=name, in_specs=[_ANY] * side.n_in, out_specs=[_ANY] * side.n_out,
               out_shape=side.out_shapes, scratch_shapes=side.scratch)(*side.ins)


def _split_rows(shape):
    return (shape[0] // 2) % 16 == 0


def _core_halves(shape, c):
    if _split_rows(shape):
        h = shape[0] // 2
        return ((pl.ds(pl.multiple_of(c * h, 16), h), slice(None)),
                (pl.ds(pl.multiple_of((1 - c) * h, 16), h), slice(None)))
    h = shape[1] // 2
    assert h % LANE == 0
    return ((slice(None), pl.ds(pl.multiple_of(c * h, LANE), h)),
            (slice(None), pl.ds(pl.multiple_of((1 - c) * h, LANE), h)))


def gather_side(bigs, tinies):
    nb, nt_ = len(bigs), len(tinies)

    def plan(ins, outs, ssem, rsem):
        x, y, c = _pos()
        me = 2 * x + y
        chips = _other_chips(x, y)
        sibling = (x, y, 1 - c)

        def copy(k, src, dst, to):
            return pltpu.make_async_remote_copy(src_ref=src, dst_ref=dst, send_sem=ssem.at[k], recv_sem=rsem.at[k],
                                                device_id=to, device_id_type=MESH)

        sends, landed, passed_on, tiny_landed = [], [], [], []
        for w in range(nb):
            mine, theirs = _core_halves(bigs[w].shape, c)
            for j, (cx, cy) in enumerate(chips):
                sends.append(copy(6 * w + j, ins[w].at[mine], outs[w].at[(me,) + mine], (cx, cy, c)))
                blk = outs[w].at[(2 * cx + cy,) + mine]
                landed.append((copy(6 * w + j, blk, blk, (cx, cy, c)), copy(6 * w + 3 + j, blk, blk, sibling)))
                blk = outs[w].at[(2 * cx + cy,) + theirs]
                passed_on.append(copy(6 * w + 3 + j, blk, blk, sibling))
        for w in range(nt_):
            for j, (cx, cy) in enumerate(chips):
                k = 6 * nb + 3 * w + j
                sends.append(copy(k, ins[nb + w], outs[nb + w].at[me], (cx, cy, c)))
                blk = outs[nb + w].at[2 * cx + cy]
                tiny_landed.append(copy(k, blk, blk, (cx, cy, c)))
        return sends, landed, passed_on, tiny_landed

    def start(ins, outs, ssem, rsem):
        for cp in plan(ins, outs, ssem, rsem)[0]:
            cp.start()

    def finish(ins, outs, ssem, rsem):
        sends, landed, passed_on, tiny_landed = plan(ins, outs, ssem, rsem)
        for arrived, forward in landed:
            arrived.wait_recv()
            forward.start()
        for arrived in tiny_landed + passed_on:
            arrived.wait_recv()
        for cp in sends + [forward for _, forward in landed]:
            cp.wait_send()

    return _Side(list(bigs) + list(tinies), [SDS((4,) + a.shape, a.dtype) for a in list(bigs) + list(tinies)],
                 6 * nb + 3 * nt_, start, finish)


def _sibling_side(srcs, out_shapes, pick):
    def plan(in_refs, out_refs, ssem, rsem):
        x, y, c = _pos()
        return [pltpu.make_async_remote_copy(src_ref=pick(in_refs[w], srcs[w].shape, c), dst_ref=out_refs[w],
                                             send_sem=ssem.at[w], recv_sem=rsem.at[w], device_id=(x, y, 1 - c),
                                             device_id_type=MESH) for w in range(len(srcs))]

    def start(*refs):
        for cp in plan(*refs):
            cp.start()

    def finish(*refs):
        for cp in plan(*refs):
            cp.wait()

    return _Side(srcs, out_shapes, len(srcs), start, finish)


def swap_side(gs):
    def half_shape(g):
        l, r, cols = g.shape
        return (l, r // 2, cols) if _split_rows((r, cols)) else (l, r, cols // 2)

    return _sibling_side(gs, [SDS(half_shape(g), g.dtype) for g in gs],
                         lambda ref, shape, c: ref.at[(slice(None),) + _core_halves(shape[1:], c)[1]])


def join_side(halves):
    return _sibling_side(halves, [SDS(h.shape, h.dtype) for h in halves], lambda ref, shape, c: ref)


def _both_sides(a, b):
    def split(ins, outs, *scr):
        return ((ins[:a.n_in], outs[:a.n_out], *scr[:2]), (ins[a.n_in:], outs[a.n_out:], *scr[2:]))

    def start(*refs):
        ra, rb = split(*refs)
        a.start(*ra)
        b.start(*rb)

    def finish(*refs):
        ra, rb = split(*refs)
        a.finish(*ra)
        b.finish(*rb)

    side = _Side(a.ins + b.ins, a.out_shapes + b.out_shapes, 1, start, finish)
    side.scratch = a.scratch + b.scratch
    return side


COL_BLOCK = 256


def add_half(sc, g, sib, name):
    l, r, cols = g.shape

    def body(sc_ref, g_ref, s_ref, o_ref):
        o_ref[...] = (g_ref[...].astype(F32) + s_ref[...].astype(F32)).astype(BF)

    if _split_rows((r, cols)):
        rh, br = _half_rows(r)
        nbk = rh // br
        blk = pl.BlockSpec((1, br, cols), lambda j, i, sc_: (j, i, 0))
        mine = pl.BlockSpec((1, br, cols), lambda j, i, sc_: (j, sc_[0] * nbk + i, 0))
        return _pc_sp(body, (l, nbk), [mine, blk], blk, SDS((l, rh, cols), BF), name)(sc, g, sib)
    nbk = cols // 2 // COL_BLOCK
    blk = pl.BlockSpec((1, r, COL_BLOCK), lambda j, i, sc_: (j, 0, i))
    mine = pl.BlockSpec((1, r, COL_BLOCK), lambda j, i, sc_: (j, 0, sc_[0] * nbk + i))
    return _pc_sp(body, (l, nbk), [mine, blk], blk, SDS((l, r, cols // 2), BF), name)(sc, g, sib)


def exchange_side(ps):
    n_ = len(ps)

    def width(p_):
        return p_.shape[2] if p_.shape[0] == 4 else p_.shape[2] // 4

    def plan(p_refs, got_refs, ssem, rsem):
        x, y, c = _pos()
        cps = []
        for w in range(n_):
            wd = width(ps[w])
            for j, (cx, cy) in enumerate(_other_chips(x, y)):
                to = 2 * cx + cy
                src = p_refs[w].at[to] if ps[w].shape[0] == 4 else p_refs[w].at[0, :, pl.ds(pl.multiple_of(to * wd, LANE), wd)]
                cps.append(pltpu.make_async_remote_copy(
                    src_ref=src, dst_ref=got_refs[w].at[j], send_sem=ssem.at[3 * w + j], recv_sem=rsem.at[3 * w + j],
                    device_id=(cx, cy, c), device_id_type=MESH))
        return cps

    def start(*refs):
        for cp in plan(*refs):
            cp.start()

    def finish(*refs):
        for cp in plan(*refs):
            cp.wait()

    return _Side(ps, [SDS((3, p_.shape[1], width(p_)), p_.dtype) for p_ in ps], 3 * n_, start, finish)


def sum4(sc, p, got, name):
    _, rh, wd = got.shape

    def body(sc_ref, p_ref, g_ref, r_ref):
        r_ref[...] = ((p_ref[0].astype(F32) + g_ref[0].astype(F32)) + (g_ref[1].astype(F32) + g_ref[2].astype(F32)))

    if rh % 16:
        assert p.shape[0] == 4
        return _pc_sp(body, (wd // COL_BLOCK,),
                      [pl.BlockSpec((1, rh, COL_BLOCK), lambda i, sc_: (sc_[1], 0, i)),
                       pl.BlockSpec((3, rh, COL_BLOCK), lambda i, sc_: (0, 0, i))],
                      pl.BlockSpec((rh, COL_BLOCK), lambda i, sc_: (0, i)), SDS((rh, wd), F32), name)(sc, p, got)
    _, br = _half_rows(2 * rh)
    own = (pl.BlockSpec((1, br, wd), lambda i, sc_: (sc_[1], i, 0)) if p.shape[0] == 4
           else pl.BlockSpec((1, br, wd), lambda i, sc_: (0, i, sc_[1])))
    return _pc_sp(body, (rh // br,), [own, pl.BlockSpec((3, br, wd), lambda i, sc_: (0, i, 0))],
                  pl.BlockSpec((br, wd), lambda i, sc_: (i, 0)), SDS((rh, wd), F32), name)(sc, p, got)


def allreduce_small(g):
    rows = g.shape[0]
    rh = rows // 2

    def body(g_ref, out_ref, sib_buf, chip_buf, sum_sc, ssem, rsem):
        x, y, c = _pos()
        me = 2 * x + y
        sibling = (x, y, 1 - c)
        mine = pl.ds(pl.multiple_of(c * rh, 8), rh)

        def copy(k, src, dst, to):
            return pltpu.make_async_remote_copy(src_ref=src, dst_ref=dst, send_sem=ssem.at[k], recv_sem=rsem.at[k],
                                                device_id=to, device_id_type=MESH)

        cp = copy(0, g_ref, sib_buf, sibling)
        cp.start()
        cp.wait()
        sum_sc[...] = g_ref[...] + sib_buf[...]
        chips = _other_chips(x, y)
        cps = [copy(1 + j, sum_sc.at[mine], chip_buf.at[me], (cx, cy, c)) for j, (cx, cy) in enumerate(chips)]
        for cp in cps:
            cp.start()
        chip_buf[me] = sum_sc[mine, :]
        for j, (cx, cy) in enumerate(chips):
            copy(1 + j, sum_sc.at[mine], chip_buf.at[2 * cx + cy], (cx, cy, c)).wait_recv()
        for cp in cps:
            cp.wait_send()
        out_ref[mine, :] = (chip_buf[0] + chip_buf[1]) + (chip_buf[2] + chip_buf[3])
        cp = copy(4, out_ref.at[mine], out_ref.at[mine], sibling)
        cp.start()
        cp.wait()

    vm = pl.BlockSpec(memory_space=pltpu.VMEM)
    return _pc(body, name="allreduce_small", in_specs=[vm], out_specs=vm, out_shape=SDS((rows, LANE), F32),
               scratch_shapes=[pltpu.VMEM((rows, LANE), F32), pltpu.VMEM((4, rh, LANE), F32), pltpu.VMEM((rows, LANE), F32),
                               pltpu.SemaphoreType.DMA((5,)), pltpu.SemaphoreType.DMA((5,))],
               compiler_params=pltpu.CompilerParams(vmem_limit_bytes=VMEM_LIMIT))(g)


def _pack_small(entries, get):
    flat = jnp.concatenate([get(n).reshape(-1).astype(F32) for n, _ in entries])
    rows = -(-flat.shape[0] // (8 * LANE)) * 8
    return jnp.pad(flat, (0, rows * LANE - flat.shape[0])).reshape(rows, LANE)


def _unpack_small(entries, packed):
    out, off = {}, 0
    flat = packed.reshape(-1)
    for name, n in entries:
        out[name] = flat[off:off + n]
        off += n
    return out


def _cols_full(blk):
    return blk.transpose(1, 0, 2).reshape(blk.shape[1], 4 * blk.shape[2])


def kernel(x, p, pre_mix_norm, w_in, a_ln_g, a_ln_b, a_spatial_w, a_spatial_b, a_out, b_gk, b_gk_bias, b_out_norm, b_out, w_mix_out, post_mix_norm, pre_ffn_norm, w_up, conv_w, conv_b, w_down, post_ffn_norm, w_ple, w_ple_gate, post_ple_norm, loss_target, m_pre_mix_norm, m_w_in, m_a_ln_g, m_a_ln_b, m_a_spatial_w, m_a_spatial_b, m_a_out, m_b_gk, m_b_gk_bias, m_b_out_norm, m_b_out, m_w_mix_out, m_post_mix_norm, m_pre_ffn_norm, m_w_up, m_conv_w, m_conv_b, m_w_down, m_post_ffn_norm, m_w_ple, m_w_ple_gate, m_post_ple_norm, v_pre_mix_norm, v_w_in, v_a_ln_g, v_a_ln_b, v_a_spatial_w, v_a_spatial_b, v_a_out, v_b_gk, v_b_gk_bias, v_b_out_norm, v_b_out, v_w_mix_out, v_post_mix_norm, v_pre_ffn_norm, v_w_up, v_conv_w, v_conv_b, v_w_down, v_post_ffn_norm, v_w_ple, v_w_ple_gate, v_post_ple_norm):
    args = dict(locals())
    order = ['pre_mix_norm', 'w_in', 'a_ln_g', 'a_ln_b', 'a_spatial_w', 'a_spatial_b', 'a_out', 'b_gk', 'b_gk_bias',
             'b_out_norm', 'b_out', 'w_mix_out', 'post_mix_norm', 'pre_ffn_norm', 'w_up', 'conv_w', 'conv_b', 'w_down',
             'post_ffn_norm', 'w_ple', 'w_ple_gate', 'post_ple_norm']
    assert sorted(BIG + TINY + tuple(n for n, _ in SMALL)) == sorted(order)
    s = x.shape[1]
    xs = x.reshape(s, D)
    ps = p.reshape(s, PLE)
    tgt = loss_target.reshape(s, D)
    t_big = min(1024, s)
    t_mid = min(512, s)
    t_small = min(256, s)
    t_gla = min(256, s)
    mx_, my_, mc_ = _pos()
    me = 2 * mx_ + my_
    sc = jnp.stack([mc_, me]).astype(jnp.int32)
    shard = lambda n: args[n].reshape(args[n].shape[1:])

    mine = {n: shard(n).astype(BF) for n in BIG}
    mine["w_in"] = shard("w_in").T.astype(BF)
    mine.update({n: shard(n) for n in TINY})
    fill = lambda names, gots: {n: lax.dynamic_update_slice(got, mine[n][None], (me, 0, 0)) for n, got in zip(names, gots)}
    first = ("w_in",) + TINY
    full = fill(first, _run_side(gather_side([mine["w_in"]], [mine[n] for n in TINY]), "gather_first"))
    wi = full["w_in"].reshape(4 * 1540, D)
    seg = lambda a, b: wi[a:b]
    qk = [seg(1024 + h * B_HK, 1024 + (h + 1) * B_HK) for h in range(B_H)]
    kk = [seg(1536 + h * B_HK, 1536 + (h + 1) * B_HK) for h in range(B_H)]
    w_z = jnp.concatenate([seg(0, 1024)] + [m_ for h in range(B_H) for m_ in (qk[h], kk[h])]
                          + [seg(2048, 4096), seg(4112, 6160), seg(4096, 4112), jnp.zeros((LANE - B_RANK, D), BF)], axis=0)
    wgk = jnp.pad(_cols_full(full["b_gk"]).astype(BF), ((0, LANE - B_RANK), (0, 0)))
    w_conv = _cols_full(full["conv_w"])
    g1, g2, g3 = pre_mix_norm.reshape(1, D), post_mix_norm.reshape(1, D), pre_ffn_norm.reshape(1, D)
    g4, g5 = post_ffn_norm.reshape(1, D), post_ple_norm.reshape(1, D)
    ln_g, ln_b = a_ln_g.reshape(1, A_W), a_ln_b.reshape(1, A_W)
    w_s = a_spatial_w.reshape(A_G, A_C, A_C)
    w_cat = w_s.transpose(1, 0, 2).reshape(A_C, A_G * A_C)
    w_cat_t = w_s.transpose(2, 0, 1).reshape(A_C, A_G * A_C)
    bias_full = jnp.repeat(a_spatial_b.reshape(A_G, A_C).T, A_GD, axis=1)
    bdm = (jnp.arange(A_G * A_C)[:, None] // A_C == jnp.arange(A_W)[None, :] // A_GD).astype(BF)
    gk_bias = b_gk_bias.reshape(1, B_H * B_HK)
    wn = b_out_norm.reshape(1, B_HV)
    cb = conv_b.reshape(1, 2 * D_FF)
    idx = jnp.arange(t_gla)
    ltri = ((idx[:, None] // B_C == idx[None, :] // B_C) & (idx[None, :] <= idx[:, None])).astype(BF)

    a, z, qk32, zl, *gots = norm_matmul(xs, g1, w_z, D, t_big, "in_proj", nblk=6, f32_blk=1, tail_blk=48,
                                        side=gather_side([mine[n] for n in BIG[1:]], []))
    full.update(fill(BIG[1:], gots))
    w_aout, w_ple_f = _cols_full(full["a_out"]), _cols_full(full["w_ple"])
    w_bout, w_mix, w_pg = (full[n].reshape(D, D) for n in ("b_out", "w_mix_out", "w_ple_gate"))
    w_dn, w_up3 = full["w_down"].reshape(D_FF, D), full["w_up"]
    sa = sgu_fwd(z, ln_g, ln_b, w_cat, bias_full, bdm, t_mid)
    ob, o, states = gla_fwd(z, qk32, zl, wgk, gk_bias, wn, ltri, t_gla)
    ya, yb, mp, mx, h1 = mix_fwd(sa, ob, z, xs, w_aout, w_bout, w_mix, g2, t_mid)
    c, up_g, up_v, cg, cv, ff = ffn_up_fwd(h1, g3, w_up3, w_conv, cb, t_mid)
    f, h2, pg, pe, dy, loss = out_fwd(ff, h1, ps, tgt, w_dn, w_pg, w_ple_f, g4, g5, t_mid)

    dh2, dpe, dpg, df, dff, gg5, gg4 = out_bwd(dy, pg, pe, f, g5, g4, w_pg, w_dn, t_mid)
    dup_g, dup_v, gcw, gcb, dh1, gg3 = ffn_up_bwd(up_g, up_v, cg, cv, dff, w_conv, _cols_full(w_up3), h1, g3, dh2,
                                                  t_small)
    grads = {
        "w_up": jnp.concatenate([mm_tn(c, dup_g, "dw_up_g"), mm_tn(c, dup_v, "dw_up_v")], axis=1)[None],
        "w_down": mm_tn(ff, df, "dw_down").reshape(4, D_FF // 4, D),
        "w_ple": mm_tn(ps, dpe, "dw_ple")[None],
        "w_ple_gate": mm_tn(h2, dpg, "dw_ple_gate").reshape(4, D // 4, D),
    }
    ffn_side = ("w_up", "w_down", "w_ple", "w_ple_gate")
    dmx, dya, dyb, dga, dgb, dsa, dob, gg2, *sibs = mix_bwd(dh1, mx, z, ya, yb, g2, w_mix, w_aout, w_bout, t_mid,
                                                            side=swap_side([grads[n] for n in ffn_side]))
    sib = dict(zip(ffn_side, sibs))
    duv, g_lng, g_lnb, g_wcat, g_bst = sgu_bwd(z, dsa, ln_g, ln_b, w_cat, w_cat_t, bias_full, bdm, t_mid)
    g_ws = g_wcat.reshape(A_C, A_G, A_C).transpose(1, 0, 2)
    grads.update({
        "a_out": mm_tn(sa, dya, "dw_a_out")[None],
        "b_out": mm_tn(ob, dyb, "dw_b_out").reshape(4, D // 4, D),
        "w_mix_out": mm_tn(mp, dmx, "dw_mix").reshape(4, D // 4, D),
    })

    def swap(names):
        sib.update(zip(names, _run_side(swap_side([grads[n] for n in names]), "swap_halves_" + names[0])))

    swap(("a_out", "b_out", "w_mix_out"))
    parts = {n: add_half(sc, grads[n], sib[n], "partial_" + n) for n in BIG[1:]}
    dqk, dvb, dog, dpre, dlr, g_gkb, g_wn, *gots = gla_bwd(
        z, qk32, zl, o, dob, states, wgk, gk_bias, wn, ltri, ltri.T, t_gla,
        side=exchange_side([parts[n] for n in BIG[1:]]))
    reds = {n: sum4(sc, parts[n], got_, "sum_" + n) for n, got_ in zip(BIG[1:], gots)}
    segs = [duv, dqk, dvb, dog, dga, dgb, dlr]

    gz = [mm_tn(sg_, a, "dw_in_%d" % k) for k, sg_ in enumerate(segs)]
    gq = [gz[1][h * 256:h * 256 + B_HK] for h in range(B_H)]
    gk = [gz[1][h * 256 + B_HK:(h + 1) * 256] for h in range(B_H)]
    g_in = jnp.concatenate([gz[0]] + gq + gk + [gz[2], gz[3], gz[6][:B_RANK], gz[4], gz[5]], axis=0)
    grads["w_in"] = g_in.reshape(4, 1540, D)
    swap(("w_in",))
    parts["w_in"] = add_half(sc, grads["w_in"], sib["w_in"], "partial_w_in")
    dx, gg1, got_in, *sib_reds = nt_normbwd(
        segs, w_z, xs, g1, dh1, t_small, "in_bwd",
        side=_both_sides(exchange_side([parts["w_in"]]), join_side([reds[n] for n in BIG[1:]])))
    sib_red = dict(zip(BIG[1:], sib_reds))

    reds["w_in"] = sum4(sc, parts["w_in"], got_in, "sum_w_in")
    sib_red["w_in"], = _run_side(join_side([reds["w_in"]]), "join_w_in")
    outs = {}
    for n in BIG:
        if n == "w_in":
            res = adamw_cols(sc, reds[n], sib_red[n], shard(n).T, shard("m_" + n).T, shard("v_" + n).T, "adamw_" + n)
            res = [r_.T for r_ in res]
        else:
            res = adamw_halves(sc, reds[n], sib_red[n], shard(n), shard("m_" + n), shard("v_" + n), "adamw_" + n)
        outs[n] = [r_.reshape(args[n].shape) for r_ in res]

    small_g = {
        "pre_mix_norm": gg1, "a_ln_g": g_lng, "a_ln_b": g_lnb, "a_spatial_w": g_ws, "a_spatial_b": g_bst.T,
        "b_gk_bias": g_gkb, "b_out_norm": g_wn, "post_mix_norm": gg2, "pre_ffn_norm": gg3,
        "conv_b": gcb, "post_ffn_norm": gg4, "post_ple_norm": gg5,
        "b_gk": mm_tn(zl, dpre, "dw_gk")[:B_RANK], "conv_w": gcw,
        "loss": loss,
    }
    red_entries = SMALL + (("b_gk", B_RANK * 512), ("conv_w", 3 * 2 * D_FF), ("loss", 1))
    g_fin = _unpack_small(red_entries, allreduce_small(_pack_small(red_entries, lambda n: small_g[n])))
    g_fin["b_gk"] = lax.dynamic_slice(g_fin["b_gk"].reshape(B_RANK, 512), (0, me * B_HK), (B_RANK, B_HK))
    g_fin["conv_w"] = lax.dynamic_slice(g_fin["conv_w"].reshape(3, 2 * D_FF), (0, me * 1408), (3, 1408))
    upd_entries = SMALL + (("b_gk", B_RANK * B_HK), ("conv_w", 3 * 1408))
    res = adamw_small(*[_pack_small(upd_entries, get) for get in
                        (lambda n: g_fin[n], lambda n: args[n], lambda n: args["m_" + n], lambda n: args["v_" + n])])
    res = [_unpack_small(upd_entries, r_) for r_ in res]
    for n, _ in upd_entries:
        outs[n] = [r_[n].reshape(args[n].shape) for r_ in [g_fin] + res]

    return (g_fin["loss"].reshape(()), dx.reshape(x.shape), *[outs[n][0] for n in order], *[outs[n][1] for n in order],
            *[outs[n][2] for n in order], *[outs[n][3] for n in order])
```

```python
import functools
import math

import jax
import jax.numpy as jnp
from jax import lax
from jax.experimental import pallas as pl
from jax.experimental.pallas import tpu as pltpu

F32 = jnp.float32
BF = jnp.bfloat16
SDS = jax.ShapeDtypeStruct
MESH = pl.DeviceIdType.MESH

EPS = 1e-6
D = 1024
A_W = 512
A_G, A_C = 8, 128
A_GD = A_W // A_G
B_H, B_HK, B_HV = 4, 128, 256
B_C = 64
GLA_HPB = 4
B_RANK = 16
D_FF = 2816
PLE = 256
LANE = 128
VMEM_LIMIT = 60 * 1024 * 1024

ADAM_LR, ADAM_B1, ADAM_B2, ADAM_EPS, ADAM_WD, ADAM_STEP = 0.001, 0.9, 0.999, 1e-08, 0.01, 10

_GC = math.sqrt(2.0 / math.pi)
_GA = 0.044715

BIG = ("w_in", "a_out", "b_out", "w_mix_out", "w_up", "w_down", "w_ple", "w_ple_gate")
TINY = ("b_gk", "conv_w")
SMALL = (("pre_mix_norm", 1024), ("a_ln_g", 512), ("a_ln_b", 512), ("a_spatial_w", 131072),
         ("a_spatial_b", 1024), ("b_gk_bias", 512), ("b_out_norm", 256), ("post_mix_norm", 1024),
         ("pre_ffn_norm", 1024), ("conv_b", 5632), ("post_ffn_norm", 1024), ("post_ple_norm", 1024))


def _pc(body, **kw):
    return pl.pallas_call(body, **kw)


def _cp(n):
    return pltpu.CompilerParams(dimension_semantics=("arbitrary",) * n, vmem_limit_bytes=VMEM_LIMIT)


def _const(shape):
    nd = len(shape)
    return pl.BlockSpec(shape, lambda *_: (0,) * nd, pipeline_mode=pl.Buffered(1))


def _acc(shape):
    nd = len(shape)
    return pl.BlockSpec(shape, lambda *_: (0,) * nd)


def _dot(a, b):
    return jnp.dot(a, b, preferred_element_type=F32)


def _dot_nt(a, b):
    return lax.dot_general(a, b, (((1,), (1,)), ((), ())), preferred_element_type=F32)


def _dot_tn(a, b):
    return lax.dot_general(a, b, (((0,), (0,)), ((), ())), preferred_element_type=F32)


def _gelu(x):
    return 0.5 * x * (1.0 + jnp.tanh(_GC * (x + _GA * x * x * x)))


def _gelu_and_grad(x):
    x2 = x * x
    s = 0.5 * jnp.tanh((_GC * x) * (1.0 + _GA * x2)) + 0.5
    g = x * s
    return g, s + g * (1.0 - s) * ((6.0 * _GC * _GA) * x2 + 2.0 * _GC)


def _log_sigmoid(x):
    return jnp.minimum(x, 0.0) - jnp.log(1.0 + jnp.exp(-jnp.abs(x)))


def _rms(x, g):
    return x * lax.rsqrt(jnp.mean(x * x, axis=-1, keepdims=True) + EPS) * g


def _rms_bwd(dy, x, g):
    r = lax.rsqrt(jnp.mean(x * x, axis=-1, keepdims=True) + EPS)
    n = x * r
    dn = dy * g
    dx = r * (dn - n * jnp.mean(dn * n, axis=-1, keepdims=True))
    return dx, jnp.sum(dy * n, axis=0, keepdims=True)


def _ldot3(l, x):
    h = x.astype(BF)
    r = x - h.astype(F32)
    m = r.astype(BF)
    lo = (r - m.astype(F32)).astype(BF)
    return _dot(l, h) + _dot(l, m) + _dot(l, lo)


def _split_side(refs, n_in, n_out, n_scratch, side):
    si, so = (side.n_in, side.n_out) if side else (0, 0)
    cuts = [n_in, si, n_out, so, n_scratch]
    out, at = [], 0
    for c in cuts:
        out.append(refs[at:at + c])
        at += c
    return (*out, refs[at:])


def _side_specs(side):
    return ([_ANY] * side.n_in, [_ANY] * side.n_out, side.out_shapes, side.scratch, side.ins) if side else ([],) * 5


def norm_matmul(x, g, wt, bn, t, name, nblk, f32_blk, tail_blk, side=None):
    s, dm = x.shape
    w_spec = pl.BlockSpec((bn, dm), lambda i, j: (j, 0))
    nt = s // t

    def body(*refs):
        (x_ref, g_ref, w_ref, wl_ref), s_in, outs, s_out, (a_sc,), s_scr = _split_side(refs, 4, 4, 1, side)
        a_ref, z_ref, f32_ref, tail_ref = outs
        i, j = pl.program_id(0), pl.program_id(1)
        if side:
            @pl.when((i == 0) & (j == 0))
            def _():
                side.start(s_in, s_out, *s_scr)

        @pl.when(j == 0)
        def _():
            a = _rms(x_ref[...], g_ref[...]).astype(BF)
            a_sc[...] = a
            a_ref[...] = a
            tail_ref[...] = _dot_nt(a, wl_ref[...]).astype(BF)

        acc = _dot_nt(a_sc[...], w_ref[...])
        z_ref[...] = acc.astype(BF)

        @pl.when(j == f32_blk)
        def _():
            f32_ref[...] = acc
        if side:
            @pl.when((i == nt - 1) & (j == nblk - 1))
            def _():
                side.finish(s_in, s_out, *s_scr)

    si_specs, so_specs, so_shapes, s_scratch, s_ins = _side_specs(side)
    return _pc(
        body, name=name, grid=(nt, nblk),
        in_specs=[pl.BlockSpec((t, dm), lambda i, j: (i, 0)), _const((1, dm)), w_spec,
                  pl.BlockSpec((LANE, dm), lambda i, j: (tail_blk, 0), pipeline_mode=pl.Buffered(1))] + si_specs,
        out_specs=[pl.BlockSpec((t, dm), lambda i, j: (i, 0)), pl.BlockSpec((t, bn), lambda i, j: (i, j)),
                   pl.BlockSpec((t, bn), lambda i, j: (i, 0)), pl.BlockSpec((t, LANE), lambda i, j: (i, 0))] + so_specs,
        out_shape=[SDS((s, dm), BF), SDS((s, nblk * bn), BF), SDS((s, bn), F32), SDS((s, LANE), BF)] + so_shapes,
        scratch_shapes=[pltpu.VMEM((t, dm), BF)] + s_scratch, compiler_params=_cp(2))(x, g, wt, wt, *s_ins)


def _sgu_weights(wc_ref, transposed):
    r = lax.broadcasted_iota(jnp.int32, (A_C, A_G * A_C), 0)
    c = lax.broadcasted_iota(jnp.int32, (A_C, A_G * A_C), 1) & (A_C - 1)
    return jnp.where((r <= c) if transposed else (c <= r), wc_ref[...], 0.0).astype(BF)


def _sgu_spread(xs, bdm):
    return jnp.concatenate([jnp.tile(x, (A_G, 1)) * bdm for x in xs], axis=1)


def _sgu_recompute(v, lng, lnb):
    gv, dgv = _gelu_and_grad(v)
    gv = gv.astype(F32)
    mu = jnp.mean(gv, axis=-1, keepdims=True)
    xc = gv - mu
    rstd = lax.rsqrt(jnp.mean(xc * xc, axis=-1, keepdims=True) + EPS)
    xhat = xc * rstd
    return dgv, rstd, xhat, (xhat * lng + lnb).astype(BF)


def sgu_fwd(z, ln_g, ln_b, w_cat, bias_full, bdm, t):
    s = z.shape[0]
    nch = t // A_C

    def body(u_ref, v_ref, g_ref, b_ref, wc_ref, bias_ref, bdm_ref, sa_ref):
        vns = [_sgu_recompute(v_ref[pl.ds(ci * A_C, A_C), :], g_ref[...], b_ref[...])[3]
               for ci in range(nch)]
        mixed = _dot(_sgu_weights(wc_ref, False), _sgu_spread(vns, bdm_ref[...]))
        for ci in range(nch):
            rows = pl.ds(ci * A_C, A_C)
            s_ = mixed[:, ci * A_W:(ci + 1) * A_W] + bias_ref[...]
            sa_ref[rows, :] = _gelu(u_ref[rows, :]) * s_.astype(BF)

    return _pc(
        body, name="sgu_fwd", grid=(s // t,),
        in_specs=[pl.BlockSpec((t, A_W), lambda i: (i, 0)), pl.BlockSpec((t, A_W), lambda i: (i, 1)),
                  _const((1, A_W)), _const((1, A_W)), _const((A_C, A_G * A_C)), _const((A_C, A_W)),
                  _const((A_G * A_C, A_W))],
        out_specs=pl.BlockSpec((t, A_W), lambda i: (i, 0)),
        out_shape=SDS((s, A_W), BF), compiler_params=_cp(1))(z, z, ln_g, ln_b, w_cat, bias_full, bdm)


def _gla_decays(qk, lr, wgk, bias, l, t):
    nc = t // B_C
    q = qk[:, :B_HK].astype(F32) * (B_HK ** -0.5)
    k = qk[:, B_HK:].astype(F32)
    pre = _dot(lr, wgk) + bias
    la = _log_sigmoid(pre) * (1.0 / 16.0)
    b = _ldot3(l, la)
    b3 = b.reshape(nc, B_C, B_HK)
    bl = jnp.broadcast_to(b3[:, B_C - 1:B_C, :], (nc, B_C, B_HK)).reshape(t, B_HK)
    eb, enb, etb = jnp.exp(b), jnp.exp(-b), jnp.exp(bl - b)
    return pre, b, bl, eb, enb, etb, q * eb, k * enb, k * etb


def gla_fwd(z, qk32, zl, wgk, bias, wn, ltri, t):
    s = z.shape[0]
    nc = t // B_C
    hpb = GLA_HPB
    assert hpb == B_H
    kw, vw = hpb * B_HK, hpb * B_HV

    def body(qk_ref, v_ref, og_ref, lr_ref, wgk_ref, bias_ref, wn_ref, l_ref, ob_ref, o_ref, st_ref, st_sc, o_sc):
        @pl.when(pl.program_id(0) == 0)
        def _():
            st_sc[...] = jnp.zeros((B_H, B_HV, B_HK), F32)

        lr, l = lr_ref[...], l_ref[...]
        for hh in range(hpb):
            h = hh
            cv, ck = slice(hh * B_HV, (hh + 1) * B_HV), slice(hh * B_HK, (hh + 1) * B_HK)
            _, _, bl, _, _, _, qd, ki, kt = _gla_decays(qk_ref[:, cv], lr, wgk_ref[:, ck], bias_ref[:, ck], l, t)
            qd, ki, kt = qd.astype(BF), ki.astype(BF), kt.astype(BF)
            vb = v_ref[:, cv]
            sc = jnp.where(l > 0, _dot_nt(qd, ki), 0.0).astype(BF)
            o_sc[hh] = _dot(sc, vb)
            for n in range(nc):
                rows = slice(n * B_C, (n + 1) * B_C)
                st = st_sc[h]
                stb = st.astype(BF)
                st_ref[n, hh] = stb
                o_sc[hh, rows, :] += _dot_nt(qd[rows], stb)
                st_sc[h] = st * jnp.exp(bl[n * B_C:n * B_C + 1, :]) + _dot_tn(vb[rows], kt[rows])
            ob = o_sc[hh].astype(BF)
            o_ref[:, cv] = ob
            og = og_ref[:, cv].astype(F32)
            ob_ref[:, cv] = (_rms(ob.astype(F32), wn_ref[...]) * og * jax.nn.sigmoid(og)).astype(BF)

    vo, go = 2048 // vw, 3072 // vw
    return _pc(
        body, name="gla_fwd", grid=(s // t, B_H // hpb),
        in_specs=[pl.BlockSpec((t, vw), lambda i, g: (i, g)), pl.BlockSpec((t, vw), lambda i, g: (i, vo + g)),
                  pl.BlockSpec((t, vw), lambda i, g: (i, go + g)), pl.BlockSpec((t, LANE), lambda i, g: (i, 0)),
                  pl.BlockSpec((LANE, kw), lambda i, g: (0, g)), pl.BlockSpec((1, kw), lambda i, g: (0, g)),
                  _const((1, B_HV)), _const((t, t))],
        out_specs=[pl.BlockSpec((t, vw), lambda i, g: (i, g)), pl.BlockSpec((t, vw), lambda i, g: (i, g)),
                   pl.BlockSpec((nc, hpb, B_HV, B_HK), lambda i, g: (i, g, 0, 0))],
        out_shape=[SDS((s, D), BF), SDS((s, D), BF), SDS((s // B_C, B_H, B_HV, B_HK), BF)],
        scratch_shapes=[pltpu.VMEM((B_H, B_HV, B_HK), F32), pltpu.VMEM((hpb, t, B_HV), F32)],
        compiler_params=_cp(2))(qk32, z, z, zl, wgk, bias, wn, ltri)


def mix_fwd(sa, ob, z, x, a_out, b_out, w_mix, g2, t):
    s = x.shape[0]

    def body(sa_ref, ob_ref, ga_ref, gb_ref, x_ref, ao_ref, bo_ref, wm_ref, g2_ref,
             ya_ref, yb_ref, mp_ref, mx_ref, h1_ref):
        ya = _dot(sa_ref[...], ao_ref[...]).astype(BF)
        yb = _dot(ob_ref[...], bo_ref[...]).astype(BF)
        ya_ref[...] = ya
        yb_ref[...] = yb
        mp = (jax.nn.sigmoid(ga_ref[...].astype(F32)) * ya.astype(F32)
              + jax.nn.sigmoid(gb_ref[...].astype(F32)) * yb.astype(F32)).astype(BF)
        mp_ref[...] = mp
        mx = _dot(mp, wm_ref[...]).astype(BF)
        mx_ref[...] = mx
        h1_ref[...] = x_ref[...] + _rms(mx.astype(F32), g2_ref[...])

    row = lambda w: pl.BlockSpec((t, w), lambda i: (i, 0))
    return _pc(
        body, name="mix_fwd", grid=(s // t,),
        in_specs=[row(A_W), row(D), pl.BlockSpec((t, D), lambda i: (i, 4)), pl.BlockSpec((t, D), lambda i: (i, 5)),
                  row(D), _const((A_W, D)), _const((D, D)), _const((D, D)), _const((1, D))],
        out_specs=[row(D)] * 5,
        out_shape=[SDS((s, D), BF)] * 4 + [SDS((s, D), F32)],
        compiler_params=_cp(1))(sa, ob, z, z, x, a_out, b_out, w_mix, g2)


def ffn_up_fwd(h1, g3, w_up3, conv_w, conv_b, t):
    s = h1.shape[0]
    bn = w_up3.shape[2]

    def body(x_ref, g_ref, wg_ref, wv_ref, cwg_ref, cwv_ref, cbg_ref, cbv_ref,
             c_ref, ug_ref, uv_ref, cg_ref, cv_ref, ff_ref, c_sc, carry):
        i, j = pl.program_id(0), pl.program_id(1)

        @pl.when(j == 0)
        def _():
            c = _rms(x_ref[...], g_ref[...]).astype(BF)
            c_sc[...] = c
            c_ref[...] = c

        @pl.when(i == 0)
        def _():
            carry[j] = jnp.zeros((2, 8, bn), F32)

        def branch(k, w_ref, cw_ref, cb_ref, u_ref, o_ref):
            ub = _dot(c_sc[...], w_ref[...]).astype(BF)
            u_ref[...] = ub
            u = ub.astype(F32)
            ext = jnp.concatenate([carry[j, k], u], axis=0)
            carry[j, k] = u[t - 8:]
            w = cw_ref[...]
            cc = (cb_ref[...] + w[0:1] * pltpu.roll(ext, 2, 0) + w[1:2] * pltpu.roll(ext, 1, 0) + w[2:3] * ext)[8:]
            cc = cc.astype(BF)
            o_ref[...] = cc
            return cc

        g = _gelu(branch(0, wg_ref, cwg_ref, cbg_ref, ug_ref, cg_ref))
        ff_ref[...] = g * branch(1, wv_ref, cwv_ref, cbv_ref, uv_ref, cv_ref)

    col = lambda rows, off: pl.BlockSpec((rows, bn), lambda i, j: (0, j + off))
    out = pl.BlockSpec((t, bn), lambda i, j: (i, j))
    return _pc(
        body, name="ffn_up_fwd", grid=(s // t, 2),
        in_specs=[pl.BlockSpec((t, D), lambda i, j: (i, 0)), _const((1, D)),
                  pl.BlockSpec((None, D, bn), lambda i, j: (j, 0, 0)), pl.BlockSpec((None, D, bn), lambda i, j: (j + 2, 0, 0)),
                  col(3, 0), col(3, 2), col(1, 0), col(1, 2)],
        out_specs=[pl.BlockSpec((t, D), lambda i, j: (i, 0))] + [out] * 5,
        out_shape=[SDS((s, D), BF)] + [SDS((s, D_FF), BF)] * 5,
        scratch_shapes=[pltpu.VMEM((t, D), BF), pltpu.VMEM((2, 2, 8, bn), F32)],
        compiler_params=_cp(2))(h1, g3, w_up3, w_up3, conv_w, conv_w, conv_b, conv_b)


def ffn_up_bwd(ug, uv, cg, cv, dff, conv_w, w_up, h1, g3, dh2, t):
    s = h1.shape[0]
    nt = s // t
    hb = t // 8
    bn = 1408
    r = t + 8

    def body(ug_ref, uv_ref, cg_ref, cv_ref, cag_ref, cav_ref, d_ref, da_ref, cw_ref, w_ref, x_ref, g_ref, dres_ref,
             du_ref, gw_ref, gb_ref, dx_ref, gg_ref):
        i = pl.program_id(0)

        @pl.when(i == 0)
        def _():
            gw_ref[...] = jnp.zeros((3, 2 * D_FF), F32)
            gb_ref[...] = jnp.zeros((1, 2 * D_FF), F32)
            gg_ref[...] = jnp.zeros((1, D), F32)

        more = (i < nt - 1).astype(BF)

        def gate(c_g, c_v, d_):
            gl, dgl = _gelu_and_grad(c_g)
            return (d_ * c_v * dgl).astype(F32), (d_ * gl).astype(F32)

        def back(dc, dc_next, u_ref, cols, off):
            w = cw_ref[:, off:off + bn]
            d_ext = jnp.concatenate([dc, dc_next], axis=0)
            d1, d2 = pltpu.roll(d_ext, r - 1, 0)[:t], pltpu.roll(d_ext, r - 2, 0)[:t]
            du_ref[:, off:off + bn] = (w[2:3] * dc + w[1:2] * d1 + w[0:1] * d2).astype(BF)
            u = u_ref[:, cols].astype(F32)
            gw_ref[0:1, off:off + bn] += jnp.sum(d2 * u, axis=0, keepdims=True)
            gw_ref[1:2, off:off + bn] += jnp.sum(d1 * u, axis=0, keepdims=True)
            gw_ref[2:3, off:off + bn] += jnp.sum(dc * u, axis=0, keepdims=True)
            gb_ref[:, off:off + bn] += jnp.sum(dc, axis=0, keepdims=True)

        for kb in range(D_FF // bn):
            cols = slice(kb * bn, (kb + 1) * bn)
            dg, dv = gate(cg_ref[:, cols], cv_ref[:, cols], d_ref[:, cols])
            dg_n, dv_n = gate(cag_ref[:, cols], cav_ref[:, cols], da_ref[:, cols] * more)
            back(dg, dg_n, ug_ref, cols, kb * bn)
            back(dv, dv_n, uv_ref, cols, D_FF + kb * bn)

        acc = _dot_nt(du_ref[...], w_ref[...])
        dxn, dg3 = _rms_bwd(acc, x_ref[...], g_ref[...])
        dx_ref[...] = dres_ref[...] + dxn
        gg_ref[...] += dg3

    tile = lambda width: pl.BlockSpec((t, width), lambda i: (i, 0))
    after = pl.BlockSpec((8, D_FF), lambda i: (jnp.minimum((i + 1) * hb, nt * hb - 1), 0))
    return _pc(
        body, name="ffn_up_bwd", grid=(nt,),
        in_specs=[tile(D_FF)] * 4 + [after, after, tile(D_FF), after, _const((3, 2 * D_FF)), _const((D, 2 * D_FF)),
                                     tile(D), _const((1, D)), tile(D)],
        out_specs=[tile(2 * D_FF), _acc((3, 2 * D_FF)), _acc((1, 2 * D_FF)), tile(D), _acc((1, D))],
        out_shape=[SDS((s, 2 * D_FF), BF), SDS((3, 2 * D_FF), F32), SDS((1, 2 * D_FF), F32),
                   SDS((s, D), F32), SDS((1, D), F32)],
        compiler_params=_cp(1))(ug, uv, cg, cv, cg, cv, dff, dff, conv_w, w_up, h1, g3, dh2)


def out_fwd(ff, h1, p, tgt, w_down, w_pg, w_ple, g4, g5, t):
    s = h1.shape[0]

    def body(ff_ref, h1_ref, p_ref, t_ref, wd_ref, wpg_ref, wpl_ref, g4_ref, g5_ref,
             f_ref, h2_ref, pg_ref, pe_ref, dy_ref, loss_ref):
        @pl.when(pl.program_id(0) == 0)
        def _():
            loss_ref[...] = jnp.zeros((1, 1), F32)

        f = _dot(ff_ref[...], wd_ref[...]).astype(BF)
        f_ref[...] = f
        h2 = h1_ref[...] + _rms(f.astype(F32), g4_ref[...])
        h2b = h2.astype(BF)
        h2_ref[...] = h2b
        pg = _dot(h2b, wpg_ref[...]).astype(BF)
        pe = _dot(p_ref[...].astype(BF), wpl_ref[...]).astype(BF)
        pg_ref[...] = pg
        pe_ref[...] = pe
        y = h2 + _rms(jax.nn.sigmoid(pg.astype(F32)) * pe.astype(F32), g5_ref[...])
        err = y - t_ref[...]
        dy_ref[...] = err * (1.0 / D)
        loss_ref[...] += (0.5 / D) * jnp.sum(err * err)

    row = lambda w: pl.BlockSpec((t, w), lambda i: (i, 0))
    return _pc(
        body, name="out_fwd", grid=(s // t,),
        in_specs=[row(D_FF), row(D), row(PLE), row(D), _const((D_FF, D)), _const((D, D)), _const((PLE, D)),
                  _const((1, D)), _const((1, D))],
        out_specs=[row(D)] * 5 + [_acc((1, 1))],
        out_shape=[SDS((s, D), BF)] * 4 + [SDS((s, D), F32), SDS((1, 1), F32)],
        compiler_params=_cp(1))(ff, h1, p, tgt, w_down, w_pg, w_ple, g4, g5)


def out_bwd(dy, pg, pe, f, g5, g4, w_pg, w_down, t):
    s = dy.shape[0]

    def body(dy_ref, pg_ref, pe_ref, f_ref, g5_ref, g4_ref, wpg_ref, wd_ref,
             dh2_ref, dpe_ref, dpg_ref, df_ref, dff_ref, gg5_ref, gg4_ref):
        @pl.when(pl.program_id(0) == 0)
        def _():
            gg5_ref[...] = jnp.zeros((1, D), F32)
            gg4_ref[...] = jnp.zeros((1, D), F32)

        dy_ = dy_ref[...]
        pg_ = pg_ref[...].astype(F32)
        pe_ = pe_ref[...].astype(F32)
        sg = jax.nn.sigmoid(pg_)
        dple, dg5 = _rms_bwd(dy_, sg * pe_, g5_ref[...])
        gg5_ref[...] += dg5
        dpe_ref[...] = (dple * sg).astype(BF)
        dpg = (dple * pe_ * sg * (1.0 - sg)).astype(BF)
        dpg_ref[...] = dpg
        dh2 = dy_ + _dot_nt(dpg, wpg_ref[...])
        dh2_ref[...] = dh2
        df, dg4 = _rms_bwd(dh2, f_ref[...].astype(F32), g4_ref[...])
        gg4_ref[...] += dg4
        dfb = df.astype(BF)
        df_ref[...] = dfb
        dff_ref[...] = _dot_nt(dfb, wd_ref[...]).astype(BF)

    row = lambda w: pl.BlockSpec((t, w), lambda i: (i, 0))
    return _pc(
        body, name="out_bwd", grid=(s // t,),
        in_specs=[row(D), row(D), row(D), row(D), _const((1, D)), _const((1, D)), _const((D, D)), _const((D_FF, D))],
        out_specs=[row(D), row(D), row(D), row(D), row(D_FF), _acc((1, D)), _acc((1, D))],
        out_shape=[SDS((s, D), F32), SDS((s, D), BF), SDS((s, D), BF), SDS((s, D), BF), SDS((s, D_FF), BF),
                   SDS((1, D), F32), SDS((1, D), F32)],
        compiler_params=_cp(1))(dy, pg, pe, f, g5, g4, w_pg, w_down)


def nt_normbwd(dys, w, xin, gain, dres, t, name, side=None):
    s = xin.shape[0]
    nt = s // t
    np_ = len(dys)

    def body(*refs):
        ins_, s_in, (dx_ref, gg_ref), s_out, _, s_scr = _split_side(refs, np_ + 4, 2, 0, side)
        dy_refs = ins_[:np_]
        w_ref, x_ref, g_ref, dres_ref = ins_[np_:]
        i = pl.program_id(0)

        @pl.when(i == 0)
        def _():
            gg_ref[...] = jnp.zeros((1, D), F32)
            if side:
                side.start(s_in, s_out, *s_scr)

        acc = _dot(jnp.concatenate([r_[...] for r_ in dy_refs], axis=1), w_ref[...])
        dxn, dg = _rms_bwd(acc, x_ref[...], g_ref[...])
        dx_ref[...] = dres_ref[...] + dxn
        gg_ref[...] += dg
        if side:
            @pl.when(i == nt - 1)
            def _():
                side.finish(s_in, s_out, *s_scr)

    row = lambda width: pl.BlockSpec((t, width), lambda i: (i, 0))
    si_specs, so_specs, so_shapes, s_scratch, s_ins = _side_specs(side)
    assert sum(dy.shape[1] for dy in dys) == w.shape[0]
    return _pc(
        body, name=name, grid=(nt,),
        in_specs=[row(dy.shape[1]) for dy in dys] + [_const(w.shape), row(D), _const((1, D)), row(D)] + si_specs,
        out_specs=[row(D), _acc((1, D))] + so_specs,
        out_shape=[SDS((s, D), F32), SDS((1, D), F32)] + so_shapes, scratch_shapes=s_scratch,
        compiler_params=_cp(1))(*dys, w, xin, gain, dres, *s_ins)


def mix_bwd(dh1, mx, z, ya, yb, g2, w_mix, a_out, b_out, t, side=None):
    s = dh1.shape[0]
    nt = s // t

    def body(*refs):
        ins_, s_in, outs_, s_out, _, s_scr = _split_side(refs, 10, 8, 0, side)
        dh_ref, mx_ref, ga_ref, gb_ref, ya_ref, yb_ref, g2_ref, wm_ref, ao_ref, bo_ref = ins_
        dmx_ref, dya_ref, dyb_ref, dga_ref, dgb_ref, dsa_ref, dob_ref, gg2_ref = outs_

        @pl.when(pl.program_id(0) == 0)
        def _():
            gg2_ref[...] = jnp.zeros((1, D), F32)
            if side:
                side.start(s_in, s_out, *s_scr)

        dmx, dg2 = _rms_bwd(dh_ref[...], mx_ref[...].astype(F32), g2_ref[...])
        gg2_ref[...] += dg2
        dmxb = dmx.astype(BF)
        dmx_ref[...] = dmxb
        dmp = _dot_nt(dmxb, wm_ref[...]).astype(BF)

        def gate(g_ref, y_ref, dy_ref, dg_ref, w_ref, dz_ref):
            sg = jax.nn.sigmoid(g_ref[...])
            dyb_ = dmp * sg
            dy_ref[...] = dyb_
            dg_ref[...] = dyb_ * y_ref[...] * (1.0 - sg)
            dz_ref[...] = _dot_nt(dyb_, w_ref[...]).astype(BF)

        gate(ga_ref, ya_ref, dya_ref, dga_ref, ao_ref, dsa_ref)
        gate(gb_ref, yb_ref, dyb_ref, dgb_ref, bo_ref, dob_ref)
        if side:
            @pl.when(pl.program_id(0) == nt - 1)
            def _():
                side.finish(s_in, s_out, *s_scr)

    row = lambda w: pl.BlockSpec((t, w), lambda i: (i, 0))
    si_specs, so_specs, so_shapes, s_scratch, s_ins = _side_specs(side)
    return _pc(
        body, name="mix_bwd", grid=(nt,),
        in_specs=[row(D), row(D), pl.BlockSpec((t, D), lambda i: (i, 4)), pl.BlockSpec((t, D), lambda i: (i, 5)),
                  row(D), row(D), _const((1, D)), _const((D, D)), _const((A_W, D)), _const((D, D))] + si_specs,
        out_specs=[row(D)] * 5 + [row(A_W), row(D), _acc((1, D))] + so_specs,
        out_shape=[SDS((s, D), BF)] * 5 + [SDS((s, A_W), BF), SDS((s, D), BF), SDS((1, D), F32)] + so_shapes,
        scratch_shapes=s_scratch,
        compiler_params=_cp(1))(dh1, mx, z, z, ya, yb, g2, w_mix, a_out, b_out, *s_ins)


def sgu_bwd(z, dsa, ln_g, ln_b, w_cat, w_cat_t, bias_full, bdm, t):
    s = z.shape[0]
    nt = s // t
    nch = t // A_C

    def body(u_ref, v_ref, dsa_ref, g_ref, b_ref, wc_ref, wct_ref, bias_ref, bdm_ref,
             duv_ref, glg_ref, glb_ref, gws_ref, gbs_ref, ds_acc):
        i = pl.program_id(0)

        @pl.when(i == 0)
        def _():
            glg_ref[...] = jnp.zeros((1, A_W), F32)
            glb_ref[...] = jnp.zeros((1, A_W), F32)
            gws_ref[...] = jnp.zeros((A_C, A_G * A_C), F32)
            ds_acc[...] = jnp.zeros((A_C, A_W), F32)

        lng, bdm_ = g_ref[...], bdm_ref[...]
        rec = [_sgu_recompute(v_ref[pl.ds(ci * A_C, A_C), :], lng, b_ref[...]) for ci in range(nch)]
        spread_vn = _sgu_spread([r_[3] for r_ in rec], bdm_)
        mixed = _dot(_sgu_weights(wc_ref, False), spread_vn)
        dsas, dss, dgus = [], [], []
        for ci in range(nch):
            rows = pl.ds(ci * A_C, A_C)
            gu, dgu = _gelu_and_grad(u_ref[rows, :])
            dsa_ = dsa_ref[rows, :]
            ds = dsa_ * gu
            ds_acc[...] += ds.astype(F32)
            dsas.append(dsa_)
            dgus.append(dgu)
            dss.append(ds)
        r = lax.broadcasted_iota(jnp.int32, (A_C, A_G * A_C), 0)
        c = lax.broadcasted_iota(jnp.int32, (A_C, A_G * A_C), 1) & (A_C - 1)
        gws_ref[...] += jnp.where(c <= r, _dot_nt(jnp.concatenate(dss, axis=1), spread_vn), 0.0)
        dvns = _dot(_sgu_weights(wct_ref, True), _sgu_spread(dss, bdm_))
        for ci in range(nch):
            rows = pl.ds(ci * A_C, A_C)
            dgv, rstd, xhat, _ = rec[ci]
            dvn = dvns[:, ci * A_W:(ci + 1) * A_W]
            glb_ref[...] += jnp.sum(dvn, axis=0, keepdims=True)
            glg_ref[...] += jnp.sum(dvn * xhat, axis=0, keepdims=True)
            dxh = dvn * lng
            dgv_ = rstd * (dxh - jnp.mean(dxh, axis=-1, keepdims=True)
                           - xhat * jnp.mean(dxh * xhat, axis=-1, keepdims=True))
            s_ = mixed[:, ci * A_W:(ci + 1) * A_W] + bias_ref[...]
            duv_ref[rows, :A_W] = dsas[ci] * dgus[ci] * s_.astype(BF)
            duv_ref[rows, A_W:] = (dgv_ * dgv).astype(BF)

        @pl.when(i == nt - 1)
        def _():
            acc = ds_acc[...]
            for g in range(A_G):
                gbs_ref[:, g:g + 1] = jnp.sum(acc[:, g * A_GD:(g + 1) * A_GD], axis=1, keepdims=True)

    return _pc(
        body, name="sgu_bwd", grid=(nt,),
        in_specs=[pl.BlockSpec((t, A_W), lambda i: (i, 0)), pl.BlockSpec((t, A_W), lambda i: (i, 1)),
                  pl.BlockSpec((t, A_W), lambda i: (i, 0)),
                  _const((1, A_W)), _const((1, A_W)), _const((A_C, A_G * A_C)), _const((A_C, A_G * A_C)),
                  _const((A_C, A_W)), _const((A_G * A_C, A_W))],
        out_specs=[pl.BlockSpec((t, D), lambda i: (i, 0)), _acc((1, A_W)), _acc((1, A_W)),
                   _acc((A_C, A_G * A_C)), _acc((A_C, A_G))],
        out_shape=[SDS((s, D), BF), SDS((1, A_W), F32), SDS((1, A_W), F32), SDS((A_C, A_G * A_C), F32),
                   SDS((A_C, A_G), F32)],
        scratch_shapes=[pltpu.VMEM((A_C, A_W), F32)],
        compiler_params=_cp(1))(z, z, dsa, ln_g, ln_b, w_cat, w_cat_t, bias_full, bdm)


def gla_bwd(z, qk32, zl, o, dob, states, wgk, bias, wn, ltri, ltri_t, t, side=None):
    s = z.shape[0]
    nt = s // t
    nc = t // B_C
    hpb = GLA_HPB
    assert hpb == B_H
    kw, vw = hpb * B_HK, hpb * B_HV

    def body(*refs):
        ins_, s_in, outs_, s_out, scr_, s_scr = _split_side(refs, 12, 7, 5, side)
        qk_ref, v_ref, og_ref, lr_ref, o_ref, dob_ref, st_ref, wgk_ref, bias_ref, wn_ref, l_ref, lt_ref = ins_
        dqk_ref, dv_ref, dog_ref, dpre_ref, dlr_ref, gbias_ref, gwn_ref = outs_
        dst_sc, dv_sc, dqd_sc, dkt_sc, ddec_sc = scr_
        i = pl.program_id(0)
        g = pl.program_id(1)

        @pl.when((i == 0) & (g == 0))
        def _():
            gbias_ref[...] = jnp.zeros((B_H, 1, B_HK), F32)
            gwn_ref[...] = jnp.zeros((1, B_HV), F32)
            if side:
                side.start(s_in, s_out, *s_scr)

        @pl.when(i == 0)
        def _():
            dst_sc[...] = jnp.zeros((B_H, B_HV, B_HK), F32)

        lr, l, lt = lr_ref[...], l_ref[...], lt_ref[...]
        keep, keep_t = l > 0, lt > 0
        wn_ = wn_ref[...]
        last = lax.broadcasted_iota(jnp.int32, (nc, B_C, B_HK), 1) == B_C - 1
        for hh in range(hpb):
            h = hh
            cv, ck = slice(hh * B_HV, (hh + 1) * B_HV), slice(hh * B_HK, (hh + 1) * B_HK)
            pre, b, bl, eb, enb, etb, qd, ki, kt = _gla_decays(qk_ref[:, cv], lr, wgk_ref[:, ck], bias_ref[:, ck], l, t)
            qdb, kib, ktb = qd.astype(BF), ki.astype(BF), kt.astype(BF)
            vb = v_ref[:, cv]
            o_ = o_ref[:, cv].astype(F32)
            og = og_ref[:, cv].astype(F32)
            sog = jax.nn.sigmoid(og)
            dob_ = dob_ref[:, cv].astype(F32)
            don = dob_ * og * sog
            do, dwn = _rms_bwd(don, o_, wn_)
            gwn_ref[...] += dwn
            dog_ref[:, cv] = (dob_ * _rms(o_, wn_) * sog * (1.0 + og * (1.0 - sog))).astype(BF)
            dob16 = do.astype(BF)
            sc_t = jnp.where(keep_t, _dot_nt(kib, qdb), 0.0).astype(BF)
            dsc = jnp.where(keep, _dot_nt(dob16, vb), 0.0).astype(BF)
            dsc_t = jnp.where(keep_t, _dot_nt(vb, dob16), 0.0).astype(BF)
            dv_sc[hh] = _dot(sc_t, dob16)
            dqd_sc[hh] = _dot(dsc, kib)
            dki = _dot(dsc_t, qdb)
            for n in reversed(range(nc)):
                rows = slice(n * B_C, (n + 1) * B_C)
                dst = dst_sc[h]
                dstb = dst.astype(BF)
                stp = st_ref[n, hh]
                dv_sc[hh, rows, :] += _dot_nt(ktb[rows], dstb)
                dkt_sc[hh, rows, :] = _dot(vb[rows], dstb)
                dqd_sc[hh, rows, :] += _dot(dob16[rows], stp)
                dec = jnp.exp(bl[n * B_C:n * B_C + 1, :])
                ddec_sc[hh, n] = jnp.sum(dst * stp.astype(F32), axis=0, keepdims=True) * dec
                dst_sc[h] = dst * dec + _dot_tn(dob16[rows], qdb[rows])
            dqd, dkt = dqd_sc[hh], dkt_sc[hh]
            dv_ref[:, cv] = dv_sc[hh].astype(BF)
            dqk_ref[:, hh * B_HV:hh * B_HV + B_HK] = (dqd * eb * (B_HK ** -0.5)).astype(BF)
            dqk_ref[:, hh * B_HV + B_HK:(hh + 1) * B_HV] = (dki * enb + dkt * etb).astype(BF)
            dktkt = dkt * kt
            db3 = (dqd * qd - dki * ki - dktkt).reshape(nc, B_C, B_HK)
            dbl = jnp.sum(dktkt.reshape(nc, B_C, B_HK), axis=1, keepdims=True) + ddec_sc[hh]
            db = (db3 + jnp.where(last, dbl, 0.0)).reshape(t, B_HK)
            dla = _ldot3(lt, db)
            dpre = dla * (1.0 / 16.0) * (1.0 - jax.nn.sigmoid(pre))
            dpreb = dpre.astype(BF)
            dpre_ref[:, ck] = dpreb
            gbias_ref[h] += jnp.sum(dpre, axis=0, keepdims=True)
            dlr_h = _dot_nt(dpreb, wgk_ref[:, ck])
            dlr = dlr_h if hh == 0 else dlr + dlr_h
        dlr_ref[...] = dlr.astype(BF)
        if side:
            @pl.when((i == nt - 1) & (g == B_H // hpb - 1))
            def _():
                side.finish(s_in, s_out, *s_scr)

    rv = lambda i: nt - 1 - i
    si_specs, so_specs, so_shapes, s_scratch, s_ins = _side_specs(side)
    vo, go = 2048 // vw, 3072 // vw
    tile = lambda off: pl.BlockSpec((t, vw), lambda i, g: (rv(i), off + g))
    return _pc(
        body, name="gla_bwd", grid=(nt, B_H // hpb),
        in_specs=[tile(0), tile(vo), tile(go), pl.BlockSpec((t, LANE), lambda i, g: (rv(i), 0)), tile(0), tile(0),
                  pl.BlockSpec((nc, hpb, B_HV, B_HK), lambda i, g: (rv(i), g, 0, 0)),
                  pl.BlockSpec((LANE, kw), lambda i, g: (0, g)), pl.BlockSpec((1, kw), lambda i, g: (0, g)),
                  _const((1, B_HV)), _const((t, t)), _const((t, t))] + si_specs,
        out_specs=[tile(0), tile(0), tile(0), pl.BlockSpec((t, kw), lambda i, g: (rv(i), g)),
                   pl.BlockSpec((t, LANE), lambda i, g: (rv(i), 0)), _acc((B_H, 1, B_HK)), _acc((1, B_HV))] + so_specs,
        out_shape=[SDS((s, D), BF), SDS((s, D), BF), SDS((s, D), BF), SDS((s, B_H * B_HK), BF), SDS((s, LANE), BF),
                   SDS((B_H, 1, B_HK), F32), SDS((1, B_HV), F32)] + so_shapes,
        scratch_shapes=[pltpu.VMEM((B_H, B_HV, B_HK), F32), pltpu.VMEM((hpb, t, B_HV), F32),
                        pltpu.VMEM((hpb, t, B_HK), F32), pltpu.VMEM((hpb, t, B_HK), F32),
                        pltpu.VMEM((hpb, nc, 1, B_HK), F32)] + s_scratch,
        compiler_params=_cp(2))(qk32, z, z, zl, o, dob, states, wgk, bias, wn, ltri, ltri_t, *s_ins)


def mm_tn(a, b, name, tk=2048):
    s, m = a.shape
    n = b.shape[1]
    bn = next(c for c in (1024, 1408, 512, 256, 128) if n % c == 0 and m * c * 4 <= 6 * 1024 * 1024)
    tk = min(tk, s)
    nk = s // tk

    def body(a_ref, b_ref, o_ref, acc):
        k = pl.program_id(1)

        @pl.when(k == 0)
        def _():
            acc[...] = jnp.zeros((m, bn), F32)

        acc[...] += _dot_tn(a_ref[...].astype(BF), b_ref[...])

        @pl.when(k == nk - 1)
        def _():
            o_ref[...] = acc[...].astype(BF)

    return _pc(
        body, name=name, grid=(n // bn, nk),
        in_specs=[pl.BlockSpec((tk, m), lambda j, k: (k, 0)), pl.BlockSpec((tk, bn), lambda j, k: (k, j))],
        out_specs=pl.BlockSpec((m, bn), lambda j, k: (0, j)),
        out_shape=SDS((m, n), BF), scratch_shapes=[pltpu.VMEM((m, bn), F32)], compiler_params=_cp(2))(a, b)


def _adamw(w, g, m, v):
    m = ADAM_B1 * m + (1.0 - ADAM_B1) * g
    v = ADAM_B2 * v + (1.0 - ADAM_B2) * (g * g)
    m_hat = m / (1.0 - ADAM_B1 ** ADAM_STEP)
    v_hat = v / (1.0 - ADAM_B2 ** ADAM_STEP)
    return -ADAM_LR * (m_hat / (jnp.sqrt(v_hat) + ADAM_EPS) + ADAM_WD * w), m, v


def _half_rows(rows):
    rh = rows // 2
    return rh, max(b for b in range(16, 257, 16) if rh % b == 0)


def _pc_sp(body, grid, in_specs, out_specs, out_shape, name):
    gs = pltpu.PrefetchScalarGridSpec(num_scalar_prefetch=1, grid=grid, in_specs=in_specs, out_specs=out_specs)
    return _pc(body, grid_spec=gs, out_shape=out_shape, name=name, compiler_params=_cp(len(grid)))


def adamw_halves(sc, own, sib, w, m, v, name):
    rows, cols = w.shape
    rh, br = _half_rows(rows)
    nbk = rh // br

    def body(sc_ref, own_ref, sib_ref, w_ref, m_ref, v_ref, go_ref, d_ref, mo_ref, vo_ref):
        g_ = jnp.where(pl.program_id(0) // nbk == sc_ref[0], own_ref[...], sib_ref[...])
        go_ref[...] = g_
        d_ref[...], mo_ref[...], vo_ref[...] = _adamw(w_ref[...], g_, m_ref[...], v_ref[...])

    half = pl.BlockSpec((br, cols), lambda i, sc_: (i % nbk, 0))
    blk = pl.BlockSpec((br, cols), lambda i, sc_: (i, 0))
    return _pc_sp(body, (2 * nbk,), [half, half, blk, blk, blk], [blk] * 4, [SDS((rows, cols), F32)] * 4,
                  name)(sc, own, sib, w, m, v)


def adamw_cols(sc, own, sib, w, m, v, name, cb=256):
    rows, cols = w.shape
    nk = cols // 2 // cb

    def body(sc_ref, own_ref, sib_ref, w_ref, m_ref, v_ref, go_ref, d_ref, mo_ref, vo_ref):
        g_ = jnp.where(pl.program_id(0) == sc_ref[0], own_ref[...], sib_ref[...])
        go_ref[...] = g_
        d_ref[...], mo_ref[...], vo_ref[...] = _adamw(w_ref[...], g_, m_ref[...], v_ref[...])

    half = pl.BlockSpec((rows, cb), lambda h, k, sc_: (0, k))
    blk = pl.BlockSpec((rows, cb), lambda h, k, sc_: (0, h * nk + k))
    return _pc_sp(body, (2, nk), [half, half, blk, blk, blk], [blk] * 4, [SDS((rows, cols), F32)] * 4,
                  name)(sc, own, sib, w, m, v)


def adamw_small(g, w, m, v):
    def body(g_ref, w_ref, m_ref, v_ref, d_ref, mo_ref, vo_ref):
        d_ref[...], mo_ref[...], vo_ref[...] = _adamw(w_ref[...], g_ref[...], m_ref[...], v_ref[...])

    vm = pl.BlockSpec(memory_space=pltpu.VMEM)
    return _pc(body, name="adamw_small", in_specs=[vm] * 4, out_specs=[vm] * 3, out_shape=[SDS(g.shape, F32)] * 3,
               compiler_params=pltpu.CompilerParams(vmem_limit_bytes=VMEM_LIMIT))(g, w, m, v)


def _pos():
    return lax.axis_index("x"), lax.axis_index("y"), lax.axis_index("c")


def _other_chips(x, y):
    return [(1 - x, y), (x, 1 - y), (1 - x, 1 - y)]


_ANY = pl.BlockSpec(memory_space=pltpu.HBM)


class _Side:
    def __init__(self, ins, out_shapes, nsem, start, finish):
        self.ins, self.out_shapes, self.start, self.finish = list(ins), list(out_shapes), start, finish
        self.scratch = [pltpu.SemaphoreType.DMA((nsem,)), pltpu.SemaphoreType.DMA((nsem,))]
        self.n_in, self.n_out = len(self.ins), len(self.out_shapes)


def _run_side(side, name):
    def body(*refs):
        args_ = (refs[:side.n_in], refs[side.n_in:side.n_in + side.n_out], *refs[side.n_in + side.n_out:])
        side.start(*args_)
        side.finish(*args_)

    return _pc(body, name=name, in_specs=[_ANY] * side.n_in, out_specs=[_ANY] * side.n_out,
               out_shape=side.out_shapes, scratch_shapes=side.scratch)(*side.ins)


def _split_rows(shape):
    return (shape[0] // 2) % 16 == 0


def _core_halves(shape, c):
    if _split_rows(shape):
        h = shape[0] // 2
        return ((pl.ds(pl.multiple_of(c * h, 16), h), slice(None)),
                (pl.ds(pl.multiple_of((1 - c) * h, 16), h), slice(None)))
    h = shape[1] // 2
    assert h % LANE == 0
    return ((slice(None), pl.ds(pl.multiple_of(c * h, LANE), h)),
            (slice(None), pl.ds(pl.multiple_of((1 - c) * h, LANE), h)))


def gather_side(bigs, tinies):
    nb, nt_ = len(bigs), len(tinies)

    def plan(ins, outs, ssem, rsem):
        x, y, c = _pos()
        me = 2 * x + y
        chips = _other_chips(x, y)
        sibling = (x, y, 1 - c)

        def copy(k, src, dst, to):
            return pltpu.make_async_remote_copy(src_ref=src, dst_ref=dst, send_sem=ssem.at[k], recv_sem=rsem.at[k],
                                                device_id=to, device_id_type=MESH)

        sends, landed, passed_on, tiny_landed = [], [], [], []
        for w in range(nb):
            mine, theirs = _core_halves(bigs[w].shape, c)
            for j, (cx, cy) in enumerate(chips):
                sends.append(copy(6 * w + j, ins[w].at[mine], outs[w].at[(me,) + mine], (cx, cy, c)))
                blk = outs[w].at[(2 * cx + cy,) + mine]
                landed.append((copy(6 * w + j, blk, blk, (cx, cy, c)), copy(6 * w + 3 + j, blk, blk, sibling)))
                blk = outs[w].at[(2 * cx + cy,) + theirs]
                passed_on.append(copy(6 * w + 3 + j, blk, blk, sibling))
        for w in range(nt_):
            for j, (cx, cy) in enumerate(chips):
                k = 6 * nb + 3 * w + j
                sends.append(copy(k, ins[nb + w], outs[nb + w].at[me], (cx, cy, c)))
                blk = outs[nb + w].at[2 * cx + cy]
                tiny_landed.append(copy(k, blk, blk, (cx, cy, c)))
        return sends, landed, passed_on, tiny_landed

    def start(ins, outs, ssem, rsem):
        for cp in plan(ins, outs, ssem, rsem)[0]:
            cp.start()

    def finish(ins, outs, ssem, rsem):
        sends, landed, passed_on, tiny_landed = plan(ins, outs, ssem, rsem)
        for arrived, forward in landed:
            arrived.wait_recv()
            forward.start()
        for arrived in tiny_landed + passed_on:
            arrived.wait_recv()
        for cp in sends + [forward for _, forward in landed]:
            cp.wait_send()

    return _Side(list(bigs) + list(tinies), [SDS((4,) + a.shape, a.dtype) for a in list(bigs) + list(tinies)],
                 6 * nb + 3 * nt_, start, finish)


def _sibling_side(srcs, out_shapes, pick):
    def plan(in_refs, out_refs, ssem, rsem):
        x, y, c = _pos()
        return [pltpu.make_async_remote_copy(src_ref=pick(in_refs[w], srcs[w].shape, c), dst_ref=out_refs[w],
                                             send_sem=ssem.at[w], recv_sem=rsem.at[w], device_id=(x, y, 1 - c),
                                             device_id_type=MESH) for w in range(len(srcs))]

    def start(*refs):
        for cp in plan(*refs):
            cp.start()

    def finish(*refs):
        for cp in plan(*refs):
            cp.wait()

    return _Side(srcs, out_shapes, len(srcs), start, finish)


def swap_side(gs):
    def half_shape(g):
        l, r, cols = g.shape
        return (l, r // 2, cols) if _split_rows((r, cols)) else (l, r, cols // 2)

    return _sibling_side(gs, [SDS(half_shape(g), g.dtype) for g in gs],
                         lambda ref, shape, c: ref.at[(slice(None),) + _core_halves(shape[1:], c)[1]])


def join_side(halves):
    return _sibling_side(halves, [SDS(h.shape, h.dtype) for h in halves], lambda ref, shape, c: ref)


def _both_sides(a, b):
    def split(ins, outs, *scr):
        return ((ins[:a.n_in], outs[:a.n_out], *scr[:2]), (ins[a.n_in:], outs[a.n_out:], *scr[2:]))

    def start(*refs):
        ra, rb = split(*refs)
        a.start(*ra)
        b.start(*rb)

    def finish(*refs):
        ra, rb = split(*refs)
        a.finish(*ra)
        b.finish(*rb)

    side = _Side(a.ins + b.ins, a.out_shapes + b.out_shapes, 1, start, finish)
    side.scratch = a.scratch + b.scratch
    return side


COL_BLOCK = 256


def add_half(sc, g, sib, name):
    l, r, cols = g.shape

    def body(sc_ref, g_ref, s_ref, o_ref):
        o_ref[...] = (g_ref[...].astype(F32) + s_ref[...].astype(F32)).astype(BF)

    if _split_rows((r, cols)):
        rh, br = _half_rows(r)
        nbk = rh // br
        blk = pl.BlockSpec((1, br, cols), lambda j, i, sc_: (j, i, 0))
        mine = pl.BlockSpec((1, br, cols), lambda j, i, sc_: (j, sc_[0] * nbk + i, 0))
        return _pc_sp(body, (l, nbk), [mine, blk], blk, SDS((l, rh, cols), BF), name)(sc, g, sib)
    nbk = cols // 2 // COL_BLOCK
    blk = pl.BlockSpec((1, r, COL_BLOCK), lambda j, i, sc_: (j, 0, i))
    mine = pl.BlockSpec((1, r, COL_BLOCK), lambda j, i, sc_: (j, 0, sc_[0] * nbk + i))
    return _pc_sp(body, (l, nbk), [mine, blk], blk, SDS((l, r, cols // 2), BF), name)(sc, g, sib)


def exchange_side(ps):
    n_ = len(ps)

    def width(p_):
        return p_.shape[2] if p_.shape[0] == 4 else p_.shape[2] // 4

    def plan(p_refs, got_refs, ssem, rsem):
        x, y, c = _pos()
        cps = []
        for w in range(n_):
            wd = width(ps[w])
            for j, (cx, cy) in enumerate(_other_chips(x, y)):
                to = 2 * cx + cy
                src = p_refs[w].at[to] if ps[w].shape[0] == 4 else p_refs[w].at[0, :, pl.ds(pl.multiple_of(to * wd, LANE), wd)]
                cps.append(pltpu.make_async_remote_copy(
                    src_ref=src, dst_ref=got_refs[w].at[j], send_sem=ssem.at[3 * w + j], recv_sem=rsem.at[3 * w + j],
                    device_id=(cx, cy, c), device_id_type=MESH))
        return cps

    def start(*refs):
        for cp in plan(*refs):
            cp.start()

    def finish(*refs):
        for cp in plan(*refs):
            cp.wait()

    return _Side(ps, [SDS((3, p_.shape[1], width(p_)), p_.dtype) for p_ in ps], 3 * n_, start, finish)


def sum4(sc, p, got, name):
    _, rh, wd = got.shape

    def body(sc_ref, p_ref, g_ref, r_ref):
        r_ref[...] = ((p_ref[0].astype(F32) + g_ref[0].astype(F32)) + (g_ref[1].astype(F32) + g_ref[2].astype(F32)))

    if rh % 16:
        assert p.shape[0] == 4
        return _pc_sp(body, (wd // COL_BLOCK,),
                      [pl.BlockSpec((1, rh, COL_BLOCK), lambda i, sc_: (sc_[1], 0, i)),
                       pl.BlockSpec((3, rh, COL_BLOCK), lambda i, sc_: (0, 0, i))],
                      pl.BlockSpec((rh, COL_BLOCK), lambda i, sc_: (0, i)), SDS((rh, wd), F32), name)(sc, p, got)
    _, br = _half_rows(2 * rh)
    own = (pl.BlockSpec((1, br, wd), lambda i, sc_: (sc_[1], i, 0)) if p.shape[0] == 4
           else pl.BlockSpec((1, br, wd), lambda i, sc_: (0, i, sc_[1])))
    return _pc_sp(body, (rh // br,), [own, pl.BlockSpec((3, br, wd), lambda i, sc_: (0, i, 0))],
                  pl.BlockSpec((br, wd), lambda i, sc_: (i, 0)), SDS((rh, wd), F32), name)(sc, p, got)


def allreduce_small(g):
    rows = g.shape[0]
    rh = rows // 2

    def body(g_ref, out_ref, sib_buf, chip_buf, sum_sc, ssem, rsem):
        x, y, c = _pos()
        me = 2 * x + y
        sibling = (x, y, 1 - c)
        mine = pl.ds(pl.multiple_of(c * rh, 8), rh)

        def copy(k, src, dst, to):
            return pltpu.make_async_remote_copy(src_ref=src, dst_ref=dst, send_sem=ssem.at[k], recv_sem=rsem.at[k],
                                                device_id=to, device_id_type=MESH)

        cp = copy(0, g_ref, sib_buf, sibling)
        cp.start()
        cp.wait()
        sum_sc[...] = g_ref[...] + sib_buf[...]
        chips = _other_chips(x, y)
        cps = [copy(1 + j, sum_sc.at[mine], chip_buf.at[me], (cx, cy, c)) for j, (cx, cy) in enumerate(chips)]
        for cp in cps:
            cp.start()
        chip_buf[me] = sum_sc[mine, :]
        for j, (cx, cy) in enumerate(chips):
            copy(1 + j, sum_sc.at[mine], chip_buf.at[2 * cx + cy], (cx, cy, c)).wait_recv()
        for cp in cps:
            cp.wait_send()
        out_ref[mine, :] = (chip_buf[0] + chip_buf[1]) + (chip_buf[2] + chip_buf[3])
        cp = copy(4, out_ref.at[mine], out_ref.at[mine], sibling)
        cp.start()
        cp.wait()

    vm = pl.BlockSpec(memory_space=pltpu.VMEM)
    return _pc(body, name="allreduce_small", in_specs=[vm], out_specs=vm, out_shape=SDS((rows, LANE), F32),
               scratch_shapes=[pltpu.VMEM((rows, LANE), F32), pltpu.VMEM((4, rh, LANE), F32), pltpu.VMEM((rows, LANE), F32),
                               pltpu.SemaphoreType.DMA((5,)), pltpu.SemaphoreType.DMA((5,))],
               compiler_params=pltpu.CompilerParams(vmem_limit_bytes=VMEM_LIMIT))(g)


def _pack_small(entries, get):
    flat = jnp.concatenate([get(n).reshape(-1).astype(F32) for n, _ in entries])
    rows = -(-flat.shape[0] // (8 * LANE)) * 8
    return jnp.pad(flat, (0, rows * LANE - flat.shape[0])).reshape(rows, LANE)


def _unpack_small(entries, packed):
    out, off = {}, 0
    flat = packed.reshape(-1)
    for name, n in entries:
        out[name] = flat[off:off + n]
        off += n
    return out


def _cols_full(blk):
    return blk.transpose(1, 0, 2).reshape(blk.shape[1], 4 * blk.shape[2])


def kernel(x, p, pre_mix_norm, w_in, a_ln_g, a_ln_b, a_spatial_w, a_spatial_b, a_out, b_gk, b_gk_bias, b_out_norm, b_out, w_mix_out, post_mix_norm, pre_ffn_norm, w_up, conv_w, conv_b, w_down, post_ffn_norm, w_ple, w_ple_gate, post_ple_norm, loss_target, m_pre_mix_norm, m_w_in, m_a_ln_g, m_a_ln_b, m_a_spatial_w, m_a_spatial_b, m_a_out, m_b_gk, m_b_gk_bias, m_b_out_norm, m_b_out, m_w_mix_out, m_post_mix_norm, m_pre_ffn_norm, m_w_up, m_conv_w, m_conv_b, m_w_down, m_post_ffn_norm, m_w_ple, m_w_ple_gate, m_post_ple_norm, v_pre_mix_norm, v_w_in, v_a_ln_g, v_a_ln_b, v_a_spatial_w, v_a_spatial_b, v_a_out, v_b_gk, v_b_gk_bias, v_b_out_norm, v_b_out, v_w_mix_out, v_post_mix_norm, v_pre_ffn_norm, v_w_up, v_conv_w, v_conv_b, v_w_down, v_post_ffn_norm, v_w_ple, v_w_ple_gate, v_post_ple_norm):
    args = dict(locals())
    order = ['pre_mix_norm', 'w_in', 'a_ln_g', 'a_ln_b', 'a_spatial_w', 'a_spatial_b', 'a_out', 'b_gk', 'b_gk_bias',
             'b_out_norm', 'b_out', 'w_mix_out', 'post_mix_norm', 'pre_ffn_norm', 'w_up', 'conv_w', 'conv_b', 'w_down',
             'post_ffn_norm', 'w_ple', 'w_ple_gate', 'post_ple_norm']
    assert sorted(BIG + TINY + tuple(n for n, _ in SMALL)) == sorted(order)
    s = x.shape[1]
    xs = x.reshape(s, D)
    ps = p.reshape(s, PLE)
    tgt = loss_target.reshape(s, D)
    t_big = min(1024, s)
    t_mid = min(512, s)
    t_small = min(256, s)
    t_gla = min(256, s)
    mx_, my_, mc_ = _pos()
    me = 2 * mx_ + my_
    sc = jnp.stack([mc_, me]).astype(jnp.int32)
    shard = lambda n: args[n].reshape(args[n].shape[1:])

    mine = {n: shard(n).astype(BF) for n in BIG}
    mine["w_in"] = shard("w_in").T.astype(BF)
    mine.update({n: shard(n) for n in TINY})
    fill = lambda names, gots: {n: lax.dynamic_update_slice(got, mine[n][None], (me, 0, 0)) for n, got in zip(names, gots)}
    first = ("w_in",) + TINY
    full = fill(first, _run_side(gather_side([mine["w_in"]], [mine[n] for n in TINY]), "gather_first"))
    wi = full["w_in"].reshape(4 * 1540, D)
    seg = lambda a, b: wi[a:b]
    qk = [seg(1024 + h * B_HK, 1024 + (h + 1) * B_HK) for h in range(B_H)]
    kk = [seg(1536 + h * B_HK, 1536 + (h + 1) * B_HK) for h in range(B_H)]
    w_z = jnp.concatenate([seg(0, 1024)] + [m_ for h in range(B_H) for m_ in (qk[h], kk[h])]
                          + [seg(2048, 4096), seg(4112, 6160), seg(4096, 4112), jnp.zeros((LANE - B_RANK, D), BF)], axis=0)
    wgk = jnp.pad(_cols_full(full["b_gk"]).astype(BF), ((0, LANE - B_RANK), (0, 0)))
    w_conv = _cols_full(full["conv_w"])
    g1, g2, g3 = pre_mix_norm.reshape(1, D), post_mix_norm.reshape(1, D), pre_ffn_norm.reshape(1, D)
    g4, g5 = post_ffn_norm.reshape(1, D), post_ple_norm.reshape(1, D)
    ln_g, ln_b = a_ln_g.reshape(1, A_W), a_ln_b.reshape(1, A_W)
    w_s = a_spatial_w.reshape(A_G, A_C, A_C)
    w_cat = w_s.transpose(1, 0, 2).reshape(A_C, A_G * A_C)
    w_cat_t = w_s.transpose(2, 0, 1).reshape(A_C, A_G * A_C)
    bias_full = jnp.repeat(a_spatial_b.reshape(A_G, A_C).T, A_GD, axis=1)
    bdm = (jnp.arange(A_G * A_C)[:, None] // A_C == jnp.arange(A_W)[None, :] // A_GD).astype(BF)
    gk_bias = b_gk_bias.reshape(1, B_H * B_HK)
    wn = b_out_norm.reshape(1, B_HV)
    cb = conv_b.reshape(1, 2 * D_FF)
    idx = jnp.arange(t_gla)
    ltri = ((idx[:, None] // B_C == idx[None, :] // B_C) & (idx[None, :] <= idx[:, None])).astype(BF)

    a, z, qk32, zl, *gots = norm_matmul(xs, g1, w_z, D, t_big, "in_proj", nblk=6, f32_blk=1, tail_blk=48,
                                        side=gather_side([mine[n] for n in BIG[1:]], []))
    full.update(fill(BIG[1:], gots))
    w_aout, w_ple_f = _cols_full(full["a_out"]), _cols_full(full["w_ple"])
    w_bout, w_mix, w_pg = (full[n].reshape(D, D) for n in ("b_out", "w_mix_out", "w_ple_gate"))
    w_dn, w_up3 = full["w_down"].reshape(D_FF, D), full["w_up"]
    sa = sgu_fwd(z, ln_g, ln_b, w_cat, bias_full, bdm, t_mid)
    ob, o, states = gla_fwd(z, qk32, zl, wgk, gk_bias, wn, ltri, t_gla)
    ya, yb, mp, mx, h1 = mix_fwd(sa, ob, z, xs, w_aout, w_bout, w_mix, g2, t_mid)
    c, up_g, up_v, cg, cv, ff = ffn_up_fwd(h1, g3, w_up3, w_conv, cb, t_mid)
    f, h2, pg, pe, dy, loss = out_fwd(ff, h1, ps, tgt, w_dn, w_pg, w_ple_f, g4, g5, t_mid)

    dh2, dpe, dpg, df, dff, gg5, gg4 = out_bwd(dy, pg, pe, f, g5, g4, w_pg, w_dn, t_mid)
    dup, gcw, gcb, dh1, gg3 = ffn_up_bwd(up_g, up_v, cg, cv, dff, w_conv, _cols_full(w_up3), h1, g3, dh2, t_small)
    grads = {
        "w_up": mm_tn(c, dup, "dw_up")[None],
        "w_down": mm_tn(ff, df, "dw_down").reshape(4, D_FF // 4, D),
        "w_ple": mm_tn(ps, dpe, "dw_ple")[None],
        "w_ple_gate": mm_tn(h2, dpg, "dw_ple_gate").reshape(4, D // 4, D),
    }
    ffn_side = ("w_up", "w_down", "w_ple", "w_ple_gate")
    dmx, dya, dyb, dga, dgb, dsa, dob, gg2, *sibs = mix_bwd(dh1, mx, z, ya, yb, g2, w_mix, w_aout, w_bout, t_mid,
                                                            side=swap_side([grads[n] for n in ffn_side]))
    sib = dict(zip(ffn_side, sibs))
    duv, g_lng, g_lnb, g_wcat, g_bst = sgu_bwd(z, dsa, ln_g, ln_b, w_cat, w_cat_t, bias_full, bdm, t_mid)
    g_ws = g_wcat.reshape(A_C, A_G, A_C).transpose(1, 0, 2)
    grads.update({
        "a_out": mm_tn(sa, dya, "dw_a_out")[None],
        "b_out": mm_tn(ob, dyb, "dw_b_out").reshape(4, D // 4, D),
        "w_mix_out": mm_tn(mp, dmx, "dw_mix").reshape(4, D // 4, D),
    })

    def swap(names):
        sib.update(zip(names, _run_side(swap_side([grads[n] for n in names]), "swap_halves_" + names[0])))

    swap(("a_out", "b_out", "w_mix_out"))
    parts = {n: add_half(sc, grads[n], sib[n], "partial_" + n) for n in BIG[1:]}
    dqk, dvb, dog, dpre, dlr, g_gkb, g_wn, *gots = gla_bwd(
        z, qk32, zl, o, dob, states, wgk, gk_bias, wn, ltri, ltri.T, t_gla,
        side=exchange_side([parts[n] for n in BIG[1:]]))
    reds = {n: sum4(sc, parts[n], got_, "sum_" + n) for n, got_ in zip(BIG[1:], gots)}
    segs = [duv, dqk, dvb, dog, dga, dgb, dlr]

    gz = [mm_tn(sg_, a, "dw_in_%d" % k) for k, sg_ in enumerate(segs)]
    gq = [gz[1][h * 256:h * 256 + B_HK] for h in range(B_H)]
    gk = [gz[1][h * 256 + B_HK:(h + 1) * 256] for h in range(B_H)]
    g_in = jnp.concatenate([gz[0]] + gq + gk + [gz[2], gz[3], gz[6][:B_RANK], gz[4], gz[5]], axis=0)
    grads["w_in"] = g_in.reshape(4, 1540, D)
    swap(("w_in",))
    parts["w_in"] = add_half(sc, grads["w_in"], sib["w_in"], "partial_w_in")
    dx, gg1, got_in, *sib_reds = nt_normbwd(
        segs, w_z, xs, g1, dh1, t_small, "in_bwd",
        side=_both_sides(exchange_side([parts["w_in"]]), join_side([reds[n] for n in BIG[1:]])))
    sib_red = dict(zip(BIG[1:], sib_reds))

    reds["w_in"] = sum4(sc, parts["w_in"], got_in, "sum_w_in")
    sib_red["w_in"], = _run_side(join_side([reds["w_in"]]), "join_w_in")
    outs = {}
    for n in BIG:
        if n == "w_in":
            res = adamw_cols(sc, reds[n], sib_red[n], shard(n).T, shard("m_" + n).T, shard("v_" + n).T, "adamw_" + n)
            res = [r_.T for r_ in res]
        else:
            res = adamw_halves(sc, reds[n], sib_red[n], shard(n), shard("m_" + n), shard("v_" + n), "adamw_" + n)
        outs[n] = [r_.reshape(args[n].shape) for r_ in res]

    small_g = {
        "pre_mix_norm": gg1, "a_ln_g": g_lng, "a_ln_b": g_lnb, "a_spatial_w": g_ws, "a_spatial_b": g_bst.T,
        "b_gk_bias": g_gkb, "b_out_norm": g_wn, "post_mix_norm": gg2, "pre_ffn_norm": gg3,
        "conv_b": gcb, "post_ffn_norm": gg4, "post_ple_norm": gg5,
        "b_gk": mm_tn(zl, dpre, "dw_gk")[:B_RANK], "conv_w": gcw,
        "loss": loss,
    }
    red_entries = SMALL + (("b_gk", B_RANK * 512), ("conv_w", 3 * 2 * D_FF), ("loss", 1))
    g_fin = _unpack_small(red_entries, allreduce_small(_pack_small(red_entries, lambda n: small_g[n])))
    g_fin["b_gk"] = lax.dynamic_slice(g_fin["b_gk"].reshape(B_RANK, 512), (0, me * B_HK), (B_RANK, B_HK))
    g_fin["conv_w"] = lax.dynamic_slice(g_fin["conv_w"].reshape(3, 2 * D_FF), (0, me * 1408), (3, 1408))
    upd_entries = SMALL + (("b_gk", B_RANK * B_HK), ("conv_w", 3 * 1408))
    res = adamw_small(*[_pack_small(upd_entries, get) for get in
                        (lambda n: g_fin[n], lambda n: args[n], lambda n: args["m_" + n], lambda n: args["v_" + n])])
    res = [_unpack_small(upd_entries, r_) for r_ in res]
    for n, _ in upd_entries:
        outs[n] = [r_[n].reshape(args[n].shape) for r_ in [g_fin] + res]

    return (g_fin["loss"].reshape(()), dx.reshape(x.shape), *[outs[n][0] for n in order], *[outs[n][1] for n in order],
            *[outs[n][2] for n in order], *[outs[n][3] for n in order])
```

```python
import math

import jax
import jax.numpy as jnp
from jax import lax
from jax.experimental import pallas as pl
from jax.experimental.pallas import tpu as pltpu

F32 = jnp.float32
BF = jnp.bfloat16
SDS = jax.ShapeDtypeStruct
MESH = pl.DeviceIdType.MESH

EPS = 1e-6
D = 1024
A_W = 512
A_G, A_C = 8, 128
A_GD = A_W // A_G
B_H, B_HK, B_HV = 4, 128, 256
B_C = 64
GLA_HPB = 4
B_RANK = 16
D_FF = 2816
PLE = 256
LANE = 128
VMEM_LIMIT = 60 * 1024 * 1024

ADAM_LR, ADAM_B1, ADAM_B2, ADAM_EPS, ADAM_WD, ADAM_STEP = 0.001, 0.9, 0.999, 1e-08, 0.01, 10

_GC = math.sqrt(2.0 / math.pi)
_GA = 0.044715

BIG = ("w_in", "a_out", "b_out", "w_mix_out", "w_up", "w_down", "w_ple", "w_ple_gate")
TINY = ("b_gk", "conv_w")
SMALL = (("pre_mix_norm", 1024), ("a_ln_g", 512), ("a_ln_b", 512), ("a_spatial_w", 131072),
         ("a_spatial_b", 1024), ("b_gk_bias", 512), ("b_out_norm", 256), ("post_mix_norm", 1024),
         ("pre_ffn_norm", 1024), ("conv_b", 5632), ("post_ffn_norm", 1024), ("post_ple_norm", 1024))


def _pc(body, **kw):
    return pl.pallas_call(body, **kw)


def _cp(n):
    return pltpu.CompilerParams(dimension_semantics=("arbitrary",) * n, vmem_limit_bytes=VMEM_LIMIT)


def _const(shape):
    nd = len(shape)
    return pl.BlockSpec(shape, lambda *_: (0,) * nd, pipeline_mode=pl.Buffered(1))


def _acc(shape):
    nd = len(shape)
    return pl.BlockSpec(shape, lambda *_: (0,) * nd)


def _dot(a, b):
    return jnp.dot(a, b, preferred_element_type=F32)


def _dot_nt(a, b):
    return lax.dot_general(a, b, (((1,), (1,)), ((), ())), preferred_element_type=F32)


def _dot_tn(a, b):
    return lax.dot_general(a, b, (((0,), (0,)), ((), ())), preferred_element_type=F32)


def _gelu(x):
    return 0.5 * x * (1.0 + jnp.tanh(_GC * (x + _GA * x * x * x)))


def _gelu_and_grad(x):
    x2 = x * x
    s = 0.5 * jnp.tanh((_GC * x) * (1.0 + _GA * x2)) + 0.5
    g = x * s
    return g, s + g * (1.0 - s) * ((6.0 * _GC * _GA) * x2 + 2.0 * _GC)


def _log_sigmoid(x):
    return jnp.minimum(x, 0.0) - jnp.log(1.0 + jnp.exp(-jnp.abs(x)))


def _rms(x, g):
    return x * lax.rsqrt(jnp.mean(x * x, axis=-1, keepdims=True) + EPS) * g


def _rms_bwd(dy, x, g):
    r = lax.rsqrt(jnp.mean(x * x, axis=-1, keepdims=True) + EPS)
    n = x * r
    dn = dy * g
    dx = r * (dn - n * jnp.mean(dn * n, axis=-1, keepdims=True))
    return dx, jnp.sum(dy * n, axis=0, keepdims=True)


def _ldot3(l, x):
    h = x.astype(BF)
    r = x - h.astype(F32)
    m = r.astype(BF)
    lo = (r - m.astype(F32)).astype(BF)
    return _dot(l, h) + _dot(l, m) + _dot(l, lo)


def _split_side(refs, n_in, n_out, n_scratch, side):
    si, so = (side.n_in, side.n_out) if side else (0, 0)
    cuts = [n_in, si, n_out, so, n_scratch]
    out, at = [], 0
    for c in cuts:
        out.append(refs[at:at + c])
        at += c
    return (*out, refs[at:])


def _side_specs(side):
    return ([_ANY] * side.n_in, [_ANY] * side.n_out, side.out_shapes, side.scratch, side.ins) if side else ([],) * 5


def norm_matmul(x, g, wt, bn, t, name, nblk, f32_blk, tail_blk, side=None):
    s, dm = x.shape
    w_spec = pl.BlockSpec((bn, dm), lambda i, j: (j, 0))
    nt = s // t

    def body(*refs):
        (x_ref, g_ref, w_ref, wl_ref), s_in, outs, s_out, (a_sc,), s_scr = _split_side(refs, 4, 4, 1, side)
        a_ref, z_ref, f32_ref, tail_ref = outs
        i, j = pl.program_id(0), pl.program_id(1)
        if side:
            @pl.when((i == 0) & (j == 0))
            def _():
                side.start(s_in, s_out, *s_scr)

        @pl.when(j == 0)
        def _():
            a = _rms(x_ref[...], g_ref[...]).astype(BF)
            a_sc[...] = a
            a_ref[...] = a
            tail_ref[...] = _dot_nt(a, wl_ref[...]).astype(BF)

        acc = _dot_nt(a_sc[...], w_ref[...])
        z_ref[...] = acc.astype(BF)

        @pl.when(j == f32_blk)
        def _():
            f32_ref[...] = acc
        if side:
            @pl.when((i == nt - 1) & (j == nblk - 1))
            def _():
                side.finish(s_in, s_out, *s_scr)

    si_specs, so_specs, so_shapes, s_scratch, s_ins = _side_specs(side)
    return _pc(
        body, name=name, grid=(nt, nblk),
        in_specs=[pl.BlockSpec((t, dm), lambda i, j: (i, 0)), _const((1, dm)), w_spec,
                  pl.BlockSpec((LANE, dm), lambda i, j: (tail_blk, 0), pipeline_mode=pl.Buffered(1))] + si_specs,
        out_specs=[pl.BlockSpec((t, dm), lambda i, j: (i, 0)), pl.BlockSpec((t, bn), lambda i, j: (i, j)),
                   pl.BlockSpec((t, bn), lambda i, j: (i, 0)), pl.BlockSpec((t, LANE), lambda i, j: (i, 0))] + so_specs,
        out_shape=[SDS((s, dm), BF), SDS((s, nblk * bn), BF), SDS((s, bn), F32), SDS((s, LANE), BF)] + so_shapes,
        scratch_shapes=[pltpu.VMEM((t, dm), BF)] + s_scratch, compiler_params=_cp(2))(x, g, wt, wt, *s_ins)


def _sgu_weights(wc_ref, transposed):
    r = lax.broadcasted_iota(jnp.int32, (A_C, A_G * A_C), 0)
    c = lax.broadcasted_iota(jnp.int32, (A_C, A_G * A_C), 1) & (A_C - 1)
    return jnp.where((r <= c) if transposed else (c <= r), wc_ref[...], 0.0).astype(BF)


def _sgu_spread(xs, bdm):
    return jnp.concatenate([jnp.tile(x, (A_G, 1)) * bdm for x in xs], axis=1)


def _sgu_recompute(v, lng, lnb):
    gv, dgv = _gelu_and_grad(v)
    gv = gv.astype(F32)
    mu = jnp.mean(gv, axis=-1, keepdims=True)
    xc = gv - mu
    rstd = lax.rsqrt(jnp.mean(xc * xc, axis=-1, keepdims=True) + EPS)
    xhat = xc * rstd
    return dgv, rstd, xhat, (xhat * lng + lnb).astype(BF)


def sgu_fwd(z, ln_g, ln_b, w_cat, bias_full, bdm, t):
    s = z.shape[0]
    nch = t // A_C

    def body(u_ref, v_ref, g_ref, b_ref, wc_ref, bias_ref, bdm_ref, sa_ref):
        vns = [_sgu_recompute(v_ref[pl.ds(ci * A_C, A_C), :], g_ref[...], b_ref[...])[3]
               for ci in range(nch)]
        mixed = _dot(_sgu_weights(wc_ref, False), _sgu_spread(vns, bdm_ref[...]))
        for ci in range(nch):
            rows = pl.ds(ci * A_C, A_C)
            s_ = mixed[:, ci * A_W:(ci + 1) * A_W] + bias_ref[...]
            sa_ref[rows, :] = _gelu(u_ref[rows, :]) * s_.astype(BF)

    return _pc(
        body, name="sgu_fwd", grid=(s // t,),
        in_specs=[pl.BlockSpec((t, A_W), lambda i: (i, 0)), pl.BlockSpec((t, A_W), lambda i: (i, 1)),
                  _const((1, A_W)), _const((1, A_W)), _const((A_C, A_G * A_C)), _const((A_C, A_W)),
                  _const((A_G * A_C, A_W))],
        out_specs=pl.BlockSpec((t, A_W), lambda i: (i, 0)),
        out_shape=SDS((s, A_W), BF), compiler_params=_cp(1))(z, z, ln_g, ln_b, w_cat, bias_full, bdm)


def _gla_decays(qk, lr, wgk, bias, l, t):
    nc = t // B_C
    q = qk[:, :B_HK].astype(F32) * (B_HK ** -0.5)
    k = qk[:, B_HK:].astype(F32)
    pre = _dot(lr, wgk) + bias
    la = _log_sigmoid(pre) * (1.0 / 16.0)
    b = _ldot3(l, la)
    b3 = b.reshape(nc, B_C, B_HK)
    bl = jnp.broadcast_to(b3[:, B_C - 1:B_C, :], (nc, B_C, B_HK)).reshape(t, B_HK)
    eb, enb, etb = jnp.exp(b), jnp.exp(-b), jnp.exp(bl - b)
    return pre, b, bl, eb, enb, etb, q * eb, k * enb, k * etb


def gla_fwd(z, qk32, zl, wgk, bias, wn, ltri, t):
    s = z.shape[0]
    nc = t // B_C
    hpb = GLA_HPB
    assert hpb == B_H
    kw, vw = hpb * B_HK, hpb * B_HV

    def body(qk_ref, v_ref, og_ref, lr_ref, wgk_ref, bias_ref, wn_ref, l_ref, ob_ref, o_ref, st_ref, st_sc, o_sc):
        @pl.when(pl.program_id(0) == 0)
        def _():
            st_sc[...] = jnp.zeros((B_H, B_HV, B_HK), F32)

        lr, l = lr_ref[...], l_ref[...]
        for hh in range(hpb):
            h = hh
            cv, ck = slice(hh * B_HV, (hh + 1) * B_HV), slice(hh * B_HK, (hh + 1) * B_HK)
            _, _, bl, _, _, _, qd, ki, kt = _gla_decays(qk_ref[:, cv], lr, wgk_ref[:, ck], bias_ref[:, ck], l, t)
            qd, ki, kt = qd.astype(BF), ki.astype(BF), kt.astype(BF)
            vb = v_ref[:, cv]
            sc = jnp.where(l > 0, _dot_nt(qd, ki), 0.0).astype(BF)
            o_sc[hh] = _dot(sc, vb)
            for n in range(nc):
                rows = slice(n * B_C, (n + 1) * B_C)
                st = st_sc[h]
                stb = st.astype(BF)
                st_ref[n, hh] = stb
                o_sc[hh, rows, :] += _dot_nt(qd[rows], stb)
                st_sc[h] = st * jnp.exp(bl[n * B_C:n * B_C + 1, :]) + _dot_tn(vb[rows], kt[rows])
            ob = o_sc[hh].astype(BF)
            o_ref[:, cv] = ob
            og = og_ref[:, cv].astype(F32)
            ob_ref[:, cv] = (_rms(ob.astype(F32), wn_ref[...]) * og * jax.nn.sigmoid(og)).astype(BF)

    vo, go = 2048 // vw, 3072 // vw
    return _pc(
        body, name="gla_fwd", grid=(s // t, B_H // hpb),
        in_specs=[pl.BlockSpec((t, vw), lambda i, g: (i, g)), pl.BlockSpec((t, vw), lambda i, g: (i, vo + g)),
                  pl.BlockSpec((t, vw), lambda i, g: (i, go + g)), pl.BlockSpec((t, LANE), lambda i, g: (i, 0)),
                  pl.BlockSpec((LANE, kw), lambda i, g: (0, g)), pl.BlockSpec((1, kw), lambda i, g: (0, g)),
                  _const((1, B_HV)), _const((t, t))],
        out_specs=[pl.BlockSpec((t, vw), lambda i, g: (i, g)), pl.BlockSpec((t, vw), lambda i, g: (i, g)),
                   pl.BlockSpec((nc, hpb, B_HV, B_HK), lambda i, g: (i, g, 0, 0))],
        out_shape=[SDS((s, D), BF), SDS((s, D), BF), SDS((s // B_C, B_H, B_HV, B_HK), BF)],
        scratch_shapes=[pltpu.VMEM((B_H, B_HV, B_HK), F32), pltpu.VMEM((hpb, t, B_HV), F32)],
        compiler_params=_cp(2))(qk32, z, z, zl, wgk, bias, wn, ltri)


def mix_fwd(sa, ob, z, x, a_out, b_out, w_mix, g2, t):
    s = x.shape[0]

    def body(sa_ref, ob_ref, ga_ref, gb_ref, x_ref, ao_ref, bo_ref, wm_ref, g2_ref,
             ya_ref, yb_ref, mp_ref, mx_ref, h1_ref):
        ya = _dot(sa_ref[...], ao_ref[...]).astype(BF)
        yb = _dot(ob_ref[...], bo_ref[...]).astype(BF)
        ya_ref[...] = ya
        yb_ref[...] = yb
        mp = (jax.nn.sigmoid(ga_ref[...].astype(F32)) * ya.astype(F32)
              + jax.nn.sigmoid(gb_ref[...].astype(F32)) * yb.astype(F32)).astype(BF)
        mp_ref[...] = mp
        mx = _dot(mp, wm_ref[...]).astype(BF)
        mx_ref[...] = mx
        h1_ref[...] = x_ref[...] + _rms(mx.astype(F32), g2_ref[...])

    row = lambda w: pl.BlockSpec((t, w), lambda i: (i, 0))
    return _pc(
        body, name="mix_fwd", grid=(s // t,),
        in_specs=[row(A_W), row(D), pl.BlockSpec((t, D), lambda i: (i, 4)), pl.BlockSpec((t, D), lambda i: (i, 5)),
                  row(D), _const((A_W, D)), _const((D, D)), _const((D, D)), _const((1, D))],
        out_specs=[row(D)] * 5,
        out_shape=[SDS((s, D), BF)] * 4 + [SDS((s, D), F32)],
        compiler_params=_cp(1))(sa, ob, z, z, x, a_out, b_out, w_mix, g2)


def ffn_up_fwd(h1, g3, w_up3, conv_w, conv_b, t):
    s = h1.shape[0]
    bn = w_up3.shape[2]

    def body(x_ref, g_ref, wg_ref, wv_ref, cwg_ref, cwv_ref, cbg_ref, cbv_ref,
             c_ref, ug_ref, uv_ref, cg_ref, cv_ref, ff_ref, c_sc, carry):
        i, j = pl.program_id(0), pl.program_id(1)

        @pl.when(j == 0)
        def _():
            c = _rms(x_ref[...], g_ref[...]).astype(BF)
            c_sc[...] = c
            c_ref[...] = c

        @pl.when(i == 0)
        def _():
            carry[j] = jnp.zeros((2, 8, bn), F32)

        def branch(k, w_ref, cw_ref, cb_ref, u_ref, o_ref):
            ub = _dot(c_sc[...], w_ref[...]).astype(BF)
            u_ref[...] = ub
            u = ub.astype(F32)
            ext = jnp.concatenate([carry[j, k], u], axis=0)
            carry[j, k] = u[t - 8:]
            w = cw_ref[...]
            cc = (cb_ref[...] + w[0:1] * pltpu.roll(ext, 2, 0) + w[1:2] * pltpu.roll(ext, 1, 0) + w[2:3] * ext)[8:]
            cc = cc.astype(BF)
            o_ref[...] = cc
            return cc

        g = _gelu(branch(0, wg_ref, cwg_ref, cbg_ref, ug_ref, cg_ref))
        ff_ref[...] = g * branch(1, wv_ref, cwv_ref, cbv_ref, uv_ref, cv_ref)

    col = lambda rows, off: pl.BlockSpec((rows, bn), lambda i, j: (0, j + off))
    out = pl.BlockSpec((t, bn), lambda i, j: (i, j))
    return _pc(
        body, name="ffn_up_fwd", grid=(s // t, 2),
        in_specs=[pl.BlockSpec((t, D), lambda i, j: (i, 0)), _const((1, D)),
                  pl.BlockSpec((None, D, bn), lambda i, j: (j, 0, 0)), pl.BlockSpec((None, D, bn), lambda i, j: (j + 2, 0, 0)),
                  col(3, 0), col(3, 2), col(1, 0), col(1, 2)],
        out_specs=[pl.BlockSpec((t, D), lambda i, j: (i, 0))] + [out] * 5,
        out_shape=[SDS((s, D), BF)] + [SDS((s, D_FF), BF)] * 5,
        scratch_shapes=[pltpu.VMEM((t, D), BF), pltpu.VMEM((2, 2, 8, bn), F32)],
        compiler_params=_cp(2))(h1, g3, w_up3, w_up3, conv_w, conv_w, conv_b, conv_b)


def ffn_up_bwd(ug, uv, cg, cv, dff, conv_w, w_up, h1, g3, dh2, t):
    s = h1.shape[0]
    nt = s // t
    hb = t // 8
    bn = 1408
    r = t + 8

    def body(ug_ref, uv_ref, cg_ref, cv_ref, cag_ref, cav_ref, d_ref, da_ref, cw_ref, w_ref, x_ref, g_ref, dres_ref,
             du_ref, gw_ref, gb_ref, dx_ref, gg_ref):
        i = pl.program_id(0)

        @pl.when(i == 0)
        def _():
            gw_ref[...] = jnp.zeros((3, 2 * D_FF), F32)
            gb_ref[...] = jnp.zeros((1, 2 * D_FF), F32)
            gg_ref[...] = jnp.zeros((1, D), F32)

        more = (i < nt - 1).astype(BF)

        def gate(c_g, c_v, d_):
            gl, dgl = _gelu_and_grad(c_g)
            return (d_ * c_v * dgl).astype(F32), (d_ * gl).astype(F32)

        def back(dc, dc_next, u_ref, cols, off):
            w = cw_ref[:, off:off + bn]
            d_ext = jnp.concatenate([dc, dc_next], axis=0)
            d1, d2 = pltpu.roll(d_ext, r - 1, 0)[:t], pltpu.roll(d_ext, r - 2, 0)[:t]
            du_ref[:, off:off + bn] = (w[2:3] * dc + w[1:2] * d1 + w[0:1] * d2).astype(BF)
            u = u_ref[:, cols].astype(F32)
            gw_ref[0:1, off:off + bn] += jnp.sum(d2 * u, axis=0, keepdims=True)
            gw_ref[1:2, off:off + bn] += jnp.sum(d1 * u, axis=0, keepdims=True)
            gw_ref[2:3, off:off + bn] += jnp.sum(dc * u, axis=0, keepdims=True)
            gb_ref[:, off:off + bn] += jnp.sum(dc, axis=0, keepdims=True)

        for kb in range(D_FF // bn):
            cols = slice(kb * bn, (kb + 1) * bn)
            dg, dv = gate(cg_ref[:, cols], cv_ref[:, cols], d_ref[:, cols])
            dg_n, dv_n = gate(cag_ref[:, cols], cav_ref[:, cols], da_ref[:, cols] * more)
            back(dg, dg_n, ug_ref, cols, kb * bn)
            back(dv, dv_n, uv_ref, cols, D_FF + kb * bn)

        acc = _dot_nt(du_ref[...], w_ref[...])
        dxn, dg3 = _rms_bwd(acc, x_ref[...], g_ref[...])
        dx_ref[...] = dres_ref[...] + dxn
        gg_ref[...] += dg3

    tile = lambda width: pl.BlockSpec((t, width), lambda i: (i, 0))
    after = pl.BlockSpec((8, D_FF), lambda i: (jnp.minimum((i + 1) * hb, nt * hb - 1), 0))
    return _pc(
        body, name="ffn_up_bwd", grid=(nt,),
        in_specs=[tile(D_FF)] * 4 + [after, after, tile(D_FF), after, _const((3, 2 * D_FF)), _const((D, 2 * D_FF)),
                                     tile(D), _const((1, D)), tile(D)],
        out_specs=[tile(2 * D_FF), _acc((3, 2 * D_FF)), _acc((1, 2 * D_FF)), tile(D), _acc((1, D))],
        out_shape=[SDS((s, 2 * D_FF), BF), SDS((3, 2 * D_FF), F32), SDS((1, 2 * D_FF), F32),
                   SDS((s, D), F32), SDS((1, D), F32)],
        compiler_params=_cp(1))(ug, uv, cg, cv, cg, cv, dff, dff, conv_w, w_up, h1, g3, dh2)


def out_fwd(ff, h1, p, tgt, w_down, w_pg, w_ple, g4, g5, t):
    s = h1.shape[0]

    def body(ff_ref, h1_ref, p_ref, t_ref, wd_ref, wpg_ref, wpl_ref, g4_ref, g5_ref,
             f_ref, h2_ref, pg_ref, pe_ref, dy_ref, loss_ref):
        @pl.when(pl.program_id(0) == 0)
        def _():
            loss_ref[...] = jnp.zeros((1, 1), F32)

        f = _dot(ff_ref[...], wd_ref[...]).astype(BF)
        f_ref[...] = f
        h2 = h1_ref[...] + _rms(f.astype(F32), g4_ref[...])
        h2b = h2.astype(BF)
        h2_ref[...] = h2b
        pg = _dot(h2b, wpg_ref[...]).astype(BF)
        pe = _dot(p_ref[...].astype(BF), wpl_ref[...]).astype(BF)
        pg_ref[...] = pg
        pe_ref[...] = pe
        y = h2 + _rms(jax.nn.sigmoid(pg.astype(F32)) * pe.astype(F32), g5_ref[...])
        err = y - t_ref[...]
        dy_ref[...] = err * (1.0 / D)
        loss_ref[...] += (0.5 / D) * jnp.sum(err * err)

    row = lambda w: pl.BlockSpec((t, w), lambda i: (i, 0))
    return _pc(
        body, name="out_fwd", grid=(s // t,),
        in_specs=[row(D_FF), row(D), row(PLE), row(D), _const((D_FF, D)), _const((D, D)), _const((PLE, D)),
                  _const((1, D)), _const((1, D))],
        out_specs=[row(D)] * 5 + [_acc((1, 1))],
        out_shape=[SDS((s, D), BF)] * 4 + [SDS((s, D), F32), SDS((1, 1), F32)],
        compiler_params=_cp(1))(ff, h1, p, tgt, w_down, w_pg, w_ple, g4, g5)


def out_bwd(dy, pg, pe, f, g5, g4, w_pg, w_down, t):
    s = dy.shape[0]

    def body(dy_ref, pg_ref, pe_ref, f_ref, g5_ref, g4_ref, wpg_ref, wd_ref,
             dh2_ref, dpe_ref, dpg_ref, df_ref, dff_ref, gg5_ref, gg4_ref):
        @pl.when(pl.program_id(0) == 0)
        def _():
            gg5_ref[...] = jnp.zeros((1, D), F32)
            gg4_ref[...] = jnp.zeros((1, D), F32)

        dy_ = dy_ref[...]
        pg_ = pg_ref[...].astype(F32)
        pe_ = pe_ref[...].astype(F32)
        sg = jax.nn.sigmoid(pg_)
        dple, dg5 = _rms_bwd(dy_, sg * pe_, g5_ref[...])
        gg5_ref[...] += dg5
        dpe_ref[...] = (dple * sg).astype(BF)
        dpg = (dple * pe_ * sg * (1.0 - sg)).astype(BF)
        dpg_ref[...] = dpg
        dh2 = dy_ + _dot_nt(dpg, wpg_ref[...])
        dh2_ref[...] = dh2
        df, dg4 = _rms_bwd(dh2, f_ref[...].astype(F32), g4_ref[...])
        gg4_ref[...] += dg4
        dfb = df.astype(BF)
        df_ref[...] = dfb
        dff_ref[...] = _dot_nt(dfb, wd_ref[...]).astype(BF)

    row = lambda w: pl.BlockSpec((t, w), lambda i: (i, 0))
    return _pc(
        body, name="out_bwd", grid=(s // t,),
        in_specs=[row(D), row(D), row(D), row(D), _const((1, D)), _const((1, D)), _const((D, D)), _const((D_FF, D))],
        out_specs=[row(D), row(D), row(D), row(D), row(D_FF), _acc((1, D)), _acc((1, D))],
        out_shape=[SDS((s, D), F32), SDS((s, D), BF), SDS((s, D), BF), SDS((s, D), BF), SDS((s, D_FF), BF),
                   SDS((1, D), F32), SDS((1, D), F32)],
        compiler_params=_cp(1))(dy, pg, pe, f, g5, g4, w_pg, w_down)


def nt_normbwd(dys, w, xin, gain, dres, t, name, side=None):
    s = xin.shape[0]
    nt = s // t
    np_ = len(dys)

    def body(*refs):
        ins_, s_in, (dx_ref, gg_ref), s_out, _, s_scr = _split_side(refs, np_ + 4, 2, 0, side)
        dy_refs = ins_[:np_]
        w_ref, x_ref, g_ref, dres_ref = ins_[np_:]
        i = pl.program_id(0)

        @pl.when(i == 0)
        def _():
            gg_ref[...] = jnp.zeros((1, D), F32)
            if side:
                side.start(s_in, s_out, *s_scr)

        acc = _dot(jnp.concatenate([r_[...] for r_ in dy_refs], axis=1), w_ref[...])
        dxn, dg = _rms_bwd(acc, x_ref[...], g_ref[...])
        dx_ref[...] = dres_ref[...] + dxn
        gg_ref[...] += dg
        if side:
            @pl.when(i == nt - 1)
            def _():
                side.finish(s_in, s_out, *s_scr)

    row = lambda width: pl.BlockSpec((t, width), lambda i: (i, 0))
    si_specs, so_specs, so_shapes, s_scratch, s_ins = _side_specs(side)
    assert sum(dy.shape[1] for dy in dys) == w.shape[0]
    return _pc(
        body, name=name, grid=(nt,),
        in_specs=[row(dy.shape[1]) for dy in dys] + [_const(w.shape), row(D), _const((1, D)), row(D)] + si_specs,
        out_specs=[row(D), _acc((1, D))] + so_specs,
        out_shape=[SDS((s, D), F32), SDS((1, D), F32)] + so_shapes, scratch_shapes=s_scratch,
        compiler_params=_cp(1))(*dys, w, xin, gain, dres, *s_ins)


def mix_bwd(dh1, mx, z, ya, yb, g2, w_mix, a_out, b_out, t, side=None):
    s = dh1.shape[0]
    nt = s // t

    def body(*refs):
        ins_, s_in, outs_, s_out, _, s_scr = _split_side(refs, 10, 8, 0, side)
        dh_ref, mx_ref, ga_ref, gb_ref, ya_ref, yb_ref, g2_ref, wm_ref, ao_ref, bo_ref = ins_
        dmx_ref, dya_ref, dyb_ref, dga_ref, dgb_ref, dsa_ref, dob_ref, gg2_ref = outs_

        @pl.when(pl.program_id(0) == 0)
        def _():
            gg2_ref[...] = jnp.zeros((1, D), F32)
            if side:
                side.start(s_in, s_out, *s_scr)

        dmx, dg2 = _rms_bwd(dh_ref[...], mx_ref[...].astype(F32), g2_ref[...])
        gg2_ref[...] += dg2
        dmxb = dmx.astype(BF)
        dmx_ref[...] = dmxb
        dmp = _dot_nt(dmxb, wm_ref[...]).astype(BF)

        def gate(g_ref, y_ref, dy_ref, dg_ref, w_ref, dz_ref):
            sg = jax.nn.sigmoid(g_ref[...])
            dyb_ = dmp * sg
            dy_ref[...] = dyb_
            dg_ref[...] = dyb_ * y_ref[...] * (1.0 - sg)
            dz_ref[...] = _dot_nt(dyb_, w_ref[...]).astype(BF)

        gate(ga_ref, ya_ref, dya_ref, dga_ref, ao_ref, dsa_ref)
        gate(gb_ref, yb_ref, dyb_ref, dgb_ref, bo_ref, dob_ref)
        if side:
            @pl.when(pl.program_id(0) == nt - 1)
            def _():
                side.finish(s_in, s_out, *s_scr)

    row = lambda w: pl.BlockSpec((t, w), lambda i: (i, 0))
    si_specs, so_specs, so_shapes, s_scratch, s_ins = _side_specs(side)
    return _pc(
        body, name="mix_bwd", grid=(nt,),
        in_specs=[row(D), row(D), pl.BlockSpec((t, D), lambda i: (i, 4)), pl.BlockSpec((t, D), lambda i: (i, 5)),
                  row(D), row(D), _const((1, D)), _const((D, D)), _const((A_W, D)), _const((D, D))] + si_specs,
        out_specs=[row(D)] * 5 + [row(A_W), row(D), _acc((1, D))] + so_specs,
        out_shape=[SDS((s, D), BF)] * 5 + [SDS((s, A_W), BF), SDS((s, D), BF), SDS((1, D), F32)] + so_shapes,
        scratch_shapes=s_scratch,
        compiler_params=_cp(1))(dh1, mx, z, z, ya, yb, g2, w_mix, a_out, b_out, *s_ins)


def sgu_bwd(z, dsa, ln_g, ln_b, w_cat, w_cat_t, bias_full, bdm, t):
    s = z.shape[0]
    nt = s // t
    nch = t // A_C

    def body(u_ref, v_ref, dsa_ref, g_ref, b_ref, wc_ref, wct_ref, bias_ref, bdm_ref,
             duv_ref, glg_ref, glb_ref, gws_ref, gbs_ref, ds_acc):
        i = pl.program_id(0)

        @pl.when(i == 0)
        def _():
            glg_ref[...] = jnp.zeros((1, A_W), F32)
            glb_ref[...] = jnp.zeros((1, A_W), F32)
            gws_ref[...] = jnp.zeros((A_C, A_G * A_C), F32)
            ds_acc[...] = jnp.zeros((A_C, A_W), F32)

        lng, bdm_ = g_ref[...], bdm_ref[...]
        rec = [_sgu_recompute(v_ref[pl.ds(ci * A_C, A_C), :], lng, b_ref[...]) for ci in range(nch)]
        spread_vn = _sgu_spread([r_[3] for r_ in rec], bdm_)
        mixed = _dot(_sgu_weights(wc_ref, False), spread_vn)
        dsas, dss, dgus = [], [], []
        for ci in range(nch):
            rows = pl.ds(ci * A_C, A_C)
            gu, dgu = _gelu_and_grad(u_ref[rows, :])
            dsa_ = dsa_ref[rows, :]
            ds = dsa_ * gu
            ds_acc[...] += ds.astype(F32)
            dsas.append(dsa_)
            dgus.append(dgu)
            dss.append(ds)
        r = lax.broadcasted_iota(jnp.int32, (A_C, A_G * A_C), 0)
        c = lax.broadcasted_iota(jnp.int32, (A_C, A_G * A_C), 1) & (A_C - 1)
        gws_ref[...] += jnp.where(c <= r, _dot_nt(jnp.concatenate(dss, axis=1), spread_vn), 0.0)
        dvns = _dot(_sgu_weights(wct_ref, True), _sgu_spread(dss, bdm_))
        for ci in range(nch):
            rows = pl.ds(ci * A_C, A_C)
            dgv, rstd, xhat, _ = rec[ci]
            dvn = dvns[:, ci * A_W:(ci + 1) * A_W]
            glb_ref[...] += jnp.sum(dvn, axis=0, keepdims=True)
            glg_ref[...] += jnp.sum(dvn * xhat, axis=0, keepdims=True)
            dxh = dvn * lng
            dgv_ = rstd * (dxh - jnp.mean(dxh, axis=-1, keepdims=True)
                           - xhat * jnp.mean(dxh * xhat, axis=-1, keepdims=True))
            s_ = mixed[:, ci * A_W:(ci + 1) * A_W] + bias_ref[...]
            duv_ref[rows, :A_W] = dsas[ci] * dgus[ci] * s_.astype(BF)
            duv_ref[rows, A_W:] = (dgv_ * dgv).astype(BF)

        @pl.when(i == nt - 1)
        def _():
            acc = ds_acc[...]
            for g in range(A_G):
                gbs_ref[:, g:g + 1] = jnp.sum(acc[:, g * A_GD:(g + 1) * A_GD], axis=1, keepdims=True)

    return _pc(
        body, name="sgu_bwd", grid=(nt,),
        in_specs=[pl.BlockSpec((t, A_W), lambda i: (i, 0)), pl.BlockSpec((t, A_W), lambda i: (i, 1)),
                  pl.BlockSpec((t, A_W), lambda i: (i, 0)),
                  _const((1, A_W)), _const((1, A_W)), _const((A_C, A_G * A_C)), _const((A_C, A_G * A_C)),
                  _const((A_C, A_W)), _const((A_G * A_C, A_W))],
        out_specs=[pl.BlockSpec((t, D), lambda i: (i, 0)), _acc((1, A_W)), _acc((1, A_W)),
                   _acc((A_C, A_G * A_C)), _acc((A_C, A_G))],
        out_shape=[SDS((s, D), BF), SDS((1, A_W), F32), SDS((1, A_W), F32), SDS((A_C, A_G * A_C), F32),
                   SDS((A_C, A_G), F32)],
        scratch_shapes=[pltpu.VMEM((A_C, A_W), F32)],
        compiler_params=_cp(1))(z, z, dsa, ln_g, ln_b, w_cat, w_cat_t, bias_full, bdm)


def gla_bwd(z, qk32, zl, o, dob, states, wgk, bias, wn, ltri, ltri_t, t, side=None):
    s = z.shape[0]
    nt = s // t
    nc = t // B_C
    hpb = GLA_HPB
    assert hpb == B_H
    kw, vw = hpb * B_HK, hpb * B_HV

    def body(*refs):
        ins_, s_in, outs_, s_out, scr_, s_scr = _split_side(refs, 12, 7, 5, side)
        qk_ref, v_ref, og_ref, lr_ref, o_ref, dob_ref, st_ref, wgk_ref, bias_ref, wn_ref, l_ref, lt_ref = ins_
        dqk_ref, dv_ref, dog_ref, dpre_ref, dlr_ref, gbias_ref, gwn_ref = outs_
        dst_sc, dv_sc, dqd_sc, dkt_sc, ddec_sc = scr_
        i = pl.program_id(0)
        g = pl.program_id(1)

        @pl.when((i == 0) & (g == 0))
        def _():
            gbias_ref[...] = jnp.zeros((B_H, 1, B_HK), F32)
            gwn_ref[...] = jnp.zeros((1, B_HV), F32)
            if side:
                side.start(s_in, s_out, *s_scr)

        @pl.when(i == 0)
        def _():
            dst_sc[...] = jnp.zeros((B_H, B_HV, B_HK), F32)

        lr, l, lt = lr_ref[...], l_ref[...], lt_ref[...]
        keep, keep_t = l > 0, lt > 0
        wn_ = wn_ref[...]
        last = lax.broadcasted_iota(jnp.int32, (nc, B_C, B_HK), 1) == B_C - 1
        for hh in range(hpb):
            h = hh
            cv, ck = slice(hh * B_HV, (hh + 1) * B_HV), slice(hh * B_HK, (hh + 1) * B_HK)
            pre, b, bl, eb, enb, etb, qd, ki, kt = _gla_decays(qk_ref[:, cv], lr, wgk_ref[:, ck], bias_ref[:, ck], l, t)
            qdb, kib, ktb = qd.astype(BF), ki.astype(BF), kt.astype(BF)
            vb = v_ref[:, cv]
            o_ = o_ref[:, cv].astype(F32)
            og = og_ref[:, cv].astype(F32)
            sog = jax.nn.sigmoid(og)
            dob_ = dob_ref[:, cv].astype(F32)
            don = dob_ * og * sog
            do, dwn = _rms_bwd(don, o_, wn_)
            gwn_ref[...] += dwn
            dog_ref[:, cv] = (dob_ * _rms(o_, wn_) * sog * (1.0 + og * (1.0 - sog))).astype(BF)
            dob16 = do.astype(BF)
            sc_t = jnp.where(keep_t, _dot_nt(kib, qdb), 0.0).astype(BF)
            dsc = jnp.where(keep, _dot_nt(dob16, vb), 0.0).astype(BF)
            dsc_t = jnp.where(keep_t, _dot_nt(vb, dob16), 0.0).astype(BF)
            dv_sc[hh] = _dot(sc_t, dob16)
            dqd_sc[hh] = _dot(dsc, kib)
            dki = _dot(dsc_t, qdb)
            for n in reversed(range(nc)):
                rows = slice(n * B_C, (n + 1) * B_C)
                dst = dst_sc[h]
                dstb = dst.astype(BF)
                stp = st_ref[n, hh]
                dv_sc[hh, rows, :] += _dot_nt(ktb[rows], dstb)
                dkt_sc[hh, rows, :] = _dot(vb[rows], dstb)
                dqd_sc[hh, rows, :] += _dot(dob16[rows], stp)
                dec = jnp.exp(bl[n * B_C:n * B_C + 1, :])
                ddec_sc[hh, n] = jnp.sum(dst * stp.astype(F32), axis=0, keepdims=True) * dec
                dst_sc[h] = dst * dec + _dot_tn(dob16[rows], qdb[rows])
            dqd, dkt = dqd_sc[hh], dkt_sc[hh]
            dv_ref[:, cv] = dv_sc[hh].astype(BF)
            dqk_ref[:, hh * B_HV:hh * B_HV + B_HK] = (dqd * eb * (B_HK ** -0.5)).astype(BF)
            dqk_ref[:, hh * B_HV + B_HK:(hh + 1) * B_HV] = (dki * enb + dkt * etb).astype(BF)
            dktkt = dkt * kt
            db3 = (dqd * qd - dki * ki - dktkt).reshape(nc, B_C, B_HK)
            dbl = jnp.sum(dktkt.reshape(nc, B_C, B_HK), axis=1, keepdims=True) + ddec_sc[hh]
            db = (db3 + jnp.where(last, dbl, 0.0)).reshape(t, B_HK)
            dla = _ldot3(lt, db)
            dpre = dla * (1.0 / 16.0) * (1.0 - jax.nn.sigmoid(pre))
            dpreb = dpre.astype(BF)
            dpre_ref[:, ck] = dpreb
            gbias_ref[h] += jnp.sum(dpre, axis=0, keepdims=True)
            dlr_h = _dot_nt(dpreb, wgk_ref[:, ck])
            dlr = dlr_h if hh == 0 else dlr + dlr_h
        dlr_ref[...] = dlr.astype(BF)
        if side:
            @pl.when((i == nt - 1) & (g == B_H // hpb - 1))
            def _():
                side.finish(s_in, s_out, *s_scr)

    rv = lambda i: nt - 1 - i
    si_specs, so_specs, so_shapes, s_scratch, s_ins = _side_specs(side)
    vo, go = 2048 // vw, 3072 // vw
    tile = lambda off: pl.BlockSpec((t, vw), lambda i, g: (rv(i), off + g))
    return _pc(
        body, name="gla_bwd", grid=(nt, B_H // hpb),
        in_specs=[tile(0), tile(vo), tile(go), pl.BlockSpec((t, LANE), lambda i, g: (rv(i), 0)), tile(0), tile(0),
                  pl.BlockSpec((nc, hpb, B_HV, B_HK), lambda i, g: (rv(i), g, 0, 0)),
                  pl.BlockSpec((LANE, kw), lambda i, g: (0, g)), pl.BlockSpec((1, kw), lambda i, g: (0, g)),
                  _const((1, B_HV)), _const((t, t)), _const((t, t))] + si_specs,
        out_specs=[tile(0), tile(0), tile(0), pl.BlockSpec((t, kw), lambda i, g: (rv(i), g)),
                   pl.BlockSpec((t, LANE), lambda i, g: (rv(i), 0)), _acc((B_H, 1, B_HK)), _acc((1, B_HV))] + so_specs,
        out_shape=[SDS((s, D), BF), SDS((s, D), BF), SDS((s, D), BF), SDS((s, B_H * B_HK), BF), SDS((s, LANE), BF),
                   SDS((B_H, 1, B_HK), F32), SDS((1, B_HV), F32)] + so_shapes,
        scratch_shapes=[pltpu.VMEM((B_H, B_HV, B_HK), F32), pltpu.VMEM((hpb, t, B_HV), F32),
                        pltpu.VMEM((hpb, t, B_HK), F32), pltpu.VMEM((hpb, t, B_HK), F32),
                        pltpu.VMEM((hpb, nc, 1, B_HK), F32)] + s_scratch,
        compiler_params=_cp(2))(qk32, z, z, zl, o, dob, states, wgk, bias, wn, ltri, ltri_t, *s_ins)


def mm_tn(a, b, name, tk=2048):
    s, m = a.shape
    n = b.shape[1]
    bn = next(c for c in (1024, 1408, 512, 256, 128) if n % c == 0 and m * c * 4 <= 6 * 1024 * 1024)
    tk = min(tk, s)
    nk = s // tk

    def body(a_ref, b_ref, o_ref, acc):
        k = pl.program_id(1)

        @pl.when(k == 0)
        def _():
            acc[...] = jnp.zeros((m, bn), F32)

        acc[...] += _dot_tn(a_ref[...].astype(BF), b_ref[...])

        @pl.when(k == nk - 1)
        def _():
            o_ref[...] = acc[...].astype(BF)

    return _pc(
        body, name=name, grid=(n // bn, nk),
        in_specs=[pl.BlockSpec((tk, m), lambda j, k: (k, 0)), pl.BlockSpec((tk, bn), lambda j, k: (k, j))],
        out_specs=pl.BlockSpec((m, bn), lambda j, k: (0, j)),
        out_shape=SDS((m, n), BF), scratch_shapes=[pltpu.VMEM((m, bn), F32)], compiler_params=_cp(2))(a, b)


def _adamw(w, g, m, v):
    m = ADAM_B1 * m + (1.0 - ADAM_B1) * g
    v = ADAM_B2 * v + (1.0 - ADAM_B2) * (g * g)
    m_hat = m / (1.0 - ADAM_B1 ** ADAM_STEP)
    v_hat = v / (1.0 - ADAM_B2 ** ADAM_STEP)
    return -ADAM_LR * (m_hat / (jnp.sqrt(v_hat) + ADAM_EPS) + ADAM_WD * w), m, v


def _half_rows(rows):
    rh = rows // 2
    return rh, max(b for b in range(16, 257, 16) if rh % b == 0)


def _pc_sp(body, grid, in_specs, out_specs, out_shape, name):
    gs = pltpu.PrefetchScalarGridSpec(num_scalar_prefetch=1, grid=grid, in_specs=in_specs, out_specs=out_specs)
    return _pc(body, grid_spec=gs, out_shape=out_shape, name=name, compiler_params=_cp(len(grid)))


def adamw_halves(sc, own, sib, w, m, v, name):
    rows, cols = w.shape
    rh, br = _half_rows(rows)
    nbk = rh // br

    def body(sc_ref, own_ref, sib_ref, w_ref, m_ref, v_ref, go_ref, d_ref, mo_ref, vo_ref):
        g_ = jnp.where(pl.program_id(0) // nbk == sc_ref[0], own_ref[...], sib_ref[...])
        go_ref[...] = g_
        d_ref[...], mo_ref[...], vo_ref[...] = _adamw(w_ref[...], g_, m_ref[...], v_ref[...])

    mine = pl.BlockSpec((br, cols), lambda i, sc_: (jnp.clip(i - sc_[0] * nbk, 0, nbk - 1), 0))
    theirs = pl.BlockSpec((br, cols), lambda i, sc_: (jnp.clip(i - (1 - sc_[0]) * nbk, 0, nbk - 1), 0))
    blk = pl.BlockSpec((br, cols), lambda i, sc_: (i, 0))
    return _pc_sp(body, (2 * nbk,), [mine, theirs, blk, blk, blk], [blk] * 4, [SDS((rows, cols), F32)] * 4,
                  name)(sc, own, sib, w, m, v)


def adamw_cols(sc, own, sib, w, m, v, name, cb=256):
    rows, cols = w.shape
    nk = cols // 2 // cb

    def body(sc_ref, own_ref, sib_ref, w_ref, m_ref, v_ref, go_ref, d_ref, mo_ref, vo_ref):
        g_ = jnp.where(pl.program_id(0) == sc_ref[0], own_ref[...], sib_ref[...])
        go_ref[...] = g_
        d_ref[...], mo_ref[...], vo_ref[...] = _adamw(w_ref[...], g_, m_ref[...], v_ref[...])

    mine = pl.BlockSpec((rows, cb), lambda h, k, sc_: (0, jnp.clip(k + (h - sc_[0]) * nk, 0, nk - 1)))
    theirs = pl.BlockSpec((rows, cb), lambda h, k, sc_: (0, jnp.clip(k + (h - 1 + sc_[0]) * nk, 0, nk - 1)))
    blk = pl.BlockSpec((rows, cb), lambda h, k, sc_: (0, h * nk + k))
    return _pc_sp(body, (2, nk), [mine, theirs, blk, blk, blk], [blk] * 4, [SDS((rows, cols), F32)] * 4,
                  name)(sc, own, sib, w, m, v)


def adamw_small(g, w, m, v):
    def body(g_ref, w_ref, m_ref, v_ref, d_ref, mo_ref, vo_ref):
        d_ref[...], mo_ref[...], vo_ref[...] = _adamw(w_ref[...], g_ref[...], m_ref[...], v_ref[...])

    vm = pl.BlockSpec(memory_space=pltpu.VMEM)
    return _pc(body, name="adamw_small", in_specs=[vm] * 4, out_specs=[vm] * 3, out_shape=[SDS(g.shape, F32)] * 3,
               compiler_params=pltpu.CompilerParams(vmem_limit_bytes=VMEM_LIMIT))(g, w, m, v)


def _pos():
    return lax.axis_index("x"), lax.axis_index("y"), lax.axis_index("c")


def _other_chips(x, y):
    return [(1 - x, y), (x, 1 - y), (1 - x, 1 - y)]


_ANY = pl.BlockSpec(memory_space=pltpu.HBM)


class _Side:
    def __init__(self, ins, out_shapes, nsem, start, finish):
        self.ins, self.out_shapes, self.start, self.finish = list(ins), list(out_shapes), start, finish
        self.scratch = [pltpu.SemaphoreType.DMA((nsem,)), pltpu.SemaphoreType.DMA((nsem,))]
        self.n_in, self.n_out = len(self.ins), len(self.out_shapes)


def _run_side(side, name):
    def body(*refs):
        args_ = (refs[:side.n_in], refs[side.n_in:side.n_in + side.n_out], *refs[side.n_in + side.n_out:])
        side.start(*args_)
        side.finish(*args_)

    return _pc(body, name=name, in_specs=[_ANY] * side.n_in, out_specs=[_ANY] * side.n_out,
               out_shape=side.out_shapes, scratch_shapes=side.scratch)(*side.ins)


def _split_rows(shape):
    return (shape[0] // 2) % 16 == 0


def _core_halves(shape, c):
    if _split_rows(shape):
        h = shape[0] // 2
        return ((pl.ds(pl.multiple_of(c * h, 16), h), slice(None)),
                (pl.ds(pl.multiple_of((1 - c) * h, 16), h), slice(None)))
    h = shape[1] // 2
    assert h % LANE == 0
    return ((slice(None), pl.ds(pl.multiple_of(c * h, LANE), h)),
            (slice(None), pl.ds(pl.multiple_of((1 - c) * h, LANE), h)))


def gather_side(bigs, tinies):
    nb, nt_ = len(bigs), len(tinies)

    def plan(ins, outs, ssem, rsem):
        x, y, c = _pos()
        me = 2 * x + y
        chips = _other_chips(x, y)
        sibling = (x, y, 1 - c)

        def copy(k, src, dst, to):
            return pltpu.make_async_remote_copy(src_ref=src, dst_ref=dst, send_sem=ssem.at[k], recv_sem=rsem.at[k],
                                                device_id=to, device_id_type=MESH)

        sends, landed, passed_on, tiny_landed = [], [], [], []
        for w in range(nb):
            mine, theirs = _core_halves(bigs[w].shape, c)
            for j, (cx, cy) in enumerate(chips):
                sends.append(copy(6 * w + j, ins[w].at[mine], outs[w].at[(me,) + mine], (cx, cy, c)))
                blk = outs[w].at[(2 * cx + cy,) + mine]
                landed.append((copy(6 * w + j, blk, blk, (cx, cy, c)), copy(6 * w + 3 + j, blk, blk, sibling)))
                blk = outs[w].at[(2 * cx + cy,) + theirs]
                passed_on.append(copy(6 * w + 3 + j, blk, blk, sibling))
        for w in range(nt_):
            for j, (cx, cy) in enumerate(chips):
                k = 6 * nb + 3 * w + j
                sends.append(copy(k, ins[nb + w], outs[nb + w].at[me], (cx, cy, c)))
                blk = outs[nb + w].at[2 * cx + cy]
                tiny_landed.append(copy(k, blk, blk, (cx, cy, c)))
        return sends, landed, passed_on, tiny_landed

    def start(ins, outs, ssem, rsem):
        for cp in plan(ins, outs, ssem, rsem)[0]:
            cp.start()

    def finish(ins, outs, ssem, rsem):
        sends, landed, passed_on, tiny_landed = plan(ins, outs, ssem, rsem)
        for arrived, forward in landed:
            arrived.wait_recv()
            forward.start()
        for arrived in tiny_landed + passed_on:
            arrived.wait_recv()
        for cp in sends + [forward for _, forward in landed]:
            cp.wait_send()

    return _Side(list(bigs) + list(tinies), [SDS((4,) + a.shape, a.dtype) for a in list(bigs) + list(tinies)],
                 6 * nb + 3 * nt_, start, finish)


def _sibling_side(srcs, out_shapes, pick):
    def plan(in_refs, out_refs, ssem, rsem):
        x, y, c = _pos()
        return [pltpu.make_async_remote_copy(src_ref=pick(in_refs[w], srcs[w].shape, c), dst_ref=out_refs[w],
                                             send_sem=ssem.at[w], recv_sem=rsem.at[w], device_id=(x, y, 1 - c),
                                             device_id_type=MESH) for w in range(len(srcs))]

    def start(*refs):
        for cp in plan(*refs):
            cp.start()

    def finish(*refs):
        for cp in plan(*refs):
            cp.wait()

    return _Side(srcs, out_shapes, len(srcs), start, finish)


def swap_side(gs):
    def half_shape(g):
        l, r, cols = g.shape
        return (l, r // 2, cols) if _split_rows((r, cols)) else (l, r, cols // 2)

    return _sibling_side(gs, [SDS(half_shape(g), g.dtype) for g in gs],
                         lambda ref, shape, c: ref.at[(slice(None),) + _core_halves(shape[1:], c)[1]])


def join_side(halves):
    return _sibling_side(halves, [SDS(h.shape, h.dtype) for h in halves], lambda ref, shape, c: ref)


def _both_sides(a, b):
    def split(ins, outs, *scr):
        return ((ins[:a.n_in], outs[:a.n_out], *scr[:2]), (ins[a.n_in:], outs[a.n_out:], *scr[2:]))

    def start(*refs):
        ra, rb = split(*refs)
        a.start(*ra)
        b.start(*rb)

    def finish(*refs):
        ra, rb = split(*refs)
        a.finish(*ra)
        b.finish(*rb)

    side = _Side(a.ins + b.ins, a.out_shapes + b.out_shapes, 1, start, finish)
    side.scratch = a.scratch + b.scratch
    return side


COL_BLOCK = 256


def add_half(sc, g, sib, name):
    l, r, cols = g.shape

    def body(sc_ref, g_ref, s_ref, o_ref):
        o_ref[...] = (g_ref[...].astype(F32) + s_ref[...].astype(F32)).astype(BF)

    if _split_rows((r, cols)):
        rh, br = _half_rows(r)
        nbk = rh // br
        blk = pl.BlockSpec((1, br, cols), lambda j, i, sc_: (j, i, 0))
        mine = pl.BlockSpec((1, br, cols), lambda j, i, sc_: (j, sc_[0] * nbk + i, 0))
        return _pc_sp(body, (l, nbk), [mine, blk], blk, SDS((l, rh, cols), BF), name)(sc, g, sib)
    nbk = cols // 2 // COL_BLOCK
    blk = pl.BlockSpec((1, r, COL_BLOCK), lambda j, i, sc_: (j, 0, i))
    mine = pl.BlockSpec((1, r, COL_BLOCK), lambda j, i, sc_: (j, 0, sc_[0] * nbk + i))
    return _pc_sp(body, (l, nbk), [mine, blk], blk, SDS((l, r, cols // 2), BF), name)(sc, g, sib)


def exchange_side(ps):
    n_ = len(ps)

    def width(p_):
        return p_.shape[2] if p_.shape[0] == 4 else p_.shape[2] // 4

    def plan(p_refs, got_refs, ssem, rsem):
        x, y, c = _pos()
        cps = []
        for w in range(n_):
            wd = width(ps[w])
            for j, (cx, cy) in enumerate(_other_chips(x, y)):
                to = 2 * cx + cy
                src = p_refs[w].at[to] if ps[w].shape[0] == 4 else p_refs[w].at[0, :, pl.ds(pl.multiple_of(to * wd, LANE), wd)]
                cps.append(pltpu.make_async_remote_copy(
                    src_ref=src, dst_ref=got_refs[w].at[j], send_sem=ssem.at[3 * w + j], recv_sem=rsem.at[3 * w + j],
                    device_id=(cx, cy, c), device_id_type=MESH))
        return cps

    def start(*refs):
        for cp in plan(*refs):
            cp.start()

    def finish(*refs):
        for cp in plan(*refs):
            cp.wait()

    return _Side(ps, [SDS((3, p_.shape[1], width(p_)), p_.dtype) for p_ in ps], 3 * n_, start, finish)


def sum4(sc, p, got, name):
    _, rh, wd = got.shape

    def body(sc_ref, p_ref, g_ref, r_ref):
        r_ref[...] = ((p_ref[0].astype(F32) + g_ref[0].astype(F32)) + (g_ref[1].astype(F32) + g_ref[2].astype(F32)))

    if rh % 16:
        assert p.shape[0] == 4
        return _pc_sp(body, (wd // COL_BLOCK,),
                      [pl.BlockSpec((1, rh, COL_BLOCK), lambda i, sc_: (sc_[1], 0, i)),
                       pl.BlockSpec((3, rh, COL_BLOCK), lambda i, sc_: (0, 0, i))],
                      pl.BlockSpec((rh, COL_BLOCK), lambda i, sc_: (0, i)), SDS((rh, wd), F32), name)(sc, p, got)
    _, br = _half_rows(2 * rh)
    own = (pl.BlockSpec((1, br, wd), lambda i, sc_: (sc_[1], i, 0)) if p.shape[0] == 4
           else pl.BlockSpec((1, br, wd), lambda i, sc_: (0, i, sc_[1])))
    return _pc_sp(body, (rh // br,), [own, pl.BlockSpec((3, br, wd), lambda i, sc_: (0, i, 0))],
                  pl.BlockSpec((br, wd), lambda i, sc_: (i, 0)), SDS((rh, wd), F32), name)(sc, p, got)


def allreduce_small(g):
    rows = g.shape[0]
    rh = rows // 2

    def body(g_ref, out_ref, sib_buf, chip_buf, sum_sc, ssem, rsem):
        x, y, c = _pos()
        me = 2 * x + y
        sibling = (x, y, 1 - c)
        mine = pl.ds(pl.multiple_of(c * rh, 8), rh)

        def copy(k, src, dst, to):
            return pltpu.make_async_remote_copy(src_ref=src, dst_ref=dst, send_sem=ssem.at[k], recv_sem=rsem.at[k],
                                                device_id=to, device_id_type=MESH)

        cp = copy(0, g_ref, sib_buf, sibling)
        cp.start()
        cp.wait()
        sum_sc[...] = g_ref[...] + sib_buf[...]
        chips = _other_chips(x, y)
        cps = [copy(1 + j, sum_sc.at[mine], chip_buf.at[me], (cx, cy, c)) for j, (cx, cy) in enumerate(chips)]
        for cp in cps:
            cp.start()
        chip_buf[me] = sum_sc[mine, :]
        for j, (cx, cy) in enumerate(chips):
            copy(1 + j, sum_sc.at[mine], chip_buf.at[2 * cx + cy], (cx, cy, c)).wait_recv()
        for cp in cps:
            cp.wait_send()
        out_ref[mine, :] = (chip_buf[0] + chip_buf[1]) + (chip_buf[2] + chip_buf[3])
        cp = copy(4, out_ref.at[mine], out_ref.at[mine], sibling)
        cp.start()
        cp.wait()

    vm = pl.BlockSpec(memory_space=pltpu.VMEM)
    return _pc(body, name="allreduce_small", in_specs=[vm], out_specs=vm, out_shape=SDS((rows, LANE), F32),
               scratch_shapes=[pltpu.VMEM((rows, LANE), F32), pltpu.VMEM((4, rh, LANE), F32), pltpu.VMEM((rows, LANE), F32),
                               pltpu.SemaphoreType.DMA((5,)), pltpu.SemaphoreType.DMA((5,))],
               compiler_params=pltpu.CompilerParams(vmem_limit_bytes=VMEM_LIMIT))(g)


def _pack_small(entries, get):
    flat = jnp.concatenate([get(n).reshape(-1).astype(F32) for n, _ in entries])
    rows = -(-flat.shape[0] // (8 * LANE)) * 8
    return jnp.pad(flat, (0, rows * LANE - flat.shape[0])).reshape(rows, LANE)


def _unpack_small(entries, packed):
    out, off = {}, 0
    flat = packed.reshape(-1)
    for name, n in entries:
        out[name] = flat[off:off + n]
        off += n
    return out


def _cols_full(blk):
    return blk.transpose(1, 0, 2).reshape(blk.shape[1], 4 * blk.shape[2])


def kernel(x, p, pre_mix_norm, w_in, a_ln_g, a_ln_b, a_spatial_w, a_spatial_b, a_out, b_gk, b_gk_bias, b_out_norm, b_out, w_mix_out, post_mix_norm, pre_ffn_norm, w_up, conv_w, conv_b, w_down, post_ffn_norm, w_ple, w_ple_gate, post_ple_norm, loss_target, m_pre_mix_norm, m_w_in, m_a_ln_g, m_a_ln_b, m_a_spatial_w, m_a_spatial_b, m_a_out, m_b_gk, m_b_gk_bias, m_b_out_norm, m_b_out, m_w_mix_out, m_post_mix_norm, m_pre_ffn_norm, m_w_up, m_conv_w, m_conv_b, m_w_down, m_post_ffn_norm, m_w_ple, m_w_ple_gate, m_post_ple_norm, v_pre_mix_norm, v_w_in, v_a_ln_g, v_a_ln_b, v_a_spatial_w, v_a_spatial_b, v_a_out, v_b_gk, v_b_gk_bias, v_b_out_norm, v_b_out, v_w_mix_out, v_post_mix_norm, v_pre_ffn_norm, v_w_up, v_conv_w, v_conv_b, v_w_down, v_post_ffn_norm, v_w_ple, v_w_ple_gate, v_post_ple_norm):
    args = dict(locals())
    order = ['pre_mix_norm', 'w_in', 'a_ln_g', 'a_ln_b', 'a_spatial_w', 'a_spatial_b', 'a_out', 'b_gk', 'b_gk_bias',
             'b_out_norm', 'b_out', 'w_mix_out', 'post_mix_norm', 'pre_ffn_norm', 'w_up', 'conv_w', 'conv_b', 'w_down',
             'post_ffn_norm', 'w_ple', 'w_ple_gate', 'post_ple_norm']
    assert sorted(BIG + TINY + tuple(n for n, _ in SMALL)) == sorted(order)
    s = x.shape[1]
    xs = x.reshape(s, D)
    ps = p.reshape(s, PLE)
    tgt = loss_target.reshape(s, D)
    t_big = min(1024, s)
    t_mid = min(512, s)
    t_small = min(256, s)
    t_gla = min(256, s)
    mx_, my_, mc_ = _pos()
    me = 2 * mx_ + my_
    sc = jnp.stack([mc_, me]).astype(jnp.int32)
    shard = lambda n: args[n].reshape(args[n].shape[1:])

    mine = {n: shard(n).astype(BF) for n in BIG}
    mine["w_in"] = shard("w_in").T.astype(BF)
    mine.update({n: shard(n) for n in TINY})
    fill = lambda names, gots: {n: lax.dynamic_update_slice(got, mine[n][None], (me, 0, 0)) for n, got in zip(names, gots)}
    first = ("w_in",) + TINY
    full = fill(first, _run_side(gather_side([mine["w_in"]], [mine[n] for n in TINY]), "gather_first"))
    wi = full["w_in"].reshape(4 * 1540, D)
    seg = lambda a, b: wi[a:b]
    qk = [seg(1024 + h * B_HK, 1024 + (h + 1) * B_HK) for h in range(B_H)]
    kk = [seg(1536 + h * B_HK, 1536 + (h + 1) * B_HK) for h in range(B_H)]
    w_z = jnp.concatenate([seg(0, 1024)] + [m_ for h in range(B_H) for m_ in (qk[h], kk[h])]
                          + [seg(2048, 4096), seg(4112, 6160), seg(4096, 4112), jnp.zeros((LANE - B_RANK, D), BF)], axis=0)
    wgk = jnp.pad(_cols_full(full["b_gk"]).astype(BF), ((0, LANE - B_RANK), (0, 0)))
    w_conv = _cols_full(full["conv_w"])
    g1, g2, g3 = pre_mix_norm.reshape(1, D), post_mix_norm.reshape(1, D), pre_ffn_norm.reshape(1, D)
    g4, g5 = post_ffn_norm.reshape(1, D), post_ple_norm.reshape(1, D)
    ln_g, ln_b = a_ln_g.reshape(1, A_W), a_ln_b.reshape(1, A_W)
    w_s = a_spatial_w.reshape(A_G, A_C, A_C)
    w_cat = w_s.transpose(1, 0, 2).reshape(A_C, A_G * A_C)
    w_cat_t = w_s.transpose(2, 0, 1).reshape(A_C, A_G * A_C)
    bias_full = jnp.repeat(a_spatial_b.reshape(A_G, A_C).T, A_GD, axis=1)
    bdm = (jnp.arange(A_G * A_C)[:, None] // A_C == jnp.arange(A_W)[None, :] // A_GD).astype(BF)
    gk_bias = b_gk_bias.reshape(1, B_H * B_HK)
    wn = b_out_norm.reshape(1, B_HV)
    cb = conv_b.reshape(1, 2 * D_FF)
    idx = jnp.arange(t_gla)
    ltri = ((idx[:, None] // B_C == idx[None, :] // B_C) & (idx[None, :] <= idx[:, None])).astype(BF)

    a, z, qk32, zl, *gots = norm_matmul(xs, g1, w_z, D, t_big, "in_proj", nblk=6, f32_blk=1, tail_blk=48,
                                        side=gather_side([mine[n] for n in BIG[1:]], []))
    full.update(fill(BIG[1:], gots))
    w_aout, w_ple_f = _cols_full(full["a_out"]), _cols_full(full["w_ple"])
    w_bout, w_mix, w_pg = (full[n].reshape(D, D) for n in ("b_out", "w_mix_out", "w_ple_gate"))
    w_dn, w_up3 = full["w_down"].reshape(D_FF, D), full["w_up"]
    sa = sgu_fwd(z, ln_g, ln_b, w_cat, bias_full, bdm, t_mid)
    ob, o, states = gla_fwd(z, qk32, zl, wgk, gk_bias, wn, ltri, t_gla)
    ya, yb, mp, mx, h1 = mix_fwd(sa, ob, z, xs, w_aout, w_bout, w_mix, g2, t_mid)
    c, up_g, up_v, cg, cv, ff = ffn_up_fwd(h1, g3, w_up3, w_conv, cb, t_mid)
    f, h2, pg, pe, dy, loss = out_fwd(ff, h1, ps, tgt, w_dn, w_pg, w_ple_f, g4, g5, t_mid)

    dh2, dpe, dpg, df, dff, gg5, gg4 = out_bwd(dy, pg, pe, f, g5, g4, w_pg, w_dn, t_mid)
    dup, gcw, gcb, dh1, gg3 = ffn_up_bwd(up_g, up_v, cg, cv, dff, w_conv, _cols_full(w_up3), h1, g3, dh2, t_small)
    grads = {
        "w_up": mm_tn(c, dup, "dw_up")[None],
        "w_down": mm_tn(ff, df, "dw_down").reshape(4, D_FF // 4, D),
        "w_ple": mm_tn(ps, dpe, "dw_ple")[None],
        "w_ple_gate": mm_tn(h2, dpg, "dw_ple_gate").reshape(4, D // 4, D),
    }
    ffn_side = ("w_up", "w_down", "w_ple", "w_ple_gate")
    dmx, dya, dyb, dga, dgb, dsa, dob, gg2, *sibs = mix_bwd(dh1, mx, z, ya, yb, g2, w_mix, w_aout, w_bout, t_mid,
                                                            side=swap_side([grads[n] for n in ffn_side]))
    sib = dict(zip(ffn_side, sibs))
    duv, g_lng, g_lnb, g_wcat, g_bst = sgu_bwd(z, dsa, ln_g, ln_b, w_cat, w_cat_t, bias_full, bdm, t_mid)
    g_ws = g_wcat.reshape(A_C, A_G, A_C).transpose(1, 0, 2)
    grads.update({
        "a_out": mm_tn(sa, dya, "dw_a_out")[None],
        "b_out": mm_tn(ob, dyb, "dw_b_out").reshape(4, D // 4, D),
        "w_mix_out": mm_tn(mp, dmx, "dw_mix").reshape(4, D // 4, D),
    })

    def swap(names):
        sib.update(zip(names, _run_side(swap_side([grads[n] for n in names]), "swap_halves_" + names[0])))

    swap(("a_out", "b_out", "w_mix_out"))
    parts = {n: add_half(sc, grads[n], sib[n], "partial_" + n) for n in BIG[1:]}
    dqk, dvb, dog, dpre, dlr, g_gkb, g_wn, *gots = gla_bwd(
        z, qk32, zl, o, dob, states, wgk, gk_bias, wn, ltri, ltri.T, t_gla,
        side=exchange_side([parts[n] for n in BIG[1:]]))
    reds = {n: sum4(sc, parts[n], got_, "sum_" + n) for n, got_ in zip(BIG[1:], gots)}
    segs = [duv, dqk, dvb, dog, dga, dgb, dlr]

    gz = [mm_tn(sg_, a, "dw_in_%d" % k) for k, sg_ in enumerate(segs)]
    gq = [gz[1][h * 256:h * 256 + B_HK] for h in range(B_H)]
    gk = [gz[1][h * 256 + B_HK:(h + 1) * 256] for h in range(B_H)]
    g_in = jnp.concatenate([gz[0]] + gq + gk + [gz[2], gz[3], gz[6][:B_RANK], gz[4], gz[5]], axis=0)
    grads["w_in"] = g_in.reshape(4, 1540, D)
    swap(("w_in",))
    parts["w_in"] = add_half(sc, grads["w_in"], sib["w_in"], "partial_w_in")
    dx, gg1, got_in, *sib_reds = nt_normbwd(
        segs, w_z, xs, g1, dh1, t_small, "in_bwd",
        side=_both_sides(exchange_side([parts["w_in"]]), join_side([reds[n] for n in BIG[1:]])))
    sib_red = dict(zip(BIG[1:], sib_reds))

    reds["w_in"] = sum4(sc, parts["w_in"], got_in, "sum_w_in")
    sib_red["w_in"], = _run_side(join_side([reds["w_in"]]), "join_w_in")
    outs = {}
    for n in BIG:
        if n == "w_in":
            res = adamw_cols(sc, reds[n], sib_red[n], shard(n).T, shard("m_" + n).T, shard("v_" + n).T, "adamw_" + n)
            res = [r_.T for r_ in res]
        else:
            res = adamw_halves(sc, reds[n], sib_red[n], shard(n), shard("m_" + n), shard("v_" + n), "adamw_" + n)
        outs[n] = [r_.reshape(args[n].shape) for r_ in res]

    small_g = {
        "pre_mix_norm": gg1, "a_ln_g": g_lng, "a_ln_b": g_lnb, "a_spatial_w": g_ws, "a_spatial_b": g_bst.T,
        "b_gk_bias": g_gkb, "b_out_norm": g_wn, "post_mix_norm": gg2, "pre_ffn_norm": gg3,
        "conv_b": gcb, "post_ffn_norm": gg4, "post_ple_norm": gg5,
        "b_gk": mm_tn(zl, dpre, "dw_gk")[:B_RANK], "conv_w": gcw,
        "loss": loss,
    }
    red_entries = SMALL + (("b_gk", B_RANK * 512), ("conv_w", 3 * 2 * D_FF), ("loss", 1))
    g_fin = _unpack_small(red_entries, allreduce_small(_pack_small(red_entries, lambda n: small_g[n])))
    g_fin["b_gk"] = lax.dynamic_slice(g_fin["b_gk"].reshape(B_RANK, 512), (0, me * B_HK), (B_RANK, B_HK))
    g_fin["conv_w"] = lax.dynamic_slice(g_fin["conv_w"].reshape(3, 2 * D_FF), (0, me * 1408), (3, 1408))
    upd_entries = SMALL + (("b_gk", B_RANK * B_HK), ("conv_w", 3 * 1408))
    res = adamw_small(*[_pack_small(upd_entries, get) for get in
                        (lambda n: g_fin[n], lambda n: args[n], lambda n: args["m_" + n], lambda n: args["v_" + n])])
    res = [_unpack_small(upd_entries, r_) for r_ in res]
    for n, _ in upd_entries:
        outs[n] = [r_[n].reshape(args[n].shape) for r_ in [g_fin] + res]

    return (g_fin["loss"].reshape(()), dx.reshape(x.shape), *[outs[n][0] for n in order], *[outs[n][1] for n in order],
            *[outs[n][2] for n in order], *[outs[n][3] for n in order])
```

```python
import math

import jax
import jax.numpy as jnp
from jax import lax
from jax.experimental import pallas as pl
from jax.experimental.pallas import tpu as pltpu

F32 = jnp.float32
BF = jnp.bfloat16
SDS = jax.ShapeDtypeStruct
MESH = pl.DeviceIdType.MESH

EPS = 1e-6
D = 1024
A_W = 512
A_G, A_C = 8, 128
A_GD = A_W // A_G
B_H, B_HK, B_HV = 4, 128, 256
B_C = 64
GLA_HPB = 4
B_RANK = 16
D_FF = 2816
PLE = 256
LANE = 128
VMEM_LIMIT = 60 * 1024 * 1024

ADAM_LR, ADAM_B1, ADAM_B2, ADAM_EPS, ADAM_WD, ADAM_STEP = 0.001, 0.9, 0.999, 1e-08, 0.01, 10

_GC = math.sqrt(2.0 / math.pi)
_GA = 0.044715

BIG = ("w_in", "a_out", "b_out", "w_mix_out", "w_up", "w_down", "w_ple", "w_ple_gate")
TINY = ("b_gk", "conv_w")
SMALL = (("pre_mix_norm", 1024), ("a_ln_g", 512), ("a_ln_b", 512), ("a_spatial_w", 131072),
         ("a_spatial_b", 1024), ("b_gk_bias", 512), ("b_out_norm", 256), ("post_mix_norm", 1024),
         ("pre_ffn_norm", 1024), ("conv_b", 5632), ("post_ffn_norm", 1024), ("post_ple_norm", 1024))


def _pc(body, **kw):
    return pl.pallas_call(body, **kw)


def _cp(n):
    return pltpu.CompilerParams(dimension_semantics=("arbitrary",) * n, vmem_limit_bytes=VMEM_LIMIT)


def _const(shape):
    nd = len(shape)
    return pl.BlockSpec(shape, lambda *_: (0,) * nd, pipeline_mode=pl.Buffered(1))


def _acc(shape):
    nd = len(shape)
    return pl.BlockSpec(shape, lambda *_: (0,) * nd)


def _dot(a, b):
    return jnp.dot(a, b, preferred_element_type=F32)


def _dot_nt(a, b):
    return lax.dot_general(a, b, (((1,), (1,)), ((), ())), preferred_element_type=F32)


def _dot_tn(a, b):
    return lax.dot_general(a, b, (((0,), (0,)), ((), ())), preferred_element_type=F32)


def _gelu(x):
    return 0.5 * x * (1.0 + jnp.tanh(_GC * (x + _GA * x * x * x)))


def _gelu_and_grad(x):
    x2 = x * x
    s = 0.5 * jnp.tanh((_GC * x) * (1.0 + _GA * x2)) + 0.5
    g = x * s
    return g, s + g * (1.0 - s) * ((6.0 * _GC * _GA) * x2 + 2.0 * _GC)


def _log_sigmoid(x):
    return jnp.minimum(x, 0.0) - jnp.log(1.0 + jnp.exp(-jnp.abs(x)))


def _rms(x, g):
    return x * lax.rsqrt(jnp.mean(x * x, axis=-1, keepdims=True) + EPS) * g


def _rms_bwd(dy, x, g):
    r = lax.rsqrt(jnp.mean(x * x, axis=-1, keepdims=True) + EPS)
    n = x * r
    dn = dy * g
    dx = r * (dn - n * jnp.mean(dn * n, axis=-1, keepdims=True))
    return dx, jnp.sum(dy * n, axis=0, keepdims=True)


def _ldot3(l, x):
    h = x.astype(BF)
    r = x - h.astype(F32)
    m = r.astype(BF)
    lo = (r - m.astype(F32)).astype(BF)
    return _dot(l, h) + _dot(l, m) + _dot(l, lo)


def _split_side(refs, n_in, n_out, n_scratch, side):
    si, so = (side.n_in, side.n_out) if side else (0, 0)
    cuts = [n_in, si, n_out, so, n_scratch]
    out, at = [], 0
    for c in cuts:
        out.append(refs[at:at + c])
        at += c
    return (*out, refs[at:])


def _side_specs(side):
    return ([_ANY] * side.n_in, [_ANY] * side.n_out, side.out_shapes, side.scratch, side.ins) if side else ([],) * 5


def norm_matmul(x, g, wt, bn, t, name, nblk, f32_blk, tail_blk, side=None):
    s, dm = x.shape
    w_spec = pl.BlockSpec((bn, dm), lambda i, j: (j, 0))
    nt = s // t

    def body(*refs):
        (x_ref, g_ref, w_ref, wl_ref), s_in, outs, s_out, (a_sc,), s_scr = _split_side(refs, 4, 4, 1, side)
        a_ref, z_ref, f32_ref, tail_ref = outs
        i, j = pl.program_id(0), pl.program_id(1)
        if side:
            @pl.when((i == 0) & (j == 0))
            def _():
                side.start(s_in, s_out, *s_scr)

        @pl.when(j == 0)
        def _():
            a = _rms(x_ref[...], g_ref[...]).astype(BF)
            a_sc[...] = a
            a_ref[...] = a
            tail_ref[...] = _dot_nt(a, wl_ref[...]).astype(BF)

        acc = _dot_nt(a_sc[...], w_ref[...])
        z_ref[...] = acc.astype(BF)

        @pl.when(j == f32_blk)
        def _():
            f32_ref[...] = acc
        if side:
            @pl.when((i == nt - 1) & (j == nblk - 1))
            def _():
                side.finish(s_in, s_out, *s_scr)

    si_specs, so_specs, so_shapes, s_scratch, s_ins = _side_specs(side)
    return _pc(
        body, name=name, grid=(nt, nblk),
        in_specs=[pl.BlockSpec((t, dm), lambda i, j: (i, 0)), _const((1, dm)), w_spec,
                  pl.BlockSpec((LANE, dm), lambda i, j: (tail_blk, 0), pipeline_mode=pl.Buffered(1))] + si_specs,
        out_specs=[pl.BlockSpec((t, dm), lambda i, j: (i, 0)), pl.BlockSpec((t, bn), lambda i, j: (i, j)),
                   pl.BlockSpec((t, bn), lambda i, j: (i, 0)), pl.BlockSpec((t, LANE), lambda i, j: (i, 0))] + so_specs,
        out_shape=[SDS((s, dm), BF), SDS((s, nblk * bn), BF), SDS((s, bn), F32), SDS((s, LANE), BF)] + so_shapes,
        scratch_shapes=[pltpu.VMEM((t, dm), BF)] + s_scratch, compiler_params=_cp(2))(x, g, wt, wt, *s_ins)


def _sgu_weights(wc_ref, transposed):
    r = lax.broadcasted_iota(jnp.int32, (A_C, A_G * A_C), 0)
    c = lax.broadcasted_iota(jnp.int32, (A_C, A_G * A_C), 1) & (A_C - 1)
    return jnp.where((r <= c) if transposed else (c <= r), wc_ref[...], 0.0).astype(BF)


def _sgu_spread(xs, bdm):
    return jnp.concatenate([jnp.tile(x, (A_G, 1)) * bdm for x in xs], axis=1)


def _sgu_recompute(v, lng, lnb):
    gv, dgv = _gelu_and_grad(v)
    gv = gv.astype(F32)
    mu = jnp.mean(gv, axis=-1, keepdims=True)
    xc = gv - mu
    rstd = lax.rsqrt(jnp.mean(xc * xc, axis=-1, keepdims=True) + EPS)
    xhat = xc * rstd
    return dgv, rstd, xhat, (xhat * lng + lnb).astype(BF)


def sgu_fwd(z, ln_g, ln_b, w_cat, bias_full, bdm, t):
    s = z.shape[0]
    nch = t // A_C

    def body(u_ref, v_ref, g_ref, b_ref, wc_ref, bias_ref, bdm_ref, sa_ref):
        vns = [_sgu_recompute(v_ref[pl.ds(ci * A_C, A_C), :], g_ref[...], b_ref[...])[3]
               for ci in range(nch)]
        mixed = _dot(_sgu_weights(wc_ref, False), _sgu_spread(vns, bdm_ref[...]))
        for ci in range(nch):
            rows = pl.ds(ci * A_C, A_C)
            s_ = mixed[:, ci * A_W:(ci + 1) * A_W] + bias_ref[...]
            sa_ref[rows, :] = _gelu(u_ref[rows, :]) * s_.astype(BF)

    return _pc(
        body, name="sgu_fwd", grid=(s // t,),
        in_specs=[pl.BlockSpec((t, A_W), lambda i: (i, 0)), pl.BlockSpec((t, A_W), lambda i: (i, 1)),
                  _const((1, A_W)), _const((1, A_W)), _const((A_C, A_G * A_C)), _const((A_C, A_W)),
                  _const((A_G * A_C, A_W))],
        out_specs=pl.BlockSpec((t, A_W), lambda i: (i, 0)),
        out_shape=SDS((s, A_W), BF), compiler_params=_cp(1))(z, z, ln_g, ln_b, w_cat, bias_full, bdm)


def _gla_decays(qk, lr, wgk, bias, l, t):
    nc = t // B_C
    q = qk[:, :B_HK].astype(F32) * (B_HK ** -0.5)
    k = qk[:, B_HK:].astype(F32)
    pre = _dot(lr, wgk) + bias
    la = _log_sigmoid(pre) * (1.0 / 16.0)
    b = _ldot3(l, la)
    b3 = b.reshape(nc, B_C, B_HK)
    bl = jnp.broadcast_to(b3[:, B_C - 1:B_C, :], (nc, B_C, B_HK)).reshape(t, B_HK)
    eb, enb, etb = jnp.exp(b), jnp.exp(-b), jnp.exp(bl - b)
    return pre, b, bl, eb, enb, etb, q * eb, k * enb, k * etb


def gla_fwd(z, qk32, zl, wgk, bias, wn, ltri, t):
    s = z.shape[0]
    nc = t // B_C
    hpb = GLA_HPB
    assert hpb == B_H
    kw, vw = hpb * B_HK, hpb * B_HV

    def body(qk_ref, v_ref, og_ref, lr_ref, wgk_ref, bias_ref, wn_ref, l_ref, ob_ref, o_ref, st_ref, st_sc, o_sc):
        @pl.when(pl.program_id(0) == 0)
        def _():
            st_sc[...] = jnp.zeros((B_H, B_HV, B_HK), F32)

        lr, l = lr_ref[...], l_ref[...]
        for hh in range(hpb):
            h = hh
            cv, ck = slice(hh * B_HV, (hh + 1) * B_HV), slice(hh * B_HK, (hh + 1) * B_HK)
            _, _, bl, _, _, _, qd, ki, kt = _gla_decays(qk_ref[:, cv], lr, wgk_ref[:, ck], bias_ref[:, ck], l, t)
            qd, ki, kt = qd.astype(BF), ki.astype(BF), kt.astype(BF)
            vb = v_ref[:, cv]
            sc = jnp.where(l > 0, _dot_nt(qd, ki), 0.0).astype(BF)
            o_sc[hh] = _dot(sc, vb)
            for n in range(nc):
                rows = slice(n * B_C, (n + 1) * B_C)
                st = st_sc[h]
                stb = st.astype(BF)
                st_ref[n, hh] = stb
                o_sc[hh, rows, :] += _dot_nt(qd[rows], stb)
                st_sc[h] = st * jnp.exp(bl[n * B_C:n * B_C + 1, :]) + _dot_tn(vb[rows], kt[rows])
            ob = o_sc[hh].astype(BF)
            o_ref[:, cv] = ob
            og = og_ref[:, cv].astype(F32)
            ob_ref[:, cv] = (_rms(ob.astype(F32), wn_ref[...]) * og * jax.nn.sigmoid(og)).astype(BF)

    vo, go = 2048 // vw, 3072 // vw
    return _pc(
        body, name="gla_fwd", grid=(s // t, B_H // hpb),
        in_specs=[pl.BlockSpec((t, vw), lambda i, g: (i, g)), pl.BlockSpec((t, vw), lambda i, g: (i, vo + g)),
                  pl.BlockSpec((t, vw), lambda i, g: (i, go + g)), pl.BlockSpec((t, LANE), lambda i, g: (i, 0)),
                  pl.BlockSpec((LANE, kw), lambda i, g: (0, g)), pl.BlockSpec((1, kw), lambda i, g: (0, g)),
                  _const((1, B_HV)), _const((t, t))],
        out_specs=[pl.BlockSpec((t, vw), lambda i, g: (i, g)), pl.BlockSpec((t, vw), lambda i, g: (i, g)),
                   pl.BlockSpec((nc, hpb, B_HV, B_HK), lambda i, g: (i, g, 0, 0))],
        out_shape=[SDS((s, D), BF), SDS((s, D), BF), SDS((s // B_C, B_H, B_HV, B_HK), BF)],
        scratch_shapes=[pltpu.VMEM((B_H, B_HV, B_HK), F32), pltpu.VMEM((hpb, t, B_HV), F32)],
        compiler_params=_cp(2))(qk32, z, z, zl, wgk, bias, wn, ltri)


def mix_fwd(sa, ob, z, x, a_out, b_out, w_mix, g2, t):
    s = x.shape[0]

    def body(sa_ref, ob_ref, ga_ref, gb_ref, x_ref, ao_ref, bo_ref, wm_ref, g2_ref,
             ya_ref, yb_ref, mp_ref, mx_ref, h1_ref):
        ya = _dot(sa_ref[...], ao_ref[...]).astype(BF)
        yb = _dot(ob_ref[...], bo_ref[...]).astype(BF)
        ya_ref[...] = ya
        yb_ref[...] = yb
        mp = (jax.nn.sigmoid(ga_ref[...].astype(F32)) * ya.astype(F32)
              + jax.nn.sigmoid(gb_ref[...].astype(F32)) * yb.astype(F32)).astype(BF)
        mp_ref[...] = mp
        mx = _dot(mp, wm_ref[...]).astype(BF)
        mx_ref[...] = mx
        h1_ref[...] = x_ref[...] + _rms(mx.astype(F32), g2_ref[...])

    row = lambda w: pl.BlockSpec((t, w), lambda i: (i, 0))
    return _pc(
        body, name="mix_fwd", grid=(s // t,),
        in_specs=[row(A_W), row(D), pl.BlockSpec((t, D), lambda i: (i, 4)), pl.BlockSpec((t, D), lambda i: (i, 5)),
                  row(D), _const((A_W, D)), _const((D, D)), _const((D, D)), _const((1, D))],
        out_specs=[row(D)] * 5,
        out_shape=[SDS((s, D), BF)] * 4 + [SDS((s, D), F32)],
        compiler_params=_cp(1))(sa, ob, z, z, x, a_out, b_out, w_mix, g2)


def ffn_up_fwd(h1, g3, w_up3, conv_w, conv_b, t):
    s = h1.shape[0]
    bn = w_up3.shape[2]

    def body(x_ref, g_ref, wg_ref, wv_ref, cwg_ref, cwv_ref, cbg_ref, cbv_ref,
             c_ref, ug_ref, uv_ref, cg_ref, cv_ref, ff_ref, c_sc, carry):
        i, j = pl.program_id(0), pl.program_id(1)

        @pl.when(j == 0)
        def _():
            c = _rms(x_ref[...], g_ref[...]).astype(BF)
            c_sc[...] = c
            c_ref[...] = c

        @pl.when(i == 0)
        def _():
            carry[j] = jnp.zeros((2, 8, bn), F32)

        def branch(k, w_ref, cw_ref, cb_ref, u_ref, o_ref):
            ub = _dot(c_sc[...], w_ref[...]).astype(BF)
            u_ref[...] = ub
            u = ub.astype(F32)
            ext = jnp.concatenate([carry[j, k], u], axis=0)
            carry[j, k] = u[t - 8:]
            w = cw_ref[...]
            cc = (cb_ref[...] + w[0:1] * pltpu.roll(ext, 2, 0) + w[1:2] * pltpu.roll(ext, 1, 0) + w[2:3] * ext)[8:]
            cc = cc.astype(BF)
            o_ref[...] = cc
            return cc

        g = _gelu(branch(0, wg_ref, cwg_ref, cbg_ref, ug_ref, cg_ref))
        ff_ref[...] = g * branch(1, wv_ref, cwv_ref, cbv_ref, uv_ref, cv_ref)

    col = lambda rows, off: pl.BlockSpec((rows, bn), lambda i, j: (0, j + off))
    out = pl.BlockSpec((t, bn), lambda i, j: (i, j))
    return _pc(
        body, name="ffn_up_fwd", grid=(s // t, 2),
        in_specs=[pl.BlockSpec((t, D), lambda i, j: (i, 0)), _const((1, D)),
                  pl.BlockSpec((None, D, bn), lambda i, j: (j, 0, 0)), pl.BlockSpec((None, D, bn), lambda i, j: (j + 2, 0, 0)),
                  col(3, 0), col(3, 2), col(1, 0), col(1, 2)],
        out_specs=[pl.BlockSpec((t, D), lambda i, j: (i, 0))] + [out] * 5,
        out_shape=[SDS((s, D), BF)] + [SDS((s, D_FF), BF)] * 5,
        scratch_shapes=[pltpu.VMEM((t, D), BF), pltpu.VMEM((2, 2, 8, bn), F32)],
        compiler_params=_cp(2))(h1, g3, w_up3, w_up3, conv_w, conv_w, conv_b, conv_b)


def ffn_up_bwd(ug, uv, cg, cv, dff, conv_w, w_up, h1, g3, dh2, t):
    s = h1.shape[0]
    nt = s // t
    hb = t // 8
    bn = 1408
    r = t + 8

    def body(ug_ref, uv_ref, cg_ref, cv_ref, cag_ref, cav_ref, d_ref, da_ref, cw_ref, w_ref, x_ref, g_ref, dres_ref,
             du_ref, gw_ref, gb_ref, dx_ref, gg_ref):
        i = pl.program_id(0)

        @pl.when(i == 0)
        def _():
            gw_ref[...] = jnp.zeros((3, 2 * D_FF), F32)
            gb_ref[...] = jnp.zeros((1, 2 * D_FF), F32)
            gg_ref[...] = jnp.zeros((1, D), F32)

        more = (i < nt - 1).astype(BF)

        def gate(c_g, c_v, d_):
            gl, dgl = _gelu_and_grad(c_g)
            return (d_ * c_v * dgl).astype(F32), (d_ * gl).astype(F32)

        def back(dc, dc_next, u_ref, cols, off):
            w = cw_ref[:, off:off + bn]
            d_ext = jnp.concatenate([dc, dc_next], axis=0)
            d1, d2 = pltpu.roll(d_ext, r - 1, 0)[:t], pltpu.roll(d_ext, r - 2, 0)[:t]
            du_ref[:, off:off + bn] = (w[2:3] * dc + w[1:2] * d1 + w[0:1] * d2).astype(BF)
            u = u_ref[:, cols].astype(F32)
            gw_ref[0:1, off:off + bn] += jnp.sum(d2 * u, axis=0, keepdims=True)
            gw_ref[1:2, off:off + bn] += jnp.sum(d1 * u, axis=0, keepdims=True)
            gw_ref[2:3, off:off + bn] += jnp.sum(dc * u, axis=0, keepdims=True)
            gb_ref[:, off:off + bn] += jnp.sum(dc, axis=0, keepdims=True)

        for kb in range(D_FF // bn):
            cols = slice(kb * bn, (kb + 1) * bn)
            dg, dv = gate(cg_ref[:, cols], cv_ref[:, cols], d_ref[:, cols])
            dg_n, dv_n = gate(cag_ref[:, cols], cav_ref[:, cols], da_ref[:, cols] * more)
            back(dg, dg_n, ug_ref, cols, kb * bn)
            back(dv, dv_n, uv_ref, cols, D_FF + kb * bn)

        acc = _dot_nt(du_ref[...], w_ref[...])
        dxn, dg3 = _rms_bwd(acc, x_ref[...], g_ref[...])
        dx_ref[...] = dres_ref[...] + dxn
        gg_ref[...] += dg3

    tile = lambda width: pl.BlockSpec((t, width), lambda i: (i, 0))
    after = pl.BlockSpec((8, D_FF), lambda i: (jnp.minimum((i + 1) * hb, nt * hb - 1), 0))
    return _pc(
        body, name="ffn_up_bwd", grid=(nt,),
        in_specs=[tile(D_FF)] * 4 + [after, after, tile(D_FF), after, _const((3, 2 * D_FF)), _const((D, 2 * D_FF)),
                                     tile(D), _const((1, D)), tile(D)],
        out_specs=[tile(2 * D_FF), _acc((3, 2 * D_FF)), _acc((1, 2 * D_FF)), tile(D), _acc((1, D))],
        out_shape=[SDS((s, 2 * D_FF), BF), SDS((3, 2 * D_FF), F32), SDS((1, 2 * D_FF), F32),
                   SDS((s, D), F32), SDS((1, D), F32)],
        compiler_params=_cp(1))(ug, uv, cg, cv, cg, cv, dff, dff, conv_w, w_up, h1, g3, dh2)


def out_fwd(ff, h1, p, tgt, w_down, w_pg, w_ple, g4, g5, t):
    s = h1.shape[0]

    def body(ff_ref, h1_ref, p_ref, t_ref, wd_ref, wpg_ref, wpl_ref, g4_ref, g5_ref,
             f_ref, h2_ref, pg_ref, pe_ref, dy_ref, loss_ref):
        @pl.when(pl.program_id(0) == 0)
        def _():
            loss_ref[...] = jnp.zeros((1, 1), F32)

        f = _dot(ff_ref[...], wd_ref[...]).astype(BF)
        f_ref[...] = f
        h2 = h1_ref[...] + _rms(f.astype(F32), g4_ref[...])
        h2b = h2.astype(BF)
        h2_ref[...] = h2b
        pg = _dot(h2b, wpg_ref[...]).astype(BF)
        pe = _dot(p_ref[...].astype(BF), wpl_ref[...]).astype(BF)
        pg_ref[...] = pg
        pe_ref[...] = pe
        y = h2 + _rms(jax.nn.sigmoid(pg.astype(F32)) * pe.astype(F32), g5_ref[...])
        err = y - t_ref[...]
        dy_ref[...] = err * (1.0 / D)
        loss_ref[...] += (0.5 / D) * jnp.sum(err * err)

    row = lambda w: pl.BlockSpec((t, w), lambda i: (i, 0))
    return _pc(
        body, name="out_fwd", grid=(s // t,),
        in_specs=[row(D_FF), row(D), row(PLE), row(D), _const((D_FF, D)), _const((D, D)), _const((PLE, D)),
                  _const((1, D)), _const((1, D))],
        out_specs=[row(D)] * 5 + [_acc((1, 1))],
        out_shape=[SDS((s, D), BF)] * 4 + [SDS((s, D), F32), SDS((1, 1), F32)],
        compiler_params=_cp(1))(ff, h1, p, tgt, w_down, w_pg, w_ple, g4, g5)


def out_bwd(dy, pg, pe, f, g5, g4, w_pg, w_down, t):
    s = dy.shape[0]

    def body(dy_ref, pg_ref, pe_ref, f_ref, g5_ref, g4_ref, wpg_ref, wd_ref,
             dh2_ref, dpe_ref, dpg_ref, df_ref, dff_ref, gg5_ref, gg4_ref):
        @pl.when(pl.program_id(0) == 0)
        def _():
            gg5_ref[...] = jnp.zeros((1, D), F32)
            gg4_ref[...] = jnp.zeros((1, D), F32)

        dy_ = dy_ref[...]
        pg_ = pg_ref[...].astype(F32)
        pe_ = pe_ref[...].astype(F32)
        sg = jax.nn.sigmoid(pg_)
        dple, dg5 = _rms_bwd(dy_, sg * pe_, g5_ref[...])
        gg5_ref[...] += dg5
        dpe_ref[...] = (dple * sg).astype(BF)
        dpg = (dple * pe_ * sg * (1.0 - sg)).astype(BF)
        dpg_ref[...] = dpg
        dh2 = dy_ + _dot_nt(dpg, wpg_ref[...])
        dh2_ref[...] = dh2
        df, dg4 = _rms_bwd(dh2, f_ref[...].astype(F32), g4_ref[...])
        gg4_ref[...] += dg4
        dfb = df.astype(BF)
        df_ref[...] = dfb
        dff_ref[...] = _dot_nt(dfb, wd_ref[...]).astype(BF)

    row = lambda w: pl.BlockSpec((t, w), lambda i: (i, 0))
    return _pc(
        body, name="out_bwd", grid=(s // t,),
        in_specs=[row(D), row(D), row(D), row(D), _const((1, D)), _const((1, D)), _const((D, D)), _const((D_FF, D))],
        out_specs=[row(D), row(D), row(D), row(D), row(D_FF), _acc((1, D)), _acc((1, D))],
        out_shape=[SDS((s, D), F32), SDS((s, D), BF), SDS((s, D), BF), SDS((s, D), BF), SDS((s, D_FF), BF),
                   SDS((1, D), F32), SDS((1, D), F32)],
        compiler_params=_cp(1))(dy, pg, pe, f, g5, g4, w_pg, w_down)


def nt_normbwd(dys, w, xin, gain, dres, t, name, side=None):
    s = xin.shape[0]
    nt = s // t
    np_ = len(dys)

    def body(*refs):
        ins_, s_in, (dx_ref, gg_ref), s_out, _, s_scr = _split_side(refs, np_ + 4, 2, 0, side)
        dy_refs = ins_[:np_]
        w_ref, x_ref, g_ref, dres_ref = ins_[np_:]
        i = pl.program_id(0)

        @pl.when(i == 0)
        def _():
            gg_ref[...] = jnp.zeros((1, D), F32)
            if side:
                side.start(s_in, s_out, *s_scr)

        acc = _dot(jnp.concatenate([r_[...] for r_ in dy_refs], axis=1), w_ref[...])
        dxn, dg = _rms_bwd(acc, x_ref[...], g_ref[...])
        dx_ref[...] = dres_ref[...] + dxn
        gg_ref[...] += dg
        if side:
            @pl.when(i == nt - 1)
            def _():
                side.finish(s_in, s_out, *s_scr)

    row = lambda width: pl.BlockSpec((t, width), lambda i: (i, 0))
    si_specs, so_specs, so_shapes, s_scratch, s_ins = _side_specs(side)
    assert sum(dy.shape[1] for dy in dys) == w.shape[0]
    return _pc(
        body, name=name, grid=(nt,),
        in_specs=[row(dy.shape[1]) for dy in dys] + [_const(w.shape), row(D), _const((1, D)), row(D)] + si_specs,
        out_specs=[row(D), _acc((1, D))] + so_specs,
        out_shape=[SDS((s, D), F32), SDS((1, D), F32)] + so_shapes, scratch_shapes=s_scratch,
        compiler_params=_cp(1))(*dys, w, xin, gain, dres, *s_ins)


def mix_bwd(dh1, mx, z, ya, yb, g2, w_mix, a_out, b_out, t, side=None):
    s = dh1.shape[0]
    nt = s // t

    def body(*refs):
        ins_, s_in, outs_, s_out, _, s_scr = _split_side(refs, 10, 8, 0, side)
        dh_ref, mx_ref, ga_ref, gb_ref, ya_ref, yb_ref, g2_ref, wm_ref, ao_ref, bo_ref = ins_
        dmx_ref, dya_ref, dyb_ref, dga_ref, dgb_ref, dsa_ref, dob_ref, gg2_ref = outs_

        @pl.when(pl.program_id(0) == 0)
        def _():
            gg2_ref[...] = jnp.zeros((1, D), F32)
            if side:
                side.start(s_in, s_out, *s_scr)

        dmx, dg2 = _rms_bwd(dh_ref[...], mx_ref[...].astype(F32), g2_ref[...])
        gg2_ref[...] += dg2
        dmxb = dmx.astype(BF)
        dmx_ref[...] = dmxb
        dmp = _dot_nt(dmxb, wm_ref[...]).astype(BF)

        def gate(g_ref, y_ref, dy_ref, dg_ref, w_ref, dz_ref):
            sg = jax.nn.sigmoid(g_ref[...])
            dyb_ = dmp * sg
            dy_ref[...] = dyb_
            dg_ref[...] = dyb_ * y_ref[...] * (1.0 - sg)
            dz_ref[...] = _dot_nt(dyb_, w_ref[...]).astype(BF)

        gate(ga_ref, ya_ref, dya_ref, dga_ref, ao_ref, dsa_ref)
        gate(gb_ref, yb_ref, dyb_ref, dgb_ref, bo_ref, dob_ref)
        if side:
            @pl.when(pl.program_id(0) == nt - 1)
            def _():
                side.finish(s_in, s_out, *s_scr)

    row = lambda w: pl.BlockSpec((t, w), lambda i: (i, 0))
    si_specs, so_specs, so_shapes, s_scratch, s_ins = _side_specs(side)
    return _pc(
        body, name="mix_bwd", grid=(nt,),
        in_specs=[row(D), row(D), pl.BlockSpec((t, D), lambda i: (i, 4)), pl.BlockSpec((t, D), lambda i: (i, 5)),
                  row(D), row(D), _const((1, D)), _const((D, D)), _const((A_W, D)), _const((D, D))] + si_specs,
        out_specs=[row(D)] * 5 + [row(A_W), row(D), _acc((1, D))] + so_specs,
        out_shape=[SDS((s, D), BF)] * 5 + [SDS((s, A_W), BF), SDS((s, D), BF), SDS((1, D), F32)] + so_shapes,
        scratch_shapes=s_scratch,
        compiler_params=_cp(1))(dh1, mx, z, z, ya, yb, g2, w_mix, a_out, b_out, *s_ins)


def sgu_bwd(z, dsa, ln_g, ln_b, w_cat, w_cat_t, bias_full, bdm, t):
    s = z.shape[0]
    nt = s // t
    nch = t // A_C

    def body(u_ref, v_ref, dsa_ref, g_ref, b_ref, wc_ref, wct_ref, bias_ref, bdm_ref,
             duv_ref, glg_ref, glb_ref, gws_ref, gbs_ref, ds_acc):
        i = pl.program_id(0)

        @pl.when(i == 0)
        def _():
            glg_ref[...] = jnp.zeros((1, A_W), F32)
            glb_ref[...] = jnp.zeros((1, A_W), F32)
            gws_ref[...] = jnp.zeros((A_C, A_G * A_C), F32)
            ds_acc[...] = jnp.zeros((A_C, A_W), F32)

        lng, bdm_ = g_ref[...], bdm_ref[...]
        rec = [_sgu_recompute(v_ref[pl.ds(ci * A_C, A_C), :], lng, b_ref[...]) for ci in range(nch)]
        spread_vn = _sgu_spread([r_[3] for r_ in rec], bdm_)
        mixed = _dot(_sgu_weights(wc_ref, False), spread_vn)
        dsas, dss, dgus = [], [], []
        for ci in range(nch):
            rows = pl.ds(ci * A_C, A_C)
            gu, dgu = _gelu_and_grad(u_ref[rows, :])
            dsa_ = dsa_ref[rows, :]
            ds = dsa_ * gu
            ds_acc[...] += ds.astype(F32)
            dsas.append(dsa_)
            dgus.append(dgu)
            dss.append(ds)
        r = lax.broadcasted_iota(jnp.int32, (A_C, A_G * A_C), 0)
        c = lax.broadcasted_iota(jnp.int32, (A_C, A_G * A_C), 1) & (A_C - 1)
        gws_ref[...] += jnp.where(c <= r, _dot_nt(jnp.concatenate(dss, axis=1), spread_vn), 0.0)
        dvns = _dot(_sgu_weights(wct_ref, True), _sgu_spread(dss, bdm_))
        for ci in range(nch):
            rows = pl.ds(ci * A_C, A_C)
            dgv, rstd, xhat, _ = rec[ci]
            dvn = dvns[:, ci * A_W:(ci + 1) * A_W]
            glb_ref[...] += jnp.sum(dvn, axis=0, keepdims=True)
            glg_ref[...] += jnp.sum(dvn * xhat, axis=0, keepdims=True)
            dxh = dvn * lng
            dgv_ = rstd * (dxh - jnp.mean(dxh, axis=-1, keepdims=True)
                           - xhat * jnp.mean(dxh * xhat, axis=-1, keepdims=True))
            s_ = mixed[:, ci * A_W:(ci + 1) * A_W] + bias_ref[...]
            duv_ref[rows, :A_W] = dsas[ci] * dgus[ci] * s_.astype(BF)
            duv_ref[rows, A_W:] = (dgv_ * dgv).astype(BF)

        @pl.when(i == nt - 1)
        def _():
            acc = ds_acc[...]
            for g in range(A_G):
                gbs_ref[:, g:g + 1] = jnp.sum(acc[:, g * A_GD:(g + 1) * A_GD], axis=1, keepdims=True)

    return _pc(
        body, name="sgu_bwd", grid=(nt,),
        in_specs=[pl.BlockSpec((t, A_W), lambda i: (i, 0)), pl.BlockSpec((t, A_W), lambda i: (i, 1)),
                  pl.BlockSpec((t, A_W), lambda i: (i, 0)),
                  _const((1, A_W)), _const((1, A_W)), _const((A_C, A_G * A_C)), _const((A_C, A_G * A_C)),
                  _const((A_C, A_W)), _const((A_G * A_C, A_W))],
        out_specs=[pl.BlockSpec((t, D), lambda i: (i, 0)), _acc((1, A_W)), _acc((1, A_W)),
                   _acc((A_C, A_G * A_C)), _acc((A_C, A_G))],
        out_shape=[SDS((s, D), BF), SDS((1, A_W), F32), SDS((1, A_W), F32), SDS((A_C, A_G * A_C), F32),
                   SDS((A_C, A_G), F32)],
        scratch_shapes=[pltpu.VMEM((A_C, A_W), F32)],
        compiler_params=_cp(1))(z, z, dsa, ln_g, ln_b, w_cat, w_cat_t, bias_full, bdm)


def gla_bwd(z, qk32, zl, o, dob, states, wgk, bias, wn, ltri, ltri_t, t, side=None):
    s = z.shape[0]
    nt = s // t
    nc = t // B_C
    hpb = GLA_HPB
    assert hpb == B_H
    kw, vw = hpb * B_HK, hpb * B_HV

    def body(*refs):
        ins_, s_in, outs_, s_out, scr_, s_scr = _split_side(refs, 12, 7, 5, side)
        qk_ref, v_ref, og_ref, lr_ref, o_ref, dob_ref, st_ref, wgk_ref, bias_ref, wn_ref, l_ref, lt_ref = ins_
        dqk_ref, dv_ref, dog_ref, dpre_ref, dlr_ref, gbias_ref, gwn_ref = outs_
        dst_sc, dv_sc, dqd_sc, dkt_sc, ddec_sc = scr_
        i = pl.program_id(0)
        g = pl.program_id(1)

        @pl.when((i == 0) & (g == 0))
        def _():
            gbias_ref[...] = jnp.zeros((B_H, 1, B_HK), F32)
            gwn_ref[...] = jnp.zeros((1, B_HV), F32)
            if side:
                side.start(s_in, s_out, *s_scr)

        @pl.when(i == 0)
        def _():
            dst_sc[...] = jnp.zeros((B_H, B_HV, B_HK), F32)

        lr, l, lt = lr_ref[...], l_ref[...], lt_ref[...]
        keep, keep_t = l > 0, lt > 0
        wn_ = wn_ref[...]
        last = lax.broadcasted_iota(jnp.int32, (nc, B_C, B_HK), 1) == B_C - 1
        for hh in range(hpb):
            h = hh
            cv, ck = slice(hh * B_HV, (hh + 1) * B_HV), slice(hh * B_HK, (hh + 1) * B_HK)
            pre, b, bl, eb, enb, etb, qd, ki, kt = _gla_decays(qk_ref[:, cv], lr, wgk_ref[:, ck], bias_ref[:, ck], l, t)
            qdb, kib, ktb = qd.astype(BF), ki.astype(BF), kt.astype(BF)
            vb = v_ref[:, cv]
            o_ = o_ref[:, cv].astype(F32)
            og = og_ref[:, cv].astype(F32)
            sog = jax.nn.sigmoid(og)
            dob_ = dob_ref[:, cv].astype(F32)
            don = dob_ * og * sog
            do, dwn = _rms_bwd(don, o_, wn_)
            gwn_ref[...] += dwn
            dog_ref[:, cv] = (dob_ * _rms(o_, wn_) * sog * (1.0 + og * (1.0 - sog))).astype(BF)
            dob16 = do.astype(BF)
            sc_t = jnp.where(keep_t, _dot_nt(kib, qdb), 0.0).astype(BF)
            dsc = jnp.where(keep, _dot_nt(dob16, vb), 0.0).astype(BF)
            dsc_t = jnp.where(keep_t, _dot_nt(vb, dob16), 0.0).astype(BF)
            dv_sc[hh] = _dot(sc_t, dob16)
            dqd_sc[hh] = _dot(dsc, kib)
            dki = _dot(dsc_t, qdb)
            for n in reversed(range(nc)):
                rows = slice(n * B_C, (n + 1) * B_C)
                dst = dst_sc[h]
                dstb = dst.astype(BF)
                stp = st_ref[n, hh]
                dv_sc[hh, rows, :] += _dot_nt(ktb[rows], dstb)
                dkt_sc[hh, rows, :] = _dot(vb[rows], dstb)
                dqd_sc[hh, rows, :] += _dot(dob16[rows], stp)
                dec = jnp.exp(bl[n * B_C:n * B_C + 1, :])
                ddec_sc[hh, n] = jnp.sum(dst * stp.astype(F32), axis=0, keepdims=True) * dec
                dst_sc[h] = dst * dec + _dot_tn(dob16[rows], qdb[rows])
            dqd, dkt = dqd_sc[hh], dkt_sc[hh]
            dv_ref[:, cv] = dv_sc[hh].astype(BF)
            dqk_ref[:, hh * B_HV:hh * B_HV + B_HK] = (dqd * eb * (B_HK ** -0.5)).astype(BF)
            dqk_ref[:, hh * B_HV + B_HK:(hh + 1) * B_HV] = (dki * enb + dkt * etb).astype(BF)
            dktkt = dkt * kt
            db3 = (dqd * qd - dki * ki - dktkt).reshape(nc, B_C, B_HK)
            dbl = jnp.sum(dktkt.reshape(nc, B_C, B_HK), axis=1, keepdims=True) + ddec_sc[hh]
            db = (db3 + jnp.where(last, dbl, 0.0)).reshape(t, B_HK)
            dla = _ldot3(lt, db)
            dpre = dla * (1.0 / 16.0) * (1.0 - jax.nn.sigmoid(pre))
            dpreb = dpre.astype(BF)
            dpre_ref[:, ck] = dpreb
            gbias_ref[h] += jnp.sum(dpre, axis=0, keepdims=True)
            dlr_h = _dot_nt(dpreb, wgk_ref[:, ck])
            dlr = dlr_h if hh == 0 else dlr + dlr_h
        dlr_ref[...] = dlr.astype(BF)
        if side:
            @pl.when((i == nt - 1) & (g == B_H // hpb - 1))
            def _():
                side.finish(s_in, s_out, *s_scr)

    rv = lambda i: nt - 1 - i
    si_specs, so_specs, so_shapes, s_scratch, s_ins = _side_specs(side)
    vo, go = 2048 // vw, 3072 // vw
    tile = lambda off: pl.BlockSpec((t, vw), lambda i, g: (rv(i), off + g))
    return _pc(
        body, name="gla_bwd", grid=(nt, B_H // hpb),
        in_specs=[tile(0), tile(vo), tile(go), pl.BlockSpec((t, LANE), lambda i, g: (rv(i), 0)), tile(0), tile(0),
                  pl.BlockSpec((nc, hpb, B_HV, B_HK), lambda i, g: (rv(i), g, 0, 0)),
                  pl.BlockSpec((LANE, kw), lambda i, g: (0, g)), pl.BlockSpec((1, kw), lambda i, g: (0, g)),
                  _const((1, B_HV)), _const((t, t)), _const((t, t))] + si_specs,
        out_specs=[tile(0), tile(0), tile(0), pl.BlockSpec((t, kw), lambda i, g: (rv(i), g)),
                   pl.BlockSpec((t, LANE), lambda i, g: (rv(i), 0)), _acc((B_H, 1, B_HK)), _acc((1, B_HV))] + so_specs,
        out_shape=[SDS((s, D), BF), SDS((s, D), BF), SDS((s, D), BF), SDS((s, B_H * B_HK), BF), SDS((s, LANE), BF),
                   SDS((B_H, 1, B_HK), F32), SDS((1, B_HV), F32)] + so_shapes,
        scratch_shapes=[pltpu.VMEM((B_H, B_HV, B_HK), F32), pltpu.VMEM((hpb, t, B_HV), F32),
                        pltpu.VMEM((hpb, t, B_HK), F32), pltpu.VMEM((hpb, t, B_HK), F32),
                        pltpu.VMEM((hpb, nc, 1, B_HK), F32)] + s_scratch,
        compiler_params=_cp(2))(qk32, z, z, zl, o, dob, states, wgk, bias, wn, ltri, ltri_t, *s_ins)


def mm_tn(a, b, name, tk=2048):
    s, m = a.shape
    n = b.shape[1]
    bn = next(c for c in (1024, 1408, 512, 256, 128) if n % c == 0 and m * c * 4 <= 6 * 1024 * 1024)
    tk = min(tk, s)
    nk = s // tk

    def body(a_ref, b_ref, o_ref, acc):
        k = pl.program_id(1)

        @pl.when(k == 0)
        def _():
            acc[...] = jnp.zeros((m, bn), F32)

        acc[...] += _dot_tn(a_ref[...].astype(BF), b_ref[...])

        @pl.when(k == nk - 1)
        def _():
            o_ref[...] = acc[...].astype(BF)

    return _pc(
        body, name=name, grid=(n // bn, nk),
        in_specs=[pl.BlockSpec((tk, m), lambda j, k: (k, 0)), pl.BlockSpec((tk, bn), lambda j, k: (k, j))],
        out_specs=pl.BlockSpec((m, bn), lambda j, k: (0, j)),
        out_shape=SDS((m, n), BF), scratch_shapes=[pltpu.VMEM((m, bn), F32)], compiler_params=_cp(2))(a, b)


def _adamw(w, g, m, v):
    m = ADAM_B1 * m + (1.0 - ADAM_B1) * g
    v = ADAM_B2 * v + (1.0 - ADAM_B2) * (g * g)
    m_hat = m / (1.0 - ADAM_B1 ** ADAM_STEP)
    v_hat = v / (1.0 - ADAM_B2 ** ADAM_STEP)
    return -ADAM_LR * (m_hat / (jnp.sqrt(v_hat) + ADAM_EPS) + ADAM_WD * w), m, v


def _half_rows(rows):
    rh = rows // 2
    return rh, max(b for b in range(16, 257, 16) if rh % b == 0)


def _pc_sp(body, grid, in_specs, out_specs, out_shape, name):
    gs = pltpu.PrefetchScalarGridSpec(num_scalar_prefetch=1, grid=grid, in_specs=in_specs, out_specs=out_specs)
    return _pc(body, grid_spec=gs, out_shape=out_shape, name=name, compiler_params=_cp(len(grid)))


def adamw_item(own, sib, w, m, v):
    rows, cols = w.shape
    rh = rows // 2
    br = max(b for b in range(16, 257, 16) if rh % b == 0 and (b == 16 or b * cols * 4 <= 256 * 1024))
    nbk = rh // br

    def fn(ins, outs, b, sc_ref):
        own_ref, sib_ref, w_ref, m_ref, v_ref = ins
        g_ = jnp.where(b // nbk == sc_ref[0], own_ref[...], sib_ref[...])
        outs[0][...] = g_
        outs[1][...], outs[2][...], outs[3][...] = _adamw(w_ref[...], g_, m_ref[...], v_ref[...])

    blk = (br, cols)
    mine = lambda b, sc_: (jnp.clip(b - sc_[0] * nbk, 0, nbk - 1), 0)
    theirs = lambda b, sc_: (jnp.clip(b - (1 - sc_[0]) * nbk, 0, nbk - 1), 0)
    each = lambda b, sc_: (b, 0)
    return ([(own, blk, mine), (sib, blk, theirs), (w, blk, each), (m, blk, each), (v, blk, each)],
            [(SDS((rows, cols), F32), blk, each)] * 4, 2 * nbk, fn)


def adamw_cols(sc, own, sib, w, m, v, name, cb=256):
    rows, cols = w.shape
    nk = cols // 2 // cb

    def body(sc_ref, own_ref, sib_ref, w_ref, m_ref, v_ref, go_ref, d_ref, mo_ref, vo_ref):
        g_ = jnp.where(pl.program_id(0) == sc_ref[0], own_ref[...], sib_ref[...])
        go_ref[...] = g_
        d_ref[...], mo_ref[...], vo_ref[...] = _adamw(w_ref[...], g_, m_ref[...], v_ref[...])

    mine = pl.BlockSpec((rows, cb), lambda h, k, sc_: (0, jnp.clip(k + (h - sc_[0]) * nk, 0, nk - 1)))
    theirs = pl.BlockSpec((rows, cb), lambda h, k, sc_: (0, jnp.clip(k + (h - 1 + sc_[0]) * nk, 0, nk - 1)))
    blk = pl.BlockSpec((rows, cb), lambda h, k, sc_: (0, h * nk + k))
    return _pc_sp(body, (2, nk), [mine, theirs, blk, blk, blk], [blk] * 4, [SDS((rows, cols), F32)] * 4,
                  name)(sc, own, sib, w, m, v)


def adamw_small(g, w, m, v):
    def body(g_ref, w_ref, m_ref, v_ref, d_ref, mo_ref, vo_ref):
        d_ref[...], mo_ref[...], vo_ref[...] = _adamw(w_ref[...], g_ref[...], m_ref[...], v_ref[...])

    vm = pl.BlockSpec(memory_space=pltpu.VMEM)
    return _pc(body, name="adamw_small", in_specs=[vm] * 4, out_specs=[vm] * 3, out_shape=[SDS(g.shape, F32)] * 3,
               compiler_params=pltpu.CompilerParams(vmem_limit_bytes=VMEM_LIMIT))(g, w, m, v)


def _pos():
    return lax.axis_index("x"), lax.axis_index("y"), lax.axis_index("c")


def _other_chips(x, y):
    return [(1 - x, y), (x, 1 - y), (1 - x, 1 - y)]


_ANY = pl.BlockSpec(memory_space=pltpu.HBM)


class _Side:
    def __init__(self, ins, out_shapes, nsem, start, finish):
        self.ins, self.out_shapes, self.start, self.finish = list(ins), list(out_shapes), start, finish
        self.scratch = [pltpu.SemaphoreType.DMA((nsem,)), pltpu.SemaphoreType.DMA((nsem,))]
        self.n_in, self.n_out = len(self.ins), len(self.out_shapes)


def _run_side(side, name):
    def body(*refs):
        args_ = (refs[:side.n_in], refs[side.n_in:side.n_in + side.n_out], *refs[side.n_in + side.n_out:])
        side.start(*args_)
        side.finish(*args_)

    return _pc(body, name=name, in_specs=[_ANY] * side.n_in, out_specs=[_ANY] * side.n_out,
               out_shape=side.out_shapes, scratch_shapes=side.scratch)(*side.ins)


def _split_rows(shape):
    return (shape[0] // 2) % 16 == 0


def _core_halves(shape, c):
    if _split_rows(shape):
        h = shape[0] // 2
        return ((pl.ds(pl.multiple_of(c * h, 16), h), slice(None)),
                (pl.ds(pl.multiple_of((1 - c) * h, 16), h), slice(None)))
    h = shape[1] // 2
    assert h % LANE == 0
    return ((slice(None), pl.ds(pl.multiple_of(c * h, LANE), h)),
            (slice(None), pl.ds(pl.multiple_of((1 - c) * h, LANE), h)))


def gather_side(bigs, tinies):
    nb, nt_ = len(bigs), len(tinies)

    def plan(ins, outs, ssem, rsem):
        x, y, c = _pos()
        me = 2 * x + y
        chips = _other_chips(x, y)
        sibling = (x, y, 1 - c)

        def copy(k, src, dst, to):
            return pltpu.make_async_remote_copy(src_ref=src, dst_ref=dst, send_sem=ssem.at[k], recv_sem=rsem.at[k],
                                                device_id=to, device_id_type=MESH)

        sends, landed, passed_on, tiny_landed = [], [], [], []
        for w in range(nb):
            mine, theirs = _core_halves(bigs[w].shape, c)
            for j, (cx, cy) in enumerate(chips):
                sends.append(copy(6 * w + j, ins[w].at[mine], outs[w].at[(me,) + mine], (cx, cy, c)))
                blk = outs[w].at[(2 * cx + cy,) + mine]
                landed.append((copy(6 * w + j, blk, blk, (cx, cy, c)), copy(6 * w + 3 + j, blk, blk, sibling)))
                blk = outs[w].at[(2 * cx + cy,) + theirs]
                passed_on.append(copy(6 * w + 3 + j, blk, blk, sibling))
        for w in range(nt_):
            for j, (cx, cy) in enumerate(chips):
                k = 6 * nb + 3 * w + j
                sends.append(copy(k, ins[nb + w], outs[nb + w].at[me], (cx, cy, c)))
                blk = outs[nb + w].at[2 * cx + cy]
                tiny_landed.append(copy(k, blk, blk, (cx, cy, c)))
        return sends, landed, passed_on, tiny_landed

    def start(ins, outs, ssem, rsem):
        for cp in plan(ins, outs, ssem, rsem)[0]:
            cp.start()

    def finish(ins, outs, ssem, rsem):
        sends, landed, passed_on, tiny_landed = plan(ins, outs, ssem, rsem)
        for arrived, forward in landed:
            arrived.wait_recv()
            forward.start()
        for arrived in tiny_landed + passed_on:
            arrived.wait_recv()
        for cp in sends + [forward for _, forward in landed]:
            cp.wait_send()

    return _Side(list(bigs) + list(tinies), [SDS((4,) + a.shape, a.dtype) for a in list(bigs) + list(tinies)],
                 6 * nb + 3 * nt_, start, finish)


def _sibling_side(srcs, out_shapes, pick):
    def plan(in_refs, out_refs, ssem, rsem):
        x, y, c = _pos()
        return [pltpu.make_async_remote_copy(src_ref=pick(in_refs[w], srcs[w].shape, c), dst_ref=out_refs[w],
                                             send_sem=ssem.at[w], recv_sem=rsem.at[w], device_id=(x, y, 1 - c),
                                             device_id_type=MESH) for w in range(len(srcs))]

    def start(*refs):
        for cp in plan(*refs):
            cp.start()

    def finish(*refs):
        for cp in plan(*refs):
            cp.wait()

    return _Side(srcs, out_shapes, len(srcs), start, finish)


def swap_side(gs):
    def half_shape(g):
        l, r, cols = g.shape
        return (l, r // 2, cols) if _split_rows((r, cols)) else (l, r, cols // 2)

    return _sibling_side(gs, [SDS(half_shape(g), g.dtype) for g in gs],
                         lambda ref, shape, c: ref.at[(slice(None),) + _core_halves(shape[1:], c)[1]])


def join_side(halves):
    return _sibling_side(halves, [SDS(h.shape, h.dtype) for h in halves], lambda ref, shape, c: ref)


def _both_sides(a, b):
    def split(ins, outs, *scr):
        return ((ins[:a.n_in], outs[:a.n_out], *scr[:2]), (ins[a.n_in:], outs[a.n_out:], *scr[2:]))

    def start(*refs):
        ra, rb = split(*refs)
        a.start(*ra)
        b.start(*rb)

    def finish(*refs):
        ra, rb = split(*refs)
        a.finish(*ra)
        b.finish(*rb)

    side = _Side(a.ins + b.ins, a.out_shapes + b.out_shapes, 1, start, finish)
    side.scratch = a.scratch + b.scratch
    return side


COL_BLOCK = 256


def _blockwise(sc, items, name):
    in_specs, out_specs, out_shapes, operands, spans = [], [], [], [], []
    start = 0
    for ins, outs, nb, _ in items:
        def spec(blk, idx, s0=start, nb=nb):
            return pl.BlockSpec(blk, lambda i, sc_: idx(jnp.clip(i - s0, 0, nb - 1), sc_))

        in_specs += [spec(blk, idx) for _, blk, idx in ins]
        out_specs += [spec(blk, idx) for _, blk, idx in outs]
        operands += [a for a, _, _ in ins]
        out_shapes += [s_ for s_, _, _ in outs]
        spans.append((start, start + nb))
        start += nb
    n_in = len(operands)

    def body(sc_ref, *refs):
        i = pl.program_id(0)
        at_in, at_out = 0, n_in
        for (ins, outs, _, fn), (lo, hi) in zip(items, spans):
            mine_in, mine_out = refs[at_in:at_in + len(ins)], refs[at_out:at_out + len(outs)]
            at_in, at_out = at_in + len(ins), at_out + len(outs)

            @pl.when((i >= lo) & (i < hi))
            def _(fn=fn, mine_in=mine_in, mine_out=mine_out, lo=lo):
                fn(mine_in, mine_out, i - lo, sc_ref)

    return _pc_sp(body, (start,), in_specs, out_specs, out_shapes, name)(sc, *operands)


def add_half_item(g, sib):
    l, r, cols = g.shape

    def fn(ins, outs, b, sc_ref):
        outs[0][...] = (ins[0][...].astype(F32) + ins[1][...].astype(F32)).astype(BF)

    if _split_rows((r, cols)):
        rh, br = _half_rows(r)
        nbk = rh // br
        blk = (1, br, cols)
        there = lambda b, sc_: (b // nbk, b % nbk, 0)
        return ([(g, blk, lambda b, sc_: (b // nbk, sc_[0] * nbk + b % nbk, 0)), (sib, blk, there)],
                [(SDS((l, rh, cols), BF), blk, there)], l * nbk, fn)
    nbk = cols // 2 // COL_BLOCK
    blk = (1, r, COL_BLOCK)
    there = lambda b, sc_: (b // nbk, 0, b % nbk)
    return ([(g, blk, lambda b, sc_: (b // nbk, 0, sc_[0] * nbk + b % nbk)), (sib, blk, there)],
            [(SDS((l, r, cols // 2), BF), blk, there)], l * nbk, fn)


def exchange_side(ps):
    n_ = len(ps)

    def width(p_):
        return p_.shape[2] if p_.shape[0] == 4 else p_.shape[2] // 4

    def plan(p_refs, got_refs, ssem, rsem):
        x, y, c = _pos()
        cps = []
        for w in range(n_):
            wd = width(ps[w])
            for j, (cx, cy) in enumerate(_other_chips(x, y)):
                to = 2 * cx + cy
                src = p_refs[w].at[to] if ps[w].shape[0] == 4 else p_refs[w].at[0, :, pl.ds(pl.multiple_of(to * wd, LANE), wd)]
                cps.append(pltpu.make_async_remote_copy(
                    src_ref=src, dst_ref=got_refs[w].at[j], send_sem=ssem.at[3 * w + j], recv_sem=rsem.at[3 * w + j],
                    device_id=(cx, cy, c), device_id_type=MESH))
        return cps

    def start(*refs):
        for cp in plan(*refs):
            cp.start()

    def finish(*refs):
        for cp in plan(*refs):
            cp.wait()

    return _Side(ps, [SDS((3, p_.shape[1], width(p_)), p_.dtype) for p_ in ps], 3 * n_, start, finish)


def sum4_item(p, got):
    _, rh, wd = got.shape

    def fn(ins, outs, b, sc_ref):
        p_ref, g_ref = ins
        outs[0][...] = ((p_ref[0].astype(F32) + g_ref[0].astype(F32))
                        + (g_ref[1].astype(F32) + g_ref[2].astype(F32)))

    if rh % 16:
        assert p.shape[0] == 4
        return ([(p, (1, rh, COL_BLOCK), lambda b, sc_: (sc_[1], 0, b)),
                 (got, (3, rh, COL_BLOCK), lambda b, sc_: (0, 0, b))],
                [(SDS((rh, wd), F32), (rh, COL_BLOCK), lambda b, sc_: (0, b))], wd // COL_BLOCK, fn)
    _, br = _half_rows(2 * rh)
    own = (lambda b, sc_: (sc_[1], b, 0)) if p.shape[0] == 4 else (lambda b, sc_: (0, b, sc_[1]))
    return ([(p, (1, br, wd), own), (got, (3, br, wd), lambda b, sc_: (0, b, 0))],
            [(SDS((rh, wd), F32), (br, wd), lambda b, sc_: (b, 0))], rh // br, fn)


def allreduce_small(g):
    rows = g.shape[0]
    rh = rows // 2

    def body(g_ref, out_ref, sib_buf, chip_buf, sum_sc, ssem, rsem):
        x, y, c = _pos()
        me = 2 * x + y
        sibling = (x, y, 1 - c)
        mine = pl.ds(pl.multiple_of(c * rh, 8), rh)

        def copy(k, src, dst, to):
            return pltpu.make_async_remote_copy(src_ref=src, dst_ref=dst, send_sem=ssem.at[k], recv_sem=rsem.at[k],
                                                device_id=to, device_id_type=MESH)

        cp = copy(0, g_ref, sib_buf, sibling)
        cp.start()
        cp.wait()
        sum_sc[...] = g_ref[...] + sib_buf[...]
        chips = _other_chips(x, y)
        cps = [copy(1 + j, sum_sc.at[mine], chip_buf.at[me], (cx, cy, c)) for j, (cx, cy) in enumerate(chips)]
        for cp in cps:
            cp.start()
        chip_buf[me] = sum_sc[mine, :]
        for j, (cx, cy) in enumerate(chips):
            copy(1 + j, sum_sc.at[mine], chip_buf.at[2 * cx + cy], (cx, cy, c)).wait_recv()
        for cp in cps:
            cp.wait_send()
        out_ref[mine, :] = (chip_buf[0] + chip_buf[1]) + (chip_buf[2] + chip_buf[3])
        cp = copy(4, out_ref.at[mine], out_ref.at[mine], sibling)
        cp.start()
        cp.wait()

    vm = pl.BlockSpec(memory_space=pltpu.VMEM)
    return _pc(body, name="allreduce_small", in_specs=[vm], out_specs=vm, out_shape=SDS((rows, LANE), F32),
               scratch_shapes=[pltpu.VMEM((rows, LANE), F32), pltpu.VMEM((4, rh, LANE), F32), pltpu.VMEM((rows, LANE), F32),
                               pltpu.SemaphoreType.DMA((5,)), pltpu.SemaphoreType.DMA((5,))],
               compiler_params=pltpu.CompilerParams(vmem_limit_bytes=VMEM_LIMIT))(g)


def _pack_small(entries, get):
    flat = jnp.concatenate([get(n).reshape(-1).astype(F32) for n, _ in entries])
    rows = -(-flat.shape[0] // (8 * LANE)) * 8
    return jnp.pad(flat, (0, rows * LANE - flat.shape[0])).reshape(rows, LANE)


def _unpack_small(entries, packed):
    out, off = {}, 0
    flat = packed.reshape(-1)
    for name, n in entries:
        out[name] = flat[off:off + n]
        off += n
    return out


def _cols_full(blk):
    return blk.transpose(1, 0, 2).reshape(blk.shape[1], 4 * blk.shape[2])


def kernel(x, p, pre_mix_norm, w_in, a_ln_g, a_ln_b, a_spatial_w, a_spatial_b, a_out, b_gk, b_gk_bias, b_out_norm, b_out, w_mix_out, post_mix_norm, pre_ffn_norm, w_up, conv_w, conv_b, w_down, post_ffn_norm, w_ple, w_ple_gate, post_ple_norm, loss_target, m_pre_mix_norm, m_w_in, m_a_ln_g, m_a_ln_b, m_a_spatial_w, m_a_spatial_b, m_a_out, m_b_gk, m_b_gk_bias, m_b_out_norm, m_b_out, m_w_mix_out, m_post_mix_norm, m_pre_ffn_norm, m_w_up, m_conv_w, m_conv_b, m_w_down, m_post_ffn_norm, m_w_ple, m_w_ple_gate, m_post_ple_norm, v_pre_mix_norm, v_w_in, v_a_ln_g, v_a_ln_b, v_a_spatial_w, v_a_spatial_b, v_a_out, v_b_gk, v_b_gk_bias, v_b_out_norm, v_b_out, v_w_mix_out, v_post_mix_norm, v_pre_ffn_norm, v_w_up, v_conv_w, v_conv_b, v_w_down, v_post_ffn_norm, v_w_ple, v_w_ple_gate, v_post_ple_norm):
    args = dict(locals())
    order = ['pre_mix_norm', 'w_in', 'a_ln_g', 'a_ln_b', 'a_spatial_w', 'a_spatial_b', 'a_out', 'b_gk', 'b_gk_bias',
             'b_out_norm', 'b_out', 'w_mix_out', 'post_mix_norm', 'pre_ffn_norm', 'w_up', 'conv_w', 'conv_b', 'w_down',
             'post_ffn_norm', 'w_ple', 'w_ple_gate', 'post_ple_norm']
    assert sorted(BIG + TINY + tuple(n for n, _ in SMALL)) == sorted(order)
    s = x.shape[1]
    xs = x.reshape(s, D)
    ps = p.reshape(s, PLE)
    tgt = loss_target.reshape(s, D)
    t_big = min(1024, s)
    t_mid = min(512, s)
    t_small = min(256, s)
    t_gla = min(256, s)
    mx_, my_, mc_ = _pos()
    me = 2 * mx_ + my_
    sc = jnp.stack([mc_, me]).astype(jnp.int32)
    shard = lambda n: args[n].reshape(args[n].shape[1:])

    mine = {n: shard(n).astype(BF) for n in BIG}
    mine["w_in"] = shard("w_in").T.astype(BF)
    mine.update({n: shard(n) for n in TINY})
    fill = lambda names, gots: {n: lax.dynamic_update_slice(got, mine[n][None], (me, 0, 0)) for n, got in zip(names, gots)}
    first = ("w_in",) + TINY
    full = fill(first, _run_side(gather_side([mine["w_in"]], [mine[n] for n in TINY]), "gather_first"))
    wi = full["w_in"].reshape(4 * 1540, D)
    seg = lambda a, b: wi[a:b]
    qk = [seg(1024 + h * B_HK, 1024 + (h + 1) * B_HK) for h in range(B_H)]
    kk = [seg(1536 + h * B_HK, 1536 + (h + 1) * B_HK) for h in range(B_H)]
    w_z = jnp.concatenate([seg(0, 1024)] + [m_ for h in range(B_H) for m_ in (qk[h], kk[h])]
                          + [seg(2048, 4096), seg(4112, 6160), seg(4096, 4112), jnp.zeros((LANE - B_RANK, D), BF)], axis=0)
    wgk = jnp.pad(_cols_full(full["b_gk"]).astype(BF), ((0, LANE - B_RANK), (0, 0)))
    w_conv = _cols_full(full["conv_w"])
    g1, g2, g3 = pre_mix_norm.reshape(1, D), post_mix_norm.reshape(1, D), pre_ffn_norm.reshape(1, D)
    g4, g5 = post_ffn_norm.reshape(1, D), post_ple_norm.reshape(1, D)
    ln_g, ln_b = a_ln_g.reshape(1, A_W), a_ln_b.reshape(1, A_W)
    w_s = a_spatial_w.reshape(A_G, A_C, A_C)
    w_cat = w_s.transpose(1, 0, 2).reshape(A_C, A_G * A_C)
    w_cat_t = w_s.transpose(2, 0, 1).reshape(A_C, A_G * A_C)
    bias_full = jnp.repeat(a_spatial_b.reshape(A_G, A_C).T, A_GD, axis=1)
    bdm = (jnp.arange(A_G * A_C)[:, None] // A_C == jnp.arange(A_W)[None, :] // A_GD).astype(BF)
    gk_bias = b_gk_bias.reshape(1, B_H * B_HK)
    wn = b_out_norm.reshape(1, B_HV)
    cb = conv_b.reshape(1, 2 * D_FF)
    idx = jnp.arange(t_gla)
    ltri = ((idx[:, None] // B_C == idx[None, :] // B_C) & (idx[None, :] <= idx[:, None])).astype(BF)

    a, z, qk32, zl, *gots = norm_matmul(xs, g1, w_z, D, t_big, "in_proj", nblk=6, f32_blk=1, tail_blk=48,
                                        side=gather_side([mine[n] for n in BIG[1:]], []))
    full.update(fill(BIG[1:], gots))
    w_aout, w_ple_f = _cols_full(full["a_out"]), _cols_full(full["w_ple"])
    w_bout, w_mix, w_pg = (full[n].reshape(D, D) for n in ("b_out", "w_mix_out", "w_ple_gate"))
    w_dn, w_up3 = full["w_down"].reshape(D_FF, D), full["w_up"]
    sa = sgu_fwd(z, ln_g, ln_b, w_cat, bias_full, bdm, t_mid)
    ob, o, states = gla_fwd(z, qk32, zl, wgk, gk_bias, wn, ltri, t_gla)
    ya, yb, mp, mx, h1 = mix_fwd(sa, ob, z, xs, w_aout, w_bout, w_mix, g2, t_mid)
    c, up_g, up_v, cg, cv, ff = ffn_up_fwd(h1, g3, w_up3, w_conv, cb, t_mid)
    f, h2, pg, pe, dy, loss = out_fwd(ff, h1, ps, tgt, w_dn, w_pg, w_ple_f, g4, g5, t_mid)

    dh2, dpe, dpg, df, dff, gg5, gg4 = out_bwd(dy, pg, pe, f, g5, g4, w_pg, w_dn, t_mid)
    dup, gcw, gcb, dh1, gg3 = ffn_up_bwd(up_g, up_v, cg, cv, dff, w_conv, _cols_full(w_up3), h1, g3, dh2, t_small)
    grads = {
        "w_up": mm_tn(c, dup, "dw_up")[None],
        "w_down": mm_tn(ff, df, "dw_down").reshape(4, D_FF // 4, D),
        "w_ple": mm_tn(ps, dpe, "dw_ple")[None],
        "w_ple_gate": mm_tn(h2, dpg, "dw_ple_gate").reshape(4, D // 4, D),
    }
    ffn_side = ("w_up", "w_down", "w_ple", "w_ple_gate")
    dmx, dya, dyb, dga, dgb, dsa, dob, gg2, *sibs = mix_bwd(dh1, mx, z, ya, yb, g2, w_mix, w_aout, w_bout, t_mid,
                                                            side=swap_side([grads[n] for n in ffn_side]))
    sib = dict(zip(ffn_side, sibs))
    duv, g_lng, g_lnb, g_wcat, g_bst = sgu_bwd(z, dsa, ln_g, ln_b, w_cat, w_cat_t, bias_full, bdm, t_mid)
    g_ws = g_wcat.reshape(A_C, A_G, A_C).transpose(1, 0, 2)
    grads.update({
        "a_out": mm_tn(sa, dya, "dw_a_out")[None],
        "b_out": mm_tn(ob, dyb, "dw_b_out").reshape(4, D // 4, D),
        "w_mix_out": mm_tn(mp, dmx, "dw_mix").reshape(4, D // 4, D),
    })

    def swap(names):
        sib.update(zip(names, _run_side(swap_side([grads[n] for n in names]), "swap_halves_" + names[0])))

    swap(("a_out", "b_out", "w_mix_out"))
    parts = dict(zip(BIG[1:], _blockwise(sc, [add_half_item(grads[n], sib[n]) for n in BIG[1:]], "partials")))
    dqk, dvb, dog, dpre, dlr, g_gkb, g_wn, *gots = gla_bwd(
        z, qk32, zl, o, dob, states, wgk, gk_bias, wn, ltri, ltri.T, t_gla,
        side=exchange_side([parts[n] for n in BIG[1:]]))
    reds = dict(zip(BIG[1:], _blockwise(sc, [sum4_item(parts[n], got_) for n, got_ in zip(BIG[1:], gots)], "sums")))
    segs = [duv, dqk, dvb, dog, dga, dgb, dlr]

    gz = [mm_tn(sg_, a, "dw_in_%d" % k) for k, sg_ in enumerate(segs)]
    gq = [gz[1][h * 256:h * 256 + B_HK] for h in range(B_H)]
    gk = [gz[1][h * 256 + B_HK:(h + 1) * 256] for h in range(B_H)]
    g_in = jnp.concatenate([gz[0]] + gq + gk + [gz[2], gz[3], gz[6][:B_RANK], gz[4], gz[5]], axis=0)
    grads["w_in"] = g_in.reshape(4, 1540, D)
    swap(("w_in",))
    parts["w_in"], = _blockwise(sc, [add_half_item(grads["w_in"], sib["w_in"])], "partial_w_in")
    dx, gg1, got_in, *sib_reds = nt_normbwd(
        segs, w_z, xs, g1, dh1, t_small, "in_bwd",
        side=_both_sides(exchange_side([parts["w_in"]]), join_side([reds[n] for n in BIG[1:]])))
    sib_red = dict(zip(BIG[1:], sib_reds))

    reds["w_in"], = _blockwise(sc, [sum4_item(parts["w_in"], got_in)], "sum_w_in")
    sib_red["w_in"], = _run_side(join_side([reds["w_in"]]), "join_w_in")
    res = _blockwise(sc, [adamw_item(reds[n], sib_red[n], shard(n), shard("m_" + n), shard("v_" + n))
                          for n in BIG[1:]], "adamw")
    outs = {n: [r_.reshape(args[n].shape) for r_ in res[4 * k:4 * k + 4]] for k, n in enumerate(BIG[1:])}
    res = adamw_cols(sc, reds["w_in"], sib_red["w_in"], shard("w_in").T, shard("m_w_in").T, shard("v_w_in").T,
                     "adamw_w_in")
    outs["w_in"] = [r_.T.reshape(w_in.shape) for r_ in res]

    small_g = {
        "pre_mix_norm": gg1, "a_ln_g": g_lng, "a_ln_b": g_lnb, "a_spatial_w": g_ws, "a_spatial_b": g_bst.T,
        "b_gk_bias": g_gkb, "b_out_norm": g_wn, "post_mix_norm": gg2, "pre_ffn_norm": gg3,
        "conv_b": gcb, "post_ffn_norm": gg4, "post_ple_norm": gg5,
        "b_gk": mm_tn(zl, dpre, "dw_gk")[:B_RANK], "conv_w": gcw,
        "loss": loss,
    }
    red_entries = SMALL + (("b_gk", B_RANK * 512), ("conv_w", 3 * 2 * D_FF), ("loss", 1))
    g_fin = _unpack_small(red_entries, allreduce_small(_pack_small(red_entries, lambda n: small_g[n])))
    g_fin["b_gk"] = lax.dynamic_slice(g_fin["b_gk"].reshape(B_RANK, 512), (0, me * B_HK), (B_RANK, B_HK))
    g_fin["conv_w"] = lax.dynamic_slice(g_fin["conv_w"].reshape(3, 2 * D_FF), (0, me * 1408), (3, 1408))
    upd_entries = SMALL + (("b_gk", B_RANK * B_HK), ("conv_w", 3 * 1408))
    res = adamw_small(*[_pack_small(upd_entries, get) for get in
                        (lambda n: g_fin[n], lambda n: args[n], lambda n: args["m_" + n], lambda n: args["v_" + n])])
    res = [_unpack_small(upd_entries, r_) for r_ in res]
    for n, _ in upd_entries:
        outs[n] = [r_[n].reshape(args[n].shape) for r_ in [g_fin] + res]

    return (g_fin["loss"].reshape(()), dx.reshape(x.shape), *[outs[n][0] for n in order], *[outs[n][1] for n in order],
            *[outs[n][2] for n in order], *[outs[n][3] for n in order])
```

```python
import math

import jax
import jax.numpy as jnp
from jax import lax
from jax.experimental import pallas as pl
from jax.experimental.pallas import tpu as pltpu

F32 = jnp.float32
BF = jnp.bfloat16
SDS = jax.ShapeDtypeStruct
MESH = pl.DeviceIdType.MESH

EPS = 1e-6
D = 1024
A_W = 512
A_G, A_C = 8, 128
A_GD = A_W // A_G
B_H, B_HK, B_HV = 4, 128, 256
B_C = 64
GLA_HPB = 4
B_RANK = 16
D_FF = 2816
PLE = 256
LANE = 128
VMEM_LIMIT = 60 * 1024 * 1024

ADAM_LR, ADAM_B1, ADAM_B2, ADAM_EPS, ADAM_WD, ADAM_STEP = 0.001, 0.9, 0.999, 1e-08, 0.01, 10

_GC = math.sqrt(2.0 / math.pi)
_GA = 0.044715

BIG = ("w_in", "a_out", "b_out", "w_mix_out", "w_up", "w_down", "w_ple", "w_ple_gate")
TINY = ("b_gk", "conv_w")
SMALL = (("pre_mix_norm", 1024), ("a_ln_g", 512), ("a_ln_b", 512), ("a_spatial_w", 131072),
         ("a_spatial_b", 1024), ("b_gk_bias", 512), ("b_out_norm", 256), ("post_mix_norm", 1024),
         ("pre_ffn_norm", 1024), ("conv_b", 5632), ("post_ffn_norm", 1024), ("post_ple_norm", 1024))


def _pc(body, **kw):
    return pl.pallas_call(body, **kw)


def _cp(n):
    return pltpu.CompilerParams(dimension_semantics=("arbitrary",) * n, vmem_limit_bytes=VMEM_LIMIT)


def _const(shape):
    nd = len(shape)
    return pl.BlockSpec(shape, lambda *_: (0,) * nd, pipeline_mode=pl.Buffered(1))


def _acc(shape):
    nd = len(shape)
    return pl.BlockSpec(shape, lambda *_: (0,) * nd)


def _dot(a, b):
    return jnp.dot(a, b, preferred_element_type=F32)


def _dot_nt(a, b):
    return lax.dot_general(a, b, (((1,), (1,)), ((), ())), preferred_element_type=F32)


def _dot_tn(a, b):
    return lax.dot_general(a, b, (((0,), (0,)), ((), ())), preferred_element_type=F32)


def _gelu(x):
    return 0.5 * x * (1.0 + jnp.tanh(_GC * (x + _GA * x * x * x)))


def _gelu_and_grad(x):
    x2 = x * x
    s = 0.5 * jnp.tanh((_GC * x) * (1.0 + _GA * x2)) + 0.5
    g = x * s
    return g, s + g * (1.0 - s) * ((6.0 * _GC * _GA) * x2 + 2.0 * _GC)


def _log_sigmoid(x):
    return jnp.minimum(x, 0.0) - jnp.log(1.0 + jnp.exp(-jnp.abs(x)))


def _rms(x, g):
    return x * lax.rsqrt(jnp.mean(x * x, axis=-1, keepdims=True) + EPS) * g


def _rms_bwd(dy, x, g):
    r = lax.rsqrt(jnp.mean(x * x, axis=-1, keepdims=True) + EPS)
    n = x * r
    dn = dy * g
    dx = r * (dn - n * jnp.mean(dn * n, axis=-1, keepdims=True))
    return dx, jnp.sum(dy * n, axis=0, keepdims=True)


def _ldot3(l, x):
    h = x.astype(BF)
    r = x - h.astype(F32)
    m = r.astype(BF)
    lo = (r - m.astype(F32)).astype(BF)
    return _dot(l, h) + _dot(l, m) + _dot(l, lo)


def _split_side(refs, n_in, n_out, n_scratch, side):
    si, so = (side.n_in, side.n_out) if side else (0, 0)
    cuts = [n_in, si, n_out, so, n_scratch]
    out, at = [], 0
    for c in cuts:
        out.append(refs[at:at + c])
        at += c
    return (*out, refs[at:])


def _side_specs(side):
    return ([_ANY] * side.n_in, [_ANY] * side.n_out, side.out_shapes, side.scratch, side.ins) if side else ([],) * 5


def norm_matmul(x, g, wt, bn, t, name, nblk, f32_blk, tail_blk, side=None):
    s, dm = x.shape
    w_spec = pl.BlockSpec((bn, dm), lambda i, j: (j, 0))
    nt = s // t

    def body(*refs):
        (x_ref, g_ref, w_ref, wl_ref), s_in, outs, s_out, (a_sc,), s_scr = _split_side(refs, 4, 4, 1, side)
        a_ref, z_ref, f32_ref, tail_ref = outs
        i, j = pl.program_id(0), pl.program_id(1)
        if side:
            @pl.when((i == 0) & (j == 0))
            def _():
                side.start(s_in, s_out, *s_scr)

        @pl.when(j == 0)
        def _():
            a = _rms(x_ref[...], g_ref[...]).astype(BF)
            a_sc[...] = a
            a_ref[...] = a
            tail_ref[...] = _dot_nt(a, wl_ref[...]).astype(BF)

        acc = _dot_nt(a_sc[...], w_ref[...])
        z_ref[...] = acc.astype(BF)

        @pl.when(j == f32_blk)
        def _():
            f32_ref[...] = acc
        if side:
            @pl.when((i == nt - 1) & (j == nblk - 1))
            def _():
                side.finish(s_in, s_out, *s_scr)

    si_specs, so_specs, so_shapes, s_scratch, s_ins = _side_specs(side)
    return _pc(
        body, name=name, grid=(nt, nblk),
        in_specs=[pl.BlockSpec((t, dm), lambda i, j: (i, 0)), _const((1, dm)), w_spec,
                  pl.BlockSpec((LANE, dm), lambda i, j: (tail_blk, 0), pipeline_mode=pl.Buffered(1))] + si_specs,
        out_specs=[pl.BlockSpec((t, dm), lambda i, j: (i, 0)), pl.BlockSpec((t, bn), lambda i, j: (i, j)),
                   pl.BlockSpec((t, bn), lambda i, j: (i, 0)), pl.BlockSpec((t, LANE), lambda i, j: (i, 0))] + so_specs,
        out_shape=[SDS((s, dm), BF), SDS((s, nblk * bn), BF), SDS((s, bn), F32), SDS((s, LANE), BF)] + so_shapes,
        scratch_shapes=[pltpu.VMEM((t, dm), BF)] + s_scratch, compiler_params=_cp(2))(x, g, wt, wt, *s_ins)


def _sgu_weights(wc_ref, transposed):
    r = lax.broadcasted_iota(jnp.int32, (A_C, A_G * A_C), 0)
    c = lax.broadcasted_iota(jnp.int32, (A_C, A_G * A_C), 1) & (A_C - 1)
    return jnp.where((r <= c) if transposed else (c <= r), wc_ref[...], 0.0).astype(BF)


def _sgu_spread(xs, bdm):
    return jnp.concatenate([jnp.tile(x, (A_G, 1)) * bdm for x in xs], axis=1)


def _sgu_recompute(v, lng, lnb):
    gv, dgv = _gelu_and_grad(v)
    gv = gv.astype(F32)
    mu = jnp.mean(gv, axis=-1, keepdims=True)
    xc = gv - mu
    rstd = lax.rsqrt(jnp.mean(xc * xc, axis=-1, keepdims=True) + EPS)
    xhat = xc * rstd
    return dgv, rstd, xhat, (xhat * lng + lnb).astype(BF)


def sgu_fwd(z, ln_g, ln_b, w_cat, bias_full, bdm, t):
    s = z.shape[0]
    nch = t // A_C

    def body(u_ref, v_ref, g_ref, b_ref, wc_ref, bias_ref, bdm_ref, sa_ref):
        vns = [_sgu_recompute(v_ref[pl.ds(ci * A_C, A_C), :], g_ref[...], b_ref[...])[3]
               for ci in range(nch)]
        mixed = _dot(_sgu_weights(wc_ref, False), _sgu_spread(vns, bdm_ref[...]))
        for ci in range(nch):
            rows = pl.ds(ci * A_C, A_C)
            s_ = mixed[:, ci * A_W:(ci + 1) * A_W] + bias_ref[...]
            sa_ref[rows, :] = _gelu(u_ref[rows, :]) * s_.astype(BF)

    return _pc(
        body, name="sgu_fwd", grid=(s // t,),
        in_specs=[pl.BlockSpec((t, A_W), lambda i: (i, 0)), pl.BlockSpec((t, A_W), lambda i: (i, 1)),
                  _const((1, A_W)), _const((1, A_W)), _const((A_C, A_G * A_C)), _const((A_C, A_W)),
                  _const((A_G * A_C, A_W))],
        out_specs=pl.BlockSpec((t, A_W), lambda i: (i, 0)),
        out_shape=SDS((s, A_W), BF), compiler_params=_cp(1))(z, z, ln_g, ln_b, w_cat, bias_full, bdm)


def _gla_decays(qk, lr, wgk, bias, l, t):
    nc = t // B_C
    q = qk[:, :B_HK].astype(F32) * (B_HK ** -0.5)
    k = qk[:, B_HK:].astype(F32)
    pre = _dot(lr, wgk) + bias
    la = _log_sigmoid(pre) * (1.0 / 16.0)
    b = _ldot3(l, la)
    b3 = b.reshape(nc, B_C, B_HK)
    bl = jnp.broadcast_to(b3[:, B_C - 1:B_C, :], (nc, B_C, B_HK)).reshape(t, B_HK)
    eb, enb, etb = jnp.exp(b), jnp.exp(-b), jnp.exp(bl - b)
    return pre, b, bl, eb, enb, etb, q * eb, k * enb, k * etb


def gla_fwd(z, qk32, zl, wgk, bias, wn, ltri, t):
    s = z.shape[0]
    nc = t // B_C
    hpb = GLA_HPB
    assert hpb == B_H
    kw, vw = hpb * B_HK, hpb * B_HV

    def body(qk_ref, v_ref, og_ref, lr_ref, wgk_ref, bias_ref, wn_ref, l_ref, ob_ref, o_ref, st_ref, st_sc, o_sc):
        @pl.when(pl.program_id(0) == 0)
        def _():
            st_sc[...] = jnp.zeros((B_H, B_HV, B_HK), F32)

        lr, l = lr_ref[...], l_ref[...]
        for hh in range(hpb):
            h = hh
            cv, ck = slice(hh * B_HV, (hh + 1) * B_HV), slice(hh * B_HK, (hh + 1) * B_HK)
            _, _, bl, _, _, _, qd, ki, kt = _gla_decays(qk_ref[:, cv], lr, wgk_ref[:, ck], bias_ref[:, ck], l, t)
            qd, ki, kt = qd.astype(BF), ki.astype(BF), kt.astype(BF)
            vb = v_ref[:, cv]
            sc = jnp.where(l > 0, _dot_nt(qd, ki), 0.0).astype(BF)
            o_sc[hh] = _dot(sc, vb)
            for n in range(nc):
                rows = slice(n * B_C, (n + 1) * B_C)
                st = st_sc[h]
                stb = st.astype(BF)
                st_ref[n, hh] = stb
                o_sc[hh, rows, :] += _dot_nt(qd[rows], stb)
                st_sc[h] = st * jnp.exp(bl[n * B_C:n * B_C + 1, :]) + _dot_tn(vb[rows], kt[rows])
            ob = o_sc[hh].astype(BF)
            o_ref[:, cv] = ob
            og = og_ref[:, cv].astype(F32)
            ob_ref[:, cv] = (_rms(ob.astype(F32), wn_ref[...]) * og * jax.nn.sigmoid(og)).astype(BF)

    vo, go = 2048 // vw, 3072 // vw
    return _pc(
        body, name="gla_fwd", grid=(s // t, B_H // hpb),
        in_specs=[pl.BlockSpec((t, vw), lambda i, g: (i, g)), pl.BlockSpec((t, vw), lambda i, g: (i, vo + g)),
                  pl.BlockSpec((t, vw), lambda i, g: (i, go + g)), pl.BlockSpec((t, LANE), lambda i, g: (i, 0)),
                  pl.BlockSpec((LANE, kw), lambda i, g: (0, g)), pl.BlockSpec((1, kw), lambda i, g: (0, g)),
                  _const((1, B_HV)), _const((t, t))],
        out_specs=[pl.BlockSpec((t, vw), lambda i, g: (i, g)), pl.BlockSpec((t, vw), lambda i, g: (i, g)),
                   pl.BlockSpec((nc, hpb, B_HV, B_HK), lambda i, g: (i, g, 0, 0))],
        out_shape=[SDS((s, D), BF), SDS((s, D), BF), SDS((s // B_C, B_H, B_HV, B_HK), BF)],
        scratch_shapes=[pltpu.VMEM((B_H, B_HV, B_HK), F32), pltpu.VMEM((hpb, t, B_HV), F32)],
        compiler_params=_cp(2))(qk32, z, z, zl, wgk, bias, wn, ltri)


def mix_fwd(sa, ob, z, x, a_out, b_out, w_mix, g2, t):
    s = x.shape[0]

    def body(sa_ref, ob_ref, ga_ref, gb_ref, x_ref, ao_ref, bo_ref, wm_ref, g2_ref,
             ya_ref, yb_ref, mp_ref, mx_ref, h1_ref):
        ya = _dot(sa_ref[...], ao_ref[...]).astype(BF)
        yb = _dot(ob_ref[...], bo_ref[...]).astype(BF)
        ya_ref[...] = ya
        yb_ref[...] = yb
        mp = (jax.nn.sigmoid(ga_ref[...].astype(F32)) * ya.astype(F32)
              + jax.nn.sigmoid(gb_ref[...].astype(F32)) * yb.astype(F32)).astype(BF)
        mp_ref[...] = mp
        mx = _dot(mp, wm_ref[...]).astype(BF)
        mx_ref[...] = mx
        h1_ref[...] = x_ref[...] + _rms(mx.astype(F32), g2_ref[...])

    row = lambda w: pl.BlockSpec((t, w), lambda i: (i, 0))
    return _pc(
        body, name="mix_fwd", grid=(s // t,),
        in_specs=[row(A_W), row(D), pl.BlockSpec((t, D), lambda i: (i, 4)), pl.BlockSpec((t, D), lambda i: (i, 5)),
                  row(D), _const((A_W, D)), _const((D, D)), _const((D, D)), _const((1, D))],
        out_specs=[row(D)] * 5,
        out_shape=[SDS((s, D), BF)] * 4 + [SDS((s, D), F32)],
        compiler_params=_cp(1))(sa, ob, z, z, x, a_out, b_out, w_mix, g2)


def ffn_up_fwd(h1, g3, w_up3, conv_w, conv_b, t):
    s = h1.shape[0]
    bn = w_up3.shape[2]

    def body(x_ref, g_ref, wg_ref, wv_ref, cwg_ref, cwv_ref, cbg_ref, cbv_ref,
             c_ref, ug_ref, uv_ref, cg_ref, cv_ref, ff_ref, c_sc, carry):
        i, j = pl.program_id(0), pl.program_id(1)

        @pl.when(j == 0)
        def _():
            c = _rms(x_ref[...], g_ref[...]).astype(BF)
            c_sc[...] = c
            c_ref[...] = c

        @pl.when(i == 0)
        def _():
            carry[j] = jnp.zeros((2, 8, bn), F32)

        def branch(k, w_ref, cw_ref, cb_ref, u_ref, o_ref):
            ub = _dot(c_sc[...], w_ref[...]).astype(BF)
            u_ref[...] = ub
            u = ub.astype(F32)
            ext = jnp.concatenate([carry[j, k], u], axis=0)
            carry[j, k] = u[t - 8:]
            w = cw_ref[...]
            cc = (cb_ref[...] + w[0:1] * pltpu.roll(ext, 2, 0) + w[1:2] * pltpu.roll(ext, 1, 0) + w[2:3] * ext)[8:]
            cc = cc.astype(BF)
            o_ref[...] = cc
            return cc

        g = _gelu(branch(0, wg_ref, cwg_ref, cbg_ref, ug_ref, cg_ref))
        ff_ref[...] = g * branch(1, wv_ref, cwv_ref, cbv_ref, uv_ref, cv_ref)

    col = lambda rows, off: pl.BlockSpec((rows, bn), lambda i, j: (0, j + off))
    out = pl.BlockSpec((t, bn), lambda i, j: (i, j))
    return _pc(
        body, name="ffn_up_fwd", grid=(s // t, 2),
        in_specs=[pl.BlockSpec((t, D), lambda i, j: (i, 0)), _const((1, D)),
                  pl.BlockSpec((None, D, bn), lambda i, j: (j, 0, 0)), pl.BlockSpec((None, D, bn), lambda i, j: (j + 2, 0, 0)),
                  col(3, 0), col(3, 2), col(1, 0), col(1, 2)],
        out_specs=[pl.BlockSpec((t, D), lambda i, j: (i, 0))] + [out] * 5,
        out_shape=[SDS((s, D), BF)] + [SDS((s, D_FF), BF)] * 5,
        scratch_shapes=[pltpu.VMEM((t, D), BF), pltpu.VMEM((2, 2, 8, bn), F32)],
        compiler_params=_cp(2))(h1, g3, w_up3, w_up3, conv_w, conv_w, conv_b, conv_b)


def ffn_up_bwd(ug, uv, cg, cv, dff, conv_w, w_up, h1, g3, dh2, t):
    s = h1.shape[0]
    nt = s // t
    hb = t // 8
    bn = 1408
    r = t + 8

    def body(ug_ref, uv_ref, cg_ref, cv_ref, cag_ref, cav_ref, d_ref, da_ref, cw_ref, w_ref, x_ref, g_ref, dres_ref,
             du_ref, gw_ref, gb_ref, dx_ref, gg_ref):
        i = pl.program_id(0)

        @pl.when(i == 0)
        def _():
            gw_ref[...] = jnp.zeros((3, 2 * D_FF), F32)
            gb_ref[...] = jnp.zeros((1, 2 * D_FF), F32)
            gg_ref[...] = jnp.zeros((1, D), F32)

        more = (i < nt - 1).astype(BF)

        def gate(c_g, c_v, d_):
            gl, dgl = _gelu_and_grad(c_g)
            return (d_ * c_v * dgl).astype(F32), (d_ * gl).astype(F32)

        def back(dc, dc_next, u_ref, cols, off):
            w = cw_ref[:, off:off + bn]
            d_ext = jnp.concatenate([dc, dc_next], axis=0)
            d1, d2 = pltpu.roll(d_ext, r - 1, 0)[:t], pltpu.roll(d_ext, r - 2, 0)[:t]
            du_ref[:, off:off + bn] = (w[2:3] * dc + w[1:2] * d1 + w[0:1] * d2).astype(BF)
            u = u_ref[:, cols].astype(F32)
            gw_ref[0:1, off:off + bn] += jnp.sum(d2 * u, axis=0, keepdims=True)
            gw_ref[1:2, off:off + bn] += jnp.sum(d1 * u, axis=0, keepdims=True)
            gw_ref[2:3, off:off + bn] += jnp.sum(dc * u, axis=0, keepdims=True)
            gb_ref[:, off:off + bn] += jnp.sum(dc, axis=0, keepdims=True)

        for kb in range(D_FF // bn):
            cols = slice(kb * bn, (kb + 1) * bn)
            dg, dv = gate(cg_ref[:, cols], cv_ref[:, cols], d_ref[:, cols])
            dg_n, dv_n = gate(cag_ref[:, cols], cav_ref[:, cols], da_ref[:, cols] * more)
            back(dg, dg_n, ug_ref, cols, kb * bn)
            back(dv, dv_n, uv_ref, cols, D_FF + kb * bn)

        acc = _dot_nt(du_ref[...], w_ref[...])
        dxn, dg3 = _rms_bwd(acc, x_ref[...], g_ref[...])
        dx_ref[...] = dres_ref[...] + dxn
        gg_ref[...] += dg3

    tile = lambda width: pl.BlockSpec((t, width), lambda i: (i, 0))
    after = pl.BlockSpec((8, D_FF), lambda i: (jnp.minimum((i + 1) * hb, nt * hb - 1), 0))
    return _pc(
        body, name="ffn_up_bwd", grid=(nt,),
        in_specs=[tile(D_FF)] * 4 + [after, after, tile(D_FF), after, _const((3, 2 * D_FF)), _const((D, 2 * D_FF)),
                                     tile(D), _const((1, D)), tile(D)],
        out_specs=[tile(2 * D_FF), _acc((3, 2 * D_FF)), _acc((1, 2 * D_FF)), tile(D), _acc((1, D))],
        out_shape=[SDS((s, 2 * D_FF), BF), SDS((3, 2 * D_FF), F32), SDS((1, 2 * D_FF), F32),
                   SDS((s, D), F32), SDS((1, D), F32)],
        compiler_params=_cp(1))(ug, uv, cg, cv, cg, cv, dff, dff, conv_w, w_up, h1, g3, dh2)


def out_fwd(ff, h1, p, tgt, w_down, w_pg, w_ple, g4, g5, t):
    s = h1.shape[0]

    def body(ff_ref, h1_ref, p_ref, t_ref, wd_ref, wpg_ref, wpl_ref, g4_ref, g5_ref,
             f_ref, h2_ref, pg_ref, pe_ref, dy_ref, loss_ref):
        @pl.when(pl.program_id(0) == 0)
        def _():
            loss_ref[...] = jnp.zeros((1, 1), F32)

        f = _dot(ff_ref[...], wd_ref[...]).astype(BF)
        f_ref[...] = f
        h2 = h1_ref[...] + _rms(f.astype(F32), g4_ref[...])
        h2b = h2.astype(BF)
        h2_ref[...] = h2b
        pg = _dot(h2b, wpg_ref[...]).astype(BF)
        pe = _dot(p_ref[...].astype(BF), wpl_ref[...]).astype(BF)
        pg_ref[...] = pg
        pe_ref[...] = pe
        y = h2 + _rms(jax.nn.sigmoid(pg.astype(F32)) * pe.astype(F32), g5_ref[...])
        err = y - t_ref[...]
        dy_ref[...] = err * (1.0 / D)
        loss_ref[...] += (0.5 / D) * jnp.sum(err * err)

    row = lambda w: pl.BlockSpec((t, w), lambda i: (i, 0))
    return _pc(
        body, name="out_fwd", grid=(s // t,),
        in_specs=[row(D_FF), row(D), row(PLE), row(D), _const((D_FF, D)), _const((D, D)), _const((PLE, D)),
                  _const((1, D)), _const((1, D))],
        out_specs=[row(D)] * 5 + [_acc((1, 1))],
        out_shape=[SDS((s, D), BF)] * 4 + [SDS((s, D), F32), SDS((1, 1), F32)],
        compiler_params=_cp(1))(ff, h1, p, tgt, w_down, w_pg, w_ple, g4, g5)


def out_bwd(dy, pg, pe, f, g5, g4, w_pg, w_down, t):
    s = dy.shape[0]

    def body(dy_ref, pg_ref, pe_ref, f_ref, g5_ref, g4_ref, wpg_ref, wd_ref,
             dh2_ref, dpe_ref, dpg_ref, df_ref, dff_ref, gg5_ref, gg4_ref):
        @pl.when(pl.program_id(0) == 0)
        def _():
            gg5_ref[...] = jnp.zeros((1, D), F32)
            gg4_ref[...] = jnp.zeros((1, D), F32)

        dy_ = dy_ref[...]
        pg_ = pg_ref[...].astype(F32)
        pe_ = pe_ref[...].astype(F32)
        sg = jax.nn.sigmoid(pg_)
        dple, dg5 = _rms_bwd(dy_, sg * pe_, g5_ref[...])
        gg5_ref[...] += dg5
        dpe_ref[...] = (dple * sg).astype(BF)
        dpg = (dple * pe_ * sg * (1.0 - sg)).astype(BF)
        dpg_ref[...] = dpg
        dh2 = dy_ + _dot_nt(dpg, wpg_ref[...])
        dh2_ref[...] = dh2
        df, dg4 = _rms_bwd(dh2, f_ref[...].astype(F32), g4_ref[...])
        gg4_ref[...] += dg4
        dfb = df.astype(BF)
        df_ref[...] = dfb
        dff_ref[...] = _dot_nt(dfb, wd_ref[...]).astype(BF)

    row = lambda w: pl.BlockSpec((t, w), lambda i: (i, 0))
    return _pc(
        body, name="out_bwd", grid=(s // t,),
        in_specs=[row(D), row(D), row(D), row(D), _const((1, D)), _const((1, D)), _const((D, D)), _const((D_FF, D))],
        out_specs=[row(D), row(D), row(D), row(D), row(D_FF), _acc((1, D)), _acc((1, D))],
        out_shape=[SDS((s, D), F32), SDS((s, D), BF), SDS((s, D), BF), SDS((s, D), BF), SDS((s, D_FF), BF),
                   SDS((1, D), F32), SDS((1, D), F32)],
        compiler_params=_cp(1))(dy, pg, pe, f, g5, g4, w_pg, w_down)


def nt_normbwd(dys, w, xin, gain, dres, t, name, side=None):
    s = xin.shape[0]
    nt = s // t
    np_ = len(dys)

    def body(*refs):
        ins_, s_in, (dx_ref, gg_ref), s_out, _, s_scr = _split_side(refs, np_ + 4, 2, 0, side)
        dy_refs = ins_[:np_]
        w_ref, x_ref, g_ref, dres_ref = ins_[np_:]
        i = pl.program_id(0)

        @pl.when(i == 0)
        def _():
            gg_ref[...] = jnp.zeros((1, D), F32)
            if side:
                side.start(s_in, s_out, *s_scr)

        acc = _dot(jnp.concatenate([r_[...] for r_ in dy_refs], axis=1), w_ref[...])
        dxn, dg = _rms_bwd(acc, x_ref[...], g_ref[...])
        dx_ref[...] = dres_ref[...] + dxn
        gg_ref[...] += dg
        if side:
            @pl.when(i == nt - 1)
            def _():
                side.finish(s_in, s_out, *s_scr)

    row = lambda width: pl.BlockSpec((t, width), lambda i: (i, 0))
    si_specs, so_specs, so_shapes, s_scratch, s_ins = _side_specs(side)
    assert sum(dy.shape[1] for dy in dys) == w.shape[0]
    return _pc(
        body, name=name, grid=(nt,),
        in_specs=[row(dy.shape[1]) for dy in dys] + [_const(w.shape), row(D), _const((1, D)), row(D)] + si_specs,
        out_specs=[row(D), _acc((1, D))] + so_specs,
        out_shape=[SDS((s, D), F32), SDS((1, D), F32)] + so_shapes, scratch_shapes=s_scratch,
        compiler_params=_cp(1))(*dys, w, xin, gain, dres, *s_ins)


def mix_bwd(dh1, mx, z, ya, yb, g2, w_mix, a_out, b_out, t, side=None):
    s = dh1.shape[0]
    nt = s // t

    def body(*refs):
        ins_, s_in, outs_, s_out, _, s_scr = _split_side(refs, 10, 8, 0, side)
        dh_ref, mx_ref, ga_ref, gb_ref, ya_ref, yb_ref, g2_ref, wm_ref, ao_ref, bo_ref = ins_
        dmx_ref, dya_ref, dyb_ref, dga_ref, dgb_ref, dsa_ref, dob_ref, gg2_ref = outs_

        @pl.when(pl.program_id(0) == 0)
        def _():
            gg2_ref[...] = jnp.zeros((1, D), F32)
            if side:
                side.start(s_in, s_out, *s_scr)

        dmx, dg2 = _rms_bwd(dh_ref[...], mx_ref[...].astype(F32), g2_ref[...])
        gg2_ref[...] += dg2
        dmxb = dmx.astype(BF)
        dmx_ref[...] = dmxb
        dmp = _dot_nt(dmxb, wm_ref[...]).astype(BF)

        def gate(g_ref, y_ref, dy_ref, dg_ref, w_ref, dz_ref):
            sg = jax.nn.sigmoid(g_ref[...])
            dyb_ = dmp * sg
            dy_ref[...] = dyb_
            dg_ref[...] = dyb_ * y_ref[...] * (1.0 - sg)
            dz_ref[...] = _dot_nt(dyb_, w_ref[...]).astype(BF)

        gate(ga_ref, ya_ref, dya_ref, dga_ref, ao_ref, dsa_ref)
        gate(gb_ref, yb_ref, dyb_ref, dgb_ref, bo_ref, dob_ref)
        if side:
            @pl.when(pl.program_id(0) == nt - 1)
            def _():
                side.finish(s_in, s_out, *s_scr)

    row = lambda w: pl.BlockSpec((t, w), lambda i: (i, 0))
    si_specs, so_specs, so_shapes, s_scratch, s_ins = _side_specs(side)
    return _pc(
        body, name="mix_bwd", grid=(nt,),
        in_specs=[row(D), row(D), pl.BlockSpec((t, D), lambda i: (i, 4)), pl.BlockSpec((t, D), lambda i: (i, 5)),
                  row(D), row(D), _const((1, D)), _const((D, D)), _const((A_W, D)), _const((D, D))] + si_specs,
        out_specs=[row(D)] * 5 + [row(A_W), row(D), _acc((1, D))] + so_specs,
        out_shape=[SDS((s, D), BF)] * 5 + [SDS((s, A_W), BF), SDS((s, D), BF), SDS((1, D), F32)] + so_shapes,
        scratch_shapes=s_scratch,
        compiler_params=_cp(1))(dh1, mx, z, z, ya, yb, g2, w_mix, a_out, b_out, *s_ins)


def sgu_bwd(z, dsa, ln_g, ln_b, w_cat, w_cat_t, bias_full, bdm, t):
    s = z.shape[0]
    nt = s // t
    nch = t // A_C

    def body(u_ref, v_ref, dsa_ref, g_ref, b_ref, wc_ref, wct_ref, bias_ref, bdm_ref,
             duv_ref, glg_ref, glb_ref, gws_ref, gbs_ref, ds_acc):
        i = pl.program_id(0)

        @pl.when(i == 0)
        def _():
            glg_ref[...] = jnp.zeros((1, A_W), F32)
            glb_ref[...] = jnp.zeros((1, A_W), F32)
            gws_ref[...] = jnp.zeros((A_C, A_G * A_C), F32)
            ds_acc[...] = jnp.zeros((A_C, A_W), F32)

        lng, bdm_ = g_ref[...], bdm_ref[...]
        rec = [_sgu_recompute(v_ref[pl.ds(ci * A_C, A_C), :], lng, b_ref[...]) for ci in range(nch)]
        spread_vn = _sgu_spread([r_[3] for r_ in rec], bdm_)
        mixed = _dot(_sgu_weights(wc_ref, False), spread_vn)
        dsas, dss, dgus = [], [], []
        for ci in range(nch):
            rows = pl.ds(ci * A_C, A_C)
            gu, dgu = _gelu_and_grad(u_ref[rows, :])
            dsa_ = dsa_ref[rows, :]
            ds = dsa_ * gu
            ds_acc[...] += ds.astype(F32)
            dsas.append(dsa_)
            dgus.append(dgu)
            dss.append(ds)
        r = lax.broadcasted_iota(jnp.int32, (A_C, A_G * A_C), 0)
        c = lax.broadcasted_iota(jnp.int32, (A_C, A_G * A_C), 1) & (A_C - 1)
        gws_ref[...] += jnp.where(c <= r, _dot_nt(jnp.concatenate(dss, axis=1), spread_vn), 0.0)
        dvns = _dot(_sgu_weights(wct_ref, True), _sgu_spread(dss, bdm_))
        for ci in range(nch):
            rows = pl.ds(ci * A_C, A_C)
            dgv, rstd, xhat, _ = rec[ci]
            dvn = dvns[:, ci * A_W:(ci + 1) * A_W]
            glb_ref[...] += jnp.sum(dvn, axis=0, keepdims=True)
            glg_ref[...] += jnp.sum(dvn * xhat, axis=0, keepdims=True)
            dxh = dvn * lng
            dgv_ = rstd * (dxh - jnp.mean(dxh, axis=-1, keepdims=True)
                           - xhat * jnp.mean(dxh * xhat, axis=-1, keepdims=True))
            s_ = mixed[:, ci * A_W:(ci + 1) * A_W] + bias_ref[...]
            duv_ref[rows, :A_W] = dsas[ci] * dgus[ci] * s_.astype(BF)
            duv_ref[rows, A_W:] = (dgv_ * dgv).astype(BF)

        @pl.when(i == nt - 1)
        def _():
            acc = ds_acc[...]
            for g in range(A_G):
                gbs_ref[:, g:g + 1] = jnp.sum(acc[:, g * A_GD:(g + 1) * A_GD], axis=1, keepdims=True)

    return _pc(
        body, name="sgu_bwd", grid=(nt,),
        in_specs=[pl.BlockSpec((t, A_W), lambda i: (i, 0)), pl.BlockSpec((t, A_W), lambda i: (i, 1)),
                  pl.BlockSpec((t, A_W), lambda i: (i, 0)),
                  _const((1, A_W)), _const((1, A_W)), _const((A_C, A_G * A_C)), _const((A_C, A_G * A_C)),
                  _const((A_C, A_W)), _const((A_G * A_C, A_W))],
        out_specs=[pl.BlockSpec((t, D), lambda i: (i, 0)), _acc((1, A_W)), _acc((1, A_W)),
                   _acc((A_C, A_G * A_C)), _acc((A_C, A_G))],
        out_shape=[SDS((s, D), BF), SDS((1, A_W), F32), SDS((1, A_W), F32), SDS((A_C, A_G * A_C), F32),
                   SDS((A_C, A_G), F32)],
        scratch_shapes=[pltpu.VMEM((A_C, A_W), F32)],
        compiler_params=_cp(1))(z, z, dsa, ln_g, ln_b, w_cat, w_cat_t, bias_full, bdm)


def gla_bwd(z, qk32, zl, o, dob, states, wgk, bias, wn, ltri, ltri_t, t, side=None):
    s = z.shape[0]
    nt = s // t
    nc = t // B_C
    hpb = GLA_HPB
    assert hpb == B_H
    kw, vw = hpb * B_HK, hpb * B_HV

    def body(*refs):
        ins_, s_in, outs_, s_out, scr_, s_scr = _split_side(refs, 12, 7, 5, side)
        qk_ref, v_ref, og_ref, lr_ref, o_ref, dob_ref, st_ref, wgk_ref, bias_ref, wn_ref, l_ref, lt_ref = ins_
        dqk_ref, dv_ref, dog_ref, dpre_ref, dlr_ref, gbias_ref, gwn_ref = outs_
        dst_sc, dv_sc, dqd_sc, dkt_sc, ddec_sc = scr_
        i = pl.program_id(0)
        g = pl.program_id(1)

        @pl.when((i == 0) & (g == 0))
        def _():
            gbias_ref[...] = jnp.zeros((B_H, 1, B_HK), F32)
            gwn_ref[...] = jnp.zeros((1, B_HV), F32)
            if side:
                side.start(s_in, s_out, *s_scr)

        @pl.when(i == 0)
        def _():
            dst_sc[...] = jnp.zeros((B_H, B_HV, B_HK), F32)

        lr, l, lt = lr_ref[...], l_ref[...], lt_ref[...]
        keep, keep_t = l > 0, lt > 0
        wn_ = wn_ref[...]
        last = lax.broadcasted_iota(jnp.int32, (nc, B_C, B_HK), 1) == B_C - 1
        for hh in range(hpb):
            h = hh
            cv, ck = slice(hh * B_HV, (hh + 1) * B_HV), slice(hh * B_HK, (hh + 1) * B_HK)
            pre, b, bl, eb, enb, etb, qd, ki, kt = _gla_decays(qk_ref[:, cv], lr, wgk_ref[:, ck], bias_ref[:, ck], l, t)
            qdb, kib, ktb = qd.astype(BF), ki.astype(BF), kt.astype(BF)
            vb = v_ref[:, cv]
            o_ = o_ref[:, cv].astype(F32)
            og = og_ref[:, cv].astype(F32)
            sog = jax.nn.sigmoid(og)
            dob_ = dob_ref[:, cv].astype(F32)
            don = dob_ * og * sog
            do, dwn = _rms_bwd(don, o_, wn_)
            gwn_ref[...] += dwn
            dog_ref[:, cv] = (dob_ * _rms(o_, wn_) * sog * (1.0 + og * (1.0 - sog))).astype(BF)
            dob16 = do.astype(BF)
            sc_t = jnp.where(keep_t, _dot_nt(kib, qdb), 0.0).astype(BF)
            dsc = jnp.where(keep, _dot_nt(dob16, vb), 0.0).astype(BF)
            dsc_t = jnp.where(keep_t, _dot_nt(vb, dob16), 0.0).astype(BF)
            dv_sc[hh] = _dot(sc_t, dob16)
            dqd_sc[hh] = _dot(dsc, kib)
            dki = _dot(dsc_t, qdb)
            for n in reversed(range(nc)):
                rows = slice(n * B_C, (n + 1) * B_C)
                dst = dst_sc[h]
                dstb = dst.astype(BF)
                stp = st_ref[n, hh]
                dv_sc[hh, rows, :] += _dot_nt(ktb[rows], dstb)
                dkt_sc[hh, rows, :] = _dot(vb[rows], dstb)
                dqd_sc[hh, rows, :] += _dot(dob16[rows], stp)
                dec = jnp.exp(bl[n * B_C:n * B_C + 1, :])
                ddec_sc[hh, n] = jnp.sum(dst * stp.astype(F32), axis=0, keepdims=True) * dec
                dst_sc[h] = dst * dec + _dot_tn(dob16[rows], qdb[rows])
            dqd, dkt = dqd_sc[hh], dkt_sc[hh]
            dv_ref[:, cv] = dv_sc[hh].astype(BF)
            dqk_ref[:, hh * B_HV:hh * B_HV + B_HK] = (dqd * eb * (B_HK ** -0.5)).astype(BF)
            dqk_ref[:, hh * B_HV + B_HK:(hh + 1) * B_HV] = (dki * enb + dkt * etb).astype(BF)
            dktkt = dkt * kt
            db3 = (dqd * qd - dki * ki - dktkt).reshape(nc, B_C, B_HK)
            dbl = jnp.sum(dktkt.reshape(nc, B_C, B_HK), axis=1, keepdims=True) + ddec_sc[hh]
            db = (db3 + jnp.where(last, dbl, 0.0)).reshape(t, B_HK)
            dla = _ldot3(lt, db)
            dpre = dla * (1.0 / 16.0) * (1.0 - jax.nn.sigmoid(pre))
            dpreb = dpre.astype(BF)
            dpre_ref[:, ck] = dpreb
            gbias_ref[h] += jnp.sum(dpre, axis=0, keepdims=True)
            dlr_h = _dot_nt(dpreb, wgk_ref[:, ck])
            dlr = dlr_h if hh == 0 else dlr + dlr_h
        dlr_ref[...] = dlr.astype(BF)
        if side:
            @pl.when((i == nt - 1) & (g == B_H // hpb - 1))
            def _():
                side.finish(s_in, s_out, *s_scr)

    rv = lambda i: nt - 1 - i
    si_specs, so_specs, so_shapes, s_scratch, s_ins = _side_specs(side)
    vo, go = 2048 // vw, 3072 // vw
    tile = lambda off: pl.BlockSpec((t, vw), lambda i, g: (rv(i), off + g))
    return _pc(
        body, name="gla_bwd", grid=(nt, B_H // hpb),
        in_specs=[tile(0), tile(vo), tile(go), pl.BlockSpec((t, LANE), lambda i, g: (rv(i), 0)), tile(0), tile(0),
                  pl.BlockSpec((nc, hpb, B_HV, B_HK), lambda i, g: (rv(i), g, 0, 0)),
                  pl.BlockSpec((LANE, kw), lambda i, g: (0, g)), pl.BlockSpec((1, kw), lambda i, g: (0, g)),
                  _const((1, B_HV)), _const((t, t)), _const((t, t))] + si_specs,
        out_specs=[tile(0), tile(0), tile(0), pl.BlockSpec((t, kw), lambda i, g: (rv(i), g)),
                   pl.BlockSpec((t, LANE), lambda i, g: (rv(i), 0)), _acc((B_H, 1, B_HK)), _acc((1, B_HV))] + so_specs,
        out_shape=[SDS((s, D), BF), SDS((s, D), BF), SDS((s, D), BF), SDS((s, B_H * B_HK), BF), SDS((s, LANE), BF),
                   SDS((B_H, 1, B_HK), F32), SDS((1, B_HV), F32)] + so_shapes,
        scratch_shapes=[pltpu.VMEM((B_H, B_HV, B_HK), F32), pltpu.VMEM((hpb, t, B_HV), F32),
                        pltpu.VMEM((hpb, t, B_HK), F32), pltpu.VMEM((hpb, t, B_HK), F32),
                        pltpu.VMEM((hpb, nc, 1, B_HK), F32)] + s_scratch,
        compiler_params=_cp(2))(qk32, z, z, zl, o, dob, states, wgk, bias, wn, ltri, ltri_t, *s_ins)


def mm_tn(a, b, name, tk=2048):
    s, m = a.shape
    n = b.shape[1]
    bn = next(c for c in (1024, 1408, 512, 256, 128) if n % c == 0 and m * c * 4 <= 6 * 1024 * 1024)
    tk = min(tk, s)
    nk = s // tk

    def body(a_ref, b_ref, o_ref, acc):
        k = pl.program_id(1)

        @pl.when(k == 0)
        def _():
            acc[...] = jnp.zeros((m, bn), F32)

        acc[...] += _dot_tn(a_ref[...].astype(BF), b_ref[...])

        @pl.when(k == nk - 1)
        def _():
            o_ref[...] = acc[...].astype(BF)

    return _pc(
        body, name=name, grid=(n // bn, nk),
        in_specs=[pl.BlockSpec((tk, m), lambda j, k: (k, 0)), pl.BlockSpec((tk, bn), lambda j, k: (k, j))],
        out_specs=pl.BlockSpec((m, bn), lambda j, k: (0, j)),
        out_shape=SDS((m, n), BF), scratch_shapes=[pltpu.VMEM((m, bn), F32)], compiler_params=_cp(2))(a, b)


def _adamw(w, g, m, v):
    m = ADAM_B1 * m + (1.0 - ADAM_B1) * g
    v = ADAM_B2 * v + (1.0 - ADAM_B2) * (g * g)
    m_hat = m / (1.0 - ADAM_B1 ** ADAM_STEP)
    v_hat = v / (1.0 - ADAM_B2 ** ADAM_STEP)
    return -ADAM_LR * (m_hat / (jnp.sqrt(v_hat) + ADAM_EPS) + ADAM_WD * w), m, v


def _half_rows(rows):
    rh = rows // 2
    return rh, max(b for b in range(16, 257, 16) if rh % b == 0)


def _pc_sp(body, grid, in_specs, out_specs, out_shape, name):
    gs = pltpu.PrefetchScalarGridSpec(num_scalar_prefetch=1, grid=grid, in_specs=in_specs, out_specs=out_specs)
    return _pc(body, grid_spec=gs, out_shape=out_shape, name=name, compiler_params=_cp(len(grid)))


def adamw_item(own, sib, w, m, v):
    rows, cols = w.shape
    rh, br = _half_rows(rows)
    nbk = rh // br

    def fn(ins, outs, b, sc_ref):
        own_ref, sib_ref, w_ref, m_ref, v_ref = ins
        g_ = jnp.where(b // nbk == sc_ref[0], own_ref[...], sib_ref[...])
        outs[0][...] = g_
        outs[1][...], outs[2][...], outs[3][...] = _adamw(w_ref[...], g_, m_ref[...], v_ref[...])

    blk = (br, cols)
    mine = lambda b, sc_: (jnp.clip(b - sc_[0] * nbk, 0, nbk - 1), 0)
    theirs = lambda b, sc_: (jnp.clip(b - (1 - sc_[0]) * nbk, 0, nbk - 1), 0)
    each = lambda b, sc_: (b, 0)
    return ([(own, blk, mine), (sib, blk, theirs), (w, blk, each), (m, blk, each), (v, blk, each)],
            [(SDS((rows, cols), F32), blk, each)] * 4, 2 * nbk, fn)


def adamw_cols(sc, own, sib, w, m, v, name, cb=256):
    rows, cols = w.shape
    nk = cols // 2 // cb

    def body(sc_ref, own_ref, sib_ref, w_ref, m_ref, v_ref, go_ref, d_ref, mo_ref, vo_ref):
        g_ = jnp.where(pl.program_id(0) == sc_ref[0], own_ref[...], sib_ref[...])
        go_ref[...] = g_
        d_ref[...], mo_ref[...], vo_ref[...] = _adamw(w_ref[...], g_, m_ref[...], v_ref[...])

    mine = pl.BlockSpec((rows, cb), lambda h, k, sc_: (0, jnp.clip(k + (h - sc_[0]) * nk, 0, nk - 1)))
    theirs = pl.BlockSpec((rows, cb), lambda h, k, sc_: (0, jnp.clip(k + (h - 1 + sc_[0]) * nk, 0, nk - 1)))
    blk = pl.BlockSpec((rows, cb), lambda h, k, sc_: (0, h * nk + k))
    return _pc_sp(body, (2, nk), [mine, theirs, blk, blk, blk], [blk] * 4, [SDS((rows, cols), F32)] * 4,
                  name)(sc, own, sib, w, m, v)


def adamw_small(g, w, m, v):
    def body(g_ref, w_ref, m_ref, v_ref, d_ref, mo_ref, vo_ref):
        d_ref[...], mo_ref[...], vo_ref[...] = _adamw(w_ref[...], g_ref[...], m_ref[...], v_ref[...])

    vm = pl.BlockSpec(memory_space=pltpu.VMEM)
    return _pc(body, name="adamw_small", in_specs=[vm] * 4, out_specs=[vm] * 3, out_shape=[SDS(g.shape, F32)] * 3,
               compiler_params=pltpu.CompilerParams(vmem_limit_bytes=VMEM_LIMIT))(g, w, m, v)


def _pos():
    return lax.axis_index("x"), lax.axis_index("y"), lax.axis_index("c")


def _other_chips(x, y):
    return [(1 - x, y), (x, 1 - y), (1 - x, 1 - y)]


_ANY = pl.BlockSpec(memory_space=pltpu.HBM)


class _Side:
    def __init__(self, ins, out_shapes, nsem, start, finish):
        self.ins, self.out_shapes, self.start, self.finish = list(ins), list(out_shapes), start, finish
        self.scratch = [pltpu.SemaphoreType.DMA((nsem,)), pltpu.SemaphoreType.DMA((nsem,))]
        self.n_in, self.n_out = len(self.ins), len(self.out_shapes)


def _run_side(side, name):
    def body(*refs):
        args_ = (refs[:side.n_in], refs[side.n_in:side.n_in + side.n_out], *refs[side.n_in + side.n_out:])
        side.start(*args_)
        side.finish(*args_)

    return _pc(body, name=name, in_specs=[_ANY] * side.n_in, out_specs=[_ANY] * side.n_out,
               out_shape=side.out_shapes, scratch_shapes=side.scratch)(*side.ins)


def _split_rows(shape):
    return (shape[0] // 2) % 16 == 0


def _core_halves(shape, c):
    if _split_rows(shape):
        h = shape[0] // 2
        return ((pl.ds(pl.multiple_of(c * h, 16), h), slice(None)),
                (pl.ds(pl.multiple_of((1 - c) * h, 16), h), slice(None)))
    h = shape[1] // 2
    assert h % LANE == 0
    return ((slice(None), pl.ds(pl.multiple_of(c * h, LANE), h)),
            (slice(None), pl.ds(pl.multiple_of((1 - c) * h, LANE), h)))


def gather_side(bigs, tinies):
    nb, nt_ = len(bigs), len(tinies)

    def plan(ins, outs, ssem, rsem):
        x, y, c = _pos()
        me = 2 * x + y
        chips = _other_chips(x, y)
        sibling = (x, y, 1 - c)

        def copy(k, src, dst, to):
            return pltpu.make_async_remote_copy(src_ref=src, dst_ref=dst, send_sem=ssem.at[k], recv_sem=rsem.at[k],
                                                device_id=to, device_id_type=MESH)

        sends, landed, passed_on, tiny_landed = [], [], [], []
        for w in range(nb):
            mine, theirs = _core_halves(bigs[w].shape, c)
            for j, (cx, cy) in enumerate(chips):
                sends.append(copy(6 * w + j, ins[w].at[mine], outs[w].at[(me,) + mine], (cx, cy, c)))
                blk = outs[w].at[(2 * cx + cy,) + mine]
                landed.append((copy(6 * w + j, blk, blk, (cx, cy, c)), copy(6 * w + 3 + j, blk, blk, sibling)))
                blk = outs[w].at[(2 * cx + cy,) + theirs]
                passed_on.append(copy(6 * w + 3 + j, blk, blk, sibling))
        for w in range(nt_):
            for j, (cx, cy) in enumerate(chips):
                k = 6 * nb + 3 * w + j
                sends.append(copy(k, ins[nb + w], outs[nb + w].at[me], (cx, cy, c)))
                blk = outs[nb + w].at[2 * cx + cy]
                tiny_landed.append(copy(k, blk, blk, (cx, cy, c)))
        return sends, landed, passed_on, tiny_landed

    def start(ins, outs, ssem, rsem):
        for cp in plan(ins, outs, ssem, rsem)[0]:
            cp.start()

    def finish(ins, outs, ssem, rsem):
        sends, landed, passed_on, tiny_landed = plan(ins, outs, ssem, rsem)
        for arrived, forward in landed:
            arrived.wait_recv()
            forward.start()
        for arrived in tiny_landed + passed_on:
            arrived.wait_recv()
        for cp in sends + [forward for _, forward in landed]:
            cp.wait_send()

    return _Side(list(bigs) + list(tinies), [SDS((4,) + a.shape, a.dtype) for a in list(bigs) + list(tinies)],
                 6 * nb + 3 * nt_, start, finish)


def _sibling_side(srcs, out_shapes, pick):
    def plan(in_refs, out_refs, ssem, rsem):
        x, y, c = _pos()
        return [pltpu.make_async_remote_copy(src_ref=pick(in_refs[w], srcs[w].shape, c), dst_ref=out_refs[w],
                                             send_sem=ssem.at[w], recv_sem=rsem.at[w], device_id=(x, y, 1 - c),
                                             device_id_type=MESH) for w in range(len(srcs))]

    def start(*refs):
        for cp in plan(*refs):
            cp.start()

    def finish(*refs):
        for cp in plan(*refs):
            cp.wait()

    return _Side(srcs, out_shapes, len(srcs), start, finish)


def swap_side(gs):
    def half_shape(g):
        l, r, cols = g.shape
        return (l, r // 2, cols) if _split_rows((r, cols)) else (l, r, cols // 2)

    return _sibling_side(gs, [SDS(half_shape(g), g.dtype) for g in gs],
                         lambda ref, shape, c: ref.at[(slice(None),) + _core_halves(shape[1:], c)[1]])


def join_side(halves):
    return _sibling_side(halves, [SDS(h.shape, h.dtype) for h in halves], lambda ref, shape, c: ref)


def _both_sides(a, b):
    def split(ins, outs, *scr):
        return ((ins[:a.n_in], outs[:a.n_out], *scr[:2]), (ins[a.n_in:], outs[a.n_out:], *scr[2:]))

    def start(*refs):
        ra, rb = split(*refs)
        a.start(*ra)
        b.start(*rb)

    def finish(*refs):
        ra, rb = split(*refs)
        a.finish(*ra)
        b.finish(*rb)

    side = _Side(a.ins + b.ins, a.out_shapes + b.out_shapes, 1, start, finish)
    side.scratch = a.scratch + b.scratch
    return side


COL_BLOCK = 256


def _blockwise(sc, items, name):
    in_specs, out_specs, out_shapes, operands, spans = [], [], [], [], []
    start = 0
    for ins, outs, nb, _ in items:
        def spec(blk, idx, s0=start, nb=nb):
            return pl.BlockSpec(blk, lambda i, sc_: idx(jnp.clip(i - s0, 0, nb - 1), sc_))

        in_specs += [spec(blk, idx) for _, blk, idx in ins]
        out_specs += [spec(blk, idx) for _, blk, idx in outs]
        operands += [a for a, _, _ in ins]
        out_shapes += [s_ for s_, _, _ in outs]
        spans.append((start, start + nb))
        start += nb
    n_in = len(operands)

    def body(sc_ref, *refs):
        i = pl.program_id(0)
        at_in, at_out = 0, n_in
        for (ins, outs, _, fn), (lo, hi) in zip(items, spans):
            mine_in, mine_out = refs[at_in:at_in + len(ins)], refs[at_out:at_out + len(outs)]
            at_in, at_out = at_in + len(ins), at_out + len(outs)

            @pl.when((i >= lo) & (i < hi))
            def _(fn=fn, mine_in=mine_in, mine_out=mine_out, lo=lo):
                fn(mine_in, mine_out, i - lo, sc_ref)

    return _pc_sp(body, (start,), in_specs, out_specs, out_shapes, name)(sc, *operands)


def add_half_item(g, sib):
    l, r, cols = g.shape

    def fn(ins, outs, b, sc_ref):
        outs[0][...] = (ins[0][...].astype(F32) + ins[1][...].astype(F32)).astype(BF)

    if _split_rows((r, cols)):
        rh, br = _half_rows(r)
        nbk = rh // br
        blk = (1, br, cols)
        there = lambda b, sc_: (b // nbk, b % nbk, 0)
        return ([(g, blk, lambda b, sc_: (b // nbk, sc_[0] * nbk + b % nbk, 0)), (sib, blk, there)],
                [(SDS((l, rh, cols), BF), blk, there)], l * nbk, fn)
    nbk = cols // 2 // COL_BLOCK
    blk = (1, r, COL_BLOCK)
    there = lambda b, sc_: (b // nbk, 0, b % nbk)
    return ([(g, blk, lambda b, sc_: (b // nbk, 0, sc_[0] * nbk + b % nbk)), (sib, blk, there)],
            [(SDS((l, r, cols // 2), BF), blk, there)], l * nbk, fn)


def exchange_side(ps):
    n_ = len(ps)

    def width(p_):
        return p_.shape[2] if p_.shape[0] == 4 else p_.shape[2] // 4

    def plan(p_refs, got_refs, ssem, rsem):
        x, y, c = _pos()
        cps = []
        for w in range(n_):
            wd = width(ps[w])
            for j, (cx, cy) in enumerate(_other_chips(x, y)):
                to = 2 * cx + cy
                src = p_refs[w].at[to] if ps[w].shape[0] == 4 else p_refs[w].at[0, :, pl.ds(pl.multiple_of(to * wd, LANE), wd)]
                cps.append(pltpu.make_async_remote_copy(
                    src_ref=src, dst_ref=got_refs[w].at[j], send_sem=ssem.at[3 * w + j], recv_sem=rsem.at[3 * w + j],
                    device_id=(cx, cy, c), device_id_type=MESH))
        return cps

    def start(*refs):
        for cp in plan(*refs):
            cp.start()

    def finish(*refs):
        for cp in plan(*refs):
            cp.wait()

    return _Side(ps, [SDS((3, p_.shape[1], width(p_)), p_.dtype) for p_ in ps], 3 * n_, start, finish)


def sum4_item(p, got):
    _, rh, wd = got.shape

    def fn(ins, outs, b, sc_ref):
        p_ref, g_ref = ins
        outs[0][...] = ((p_ref[0].astype(F32) + g_ref[0].astype(F32))
                        + (g_ref[1].astype(F32) + g_ref[2].astype(F32)))

    if rh % 16:
        assert p.shape[0] == 4
        return ([(p, (1, rh, COL_BLOCK), lambda b, sc_: (sc_[1], 0, b)),
                 (got, (3, rh, COL_BLOCK), lambda b, sc_: (0, 0, b))],
                [(SDS((rh, wd), F32), (rh, COL_BLOCK), lambda b, sc_: (0, b))], wd // COL_BLOCK, fn)
    _, br = _half_rows(2 * rh)
    own = (lambda b, sc_: (sc_[1], b, 0)) if p.shape[0] == 4 else (lambda b, sc_: (0, b, sc_[1]))
    return ([(p, (1, br, wd), own), (got, (3, br, wd), lambda b, sc_: (0, b, 0))],
            [(SDS((rh, wd), F32), (br, wd), lambda b, sc_: (b, 0))], rh // br, fn)


def allreduce_small(g):
    rows = g.shape[0]
    rh = rows // 2

    def body(g_ref, out_ref, sib_buf, chip_buf, sum_sc, ssem, rsem):
        x, y, c = _pos()
        me = 2 * x + y
        sibling = (x, y, 1 - c)
        mine = pl.ds(pl.multiple_of(c * rh, 8), rh)

        def copy(k, src, dst, to):
            return pltpu.make_async_remote_copy(src_ref=src, dst_ref=dst, send_sem=ssem.at[k], recv_sem=rsem.at[k],
                                                device_id=to, device_id_type=MESH)

        cp = copy(0, g_ref, sib_buf, sibling)
        cp.start()
        cp.wait()
        sum_sc[...] = g_ref[...] + sib_buf[...]
        chips = _other_chips(x, y)
        cps = [copy(1 + j, sum_sc.at[mine], chip_buf.at[me], (cx, cy, c)) for j, (cx, cy) in enumerate(chips)]
        for cp in cps:
            cp.start()
        chip_buf[me] = sum_sc[mine, :]
        for j, (cx, cy) in enumerate(chips):
            copy(1 + j, sum_sc.at[mine], chip_buf.at[2 * cx + cy], (cx, cy, c)).wait_recv()
        for cp in cps:
            cp.wait_send()
        out_ref[mine, :] = (chip_buf[0] + chip_buf[1]) + (chip_buf[2] + chip_buf[3])
        cp = copy(4, out_ref.at[mine], out_ref.at[mine], sibling)
        cp.start()
        cp.wait()

    vm = pl.BlockSpec(memory_space=pltpu.VMEM)
    return _pc(body, name="allreduce_small", in_specs=[vm], out_specs=vm, out_shape=SDS((rows, LANE), F32),
               scratch_shapes=[pltpu.VMEM((rows, LANE), F32), pltpu.VMEM((4, rh, LANE), F32), pltpu.VMEM((rows, LANE), F32),
                               pltpu.SemaphoreType.DMA((5,)), pltpu.SemaphoreType.DMA((5,))],
               compiler_params=pltpu.CompilerParams(vmem_limit_bytes=VMEM_LIMIT))(g)


def _pack_small(entries, get):
    flat = jnp.concatenate([get(n).reshape(-1).astype(F32) for n, _ in entries])
    rows = -(-flat.shape[0] // (8 * LANE)) * 8
    return jnp.pad(flat, (0, rows * LANE - flat.shape[0])).reshape(rows, LANE)


def _unpack_small(entries, packed):
    out, off = {}, 0
    flat = packed.reshape(-1)
    for name, n in entries:
        out[name] = flat[off:off + n]
        off += n
    return out


def _cols_full(blk):
    return blk.transpose(1, 0, 2).reshape(blk.shape[1], 4 * blk.shape[2])


def kernel(x, p, pre_mix_norm, w_in, a_ln_g, a_ln_b, a_spatial_w, a_spatial_b, a_out, b_gk, b_gk_bias, b_out_norm, b_out, w_mix_out, post_mix_norm, pre_ffn_norm, w_up, conv_w, conv_b, w_down, post_ffn_norm, w_ple, w_ple_gate, post_ple_norm, loss_target, m_pre_mix_norm, m_w_in, m_a_ln_g, m_a_ln_b, m_a_spatial_w, m_a_spatial_b, m_a_out, m_b_gk, m_b_gk_bias, m_b_out_norm, m_b_out, m_w_mix_out, m_post_mix_norm, m_pre_ffn_norm, m_w_up, m_conv_w, m_conv_b, m_w_down, m_post_ffn_norm, m_w_ple, m_w_ple_gate, m_post_ple_norm, v_pre_mix_norm, v_w_in, v_a_ln_g, v_a_ln_b, v_a_spatial_w, v_a_spatial_b, v_a_out, v_b_gk, v_b_gk_bias, v_b_out_norm, v_b_out, v_w_mix_out, v_post_mix_norm, v_pre_ffn_norm, v_w_up, v_conv_w, v_conv_b, v_w_down, v_post_ffn_norm, v_w_ple, v_w_ple_gate, v_post_ple_norm):
    args = dict(locals())
    order = ['pre_mix_norm', 'w_in', 'a_ln_g', 'a_ln_b', 'a_spatial_w', 'a_spatial_b', 'a_out', 'b_gk', 'b_gk_bias',
             'b_out_norm', 'b_out', 'w_mix_out', 'post_mix_norm', 'pre_ffn_norm', 'w_up', 'conv_w', 'conv_b', 'w_down',
             'post_ffn_norm', 'w_ple', 'w_ple_gate', 'post_ple_norm']
    assert sorted(BIG + TINY + tuple(n for n, _ in SMALL)) == sorted(order)
    s = x.shape[1]
    xs = x.reshape(s, D)
    ps = p.reshape(s, PLE)
    tgt = loss_target.reshape(s, D)
    t_big = min(1024, s)
    t_mid = min(512, s)
    t_small = min(256, s)
    t_gla = min(256, s)
    mx_, my_, mc_ = _pos()
    me = 2 * mx_ + my_
    sc = jnp.stack([mc_, me]).astype(jnp.int32)
    shard = lambda n: args[n].reshape(args[n].shape[1:])

    mine = {n: shard(n).astype(BF) for n in BIG}
    mine["w_in"] = shard("w_in").T.astype(BF)
    mine.update({n: shard(n) for n in TINY})
    fill = lambda names, gots: {n: lax.dynamic_update_slice(got, mine[n][None], (me, 0, 0)) for n, got in zip(names, gots)}
    first = ("w_in",) + TINY
    full = fill(first, _run_side(gather_side([mine["w_in"]], [mine[n] for n in TINY]), "gather_first"))
    wi = full["w_in"].reshape(4 * 1540, D)
    seg = lambda a, b: wi[a:b]
    qk = [seg(1024 + h * B_HK, 1024 + (h + 1) * B_HK) for h in range(B_H)]
    kk = [seg(1536 + h * B_HK, 1536 + (h + 1) * B_HK) for h in range(B_H)]
    w_z = jnp.concatenate([seg(0, 1024)] + [m_ for h in range(B_H) for m_ in (qk[h], kk[h])]
                          + [seg(2048, 4096), seg(4112, 6160), seg(4096, 4112), jnp.zeros((LANE - B_RANK, D), BF)], axis=0)
    wgk = jnp.pad(_cols_full(full["b_gk"]).astype(BF), ((0, LANE - B_RANK), (0, 0)))
    w_conv = _cols_full(full["conv_w"])
    g1, g2, g3 = pre_mix_norm.reshape(1, D), post_mix_norm.reshape(1, D), pre_ffn_norm.reshape(1, D)
    g4, g5 = post_ffn_norm.reshape(1, D), post_ple_norm.reshape(1, D)
    ln_g, ln_b = a_ln_g.reshape(1, A_W), a_ln_b.reshape(1, A_W)
    w_s = a_spatial_w.reshape(A_G, A_C, A_C)
    w_cat = w_s.transpose(1, 0, 2).reshape(A_C, A_G * A_C)
    w_cat_t = w_s.transpose(2, 0, 1).reshape(A_C, A_G * A_C)
    bias_full = jnp.repeat(a_spatial_b.reshape(A_G, A_C).T, A_GD, axis=1)
    bdm = (jnp.arange(A_G * A_C)[:, None] // A_C == jnp.arange(A_W)[None, :] // A_GD).astype(BF)
    gk_bias = b_gk_bias.reshape(1, B_H * B_HK)
    wn = b_out_norm.reshape(1, B_HV)
    cb = conv_b.reshape(1, 2 * D_FF)
    idx = jnp.arange(t_gla)
    ltri = ((idx[:, None] // B_C == idx[None, :] // B_C) & (idx[None, :] <= idx[:, None])).astype(BF)

    a, z, qk32, zl, *gots = norm_matmul(xs, g1, w_z, D, t_big, "in_proj", nblk=6, f32_blk=1, tail_blk=48,
                                        side=gather_side([mine[n] for n in BIG[1:]], []))
    full.update(fill(BIG[1:], gots))
    w_aout, w_ple_f = _cols_full(full["a_out"]), _cols_full(full["w_ple"])
    w_bout, w_mix, w_pg = (full[n].reshape(D, D) for n in ("b_out", "w_mix_out", "w_ple_gate"))
    w_dn, w_up3 = full["w_down"].reshape(D_FF, D), full["w_up"]
    sa = sgu_fwd(z, ln_g, ln_b, w_cat, bias_full, bdm, t_mid)
    ob, o, states = gla_fwd(z, qk32, zl, wgk, gk_bias, wn, ltri, t_gla)
    ya, yb, mp, mx, h1 = mix_fwd(sa, ob, z, xs, w_aout, w_bout, w_mix, g2, t_mid)
    c, up_g, up_v, cg, cv, ff = ffn_up_fwd(h1, g3, w_up3, w_conv, cb, t_mid)
    f, h2, pg, pe, dy, loss = out_fwd(ff, h1, ps, tgt, w_dn, w_pg, w_ple_f, g4, g5, t_mid)

    dh2, dpe, dpg, df, dff, gg5, gg4 = out_bwd(dy, pg, pe, f, g5, g4, w_pg, w_dn, t_mid)
    dup, gcw, gcb, dh1, gg3 = ffn_up_bwd(up_g, up_v, cg, cv, dff, w_conv, _cols_full(w_up3), h1, g3, dh2, t_small)
    grads = {
        "w_up": mm_tn(c, dup, "dw_up")[None],
        "w_down": mm_tn(ff, df, "dw_down").reshape(4, D_FF // 4, D),
        "w_ple": mm_tn(ps, dpe, "dw_ple")[None],
        "w_ple_gate": mm_tn(h2, dpg, "dw_ple_gate").reshape(4, D // 4, D),
    }
    ffn_side = ("w_up", "w_down", "w_ple", "w_ple_gate")
    dmx, dya, dyb, dga, dgb, dsa, dob, gg2, *sibs = mix_bwd(dh1, mx, z, ya, yb, g2, w_mix, w_aout, w_bout, t_mid,
                                                            side=swap_side([grads[n] for n in ffn_side]))
    sib = dict(zip(ffn_side, sibs))
    duv, g_lng, g_lnb, g_wcat, g_bst = sgu_bwd(z, dsa, ln_g, ln_b, w_cat, w_cat_t, bias_full, bdm, t_mid)
    g_ws = g_wcat.reshape(A_C, A_G, A_C).transpose(1, 0, 2)
    grads.update({
        "a_out": mm_tn(sa, dya, "dw_a_out")[None],
        "b_out": mm_tn(ob, dyb, "dw_b_out").reshape(4, D // 4, D),
        "w_mix_out": mm_tn(mp, dmx, "dw_mix").reshape(4, D // 4, D),
    })

    def swap(names):
        sib.update(zip(names, _run_side(swap_side([grads[n] for n in names]), "swap_halves_" + names[0])))

    swap(("a_out", "b_out", "w_mix_out"))
    parts = dict(zip(BIG[1:], _blockwise(sc, [add_half_item(grads[n], sib[n]) for n in BIG[1:]], "partials")))
    dqk, dvb, dog, dpre, dlr, g_gkb, g_wn, *gots = gla_bwd(
        z, qk32, zl, o, dob, states, wgk, gk_bias, wn, ltri, ltri.T, t_gla,
        side=exchange_side([parts[n] for n in BIG[1:]]))
    reds = dict(zip(BIG[1:], _blockwise(sc, [sum4_item(parts[n], got_) for n, got_ in zip(BIG[1:], gots)], "sums")))
    segs = [duv, dqk, dvb, dog, dga, dgb, dlr]

    gz = [mm_tn(sg_, a, "dw_in_%d" % k) for k, sg_ in enumerate(segs)]
    gq = [gz[1][h * 256:h * 256 + B_HK] for h in range(B_H)]
    gk = [gz[1][h * 256 + B_HK:(h + 1) * 256] for h in range(B_H)]
    g_in = jnp.concatenate([gz[0]] + gq + gk + [gz[2], gz[3], gz[6][:B_RANK], gz[4], gz[5]], axis=0)
    grads["w_in"] = g_in.reshape(4, 1540, D)
    swap(("w_in",))
    parts["w_in"], = _blockwise(sc, [add_half_item(grads["w_in"], sib["w_in"])], "partial_w_in")
    dx, gg1, got_in, *sib_reds = nt_normbwd(
        segs, w_z, xs, g1, dh1, t_small, "in_bwd",
        side=_both_sides(exchange_side([parts["w_in"]]), join_side([reds[n] for n in BIG[1:]])))
    sib_red = dict(zip(BIG[1:], sib_reds))

    reds["w_in"], = _blockwise(sc, [sum4_item(parts["w_in"], got_in)], "sum_w_in")
    sib_red["w_in"], = _run_side(join_side([reds["w_in"]]), "join_w_in")
    outs = {}
    for n in BIG[1:]:
        res = _blockwise(sc, [adamw_item(reds[n], sib_red[n], shard(n), shard("m_" + n), shard("v_" + n))],
                         "adamw_" + n)
        outs[n] = [r_.reshape(args[n].shape) for r_ in res]
    res = adamw_cols(sc, reds["w_in"], sib_red["w_in"], shard("w_in").T, shard("m_w_in").T, shard("v_w_in").T,
                     "adamw_w_in")
    outs["w_in"] = [r_.T.reshape(w_in.shape) for r_ in res]

    small_g = {
        "pre_mix_norm": gg1, "a_ln_g": g_lng, "a_ln_b": g_lnb, "a_spatial_w": g_ws, "a_spatial_b": g_bst.T,
        "b_gk_bias": g_gkb, "b_out_norm": g_wn, "post_mix_norm": gg2, "pre_ffn_norm": gg3,
        "conv_b": gcb, "post_ffn_norm": gg4, "post_ple_norm": gg5,
        "b_gk": mm_tn(zl, dpre, "dw_gk")[:B_RANK], "conv_w": gcw,
        "loss": loss,
    }
    red_entries = SMALL + (("b_gk", B_RANK * 512), ("conv_w", 3 * 2 * D_FF), ("loss", 1))
    g_fin = _unpack_small(red_entries, allreduce_small(_pack_small(red_entries, lambda n: small_g[n])))
    g_fin["b_gk"] = lax.dynamic_slice(g_fin["b_gk"].reshape(B_RANK, 512), (0, me * B_HK), (B_RANK, B_HK))
    g_fin["conv_w"] = lax.dynamic_slice(g_fin["conv_w"].reshape(3, 2 * D_FF), (0, me * 1408), (3, 1408))
    upd_entries = SMALL + (("b_gk", B_RANK * B_HK), ("conv_w", 3 * 1408))
    res = adamw_small(*[_pack_small(upd_entries, get) for get in
                        (lambda n: g_fin[n], lambda n: args[n], lambda n: args["m_" + n], lambda n: args["v_" + n])])
    res = [_unpack_small(upd_entries, r_) for r_ in res]
    for n, _ in upd_entries:
        outs[n] = [r_[n].reshape(args[n].shape) for r_ in [g_fin] + res]

    return (g_fin["loss"].reshape(()), dx.reshape(x.shape), *[outs[n][0] for n in order], *[outs[n][1] for n in order],
            *[outs[n][2] for n in order], *[outs[n][3] for n in order])
```

```python
import math

import jax
import jax.numpy as jnp
from jax import lax
from jax.experimental import pallas as pl
from jax.experimental.pallas import tpu as pltpu

F32 = jnp.float32
BF = jnp.bfloat16
SDS = jax.ShapeDtypeStruct
MESH = pl.DeviceIdType.MESH

EPS = 1e-6
D = 1024
A_W = 512
A_G, A_C = 8, 128
A_GD = A_W // A_G
B_H, B_HK, B_HV = 4, 128, 256
B_C = 64
GLA_HPB = 4
B_RANK = 16
D_FF = 2816
PLE = 256
LANE = 128
VMEM_LIMIT = 60 * 1024 * 1024

ADAM_LR, ADAM_B1, ADAM_B2, ADAM_EPS, ADAM_WD, ADAM_STEP = 0.001, 0.9, 0.999, 1e-08, 0.01, 10

_GC = math.sqrt(2.0 / math.pi)
_GA = 0.044715

BIG = ("w_in", "a_out", "b_out", "w_mix_out", "w_up", "w_down", "w_ple", "w_ple_gate")
TINY = ("b_gk", "conv_w")
SMALL = (("pre_mix_norm", 1024), ("a_ln_g", 512), ("a_ln_b", 512), ("a_spatial_w", 131072),
         ("a_spatial_b", 1024), ("b_gk_bias", 512), ("b_out_norm", 256), ("post_mix_norm", 1024),
         ("pre_ffn_norm", 1024), ("conv_b", 5632), ("post_ffn_norm", 1024), ("post_ple_norm", 1024))


def _pc(body, **kw):
    return pl.pallas_call(body, **kw)


def _cp(n):
    return pltpu.CompilerParams(dimension_semantics=("arbitrary",) * n, vmem_limit_bytes=VMEM_LIMIT)


def _const(shape):
    nd = len(shape)
    return pl.BlockSpec(shape, lambda *_: (0,) * nd, pipeline_mode=pl.Buffered(1))


def _acc(shape):
    nd = len(shape)
    return pl.BlockSpec(shape, lambda *_: (0,) * nd)


def _dot(a, b):
    return jnp.dot(a, b, preferred_element_type=F32)


def _dot_nt(a, b):
    return lax.dot_general(a, b, (((1,), (1,)), ((), ())), preferred_element_type=F32)


def _dot_tn(a, b):
    return lax.dot_general(a, b, (((0,), (0,)), ((), ())), preferred_element_type=F32)


def _gelu(x):
    return 0.5 * x * (1.0 + jnp.tanh(_GC * (x + _GA * x * x * x)))


def _gelu_and_grad(x):
    x2 = x * x
    s = 0.5 * jnp.tanh((_GC * x) * (1.0 + _GA * x2)) + 0.5
    g = x * s
    return g, s + g * (1.0 - s) * ((6.0 * _GC * _GA) * x2 + 2.0 * _GC)


def _log_sigmoid(x):
    return jnp.minimum(x, 0.0) - jnp.log(1.0 + jnp.exp(-jnp.abs(x)))


def _rms(x, g):
    return x * lax.rsqrt(jnp.mean(x * x, axis=-1, keepdims=True) + EPS) * g


def _rms_bwd(dy, x, g):
    r = lax.rsqrt(jnp.mean(x * x, axis=-1, keepdims=True) + EPS)
    n = x * r
    dn = dy * g
    dx = r * (dn - n * jnp.mean(dn * n, axis=-1, keepdims=True))
    return dx, jnp.sum(dy * n, axis=0, keepdims=True)


def _ldot3(l, x):
    h = x.astype(BF)
    r = x - h.astype(F32)
    m = r.astype(BF)
    lo = (r - m.astype(F32)).astype(BF)
    return _dot(l, h) + _dot(l, m) + _dot(l, lo)


def _split_side(refs, n_in, n_out, n_scratch, side):
    si, so = (side.n_in, side.n_out) if side else (0, 0)
    cuts = [n_in, si, n_out, so, n_scratch]
    out, at = [], 0
    for c in cuts:
        out.append(refs[at:at + c])
        at += c
    return (*out, refs[at:])


def _side_specs(side):
    return ([_ANY] * side.n_in, [_ANY] * side.n_out, side.out_shapes, side.scratch, side.ins) if side else ([],) * 5


def norm_matmul(x, g, wt, bn, t, name, nblk, f32_blk, tail_blk, side=None):
    s, dm = x.shape
    w_spec = pl.BlockSpec((bn, dm), lambda i, j: (j, 0))
    nt = s // t

    def body(*refs):
        (x_ref, g_ref, w_ref, wl_ref), s_in, outs, s_out, (a_sc,), s_scr = _split_side(refs, 4, 4, 1, side)
        a_ref, z_ref, f32_ref, tail_ref = outs
        i, j = pl.program_id(0), pl.program_id(1)
        if side:
            @pl.when((i == 0) & (j == 0))
            def _():
                side.start(s_in, s_out, *s_scr)

        @pl.when(j == 0)
        def _():
            a = _rms(x_ref[...], g_ref[...]).astype(BF)
            a_sc[...] = a
            a_ref[...] = a
            tail_ref[...] = _dot_nt(a, wl_ref[...]).astype(BF)

        acc = _dot_nt(a_sc[...], w_ref[...])
        z_ref[...] = acc.astype(BF)

        @pl.when(j == f32_blk)
        def _():
            f32_ref[...] = acc
        if side:
            @pl.when((i == nt - 1) & (j == nblk - 1))
            def _():
                side.finish(s_in, s_out, *s_scr)

    si_specs, so_specs, so_shapes, s_scratch, s_ins = _side_specs(side)
    return _pc(
        body, name=name, grid=(nt, nblk),
        in_specs=[pl.BlockSpec((t, dm), lambda i, j: (i, 0)), _const((1, dm)), w_spec,
                  pl.BlockSpec((LANE, dm), lambda i, j: (tail_blk, 0), pipeline_mode=pl.Buffered(1))] + si_specs,
        out_specs=[pl.BlockSpec((t, dm), lambda i, j: (i, 0)), pl.BlockSpec((t, bn), lambda i, j: (i, j)),
                   pl.BlockSpec((t, bn), lambda i, j: (i, 0)), pl.BlockSpec((t, LANE), lambda i, j: (i, 0))] + so_specs,
        out_shape=[SDS((s, dm), BF), SDS((s, nblk * bn), BF), SDS((s, bn), F32), SDS((s, LANE), BF)] + so_shapes,
        scratch_shapes=[pltpu.VMEM((t, dm), BF)] + s_scratch, compiler_params=_cp(2))(x, g, wt, wt, *s_ins)


def _sgu_weights(wc_ref, transposed):
    r = lax.broadcasted_iota(jnp.int32, (A_C, A_G * A_C), 0)
    c = lax.broadcasted_iota(jnp.int32, (A_C, A_G * A_C), 1) & (A_C - 1)
    return jnp.where((r <= c) if transposed else (c <= r), wc_ref[...], 0.0).astype(BF)


def _sgu_spread(xs, bdm):
    return jnp.concatenate([jnp.tile(x, (A_G, 1)) * bdm for x in xs], axis=1)


def _sgu_recompute(v, lng, lnb):
    gv, dgv = _gelu_and_grad(v)
    gv = gv.astype(F32)
    mu = jnp.mean(gv, axis=-1, keepdims=True)
    xc = gv - mu
    rstd = lax.rsqrt(jnp.mean(xc * xc, axis=-1, keepdims=True) + EPS)
    xhat = xc * rstd
    return dgv, rstd, xhat, (xhat * lng + lnb).astype(BF)


def sgu_fwd(z, ln_g, ln_b, w_cat, bias_full, bdm, t):
    s = z.shape[0]
    nch = t // A_C

    def body(u_ref, v_ref, g_ref, b_ref, wc_ref, bias_ref, bdm_ref, sa_ref):
        vns = [_sgu_recompute(v_ref[pl.ds(ci * A_C, A_C), :], g_ref[...], b_ref[...])[3]
               for ci in range(nch)]
        mixed = _dot(_sgu_weights(wc_ref, False), _sgu_spread(vns, bdm_ref[...]))
        for ci in range(nch):
            rows = pl.ds(ci * A_C, A_C)
            s_ = mixed[:, ci * A_W:(ci + 1) * A_W] + bias_ref[...]
            sa_ref[rows, :] = _gelu(u_ref[rows, :]) * s_.astype(BF)

    return _pc(
        body, name="sgu_fwd", grid=(s // t,),
        in_specs=[pl.BlockSpec((t, A_W), lambda i: (i, 0)), pl.BlockSpec((t, A_W), lambda i: (i, 1)),
                  _const((1, A_W)), _const((1, A_W)), _const((A_C, A_G * A_C)), _const((A_C, A_W)),
                  _const((A_G * A_C, A_W))],
        out_specs=pl.BlockSpec((t, A_W), lambda i: (i, 0)),
        out_shape=SDS((s, A_W), BF), compiler_params=_cp(1))(z, z, ln_g, ln_b, w_cat, bias_full, bdm)


def _gla_decays(qk, lr, wgk, bias, l, t):
    nc = t // B_C
    q = qk[:, :B_HK].astype(F32) * (B_HK ** -0.5)
    k = qk[:, B_HK:].astype(F32)
    pre = _dot(lr, wgk) + bias
    la = _log_sigmoid(pre) * (1.0 / 16.0)
    b = _ldot3(l, la)
    b3 = b.reshape(nc, B_C, B_HK)
    bl = jnp.broadcast_to(b3[:, B_C - 1:B_C, :], (nc, B_C, B_HK)).reshape(t, B_HK)
    eb, enb, etb = jnp.exp(b), jnp.exp(-b), jnp.exp(bl - b)
    return pre, b, bl, eb, enb, etb, q * eb, k * enb, k * etb


def gla_fwd(z, qk32, zl, wgk, bias, wn, ltri, t):
    s = z.shape[0]
    nc = t // B_C
    hpb = GLA_HPB
    assert hpb == B_H
    kw, vw = hpb * B_HK, hpb * B_HV

    def body(qk_ref, v_ref, og_ref, lr_ref, wgk_ref, bias_ref, wn_ref, l_ref, ob_ref, o_ref, st_ref, st_sc, o_sc):
        @pl.when(pl.program_id(0) == 0)
        def _():
            st_sc[...] = jnp.zeros((B_H, B_HV, B_HK), F32)

        lr, l = lr_ref[...], l_ref[...]
        for hh in range(hpb):
            h = hh
            cv, ck = slice(hh * B_HV, (hh + 1) * B_HV), slice(hh * B_HK, (hh + 1) * B_HK)
            _, _, bl, _, _, _, qd, ki, kt = _gla_decays(qk_ref[:, cv], lr, wgk_ref[:, ck], bias_ref[:, ck], l, t)
            qd, ki, kt = qd.astype(BF), ki.astype(BF), kt.astype(BF)
            vb = v_ref[:, cv]
            sc = jnp.where(l > 0, _dot_nt(qd, ki), 0.0).astype(BF)
            o_sc[hh] = _dot(sc, vb)
            for n in range(nc):
                rows = slice(n * B_C, (n + 1) * B_C)
                st = st_sc[h]
                stb = st.astype(BF)
                st_ref[n, hh] = stb
                o_sc[hh, rows, :] += _dot_nt(qd[rows], stb)
                st_sc[h] = st * jnp.exp(bl[n * B_C:n * B_C + 1, :]) + _dot_tn(vb[rows], kt[rows])
            ob = o_sc[hh].astype(BF)
            o_ref[:, cv] = ob
            og = og_ref[:, cv].astype(F32)
            ob_ref[:, cv] = (_rms(ob.astype(F32), wn_ref[...]) * og * jax.nn.sigmoid(og)).astype(BF)

    vo, go = 2048 // vw, 3072 // vw
    return _pc(
        body, name="gla_fwd", grid=(s // t, B_H // hpb),
        in_specs=[pl.BlockSpec((t, vw), lambda i, g: (i, g)), pl.BlockSpec((t, vw), lambda i, g: (i, vo + g)),
                  pl.BlockSpec((t, vw), lambda i, g: (i, go + g)), pl.BlockSpec((t, LANE), lambda i, g: (i, 0)),
                  pl.BlockSpec((LANE, kw), lambda i, g: (0, g)), pl.BlockSpec((1, kw), lambda i, g: (0, g)),
                  _const((1, B_HV)), _const((t, t))],
        out_specs=[pl.BlockSpec((t, vw), lambda i, g: (i, g)), pl.BlockSpec((t, vw), lambda i, g: (i, g)),
                   pl.BlockSpec((nc, hpb, B_HV, B_HK), lambda i, g: (i, g, 0, 0))],
        out_shape=[SDS((s, D), BF), SDS((s, D), BF), SDS((s // B_C, B_H, B_HV, B_HK), BF)],
        scratch_shapes=[pltpu.VMEM((B_H, B_HV, B_HK), F32), pltpu.VMEM((hpb, t, B_HV), F32)],
        compiler_params=_cp(2))(qk32, z, z, zl, wgk, bias, wn, ltri)


def mix_fwd(sa, ob, z, x, a_out, b_out, w_mix, g2, t):
    s = x.shape[0]

    def body(sa_ref, ob_ref, ga_ref, gb_ref, x_ref, ao_ref, bo_ref, wm_ref, g2_ref,
             ya_ref, yb_ref, mp_ref, mx_ref, h1_ref):
        ya = _dot(sa_ref[...], ao_ref[...]).astype(BF)
        yb = _dot(ob_ref[...], bo_ref[...]).astype(BF)
        ya_ref[...] = ya
        yb_ref[...] = yb
        mp = (jax.nn.sigmoid(ga_ref[...].astype(F32)) * ya.astype(F32)
              + jax.nn.sigmoid(gb_ref[...].astype(F32)) * yb.astype(F32)).astype(BF)
        mp_ref[...] = mp
        mx = _dot(mp, wm_ref[...]).astype(BF)
        mx_ref[...] = mx
        h1_ref[...] = x_ref[...] + _rms(mx.astype(F32), g2_ref[...])

    row = lambda w: pl.BlockSpec((t, w), lambda i: (i, 0))
    return _pc(
        body, name="mix_fwd", grid=(s // t,),
        in_specs=[row(A_W), row(D), pl.BlockSpec((t, D), lambda i: (i, 4)), pl.BlockSpec((t, D), lambda i: (i, 5)),
                  row(D), _const((A_W, D)), _const((D, D)), _const((D, D)), _const((1, D))],
        out_specs=[row(D)] * 5,
        out_shape=[SDS((s, D), BF)] * 4 + [SDS((s, D), F32)],
        compiler_params=_cp(1))(sa, ob, z, z, x, a_out, b_out, w_mix, g2)


def ffn_up_fwd(h1, g3, w_up3, conv_w, conv_b, t):
    s = h1.shape[0]
    bn = w_up3.shape[2]

    def body(x_ref, g_ref, wg_ref, wv_ref, cwg_ref, cwv_ref, cbg_ref, cbv_ref,
             c_ref, ug_ref, uv_ref, cg_ref, cv_ref, ff_ref, c_sc, carry):
        i, j = pl.program_id(0), pl.program_id(1)

        @pl.when(j == 0)
        def _():
            c = _rms(x_ref[...], g_ref[...]).astype(BF)
            c_sc[...] = c
            c_ref[...] = c

        @pl.when(i == 0)
        def _():
            carry[j] = jnp.zeros((2, 8, bn), F32)

        def branch(k, w_ref, cw_ref, cb_ref, u_ref, o_ref):
            ub = _dot(c_sc[...], w_ref[...]).astype(BF)
            u_ref[...] = ub
            u = ub.astype(F32)
            ext = jnp.concatenate([carry[j, k], u], axis=0)
            carry[j, k] = u[t - 8:]
            w = cw_ref[...]
            cc = (cb_ref[...] + w[0:1] * pltpu.roll(ext, 2, 0) + w[1:2] * pltpu.roll(ext, 1, 0) + w[2:3] * ext)[8:]
            cc = cc.astype(BF)
            o_ref[...] = cc
            return cc

        g = _gelu(branch(0, wg_ref, cwg_ref, cbg_ref, ug_ref, cg_ref))
        ff_ref[...] = g * branch(1, wv_ref, cwv_ref, cbv_ref, uv_ref, cv_ref)

    col = lambda rows, off: pl.BlockSpec((rows, bn), lambda i, j: (0, j + off))
    out = pl.BlockSpec((t, bn), lambda i, j: (i, j))
    return _pc(
        body, name="ffn_up_fwd", grid=(s // t, 2),
        in_specs=[pl.BlockSpec((t, D), lambda i, j: (i, 0)), _const((1, D)),
                  pl.BlockSpec((None, D, bn), lambda i, j: (j, 0, 0)), pl.BlockSpec((None, D, bn), lambda i, j: (j + 2, 0, 0)),
                  col(3, 0), col(3, 2), col(1, 0), col(1, 2)],
        out_specs=[pl.BlockSpec((t, D), lambda i, j: (i, 0))] + [out] * 5,
        out_shape=[SDS((s, D), BF)] + [SDS((s, D_FF), BF)] * 5,
        scratch_shapes=[pltpu.VMEM((t, D), BF), pltpu.VMEM((2, 2, 8, bn), F32)],
        compiler_params=_cp(2))(h1, g3, w_up3, w_up3, conv_w, conv_w, conv_b, conv_b)


def ffn_up_bwd(ug, uv, cg, cv, dff, conv_w, w_up, h1, g3, dh2, t):
    s = h1.shape[0]
    nt = s // t
    hb = t // 8
    bn = 1408
    r = t + 8

    def body(ug_ref, uv_ref, cg_ref, cv_ref, cag_ref, cav_ref, d_ref, da_ref, cw_ref, w_ref, x_ref, g_ref, dres_ref,
             du_ref, gw_ref, gb_ref, dx_ref, gg_ref):
        i = pl.program_id(0)

        @pl.when(i == 0)
        def _():
            gw_ref[...] = jnp.zeros((3, 2 * D_FF), F32)
            gb_ref[...] = jnp.zeros((1, 2 * D_FF), F32)
            gg_ref[...] = jnp.zeros((1, D), F32)

        more = (i < nt - 1).astype(BF)

        def gate(c_g, c_v, d_):
            gl, dgl = _gelu_and_grad(c_g)
            return (d_ * c_v * dgl).astype(F32), (d_ * gl).astype(F32)

        def back(dc, dc_next, u_ref, cols, off):
            w = cw_ref[:, off:off + bn]
            d_ext = jnp.concatenate([dc, dc_next], axis=0)
            d1, d2 = pltpu.roll(d_ext, r - 1, 0)[:t], pltpu.roll(d_ext, r - 2, 0)[:t]
            du_ref[:, off:off + bn] = (w[2:3] * dc + w[1:2] * d1 + w[0:1] * d2).astype(BF)
            u = u_ref[:, cols].astype(F32)
            gw_ref[0:1, off:off + bn] += jnp.sum(d2 * u, axis=0, keepdims=True)
            gw_ref[1:2, off:off + bn] += jnp.sum(d1 * u, axis=0, keepdims=True)
            gw_ref[2:3, off:off + bn] += jnp.sum(dc * u, axis=0, keepdims=True)
            gb_ref[:, off:off + bn] += jnp.sum(dc, axis=0, keepdims=True)

        for kb in range(D_FF // bn):
            cols = slice(kb * bn, (kb + 1) * bn)
            dg, dv = gate(cg_ref[:, cols], cv_ref[:, cols], d_ref[:, cols])
            dg_n, dv_n = gate(cag_ref[:, cols], cav_ref[:, cols], da_ref[:, cols] * more)
            back(dg, dg_n, ug_ref, cols, kb * bn)
            back(dv, dv_n, uv_ref, cols, D_FF + kb * bn)

        acc = _dot_nt(du_ref[...], w_ref[...])
        dxn, dg3 = _rms_bwd(acc, x_ref[...], g_ref[...])
        dx_ref[...] = dres_ref[...] + dxn
        gg_ref[...] += dg3

    tile = lambda width: pl.BlockSpec((t, width), lambda i: (i, 0))
    after = pl.BlockSpec((8, D_FF), lambda i: (jnp.minimum((i + 1) * hb, nt * hb - 1), 0))
    return _pc(
        body, name="ffn_up_bwd", grid=(nt,),
        in_specs=[tile(D_FF)] * 4 + [after, after, tile(D_FF), after, _const((3, 2 * D_FF)), _const((D, 2 * D_FF)),
                                     tile(D), _const((1, D)), tile(D)],
        out_specs=[tile(2 * D_FF), _acc((3, 2 * D_FF)), _acc((1, 2 * D_FF)), tile(D), _acc((1, D))],
        out_shape=[SDS((s, 2 * D_FF), BF), SDS((3, 2 * D_FF), F32), SDS((1, 2 * D_FF), F32),
                   SDS((s, D), F32), SDS((1, D), F32)],
        compiler_params=_cp(1))(ug, uv, cg, cv, cg, cv, dff, dff, conv_w, w_up, h1, g3, dh2)


def out_fwd(ff, h1, p, tgt, w_down, w_pg, w_ple, g4, g5, t):
    s = h1.shape[0]

    def body(ff_ref, h1_ref, p_ref, t_ref, wd_ref, wpg_ref, wpl_ref, g4_ref, g5_ref,
             f_ref, h2_ref, pg_ref, pe_ref, dy_ref, loss_ref):
        @pl.when(pl.program_id(0) == 0)
        def _():
            loss_ref[...] = jnp.zeros((1, 1), F32)

        f = _dot(ff_ref[...], wd_ref[...]).astype(BF)
        f_ref[...] = f
        h2 = h1_ref[...] + _rms(f.astype(F32), g4_ref[...])
        h2b = h2.astype(BF)
        h2_ref[...] = h2b
        pg = _dot(h2b, wpg_ref[...]).astype(BF)
        pe = _dot(p_ref[...].astype(BF), wpl_ref[...]).astype(BF)
        pg_ref[...] = pg
        pe_ref[...] = pe
        y = h2 + _rms(jax.nn.sigmoid(pg.astype(F32)) * pe.astype(F32), g5_ref[...])
        err = y - t_ref[...]
        dy_ref[...] = err * (1.0 / D)
        loss_ref[...] += (0.5 / D) * jnp.sum(err * err)

    row = lambda w: pl.BlockSpec((t, w), lambda i: (i, 0))
    return _pc(
        body, name="out_fwd", grid=(s // t,),
        in_specs=[row(D_FF), row(D), row(PLE), row(D), _const((D_FF, D)), _const((D, D)), _const((PLE, D)),
                  _const((1, D)), _const((1, D))],
        out_specs=[row(D)] * 5 + [_acc((1, 1))],
        out_shape=[SDS((s, D), BF)] * 4 + [SDS((s, D), F32), SDS((1, 1), F32)],
        compiler_params=_cp(1))(ff, h1, p, tgt, w_down, w_pg, w_ple, g4, g5)


def out_bwd(dy, pg, pe, f, g5, g4, w_pg, w_down, t):
    s = dy.shape[0]

    def body(dy_ref, pg_ref, pe_ref, f_ref, g5_ref, g4_ref, wpg_ref, wd_ref,
             dh2_ref, dpe_ref, dpg_ref, df_ref, dff_ref, gg5_ref, gg4_ref):
        @pl.when(pl.program_id(0) == 0)
        def _():
            gg5_ref[...] = jnp.zeros((1, D), F32)
            gg4_ref[...] = jnp.zeros((1, D), F32)

        dy_ = dy_ref[...]
        pg_ = pg_ref[...].astype(F32)
        pe_ = pe_ref[...].astype(F32)
        sg = jax.nn.sigmoid(pg_)
        dple, dg5 = _rms_bwd(dy_, sg * pe_, g5_ref[...])
        gg5_ref[...] += dg5
        dpe_ref[...] = (dple * sg).astype(BF)
        dpg = (dple * pe_ * sg * (1.0 - sg)).astype(BF)
        dpg_ref[...] = dpg
        dh2 = dy_ + _dot_nt(dpg, wpg_ref[...])
        dh2_ref[...] = dh2
        df, dg4 = _rms_bwd(dh2, f_ref[...].astype(F32), g4_ref[...])
        gg4_ref[...] += dg4
        dfb = df.astype(BF)
        df_ref[...] = dfb
        dff_ref[...] = _dot_nt(dfb, wd_ref[...]).astype(BF)

    row = lambda w: pl.BlockSpec((t, w), lambda i: (i, 0))
    return _pc(
        body, name="out_bwd", grid=(s // t,),
        in_specs=[row(D), row(D), row(D), row(D), _const((1, D)), _const((1, D)), _const((D, D)), _const((D_FF, D))],
        out_specs=[row(D), row(D), row(D), row(D), row(D_FF), _acc((1, D)), _acc((1, D))],
        out_shape=[SDS((s, D), F32), SDS((s, D), BF), SDS((s, D), BF), SDS((s, D), BF), SDS((s, D_FF), BF),
                   SDS((1, D), F32), SDS((1, D), F32)],
        compiler_params=_cp(1))(dy, pg, pe, f, g5, g4, w_pg, w_down)


def nt_normbwd(dys, w, xin, gain, dres, t, name, side=None):
    s = xin.shape[0]
    nt = s // t
    np_ = len(dys)

    def body(*refs):
        ins_, s_in, (dx_ref, gg_ref), s_out, _, s_scr = _split_side(refs, np_ + 4, 2, 0, side)
        dy_refs = ins_[:np_]
        w_ref, x_ref, g_ref, dres_ref = ins_[np_:]
        i = pl.program_id(0)

        @pl.when(i == 0)
        def _():
            gg_ref[...] = jnp.zeros((1, D), F32)
            if side:
                side.start(s_in, s_out, *s_scr)

        acc = _dot(jnp.concatenate([r_[...] for r_ in dy_refs], axis=1), w_ref[...])
        dxn, dg = _rms_bwd(acc, x_ref[...], g_ref[...])
        dx_ref[...] = dres_ref[...] + dxn
        gg_ref[...] += dg
        if side:
            @pl.when(i == nt - 1)
            def _():
                side.finish(s_in, s_out, *s_scr)

    row = lambda width: pl.BlockSpec((t, width), lambda i: (i, 0))
    si_specs, so_specs, so_shapes, s_scratch, s_ins = _side_specs(side)
    assert sum(dy.shape[1] for dy in dys) == w.shape[0]
    return _pc(
        body, name=name, grid=(nt,),
        in_specs=[row(dy.shape[1]) for dy in dys] + [_const(w.shape), row(D), _const((1, D)), row(D)] + si_specs,
        out_specs=[row(D), _acc((1, D))] + so_specs,
        out_shape=[SDS((s, D), F32), SDS((1, D), F32)] + so_shapes, scratch_shapes=s_scratch,
        compiler_params=_cp(1))(*dys, w, xin, gain, dres, *s_ins)


def mix_bwd(dh1, mx, z, ya, yb, g2, w_mix, a_out, b_out, t, side=None):
    s = dh1.shape[0]
    nt = s // t

    def body(*refs):
        ins_, s_in, outs_, s_out, _, s_scr = _split_side(refs, 10, 8, 0, side)
        dh_ref, mx_ref, ga_ref, gb_ref, ya_ref, yb_ref, g2_ref, wm_ref, ao_ref, bo_ref = ins_
        dmx_ref, dya_ref, dyb_ref, dga_ref, dgb_ref, dsa_ref, dob_ref, gg2_ref = outs_

        @pl.when(pl.program_id(0) == 0)
        def _():
            gg2_ref[...] = jnp.zeros((1, D), F32)
            if side:
                side.start(s_in, s_out, *s_scr)

        dmx, dg2 = _rms_bwd(dh_ref[...], mx_ref[...].astype(F32), g2_ref[...])
        gg2_ref[...] += dg2
        dmxb = dmx.astype(BF)
        dmx_ref[...] = dmxb
        dmp = _dot_nt(dmxb, wm_ref[...]).astype(BF)

        def gate(g_ref, y_ref, dy_ref, dg_ref, w_ref, dz_ref):
            sg = jax.nn.sigmoid(g_ref[...])
            dyb_ = dmp * sg
            dy_ref[...] = dyb_
            dg_ref[...] = dyb_ * y_ref[...] * (1.0 - sg)
            dz_ref[...] = _dot_nt(dyb_, w_ref[...]).astype(BF)

        gate(ga_ref, ya_ref, dya_ref, dga_ref, ao_ref, dsa_ref)
        gate(gb_ref, yb_ref, dyb_ref, dgb_ref, bo_ref, dob_ref)
        if side:
            @pl.when(pl.program_id(0) == nt - 1)
            def _():
                side.finish(s_in, s_out, *s_scr)

    row = lambda w: pl.BlockSpec((t, w), lambda i: (i, 0))
    si_specs, so_specs, so_shapes, s_scratch, s_ins = _side_specs(side)
    return _pc(
        body, name="mix_bwd", grid=(nt,),
        in_specs=[row(D), row(D), pl.BlockSpec((t, D), lambda i: (i, 4)), pl.BlockSpec((t, D), lambda i: (i, 5)),
                  row(D), row(D), _const((1, D)), _const((D, D)), _const((A_W, D)), _const((D, D))] + si_specs,
        out_specs=[row(D)] * 5 + [row(A_W), row(D), _acc((1, D))] + so_specs,
        out_shape=[SDS((s, D), BF)] * 5 + [SDS((s, A_W), BF), SDS((s, D), BF), SDS((1, D), F32)] + so_shapes,
        scratch_shapes=s_scratch,
        compiler_params=_cp(1))(dh1, mx, z, z, ya, yb, g2, w_mix, a_out, b_out, *s_ins)


def sgu_bwd(z, dsa, ln_g, ln_b, w_cat, w_cat_t, bias_full, bdm, t):
    s = z.shape[0]
    nt = s // t
    nch = t // A_C

    def body(u_ref, v_ref, dsa_ref, g_ref, b_ref, wc_ref, wct_ref, bias_ref, bdm_ref,
             duv_ref, glg_ref, glb_ref, gws_ref, gbs_ref, ds_acc):
        i = pl.program_id(0)

        @pl.when(i == 0)
        def _():
            glg_ref[...] = jnp.zeros((1, A_W), F32)
            glb_ref[...] = jnp.zeros((1, A_W), F32)
            gws_ref[...] = jnp.zeros((A_C, A_G * A_C), F32)
            ds_acc[...] = jnp.zeros((A_C, A_W), F32)

        lng, bdm_ = g_ref[...], bdm_ref[...]
        rec = [_sgu_recompute(v_ref[pl.ds(ci * A_C, A_C), :], lng, b_ref[...]) for ci in range(nch)]
        spread_vn = _sgu_spread([r_[3] for r_ in rec], bdm_)
        mixed = _dot(_sgu_weights(wc_ref, False), spread_vn)
        dsas, dss, dgus = [], [], []
        for ci in range(nch):
            rows = pl.ds(ci * A_C, A_C)
            gu, dgu = _gelu_and_grad(u_ref[rows, :])
            dsa_ = dsa_ref[rows, :]
            ds = dsa_ * gu
            ds_acc[...] += ds.astype(F32)
            dsas.append(dsa_)
            dgus.append(dgu)
            dss.append(ds)
        r = lax.broadcasted_iota(jnp.int32, (A_C, A_G * A_C), 0)
        c = lax.broadcasted_iota(jnp.int32, (A_C, A_G * A_C), 1) & (A_C - 1)
        gws_ref[...] += jnp.where(c <= r, _dot_nt(jnp.concatenate(dss, axis=1), spread_vn), 0.0)
        dvns = _dot(_sgu_weights(wct_ref, True), _sgu_spread(dss, bdm_))
        for ci in range(nch):
            rows = pl.ds(ci * A_C, A_C)
            dgv, rstd, xhat, _ = rec[ci]
            dvn = dvns[:, ci * A_W:(ci + 1) * A_W]
            glb_ref[...] += jnp.sum(dvn, axis=0, keepdims=True)
            glg_ref[...] += jnp.sum(dvn * xhat, axis=0, keepdims=True)
            dxh = dvn * lng
            dgv_ = rstd * (dxh - jnp.mean(dxh, axis=-1, keepdims=True)
                           - xhat * jnp.mean(dxh * xhat, axis=-1, keepdims=True))
            s_ = mixed[:, ci * A_W:(ci + 1) * A_W] + bias_ref[...]
            duv_ref[rows, :A_W] = dsas[ci] * dgus[ci] * s_.astype(BF)
            duv_ref[rows, A_W:] = (dgv_ * dgv).astype(BF)

        @pl.when(i == nt - 1)
        def _():
            acc = ds_acc[...]
            for g in range(A_G):
                gbs_ref[:, g:g + 1] = jnp.sum(acc[:, g * A_GD:(g + 1) * A_GD], axis=1, keepdims=True)

    return _pc(
        body, name="sgu_bwd", grid=(nt,),
        in_specs=[pl.BlockSpec((t, A_W), lambda i: (i, 0)), pl.BlockSpec((t, A_W), lambda i: (i, 1)),
                  pl.BlockSpec((t, A_W), lambda i: (i, 0)),
                  _const((1, A_W)), _const((1, A_W)), _const((A_C, A_G * A_C)), _const((A_C, A_G * A_C)),
                  _const((A_C, A_W)), _const((A_G * A_C, A_W))],
        out_specs=[pl.BlockSpec((t, D), lambda i: (i, 0)), _acc((1, A_W)), _acc((1, A_W)),
                   _acc((A_C, A_G * A_C)), _acc((A_C, A_G))],
        out_shape=[SDS((s, D), BF), SDS((1, A_W), F32), SDS((1, A_W), F32), SDS((A_C, A_G * A_C), F32),
                   SDS((A_C, A_G), F32)],
        scratch_shapes=[pltpu.VMEM((A_C, A_W), F32)],
        compiler_params=_cp(1))(z, z, dsa, ln_g, ln_b, w_cat, w_cat_t, bias_full, bdm)


def gla_bwd(z, qk32, zl, o, dob, states, wgk, bias, wn, ltri, ltri_t, t, side=None):
    s = z.shape[0]
    nt = s // t
    nc = t // B_C
    hpb = GLA_HPB
    assert hpb == B_H
    kw, vw = hpb * B_HK, hpb * B_HV

    def body(*refs):
        ins_, s_in, outs_, s_out, scr_, s_scr = _split_side(refs, 12, 7, 5, side)
        qk_ref, v_ref, og_ref, lr_ref, o_ref, dob_ref, st_ref, wgk_ref, bias_ref, wn_ref, l_ref, lt_ref = ins_
        dqk_ref, dv_ref, dog_ref, dpre_ref, dlr_ref, gbias_ref, gwn_ref = outs_
        dst_sc, dv_sc, dqd_sc, dkt_sc, ddec_sc = scr_
        i = pl.program_id(0)
        g = pl.program_id(1)

        @pl.when((i == 0) & (g == 0))
        def _():
            gbias_ref[...] = jnp.zeros((B_H, 1, B_HK), F32)
            gwn_ref[...] = jnp.zeros((1, B_HV), F32)
            if side:
                side.start(s_in, s_out, *s_scr)

        @pl.when(i == 0)
        def _():
            dst_sc[...] = jnp.zeros((B_H, B_HV, B_HK), F32)

        lr, l, lt = lr_ref[...], l_ref[...], lt_ref[...]
        keep, keep_t = l > 0, lt > 0
        wn_ = wn_ref[...]
        last = lax.broadcasted_iota(jnp.int32, (nc, B_C, B_HK), 1) == B_C - 1
        for hh in range(hpb):
            h = hh
            cv, ck = slice(hh * B_HV, (hh + 1) * B_HV), slice(hh * B_HK, (hh + 1) * B_HK)
            pre, b, bl, eb, enb, etb, qd, ki, kt = _gla_decays(qk_ref[:, cv], lr, wgk_ref[:, ck], bias_ref[:, ck], l, t)
            qdb, kib, ktb = qd.astype(BF), ki.astype(BF), kt.astype(BF)
            vb = v_ref[:, cv]
            o_ = o_ref[:, cv].astype(F32)
            og = og_ref[:, cv].astype(F32)
            sog = jax.nn.sigmoid(og)
            dob_ = dob_ref[:, cv].astype(F32)
            don = dob_ * og * sog
            do, dwn = _rms_bwd(don, o_, wn_)
            gwn_ref[...] += dwn
            dog_ref[:, cv] = (dob_ * _rms(o_, wn_) * sog * (1.0 + og * (1.0 - sog))).astype(BF)
            dob16 = do.astype(BF)
            sc_t = jnp.where(keep_t, _dot_nt(kib, qdb), 0.0).astype(BF)
            dsc = jnp.where(keep, _dot_nt(dob16, vb), 0.0).astype(BF)
            dsc_t = jnp.where(keep_t, _dot_nt(vb, dob16), 0.0).astype(BF)
            dv_sc[hh] = _dot(sc_t, dob16)
            dqd_sc[hh] = _dot(dsc, kib)
            dki = _dot(dsc_t, qdb)
            for n in reversed(range(nc)):
                rows = slice(n * B_C, (n + 1) * B_C)
                dst = dst_sc[h]
                dstb = dst.astype(BF)
                stp = st_ref[n, hh]
                dv_sc[hh, rows, :] += _dot_nt(ktb[rows], dstb)
                dkt_sc[hh, rows, :] = _dot(vb[rows], dstb)
                dqd_sc[hh, rows, :] += _dot(dob16[rows], stp)
                dec = jnp.exp(bl[n * B_C:n * B_C + 1, :])
                ddec_sc[hh, n] = jnp.sum(dst * stp.astype(F32), axis=0, keepdims=True) * dec
                dst_sc[h] = dst * dec + _dot_tn(dob16[rows], qdb[rows])
            dqd, dkt = dqd_sc[hh], dkt_sc[hh]
            dv_ref[:, cv] = dv_sc[hh].astype(BF)
            dqk_ref[:, hh * B_HV:hh * B_HV + B_HK] = (dqd * eb * (B_HK ** -0.5)).astype(BF)
            dqk_ref[:, hh * B_HV + B_HK:(hh + 1) * B_HV] = (dki * enb + dkt * etb).astype(BF)
            dktkt = dkt * kt
            db3 = (dqd * qd - dki * ki - dktkt).reshape(nc, B_C, B_HK)
            dbl = jnp.sum(dktkt.reshape(nc, B_C, B_HK), axis=1, keepdims=True) + ddec_sc[hh]
            db = (db3 + jnp.where(last, dbl, 0.0)).reshape(t, B_HK)
            dla = _ldot3(lt, db)
            dpre = dla * (1.0 / 16.0) * (1.0 - jax.nn.sigmoid(pre))
            dpreb = dpre.astype(BF)
            dpre_ref[:, ck] = dpreb
            gbias_ref[h] += jnp.sum(dpre, axis=0, keepdims=True)
            dlr_h = _dot_nt(dpreb, wgk_ref[:, ck])
            dlr = dlr_h if hh == 0 else dlr + dlr_h
        dlr_ref[...] = dlr.astype(BF)
        if side:
            @pl.when((i == nt - 1) & (g == B_H // hpb - 1))
            def _():
                side.finish(s_in, s_out, *s_scr)

    rv = lambda i: nt - 1 - i
    si_specs, so_specs, so_shapes, s_scratch, s_ins = _side_specs(side)
    vo, go = 2048 // vw, 3072 // vw
    tile = lambda off: pl.BlockSpec((t, vw), lambda i, g: (rv(i), off + g))
    return _pc(
        body, name="gla_bwd", grid=(nt, B_H // hpb),
        in_specs=[tile(0), tile(vo), tile(go), pl.BlockSpec((t, LANE), lambda i, g: (rv(i), 0)), tile(0), tile(0),
                  pl.BlockSpec((nc, hpb, B_HV, B_HK), lambda i, g: (rv(i), g, 0, 0)),
                  pl.BlockSpec((LANE, kw), lambda i, g: (0, g)), pl.BlockSpec((1, kw), lambda i, g: (0, g)),
                  _const((1, B_HV)), _const((t, t)), _const((t, t))] + si_specs,
        out_specs=[tile(0), tile(0), tile(0), pl.BlockSpec((t, kw), lambda i, g: (rv(i), g)),
                   pl.BlockSpec((t, LANE), lambda i, g: (rv(i), 0)), _acc((B_H, 1, B_HK)), _acc((1, B_HV))] + so_specs,
        out_shape=[SDS((s, D), BF), SDS((s, D), BF), SDS((s, D), BF), SDS((s, B_H * B_HK), BF), SDS((s, LANE), BF),
                   SDS((B_H, 1, B_HK), F32), SDS((1, B_HV), F32)] + so_shapes,
        scratch_shapes=[pltpu.VMEM((B_H, B_HV, B_HK), F32), pltpu.VMEM((hpb, t, B_HV), F32),
                        pltpu.VMEM((hpb, t, B_HK), F32), pltpu.VMEM((hpb, t, B_HK), F32),
                        pltpu.VMEM((hpb, nc, 1, B_HK), F32)] + s_scratch,
        compiler_params=_cp(2))(qk32, z, z, zl, o, dob, states, wgk, bias, wn, ltri, ltri_t, *s_ins)


def mm_tn(a, b, name, tk=2048):
    s, m = a.shape
    n = b.shape[1]
    bn = next(c for c in (1024, 1408, 512, 256, 128) if n % c == 0 and m * c * 4 <= 6 * 1024 * 1024)
    tk = min(tk, s)
    nk = s // tk

    def body(a_ref, b_ref, o_ref, acc):
        k = pl.program_id(1)

        @pl.when(k == 0)
        def _():
            acc[...] = jnp.zeros((m, bn), F32)

        acc[...] += _dot_tn(a_ref[...].astype(BF), b_ref[...])

        @pl.when(k == nk - 1)
        def _():
            o_ref[...] = acc[...].astype(BF)

    return _pc(
        body, name=name, grid=(n // bn, nk),
        in_specs=[pl.BlockSpec((tk, m), lambda j, k: (k, 0)), pl.BlockSpec((tk, bn), lambda j, k: (k, j))],
        out_specs=pl.BlockSpec((m, bn), lambda j, k: (0, j)),
        out_shape=SDS((m, n), BF), scratch_shapes=[pltpu.VMEM((m, bn), F32)], compiler_params=_cp(2))(a, b)


def _adamw(w, g, m, v):
    m = ADAM_B1 * m + (1.0 - ADAM_B1) * g
    v = ADAM_B2 * v + (1.0 - ADAM_B2) * (g * g)
    m_hat = m / (1.0 - ADAM_B1 ** ADAM_STEP)
    v_hat = v / (1.0 - ADAM_B2 ** ADAM_STEP)
    return -ADAM_LR * (m_hat / (jnp.sqrt(v_hat) + ADAM_EPS) + ADAM_WD * w), m, v


def _half_rows(rows):
    rh = rows // 2
    return rh, max(b for b in range(16, 257, 16) if rh % b == 0)


def _pc_sp(body, grid, in_specs, out_specs, out_shape, name):
    gs = pltpu.PrefetchScalarGridSpec(num_scalar_prefetch=1, grid=grid, in_specs=in_specs, out_specs=out_specs)
    return _pc(body, grid_spec=gs, out_shape=out_shape, name=name, compiler_params=_cp(len(grid)))


def adamw_item(own, sib, w, m, v):
    rows, cols = w.shape
    rh, br = _half_rows(rows)
    nbk = rh // br

    def fn(ins, outs, b, sc_ref):
        own_ref, sib_ref, w_ref, m_ref, v_ref = ins
        g_ = jnp.where(b // nbk == sc_ref[0], own_ref[...], sib_ref[...])
        outs[0][...] = g_
        outs[1][...], outs[2][...], outs[3][...] = _adamw(w_ref[...], g_, m_ref[...], v_ref[...])

    blk = (br, cols)
    mine = lambda b, sc_: (jnp.clip(b - sc_[0] * nbk, 0, nbk - 1), 0)
    theirs = lambda b, sc_: (jnp.clip(b - (1 - sc_[0]) * nbk, 0, nbk - 1), 0)
    each = lambda b, sc_: (b, 0)
    return ([(own, blk, mine), (sib, blk, theirs), (w, blk, each), (m, blk, each), (v, blk, each)],
            [(SDS((rows, cols), F32), blk, each)] * 4, 2 * nbk, fn)


def adamw_cols(sc, own, sib, w, m, v, name, cb=256):
    rows, cols = w.shape
    nk = cols // 2 // cb

    def body(sc_ref, own_ref, sib_ref, w_ref, m_ref, v_ref, go_ref, d_ref, mo_ref, vo_ref):
        g_ = jnp.where(pl.program_id(0) == sc_ref[0], own_ref[...], sib_ref[...])
        go_ref[...] = g_
        d_ref[...], mo_ref[...], vo_ref[...] = _adamw(w_ref[...], g_, m_ref[...], v_ref[...])

    mine = pl.BlockSpec((rows, cb), lambda h, k, sc_: (0, jnp.clip(k + (h - sc_[0]) * nk, 0, nk - 1)))
    theirs = pl.BlockSpec((rows, cb), lambda h, k, sc_: (0, jnp.clip(k + (h - 1 + sc_[0]) * nk, 0, nk - 1)))
    blk = pl.BlockSpec((rows, cb), lambda h, k, sc_: (0, h * nk + k))
    return _pc_sp(body, (2, nk), [mine, theirs, blk, blk, blk], [blk] * 4, [SDS((rows, cols), F32)] * 4,
                  name)(sc, own, sib, w, m, v)


def adamw_small(g, w, m, v):
    def body(g_ref, w_ref, m_ref, v_ref, d_ref, mo_ref, vo_ref):
        d_ref[...], mo_ref[...], vo_ref[...] = _adamw(w_ref[...], g_ref[...], m_ref[...], v_ref[...])

    vm = pl.BlockSpec(memory_space=pltpu.VMEM)
    return _pc(body, name="adamw_small", in_specs=[vm] * 4, out_specs=[vm] * 3, out_shape=[SDS(g.shape, F32)] * 3,
               compiler_params=pltpu.CompilerParams(vmem_limit_bytes=VMEM_LIMIT))(g, w, m, v)


def _pos():
    return lax.axis_index("x"), lax.axis_index("y"), lax.axis_index("c")


def _other_chips(x, y):
    return [(1 - x, y), (x, 1 - y), (1 - x, 1 - y)]


_ANY = pl.BlockSpec(memory_space=pltpu.HBM)


class _Side:
    def __init__(self, ins, out_shapes, nsem, start, finish):
        self.ins, self.out_shapes, self.start, self.finish = list(ins), list(out_shapes), start, finish
        self.scratch = [pltpu.SemaphoreType.DMA((nsem,)), pltpu.SemaphoreType.DMA((nsem,))]
        self.n_in, self.n_out = len(self.ins), len(self.out_shapes)


def _run_side(side, name):
    def body(*refs):
        args_ = (refs[:side.n_in], refs[side.n_in:side.n_in + side.n_out], *refs[side.n_in + side.n_out:])
        side.start(*args_)
        side.finish(*args_)

    return _pc(body, name=name, in_specs=[_ANY] * side.n_in, out_specs=[_ANY] * side.n_out,
               out_shape=side.out_shapes, scratch_shapes=side.scratch)(*side.ins)


def _split_rows(shape):
    return (shape[0] // 2) % 16 == 0


def _core_halves(shape, c):
    if _split_rows(shape):
        h = shape[0] // 2
        return ((pl.ds(pl.multiple_of(c * h, 16), h), slice(None)),
                (pl.ds(pl.multiple_of((1 - c) * h, 16), h), slice(None)))
    h = shape[1] // 2
    assert h % LANE == 0
    return ((slice(None), pl.ds(pl.multiple_of(c * h, LANE), h)),
            (slice(None), pl.ds(pl.multiple_of((1 - c) * h, LANE), h)))


def gather_side(bigs, tinies):
    nb, nt_ = len(bigs), len(tinies)

    def plan(ins, outs, ssem, rsem):
        x, y, c = _pos()
        me = 2 * x + y
        chips = _other_chips(x, y)
        sibling = (x, y, 1 - c)

        def copy(k, src, dst, to):
            return pltpu.make_async_remote_copy(src_ref=src, dst_ref=dst, send_sem=ssem.at[k], recv_sem=rsem.at[k],
                                                device_id=to, device_id_type=MESH)

        sends, landed, passed_on, tiny_landed = [], [], [], []
        for w in range(nb):
            mine, theirs = _core_halves(bigs[w].shape, c)
            for j, (cx, cy) in enumerate(chips):
                sends.append(copy(6 * w + j, ins[w].at[mine], outs[w].at[(me,) + mine], (cx, cy, c)))
                blk = outs[w].at[(2 * cx + cy,) + mine]
                landed.append((copy(6 * w + j, blk, blk, (cx, cy, c)), copy(6 * w + 3 + j, blk, blk, sibling)))
                blk = outs[w].at[(2 * cx + cy,) + theirs]
                passed_on.append(copy(6 * w + 3 + j, blk, blk, sibling))
        for w in range(nt_):
            for j, (cx, cy) in enumerate(chips):
                k = 6 * nb + 3 * w + j
                sends.append(copy(k, ins[nb + w], outs[nb + w].at[me], (cx, cy, c)))
                blk = outs[nb + w].at[2 * cx + cy]
                tiny_landed.append(copy(k, blk, blk, (cx, cy, c)))
        return sends, landed, passed_on, tiny_landed

    def start(ins, outs, ssem, rsem):
        for cp in plan(ins, outs, ssem, rsem)[0]:
            cp.start()

    def finish(ins, outs, ssem, rsem):
        sends, landed, passed_on, tiny_landed = plan(ins, outs, ssem, rsem)
        for arrived, forward in landed:
            arrived.wait_recv()
            forward.start()
        for arrived in tiny_landed + passed_on:
            arrived.wait_recv()
        for cp in sends + [forward for _, forward in landed]:
            cp.wait_send()

    return _Side(list(bigs) + list(tinies), [SDS((4,) + a.shape, a.dtype) for a in list(bigs) + list(tinies)],
                 6 * nb + 3 * nt_, start, finish)


def _sibling_side(srcs, out_shapes, pick):
    def plan(in_refs, out_refs, ssem, rsem):
        x, y, c = _pos()
        return [pltpu.make_async_remote_copy(src_ref=pick(in_refs[w], srcs[w].shape, c), dst_ref=out_refs[w],
                                             send_sem=ssem.at[w], recv_sem=rsem.at[w], device_id=(x, y, 1 - c),
                                             device_id_type=MESH) for w in range(len(srcs))]

    def start(*refs):
        for cp in plan(*refs):
            cp.start()

    def finish(*refs):
        for cp in plan(*refs):
            cp.wait()

    return _Side(srcs, out_shapes, len(srcs), start, finish)


def swap_side(gs):
    def half_shape(g):
        l, r, cols = g.shape
        return (l, r // 2, cols) if _split_rows((r, cols)) else (l, r, cols // 2)

    return _sibling_side(gs, [SDS(half_shape(g), g.dtype) for g in gs],
                         lambda ref, shape, c: ref.at[(slice(None),) + _core_halves(shape[1:], c)[1]])


def join_side(halves):
    return _sibling_side(halves, [SDS(h.shape, h.dtype) for h in halves], lambda ref, shape, c: ref)


def _both_sides(a, b):
    def split(ins, outs, *scr):
        return ((ins[:a.n_in], outs[:a.n_out], *scr[:2]), (ins[a.n_in:], outs[a.n_out:], *scr[2:]))

    def start(*refs):
        ra, rb = split(*refs)
        a.start(*ra)
        b.start(*rb)

    def finish(*refs):
        ra, rb = split(*refs)
        a.finish(*ra)
        b.finish(*rb)

    side = _Side(a.ins + b.ins, a.out_shapes + b.out_shapes, 1, start, finish)
    side.scratch = a.scratch + b.scratch
    return side


COL_BLOCK = 256


def _blockwise(sc, items, name):
    in_specs, out_specs, out_shapes, operands, spans = [], [], [], [], []
    start = 0
    for ins, outs, nb, _ in items:
        def spec(blk, idx, s0=start, nb=nb):
            return pl.BlockSpec(blk, lambda i, sc_: idx(jnp.clip(i - s0, 0, nb - 1), sc_))

        in_specs += [spec(blk, idx) for _, blk, idx in ins]
        out_specs += [spec(blk, idx) for _, blk, idx in outs]
        operands += [a for a, _, _ in ins]
        out_shapes += [s_ for s_, _, _ in outs]
        spans.append((start, start + nb))
        start += nb
    n_in = len(operands)

    def body(sc_ref, *refs):
        i = pl.program_id(0)
        at_in, at_out = 0, n_in
        for (ins, outs, _, fn), (lo, hi) in zip(items, spans):
            mine_in, mine_out = refs[at_in:at_in + len(ins)], refs[at_out:at_out + len(outs)]
            at_in, at_out = at_in + len(ins), at_out + len(outs)

            @pl.when((i >= lo) & (i < hi))
            def _(fn=fn, mine_in=mine_in, mine_out=mine_out, lo=lo):
                fn(mine_in, mine_out, i - lo, sc_ref)

    return _pc_sp(body, (start,), in_specs, out_specs, out_shapes, name)(sc, *operands)


def add_half_item(g, sib):
    l, r, cols = g.shape

    def fn(ins, outs, b, sc_ref):
        outs[0][...] = (ins[0][...].astype(F32) + ins[1][...].astype(F32)).astype(BF)

    if _split_rows((r, cols)):
        rh, br = _half_rows(r)
        nbk = rh // br
        blk = (1, br, cols)
        there = lambda b, sc_: (b // nbk, b % nbk, 0)
        return ([(g, blk, lambda b, sc_: (b // nbk, sc_[0] * nbk + b % nbk, 0)), (sib, blk, there)],
                [(SDS((l, rh, cols), BF), blk, there)], l * nbk, fn)
    nbk = cols // 2 // COL_BLOCK
    blk = (1, r, COL_BLOCK)
    there = lambda b, sc_: (b // nbk, 0, b % nbk)
    return ([(g, blk, lambda b, sc_: (b // nbk, 0, sc_[0] * nbk + b % nbk)), (sib, blk, there)],
            [(SDS((l, r, cols // 2), BF), blk, there)], l * nbk, fn)


def exchange_side(ps):
    n_ = len(ps)

    def width(p_):
        return p_.shape[2] if p_.shape[0] == 4 else p_.shape[2] // 4

    def plan(p_refs, got_refs, ssem, rsem):
        x, y, c = _pos()
        cps = []
        for w in range(n_):
            wd = width(ps[w])
            for j, (cx, cy) in enumerate(_other_chips(x, y)):
                to = 2 * cx + cy
                src = p_refs[w].at[to] if ps[w].shape[0] == 4 else p_refs[w].at[0, :, pl.ds(pl.multiple_of(to * wd, LANE), wd)]
                cps.append(pltpu.make_async_remote_copy(
                    src_ref=src, dst_ref=got_refs[w].at[j], send_sem=ssem.at[3 * w + j], recv_sem=rsem.at[3 * w + j],
                    device_id=(cx, cy, c), device_id_type=MESH))
        return cps

    def start(*refs):
        for cp in plan(*refs):
            cp.start()

    def finish(*refs):
        for cp in plan(*refs):
            cp.wait()

    return _Side(ps, [SDS((3, p_.shape[1], width(p_)), p_.dtype) for p_ in ps], 3 * n_, start, finish)


def sum4_item(p, got):
    _, rh, wd = got.shape

    def fn(ins, outs, b, sc_ref):
        p_ref, g_ref = ins
        outs[0][...] = ((p_ref[0].astype(F32) + g_ref[0].astype(F32))
                        + (g_ref[1].astype(F32) + g_ref[2].astype(F32)))

    if rh % 16:
        assert p.shape[0] == 4
        return ([(p, (1, rh, COL_BLOCK), lambda b, sc_: (sc_[1], 0, b)),
                 (got, (3, rh, COL_BLOCK), lambda b, sc_: (0, 0, b))],
                [(SDS((rh, wd), F32), (rh, COL_BLOCK), lambda b, sc_: (0, b))], wd // COL_BLOCK, fn)
    _, br = _half_rows(2 * rh)
    own = (lambda b, sc_: (sc_[1], b, 0)) if p.shape[0] == 4 else (lambda b, sc_: (0, b, sc_[1]))
    return ([(p, (1, br, wd), own), (got, (3, br, wd), lambda b, sc_: (0, b, 0))],
            [(SDS((rh, wd), F32), (br, wd), lambda b, sc_: (b, 0))], rh // br, fn)


def allreduce_small(g):
    rows = g.shape[0]
    rh = rows // 2

    def body(g_ref, out_ref, sib_buf, chip_buf, sum_sc, ssem, rsem):
        x, y, c = _pos()
        me = 2 * x + y
        sibling = (x, y, 1 - c)
        mine = pl.ds(pl.multiple_of(c * rh, 8), rh)

        def copy(k, src, dst, to):
            return pltpu.make_async_remote_copy(src_ref=src, dst_ref=dst, send_sem=ssem.at[k], recv_sem=rsem.at[k],
                                                device_id=to, device_id_type=MESH)

        cp = copy(0, g_ref, sib_buf, sibling)
        cp.start()
        cp.wait()
        sum_sc[...] = g_ref[...] + sib_buf[...]
        chips = _other_chips(x, y)
        cps = [copy(1 + j, sum_sc.at[mine], chip_buf.at[me], (cx, cy, c)) for j, (cx, cy) in enumerate(chips)]
        for cp in cps:
            cp.start()
        chip_buf[me] = sum_sc[mine, :]
        for j, (cx, cy) in enumerate(chips):
            copy(1 + j, sum_sc.at[mine], chip_buf.at[2 * cx + cy], (cx, cy, c)).wait_recv()
        for cp in cps:
            cp.wait_send()
        out_ref[mine, :] = (chip_buf[0] + chip_buf[1]) + (chip_buf[2] + chip_buf[3])
        cp = copy(4, out_ref.at[mine], out_ref.at[mine], sibling)
        cp.start()
        cp.wait()

    vm = pl.BlockSpec(memory_space=pltpu.VMEM)
    return _pc(body, name="allreduce_small", in_specs=[vm], out_specs=vm, out_shape=SDS((rows, LANE), F32),
               scratch_shapes=[pltpu.VMEM((rows, LANE), F32), pltpu.VMEM((4, rh, LANE), F32), pltpu.VMEM((rows, LANE), F32),
                               pltpu.SemaphoreType.DMA((5,)), pltpu.SemaphoreType.DMA((5,))],
               compiler_params=pltpu.CompilerParams(vmem_limit_bytes=VMEM_LIMIT))(g)


def _pack_small(entries, get):
    flat = jnp.concatenate([get(n).reshape(-1).astype(F32) for n, _ in entries])
    rows = -(-flat.shape[0] // (8 * LANE)) * 8
    return jnp.pad(flat, (0, rows * LANE - flat.shape[0])).reshape(rows, LANE)


def _unpack_small(entries, packed):
    out, off = {}, 0
    flat = packed.reshape(-1)
    for name, n in entries:
        out[name] = flat[off:off + n]
        off += n
    return out


def _cols_full(blk):
    return blk.transpose(1, 0, 2).reshape(blk.shape[1], 4 * blk.shape[2])


def kernel(x, p, pre_mix_norm, w_in, a_ln_g, a_ln_b, a_spatial_w, a_spatial_b, a_out, b_gk, b_gk_bias, b_out_norm, b_out, w_mix_out, post_mix_norm, pre_ffn_norm, w_up, conv_w, conv_b, w_down, post_ffn_norm, w_ple, w_ple_gate, post_ple_norm, loss_target, m_pre_mix_norm, m_w_in, m_a_ln_g, m_a_ln_b, m_a_spatial_w, m_a_spatial_b, m_a_out, m_b_gk, m_b_gk_bias, m_b_out_norm, m_b_out, m_w_mix_out, m_post_mix_norm, m_pre_ffn_norm, m_w_up, m_conv_w, m_conv_b, m_w_down, m_post_ffn_norm, m_w_ple, m_w_ple_gate, m_post_ple_norm, v_pre_mix_norm, v_w_in, v_a_ln_g, v_a_ln_b, v_a_spatial_w, v_a_spatial_b, v_a_out, v_b_gk, v_b_gk_bias, v_b_out_norm, v_b_out, v_w_mix_out, v_post_mix_norm, v_pre_ffn_norm, v_w_up, v_conv_w, v_conv_b, v_w_down, v_post_ffn_norm, v_w_ple, v_w_ple_gate, v_post_ple_norm):
    args = dict(locals())
    order = ['pre_mix_norm', 'w_in', 'a_ln_g', 'a_ln_b', 'a_spatial_w', 'a_spatial_b', 'a_out', 'b_gk', 'b_gk_bias',
             'b_out_norm', 'b_out', 'w_mix_out', 'post_mix_norm', 'pre_ffn_norm', 'w_up', 'conv_w', 'conv_b', 'w_down',
             'post_ffn_norm', 'w_ple', 'w_ple_gate', 'post_ple_norm']
    assert sorted(BIG + TINY + tuple(n for n, _ in SMALL)) == sorted(order)
    s = x.shape[1]
    xs = x.reshape(s, D)
    ps = p.reshape(s, PLE)
    tgt = loss_target.reshape(s, D)
    t_big = min(1024, s)
    t_mid = min(512, s)
    t_small = min(256, s)
    t_gla = min(256, s)
    mx_, my_, mc_ = _pos()
    me = 2 * mx_ + my_
    sc = jnp.stack([mc_, me]).astype(jnp.int32)
    shard = lambda n: args[n].reshape(args[n].shape[1:])

    mine = {n: shard(n).astype(BF) for n in BIG}
    mine["w_in"] = shard("w_in").T.astype(BF)
    mine.update({n: shard(n) for n in TINY})
    fill = lambda names, gots: {n: lax.dynamic_update_slice(got, mine[n][None], (me, 0, 0)) for n, got in zip(names, gots)}
    first = ("w_in",) + TINY
    full = fill(first, _run_side(gather_side([mine["w_in"]], [mine[n] for n in TINY]), "gather_first"))
    wi = full["w_in"].reshape(4 * 1540, D)
    seg = lambda a, b: wi[a:b]
    qk = [seg(1024 + h * B_HK, 1024 + (h + 1) * B_HK) for h in range(B_H)]
    kk = [seg(1536 + h * B_HK, 1536 + (h + 1) * B_HK) for h in range(B_H)]
    w_z = jnp.concatenate([seg(0, 1024)] + [m_ for h in range(B_H) for m_ in (qk[h], kk[h])]
                          + [seg(2048, 4096), seg(4112, 6160), seg(4096, 4112), jnp.zeros((LANE - B_RANK, D), BF)], axis=0)
    wgk = jnp.pad(_cols_full(full["b_gk"]).astype(BF), ((0, LANE - B_RANK), (0, 0)))
    w_conv = _cols_full(full["conv_w"])
    g1, g2, g3 = pre_mix_norm.reshape(1, D), post_mix_norm.reshape(1, D), pre_ffn_norm.reshape(1, D)
    g4, g5 = post_ffn_norm.reshape(1, D), post_ple_norm.reshape(1, D)
    ln_g, ln_b = a_ln_g.reshape(1, A_W), a_ln_b.reshape(1, A_W)
    w_s = a_spatial_w.reshape(A_G, A_C, A_C)
    w_cat = w_s.transpose(1, 0, 2).reshape(A_C, A_G * A_C)
    w_cat_t = w_s.transpose(2, 0, 1).reshape(A_C, A_G * A_C)
    bias_full = jnp.repeat(a_spatial_b.reshape(A_G, A_C).T, A_GD, axis=1)
    bdm = (jnp.arange(A_G * A_C)[:, None] // A_C == jnp.arange(A_W)[None, :] // A_GD).astype(BF)
    gk_bias = b_gk_bias.reshape(1, B_H * B_HK)
    wn = b_out_norm.reshape(1, B_HV)
    cb = conv_b.reshape(1, 2 * D_FF)
    idx = jnp.arange(t_gla)
    ltri = ((idx[:, None] // B_C == idx[None, :] // B_C) & (idx[None, :] <= idx[:, None])).astype(BF)

    a, z, qk32, zl, *gots = norm_matmul(xs, g1, w_z, D, t_big, "in_proj", nblk=6, f32_blk=1, tail_blk=48,
                                        side=gather_side([mine[n] for n in BIG[1:]], []))
    full.update(fill(BIG[1:], gots))
    w_aout, w_ple_f = _cols_full(full["a_out"]), _cols_full(full["w_ple"])
    w_bout, w_mix, w_pg = (full[n].reshape(D, D) for n in ("b_out", "w_mix_out", "w_ple_gate"))
    w_dn, w_up3 = full["w_down"].reshape(D_FF, D), full["w_up"]
    sa = sgu_fwd(z, ln_g, ln_b, w_cat, bias_full, bdm, t_mid)
    ob, o, states = gla_fwd(z, qk32, zl, wgk, gk_bias, wn, ltri, t_gla)
    ya, yb, mp, mx, h1 = mix_fwd(sa, ob, z, xs, w_aout, w_bout, w_mix, g2, t_mid)
    c, up_g, up_v, cg, cv, ff = ffn_up_fwd(h1, g3, w_up3, w_conv, cb, t_mid)
    f, h2, pg, pe, dy, loss = out_fwd(ff, h1, ps, tgt, w_dn, w_pg, w_ple_f, g4, g5, t_mid)

    dh2, dpe, dpg, df, dff, gg5, gg4 = out_bwd(dy, pg, pe, f, g5, g4, w_pg, w_dn, t_mid)
    dup, gcw, gcb, dh1, gg3 = ffn_up_bwd(up_g, up_v, cg, cv, dff, w_conv, _cols_full(w_up3), h1, g3, dh2, t_small)
    grads = {
        "w_up": mm_tn(c, dup, "dw_up")[None],
        "w_down": mm_tn(ff, df, "dw_down").reshape(4, D_FF // 4, D),
        "w_ple": mm_tn(ps, dpe, "dw_ple")[None],
        "w_ple_gate": mm_tn(h2, dpg, "dw_ple_gate").reshape(4, D // 4, D),
    }
    ffn_side = ("w_up", "w_down", "w_ple", "w_ple_gate")
    dmx, dya, dyb, dga, dgb, dsa, dob, gg2, *sibs = mix_bwd(dh1, mx, z, ya, yb, g2, w_mix, w_aout, w_bout, t_mid,
                                                            side=swap_side([grads[n] for n in ffn_side]))
    sib = dict(zip(ffn_side, sibs))
    duv, g_lng, g_lnb, g_wcat, g_bst = sgu_bwd(z, dsa, ln_g, ln_b, w_cat, w_cat_t, bias_full, bdm, t_mid)
    g_ws = g_wcat.reshape(A_C, A_G, A_C).transpose(1, 0, 2)
    grads.update({
        "a_out": mm_tn(sa, dya, "dw_a_out")[None],
        "b_out": mm_tn(ob, dyb, "dw_b_out").reshape(4, D // 4, D),
        "w_mix_out": mm_tn(mp, dmx, "dw_mix").reshape(4, D // 4, D),
    })

    def swap(names):
        sib.update(zip(names, _run_side(swap_side([grads[n] for n in names]), "swap_halves_" + names[0])))

    swap(("a_out", "b_out", "w_mix_out"))
    parts = dict(zip(BIG[1:], _blockwise(sc, [add_half_item(grads[n], sib[n]) for n in BIG[1:]], "partials")))
    dqk, dvb, dog, dpre, dlr, g_gkb, g_wn, *gots = gla_bwd(
        z, qk32, zl, o, dob, states, wgk, gk_bias, wn, ltri, ltri.T, t_gla,
        side=exchange_side([parts[n] for n in BIG[1:]]))
    reds = dict(zip(BIG[1:], _blockwise(sc, [sum4_item(parts[n], got_) for n, got_ in zip(BIG[1:], gots)], "sums")))
    segs = [duv, dqk, dvb, dog, dga, dgb, dlr]

    gz = [mm_tn(sg_, a, "dw_in_%d" % k) for k, sg_ in enumerate(segs)]
    gq = [gz[1][h * 256:h * 256 + B_HK] for h in range(B_H)]
    gk = [gz[1][h * 256 + B_HK:(h + 1) * 256] for h in range(B_H)]
    g_in = jnp.concatenate([gz[0]] + gq + gk + [gz[2], gz[3], gz[6][:B_RANK], gz[4], gz[5]], axis=0)
    grads["w_in"] = g_in.reshape(4, 1540, D)
    swap(("w_in",))
    parts["w_in"], = _blockwise(sc, [add_half_item(grads["w_in"], sib["w_in"])], "partial_w_in")
    dx, gg1, got_in, *sib_reds = nt_normbwd(
        segs, w_z, xs, g1, dh1, t_mid, "in_bwd",
        side=_both_sides(exchange_side([parts["w_in"]]), join_side([reds[n] for n in BIG[1:]])))
    sib_red = dict(zip(BIG[1:], sib_reds))

    reds["w_in"], = _blockwise(sc, [sum4_item(parts["w_in"], got_in)], "sum_w_in")
    sib_red["w_in"], = _run_side(join_side([reds["w_in"]]), "join_w_in")
    outs = {}
    for n in BIG[1:]:
        res = _blockwise(sc, [adamw_item(reds[n], sib_red[n], shard(n), shard("m_" + n), shard("v_" + n))],
                         "adamw_" + n)
        outs[n] = [r_.reshape(args[n].shape) for r_ in res]
    res = adamw_cols(sc, reds["w_in"], sib_red["w_in"], shard("w_in").T, shard("m_w_in").T, shard("v_w_in").T,
                     "adamw_w_in")
    outs["w_in"] = [r_.T.reshape(w_in.shape) for r_ in res]

    small_g = {
        "pre_mix_norm": gg1, "a_ln_g": g_lng, "a_ln_b": g_lnb, "a_spatial_w": g_ws, "a_spatial_b": g_bst.T,
        "b_gk_bias": g_gkb, "b_out_norm": g_wn, "post_mix_norm": gg2, "pre_ffn_norm": gg3,
        "conv_b": gcb, "post_ffn_norm": gg4, "post_ple_norm": gg5,
        "b_gk": mm_tn(zl, dpre, "dw_gk")[:B_RANK], "conv_w": gcw,
        "loss": loss,
    }
    red_entries = SMALL + (("b_gk", B_RANK * 512), ("conv_w", 3 * 2 * D_FF), ("loss", 1))
    g_fin = _unpack_small(red_entries, allreduce_small(_pack_small(red_entries, lambda n: small_g[n])))
    g_fin["b_gk"] = lax.dynamic_slice(g_fin["b_gk"].reshape(B_RANK, 512), (0, me * B_HK), (B_RANK, B_HK))
    g_fin["conv_w"] = lax.dynamic_slice(g_fin["conv_w"].reshape(3, 2 * D_FF), (0, me * 1408), (3, 1408))
    upd_entries = SMALL + (("b_gk", B_RANK * B_HK), ("conv_w", 3 * 1408))
    res = adamw_small(*[_pack_small(upd_entries, get) for get in
                        (lambda n: g_fin[n], lambda n: args[n], lambda n: args["m_" + n], lambda n: args["v_" + n])])
    res = [_unpack_small(upd_entries, r_) for r_ in res]
    for n, _ in upd_entries:
        outs[n] = [r_[n].reshape(args[n].shape) for r_ in [g_fin] + res]

    return (g_fin["loss"].reshape(()), dx.reshape(x.shape), *[outs[n][0] for n in order], *[outs[n][1] for n in order],
            *[outs[n][2] for n in order], *[outs[n][3] for n in order])
```

```python
import math

import jax
import jax.numpy as jnp
from jax import lax
from jax.experimental import pallas as pl
from jax.experimental.pallas import tpu as pltpu

F32 = jnp.float32
BF = jnp.bfloat16
SDS = jax.ShapeDtypeStruct
MESH = pl.DeviceIdType.MESH

EPS = 1e-6
D = 1024
A_W = 512
A_G, A_C = 8, 128
A_GD = A_W // A_G
B_H, B_HK, B_HV = 4, 128, 256
B_C = 64
GLA_HPB = 4
B_RANK = 16
D_FF = 2816
PLE = 256
LANE = 128
VMEM_LIMIT = 60 * 1024 * 1024

ADAM_LR, ADAM_B1, ADAM_B2, ADAM_EPS, ADAM_WD, ADAM_STEP = 0.001, 0.9, 0.999, 1e-08, 0.01, 10

_GC = math.sqrt(2.0 / math.pi)
_GA = 0.044715

BIG = ("w_in", "a_out", "b_out", "w_mix_out", "w_up", "w_down", "w_ple", "w_ple_gate")
TINY = ("b_gk", "conv_w")
SMALL = (("pre_mix_norm", 1024), ("a_ln_g", 512), ("a_ln_b", 512), ("a_spatial_w", 131072),
         ("a_spatial_b", 1024), ("b_gk_bias", 512), ("b_out_norm", 256), ("post_mix_norm", 1024),
         ("pre_ffn_norm", 1024), ("conv_b", 5632), ("post_ffn_norm", 1024), ("post_ple_norm", 1024))


def _pc(body, **kw):
    return pl.pallas_call(body, **kw)


def _cp(n):
    return pltpu.CompilerParams(dimension_semantics=("arbitrary",) * n, vmem_limit_bytes=VMEM_LIMIT)


def _const(shape):
    nd = len(shape)
    return pl.BlockSpec(shape, lambda *_: (0,) * nd, pipeline_mode=pl.Buffered(1))


def _acc(shape):
    nd = len(shape)
    return pl.BlockSpec(shape, lambda *_: (0,) * nd)


def _dot(a, b):
    return jnp.dot(a, b, preferred_element_type=F32)


def _dot_nt(a, b):
    return lax.dot_general(a, b, (((1,), (1,)), ((), ())), preferred_element_type=F32)


def _dot_tn(a, b):
    return lax.dot_general(a, b, (((0,), (0,)), ((), ())), preferred_element_type=F32)


def _gelu(x):
    return 0.5 * x * (1.0 + jnp.tanh(_GC * (x + _GA * x * x * x)))


def _gelu_and_grad(x):
    x2 = x * x
    s = 0.5 * jnp.tanh((_GC * x) * (1.0 + _GA * x2)) + 0.5
    g = x * s
    return g, s + g * (1.0 - s) * ((6.0 * _GC * _GA) * x2 + 2.0 * _GC)


def _log_sigmoid(x):
    return jnp.minimum(x, 0.0) - jnp.log(1.0 + jnp.exp(-jnp.abs(x)))


def _rms(x, g):
    return x * lax.rsqrt(jnp.mean(x * x, axis=-1, keepdims=True) + EPS) * g


def _rms_bwd(dy, x, g):
    r = lax.rsqrt(jnp.mean(x * x, axis=-1, keepdims=True) + EPS)
    n = x * r
    dn = dy * g
    dx = r * (dn - n * jnp.mean(dn * n, axis=-1, keepdims=True))
    return dx, jnp.sum(dy * n, axis=0, keepdims=True)


def _ldot3(l, x):
    h = x.astype(BF)
    r = x - h.astype(F32)
    m = r.astype(BF)
    lo = (r - m.astype(F32)).astype(BF)
    return _dot(l, h) + _dot(l, m) + _dot(l, lo)


def _split_side(refs, n_in, n_out, n_scratch, side):
    si, so = (side.n_in, side.n_out) if side else (0, 0)
    cuts = [n_in, si, n_out, so, n_scratch]
    out, at = [], 0
    for c in cuts:
        out.append(refs[at:at + c])
        at += c
    return (*out, refs[at:])


def _side_specs(side):
    return ([_ANY] * side.n_in, [_ANY] * side.n_out, side.out_shapes, side.scratch, side.ins) if side else ([],) * 5


def norm_matmul(x, g, wt, bn, t, name, nblk, f32_blk, tail_blk, side=None):
    s, dm = x.shape
    w_spec = pl.BlockSpec((bn, dm), lambda i, j: (j, 0))
    nt = s // t

    def body(*refs):
        (x_ref, g_ref, w_ref, wl_ref), s_in, outs, s_out, (a_sc,), s_scr = _split_side(refs, 4, 4, 1, side)
        a_ref, z_ref, f32_ref, tail_ref = outs
        i, j = pl.program_id(0), pl.program_id(1)
        if side:
            @pl.when((i == 0) & (j == 0))
            def _():
                side.start(s_in, s_out, *s_scr)

        @pl.when(j == 0)
        def _():
            a = _rms(x_ref[...], g_ref[...]).astype(BF)
            a_sc[...] = a
            a_ref[...] = a
            tail_ref[...] = _dot_nt(a, wl_ref[...]).astype(BF)

        acc = _dot_nt(a_sc[...], w_ref[...])
        z_ref[...] = acc.astype(BF)

        @pl.when(j == f32_blk)
        def _():
            f32_ref[...] = acc
        if side:
            @pl.when((i == nt - 1) & (j == nblk - 1))
            def _():
                side.finish(s_in, s_out, *s_scr)

    si_specs, so_specs, so_shapes, s_scratch, s_ins = _side_specs(side)
    return _pc(
        body, name=name, grid=(nt, nblk),
        in_specs=[pl.BlockSpec((t, dm), lambda i, j: (i, 0)), _const((1, dm)), w_spec,
                  pl.BlockSpec((LANE, dm), lambda i, j: (tail_blk, 0), pipeline_mode=pl.Buffered(1))] + si_specs,
        out_specs=[pl.BlockSpec((t, dm), lambda i, j: (i, 0)), pl.BlockSpec((t, bn), lambda i, j: (i, j)),
                   pl.BlockSpec((t, bn), lambda i, j: (i, 0)), pl.BlockSpec((t, LANE), lambda i, j: (i, 0))] + so_specs,
        out_shape=[SDS((s, dm), BF), SDS((s, nblk * bn), BF), SDS((s, bn), F32), SDS((s, LANE), BF)] + so_shapes,
        scratch_shapes=[pltpu.VMEM((t, dm), BF)] + s_scratch, compiler_params=_cp(2))(x, g, wt, wt, *s_ins)


def _sgu_weights(wc_ref, transposed):
    r = lax.broadcasted_iota(jnp.int32, (A_C, A_G * A_C), 0)
    c = lax.broadcasted_iota(jnp.int32, (A_C, A_G * A_C), 1) & (A_C - 1)
    return jnp.where((r <= c) if transposed else (c <= r), wc_ref[...], 0.0).astype(BF)


def _sgu_spread(xs, bdm):
    return jnp.concatenate([jnp.tile(x, (A_G, 1)) * bdm for x in xs], axis=1)


def _sgu_recompute(v, lng, lnb):
    gv, dgv = _gelu_and_grad(v)
    gv = gv.astype(F32)
    mu = jnp.mean(gv, axis=-1, keepdims=True)
    xc = gv - mu
    rstd = lax.rsqrt(jnp.mean(xc * xc, axis=-1, keepdims=True) + EPS)
    xhat = xc * rstd
    return dgv, rstd, xhat, (xhat * lng + lnb).astype(BF)


def sgu_fwd(z, ln_g, ln_b, w_cat, bias_full, bdm, t):
    s = z.shape[0]
    nch = t // A_C

    def body(u_ref, v_ref, g_ref, b_ref, wc_ref, bias_ref, bdm_ref, sa_ref):
        vns = [_sgu_recompute(v_ref[pl.ds(ci * A_C, A_C), :], g_ref[...], b_ref[...])[3]
               for ci in range(nch)]
        mixed = _dot(_sgu_weights(wc_ref, False), _sgu_spread(vns, bdm_ref[...]))
        for ci in range(nch):
            rows = pl.ds(ci * A_C, A_C)
            s_ = mixed[:, ci * A_W:(ci + 1) * A_W] + bias_ref[...]
            sa_ref[rows, :] = _gelu(u_ref[rows, :]) * s_.astype(BF)

    return _pc(
        body, name="sgu_fwd", grid=(s // t,),
        in_specs=[pl.BlockSpec((t, A_W), lambda i: (i, 0)), pl.BlockSpec((t, A_W), lambda i: (i, 1)),
                  _const((1, A_W)), _const((1, A_W)), _const((A_C, A_G * A_C)), _const((A_C, A_W)),
                  _const((A_G * A_C, A_W))],
        out_specs=pl.BlockSpec((t, A_W), lambda i: (i, 0)),
        out_shape=SDS((s, A_W), BF), compiler_params=_cp(1))(z, z, ln_g, ln_b, w_cat, bias_full, bdm)


def _gla_decays(qk, lr, wgk, bias, l, t):
    nc = t // B_C
    q = qk[:, :B_HK].astype(F32) * (B_HK ** -0.5)
    k = qk[:, B_HK:].astype(F32)
    pre = _dot(lr, wgk) + bias
    la = _log_sigmoid(pre) * (1.0 / 16.0)
    b = _ldot3(l, la)
    b3 = b.reshape(nc, B_C, B_HK)
    bl = jnp.broadcast_to(b3[:, B_C - 1:B_C, :], (nc, B_C, B_HK)).reshape(t, B_HK)
    eb, enb, etb = jnp.exp(b), jnp.exp(-b), jnp.exp(bl - b)
    return pre, b, bl, eb, enb, etb, q * eb, k * enb, k * etb


def gla_fwd(z, qk32, zl, wgk, bias, wn, ltri, t):
    s = z.shape[0]
    nc = t // B_C
    hpb = GLA_HPB
    assert hpb == B_H
    kw, vw = hpb * B_HK, hpb * B_HV

    def body(qk_ref, v_ref, og_ref, lr_ref, wgk_ref, bias_ref, wn_ref, l_ref, ob_ref, o_ref, st_ref, st_sc, o_sc):
        @pl.when(pl.program_id(0) == 0)
        def _():
            st_sc[...] = jnp.zeros((B_H, B_HV, B_HK), F32)

        lr, l = lr_ref[...], l_ref[...]
        for hh in range(hpb):
            h = hh
            cv, ck = slice(hh * B_HV, (hh + 1) * B_HV), slice(hh * B_HK, (hh + 1) * B_HK)
            _, _, bl, _, _, _, qd, ki, kt = _gla_decays(qk_ref[:, cv], lr, wgk_ref[:, ck], bias_ref[:, ck], l, t)
            qd, ki, kt = qd.astype(BF), ki.astype(BF), kt.astype(BF)
            vb = v_ref[:, cv]
            sc = jnp.where(l > 0, _dot_nt(qd, ki), 0.0).astype(BF)
            o_sc[hh] = _dot(sc, vb)
            for n in range(nc):
                rows = slice(n * B_C, (n + 1) * B_C)
                st = st_sc[h]
                stb = st.astype(BF)
                st_ref[n, hh] = stb
                o_sc[hh, rows, :] += _dot_nt(qd[rows], stb)
                st_sc[h] = st * jnp.exp(bl[n * B_C:n * B_C + 1, :]) + _dot_tn(vb[rows], kt[rows])
            ob = o_sc[hh].astype(BF)
            o_ref[:, cv] = ob
            og = og_ref[:, cv].astype(F32)
            ob_ref[:, cv] = (_rms(ob.astype(F32), wn_ref[...]) * og * jax.nn.sigmoid(og)).astype(BF)

    vo, go = 2048 // vw, 3072 // vw
    return _pc(
        body, name="gla_fwd", grid=(s // t, B_H // hpb),
        in_specs=[pl.BlockSpec((t, vw), lambda i, g: (i, g)), pl.BlockSpec((t, vw), lambda i, g: (i, vo + g)),
                  pl.BlockSpec((t, vw), lambda i, g: (i, go + g)), pl.BlockSpec((t, LANE), lambda i, g: (i, 0)),
                  pl.BlockSpec((LANE, kw), lambda i, g: (0, g)), pl.BlockSpec((1, kw), lambda i, g: (0, g)),
                  _const((1, B_HV)), _const((t, t))],
        out_specs=[pl.BlockSpec((t, vw), lambda i, g: (i, g)), pl.BlockSpec((t, vw), lambda i, g: (i, g)),
                   pl.BlockSpec((nc, hpb, B_HV, B_HK), lambda i, g: (i, g, 0, 0))],
        out_shape=[SDS((s, D), BF), SDS((s, D), BF), SDS((s // B_C, B_H, B_HV, B_HK), BF)],
        scratch_shapes=[pltpu.VMEM((B_H, B_HV, B_HK), F32), pltpu.VMEM((hpb, t, B_HV), F32)],
        compiler_params=_cp(2))(qk32, z, z, zl, wgk, bias, wn, ltri)


def mix_fwd(sa, ob, z, x, a_out, b_out, w_mix, g2, t):
    s = x.shape[0]

    def body(sa_ref, ob_ref, ga_ref, gb_ref, x_ref, ao_ref, bo_ref, wm_ref, g2_ref,
             ya_ref, yb_ref, mp_ref, mx_ref, h1_ref):
        ya = _dot(sa_ref[...], ao_ref[...]).astype(BF)
        yb = _dot(ob_ref[...], bo_ref[...]).astype(BF)
        ya_ref[...] = ya
        yb_ref[...] = yb
        mp = (jax.nn.sigmoid(ga_ref[...].astype(F32)) * ya.astype(F32)
              + jax.nn.sigmoid(gb_ref[...].astype(F32)) * yb.astype(F32)).astype(BF)
        mp_ref[...] = mp
        mx = _dot(mp, wm_ref[...]).astype(BF)
        mx_ref[...] = mx
        h1_ref[...] = x_ref[...] + _rms(mx.astype(F32), g2_ref[...])

    row = lambda w: pl.BlockSpec((t, w), lambda i: (i, 0))
    return _pc(
        body, name="mix_fwd", grid=(s // t,),
        in_specs=[row(A_W), row(D), pl.BlockSpec((t, D), lambda i: (i, 4)), pl.BlockSpec((t, D), lambda i: (i, 5)),
                  row(D), _const((A_W, D)), _const((D, D)), _const((D, D)), _const((1, D))],
        out_specs=[row(D)] * 5,
        out_shape=[SDS((s, D), BF)] * 4 + [SDS((s, D), F32)],
        compiler_params=_cp(1))(sa, ob, z, z, x, a_out, b_out, w_mix, g2)


def ffn_up_fwd(h1, g3, w_up3, conv_w, conv_b, t):
    s = h1.shape[0]
    bn = w_up3.shape[2]

    def body(x_ref, g_ref, wg_ref, wv_ref, cwg_ref, cwv_ref, cbg_ref, cbv_ref,
             c_ref, ug_ref, uv_ref, cg_ref, cv_ref, ff_ref, c_sc, carry):
        i, j = pl.program_id(0), pl.program_id(1)

        @pl.when(j == 0)
        def _():
            c = _rms(x_ref[...], g_ref[...]).astype(BF)
            c_sc[...] = c
            c_ref[...] = c

        @pl.when(i == 0)
        def _():
            carry[j] = jnp.zeros((2, 8, bn), F32)

        def branch(k, w_ref, cw_ref, cb_ref, u_ref, o_ref):
            ub = _dot(c_sc[...], w_ref[...]).astype(BF)
            u_ref[...] = ub
            u = ub.astype(F32)
            ext = jnp.concatenate([carry[j, k], u], axis=0)
            carry[j, k] = u[t - 8:]
            w = cw_ref[...]
            cc = (cb_ref[...] + w[0:1] * pltpu.roll(ext, 2, 0) + w[1:2] * pltpu.roll(ext, 1, 0) + w[2:3] * ext)[8:]
            cc = cc.astype(BF)
            o_ref[...] = cc
            return cc

        g = _gelu(branch(0, wg_ref, cwg_ref, cbg_ref, ug_ref, cg_ref))
        ff_ref[...] = g * branch(1, wv_ref, cwv_ref, cbv_ref, uv_ref, cv_ref)

    col = lambda rows, off: pl.BlockSpec((rows, bn), lambda i, j: (0, j + off))
    out = pl.BlockSpec((t, bn), lambda i, j: (i, j))
    return _pc(
        body, name="ffn_up_fwd", grid=(s // t, 2),
        in_specs=[pl.BlockSpec((t, D), lambda i, j: (i, 0)), _const((1, D)),
                  pl.BlockSpec((None, D, bn), lambda i, j: (j, 0, 0)), pl.BlockSpec((None, D, bn), lambda i, j: (j + 2, 0, 0)),
                  col(3, 0), col(3, 2), col(1, 0), col(1, 2)],
        out_specs=[pl.BlockSpec((t, D), lambda i, j: (i, 0))] + [out] * 5,
        out_shape=[SDS((s, D), BF)] + [SDS((s, D_FF), BF)] * 5,
        scratch_shapes=[pltpu.VMEM((t, D), BF), pltpu.VMEM((2, 2, 8, bn), F32)],
        compiler_params=_cp(2))(h1, g3, w_up3, w_up3, conv_w, conv_w, conv_b, conv_b)


def ffn_up_bwd(ug, uv, cg, cv, dff, conv_w, w_up, h1, g3, dh2, t):
    s = h1.shape[0]
    nt = s // t
    hb = t // 8
    bn = 1408
    r = t + 8

    def body(ug_ref, uv_ref, cg_ref, cv_ref, cag_ref, cav_ref, d_ref, da_ref, cw_ref, w_ref, x_ref, g_ref, dres_ref,
             du_ref, gw_ref, gb_ref, dx_ref, gg_ref):
        i = pl.program_id(0)

        @pl.when(i == 0)
        def _():
            gw_ref[...] = jnp.zeros((3, 2 * D_FF), F32)
            gb_ref[...] = jnp.zeros((1, 2 * D_FF), F32)
            gg_ref[...] = jnp.zeros((1, D), F32)

        more = (i < nt - 1).astype(BF)

        def gate(c_g, c_v, d_):
            gl, dgl = _gelu_and_grad(c_g)
            return (d_ * c_v * dgl).astype(F32), (d_ * gl).astype(F32)

        def back(dc, dc_next, u_ref, cols, off):
            w = cw_ref[:, off:off + bn]
            d_ext = jnp.concatenate([dc, dc_next], axis=0)
            d1, d2 = pltpu.roll(d_ext, r - 1, 0)[:t], pltpu.roll(d_ext, r - 2, 0)[:t]
            du_ref[:, off:off + bn] = (w[2:3] * dc + w[1:2] * d1 + w[0:1] * d2).astype(BF)
            u = u_ref[:, cols].astype(F32)
            gw_ref[0:1, off:off + bn] += jnp.sum(d2 * u, axis=0, keepdims=True)
            gw_ref[1:2, off:off + bn] += jnp.sum(d1 * u, axis=0, keepdims=True)
            gw_ref[2:3, off:off + bn] += jnp.sum(dc * u, axis=0, keepdims=True)
            gb_ref[:, off:off + bn] += jnp.sum(dc, axis=0, keepdims=True)

        for kb in range(D_FF // bn):
            cols = slice(kb * bn, (kb + 1) * bn)
            dg, dv = gate(cg_ref[:, cols], cv_ref[:, cols], d_ref[:, cols])
            dg_n, dv_n = gate(cag_ref[:, cols], cav_ref[:, cols], da_ref[:, cols] * more)
            back(dg, dg_n, ug_ref, cols, kb * bn)
            back(dv, dv_n, uv_ref, cols, D_FF + kb * bn)

        acc = _dot_nt(du_ref[...], w_ref[...])
        dxn, dg3 = _rms_bwd(acc, x_ref[...], g_ref[...])
        dx_ref[...] = dres_ref[...] + dxn
        gg_ref[...] += dg3

    tile = lambda width: pl.BlockSpec((t, width), lambda i: (i, 0))
    after = pl.BlockSpec((8, D_FF), lambda i: (jnp.minimum((i + 1) * hb, nt * hb - 1), 0))
    return _pc(
        body, name="ffn_up_bwd", grid=(nt,),
        in_specs=[tile(D_FF)] * 4 + [after, after, tile(D_FF), after, _const((3, 2 * D_FF)), _const((D, 2 * D_FF)),
                                     tile(D), _const((1, D)), tile(D)],
        out_specs=[tile(2 * D_FF), _acc((3, 2 * D_FF)), _acc((1, 2 * D_FF)), tile(D), _acc((1, D))],
        out_shape=[SDS((s, 2 * D_FF), BF), SDS((3, 2 * D_FF), F32), SDS((1, 2 * D_FF), F32),
                   SDS((s, D), F32), SDS((1, D), F32)],
        compiler_params=_cp(1))(ug, uv, cg, cv, cg, cv, dff, dff, conv_w, w_up, h1, g3, dh2)


def out_fwd(ff, h1, p, tgt, w_down, w_pg, w_ple, g4, g5, t):
    s = h1.shape[0]

    def body(ff_ref, h1_ref, p_ref, t_ref, wd_ref, wpg_ref, wpl_ref, g4_ref, g5_ref,
             f_ref, h2_ref, pg_ref, pe_ref, dy_ref, loss_ref):
        @pl.when(pl.program_id(0) == 0)
        def _():
            loss_ref[...] = jnp.zeros((1, 1), F32)

        f = _dot(ff_ref[...], wd_ref[...]).astype(BF)
        f_ref[...] = f
        h2 = h1_ref[...] + _rms(f.astype(F32), g4_ref[...])
        h2b = h2.astype(BF)
        h2_ref[...] = h2b
        pg = _dot(h2b, wpg_ref[...]).astype(BF)
        pe = _dot(p_ref[...].astype(BF), wpl_ref[...]).astype(BF)
        pg_ref[...] = pg
        pe_ref[...] = pe
        y = h2 + _rms(jax.nn.sigmoid(pg.astype(F32)) * pe.astype(F32), g5_ref[...])
        err = y - t_ref[...]
        dy_ref[...] = err * (1.0 / D)
        loss_ref[...] += (0.5 / D) * jnp.sum(err * err)

    row = lambda w: pl.BlockSpec((t, w), lambda i: (i, 0))
    return _pc(
        body, name="out_fwd", grid=(s // t,),
        in_specs=[row(D_FF), row(D), row(PLE), row(D), _const((D_FF, D)), _const((D, D)), _const((PLE, D)),
                  _const((1, D)), _const((1, D))],
        out_specs=[row(D)] * 5 + [_acc((1, 1))],
        out_shape=[SDS((s, D), BF)] * 4 + [SDS((s, D), F32), SDS((1, 1), F32)],
        compiler_params=_cp(1))(ff, h1, p, tgt, w_down, w_pg, w_ple, g4, g5)


def out_bwd(dy, pg, pe, f, g5, g4, w_pg, w_down, t):
    s = dy.shape[0]

    def body(dy_ref, pg_ref, pe_ref, f_ref, g5_ref, g4_ref, wpg_ref, wd_ref,
             dh2_ref, dpe_ref, dpg_ref, df_ref, dff_ref, gg5_ref, gg4_ref):
        @pl.when(pl.program_id(0) == 0)
        def _():
            gg5_ref[...] = jnp.zeros((1, D), F32)
            gg4_ref[...] = jnp.zeros((1, D), F32)

        dy_ = dy_ref[...]
        pg_ = pg_ref[...].astype(F32)
        pe_ = pe_ref[...].astype(F32)
        sg = jax.nn.sigmoid(pg_)
        dple, dg5 = _rms_bwd(dy_, sg * pe_, g5_ref[...])
        gg5_ref[...] += dg5
        dpe_ref[...] = (dple * sg).astype(BF)
        dpg = (dple * pe_ * sg * (1.0 - sg)).astype(BF)
        dpg_ref[...] = dpg
        dh2 = dy_ + _dot_nt(dpg, wpg_ref[...])
        dh2_ref[...] = dh2
        df, dg4 = _rms_bwd(dh2, f_ref[...].astype(F32), g4_ref[...])
        gg4_ref[...] += dg4
        dfb = df.astype(BF)
        df_ref[...] = dfb
        dff_ref[...] = _dot_nt(dfb, wd_ref[...]).astype(BF)

    row = lambda w: pl.BlockSpec((t, w), lambda i: (i, 0))
    return _pc(
        body, name="out_bwd", grid=(s // t,),
        in_specs=[row(D), row(D), row(D), row(D), _const((1, D)), _const((1, D)), _const((D, D)), _const((D_FF, D))],
        out_specs=[row(D), row(D), row(D), row(D), row(D_FF), _acc((1, D)), _acc((1, D))],
        out_shape=[SDS((s, D), F32), SDS((s, D), BF), SDS((s, D), BF), SDS((s, D), BF), SDS((s, D_FF), BF),
                   SDS((1, D), F32), SDS((1, D), F32)],
        compiler_params=_cp(1))(dy, pg, pe, f, g5, g4, w_pg, w_down)


def nt_normbwd(dys, w, xin, gain, dres, t, name, side=None):
    s = xin.shape[0]
    nt = s // t
    np_ = len(dys)

    def body(*refs):
        ins_, s_in, (dx_ref, gg_ref), s_out, _, s_scr = _split_side(refs, np_ + 4, 2, 0, side)
        dy_refs = ins_[:np_]
        w_ref, x_ref, g_ref, dres_ref = ins_[np_:]
        i = pl.program_id(0)

        @pl.when(i == 0)
        def _():
            gg_ref[...] = jnp.zeros((1, D), F32)
            if side:
                side.start(s_in, s_out, *s_scr)

        acc = _dot(jnp.concatenate([r_[...] for r_ in dy_refs], axis=1), w_ref[...])
        dxn, dg = _rms_bwd(acc, x_ref[...], g_ref[...])
        dx_ref[...] = dres_ref[...] + dxn
        gg_ref[...] += dg
        if side:
            @pl.when(i == nt - 1)
            def _():
                side.finish(s_in, s_out, *s_scr)

    row = lambda width: pl.BlockSpec((t, width), lambda i: (i, 0))
    si_specs, so_specs, so_shapes, s_scratch, s_ins = _side_specs(side)
    assert sum(dy.shape[1] for dy in dys) == w.shape[0]
    return _pc(
        body, name=name, grid=(nt,),
        in_specs=[row(dy.shape[1]) for dy in dys] + [_const(w.shape), row(D), _const((1, D)), row(D)] + si_specs,
        out_specs=[row(D), _acc((1, D))] + so_specs,
        out_shape=[SDS((s, D), F32), SDS((1, D), F32)] + so_shapes, scratch_shapes=s_scratch,
        compiler_params=_cp(1))(*dys, w, xin, gain, dres, *s_ins)


def mix_bwd(dh1, mx, z, ya, yb, g2, w_mix, a_out, b_out, t, side=None):
    s = dh1.shape[0]
    nt = s // t

    def body(*refs):
        ins_, s_in, outs_, s_out, _, s_scr = _split_side(refs, 10, 8, 0, side)
        dh_ref, mx_ref, ga_ref, gb_ref, ya_ref, yb_ref, g2_ref, wm_ref, ao_ref, bo_ref = ins_
        dmx_ref, dya_ref, dyb_ref, dga_ref, dgb_ref, dsa_ref, dob_ref, gg2_ref = outs_

        @pl.when(pl.program_id(0) == 0)
        def _():
            gg2_ref[...] = jnp.zeros((1, D), F32)
            if side:
                side.start(s_in, s_out, *s_scr)

        dmx, dg2 = _rms_bwd(dh_ref[...], mx_ref[...].astype(F32), g2_ref[...])
        gg2_ref[...] += dg2
        dmxb = dmx.astype(BF)
        dmx_ref[...] = dmxb
        dmp = _dot_nt(dmxb, wm_ref[...]).astype(BF)

        def gate(g_ref, y_ref, dy_ref, dg_ref, w_ref, dz_ref):
            sg = jax.nn.sigmoid(g_ref[...])
            dyb_ = dmp * sg
            dy_ref[...] = dyb_
            dg_ref[...] = dyb_ * y_ref[...] * (1.0 - sg)
            dz_ref[...] = _dot_nt(dyb_, w_ref[...]).astype(BF)

        gate(ga_ref, ya_ref, dya_ref, dga_ref, ao_ref, dsa_ref)
        gate(gb_ref, yb_ref, dyb_ref, dgb_ref, bo_ref, dob_ref)
        if side:
            @pl.when(pl.program_id(0) == nt - 1)
            def _():
                side.finish(s_in, s_out, *s_scr)

    row = lambda w: pl.BlockSpec((t, w), lambda i: (i, 0))
    si_specs, so_specs, so_shapes, s_scratch, s_ins = _side_specs(side)
    return _pc(
        body, name="mix_bwd", grid=(nt,),
        in_specs=[row(D), row(D), pl.BlockSpec((t, D), lambda i: (i, 4)), pl.BlockSpec((t, D), lambda i: (i, 5)),
                  row(D), row(D), _const((1, D)), _const((D, D)), _const((A_W, D)), _const((D, D))] + si_specs,
        out_specs=[row(D)] * 5 + [row(A_W), row(D), _acc((1, D))] + so_specs,
        out_shape=[SDS((s, D), BF)] * 5 + [SDS((s, A_W), BF), SDS((s, D), BF), SDS((1, D), F32)] + so_shapes,
        scratch_shapes=s_scratch,
        compiler_params=_cp(1))(dh1, mx, z, z, ya, yb, g2, w_mix, a_out, b_out, *s_ins)


def sgu_bwd(z, dsa, ln_g, ln_b, w_cat, w_cat_t, bias_full, bdm, t):
    s = z.shape[0]
    nt = s // t
    nch = t // A_C

    def body(u_ref, v_ref, dsa_ref, g_ref, b_ref, wc_ref, wct_ref, bias_ref, bdm_ref,
             duv_ref, glg_ref, glb_ref, gws_ref, gbs_ref, ds_acc):
        i = pl.program_id(0)

        @pl.when(i == 0)
        def _():
            glg_ref[...] = jnp.zeros((1, A_W), F32)
            glb_ref[...] = jnp.zeros((1, A_W), F32)
            gws_ref[...] = jnp.zeros((A_C, A_G * A_C), F32)
            ds_acc[...] = jnp.zeros((A_C, A_W), F32)

        lng, bdm_ = g_ref[...], bdm_ref[...]
        rec = [_sgu_recompute(v_ref[pl.ds(ci * A_C, A_C), :], lng, b_ref[...]) for ci in range(nch)]
        spread_vn = _sgu_spread([r_[3] for r_ in rec], bdm_)
        mixed = _dot(_sgu_weights(wc_ref, False), spread_vn)
        dsas, dss, dgus = [], [], []
        for ci in range(nch):
            rows = pl.ds(ci * A_C, A_C)
            gu, dgu = _gelu_and_grad(u_ref[rows, :])
            dsa_ = dsa_ref[rows, :]
            ds = dsa_ * gu
            ds_acc[...] += ds.astype(F32)
            dsas.append(dsa_)
            dgus.append(dgu)
            dss.append(ds)
        r = lax.broadcasted_iota(jnp.int32, (A_C, A_G * A_C), 0)
        c = lax.broadcasted_iota(jnp.int32, (A_C, A_G * A_C), 1) & (A_C - 1)
        gws_ref[...] += jnp.where(c <= r, _dot_nt(jnp.concatenate(dss, axis=1), spread_vn), 0.0)
        dvns = _dot(_sgu_weights(wct_ref, True), _sgu_spread(dss, bdm_))
        for ci in range(nch):
            rows = pl.ds(ci * A_C, A_C)
            dgv, rstd, xhat, _ = rec[ci]
            dvn = dvns[:, ci * A_W:(ci + 1) * A_W]
            glb_ref[...] += jnp.sum(dvn, axis=0, keepdims=True)
            glg_ref[...] += jnp.sum(dvn * xhat, axis=0, keepdims=True)
            dxh = dvn * lng
            dgv_ = rstd * (dxh - jnp.mean(dxh, axis=-1, keepdims=True)
                           - xhat * jnp.mean(dxh * xhat, axis=-1, keepdims=True))
            s_ = mixed[:, ci * A_W:(ci + 1) * A_W] + bias_ref[...]
            duv_ref[rows, :A_W] = dsas[ci] * dgus[ci] * s_.astype(BF)
            duv_ref[rows, A_W:] = (dgv_ * dgv).astype(BF)

        @pl.when(i == nt - 1)
        def _():
            acc = ds_acc[...]
            for g in range(A_G):
                gbs_ref[:, g:g + 1] = jnp.sum(acc[:, g * A_GD:(g + 1) * A_GD], axis=1, keepdims=True)

    return _pc(
        body, name="sgu_bwd", grid=(nt,),
        in_specs=[pl.BlockSpec((t, A_W), lambda i: (i, 0)), pl.BlockSpec((t, A_W), lambda i: (i, 1)),
                  pl.BlockSpec((t, A_W), lambda i: (i, 0)),
                  _const((1, A_W)), _const((1, A_W)), _const((A_C, A_G * A_C)), _const((A_C, A_G * A_C)),
                  _const((A_C, A_W)), _const((A_G * A_C, A_W))],
        out_specs=[pl.BlockSpec((t, D), lambda i: (i, 0)), _acc((1, A_W)), _acc((1, A_W)),
                   _acc((A_C, A_G * A_C)), _acc((A_C, A_G))],
        out_shape=[SDS((s, D), BF), SDS((1, A_W), F32), SDS((1, A_W), F32), SDS((A_C, A_G * A_C), F32),
                   SDS((A_C, A_G), F32)],
        scratch_shapes=[pltpu.VMEM((A_C, A_W), F32)],
        compiler_params=_cp(1))(z, z, dsa, ln_g, ln_b, w_cat, w_cat_t, bias_full, bdm)


def gla_bwd(z, qk32, zl, o, dob, states, wgk, bias, wn, ltri, ltri_t, t, side=None):
    s = z.shape[0]
    nt = s // t
    nc = t // B_C
    hpb = GLA_HPB
    assert hpb == B_H
    kw, vw = hpb * B_HK, hpb * B_HV

    def body(*refs):
        ins_, s_in, outs_, s_out, scr_, s_scr = _split_side(refs, 12, 7, 5, side)
        qk_ref, v_ref, og_ref, lr_ref, o_ref, dob_ref, st_ref, wgk_ref, bias_ref, wn_ref, l_ref, lt_ref = ins_
        dqk_ref, dv_ref, dog_ref, dpre_ref, dlr_ref, gbias_ref, gwn_ref = outs_
        dst_sc, dv_sc, dqd_sc, dkt_sc, ddec_sc = scr_
        i = pl.program_id(0)
        g = pl.program_id(1)

        @pl.when((i == 0) & (g == 0))
        def _():
            gbias_ref[...] = jnp.zeros((B_H, 1, B_HK), F32)
            gwn_ref[...] = jnp.zeros((1, B_HV), F32)
            if side:
                side.start(s_in, s_out, *s_scr)

        @pl.when(i == 0)
        def _():
            dst_sc[...] = jnp.zeros((B_H, B_HV, B_HK), F32)

        lr, l, lt = lr_ref[...], l_ref[...], lt_ref[...]
        keep, keep_t = l > 0, lt > 0
        wn_ = wn_ref[...]
        last = lax.broadcasted_iota(jnp.int32, (nc, B_C, B_HK), 1) == B_C - 1
        for hh in range(hpb):
            h = hh
            cv, ck = slice(hh * B_HV, (hh + 1) * B_HV), slice(hh * B_HK, (hh + 1) * B_HK)
            pre, b, bl, eb, enb, etb, qd, ki, kt = _gla_decays(qk_ref[:, cv], lr, wgk_ref[:, ck], bias_ref[:, ck], l, t)
            qdb, kib, ktb = qd.astype(BF), ki.astype(BF), kt.astype(BF)
            vb = v_ref[:, cv]
            o_ = o_ref[:, cv].astype(F32)
            og = og_ref[:, cv].astype(F32)
            sog = jax.nn.sigmoid(og)
            dob_ = dob_ref[:, cv].astype(F32)
            don = dob_ * og * sog
            do, dwn = _rms_bwd(don, o_, wn_)
            gwn_ref[...] += dwn
            dog_ref[:, cv] = (dob_ * _rms(o_, wn_) * sog * (1.0 + og * (1.0 - sog))).astype(BF)
            dob16 = do.astype(BF)
            sc_t = jnp.where(keep_t, _dot_nt(kib, qdb), 0.0).astype(BF)
            dsc = jnp.where(keep, _dot_nt(dob16, vb), 0.0).astype(BF)
            dsc_t = jnp.where(keep_t, _dot_nt(vb, dob16), 0.0).astype(BF)
            dv_sc[hh] = _dot(sc_t, dob16)
            dqd_sc[hh] = _dot(dsc, kib)
            dki = _dot(dsc_t, qdb)
            for n in reversed(range(nc)):
                rows = slice(n * B_C, (n + 1) * B_C)
                dst = dst_sc[h]
                dstb = dst.astype(BF)
                stp = st_ref[n, hh]
                dv_sc[hh, rows, :] += _dot_nt(ktb[rows], dstb)
                dkt_sc[hh, rows, :] = _dot(vb[rows], dstb)
                dqd_sc[hh, rows, :] += _dot(dob16[rows], stp)
                dec = jnp.exp(bl[n * B_C:n * B_C + 1, :])
                ddec_sc[hh, n] = jnp.sum(dst * stp.astype(F32), axis=0, keepdims=True) * dec
                dst_sc[h] = dst * dec + _dot_tn(dob16[rows], qdb[rows])
            dqd, dkt = dqd_sc[hh], dkt_sc[hh]
            dv_ref[:, cv] = dv_sc[hh].astype(BF)
            dqk_ref[:, hh * B_HV:hh * B_HV + B_HK] = (dqd * eb * (B_HK ** -0.5)).astype(BF)
            dqk_ref[:, hh * B_HV + B_HK:(hh + 1) * B_HV] = (dki * enb + dkt * etb).astype(BF)
            dktkt = dkt * kt
            db3 = (dqd * qd - dki * ki - dktkt).reshape(nc, B_C, B_HK)
            dbl = jnp.sum(dktkt.reshape(nc, B_C, B_HK), axis=1, keepdims=True) + ddec_sc[hh]
            db = (db3 + jnp.where(last, dbl, 0.0)).reshape(t, B_HK)
            dla = _ldot3(lt, db)
            dpre = dla * (1.0 / 16.0) * (1.0 - jax.nn.sigmoid(pre))
            dpreb = dpre.astype(BF)
            dpre_ref[:, ck] = dpreb
            gbias_ref[h] += jnp.sum(dpre, axis=0, keepdims=True)
            dlr_h = _dot_nt(dpreb, wgk_ref[:, ck])
            dlr = dlr_h if hh == 0 else dlr + dlr_h
        dlr_ref[...] = dlr.astype(BF)
        if side:
            @pl.when((i == nt - 1) & (g == B_H // hpb - 1))
            def _():
                side.finish(s_in, s_out, *s_scr)

    rv = lambda i: nt - 1 - i
    si_specs, so_specs, so_shapes, s_scratch, s_ins = _side_specs(side)
    vo, go = 2048 // vw, 3072 // vw
    tile = lambda off: pl.BlockSpec((t, vw), lambda i, g: (rv(i), off + g))
    return _pc(
        body, name="gla_bwd", grid=(nt, B_H // hpb),
        in_specs=[tile(0), tile(vo), tile(go), pl.BlockSpec((t, LANE), lambda i, g: (rv(i), 0)), tile(0), tile(0),
                  pl.BlockSpec((nc, hpb, B_HV, B_HK), lambda i, g: (rv(i), g, 0, 0)),
                  pl.BlockSpec((LANE, kw), lambda i, g: (0, g)), pl.BlockSpec((1, kw), lambda i, g: (0, g)),
                  _const((1, B_HV)), _const((t, t)), _const((t, t))] + si_specs,
        out_specs=[tile(0), tile(0), tile(0), pl.BlockSpec((t, kw), lambda i, g: (rv(i), g)),
                   pl.BlockSpec((t, LANE), lambda i, g: (rv(i), 0)), _acc((B_H, 1, B_HK)), _acc((1, B_HV))] + so_specs,
        out_shape=[SDS((s, D), BF), SDS((s, D), BF), SDS((s, D), BF), SDS((s, B_H * B_HK), BF), SDS((s, LANE), BF),
                   SDS((B_H, 1, B_HK), F32), SDS((1, B_HV), F32)] + so_shapes,
        scratch_shapes=[pltpu.VMEM((B_H, B_HV, B_HK), F32), pltpu.VMEM((hpb, t, B_HV), F32),
                        pltpu.VMEM((hpb, t, B_HK), F32), pltpu.VMEM((hpb, t, B_HK), F32),
                        pltpu.VMEM((hpb, nc, 1, B_HK), F32)] + s_scratch,
        compiler_params=_cp(2))(qk32, z, z, zl, o, dob, states, wgk, bias, wn, ltri, ltri_t, *s_ins)


def mm_tn(a, b, name, tk=2048):
    s, m = a.shape
    n = b.shape[1]
    bn = next(c for c in (1024, 1408, 512, 256, 128) if n % c == 0 and m * c * 4 <= 6 * 1024 * 1024)
    tk = min(tk, s)
    nk = s // tk

    def body(a_ref, b_ref, o_ref, acc):
        k = pl.program_id(1)

        @pl.when(k == 0)
        def _():
            acc[...] = jnp.zeros((m, bn), F32)

        acc[...] += _dot_tn(a_ref[...].astype(BF), b_ref[...])

        @pl.when(k == nk - 1)
        def _():
            o_ref[...] = acc[...].astype(BF)

    return _pc(
        body, name=name, grid=(n // bn, nk),
        in_specs=[pl.BlockSpec((tk, m), lambda j, k: (k, 0)), pl.BlockSpec((tk, bn), lambda j, k: (k, j))],
        out_specs=pl.BlockSpec((m, bn), lambda j, k: (0, j)),
        out_shape=SDS((m, n), BF), scratch_shapes=[pltpu.VMEM((m, bn), F32)], compiler_params=_cp(2))(a, b)


def _adamw(w, g, m, v):
    m = ADAM_B1 * m + (1.0 - ADAM_B1) * g
    v = ADAM_B2 * v + (1.0 - ADAM_B2) * (g * g)
    m_hat = m / (1.0 - ADAM_B1 ** ADAM_STEP)
    v_hat = v / (1.0 - ADAM_B2 ** ADAM_STEP)
    return -ADAM_LR * (m_hat / (jnp.sqrt(v_hat) + ADAM_EPS) + ADAM_WD * w), m, v


def _half_rows(rows):
    rh = rows // 2
    return rh, max(b for b in range(16, 257, 16) if rh % b == 0)


def _pc_sp(body, grid, in_specs, out_specs, out_shape, name):
    gs = pltpu.PrefetchScalarGridSpec(num_scalar_prefetch=1, grid=grid, in_specs=in_specs, out_specs=out_specs)
    return _pc(body, grid_spec=gs, out_shape=out_shape, name=name, compiler_params=_cp(len(grid)))


def adamw_item(own, sib, w, m, v):
    rows, cols = w.shape
    rh, br = _half_rows(rows)
    nbk = rh // br

    def fn(ins, outs, b, sc_ref):
        own_ref, sib_ref, w_ref, m_ref, v_ref = ins
        g_ = jnp.where(b // nbk == sc_ref[0], own_ref[...], sib_ref[...])
        outs[0][...] = g_
        outs[1][...], outs[2][...], outs[3][...] = _adamw(w_ref[...], g_, m_ref[...], v_ref[...])

    blk = (br, cols)
    mine = lambda b, sc_: (jnp.clip(b - sc_[0] * nbk, 0, nbk - 1), 0)
    theirs = lambda b, sc_: (jnp.clip(b - (1 - sc_[0]) * nbk, 0, nbk - 1), 0)
    each = lambda b, sc_: (b, 0)
    return ([(own, blk, mine), (sib, blk, theirs), (w, blk, each), (m, blk, each), (v, blk, each)],
            [(SDS((rows, cols), F32), blk, each)] * 4, 2 * nbk, fn)


def adamw_cols(sc, own, sib, w, m, v, name, cb=256):
    rows, cols = w.shape
    nk = cols // 2 // cb

    def body(sc_ref, own_ref, sib_ref, w_ref, m_ref, v_ref, go_ref, d_ref, mo_ref, vo_ref):
        g_ = jnp.where(pl.program_id(0) == sc_ref[0], own_ref[...], sib_ref[...])
        go_ref[...] = g_
        d_ref[...], mo_ref[...], vo_ref[...] = _adamw(w_ref[...], g_, m_ref[...], v_ref[...])

    mine = pl.BlockSpec((rows, cb), lambda h, k, sc_: (0, jnp.clip(k + (h - sc_[0]) * nk, 0, nk - 1)))
    theirs = pl.BlockSpec((rows, cb), lambda h, k, sc_: (0, jnp.clip(k + (h - 1 + sc_[0]) * nk, 0, nk - 1)))
    blk = pl.BlockSpec((rows, cb), lambda h, k, sc_: (0, h * nk + k))
    return _pc_sp(body, (2, nk), [mine, theirs, blk, blk, blk], [blk] * 4, [SDS((rows, cols), F32)] * 4,
                  name)(sc, own, sib, w, m, v)


def adamw_small(g, w, m, v):
    def body(g_ref, w_ref, m_ref, v_ref, d_ref, mo_ref, vo_ref):
        d_ref[...], mo_ref[...], vo_ref[...] = _adamw(w_ref[...], g_ref[...], m_ref[...], v_ref[...])

    vm = pl.BlockSpec(memory_space=pltpu.VMEM)
    return _pc(body, name="adamw_small", in_specs=[vm] * 4, out_specs=[vm] * 3, out_shape=[SDS(g.shape, F32)] * 3,
               compiler_params=pltpu.CompilerParams(vmem_limit_bytes=VMEM_LIMIT))(g, w, m, v)


def _pos():
    return lax.axis_index("x"), lax.axis_index("y"), lax.axis_index("c")


def _other_chips(x, y):
    return [(1 - x, y), (x, 1 - y), (1 - x, 1 - y)]


_ANY = pl.BlockSpec(memory_space=pltpu.HBM)


class _Side:
    def __init__(self, ins, out_shapes, nsem, start, finish):
        self.ins, self.out_shapes, self.start, self.finish = list(ins), list(out_shapes), start, finish
        self.scratch = [pltpu.SemaphoreType.DMA((nsem,)), pltpu.SemaphoreType.DMA((nsem,))]
        self.n_in, self.n_out = len(self.ins), len(self.out_shapes)


def _run_side(side, name):
    def body(*refs):
        args_ = (refs[:side.n_in], refs[side.n_in:side.n_in + side.n_out], *refs[side.n_in + side.n_out:])
        side.start(*args_)
        side.finish(*args_)

    return _pc(body, name=name, in_specs=[_ANY] * side.n_in, out_specs=[_ANY] * side.n_out,
               out_shape=side.out_shapes, scratch_shapes=side.scratch)(*side.ins)


def _split_rows(shape):
    return (shape[0] // 2) % 16 == 0


def _core_halves(shape, c):
    if _split_rows(shape):
        h = shape[0] // 2
        return ((pl.ds(pl.multiple_of(c * h, 16), h), slice(None)),
                (pl.ds(pl.multiple_of((1 - c) * h, 16), h), slice(None)))
    h = shape[1] // 2
    assert h % LANE == 0
    return ((slice(None), pl.ds(pl.multiple_of(c * h, LANE), h)),
            (slice(None), pl.ds(pl.multiple_of((1 - c) * h, LANE), h)))


def gather_side(bigs, tinies):
    nb, nt_ = len(bigs), len(tinies)

    def plan(ins, outs, ssem, rsem):
        x, y, c = _pos()
        me = 2 * x + y
        chips = _other_chips(x, y)
        sibling = (x, y, 1 - c)

        def copy(k, src, dst, to):
            return pltpu.make_async_remote_copy(src_ref=src, dst_ref=dst, send_sem=ssem.at[k], recv_sem=rsem.at[k],
                                                device_id=to, device_id_type=MESH)

        sends, landed, passed_on, tiny_landed = [], [], [], []
        for w in range(nb):
            mine, theirs = _core_halves(bigs[w].shape, c)
            for j, (cx, cy) in enumerate(chips):
                sends.append(copy(6 * w + j, ins[w].at[mine], outs[w].at[(me,) + mine], (cx, cy, c)))
                blk = outs[w].at[(2 * cx + cy,) + mine]
                landed.append((copy(6 * w + j, blk, blk, (cx, cy, c)), copy(6 * w + 3 + j, blk, blk, sibling)))
                blk = outs[w].at[(2 * cx + cy,) + theirs]
                passed_on.append(copy(6 * w + 3 + j, blk, blk, sibling))
        for w in range(nt_):
            for j, (cx, cy) in enumerate(chips):
                k = 6 * nb + 3 * w + j
                sends.append(copy(k, ins[nb + w], outs[nb + w].at[me], (cx, cy, c)))
                blk = outs[nb + w].at[2 * cx + cy]
                tiny_landed.append(copy(k, blk, blk, (cx, cy, c)))
        return sends, landed, passed_on, tiny_landed

    def start(ins, outs, ssem, rsem):
        for cp in plan(ins, outs, ssem, rsem)[0]:
            cp.start()

    def finish(ins, outs, ssem, rsem):
        sends, landed, passed_on, tiny_landed = plan(ins, outs, ssem, rsem)
        for arrived, forward in landed:
            arrived.wait_recv()
            forward.start()
        for arrived in tiny_landed + passed_on:
            arrived.wait_recv()
        for cp in sends + [forward for _, forward in landed]:
            cp.wait_send()

    return _Side(list(bigs) + list(tinies), [SDS((4,) + a.shape, a.dtype) for a in list(bigs) + list(tinies)],
                 6 * nb + 3 * nt_, start, finish)


def _sibling_side(srcs, out_shapes, pick):
    def plan(in_refs, out_refs, ssem, rsem):
        x, y, c = _pos()
        return [pltpu.make_async_remote_copy(src_ref=pick(in_refs[w], srcs[w].shape, c), dst_ref=out_refs[w],
                                             send_sem=ssem.at[w], recv_sem=rsem.at[w], device_id=(x, y, 1 - c),
                                             device_id_type=MESH) for w in range(len(srcs))]

    def start(*refs):
        for cp in plan(*refs):
            cp.start()

    def finish(*refs):
        for cp in plan(*refs):
            cp.wait()

    return _Side(srcs, out_shapes, len(srcs), start, finish)


def swap_side(gs):
    def half_shape(g):
        l, r, cols = g.shape
        return (l, r // 2, cols) if _split_rows((r, cols)) else (l, r, cols // 2)

    return _sibling_side(gs, [SDS(half_shape(g), g.dtype) for g in gs],
                         lambda ref, shape, c: ref.at[(slice(None),) + _core_halves(shape[1:], c)[1]])


def join_side(halves):
    return _sibling_side(halves, [SDS(h.shape, h.dtype) for h in halves], lambda ref, shape, c: ref)


def _both_sides(a, b):
    def split(ins, outs, *scr):
        return ((ins[:a.n_in], outs[:a.n_out], *scr[:2]), (ins[a.n_in:], outs[a.n_out:], *scr[2:]))

    def start(*refs):
        ra, rb = split(*refs)
        a.start(*ra)
        b.start(*rb)

    def finish(*refs):
        ra, rb = split(*refs)
        a.finish(*ra)
        b.finish(*rb)

    side = _Side(a.ins + b.ins, a.out_shapes + b.out_shapes, 1, start, finish)
    side.scratch = a.scratch + b.scratch
    return side


COL_BLOCK = 256


def _blockwise(sc, items, name):
    in_specs, out_specs, out_shapes, operands, spans = [], [], [], [], []
    start = 0
    for ins, outs, nb, _ in items:
        def spec(blk, idx, s0=start, nb=nb):
            return pl.BlockSpec(blk, lambda i, sc_: idx(jnp.clip(i - s0, 0, nb - 1), sc_))

        in_specs += [spec(blk, idx) for _, blk, idx in ins]
        out_specs += [spec(blk, idx) for _, blk, idx in outs]
        operands += [a for a, _, _ in ins]
        out_shapes += [s_ for s_, _, _ in outs]
        spans.append((start, start + nb))
        start += nb
    n_in = len(operands)

    def body(sc_ref, *refs):
        i = pl.program_id(0)
        at_in, at_out = 0, n_in
        for (ins, outs, _, fn), (lo, hi) in zip(items, spans):
            mine_in, mine_out = refs[at_in:at_in + len(ins)], refs[at_out:at_out + len(outs)]
            at_in, at_out = at_in + len(ins), at_out + len(outs)

            @pl.when((i >= lo) & (i < hi))
            def _(fn=fn, mine_in=mine_in, mine_out=mine_out, lo=lo):
                fn(mine_in, mine_out, i - lo, sc_ref)

    return _pc_sp(body, (start,), in_specs, out_specs, out_shapes, name)(sc, *operands)


def add_half_item(g, sib):
    l, r, cols = g.shape

    def fn(ins, outs, b, sc_ref):
        outs[0][...] = (ins[0][...].astype(F32) + ins[1][...].astype(F32)).astype(BF)

    if _split_rows((r, cols)):
        rh, br = _half_rows(r)
        nbk = rh // br
        blk = (1, br, cols)
        there = lambda b, sc_: (b // nbk, b % nbk, 0)
        return ([(g, blk, lambda b, sc_: (b // nbk, sc_[0] * nbk + b % nbk, 0)), (sib, blk, there)],
                [(SDS((l, rh, cols), BF), blk, there)], l * nbk, fn)
    nbk = cols // 2 // COL_BLOCK
    blk = (1, r, COL_BLOCK)
    there = lambda b, sc_: (b // nbk, 0, b % nbk)
    return ([(g, blk, lambda b, sc_: (b // nbk, 0, sc_[0] * nbk + b % nbk)), (sib, blk, there)],
            [(SDS((l, r, cols // 2), BF), blk, there)], l * nbk, fn)


def exchange_side(ps):
    n_ = len(ps)

    def width(p_):
        return p_.shape[2] if p_.shape[0] == 4 else p_.shape[2] // 4

    def plan(p_refs, got_refs, ssem, rsem):
        x, y, c = _pos()
        cps = []
        for w in range(n_):
            wd = width(ps[w])
            for j, (cx, cy) in enumerate(_other_chips(x, y)):
                to = 2 * cx + cy
                src = p_refs[w].at[to] if ps[w].shape[0] == 4 else p_refs[w].at[0, :, pl.ds(pl.multiple_of(to * wd, LANE), wd)]
                cps.append(pltpu.make_async_remote_copy(
                    src_ref=src, dst_ref=got_refs[w].at[j], send_sem=ssem.at[3 * w + j], recv_sem=rsem.at[3 * w + j],
                    device_id=(cx, cy, c), device_id_type=MESH))
        return cps

    def start(*refs):
        for cp in plan(*refs):
            cp.start()

    def finish(*refs):
        for cp in plan(*refs):
            cp.wait()

    return _Side(ps, [SDS((3, p_.shape[1], width(p_)), p_.dtype) for p_ in ps], 3 * n_, start, finish)


def sum4_item(p, got):
    _, rh, wd = got.shape

    def fn(ins, outs, b, sc_ref):
        p_ref, g_ref = ins
        outs[0][...] = ((p_ref[0].astype(F32) + g_ref[0].astype(F32))
                        + (g_ref[1].astype(F32) + g_ref[2].astype(F32)))

    if rh % 16:
        assert p.shape[0] == 4
        return ([(p, (1, rh, COL_BLOCK), lambda b, sc_: (sc_[1], 0, b)),
                 (got, (3, rh, COL_BLOCK), lambda b, sc_: (0, 0, b))],
                [(SDS((rh, wd), F32), (rh, COL_BLOCK), lambda b, sc_: (0, b))], wd // COL_BLOCK, fn)
    _, br = _half_rows(2 * rh)
    own = (lambda b, sc_: (sc_[1], b, 0)) if p.shape[0] == 4 else (lambda b, sc_: (0, b, sc_[1]))
    return ([(p, (1, br, wd), own), (got, (3, br, wd), lambda b, sc_: (0, b, 0))],
            [(SDS((rh, wd), F32), (br, wd), lambda b, sc_: (b, 0))], rh // br, fn)


def allreduce_small(g):
    rows = g.shape[0]
    rh = rows // 2

    def body(g_ref, out_ref, sib_buf, chip_buf, sum_sc, ssem, rsem):
        x, y, c = _pos()
        me = 2 * x + y
        sibling = (x, y, 1 - c)
        mine = pl.ds(pl.multiple_of(c * rh, 8), rh)

        def copy(k, src, dst, to):
            return pltpu.make_async_remote_copy(src_ref=src, dst_ref=dst, send_sem=ssem.at[k], recv_sem=rsem.at[k],
                                                device_id=to, device_id_type=MESH)

        cp = copy(0, g_ref, sib_buf, sibling)
        cp.start()
        cp.wait()
        sum_sc[...] = g_ref[...] + sib_buf[...]
        chips = _other_chips(x, y)
        cps = [copy(1 + j, sum_sc.at[mine], chip_buf.at[me], (cx, cy, c)) for j, (cx, cy) in enumerate(chips)]
        for cp in cps:
            cp.start()
        chip_buf[me] = sum_sc[mine, :]
        for j, (cx, cy) in enumerate(chips):
            copy(1 + j, sum_sc.at[mine], chip_buf.at[2 * cx + cy], (cx, cy, c)).wait_recv()
        for cp in cps:
            cp.wait_send()
        out_ref[mine, :] = (chip_buf[0] + chip_buf[1]) + (chip_buf[2] + chip_buf[3])
        cp = copy(4, out_ref.at[mine], out_ref.at[mine], sibling)
        cp.start()
        cp.wait()

    vm = pl.BlockSpec(memory_space=pltpu.VMEM)
    return _pc(body, name="allreduce_small", in_specs=[vm], out_specs=vm, out_shape=SDS((rows, LANE), F32),
               scratch_shapes=[pltpu.VMEM((rows, LANE), F32), pltpu.VMEM((4, rh, LANE), F32), pltpu.VMEM((rows, LANE), F32),
                               pltpu.SemaphoreType.DMA((5,)), pltpu.SemaphoreType.DMA((5,))],
               compiler_params=pltpu.CompilerParams(vmem_limit_bytes=VMEM_LIMIT))(g)


def _pack_small(entries, get):
    flat = jnp.concatenate([get(n).reshape(-1).astype(F32) for n, _ in entries])
    rows = -(-flat.shape[0] // (8 * LANE)) * 8
    return jnp.pad(flat, (0, rows * LANE - flat.shape[0])).reshape(rows, LANE)


def _unpack_small(entries, packed):
    out, off = {}, 0
    flat = packed.reshape(-1)
    for name, n in entries:
        out[name] = flat[off:off + n]
        off += n
    return out


def _cols_full(blk):
    return blk.transpose(1, 0, 2).reshape(blk.shape[1], 4 * blk.shape[2])


def kernel(x, p, pre_mix_norm, w_in, a_ln_g, a_ln_b, a_spatial_w, a_spatial_b, a_out, b_gk, b_gk_bias, b_out_norm, b_out, w_mix_out, post_mix_norm, pre_ffn_norm, w_up, conv_w, conv_b, w_down, post_ffn_norm, w_ple, w_ple_gate, post_ple_norm, loss_target, m_pre_mix_norm, m_w_in, m_a_ln_g, m_a_ln_b, m_a_spatial_w, m_a_spatial_b, m_a_out, m_b_gk, m_b_gk_bias, m_b_out_norm, m_b_out, m_w_mix_out, m_post_mix_norm, m_pre_ffn_norm, m_w_up, m_conv_w, m_conv_b, m_w_down, m_post_ffn_norm, m_w_ple, m_w_ple_gate, m_post_ple_norm, v_pre_mix_norm, v_w_in, v_a_ln_g, v_a_ln_b, v_a_spatial_w, v_a_spatial_b, v_a_out, v_b_gk, v_b_gk_bias, v_b_out_norm, v_b_out, v_w_mix_out, v_post_mix_norm, v_pre_ffn_norm, v_w_up, v_conv_w, v_conv_b, v_w_down, v_post_ffn_norm, v_w_ple, v_w_ple_gate, v_post_ple_norm):
    args = dict(locals())
    order = ['pre_mix_norm', 'w_in', 'a_ln_g', 'a_ln_b', 'a_spatial_w', 'a_spatial_b', 'a_out', 'b_gk', 'b_gk_bias',
             'b_out_norm', 'b_out', 'w_mix_out', 'post_mix_norm', 'pre_ffn_norm', 'w_up', 'conv_w', 'conv_b', 'w_down',
             'post_ffn_norm', 'w_ple', 'w_ple_gate', 'post_ple_norm']
    assert sorted(BIG + TINY + tuple(n for n, _ in SMALL)) == sorted(order)
    s = x.shape[1]
    xs = x.reshape(s, D)
    ps = p.reshape(s, PLE)
    tgt = loss_target.reshape(s, D)
    t_big = min(1024, s)
    t_mid = min(512, s)
    t_small = min(256, s)
    t_gla = min(256, s)
    mx_, my_, mc_ = _pos()
    me = 2 * mx_ + my_
    sc = jnp.stack([mc_, me]).astype(jnp.int32)
    shard = lambda n: args[n].reshape(args[n].shape[1:])

    mine = {n: shard(n).astype(BF) for n in BIG}
    mine["w_in"] = shard("w_in").T.astype(BF)
    mine.update({n: shard(n) for n in TINY})
    fill = lambda names, gots: {n: lax.dynamic_update_slice(got, mine[n][None], (me, 0, 0)) for n, got in zip(names, gots)}
    first = ("w_in",) + TINY
    full = fill(first, _run_side(gather_side([mine["w_in"]], [mine[n] for n in TINY]), "gather_first"))
    wi = full["w_in"].reshape(4 * 1540, D)
    seg = lambda a, b: wi[a:b]
    qk = [seg(1024 + h * B_HK, 1024 + (h + 1) * B_HK) for h in range(B_H)]
    kk = [seg(1536 + h * B_HK, 1536 + (h + 1) * B_HK) for h in range(B_H)]
    w_z = jnp.concatenate([seg(0, 1024)] + [m_ for h in range(B_H) for m_ in (qk[h], kk[h])]
                          + [seg(2048, 4096), seg(4112, 6160), seg(4096, 4112), jnp.zeros((LANE - B_RANK, D), BF)], axis=0)
    wgk = jnp.pad(_cols_full(full["b_gk"]).astype(BF), ((0, LANE - B_RANK), (0, 0)))
    w_conv = _cols_full(full["conv_w"])
    g1, g2, g3 = pre_mix_norm.reshape(1, D), post_mix_norm.reshape(1, D), pre_ffn_norm.reshape(1, D)
    g4, g5 = post_ffn_norm.reshape(1, D), post_ple_norm.reshape(1, D)
    ln_g, ln_b = a_ln_g.reshape(1, A_W), a_ln_b.reshape(1, A_W)
    w_s = a_spatial_w.reshape(A_G, A_C, A_C)
    w_cat = w_s.transpose(1, 0, 2).reshape(A_C, A_G * A_C)
    w_cat_t = w_s.transpose(2, 0, 1).reshape(A_C, A_G * A_C)
    bias_full = jnp.repeat(a_spatial_b.reshape(A_G, A_C).T, A_GD, axis=1)
    bdm = (jnp.arange(A_G * A_C)[:, None] // A_C == jnp.arange(A_W)[None, :] // A_GD).astype(BF)
    gk_bias = b_gk_bias.reshape(1, B_H * B_HK)
    wn = b_out_norm.reshape(1, B_HV)
    cb = conv_b.reshape(1, 2 * D_FF)
    idx = jnp.arange(t_gla)
    ltri = ((idx[:, None] // B_C == idx[None, :] // B_C) & (idx[None, :] <= idx[:, None])).astype(BF)

    a, z, qk32, zl, *gots = norm_matmul(xs, g1, w_z, D, t_big, "in_proj", nblk=6, f32_blk=1, tail_blk=48,
                                        side=gather_side([mine[n] for n in BIG[1:]], []))
    full.update(fill(BIG[1:], gots))
    w_aout, w_ple_f = _cols_full(full["a_out"]), _cols_full(full["w_ple"])
    w_bout, w_mix, w_pg = (full[n].reshape(D, D) for n in ("b_out", "w_mix_out", "w_ple_gate"))
    w_dn, w_up3 = full["w_down"].reshape(D_FF, D), full["w_up"]
    sa = sgu_fwd(z, ln_g, ln_b, w_cat, bias_full, bdm, t_big)
    ob, o, states = gla_fwd(z, qk32, zl, wgk, gk_bias, wn, ltri, t_gla)
    ya, yb, mp, mx, h1 = mix_fwd(sa, ob, z, xs, w_aout, w_bout, w_mix, g2, t_mid)
    c, up_g, up_v, cg, cv, ff = ffn_up_fwd(h1, g3, w_up3, w_conv, cb, t_mid)
    f, h2, pg, pe, dy, loss = out_fwd(ff, h1, ps, tgt, w_dn, w_pg, w_ple_f, g4, g5, t_mid)

    dh2, dpe, dpg, df, dff, gg5, gg4 = out_bwd(dy, pg, pe, f, g5, g4, w_pg, w_dn, t_mid)
    dup, gcw, gcb, dh1, gg3 = ffn_up_bwd(up_g, up_v, cg, cv, dff, w_conv, _cols_full(w_up3), h1, g3, dh2, t_small)
    grads = {
        "w_up": mm_tn(c, dup, "dw_up")[None],
        "w_down": mm_tn(ff, df, "dw_down").reshape(4, D_FF // 4, D),
        "w_ple": mm_tn(ps, dpe, "dw_ple")[None],
        "w_ple_gate": mm_tn(h2, dpg, "dw_ple_gate").reshape(4, D // 4, D),
    }
    ffn_side = ("w_up", "w_down", "w_ple", "w_ple_gate")
    dmx, dya, dyb, dga, dgb, dsa, dob, gg2, *sibs = mix_bwd(dh1, mx, z, ya, yb, g2, w_mix, w_aout, w_bout, t_mid,
                                                            side=swap_side([grads[n] for n in ffn_side]))
    sib = dict(zip(ffn_side, sibs))
    duv, g_lng, g_lnb, g_wcat, g_bst = sgu_bwd(z, dsa, ln_g, ln_b, w_cat, w_cat_t, bias_full, bdm, t_big)
    g_ws = g_wcat.reshape(A_C, A_G, A_C).transpose(1, 0, 2)
    grads.update({
        "a_out": mm_tn(sa, dya, "dw_a_out")[None],
        "b_out": mm_tn(ob, dyb, "dw_b_out").reshape(4, D // 4, D),
        "w_mix_out": mm_tn(mp, dmx, "dw_mix").reshape(4, D // 4, D),
    })

    def swap(names):
        sib.update(zip(names, _run_side(swap_side([grads[n] for n in names]), "swap_halves_" + names[0])))

    swap(("a_out", "b_out", "w_mix_out"))
    parts = dict(zip(BIG[1:], _blockwise(sc, [add_half_item(grads[n], sib[n]) for n in BIG[1:]], "partials")))
    dqk, dvb, dog, dpre, dlr, g_gkb, g_wn, *gots = gla_bwd(
        z, qk32, zl, o, dob, states, wgk, gk_bias, wn, ltri, ltri.T, t_gla,
        side=exchange_side([parts[n] for n in BIG[1:]]))
    reds = dict(zip(BIG[1:], _blockwise(sc, [sum4_item(parts[n], got_) for n, got_ in zip(BIG[1:], gots)], "sums")))
    segs = [duv, dqk, dvb, dog, dga, dgb, dlr]

    gz = [mm_tn(sg_, a, "dw_in_%d" % k) for k, sg_ in enumerate(segs)]
    gq = [gz[1][h * 256:h * 256 + B_HK] for h in range(B_H)]
    gk = [gz[1][h * 256 + B_HK:(h + 1) * 256] for h in range(B_H)]
    g_in = jnp.concatenate([gz[0]] + gq + gk + [gz[2], gz[3], gz[6][:B_RANK], gz[4], gz[5]], axis=0)
    grads["w_in"] = g_in.reshape(4, 1540, D)
    swap(("w_in",))
    parts["w_in"], = _blockwise(sc, [add_half_item(grads["w_in"], sib["w_in"])], "partial_w_in")
    dx, gg1, got_in, *sib_reds = nt_normbwd(
        segs, w_z, xs, g1, dh1, t_mid, "in_bwd",
        side=_both_sides(exchange_side([parts["w_in"]]), join_side([reds[n] for n in BIG[1:]])))
    sib_red = dict(zip(BIG[1:], sib_reds))

    reds["w_in"], = _blockwise(sc, [sum4_item(parts["w_in"], got_in)], "sum_w_in")
    sib_red["w_in"], = _run_side(join_side([reds["w_in"]]), "join_w_in")
    outs = {}
    for n in BIG[1:]:
        res = _blockwise(sc, [adamw_item(reds[n], sib_red[n], shard(n), shard("m_" + n), shard("v_" + n))],
                         "adamw_" + n)
        outs[n] = [r_.reshape(args[n].shape) for r_ in res]
    res = adamw_cols(sc, reds["w_in"], sib_red["w_in"], shard("w_in").T, shard("m_w_in").T, shard("v_w_in").T,
                     "adamw_w_in")
    outs["w_in"] = [r_.T.reshape(w_in.shape) for r_ in res]

    small_g = {
        "pre_mix_norm": gg1, "a_ln_g": g_lng, "a_ln_b": g_lnb, "a_spatial_w": g_ws, "a_spatial_b": g_bst.T,
        "b_gk_bias": g_gkb, "b_out_norm": g_wn, "post_mix_norm": gg2, "pre_ffn_norm": gg3,
        "conv_b": gcb, "post_ffn_norm": gg4, "post_ple_norm": gg5,
        "b_gk": mm_tn(zl, dpre, "dw_gk")[:B_RANK], "conv_w": gcw,
        "loss": loss,
    }
    red_entries = SMALL + (("b_gk", B_RANK * 512), ("conv_w", 3 * 2 * D_FF), ("loss", 1))
    g_fin = _unpack_small(red_entries, allreduce_small(_pack_small(red_entries, lambda n: small_g[n])))
    g_fin["b_gk"] = lax.dynamic_slice(g_fin["b_gk"].reshape(B_RANK, 512), (0, me * B_HK), (B_RANK, B_HK))
    g_fin["conv_w"] = lax.dynamic_slice(g_fin["conv_w"].reshape(3, 2 * D_FF), (0, me * 1408), (3, 1408))
    upd_entries = SMALL + (("b_gk", B_RANK * B_HK), ("conv_w", 3 * 1408))
    res = adamw_small(*[_pack_small(upd_entries, get) for get in
                        (lambda n: g_fin[n], lambda n: args[n], lambda n: args["m_" + n], lambda n: args["v_" + n])])
    res = [_unpack_small(upd_entries, r_) for r_ in res]
    for n, _ in upd_entries:
        outs[n] = [r_[n].reshape(args[n].shape) for r_ in [g_fin] + res]

    return (g_fin["loss"].reshape(()), dx.reshape(x.shape), *[outs[n][0] for n in order], *[outs[n][1] for n in order],
            *[outs[n][2] for n in order], *[outs[n][3] for n in order])
```

```python
import math

import jax
import jax.numpy as jnp
from jax import lax
from jax.experimental import pallas as pl
from jax.experimental.pallas import tpu as pltpu

F32 = jnp.float32
BF = jnp.bfloat16
SDS = jax.ShapeDtypeStruct
MESH = pl.DeviceIdType.MESH

EPS = 1e-6
D = 1024
A_W = 512
A_G, A_C = 8, 128
A_GD = A_W // A_G
B_H, B_HK, B_HV = 4, 128, 256
B_C = 64
GLA_HPB = 4
B_RANK = 16
D_FF = 2816
PLE = 256
LANE = 128
VMEM_LIMIT = 60 * 1024 * 1024

ADAM_LR, ADAM_B1, ADAM_B2, ADAM_EPS, ADAM_WD, ADAM_STEP = 0.001, 0.9, 0.999, 1e-08, 0.01, 10

_GC = math.sqrt(2.0 / math.pi)
_GA = 0.044715

BIG = ("w_in", "a_out", "b_out", "w_mix_out", "w_up", "w_down", "w_ple", "w_ple_gate")
TINY = ("b_gk", "conv_w")
SMALL = (("pre_mix_norm", 1024), ("a_ln_g", 512), ("a_ln_b", 512), ("a_spatial_w", 131072),
         ("a_spatial_b", 1024), ("b_gk_bias", 512), ("b_out_norm", 256), ("post_mix_norm", 1024),
         ("pre_ffn_norm", 1024), ("conv_b", 5632), ("post_ffn_norm", 1024), ("post_ple_norm", 1024))


def _pc(body, **kw):
    return pl.pallas_call(body, **kw)


def _cp(n):
    return pltpu.CompilerParams(dimension_semantics=("arbitrary",) * n, vmem_limit_bytes=VMEM_LIMIT)


def _const(shape):
    nd = len(shape)
    return pl.BlockSpec(shape, lambda *_: (0,) * nd, pipeline_mode=pl.Buffered(1))


def _acc(shape):
    nd = len(shape)
    return pl.BlockSpec(shape, lambda *_: (0,) * nd)


def _dot(a, b):
    return jnp.dot(a, b, preferred_element_type=F32)


def _dot_nt(a, b):
    return lax.dot_general(a, b, (((1,), (1,)), ((), ())), preferred_element_type=F32)


def _dot_tn(a, b):
    return lax.dot_general(a, b, (((0,), (0,)), ((), ())), preferred_element_type=F32)


def _gelu(x):
    return 0.5 * x * (1.0 + jnp.tanh(_GC * (x + _GA * x * x * x)))


def _gelu_and_grad(x):
    x2 = x * x
    s = 0.5 * jnp.tanh((_GC * x) * (1.0 + _GA * x2)) + 0.5
    g = x * s
    return g, s + g * (1.0 - s) * ((6.0 * _GC * _GA) * x2 + 2.0 * _GC)


def _log_sigmoid(x):
    return jnp.minimum(x, 0.0) - jnp.log(1.0 + jnp.exp(-jnp.abs(x)))


def _rms(x, g):
    return x * lax.rsqrt(jnp.mean(x * x, axis=-1, keepdims=True) + EPS) * g


def _rms_bwd(dy, x, g):
    r = lax.rsqrt(jnp.mean(x * x, axis=-1, keepdims=True) + EPS)
    n = x * r
    dn = dy * g
    dx = r * (dn - n * jnp.mean(dn * n, axis=-1, keepdims=True))
    return dx, jnp.sum(dy * n, axis=0, keepdims=True)


def _ldot3(l, x):
    h = x.astype(BF)
    r = x - h.astype(F32)
    m = r.astype(BF)
    lo = (r - m.astype(F32)).astype(BF)
    return _dot(l, h) + _dot(l, m) + _dot(l, lo)


def _split_side(refs, n_in, n_out, n_scratch, side):
    si, so = (side.n_in, side.n_out) if side else (0, 0)
    cuts = [n_in, si, n_out, so, n_scratch]
    out, at = [], 0
    for c in cuts:
        out.append(refs[at:at + c])
        at += c
    return (*out, refs[at:])


def _side_specs(side):
    return ([_ANY] * side.n_in, [_ANY] * side.n_out, side.out_shapes, side.scratch, side.ins) if side else ([],) * 5


def norm_matmul(x, g, wt, bn, t, name, nblk, f32_blk, tail_blk, side=None):
    s, dm = x.shape
    w_spec = pl.BlockSpec((bn, dm), lambda i, j: (j, 0))
    nt = s // t

    def body(*refs):
        (x_ref, g_ref, w_ref, wl_ref), s_in, outs, s_out, (a_sc,), s_scr = _split_side(refs, 4, 4, 1, side)
        a_ref, z_ref, f32_ref, tail_ref = outs
        i, j = pl.program_id(0), pl.program_id(1)
        if side:
            @pl.when((i == 0) & (j == 0))
            def _():
                side.start(s_in, s_out, *s_scr)

        @pl.when(j == 0)
        def _():
            a = _rms(x_ref[...], g_ref[...]).astype(BF)
            a_sc[...] = a
            a_ref[...] = a
            tail_ref[...] = _dot_nt(a, wl_ref[...]).astype(BF)

        acc = _dot_nt(a_sc[...], w_ref[...])
        z_ref[...] = acc.astype(BF)

        @pl.when(j == f32_blk)
        def _():
            f32_ref[...] = acc
        if side:
            @pl.when((i == nt - 1) & (j == nblk - 1))
            def _():
                side.finish(s_in, s_out, *s_scr)

    si_specs, so_specs, so_shapes, s_scratch, s_ins = _side_specs(side)
    return _pc(
        body, name=name, grid=(nt, nblk),
        in_specs=[pl.BlockSpec((t, dm), lambda i, j: (i, 0)), _const((1, dm)), w_spec,
                  pl.BlockSpec((LANE, dm), lambda i, j: (tail_blk, 0), pipeline_mode=pl.Buffered(1))] + si_specs,
        out_specs=[pl.BlockSpec((t, dm), lambda i, j: (i, 0)), pl.BlockSpec((t, bn), lambda i, j: (i, j)),
                   pl.BlockSpec((t, bn), lambda i, j: (i, 0)), pl.BlockSpec((t, LANE), lambda i, j: (i, 0))] + so_specs,
        out_shape=[SDS((s, dm), BF), SDS((s, nblk * bn), BF), SDS((s, bn), F32), SDS((s, LANE), BF)] + so_shapes,
        scratch_shapes=[pltpu.VMEM((t, dm), BF)] + s_scratch, compiler_params=_cp(2))(x, g, wt, wt, *s_ins)


def _sgu_weights(wc_ref, transposed):
    r = lax.broadcasted_iota(jnp.int32, (A_C, A_G * A_C), 0)
    c = lax.broadcasted_iota(jnp.int32, (A_C, A_G * A_C), 1) & (A_C - 1)
    return jnp.where((r <= c) if transposed else (c <= r), wc_ref[...], 0.0).astype(BF)


def _sgu_spread(xs, bdm):
    return jnp.concatenate([jnp.tile(x, (A_G, 1)) * bdm for x in xs], axis=1)


def _sgu_recompute(v, lng, lnb):
    gv, dgv = _gelu_and_grad(v)
    gv = gv.astype(F32)
    mu = jnp.mean(gv, axis=-1, keepdims=True)
    xc = gv - mu
    rstd = lax.rsqrt(jnp.mean(xc * xc, axis=-1, keepdims=True) + EPS)
    xhat = xc * rstd
    return dgv, rstd, xhat, (xhat * lng + lnb).astype(BF)


def sgu_fwd(z, ln_g, ln_b, w_cat, bias_full, bdm, t):
    s = z.shape[0]
    nch = t // A_C

    def body(u_ref, v_ref, g_ref, b_ref, wc_ref, bias_ref, bdm_ref, sa_ref):
        vns = [_sgu_recompute(v_ref[pl.ds(ci * A_C, A_C), :], g_ref[...], b_ref[...])[3]
               for ci in range(nch)]
        mixed = _dot(_sgu_weights(wc_ref, False), _sgu_spread(vns, bdm_ref[...]))
        for ci in range(nch):
            rows = pl.ds(ci * A_C, A_C)
            s_ = mixed[:, ci * A_W:(ci + 1) * A_W] + bias_ref[...]
            sa_ref[rows, :] = _gelu(u_ref[rows, :]) * s_.astype(BF)

    return _pc(
        body, name="sgu_fwd", grid=(s // t,),
        in_specs=[pl.BlockSpec((t, A_W), lambda i: (i, 0)), pl.BlockSpec((t, A_W), lambda i: (i, 1)),
                  _const((1, A_W)), _const((1, A_W)), _const((A_C, A_G * A_C)), _const((A_C, A_W)),
                  _const((A_G * A_C, A_W))],
        out_specs=pl.BlockSpec((t, A_W), lambda i: (i, 0)),
        out_shape=SDS((s, A_W), BF), compiler_params=_cp(1))(z, z, ln_g, ln_b, w_cat, bias_full, bdm)


def _gla_decays(qk, lr, wgk, bias, l, t):
    nc = t // B_C
    q = qk[:, :B_HK].astype(F32) * (B_HK ** -0.5)
    k = qk[:, B_HK:].astype(F32)
    pre = _dot(lr, wgk) + bias
    la = _log_sigmoid(pre) * (1.0 / 16.0)
    b = _ldot3(l, la)
    b3 = b.reshape(nc, B_C, B_HK)
    bl = jnp.broadcast_to(b3[:, B_C - 1:B_C, :], (nc, B_C, B_HK)).reshape(t, B_HK)
    eb, enb, etb = jnp.exp(b), jnp.exp(-b), jnp.exp(bl - b)
    return pre, b, bl, eb, enb, etb, q * eb, k * enb, k * etb


def gla_fwd(z, qk32, zl, wgk, bias, wn, ltri, t):
    s = z.shape[0]
    nc = t // B_C
    hpb = GLA_HPB
    assert hpb == B_H
    kw, vw = hpb * B_HK, hpb * B_HV

    def body(qk_ref, v_ref, og_ref, lr_ref, wgk_ref, bias_ref, wn_ref, l_ref, ob_ref, o_ref, st_ref, st_sc, o_sc):
        @pl.when(pl.program_id(0) == 0)
        def _():
            st_sc[...] = jnp.zeros((B_H, B_HV, B_HK), F32)

        lr, l = lr_ref[...], l_ref[...]
        for hh in range(hpb):
            h = hh
            cv, ck = slice(hh * B_HV, (hh + 1) * B_HV), slice(hh * B_HK, (hh + 1) * B_HK)
            _, _, bl, _, _, _, qd, ki, kt = _gla_decays(qk_ref[:, cv], lr, wgk_ref[:, ck], bias_ref[:, ck], l, t)
            qd, ki, kt = qd.astype(BF), ki.astype(BF), kt.astype(BF)
            vb = v_ref[:, cv]
            sc = jnp.where(l > 0, _dot_nt(qd, ki), 0.0).astype(BF)
            o_sc[hh] = _dot(sc, vb)
            for n in range(nc):
                rows = slice(n * B_C, (n + 1) * B_C)
                st = st_sc[h]
                stb = st.astype(BF)
                st_ref[n, hh] = stb
                o_sc[hh, rows, :] += _dot_nt(qd[rows], stb)
                st_sc[h] = st * jnp.exp(bl[n * B_C:n * B_C + 1, :]) + _dot_tn(vb[rows], kt[rows])
            ob = o_sc[hh].astype(BF)
            o_ref[:, cv] = ob
            og = og_ref[:, cv].astype(F32)
            ob_ref[:, cv] = (_rms(ob.astype(F32), wn_ref[...]) * og * jax.nn.sigmoid(og)).astype(BF)

    vo, go = 2048 // vw, 3072 // vw
    return _pc(
        body, name="gla_fwd", grid=(s // t, B_H // hpb),
        in_specs=[pl.BlockSpec((t, vw), lambda i, g: (i, g)), pl.BlockSpec((t, vw), lambda i, g: (i, vo + g)),
                  pl.BlockSpec((t, vw), lambda i, g: (i, go + g)), pl.BlockSpec((t, LANE), lambda i, g: (i, 0)),
                  pl.BlockSpec((LANE, kw), lambda i, g: (0, g)), pl.BlockSpec((1, kw), lambda i, g: (0, g)),
                  _const((1, B_HV)), _const((t, t))],
        out_specs=[pl.BlockSpec((t, vw), lambda i, g: (i, g)), pl.BlockSpec((t, vw), lambda i, g: (i, g)),
                   pl.BlockSpec((nc, hpb, B_HV, B_HK), lambda i, g: (i, g, 0, 0))],
        out_shape=[SDS((s, D), BF), SDS((s, D), BF), SDS((s // B_C, B_H, B_HV, B_HK), BF)],
        scratch_shapes=[pltpu.VMEM((B_H, B_HV, B_HK), F32), pltpu.VMEM((hpb, t, B_HV), F32)],
        compiler_params=_cp(2))(qk32, z, z, zl, wgk, bias, wn, ltri)


def mix_fwd(sa, ob, z, x, a_out, b_out, w_mix, g2, t):
    s = x.shape[0]

    def body(sa_ref, ob_ref, ga_ref, gb_ref, x_ref, ao_ref, bo_ref, wm_ref, g2_ref,
             ya_ref, yb_ref, mp_ref, mx_ref, h1_ref):
        ya = _dot(sa_ref[...], ao_ref[...]).astype(BF)
        yb = _dot(ob_ref[...], bo_ref[...]).astype(BF)
        ya_ref[...] = ya
        yb_ref[...] = yb
        mp = (jax.nn.sigmoid(ga_ref[...].astype(F32)) * ya.astype(F32)
              + jax.nn.sigmoid(gb_ref[...].astype(F32)) * yb.astype(F32)).astype(BF)
        mp_ref[...] = mp
        mx = _dot(mp, wm_ref[...]).astype(BF)
        mx_ref[...] = mx
        h1_ref[...] = x_ref[...] + _rms(mx.astype(F32), g2_ref[...])

    row = lambda w: pl.BlockSpec((t, w), lambda i: (i, 0))
    return _pc(
        body, name="mix_fwd", grid=(s // t,),
        in_specs=[row(A_W), row(D), pl.BlockSpec((t, D), lambda i: (i, 4)), pl.BlockSpec((t, D), lambda i: (i, 5)),
                  row(D), _const((A_W, D)), _const((D, D)), _const((D, D)), _const((1, D))],
        out_specs=[row(D)] * 5,
        out_shape=[SDS((s, D), BF)] * 4 + [SDS((s, D), F32)],
        compiler_params=_cp(1))(sa, ob, z, z, x, a_out, b_out, w_mix, g2)


def ffn_up_fwd(h1, g3, w_up3, conv_w, conv_b, t):
    s = h1.shape[0]
    bn = w_up3.shape[2]

    def body(x_ref, g_ref, wg_ref, wv_ref, cwg_ref, cwv_ref, cbg_ref, cbv_ref,
             c_ref, ug_ref, uv_ref, cg_ref, cv_ref, ff_ref, c_sc, carry):
        i, j = pl.program_id(0), pl.program_id(1)

        @pl.when(j == 0)
        def _():
            c = _rms(x_ref[...], g_ref[...]).astype(BF)
            c_sc[...] = c
            c_ref[...] = c

        @pl.when(i == 0)
        def _():
            carry[j] = jnp.zeros((2, 8, bn), F32)

        def branch(k, w_ref, cw_ref, cb_ref, u_ref, o_ref):
            ub = _dot(c_sc[...], w_ref[...]).astype(BF)
            u_ref[...] = ub
            u = ub.astype(F32)
            ext = jnp.concatenate([carry[j, k], u], axis=0)
            carry[j, k] = u[t - 8:]
            w = cw_ref[...]
            cc = (cb_ref[...] + w[0:1] * pltpu.roll(ext, 2, 0) + w[1:2] * pltpu.roll(ext, 1, 0) + w[2:3] * ext)[8:]
            cc = cc.astype(BF)
            o_ref[...] = cc
            return cc

        g = _gelu(branch(0, wg_ref, cwg_ref, cbg_ref, ug_ref, cg_ref))
        ff_ref[...] = g * branch(1, wv_ref, cwv_ref, cbv_ref, uv_ref, cv_ref)

    col = lambda rows, off: pl.BlockSpec((rows, bn), lambda i, j: (0, j + off))
    out = pl.BlockSpec((t, bn), lambda i, j: (i, j))
    return _pc(
        body, name="ffn_up_fwd", grid=(s // t, 2),
        in_specs=[pl.BlockSpec((t, D), lambda i, j: (i, 0)), _const((1, D)),
                  pl.BlockSpec((None, D, bn), lambda i, j: (j, 0, 0)), pl.BlockSpec((None, D, bn), lambda i, j: (j + 2, 0, 0)),
                  col(3, 0), col(3, 2), col(1, 0), col(1, 2)],
        out_specs=[pl.BlockSpec((t, D), lambda i, j: (i, 0))] + [out] * 5,
        out_shape=[SDS((s, D), BF)] + [SDS((s, D_FF), BF)] * 5,
        scratch_shapes=[pltpu.VMEM((t, D), BF), pltpu.VMEM((2, 2, 8, bn), F32)],
        compiler_params=_cp(2))(h1, g3, w_up3, w_up3, conv_w, conv_w, conv_b, conv_b)


def ffn_up_bwd(ug, uv, cg, cv, dff, conv_w, w_up, h1, g3, dh2, t):
    s = h1.shape[0]
    nt = s // t
    hb = t // 8
    bn = 1408
    r = t + 8

    def body(ug_ref, uv_ref, cg_ref, cv_ref, cag_ref, cav_ref, d_ref, da_ref, cw_ref, w_ref, x_ref, g_ref, dres_ref,
             du_ref, gw_ref, gb_ref, dx_ref, gg_ref):
        i = pl.program_id(0)

        @pl.when(i == 0)
        def _():
            gw_ref[...] = jnp.zeros((3, 2 * D_FF), F32)
            gb_ref[...] = jnp.zeros((1, 2 * D_FF), F32)
            gg_ref[...] = jnp.zeros((1, D), F32)

        more = (i < nt - 1).astype(BF)

        def gate(c_g, c_v, d_):
            gl, dgl = _gelu_and_grad(c_g)
            return (d_ * c_v * dgl).astype(F32), (d_ * gl).astype(F32)

        def back(dc, dc_next, u_ref, cols, off):
            w = cw_ref[:, off:off + bn]
            d_ext = jnp.concatenate([dc, dc_next], axis=0)
            d1, d2 = pltpu.roll(d_ext, r - 1, 0)[:t], pltpu.roll(d_ext, r - 2, 0)[:t]
            du_ref[:, off:off + bn] = (w[2:3] * dc + w[1:2] * d1 + w[0:1] * d2).astype(BF)
            u = u_ref[:, cols].astype(F32)
            gw_ref[0:1, off:off + bn] += jnp.sum(d2 * u, axis=0, keepdims=True)
            gw_ref[1:2, off:off + bn] += jnp.sum(d1 * u, axis=0, keepdims=True)
            gw_ref[2:3, off:off + bn] += jnp.sum(dc * u, axis=0, keepdims=True)
            gb_ref[:, off:off + bn] += jnp.sum(dc, axis=0, keepdims=True)

        for kb in range(D_FF // bn):
            cols = slice(kb * bn, (kb + 1) * bn)
            dg, dv = gate(cg_ref[:, cols], cv_ref[:, cols], d_ref[:, cols])
            dg_n, dv_n = gate(cag_ref[:, cols], cav_ref[:, cols], da_ref[:, cols] * more)
            back(dg, dg_n, ug_ref, cols, kb * bn)
            back(dv, dv_n, uv_ref, cols, D_FF + kb * bn)

        acc = _dot_nt(du_ref[...], w_ref[...])
        dxn, dg3 = _rms_bwd(acc, x_ref[...], g_ref[...])
        dx_ref[...] = dres_ref[...] + dxn
        gg_ref[...] += dg3

    tile = lambda width: pl.BlockSpec((t, width), lambda i: (i, 0))
    after = pl.BlockSpec((8, D_FF), lambda i: (jnp.minimum((i + 1) * hb, nt * hb - 1), 0))
    return _pc(
        body, name="ffn_up_bwd", grid=(nt,),
        in_specs=[tile(D_FF)] * 4 + [after, after, tile(D_FF), after, _const((3, 2 * D_FF)), _const((D, 2 * D_FF)),
                                     tile(D), _const((1, D)), tile(D)],
        out_specs=[tile(2 * D_FF), _acc((3, 2 * D_FF)), _acc((1, 2 * D_FF)), tile(D), _acc((1, D))],
        out_shape=[SDS((s, 2 * D_FF), BF), SDS((3, 2 * D_FF), F32), SDS((1, 2 * D_FF), F32),
                   SDS((s, D), F32), SDS((1, D), F32)],
        compiler_params=_cp(1))(ug, uv, cg, cv, cg, cv, dff, dff, conv_w, w_up, h1, g3, dh2)


def out_fwd(ff, h1, p, tgt, w_down, w_pg, w_ple, g4, g5, t):
    s = h1.shape[0]

    def body(ff_ref, h1_ref, p_ref, t_ref, wd_ref, wpg_ref, wpl_ref, g4_ref, g5_ref,
             f_ref, h2_ref, pg_ref, pe_ref, dy_ref, loss_ref):
        @pl.when(pl.program_id(0) == 0)
        def _():
            loss_ref[...] = jnp.zeros((1, 1), F32)

        f = _dot(ff_ref[...], wd_ref[...]).astype(BF)
        f_ref[...] = f
        h2 = h1_ref[...] + _rms(f.astype(F32), g4_ref[...])
        h2b = h2.astype(BF)
        h2_ref[...] = h2b
        pg = _dot(h2b, wpg_ref[...]).astype(BF)
        pe = _dot(p_ref[...].astype(BF), wpl_ref[...]).astype(BF)
        pg_ref[...] = pg
        pe_ref[...] = pe
        y = h2 + _rms(jax.nn.sigmoid(pg.astype(F32)) * pe.astype(F32), g5_ref[...])
        err = y - t_ref[...]
        dy_ref[...] = err * (1.0 / D)
        loss_ref[...] += (0.5 / D) * jnp.sum(err * err)

    row = lambda w: pl.BlockSpec((t, w), lambda i: (i, 0))
    return _pc(
        body, name="out_fwd", grid=(s // t,),
        in_specs=[row(D_FF), row(D), row(PLE), row(D), _const((D_FF, D)), _const((D, D)), _const((PLE, D)),
                  _const((1, D)), _const((1, D))],
        out_specs=[row(D)] * 5 + [_acc((1, 1))],
        out_shape=[SDS((s, D), BF)] * 4 + [SDS((s, D), F32), SDS((1, 1), F32)],
        compiler_params=_cp(1))(ff, h1, p, tgt, w_down, w_pg, w_ple, g4, g5)


def out_bwd(dy, pg, pe, f, g5, g4, w_pg, w_down, t):
    s = dy.shape[0]

    def body(dy_ref, pg_ref, pe_ref, f_ref, g5_ref, g4_ref, wpg_ref, wd_ref,
             dh2_ref, dpe_ref, dpg_ref, df_ref, dff_ref, gg5_ref, gg4_ref):
        @pl.when(pl.program_id(0) == 0)
        def _():
            gg5_ref[...] = jnp.zeros((1, D), F32)
            gg4_ref[...] = jnp.zeros((1, D), F32)

        dy_ = dy_ref[...]
        pg_ = pg_ref[...].astype(F32)
        pe_ = pe_ref[...].astype(F32)
        sg = jax.nn.sigmoid(pg_)
        dple, dg5 = _rms_bwd(dy_, sg * pe_, g5_ref[...])
        gg5_ref[...] += dg5
        dpe_ref[...] = (dple * sg).astype(BF)
        dpg = (dple * pe_ * sg * (1.0 - sg)).astype(BF)
        dpg_ref[...] = dpg
        dh2 = dy_ + _dot_nt(dpg, wpg_ref[...])
        dh2_ref[...] = dh2
        df, dg4 = _rms_bwd(dh2, f_ref[...].astype(F32), g4_ref[...])
        gg4_ref[...] += dg4
        dfb = df.astype(BF)
        df_ref[...] = dfb
        dff_ref[...] = _dot_nt(dfb, wd_ref[...]).astype(BF)

    row = lambda w: pl.BlockSpec((t, w), lambda i: (i, 0))
    return _pc(
        body, name="out_bwd", grid=(s // t,),
        in_specs=[row(D), row(D), row(D), row(D), _const((1, D)), _const((1, D)), _const((D, D)), _const((D_FF, D))],
        out_specs=[row(D), row(D), row(D), row(D), row(D_FF), _acc((1, D)), _acc((1, D))],
        out_shape=[SDS((s, D), F32), SDS((s, D), BF), SDS((s, D), BF), SDS((s, D), BF), SDS((s, D_FF), BF),
                   SDS((1, D), F32), SDS((1, D), F32)],
        compiler_params=_cp(1))(dy, pg, pe, f, g5, g4, w_pg, w_down)


def nt_normbwd(dys, w, xin, gain, dres, t, name, side=None):
    s = xin.shape[0]
    nt = s // t
    np_ = len(dys)

    def body(*refs):
        ins_, s_in, (dx_ref, gg_ref), s_out, _, s_scr = _split_side(refs, np_ + 4, 2, 0, side)
        dy_refs = ins_[:np_]
        w_ref, x_ref, g_ref, dres_ref = ins_[np_:]
        i = pl.program_id(0)

        @pl.when(i == 0)
        def _():
            gg_ref[...] = jnp.zeros((1, D), F32)
            if side:
                side.start(s_in, s_out, *s_scr)

        acc = _dot(jnp.concatenate([r_[...] for r_ in dy_refs], axis=1), w_ref[...])
        dxn, dg = _rms_bwd(acc, x_ref[...], g_ref[...])
        dx_ref[...] = dres_ref[...] + dxn
        gg_ref[...] += dg
        if side:
            @pl.when(i == nt - 1)
            def _():
                side.finish(s_in, s_out, *s_scr)

    row = lambda width: pl.BlockSpec((t, width), lambda i: (i, 0))
    si_specs, so_specs, so_shapes, s_scratch, s_ins = _side_specs(side)
    assert sum(dy.shape[1] for dy in dys) == w.shape[0]
    return _pc(
        body, name=name, grid=(nt,),
        in_specs=[row(dy.shape[1]) for dy in dys] + [_const(w.shape), row(D), _const((1, D)), row(D)] + si_specs,
        out_specs=[row(D), _acc((1, D))] + so_specs,
        out_shape=[SDS((s, D), F32), SDS((1, D), F32)] + so_shapes, scratch_shapes=s_scratch,
        compiler_params=_cp(1))(*dys, w, xin, gain, dres, *s_ins)


def mix_bwd(dh1, mx, z, ya, yb, g2, w_mix, a_out, b_out, t, side=None):
    s = dh1.shape[0]
    nt = s // t

    def body(*refs):
        ins_, s_in, outs_, s_out, _, s_scr = _split_side(refs, 10, 8, 0, side)
        dh_ref, mx_ref, ga_ref, gb_ref, ya_ref, yb_ref, g2_ref, wm_ref, ao_ref, bo_ref = ins_
        dmx_ref, dya_ref, dyb_ref, dga_ref, dgb_ref, dsa_ref, dob_ref, gg2_ref = outs_

        @pl.when(pl.program_id(0) == 0)
        def _():
            gg2_ref[...] = jnp.zeros((1, D), F32)
            if side:
                side.start(s_in, s_out, *s_scr)

        dmx, dg2 = _rms_bwd(dh_ref[...], mx_ref[...].astype(F32), g2_ref[...])
        gg2_ref[...] += dg2
        dmxb = dmx.astype(BF)
        dmx_ref[...] = dmxb
        dmp = _dot_nt(dmxb, wm_ref[...]).astype(BF)

        def gate(g_ref, y_ref, dy_ref, dg_ref, w_ref, dz_ref):
            sg = jax.nn.sigmoid(g_ref[...])
            dyb_ = dmp * sg
            dy_ref[...] = dyb_
            dg_ref[...] = dyb_ * y_ref[...] * (1.0 - sg)
            dz_ref[...] = _dot_nt(dyb_, w_ref[...]).astype(BF)

        gate(ga_ref, ya_ref, dya_ref, dga_ref, ao_ref, dsa_ref)
        gate(gb_ref, yb_ref, dyb_ref, dgb_ref, bo_ref, dob_ref)
        if side:
            @pl.when(pl.program_id(0) == nt - 1)
            def _():
                side.finish(s_in, s_out, *s_scr)

    row = lambda w: pl.BlockSpec((t, w), lambda i: (i, 0))
    si_specs, so_specs, so_shapes, s_scratch, s_ins = _side_specs(side)
    return _pc(
        body, name="mix_bwd", grid=(nt,),
        in_specs=[row(D), row(D), pl.BlockSpec((t, D), lambda i: (i, 4)), pl.BlockSpec((t, D), lambda i: (i, 5)),
                  row(D), row(D), _const((1, D)), _const((D, D)), _const((A_W, D)), _const((D, D))] + si_specs,
        out_specs=[row(D)] * 5 + [row(A_W), row(D), _acc((1, D))] + so_specs,
        out_shape=[SDS((s, D), BF)] * 5 + [SDS((s, A_W), BF), SDS((s, D), BF), SDS((1, D), F32)] + so_shapes,
        scratch_shapes=s_scratch,
        compiler_params=_cp(1))(dh1, mx, z, z, ya, yb, g2, w_mix, a_out, b_out, *s_ins)


def sgu_bwd(z, dsa, ln_g, ln_b, w_cat, w_cat_t, bias_full, bdm, t):
    s = z.shape[0]
    nt = s // t
    nch = t // A_C

    def body(u_ref, v_ref, dsa_ref, g_ref, b_ref, wc_ref, wct_ref, bias_ref, bdm_ref,
             duv_ref, glg_ref, glb_ref, gws_ref, gbs_ref, ds_acc):
        i = pl.program_id(0)

        @pl.when(i == 0)
        def _():
            glg_ref[...] = jnp.zeros((1, A_W), F32)
            glb_ref[...] = jnp.zeros((1, A_W), F32)
            gws_ref[...] = jnp.zeros((A_C, A_G * A_C), F32)
            ds_acc[...] = jnp.zeros((A_C, A_W), F32)

        lng, bdm_ = g_ref[...], bdm_ref[...]
        rec = [_sgu_recompute(v_ref[pl.ds(ci * A_C, A_C), :], lng, b_ref[...]) for ci in range(nch)]
        spread_vn = _sgu_spread([r_[3] for r_ in rec], bdm_)
        mixed = _dot(_sgu_weights(wc_ref, False), spread_vn)
        dsas, dss, dgus = [], [], []
        for ci in range(nch):
            rows = pl.ds(ci * A_C, A_C)
            gu, dgu = _gelu_and_grad(u_ref[rows, :])
            dsa_ = dsa_ref[rows, :]
            ds = dsa_ * gu
            ds_acc[...] += ds.astype(F32)
            dsas.append(dsa_)
            dgus.append(dgu)
            dss.append(ds)
        r = lax.broadcasted_iota(jnp.int32, (A_C, A_G * A_C), 0)
        c = lax.broadcasted_iota(jnp.int32, (A_C, A_G * A_C), 1) & (A_C - 1)
        gws_ref[...] += jnp.where(c <= r, _dot_nt(jnp.concatenate(dss, axis=1), spread_vn), 0.0)
        dvns = _dot(_sgu_weights(wct_ref, True), _sgu_spread(dss, bdm_))
        for ci in range(nch):
            rows = pl.ds(ci * A_C, A_C)
            dgv, rstd, xhat, _ = rec[ci]
            dvn = dvns[:, ci * A_W:(ci + 1) * A_W]
            glb_ref[...] += jnp.sum(dvn, axis=0, keepdims=True)
            glg_ref[...] += jnp.sum(dvn * xhat, axis=0, keepdims=True)
            dxh = dvn * lng
            dgv_ = rstd * (dxh - jnp.mean(dxh, axis=-1, keepdims=True)
                           - xhat * jnp.mean(dxh * xhat, axis=-1, keepdims=True))
            s_ = mixed[:, ci * A_W:(ci + 1) * A_W] + bias_ref[...]
            duv_ref[rows, :A_W] = dsas[ci] * dgus[ci] * s_.astype(BF)
            duv_ref[rows, A_W:] = (dgv_ * dgv).astype(BF)

        @pl.when(i == nt - 1)
        def _():
            acc = ds_acc[...]
            for g in range(A_G):
                gbs_ref[:, g:g + 1] = jnp.sum(acc[:, g * A_GD:(g + 1) * A_GD], axis=1, keepdims=True)

    return _pc(
        body, name="sgu_bwd", grid=(nt,),
        in_specs=[pl.BlockSpec((t, A_W), lambda i: (i, 0)), pl.BlockSpec((t, A_W), lambda i: (i, 1)),
                  pl.BlockSpec((t, A_W), lambda i: (i, 0)),
                  _const((1, A_W)), _const((1, A_W)), _const((A_C, A_G * A_C)), _const((A_C, A_G * A_C)),
                  _const((A_C, A_W)), _const((A_G * A_C, A_W))],
        out_specs=[pl.BlockSpec((t, D), lambda i: (i, 0)), _acc((1, A_W)), _acc((1, A_W)),
                   _acc((A_C, A_G * A_C)), _acc((A_C, A_G))],
        out_shape=[SDS((s, D), BF), SDS((1, A_W), F32), SDS((1, A_W), F32), SDS((A_C, A_G * A_C), F32),
                   SDS((A_C, A_G), F32)],
        scratch_shapes=[pltpu.VMEM((A_C, A_W), F32)],
        compiler_params=_cp(1))(z, z, dsa, ln_g, ln_b, w_cat, w_cat_t, bias_full, bdm)


def gla_bwd(z, qk32, zl, o, dob, states, wgk, bias, wn, ltri, ltri_t, t, side=None):
    s = z.shape[0]
    nt = s // t
    nc = t // B_C
    hpb = GLA_HPB
    assert hpb == B_H
    kw, vw = hpb * B_HK, hpb * B_HV

    def body(*refs):
        ins_, s_in, outs_, s_out, scr_, s_scr = _split_side(refs, 12, 7, 5, side)
        qk_ref, v_ref, og_ref, lr_ref, o_ref, dob_ref, st_ref, wgk_ref, bias_ref, wn_ref, l_ref, lt_ref = ins_
        dqk_ref, dv_ref, dog_ref, dpre_ref, dlr_ref, gbias_ref, gwn_ref = outs_
        dst_sc, dv_sc, dqd_sc, dkt_sc, ddec_sc = scr_
        i = pl.program_id(0)
        g = pl.program_id(1)

        @pl.when((i == 0) & (g == 0))
        def _():
            gbias_ref[...] = jnp.zeros((B_H, 1, B_HK), F32)
            gwn_ref[...] = jnp.zeros((1, B_HV), F32)
            if side:
                side.start(s_in, s_out, *s_scr)

        @pl.when(i == 0)
        def _():
            dst_sc[...] = jnp.zeros((B_H, B_HV, B_HK), F32)

        lr, l, lt = lr_ref[...], l_ref[...], lt_ref[...]
        keep, keep_t = l > 0, lt > 0
        wn_ = wn_ref[...]
        last = lax.broadcasted_iota(jnp.int32, (nc, B_C, B_HK), 1) == B_C - 1
        for hh in range(hpb):
            h = hh
            cv, ck = slice(hh * B_HV, (hh + 1) * B_HV), slice(hh * B_HK, (hh + 1) * B_HK)
            pre, b, bl, eb, enb, etb, qd, ki, kt = _gla_decays(qk_ref[:, cv], lr, wgk_ref[:, ck], bias_ref[:, ck], l, t)
            qdb, kib, ktb = qd.astype(BF), ki.astype(BF), kt.astype(BF)
            vb = v_ref[:, cv]
            o_ = o_ref[:, cv].astype(F32)
            og = og_ref[:, cv].astype(F32)
            sog = jax.nn.sigmoid(og)
            dob_ = dob_ref[:, cv].astype(F32)
            don = dob_ * og * sog
            do, dwn = _rms_bwd(don, o_, wn_)
            gwn_ref[...] += dwn
            dog_ref[:, cv] = (dob_ * _rms(o_, wn_) * sog * (1.0 + og * (1.0 - sog))).astype(BF)
            dob16 = do.astype(BF)
            sc_t = jnp.where(keep_t, _dot_nt(kib, qdb), 0.0).astype(BF)
            dsc = jnp.where(keep, _dot_nt(dob16, vb), 0.0).astype(BF)
            dsc_t = jnp.where(keep_t, _dot_nt(vb, dob16), 0.0).astype(BF)
            dv_sc[hh] = _dot(sc_t, dob16)
            dqd_sc[hh] = _dot(dsc, kib)
            dki = _dot(dsc_t, qdb)
            for n in reversed(range(nc)):
                rows = slice(n * B_C, (n + 1) * B_C)
                dst = dst_sc[h]
                dstb = dst.astype(BF)
                stp = st_ref[n, hh]
                dv_sc[hh, rows, :] += _dot_nt(ktb[rows], dstb)
                dkt_sc[hh, rows, :] = _dot(vb[rows], dstb)
                dqd_sc[hh, rows, :] += _dot(dob16[rows], stp)
                dec = jnp.exp(bl[n * B_C:n * B_C + 1, :])
                ddec_sc[hh, n] = jnp.sum(dst * stp.astype(F32), axis=0, keepdims=True) * dec
                dst_sc[h] = dst * dec + _dot_tn(dob16[rows], qdb[rows])
            dqd, dkt = dqd_sc[hh], dkt_sc[hh]
            dv_ref[:, cv] = dv_sc[hh].astype(BF)
            dqk_ref[:, hh * B_HV:hh * B_HV + B_HK] = (dqd * eb * (B_HK ** -0.5)).astype(BF)
            dqk_ref[:, hh * B_HV + B_HK:(hh + 1) * B_HV] = (dki * enb + dkt * etb).astype(BF)
            dktkt = dkt * kt
            db3 = (dqd * qd - dki * ki - dktkt).reshape(nc, B_C, B_HK)
            dbl = jnp.sum(dktkt.reshape(nc, B_C, B_HK), axis=1, keepdims=True) + ddec_sc[hh]
            db = (db3 + jnp.where(last, dbl, 0.0)).reshape(t, B_HK)
            dla = _ldot3(lt, db)
            dpre = dla * (1.0 / 16.0) * (1.0 - jax.nn.sigmoid(pre))
            dpreb = dpre.astype(BF)
            dpre_ref[:, ck] = dpreb
            gbias_ref[h] += jnp.sum(dpre, axis=0, keepdims=True)
            dlr_h = _dot_nt(dpreb, wgk_ref[:, ck])
            dlr = dlr_h if hh == 0 else dlr + dlr_h
        dlr_ref[...] = dlr.astype(BF)
        if side:
            @pl.when((i == nt - 1) & (g == B_H // hpb - 1))
            def _():
                side.finish(s_in, s_out, *s_scr)

    rv = lambda i: nt - 1 - i
    si_specs, so_specs, so_shapes, s_scratch, s_ins = _side_specs(side)
    vo, go = 2048 // vw, 3072 // vw
    tile = lambda off: pl.BlockSpec((t, vw), lambda i, g: (rv(i), off + g))
    return _pc(
        body, name="gla_bwd", grid=(nt, B_H // hpb),
        in_specs=[tile(0), tile(vo), tile(go), pl.BlockSpec((t, LANE), lambda i, g: (rv(i), 0)), tile(0), tile(0),
                  pl.BlockSpec((nc, hpb, B_HV, B_HK), lambda i, g: (rv(i), g, 0, 0)),
                  pl.BlockSpec((LANE, kw), lambda i, g: (0, g)), pl.BlockSpec((1, kw), lambda i, g: (0, g)),
                  _const((1, B_HV)), _const((t, t)), _const((t, t))] + si_specs,
        out_specs=[tile(0), tile(0), tile(0), pl.BlockSpec((t, kw), lambda i, g: (rv(i), g)),
                   pl.BlockSpec((t, LANE), lambda i, g: (rv(i), 0)), _acc((B_H, 1, B_HK)), _acc((1, B_HV))] + so_specs,
        out_shape=[SDS((s, D), BF), SDS((s, D), BF), SDS((s, D), BF), SDS((s, B_H * B_HK), BF), SDS((s, LANE), BF),
                   SDS((B_H, 1, B_HK), F32), SDS((1, B_HV), F32)] + so_shapes,
        scratch_shapes=[pltpu.VMEM((B_H, B_HV, B_HK), F32), pltpu.VMEM((hpb, t, B_HV), F32),
                        pltpu.VMEM((hpb, t, B_HK), F32), pltpu.VMEM((hpb, t, B_HK), F32),
                        pltpu.VMEM((hpb, nc, 1, B_HK), F32)] + s_scratch,
        compiler_params=_cp(2))(qk32, z, z, zl, o, dob, states, wgk, bias, wn, ltri, ltri_t, *s_ins)


def mm_tn(a, b, name, tk=2048):
    s, m = a.shape
    n = b.shape[1]
    bn = next(c for c in (1024, 1408, 512, 256, 128) if n % c == 0 and m * c * 4 <= 6 * 1024 * 1024)
    tk = min(tk, s)
    nk = s // tk

    def body(a_ref, b_ref, o_ref, acc):
        k = pl.program_id(1)

        @pl.when(k == 0)
        def _():
            acc[...] = jnp.zeros((m, bn), F32)

        acc[...] += _dot_tn(a_ref[...].astype(BF), b_ref[...])

        @pl.when(k == nk - 1)
        def _():
            o_ref[...] = acc[...].astype(BF)

    return _pc(
        body, name=name, grid=(n // bn, nk),
        in_specs=[pl.BlockSpec((tk, m), lambda j, k: (k, 0)), pl.BlockSpec((tk, bn), lambda j, k: (k, j))],
        out_specs=pl.BlockSpec((m, bn), lambda j, k: (0, j)),
        out_shape=SDS((m, n), BF), scratch_shapes=[pltpu.VMEM((m, bn), F32)], compiler_params=_cp(2))(a, b)


def _adamw(w, g, m, v):
    m = ADAM_B1 * m + (1.0 - ADAM_B1) * g
    v = ADAM_B2 * v + (1.0 - ADAM_B2) * (g * g)
    m_hat = m / (1.0 - ADAM_B1 ** ADAM_STEP)
    v_hat = v / (1.0 - ADAM_B2 ** ADAM_STEP)
    return -ADAM_LR * (m_hat / (jnp.sqrt(v_hat) + ADAM_EPS) + ADAM_WD * w), m, v


def _half_rows(rows):
    rh = rows // 2
    return rh, max(b for b in range(16, 257, 16) if rh % b == 0)


def _pc_sp(body, grid, in_specs, out_specs, out_shape, name):
    gs = pltpu.PrefetchScalarGridSpec(num_scalar_prefetch=1, grid=grid, in_specs=in_specs, out_specs=out_specs)
    return _pc(body, grid_spec=gs, out_shape=out_shape, name=name, compiler_params=_cp(len(grid)))


def adamw_item(own, sib, w, m, v):
    rows, cols = w.shape
    rh, br = _half_rows(rows)
    nbk = rh // br

    def fn(ins, outs, b, sc_ref):
        own_ref, sib_ref, w_ref, m_ref, v_ref = ins
        g_ = jnp.where(b // nbk == sc_ref[0], own_ref[...], sib_ref[...])
        outs[0][...] = g_
        outs[1][...], outs[2][...], outs[3][...] = _adamw(w_ref[...], g_, m_ref[...], v_ref[...])

    blk = (br, cols)
    mine = lambda b, sc_: (jnp.clip(b - sc_[0] * nbk, 0, nbk - 1), 0)
    theirs = lambda b, sc_: (jnp.clip(b - (1 - sc_[0]) * nbk, 0, nbk - 1), 0)
    each = lambda b, sc_: (b, 0)
    return ([(own, blk, mine), (sib, blk, theirs), (w, blk, each), (m, blk, each), (v, blk, each)],
            [(SDS((rows, cols), F32), blk, each)] * 4, 2 * nbk, fn)


def adamw_cols(sc, own, sib, w, m, v, name, cb=256):
    rows, cols = w.shape
    nk = cols // 2 // cb

    def body(sc_ref, own_ref, sib_ref, w_ref, m_ref, v_ref, go_ref, d_ref, mo_ref, vo_ref):
        g_ = jnp.where(pl.program_id(0) == sc_ref[0], own_ref[...], sib_ref[...])
        go_ref[...] = g_
        d_ref[...], mo_ref[...], vo_ref[...] = _adamw(w_ref[...], g_, m_ref[...], v_ref[...])

    mine = pl.BlockSpec((rows, cb), lambda h, k, sc_: (0, jnp.clip(k + (h - sc_[0]) * nk, 0, nk - 1)))
    theirs = pl.BlockSpec((rows, cb), lambda h, k, sc_: (0, jnp.clip(k + (h - 1 + sc_[0]) * nk, 0, nk - 1)))
    blk = pl.BlockSpec((rows, cb), lambda h, k, sc_: (0, h * nk + k))
    return _pc_sp(body, (2, nk), [mine, theirs, blk, blk, blk], [blk] * 4, [SDS((rows, cols), F32)] * 4,
                  name)(sc, own, sib, w, m, v)


def adamw_small(g, w, m, v):
    def body(g_ref, w_ref, m_ref, v_ref, d_ref, mo_ref, vo_ref):
        d_ref[...], mo_ref[...], vo_ref[...] = _adamw(w_ref[...], g_ref[...], m_ref[...], v_ref[...])

    vm = pl.BlockSpec(memory_space=pltpu.VMEM)
    return _pc(body, name="adamw_small", in_specs=[vm] * 4, out_specs=[vm] * 3, out_shape=[SDS(g.shape, F32)] * 3,
               compiler_params=pltpu.CompilerParams(vmem_limit_bytes=VMEM_LIMIT))(g, w, m, v)


def _pos():
    return lax.axis_index("x"), lax.axis_index("y"), lax.axis_index("c")


def _other_chips(x, y):
    return [(1 - x, y), (x, 1 - y), (1 - x, 1 - y)]


_ANY = pl.BlockSpec(memory_space=pltpu.HBM)


class _Side:
    def __init__(self, ins, out_shapes, nsem, start, finish):
        self.ins, self.out_shapes, self.start, self.finish = list(ins), list(out_shapes), start, finish
        self.scratch = [pltpu.SemaphoreType.DMA((nsem,)), pltpu.SemaphoreType.DMA((nsem,))]
        self.n_in, self.n_out = len(self.ins), len(self.out_shapes)


def _run_side(side, name):
    def body(*refs):
        args_ = (refs[:side.n_in], refs[side.n_in:side.n_in + side.n_out], *refs[side.n_in + side.n_out:])
        side.start(*args_)
        side.finish(*args_)

    return _pc(body, name=name, in_specs=[_ANY] * side.n_in, out_specs=[_ANY] * side.n_out,
               out_shape=side.out_shapes, scratch_shapes=side.scratch)(*side.ins)


def _split_rows(shape):
    return (shape[0] // 2) % 16 == 0


def _core_halves(shape, c):
    if _split_rows(shape):
        h = shape[0] // 2
        return ((pl.ds(pl.multiple_of(c * h, 16), h), slice(None)),
                (pl.ds(pl.multiple_of((1 - c) * h, 16), h), slice(None)))
    h = shape[1] // 2
    assert h % LANE == 0
    return ((slice(None), pl.ds(pl.multiple_of(c * h, LANE), h)),
            (slice(None), pl.ds(pl.multiple_of((1 - c) * h, LANE), h)))


def gather_side(bigs, tinies):
    nb, nt_ = len(bigs), len(tinies)

    def plan(ins, outs, ssem, rsem):
        x, y, c = _pos()
        me = 2 * x + y
        chips = _other_chips(x, y)
        sibling = (x, y, 1 - c)

        def copy(k, src, dst, to):
            return pltpu.make_async_remote_copy(src_ref=src, dst_ref=dst, send_sem=ssem.at[k], recv_sem=rsem.at[k],
                                                device_id=to, device_id_type=MESH)

        sends, landed, passed_on, tiny_landed = [], [], [], []
        for w in range(nb):
            mine, theirs = _core_halves(bigs[w].shape, c)
            for j, (cx, cy) in enumerate(chips):
                sends.append(copy(6 * w + j, ins[w].at[mine], outs[w].at[(me,) + mine], (cx, cy, c)))
                blk = outs[w].at[(2 * cx + cy,) + mine]
                landed.append((copy(6 * w + j, blk, blk, (cx, cy, c)), copy(6 * w + 3 + j, blk, blk, sibling)))
                blk = outs[w].at[(2 * cx + cy,) + theirs]
                passed_on.append(copy(6 * w + 3 + j, blk, blk, sibling))
        for w in range(nt_):
            for j, (cx, cy) in enumerate(chips):
                k = 6 * nb + 3 * w + j
                sends.append(copy(k, ins[nb + w], outs[nb + w].at[me], (cx, cy, c)))
                blk = outs[nb + w].at[2 * cx + cy]
                tiny_landed.append(copy(k, blk, blk, (cx, cy, c)))
        return sends, landed, passed_on, tiny_landed

    def start(ins, outs, ssem, rsem):
        for cp in plan(ins, outs, ssem, rsem)[0]:
            cp.start()

    def finish(ins, outs, ssem, rsem):
        sends, landed, passed_on, tiny_landed = plan(ins, outs, ssem, rsem)
        for arrived, forward in landed:
            arrived.wait_recv()
            forward.start()
        for arrived in tiny_landed + passed_on:
            arrived.wait_recv()
        for cp in sends + [forward for _, forward in landed]:
            cp.wait_send()

    return _Side(list(bigs) + list(tinies), [SDS((4,) + a.shape, a.dtype) for a in list(bigs) + list(tinies)],
                 6 * nb + 3 * nt_, start, finish)


def _sibling_side(srcs, out_shapes, pick):
    def plan(in_refs, out_refs, ssem, rsem):
        x, y, c = _pos()
        return [pltpu.make_async_remote_copy(src_ref=pick(in_refs[w], srcs[w].shape, c), dst_ref=out_refs[w],
                                             send_sem=ssem.at[w], recv_sem=rsem.at[w], device_id=(x, y, 1 - c),
                                             device_id_type=MESH) for w in range(len(srcs))]

    def start(*refs):
        for cp in plan(*refs):
            cp.start()

    def finish(*refs):
        for cp in plan(*refs):
            cp.wait()

    return _Side(srcs, out_shapes, len(srcs), start, finish)


def swap_side(gs):
    def half_shape(g):
        l, r, cols = g.shape
        return (l, r // 2, cols) if _split_rows((r, cols)) else (l, r, cols // 2)

    return _sibling_side(gs, [SDS(half_shape(g), g.dtype) for g in gs],
                         lambda ref, shape, c: ref.at[(slice(None),) + _core_halves(shape[1:], c)[1]])


def join_side(halves):
    return _sibling_side(halves, [SDS(h.shape, h.dtype) for h in halves], lambda ref, shape, c: ref)


def _both_sides(a, b):
    def split(ins, outs, *scr):
        return ((ins[:a.n_in], outs[:a.n_out], *scr[:2]), (ins[a.n_in:], outs[a.n_out:], *scr[2:]))

    def start(*refs):
        ra, rb = split(*refs)
        a.start(*ra)
        b.start(*rb)

    def finish(*refs):
        ra, rb = split(*refs)
        a.finish(*ra)
        b.finish(*rb)

    side = _Side(a.ins + b.ins, a.out_shapes + b.out_shapes, 1, start, finish)
    side.scratch = a.scratch + b.scratch
    return side


COL_BLOCK = 256


def _blockwise(sc, items, name):
    in_specs, out_specs, out_shapes, operands, spans = [], [], [], [], []
    start = 0
    for ins, outs, nb, _ in items:
        def spec(blk, idx, s0=start, nb=nb):
            return pl.BlockSpec(blk, lambda i, sc_: idx(jnp.clip(i - s0, 0, nb - 1), sc_))

        in_specs += [spec(blk, idx) for _, blk, idx in ins]
        out_specs += [spec(blk, idx) for _, blk, idx in outs]
        operands += [a for a, _, _ in ins]
        out_shapes += [s_ for s_, _, _ in outs]
        spans.append((start, start + nb))
        start += nb
    n_in = len(operands)

    def body(sc_ref, *refs):
        i = pl.program_id(0)
        at_in, at_out = 0, n_in
        for (ins, outs, _, fn), (lo, hi) in zip(items, spans):
            mine_in, mine_out = refs[at_in:at_in + len(ins)], refs[at_out:at_out + len(outs)]
            at_in, at_out = at_in + len(ins), at_out + len(outs)

            @pl.when((i >= lo) & (i < hi))
            def _(fn=fn, mine_in=mine_in, mine_out=mine_out, lo=lo):
                fn(mine_in, mine_out, i - lo, sc_ref)

    return _pc_sp(body, (start,), in_specs, out_specs, out_shapes, name)(sc, *operands)


def add_half_item(g, sib):
    l, r, cols = g.shape

    def fn(ins, outs, b, sc_ref):
        outs[0][...] = (ins[0][...].astype(F32) + ins[1][...].astype(F32)).astype(BF)

    if _split_rows((r, cols)):
        rh, br = _half_rows(r)
        nbk = rh // br
        blk = (1, br, cols)
        there = lambda b, sc_: (b // nbk, b % nbk, 0)
        return ([(g, blk, lambda b, sc_: (b // nbk, sc_[0] * nbk + b % nbk, 0)), (sib, blk, there)],
                [(SDS((l, rh, cols), BF), blk, there)], l * nbk, fn)
    nbk = cols // 2 // COL_BLOCK
    blk = (1, r, COL_BLOCK)
    there = lambda b, sc_: (b // nbk, 0, b % nbk)
    return ([(g, blk, lambda b, sc_: (b // nbk, 0, sc_[0] * nbk + b % nbk)), (sib, blk, there)],
            [(SDS((l, r, cols // 2), BF), blk, there)], l * nbk, fn)


def exchange_side(ps):
    n_ = len(ps)

    def width(p_):
        return p_.shape[2] if p_.shape[0] == 4 else p_.shape[2] // 4

    def plan(p_refs, got_refs, ssem, rsem):
        x, y, c = _pos()
        cps = []
        for w in range(n_):
            wd = width(ps[w])
            for j, (cx, cy) in enumerate(_other_chips(x, y)):
                to = 2 * cx + cy
                src = p_refs[w].at[to] if ps[w].shape[0] == 4 else p_refs[w].at[0, :, pl.ds(pl.multiple_of(to * wd, LANE), wd)]
                cps.append(pltpu.make_async_remote_copy(
                    src_ref=src, dst_ref=got_refs[w].at[j], send_sem=ssem.at[3 * w + j], recv_sem=rsem.at[3 * w + j],
                    device_id=(cx, cy, c), device_id_type=MESH))
        return cps

    def start(*refs):
        for cp in plan(*refs):
            cp.start()

    def finish(*refs):
        for cp in plan(*refs):
            cp.wait()

    return _Side(ps, [SDS((3, p_.shape[1], width(p_)), p_.dtype) for p_ in ps], 3 * n_, start, finish)


def sum4_item(p, got):
    _, rh, wd = got.shape

    def fn(ins, outs, b, sc_ref):
        p_ref, g_ref = ins
        outs[0][...] = ((p_ref[0].astype(F32) + g_ref[0].astype(F32))
                        + (g_ref[1].astype(F32) + g_ref[2].astype(F32)))

    if rh % 16:
        assert p.shape[0] == 4
        return ([(p, (1, rh, COL_BLOCK), lambda b, sc_: (sc_[1], 0, b)),
                 (got, (3, rh, COL_BLOCK), lambda b, sc_: (0, 0, b))],
                [(SDS((rh, wd), F32), (rh, COL_BLOCK), lambda b, sc_: (0, b))], wd // COL_BLOCK, fn)
    _, br = _half_rows(2 * rh)
    own = (lambda b, sc_: (sc_[1], b, 0)) if p.shape[0] == 4 else (lambda b, sc_: (0, b, sc_[1]))
    return ([(p, (1, br, wd), own), (got, (3, br, wd), lambda b, sc_: (0, b, 0))],
            [(SDS((rh, wd), F32), (br, wd), lambda b, sc_: (b, 0))], rh // br, fn)


def allreduce_small(g):
    rows = g.shape[0]
    rh = rows // 2

    def body(g_ref, out_ref, sib_buf, chip_buf, sum_sc, ssem, rsem):
        x, y, c = _pos()
        me = 2 * x + y
        sibling = (x, y, 1 - c)
        mine = pl.ds(pl.multiple_of(c * rh, 8), rh)

        def copy(k, src, dst, to):
            return pltpu.make_async_remote_copy(src_ref=src, dst_ref=dst, send_sem=ssem.at[k], recv_sem=rsem.at[k],
                                                device_id=to, device_id_type=MESH)

        cp = copy(0, g_ref, sib_buf, sibling)
        cp.start()
        cp.wait()
        sum_sc[...] = g_ref[...] + sib_buf[...]
        chips = _other_chips(x, y)
        cps = [copy(1 + j, sum_sc.at[mine], chip_buf.at[me], (cx, cy, c)) for j, (cx, cy) in enumerate(chips)]
        for cp in cps:
            cp.start()
        chip_buf[me] = sum_sc[mine, :]
        for j, (cx, cy) in enumerate(chips):
            copy(1 + j, sum_sc.at[mine], chip_buf.at[2 * cx + cy], (cx, cy, c)).wait_recv()
        for cp in cps:
            cp.wait_send()
        out_ref[mine, :] = (chip_buf[0] + chip_buf[1]) + (chip_buf[2] + chip_buf[3])
        cp = copy(4, out_ref.at[mine], out_ref.at[mine], sibling)
        cp.start()
        cp.wait()

    vm = pl.BlockSpec(memory_space=pltpu.VMEM)
    return _pc(body, name="allreduce_small", in_specs=[vm], out_specs=vm, out_shape=SDS((rows, LANE), F32),
               scratch_shapes=[pltpu.VMEM((rows, LANE), F32), pltpu.VMEM((4, rh, LANE), F32), pltpu.VMEM((rows, LANE), F32),
                               pltpu.SemaphoreType.DMA((5,)), pltpu.SemaphoreType.DMA((5,))],
               compiler_params=pltpu.CompilerParams(vmem_limit_bytes=VMEM_LIMIT))(g)


def _pack_small(entries, get):
    flat = jnp.concatenate([get(n).reshape(-1).astype(F32) for n, _ in entries])
    rows = -(-flat.shape[0] // (8 * LANE)) * 8
    return jnp.pad(flat, (0, rows * LANE - flat.shape[0])).reshape(rows, LANE)


def _unpack_small(entries, packed):
    out, off = {}, 0
    flat = packed.reshape(-1)
    for name, n in entries:
        out[name] = flat[off:off + n]
        off += n
    return out


def _cols_full(blk):
    return blk.transpose(1, 0, 2).reshape(blk.shape[1], 4 * blk.shape[2])


def kernel(x, p, pre_mix_norm, w_in, a_ln_g, a_ln_b, a_spatial_w, a_spatial_b, a_out, b_gk, b_gk_bias, b_out_norm, b_out, w_mix_out, post_mix_norm, pre_ffn_norm, w_up, conv_w, conv_b, w_down, post_ffn_norm, w_ple, w_ple_gate, post_ple_norm, loss_target, m_pre_mix_norm, m_w_in, m_a_ln_g, m_a_ln_b, m_a_spatial_w, m_a_spatial_b, m_a_out, m_b_gk, m_b_gk_bias, m_b_out_norm, m_b_out, m_w_mix_out, m_post_mix_norm, m_pre_ffn_norm, m_w_up, m_conv_w, m_conv_b, m_w_down, m_post_ffn_norm, m_w_ple, m_w_ple_gate, m_post_ple_norm, v_pre_mix_norm, v_w_in, v_a_ln_g, v_a_ln_b, v_a_spatial_w, v_a_spatial_b, v_a_out, v_b_gk, v_b_gk_bias, v_b_out_norm, v_b_out, v_w_mix_out, v_post_mix_norm, v_pre_ffn_norm, v_w_up, v_conv_w, v_conv_b, v_w_down, v_post_ffn_norm, v_w_ple, v_w_ple_gate, v_post_ple_norm):
    args = dict(locals())
    order = ['pre_mix_norm', 'w_in', 'a_ln_g', 'a_ln_b', 'a_spatial_w', 'a_spatial_b', 'a_out', 'b_gk', 'b_gk_bias',
             'b_out_norm', 'b_out', 'w_mix_out', 'post_mix_norm', 'pre_ffn_norm', 'w_up', 'conv_w', 'conv_b', 'w_down',
             'post_ffn_norm', 'w_ple', 'w_ple_gate', 'post_ple_norm']
    assert sorted(BIG + TINY + tuple(n for n, _ in SMALL)) == sorted(order)
    s = x.shape[1]
    xs = x.reshape(s, D)
    ps = p.reshape(s, PLE)
    tgt = loss_target.reshape(s, D)
    t_big = min(1024, s)
    t_mid = min(512, s)
    t_small = min(256, s)
    t_gla = min(256, s)
    mx_, my_, mc_ = _pos()
    me = 2 * mx_ + my_
    sc = jnp.stack([mc_, me]).astype(jnp.int32)
    shard = lambda n: args[n].reshape(args[n].shape[1:])

    mine = {n: shard(n).astype(BF) for n in BIG}
    mine["w_in"] = shard("w_in").T.astype(BF)
    mine.update({n: shard(n) for n in TINY})
    fill = lambda names, gots: {n: lax.dynamic_update_slice(got, mine[n][None], (me, 0, 0)) for n, got in zip(names, gots)}
    first = ("w_in",) + TINY
    full = fill(first, _run_side(gather_side([mine["w_in"]], [mine[n] for n in TINY]), "gather_first"))
    def seg(a, b):
        pieces = []
        while a < b:
            s_, hi = a // 1540, min(b, (a // 1540 + 1) * 1540)
            pieces.append(full["w_in"][s_, a - s_ * 1540:hi - s_ * 1540])
            a = hi
        return pieces

    qk = [seg(1024 + h * B_HK, 1024 + (h + 1) * B_HK) for h in range(B_H)]
    kk = [seg(1536 + h * B_HK, 1536 + (h + 1) * B_HK) for h in range(B_H)]
    w_z = jnp.concatenate(seg(0, 1024) + [m_ for h in range(B_H) for m_ in qk[h] + kk[h]]
                          + seg(2048, 4096) + seg(4112, 6160) + seg(4096, 4112)
                          + [jnp.zeros((LANE - B_RANK, D), BF)], axis=0)
    wgk = jnp.pad(_cols_full(full["b_gk"]).astype(BF), ((0, LANE - B_RANK), (0, 0)))
    w_conv = _cols_full(full["conv_w"])
    g1, g2, g3 = pre_mix_norm.reshape(1, D), post_mix_norm.reshape(1, D), pre_ffn_norm.reshape(1, D)
    g4, g5 = post_ffn_norm.reshape(1, D), post_ple_norm.reshape(1, D)
    ln_g, ln_b = a_ln_g.reshape(1, A_W), a_ln_b.reshape(1, A_W)
    w_s = a_spatial_w.reshape(A_G, A_C, A_C)
    w_cat = w_s.transpose(1, 0, 2).reshape(A_C, A_G * A_C)
    w_cat_t = w_s.transpose(2, 0, 1).reshape(A_C, A_G * A_C)
    bias_full = jnp.repeat(a_spatial_b.reshape(A_G, A_C).T, A_GD, axis=1)
    bdm = (jnp.arange(A_G * A_C)[:, None] // A_C == jnp.arange(A_W)[None, :] // A_GD).astype(BF)
    gk_bias = b_gk_bias.reshape(1, B_H * B_HK)
    wn = b_out_norm.reshape(1, B_HV)
    cb = conv_b.reshape(1, 2 * D_FF)
    idx = jnp.arange(t_gla)
    ltri = ((idx[:, None] // B_C == idx[None, :] // B_C) & (idx[None, :] <= idx[:, None])).astype(BF)

    a, z, qk32, zl, *gots = norm_matmul(xs, g1, w_z, D, t_big, "in_proj", nblk=6, f32_blk=1, tail_blk=48,
                                        side=gather_side([mine[n] for n in BIG[1:]], []))
    full.update(fill(BIG[1:], gots))
    w_aout, w_ple_f = _cols_full(full["a_out"]), _cols_full(full["w_ple"])
    w_bout, w_mix, w_pg = (full[n].reshape(D, D) for n in ("b_out", "w_mix_out", "w_ple_gate"))
    w_dn, w_up3 = full["w_down"].reshape(D_FF, D), full["w_up"]
    sa = sgu_fwd(z, ln_g, ln_b, w_cat, bias_full, bdm, t_mid)
    ob, o, states = gla_fwd(z, qk32, zl, wgk, gk_bias, wn, ltri, t_gla)
    ya, yb, mp, mx, h1 = mix_fwd(sa, ob, z, xs, w_aout, w_bout, w_mix, g2, t_mid)
    c, up_g, up_v, cg, cv, ff = ffn_up_fwd(h1, g3, w_up3, w_conv, cb, t_mid)
    f, h2, pg, pe, dy, loss = out_fwd(ff, h1, ps, tgt, w_dn, w_pg, w_ple_f, g4, g5, t_mid)

    dh2, dpe, dpg, df, dff, gg5, gg4 = out_bwd(dy, pg, pe, f, g5, g4, w_pg, w_dn, t_mid)
    dup, gcw, gcb, dh1, gg3 = ffn_up_bwd(up_g, up_v, cg, cv, dff, w_conv, _cols_full(w_up3), h1, g3, dh2, t_small)
    grads = {
        "w_up": mm_tn(c, dup, "dw_up")[None],
        "w_down": mm_tn(ff, df, "dw_down").reshape(4, D_FF // 4, D),
        "w_ple": mm_tn(ps, dpe, "dw_ple")[None],
        "w_ple_gate": mm_tn(h2, dpg, "dw_ple_gate").reshape(4, D // 4, D),
    }
    ffn_side = ("w_up", "w_down", "w_ple", "w_ple_gate")
    dmx, dya, dyb, dga, dgb, dsa, dob, gg2, *sibs = mix_bwd(dh1, mx, z, ya, yb, g2, w_mix, w_aout, w_bout, t_mid,
                                                            side=swap_side([grads[n] for n in ffn_side]))
    sib = dict(zip(ffn_side, sibs))
    duv, g_lng, g_lnb, g_wcat, g_bst = sgu_bwd(z, dsa, ln_g, ln_b, w_cat, w_cat_t, bias_full, bdm, t_mid)
    g_ws = g_wcat.reshape(A_C, A_G, A_C).transpose(1, 0, 2)
    grads.update({
        "a_out": mm_tn(sa, dya, "dw_a_out")[None],
        "b_out": mm_tn(ob, dyb, "dw_b_out").reshape(4, D // 4, D),
        "w_mix_out": mm_tn(mp, dmx, "dw_mix").reshape(4, D // 4, D),
    })

    def swap(names):
        sib.update(zip(names, _run_side(swap_side([grads[n] for n in names]), "swap_halves_" + names[0])))

    swap(("a_out", "b_out", "w_mix_out"))
    parts = dict(zip(BIG[1:], _blockwise(sc, [add_half_item(grads[n], sib[n]) for n in BIG[1:]], "partials")))
    dqk, dvb, dog, dpre, dlr, g_gkb, g_wn, *gots = gla_bwd(
        z, qk32, zl, o, dob, states, wgk, gk_bias, wn, ltri, ltri.T, t_gla,
        side=exchange_side([parts[n] for n in BIG[1:]]))
    reds = dict(zip(BIG[1:], _blockwise(sc, [sum4_item(parts[n], got_) for n, got_ in zip(BIG[1:], gots)], "sums")))
    segs = [duv, dqk, dvb, dog, dga, dgb, dlr]

    gz = [mm_tn(sg_, a, "dw_in_%d" % k) for k, sg_ in enumerate(segs)]
    gq = [gz[1][h * 256:h * 256 + B_HK] for h in range(B_H)]
    gk = [gz[1][h * 256 + B_HK:(h + 1) * 256] for h in range(B_H)]
    in_order = [gz[0]] + gq + gk + [gz[2], gz[3], gz[6][:B_RANK], gz[4], gz[5]]
    shards, at = [[] for _ in range(4)], 0
    for piece in in_order:
        lo = 0
        while lo < piece.shape[0]:
            s_ = (at + lo) // 1540
            hi = min(piece.shape[0], (s_ + 1) * 1540 - at)
            shards[s_].append(piece[lo:hi])
            lo = hi
        at += piece.shape[0]
    grads["w_in"] = jnp.stack([jnp.concatenate(p_, axis=0) for p_ in shards])
    swap(("w_in",))
    parts["w_in"], = _blockwise(sc, [add_half_item(grads["w_in"], sib["w_in"])], "partial_w_in")
    dx, gg1, got_in, *sib_reds = nt_normbwd(
        segs, w_z, xs, g1, dh1, t_mid, "in_bwd",
        side=_both_sides(exchange_side([parts["w_in"]]), join_side([reds[n] for n in BIG[1:]])))
    sib_red = dict(zip(BIG[1:], sib_reds))

    reds["w_in"], = _blockwise(sc, [sum4_item(parts["w_in"], got_in)], "sum_w_in")
    sib_red["w_in"], = _run_side(join_side([reds["w_in"]]), "join_w_in")
    outs = {}
    for n in BIG[1:]:
        res = _blockwise(sc, [adamw_item(reds[n], sib_red[n], shard(n), shard("m_" + n), shard("v_" + n))],
                         "adamw_" + n)
        outs[n] = [r_.reshape(args[n].shape) for r_ in res]
    res = adamw_cols(sc, reds["w_in"], sib_red["w_in"], shard("w_in").T, shard("m_w_in").T, shard("v_w_in").T,
                     "adamw_w_in")
    outs["w_in"] = [r_.T.reshape(w_in.shape) for r_ in res]

    small_g = {
        "pre_mix_norm": gg1, "a_ln_g": g_lng, "a_ln_b": g_lnb, "a_spatial_w": g_ws, "a_spatial_b": g_bst.T,
        "b_gk_bias": g_gkb, "b_out_norm": g_wn, "post_mix_norm": gg2, "pre_ffn_norm": gg3,
        "conv_b": gcb, "post_ffn_norm": gg4, "post_ple_norm": gg5,
        "b_gk": mm_tn(zl, dpre, "dw_gk")[:B_RANK], "conv_w": gcw,
        "loss": loss,
    }
    red_entries = SMALL + (("b_gk", B_RANK * 512), ("conv_w", 3 * 2 * D_FF), ("loss", 1))
    g_fin = _unpack_small(red_entries, allreduce_small(_pack_small(red_entries, lambda n: small_g[n])))
    g_fin["b_gk"] = lax.dynamic_slice(g_fin["b_gk"].reshape(B_RANK, 512), (0, me * B_HK), (B_RANK, B_HK))
    g_fin["conv_w"] = lax.dynamic_slice(g_fin["conv_w"].reshape(3, 2 * D_FF), (0, me * 1408), (3, 1408))
    upd_entries = SMALL + (("b_gk", B_RANK * B_HK), ("conv_w", 3 * 1408))
    res = adamw_small(*[_pack_small(upd_entries, get) for get in
                        (lambda n: g_fin[n], lambda n: args[n], lambda n: args["m_" + n], lambda n: args["v_" + n])])
    res = [_unpack_small(upd_entries, r_) for r_ in res]
    for n, _ in upd_entries:
        outs[n] = [r_[n].reshape(args[n].shape) for r_ in [g_fin] + res]

    return (g_fin["loss"].reshape(()), dx.reshape(x.shape), *[outs[n][0] for n in order], *[outs[n][1] for n in order],
            *[outs[n][2] for n in order], *[outs[n][3] for n in order])
```
